```python
import math
import jax, jax.numpy as jnp
from jax import lax
import numpy as np

D_MODEL = 1024
BATCH = 8
SEQ = 4096
DEPTH = 1

CHUNK = 64
D_FF = 2816
POOL_WIDTH = D_MODEL // 2
POOL_WINDOWS = (2, 4, 8, 16)
N_POOL_GROUPS = len(POOL_WINDOWS)
POOL_GROUP = POOL_WIDTH // N_POOL_GROUPS
SSM_WIDTH = D_MODEL // 2
SSM_GROUP = 16
N_SSM_GROUPS = SSM_WIDTH // SSM_GROUP
SSM_STATE = 64
N_SUBLAYERS = 3
N_MOD = 3
IN_WIDTH = POOL_WIDTH + SSM_WIDTH + 2 * D_MODEL
EPS = 1e-6
DT_MIN = 1e-3
DT_MAX = 1e-1

kernel_name = "hybrid_pool_s5_macaron_adaln"


def rms_norm(x, g):
    xf = x.astype(jnp.float32)
    y = xf * lax.rsqrt(jnp.mean(xf * xf, axis=-1, keepdims=True) + EPS)
    return (y * g.astype(jnp.float32)).astype(x.dtype)


def modulate(h, shift, scale):
    return h * (1 + scale[:, None, :]) + shift[:, None, :]


def swiglu(h, w_in, w_out):
    a, b = jnp.split(h @ w_in, 2, axis=-1)
    return (jax.nn.silu(a) * b) @ w_out


def multiscale_pool(u, pool_w, pool_b, pool_scale):
    B, T, _ = u.shape
    ug = u.reshape(B, T, N_POOL_GROUPS, POOL_GROUP)
    cs = jnp.cumsum(ug.astype(jnp.float32), axis=1)
    pos = jnp.arange(T)
    means = []
    for k, w in enumerate(POOL_WINDOWS):
        csk = cs[:, :, k]
        prev = jnp.pad(csk, ((0, 0), (w, 0), (0, 0)))[:, :T]
        cnt = jnp.minimum(pos + 1, w).astype(jnp.float32)[None, :, None]
        means.append((csk - prev) / cnt)
    mean = jnp.stack(means, axis=2).astype(u.dtype)
    z = mean - ug
    z = jnp.einsum('btgc,gcd->btgd', z, pool_w) + pool_b.reshape(N_POOL_GROUPS, POOL_GROUP)
    return z.reshape(B, T, POOL_WIDTH) * pool_scale


def _ssm_combine(left, right):
    ar_l, ai_l, br_l, bi_l = left
    ar_r, ai_r, br_r, bi_r = right
    ar = ar_r * ar_l - ai_r * ai_l
    ai = ar_r * ai_l + ai_r * ar_l
    br = ar_r * br_l - ai_r * bi_l + br_r
    bi = ar_r * bi_l + ai_r * br_l + bi_r
    return (ar, ai, br, bi)


def s5_mixer(u, lam_re_log, lam_im, log_dt, b_re, b_im, c_re, c_im, d_skip, w_glu, b_glu):
    B, T, _ = u.shape
    f32 = jnp.float32
    uf = u.astype(f32).reshape(B, T, N_SSM_GROUPS, SSM_GROUP)
    lr = -jnp.exp(lam_re_log.astype(f32))
    li = lam_im.astype(f32)
    dt = jnp.exp(log_dt.astype(f32))[:, None]
    mag = jnp.exp(lr * dt)
    ang = li * dt
    ab_re = mag * jnp.cos(ang)
    ab_im = mag * jnp.sin(ang)
    num_re = ab_re - 1.0
    num_im = ab_im
    den = lr * lr + li * li
    f_re = (num_re * lr + num_im * li) / den
    f_im = (num_im * lr - num_re * li) / den
    br = b_re.astype(f32)
    bi = b_im.astype(f32)
    bb_re = f_re[..., None] * br - f_im[..., None] * bi
    bb_im = f_re[..., None] * bi + f_im[..., None] * br
    bu_re = jnp.einsum('btgh,gnh->btgn', uf, bb_re)
    bu_im = jnp.einsum('btgh,gnh->btgn', uf, bb_im)
    a_re = jnp.broadcast_to(ab_re[None, None], (1, T, N_SSM_GROUPS, SSM_STATE))
    a_im = jnp.broadcast_to(ab_im[None, None], (1, T, N_SSM_GROUPS, SSM_STATE))
    _, _, s_re, s_im = lax.associative_scan(_ssm_combine, (a_re, a_im, bu_re, bu_im), axis=1)
    y = (jnp.einsum('btgn,ghn->btgh', s_re, c_re.astype(f32))
         - jnp.einsum('btgn,ghn->btgh', s_im, c_im.astype(f32)))
    y = y.reshape(B, T, SSM_WIDTH) + d_skip.astype(f32) * uf.reshape(B, T, SSM_WIDTH)
    y = jax.nn.gelu(y.astype(u.dtype), approximate=False)
    val, gate = jnp.split(y @ w_glu + b_glu, 2, axis=-1)
    return val * jax.nn.sigmoid(gate)


def _fwd_setup_inputs(seed: int = 0) -> dict:
    key = jax.random.key(seed)
    ks = jax.random.split(key, 32)
    L, D, F = DEPTH, D_MODEL, D_FF
    G, H, N = N_SSM_GROUPS, SSM_GROUP, SSM_STATE
    nrm = lambda k, shape, s: jax.random.normal(k, shape, jnp.float32) * s
    n_idx = jnp.arange(N, dtype=jnp.float32)[None, :]
    return {
        "x": nrm(ks[0], (BATCH, SEQ, D), 1.0),
        "c": nrm(ks[1], (BATCH, D), 1.0),
        "w_ada": nrm(ks[2], (L, D, N_SUBLAYERS * N_MOD * D), 0.5 * D ** -0.5),
        "b_ada": nrm(ks[3], (L, N_SUBLAYERS * N_MOD * D), 0.01),
        "g_ffn1": 1.0 + nrm(ks[4], (L, D), 0.01),
        "w_ffn1_in": nrm(ks[5], (L, D, 2 * F), D ** -0.5),
        "w_ffn1_out": nrm(ks[6], (L, F, D), F ** -0.5),
        "g_mix": 1.0 + nrm(ks[7], (L, D), 0.01),
        "w_in": nrm(ks[8], (L, D, IN_WIDTH), D ** -0.5),
        "pool_w": nrm(ks[9], (L, N_POOL_GROUPS, POOL_GROUP, POOL_GROUP), POOL_GROUP ** -0.5),
        "pool_b": nrm(ks[10], (L, POOL_WIDTH), 0.01),
        "pool_scale": 1.0 + nrm(ks[11], (L, POOL_WIDTH), 0.05),
        "w_pool_up": nrm(ks[12], (L, POOL_WIDTH, D), POOL_WIDTH ** -0.5),
        "ssm_lam_re_log": jnp.log(0.5) + nrm(ks[13], (L, G, N), 0.01),
        "ssm_lam_im": math.pi * n_idx + nrm(ks[14], (L, G, N), 0.01),
        "ssm_log_dt": jax.random.uniform(ks[15], (L, G), jnp.float32, math.log(DT_MIN), math.log(DT_MAX)),
        "ssm_b_re": nrm(ks[16], (L, G, N, H), (2 * H) ** -0.5),
        "ssm_b_im": nrm(ks[17], (L, G, N, H), (2 * H) ** -0.5),
        "ssm_c_re": nrm(ks[18], (L, G, H, N), N ** -0.5),
        "ssm_c_im": nrm(ks[19], (L, G, H, N), N ** -0.5),
        "ssm_d": nrm(ks[20], (L, SSM_WIDTH), 1.0),
        "w_glu": nrm(ks[21], (L, SSM_WIDTH, 2 * SSM_WIDTH), SSM_WIDTH ** -0.5),
        "b_glu": nrm(ks[22], (L, 2 * SSM_WIDTH), 0.01),
        "w_ssm_up": nrm(ks[23], (L, SSM_WIDTH, D), SSM_WIDTH ** -0.5),
        "w_out": nrm(ks[24], (L, D, D), D ** -0.5),
        "g_ffn2": 1.0 + nrm(ks[25], (L, D), 0.01),
        "w_ffn2_in": nrm(ks[26], (L, D, 2 * F), D ** -0.5),
        "w_ffn2_out": nrm(ks[27], (L, F, D), F ** -0.5),
        "g_final": 1.0 + nrm(ks[28], (D,), 0.01),
    }


def _fwd_reference(x, c, w_ada, b_ada, g_ffn1, w_ffn1_in, w_ffn1_out, g_mix, w_in,
              pool_w, pool_b, pool_scale, w_pool_up,
              ssm_lam_re_log, ssm_lam_im, ssm_log_dt, ssm_b_re, ssm_b_im, ssm_c_re, ssm_c_im, ssm_d,
              w_glu, b_glu, w_ssm_up, w_out, g_ffn2, w_ffn2_in, w_ffn2_out, g_final):
    B = x.shape[0]
    split_pts = (POOL_WIDTH, POOL_WIDTH + SSM_WIDTH, POOL_WIDTH + SSM_WIDTH + D_MODEL)
    for l in range(DEPTH):
        mod = (jax.nn.silu(c) @ w_ada[l] + b_ada[l]).reshape(B, N_SUBLAYERS, N_MOD, D_MODEL)

        h = modulate(rms_norm(x, g_ffn1[l]), mod[:, 0, 0], mod[:, 0, 1])
        x = x + 0.5 * mod[:, 0, 2][:, None, :] * swiglu(h, w_ffn1_in[l], w_ffn1_out[l])

        h = modulate(rms_norm(x, g_mix[l]), mod[:, 1, 0], mod[:, 1, 1])
        u_pool, u_ssm, gl_pool, gl_ssm = jnp.split(h @ w_in[l], split_pts, axis=-1)
        y_pool = multiscale_pool(u_pool, pool_w[l], pool_b[l], pool_scale[l]) @ w_pool_up[l]
        y_ssm = s5_mixer(u_ssm, ssm_lam_re_log[l], ssm_lam_im[l], ssm_log_dt[l],
                         ssm_b_re[l], ssm_b_im[l], ssm_c_re[l], ssm_c_im[l], ssm_d[l],
                         w_glu[l], b_glu[l]) @ w_ssm_up[l]
        merged = jax.nn.sigmoid(gl_pool) * y_pool + jax.nn.sigmoid(gl_ssm) * y_ssm
        x = x + mod[:, 1, 2][:, None, :] * (merged @ w_out[l])

        h = modulate(rms_norm(x, g_ffn2[l]), mod[:, 2, 0], mod[:, 2, 1])
        x = x + 0.5 * mod[:, 2, 2][:, None, :] * swiglu(h, w_ffn2_in[l], w_ffn2_out[l])
    return rms_norm(x, g_final)


import jax as _jax
import jax.numpy as _jnp

TWIN_FORMAT = 'train_step'
FWD_PARAMS = ['x', 'c', 'w_ada', 'b_ada', 'g_ffn1', 'w_ffn1_in', 'w_ffn1_out', 'g_mix', 'w_in', 'pool_w', 'pool_b', 'pool_scale', 'w_pool_up', 'ssm_lam_re_log', 'ssm_lam_im', 'ssm_log_dt', 'ssm_b_re', 'ssm_b_im', 'ssm_c_re', 'ssm_c_im', 'ssm_d', 'w_glu', 'b_glu', 'w_ssm_up', 'w_out', 'g_ffn2', 'w_ffn2_in', 'w_ffn2_out', 'g_final']
TWIN_WEIGHTS = ['w_ada', 'b_ada', 'g_ffn1', 'w_ffn1_in', 'w_ffn1_out', 'g_mix', 'w_in', 'pool_w', 'pool_b', 'pool_scale', 'w_pool_up', 'ssm_lam_re_log', 'ssm_lam_im', 'ssm_log_dt', 'ssm_b_re', 'ssm_b_im', 'ssm_c_re', 'ssm_c_im', 'ssm_d', 'w_glu', 'b_glu', 'w_ssm_up', 'w_out', 'g_ffn2', 'w_ffn2_in', 'w_ffn2_out', 'g_final']
TWIN_DIFF_INPUT = 'x'
TWIN_INPUTS = ['x', 'c', 'w_ada', 'b_ada', 'g_ffn1', 'w_ffn1_in', 'w_ffn1_out', 'g_mix', 'w_in', 'pool_w', 'pool_b', 'pool_scale', 'w_pool_up', 'ssm_lam_re_log', 'ssm_lam_im', 'ssm_log_dt', 'ssm_b_re', 'ssm_b_im', 'ssm_c_re', 'ssm_c_im', 'ssm_d', 'w_glu', 'b_glu', 'w_ssm_up', 'w_out', 'g_ffn2', 'w_ffn2_in', 'w_ffn2_out', 'g_final', 'loss_target', 'm_w_ada', 'm_b_ada', 'm_g_ffn1', 'm_w_ffn1_in', 'm_w_ffn1_out', 'm_g_mix', 'm_w_in', 'm_pool_w', 'm_pool_b', 'm_pool_scale', 'm_w_pool_up', 'm_ssm_lam_re_log', 'm_ssm_lam_im', 'm_ssm_log_dt', 'm_ssm_b_re', 'm_ssm_b_im', 'm_ssm_c_re', 'm_ssm_c_im', 'm_ssm_d', 'm_w_glu', 'm_b_glu', 'm_w_ssm_up', 'm_w_out', 'm_g_ffn2', 'm_w_ffn2_in', 'm_w_ffn2_out', 'm_g_final', 'v_w_ada', 'v_b_ada', 'v_g_ffn1', 'v_w_ffn1_in', 'v_w_ffn1_out', 'v_g_mix', 'v_w_in', 'v_pool_w', 'v_pool_b', 'v_pool_scale', 'v_w_pool_up', 'v_ssm_lam_re_log', 'v_ssm_lam_im', 'v_ssm_log_dt', 'v_ssm_b_re', 'v_ssm_b_im', 'v_ssm_c_re', 'v_ssm_c_im', 'v_ssm_d', 'v_w_glu', 'v_b_glu', 'v_w_ssm_up', 'v_w_out', 'v_g_ffn2', 'v_w_ffn2_in', 'v_w_ffn2_out', 'v_g_final']
TWIN_OUTPUTS = ['loss', 'grad_x', 'grad_w_ada', 'grad_b_ada', 'grad_g_ffn1', 'grad_w_ffn1_in', 'grad_w_ffn1_out', 'grad_g_mix', 'grad_w_in', 'grad_pool_w', 'grad_pool_b', 'grad_pool_scale', 'grad_w_pool_up', 'grad_ssm_lam_re_log', 'grad_ssm_lam_im', 'grad_ssm_log_dt', 'grad_ssm_b_re', 'grad_ssm_b_im', 'grad_ssm_c_re', 'grad_ssm_c_im', 'grad_ssm_d', 'grad_w_glu', 'grad_b_glu', 'grad_w_ssm_up', 'grad_w_out', 'grad_g_ffn2', 'grad_w_ffn2_in', 'grad_w_ffn2_out', 'grad_g_final', 'delta_w_ada', 'delta_b_ada', 'delta_g_ffn1', 'delta_w_ffn1_in', 'delta_w_ffn1_out', 'delta_g_mix', 'delta_w_in', 'delta_pool_w', 'delta_pool_b', 'delta_pool_scale', 'delta_w_pool_up', 'delta_ssm_lam_re_log', 'delta_ssm_lam_im', 'delta_ssm_log_dt', 'delta_ssm_b_re', 'delta_ssm_b_im', 'delta_ssm_c_re', 'delta_ssm_c_im', 'delta_ssm_d', 'delta_w_glu', 'delta_b_glu', 'delta_w_ssm_up', 'delta_w_out', 'delta_g_ffn2', 'delta_w_ffn2_in', 'delta_w_ffn2_out', 'delta_g_final', 'new_m_w_ada', 'new_m_b_ada', 'new_m_g_ffn1', 'new_m_w_ffn1_in', 'new_m_w_ffn1_out', 'new_m_g_mix', 'new_m_w_in', 'new_m_pool_w', 'new_m_pool_b', 'new_m_pool_scale', 'new_m_w_pool_up', 'new_m_ssm_lam_re_log', 'new_m_ssm_lam_im', 'new_m_ssm_log_dt', 'new_m_ssm_b_re', 'new_m_ssm_b_im', 'new_m_ssm_c_re', 'new_m_ssm_c_im', 'new_m_ssm_d', 'new_m_w_glu', 'new_m_b_glu', 'new_m_w_ssm_up', 'new_m_w_out', 'new_m_g_ffn2', 'new_m_w_ffn2_in', 'new_m_w_ffn2_out', 'new_m_g_final', 'new_v_w_ada', 'new_v_b_ada', 'new_v_g_ffn1', 'new_v_w_ffn1_in', 'new_v_w_ffn1_out', 'new_v_g_mix', 'new_v_w_in', 'new_v_pool_w', 'new_v_pool_b', 'new_v_pool_scale', 'new_v_w_pool_up', 'new_v_ssm_lam_re_log', 'new_v_ssm_lam_im', 'new_v_ssm_log_dt', 'new_v_ssm_b_re', 'new_v_ssm_b_im', 'new_v_ssm_c_re', 'new_v_ssm_c_im', 'new_v_ssm_d', 'new_v_w_glu', 'new_v_b_glu', 'new_v_w_ssm_up', 'new_v_w_out', 'new_v_g_ffn2', 'new_v_w_ffn2_in', 'new_v_w_ffn2_out', 'new_v_g_final']
TWIN_LEAF_KINDS = {'loss': 'loss', 'grad_x': 'grad_x', 'grad_w_ada': 'grad_w', 'grad_b_ada': 'grad_w', 'grad_g_ffn1': 'grad_w', 'grad_w_ffn1_in': 'grad_w', 'grad_w_ffn1_out': 'grad_w', 'grad_g_mix': 'grad_w', 'grad_w_in': 'grad_w', 'grad_pool_w': 'grad_w', 'grad_pool_b': 'grad_w', 'grad_pool_scale': 'grad_w', 'grad_w_pool_up': 'grad_w', 'grad_ssm_lam_re_log': 'grad_w', 'grad_ssm_lam_im': 'grad_w', 'grad_ssm_log_dt': 'grad_w', 'grad_ssm_b_re': 'grad_w', 'grad_ssm_b_im': 'grad_w', 'grad_ssm_c_re': 'grad_w', 'grad_ssm_c_im': 'grad_w', 'grad_ssm_d': 'grad_w', 'grad_w_glu': 'grad_w', 'grad_b_glu': 'grad_w', 'grad_w_ssm_up': 'grad_w', 'grad_w_out': 'grad_w', 'grad_g_ffn2': 'grad_w', 'grad_w_ffn2_in': 'grad_w', 'grad_w_ffn2_out': 'grad_w', 'grad_g_final': 'grad_w', 'delta_w_ada': 'delta_w', 'delta_b_ada': 'delta_w', 'delta_g_ffn1': 'delta_w', 'delta_w_ffn1_in': 'delta_w', 'delta_w_ffn1_out': 'delta_w', 'delta_g_mix': 'delta_w', 'delta_w_in': 'delta_w', 'delta_pool_w': 'delta_w', 'delta_pool_b': 'delta_w', 'delta_pool_scale': 'delta_w', 'delta_w_pool_up': 'delta_w', 'delta_ssm_lam_re_log': 'delta_w', 'delta_ssm_lam_im': 'delta_w', 'delta_ssm_log_dt': 'delta_w', 'delta_ssm_b_re': 'delta_w', 'delta_ssm_b_im': 'delta_w', 'delta_ssm_c_re': 'delta_w', 'delta_ssm_c_im': 'delta_w', 'delta_ssm_d': 'delta_w', 'delta_w_glu': 'delta_w', 'delta_b_glu': 'delta_w', 'delta_w_ssm_up': 'delta_w', 'delta_w_out': 'delta_w', 'delta_g_ffn2': 'delta_w', 'delta_w_ffn2_in': 'delta_w', 'delta_w_ffn2_out': 'delta_w', 'delta_g_final': 'delta_w', 'new_m_w_ada': 'new_m', 'new_m_b_ada': 'new_m', 'new_m_g_ffn1': 'new_m', 'new_m_w_ffn1_in': 'new_m', 'new_m_w_ffn1_out': 'new_m', 'new_m_g_mix': 'new_m', 'new_m_w_in': 'new_m', 'new_m_pool_w': 'new_m', 'new_m_pool_b': 'new_m', 'new_m_pool_scale': 'new_m', 'new_m_w_pool_up': 'new_m', 'new_m_ssm_lam_re_log': 'new_m', 'new_m_ssm_lam_im': 'new_m', 'new_m_ssm_log_dt': 'new_m', 'new_m_ssm_b_re': 'new_m', 'new_m_ssm_b_im': 'new_m', 'new_m_ssm_c_re': 'new_m', 'new_m_ssm_c_im': 'new_m', 'new_m_ssm_d': 'new_m', 'new_m_w_glu': 'new_m', 'new_m_b_glu': 'new_m', 'new_m_w_ssm_up': 'new_m', 'new_m_w_out': 'new_m', 'new_m_g_ffn2': 'new_m', 'new_m_w_ffn2_in': 'new_m', 'new_m_w_ffn2_out': 'new_m', 'new_m_g_final': 'new_m', 'new_v_w_ada': 'new_v', 'new_v_b_ada': 'new_v', 'new_v_g_ffn1': 'new_v', 'new_v_w_ffn1_in': 'new_v', 'new_v_w_ffn1_out': 'new_v', 'new_v_g_mix': 'new_v', 'new_v_w_in': 'new_v', 'new_v_pool_w': 'new_v', 'new_v_pool_b': 'new_v', 'new_v_pool_scale': 'new_v', 'new_v_w_pool_up': 'new_v', 'new_v_ssm_lam_re_log': 'new_v', 'new_v_ssm_lam_im': 'new_v', 'new_v_ssm_log_dt': 'new_v', 'new_v_ssm_b_re': 'new_v', 'new_v_ssm_b_im': 'new_v', 'new_v_ssm_c_re': 'new_v', 'new_v_ssm_c_im': 'new_v', 'new_v_ssm_d': 'new_v', 'new_v_w_glu': 'new_v', 'new_v_b_glu': 'new_v', 'new_v_w_ssm_up': 'new_v', 'new_v_w_out': 'new_v', 'new_v_g_ffn2': 'new_v', 'new_v_w_ffn2_in': 'new_v', 'new_v_w_ffn2_out': 'new_v', 'new_v_g_final': 'new_v'}


def _forward(args):
    return _fwd_reference(*[args[k] for k in FWD_PARAMS])


def _output_shape():
    out = _jax.eval_shape(lambda: _forward(_fwd_setup_inputs(0)))
    return out.shape, out.dtype

N_MICROBATCH = 1
ADAM_LR = 0.001
ADAM_B1 = 0.9
ADAM_B2 = 0.999
ADAM_EPS = 1e-08
ADAM_WD = 0.01
ADAM_STEP = 10
PER_EXAMPLE_BATCH_AXIS = {'x': 0, 'c': 0, 'loss_target': 0}
SHARED_INPUTS = []
_WEIGHT_DTYPES = {'w_ada': _jnp.float32, 'b_ada': _jnp.float32, 'g_ffn1': _jnp.float32, 'w_ffn1_in': _jnp.float32, 'w_ffn1_out': _jnp.float32, 'g_mix': _jnp.float32, 'w_in': _jnp.float32, 'pool_w': _jnp.float32, 'pool_b': _jnp.float32, 'pool_scale': _jnp.float32, 'w_pool_up': _jnp.float32, 'ssm_lam_re_log': _jnp.float32, 'ssm_lam_im': _jnp.float32, 'ssm_log_dt': _jnp.float32, 'ssm_b_re': _jnp.float32, 'ssm_b_im': _jnp.float32, 'ssm_c_re': _jnp.float32, 'ssm_c_im': _jnp.float32, 'ssm_d': _jnp.float32, 'w_glu': _jnp.float32, 'b_glu': _jnp.float32, 'w_ssm_up': _jnp.float32, 'w_out': _jnp.float32, 'g_ffn2': _jnp.float32, 'w_ffn2_in': _jnp.float32, 'w_ffn2_out': _jnp.float32, 'g_final': _jnp.float32}
MOMENT_SCALE = {'w_ada': 3.005137e-02, 'b_ada': 4.860967e-02, 'g_ffn1': 2.926728e-02, 'w_ffn1_in': 1.185267e-02, 'w_ffn1_out': 1.930935e-02, 'g_mix': 3.215761e-02, 'w_in': 1.835143e-02, 'pool_w': 3.850672e-02, 'pool_b': 4.133066e-02, 'pool_scale': 3.941419e-02, 'w_pool_up': 2.708810e-02, 'ssm_lam_re_log': 1.759662e-03, 'ssm_lam_im': 2.629393e-03, 'ssm_log_dt': 1.333233e+00, 'ssm_b_re': 1.233994e-03, 'ssm_b_im': 1.163979e-03, 'ssm_c_re': 1.596144e-03, 'ssm_c_im': 1.534382e-03, 'ssm_d': 1.736461e-02, 'w_glu': 1.156145e-02, 'b_glu': 1.643160e-02, 'w_ssm_up': 1.119503e-02, 'w_out': 2.927312e-02, 'g_ffn2': 2.705675e-02, 'w_ffn2_in': 1.172928e-02, 'w_ffn2_out': 1.912990e-02, 'g_final': 3.195853e+01}


def _to_microbatches(a, axis):
    t = _jnp.moveaxis(a, axis, 0)
    t = t.reshape((N_MICROBATCH, t.shape[0] // N_MICROBATCH) + t.shape[1:])
    return _jnp.moveaxis(t, 1, axis + 1)


def setup_inputs(seed: int = 0) -> dict:
    inp = _fwd_setup_inputs(seed)
    key = _jax.random.fold_in(_jax.random.key(seed), 7919)
    shape, _ = _output_shape()
    out = dict(inp)
    out["loss_target"] = _jax.random.normal(_jax.random.fold_in(key, 0), shape, _jnp.float32)
    for i, name in enumerate(TWIN_WEIGHTS):
        w = inp[name].astype(_jnp.float32)
        if MOMENT_SCALE is None:
            s = _jnp.sqrt(_jnp.mean(_jnp.square(w)) + 1e-30)
        else:
            s = MOMENT_SCALE[name]
        km, kv = _jax.random.split(_jax.random.fold_in(key, i + 1))
        out[name] = w
        out["m_" + name] = s * _jax.random.normal(km, w.shape, _jnp.float32)
        out["v_" + name] = (s * s) * _jax.random.uniform(kv, w.shape, _jnp.float32, 0.5, 1.5)
    if N_MICROBATCH > 1:
        for name, axis in PER_EXAMPLE_BATCH_AXIS.items():
            out[name] = _to_microbatches(out[name], axis)
    return {'x': out['x'], 'c': out['c'], 'w_ada': out['w_ada'], 'b_ada': out['b_ada'], 'g_ffn1': out['g_ffn1'], 'w_ffn1_in': out['w_ffn1_in'], 'w_ffn1_out': out['w_ffn1_out'], 'g_mix': out['g_mix'], 'w_in': out['w_in'], 'pool_w': out['pool_w'], 'pool_b': out['pool_b'], 'pool_scale': out['pool_scale'], 'w_pool_up': out['w_pool_up'], 'ssm_lam_re_log': out['ssm_lam_re_log'], 'ssm_lam_im': out['ssm_lam_im'], 'ssm_log_dt': out['ssm_log_dt'], 'ssm_b_re': out['ssm_b_re'], 'ssm_b_im': out['ssm_b_im'], 'ssm_c_re': out['ssm_c_re'], 'ssm_c_im': out['ssm_c_im'], 'ssm_d': out['ssm_d'], 'w_glu': out['w_glu'], 'b_glu': out['b_glu'], 'w_ssm_up': out['w_ssm_up'], 'w_out': out['w_out'], 'g_ffn2': out['g_ffn2'], 'w_ffn2_in': out['w_ffn2_in'], 'w_ffn2_out': out['w_ffn2_out'], 'g_final': out['g_final'], 'loss_target': out['loss_target'], 'm_w_ada': out['m_w_ada'], 'm_b_ada': out['m_b_ada'], 'm_g_ffn1': out['m_g_ffn1'], 'm_w_ffn1_in': out['m_w_ffn1_in'], 'm_w_ffn1_out': out['m_w_ffn1_out'], 'm_g_mix': out['m_g_mix'], 'm_w_in': out['m_w_in'], 'm_pool_w': out['m_pool_w'], 'm_pool_b': out['m_pool_b'], 'm_pool_scale': out['m_pool_scale'], 'm_w_pool_up': out['m_w_pool_up'], 'm_ssm_lam_re_log': out['m_ssm_lam_re_log'], 'm_ssm_lam_im': out['m_ssm_lam_im'], 'm_ssm_log_dt': out['m_ssm_log_dt'], 'm_ssm_b_re': out['m_ssm_b_re'], 'm_ssm_b_im': out['m_ssm_b_im'], 'm_ssm_c_re': out['m_ssm_c_re'], 'm_ssm_c_im': out['m_ssm_c_im'], 'm_ssm_d': out['m_ssm_d'], 'm_w_glu': out['m_w_glu'], 'm_b_glu': out['m_b_glu'], 'm_w_ssm_up': out['m_w_ssm_up'], 'm_w_out': out['m_w_out'], 'm_g_ffn2': out['m_g_ffn2'], 'm_w_ffn2_in': out['m_w_ffn2_in'], 'm_w_ffn2_out': out['m_w_ffn2_out'], 'm_g_final': out['m_g_final'], 'v_w_ada': out['v_w_ada'], 'v_b_ada': out['v_b_ada'], 'v_g_ffn1': out['v_g_ffn1'], 'v_w_ffn1_in': out['v_w_ffn1_in'], 'v_w_ffn1_out': out['v_w_ffn1_out'], 'v_g_mix': out['v_g_mix'], 'v_w_in': out['v_w_in'], 'v_pool_w': out['v_pool_w'], 'v_pool_b': out['v_pool_b'], 'v_pool_scale': out['v_pool_scale'], 'v_w_pool_up': out['v_w_pool_up'], 'v_ssm_lam_re_log': out['v_ssm_lam_re_log'], 'v_ssm_lam_im': out['v_ssm_lam_im'], 'v_ssm_log_dt': out['v_ssm_log_dt'], 'v_ssm_b_re': out['v_ssm_b_re'], 'v_ssm_b_im': out['v_ssm_b_im'], 'v_ssm_c_re': out['v_ssm_c_re'], 'v_ssm_c_im': out['v_ssm_c_im'], 'v_ssm_d': out['v_ssm_d'], 'v_w_glu': out['v_w_glu'], 'v_b_glu': out['v_b_glu'], 'v_w_ssm_up': out['v_w_ssm_up'], 'v_w_out': out['v_w_out'], 'v_g_ffn2': out['v_g_ffn2'], 'v_w_ffn2_in': out['v_w_ffn2_in'], 'v_w_ffn2_out': out['v_w_ffn2_out'], 'v_g_final': out['v_g_final']}


def _loss(weights, diff, rest, loss_target):
    with _jax.named_scope("forward"):
        args = {**rest, TWIN_DIFF_INPUT: diff, **{k: w.astype(_WEIGHT_DTYPES[k]) for k, w in weights.items()}}
        y = _forward(args)
    with _jax.named_scope("loss_head"):
        err = _jnp.square(y.astype(_jnp.float32) - loss_target)
        return 0.5 * _jnp.sum(_jnp.mean(err, axis=-1)) if err.ndim else 0.5 * err


def _adamw(w, g, m, v):
    m = ADAM_B1 * m + (1.0 - ADAM_B1) * g
    v = ADAM_B2 * v + (1.0 - ADAM_B2) * _jnp.square(g)
    m_hat = m / (1.0 - ADAM_B1 ** ADAM_STEP)
    v_hat = v / (1.0 - ADAM_B2 ** ADAM_STEP)
    delta = -ADAM_LR * (m_hat / (_jnp.sqrt(v_hat) + ADAM_EPS) + ADAM_WD * w)
    return delta, m, v


def reference(x, c, w_ada, b_ada, g_ffn1, w_ffn1_in, w_ffn1_out, g_mix, w_in, pool_w, pool_b, pool_scale, w_pool_up, ssm_lam_re_log, ssm_lam_im, ssm_log_dt, ssm_b_re, ssm_b_im, ssm_c_re, ssm_c_im, ssm_d, w_glu, b_glu, w_ssm_up, w_out, g_ffn2, w_ffn2_in, w_ffn2_out, g_final, loss_target, m_w_ada, m_b_ada, m_g_ffn1, m_w_ffn1_in, m_w_ffn1_out, m_g_mix, m_w_in, m_pool_w, m_pool_b, m_pool_scale, m_w_pool_up, m_ssm_lam_re_log, m_ssm_lam_im, m_ssm_log_dt, m_ssm_b_re, m_ssm_b_im, m_ssm_c_re, m_ssm_c_im, m_ssm_d, m_w_glu, m_b_glu, m_w_ssm_up, m_w_out, m_g_ffn2, m_w_ffn2_in, m_w_ffn2_out, m_g_final, v_w_ada, v_b_ada, v_g_ffn1, v_w_ffn1_in, v_w_ffn1_out, v_g_mix, v_w_in, v_pool_w, v_pool_b, v_pool_scale, v_w_pool_up, v_ssm_lam_re_log, v_ssm_lam_im, v_ssm_log_dt, v_ssm_b_re, v_ssm_b_im, v_ssm_c_re, v_ssm_c_im, v_ssm_d, v_w_glu, v_b_glu, v_w_ssm_up, v_w_out, v_g_ffn2, v_w_ffn2_in, v_w_ffn2_out, v_g_final):
    given = dict(x=x, c=c, w_ada=w_ada, b_ada=b_ada, g_ffn1=g_ffn1, w_ffn1_in=w_ffn1_in, w_ffn1_out=w_ffn1_out, g_mix=g_mix, w_in=w_in, pool_w=pool_w, pool_b=pool_b, pool_scale=pool_scale, w_pool_up=w_pool_up, ssm_lam_re_log=ssm_lam_re_log, ssm_lam_im=ssm_lam_im, ssm_log_dt=ssm_log_dt, ssm_b_re=ssm_b_re, ssm_b_im=ssm_b_im, ssm_c_re=ssm_c_re, ssm_c_im=ssm_c_im, ssm_d=ssm_d, w_glu=w_glu, b_glu=b_glu, w_ssm_up=w_ssm_up, w_out=w_out, g_ffn2=g_ffn2, w_ffn2_in=w_ffn2_in, w_ffn2_out=w_ffn2_out, g_final=g_final, loss_target=loss_target, m_w_ada=m_w_ada, m_b_ada=m_b_ada, m_g_ffn1=m_g_ffn1, m_w_ffn1_in=m_w_ffn1_in, m_w_ffn1_out=m_w_ffn1_out, m_g_mix=m_g_mix, m_w_in=m_w_in, m_pool_w=m_pool_w, m_pool_b=m_pool_b, m_pool_scale=m_pool_scale, m_w_pool_up=m_w_pool_up, m_ssm_lam_re_log=m_ssm_lam_re_log, m_ssm_lam_im=m_ssm_lam_im, m_ssm_log_dt=m_ssm_log_dt, m_ssm_b_re=m_ssm_b_re, m_ssm_b_im=m_ssm_b_im, m_ssm_c_re=m_ssm_c_re, m_ssm_c_im=m_ssm_c_im, m_ssm_d=m_ssm_d, m_w_glu=m_w_glu, m_b_glu=m_b_glu, m_w_ssm_up=m_w_ssm_up, m_w_out=m_w_out, m_g_ffn2=m_g_ffn2, m_w_ffn2_in=m_w_ffn2_in, m_w_ffn2_out=m_w_ffn2_out, m_g_final=m_g_final, v_w_ada=v_w_ada, v_b_ada=v_b_ada, v_g_ffn1=v_g_ffn1, v_w_ffn1_in=v_w_ffn1_in, v_w_ffn1_out=v_w_ffn1_out, v_g_mix=v_g_mix, v_w_in=v_w_in, v_pool_w=v_pool_w, v_pool_b=v_pool_b, v_pool_scale=v_pool_scale, v_w_pool_up=v_w_pool_up, v_ssm_lam_re_log=v_ssm_lam_re_log, v_ssm_lam_im=v_ssm_lam_im, v_ssm_log_dt=v_ssm_log_dt, v_ssm_b_re=v_ssm_b_re, v_ssm_b_im=v_ssm_b_im, v_ssm_c_re=v_ssm_c_re, v_ssm_c_im=v_ssm_c_im, v_ssm_d=v_ssm_d, v_w_glu=v_w_glu, v_b_glu=v_b_glu, v_w_ssm_up=v_w_ssm_up, v_w_out=v_w_out, v_g_ffn2=v_g_ffn2, v_w_ffn2_in=v_w_ffn2_in, v_w_ffn2_out=v_w_ffn2_out, v_g_final=v_g_final)
    weights = {n: given[n] for n in TWIN_WEIGHTS}
    shared = {n: given[n] for n in SHARED_INPUTS}
    per_example = {n: given[n] for n in ['x', 'c']}
    grad_fn = _jax.value_and_grad(_loss, argnums=(0, 1))

    def one_microbatch(ex, loss_target):
        ex = dict(ex)
        diff = ex.pop(TWIN_DIFF_INPUT)
        return grad_fn(weights, diff, {**shared, **ex}, loss_target)

    if N_MICROBATCH == 1:
        loss, (grad_w, grad_x) = one_microbatch(per_example, given["loss_target"])
    else:
        def body(carry, xs):
            loss_sum, grad_sum = carry
            l_k, (gw_k, gx_k) = one_microbatch(xs[0], xs[1])
            with _jax.named_scope("update"):
                return (loss_sum + l_k, _jax.tree.map(_jnp.add, grad_sum, gw_k)), gx_k

        init = (_jnp.zeros((), _jnp.float32), _jax.tree.map(_jnp.zeros_like, weights))
        (loss, grad_w), grad_x = _jax.lax.scan(body, init, (per_example, given["loss_target"]))
    with _jax.named_scope("update"):
        delta_w, new_m, new_v = {}, {}, {}
        for n in TWIN_WEIGHTS:
            delta_w[n], new_m[n], new_v[n] = _adamw(weights[n], grad_w[n], given["m_" + n], given["v_" + n])
    return (loss, grad_x, *[grad_w[n] for n in TWIN_WEIGHTS], *[delta_w[n] for n in TWIN_WEIGHTS],
            *[new_m[n] for n in TWIN_WEIGHTS], *[new_v[n] for n in TWIN_WEIGHTS])
```

```python
import functools

import jax
import jax.numpy as jnp
from jax import lax
from jax.experimental import pallas as pl
from jax.experimental.pallas import tpu as pltpu

F32 = jnp.float32
MXU_DTYPE = jnp.bfloat16
WIRE_DTYPE = jnp.bfloat16

NDEV = 8
D_MODEL = 1024
D_FF = 2816
FF_SHARD = 2 * D_FF // NDEV
POOL_WIDTH = 512
POOL_GROUP = 128
SSM_WIDTH = 512
SSM_STATE = 64
SSM_GROUP = 16
SSM_BLOCKS = 4
SSM_BLOCK_STATE = 512
N_STATE = 2048
IN_WIDTH = 3072
EPS = 1e-6
ADAM_LR = 0.001
ADAM_B1 = 0.9
ADAM_B2 = 0.999
ADAM_EPS = 1e-08
ADAM_WD = 0.01
ADAM_STEP = 10

TM_FFN = 512
TM_MIX = 256
TM_MIX_BWD = 128
TM_EW = 512
SCAN_ROWS = 8
POOL_HALO = 16
VMEM_LIMIT = 60 * 1024 * 1024

ROW_G_FFN1, ROW_G_MIX, ROW_G_FFN2, ROW_G_FINAL = 9, 10, 11, 12
ROW_POOL_B, ROW_POOL_SCALE, ROW_SSM_D, ROW_B_GLU = 0, 1, 2, 3

SMALL_PARAMS = (
    ("g_ffn1", 1024), ("g_mix", 1024), ("g_ffn2", 1024), ("g_final", 1024), ("pool_w", 65536),
    ("pool_b", 512), ("pool_scale", 512), ("ssm_lam_re_log", 2048), ("ssm_lam_im", 2048),
    ("ssm_log_dt", 32), ("ssm_b_re", 32768), ("ssm_b_im", 32768), ("ssm_c_re", 32768),
    ("ssm_c_im", 32768), ("ssm_d", 512), ("b_glu", 1024),
)
SMALL_TOTAL = sum(n for _, n in SMALL_PARAMS)
ADA_ROWS = 9
REST_ROWS = 203
PACK_ROWS = 216
MESH = pl.DeviceIdType.MESH


def _mm(a, b):
    return jnp.dot(a.astype(MXU_DTYPE), b.astype(MXU_DTYPE), preferred_element_type=F32)


def _mm_nt(a, b):
    return lax.dot_general(a.astype(MXU_DTYPE), b.astype(MXU_DTYPE), (((1,), (1,)), ((), ())),
                           preferred_element_type=F32)


def _mm_tn(a, b):
    return lax.dot_general(a.astype(MXU_DTYPE), b.astype(MXU_DTYPE), (((0,), (0,)), ((), ())),
                           preferred_element_type=F32)


def _rms_scale(x):
    return lax.rsqrt(jnp.mean(x * x, axis=-1, keepdims=True) + EPS)


def _sigmoid(x):
    return jax.nn.sigmoid(x)


def _colsum(x):
    return jnp.sum(x, axis=0, keepdims=True)


def _row(ref, r):
    return ref[r:r + 1, :]


def _params(*sem):
    return pltpu.CompilerParams(dimension_semantics=sem, vmem_limit_bytes=VMEM_LIMIT)


def _resident(a):
    return pl.BlockSpec(a.shape, lambda *_: (0,) * a.ndim, pipeline_mode=pl.Buffered(1))


def _me():
    return lax.axis_index("x"), lax.axis_index("y"), lax.axis_index("c")


def _peer(rel):
    x, y, c = _me()
    px = 1 - x if rel & 4 else x
    py = 1 - y if rel & 2 else y
    pc = 1 - c if rel & 1 else c
    return (px, py, pc), 4 * px + 2 * py + pc


def _allgather_weights(shards):
    n = len(shards)

    def body(*refs):
        ins, outs, stage = refs[:n], refs[n:2 * n], refs[2 * n:3 * n]
        send_sems, recv_sems, local_sems = refs[3 * n:]
        x, y, c = _me()
        me = 4 * x + 2 * y + c
        sibling = (x, y, 1 - c)
        chips = [(1 - x, y), (x, 1 - y), (1 - x, 1 - y)]

        def blk(px, py, pc):
            return 4 * px + 2 * py + pc

        def copy(a, k, block, to, src=None):
            return pltpu.make_async_remote_copy(
                src_ref=outs[a].at[block] if src is None else src, dst_ref=outs[a].at[block],
                send_sem=send_sems.at[7 * a + k], recv_sem=recv_sems.at[7 * a + k],
                device_id=to, device_id_type=MESH)

        started = []
        for a in range(n):
            stage[a][...] = ins[a][...].astype(WIRE_DTYPE)
            mine = pltpu.make_async_copy(stage[a], outs[a].at[me], local_sems.at[a])
            mine.start()
            started.append(mine)
            first = [copy(a, 0, me, sibling, src=stage[a])]
            first += [copy(a, 1 + j, me, (*chip, c), src=stage[a]) for j, chip in enumerate(chips)]
            for cp in first:
                cp.start()
            started += [cp for cp in first]
        sends = []
        for a in range(n):
            for j, chip in enumerate(chips):
                copy(a, 1 + j, blk(*chip, c), (x, y, c)).wait_recv()
                fwd = copy(a, 4 + j, blk(*chip, c), sibling)
                fwd.start()
                sends.append(fwd)
        for a in range(n):
            copy(a, 0, blk(x, y, 1 - c), (x, y, c)).wait_recv()
            for j, chip in enumerate(chips):
                copy(a, 4 + j, blk(*chip, 1 - c), (x, y, c)).wait_recv()
        for a in range(n):
            started[5 * a].wait()
            for cp in started[5 * a + 1:5 * a + 5]:
                cp.wait_send()
        for cp in sends:
            cp.wait_send()

    return pl.pallas_call(
        body, name="allgather_weights",
        out_shape=[jax.ShapeDtypeStruct((NDEV,) + s.shape, WIRE_DTYPE) for s in shards],
        in_specs=[pl.BlockSpec(memory_space=pltpu.VMEM)] * n,
        out_specs=[pl.BlockSpec(memory_space=pl.ANY)] * n,
        scratch_shapes=[pltpu.VMEM(s.shape, WIRE_DTYPE) for s in shards]
        + [pltpu.SemaphoreType.DMA((7 * n,)), pltpu.SemaphoreType.DMA((7 * n,)), pltpu.SemaphoreType.DMA((n,))],
        compiler_params=pltpu.CompilerParams(vmem_limit_bytes=VMEM_LIMIT),
    )(*shards)


def _ada_forward(c_row, w_ada, b_ada8):
    cols = w_ada.shape[1]

    def body(c_ref, w_ref, b_ref, mod_ref, sc_ref, c_all, send_buf, recv_buf, send1, recv1, send2, recv2):
        x, y, c = _me()
        me = 4 * x + 2 * y + c
        rowi = lax.broadcasted_iota(jnp.int32, (8, D_MODEL), 0)
        c_all[me] = jnp.broadcast_to(c_ref[...], (8, D_MODEL))
        copies = []
        for rel in range(1, 8):
            to, _ = _peer(rel)
            cp = pltpu.make_async_remote_copy(src_ref=c_all.at[me], dst_ref=c_all.at[me], send_sem=send1.at[rel - 1],
                                              recv_sem=recv1.at[rel - 1], device_id=to, device_id_type=MESH)
            cp.start()
            copies.append(cp)
        for rel in range(1, 8):
            _, p = _peer(rel)
            pltpu.make_async_remote_copy(src_ref=c_all.at[p], dst_ref=c_all.at[p], send_sem=send1.at[rel - 1],
                                         recv_sem=recv1.at[rel - 1], device_id=(x, y, c), device_id_type=MESH).wait_recv()
        for cp in copies:
            cp.wait_send()
        cmat = jnp.zeros((8, D_MODEL), F32)
        for b in range(8):
            cmat = jnp.where(rowi == b, c_all[b], cmat)
        sc = cmat * _sigmoid(cmat)
        sc_ref[...] = sc
        modcols = _mm(sc, w_ref[...]) + b_ref[pl.ds(me, 1), :]
        for b in range(8):
            send_buf[b] = jnp.broadcast_to(modcols[b:b + 1, :], (8, cols))
        recv_buf[me] = send_buf[me]
        copies = []
        for rel in range(1, 8):
            to, p = _peer(rel)
            cp = pltpu.make_async_remote_copy(src_ref=send_buf.at[p], dst_ref=recv_buf.at[me], send_sem=send2.at[rel - 1],
                                              recv_sem=recv2.at[rel - 1], device_id=to, device_id_type=MESH)
            cp.start()
            copies.append(cp)
        for rel in range(1, 8):
            _, p = _peer(rel)
            pltpu.make_async_remote_copy(src_ref=send_buf.at[p], dst_ref=recv_buf.at[p], send_sem=send2.at[rel - 1],
                                         recv_sem=recv2.at[rel - 1], device_id=(x, y, c), device_id_type=MESH).wait_recv()
        for cp in copies:
            cp.wait_send()
        rowc = lax.broadcasted_iota(jnp.int32, (8, cols), 0)
        out = jnp.zeros((8, cols), F32)
        for k in range(8):
            out = jnp.where(rowc == k, recv_buf[k], out)
        mod_ref[...] = out

    return pl.pallas_call(
        body, name="ada_forward",
        out_shape=[jax.ShapeDtypeStruct((8, cols), F32), jax.ShapeDtypeStruct((8, D_MODEL), F32)],
        in_specs=[pl.BlockSpec(memory_space=pltpu.VMEM)] * 3,
        out_specs=[pl.BlockSpec(memory_space=pltpu.VMEM)] * 2,
        scratch_shapes=[pltpu.VMEM((8, 8, D_MODEL), F32), pltpu.VMEM((8, 8, cols), F32), pltpu.VMEM((8, 8, cols), F32)]
        + [pltpu.SemaphoreType.DMA((7,))] * 4,
        compiler_params=pltpu.CompilerParams(vmem_limit_bytes=VMEM_LIMIT),
    )(c_row, w_ada, b_ada8)


def _allreduce_small(pack):
    rows = pack.shape[1]

    def body(pack_ref, total_ref, land_ref, send1, recv1, send2, recv2):
        x, y, c = _me()
        me = 4 * x + 2 * y + c
        land_ref[me] = pack_ref[me]
        copies = []
        for rel in range(1, 8):
            to, p = _peer(rel)
            cp = pltpu.make_async_remote_copy(src_ref=pack_ref.at[p], dst_ref=land_ref.at[me], send_sem=send1.at[rel - 1],
                                              recv_sem=recv1.at[rel - 1], device_id=to, device_id_type=MESH)
            cp.start()
            copies.append(cp)
        for rel in range(1, 8):
            _, p = _peer(rel)
            pltpu.make_async_remote_copy(src_ref=pack_ref.at[p], dst_ref=land_ref.at[p], send_sem=send1.at[rel - 1],
                                         recv_sem=recv1.at[rel - 1], device_id=(x, y, c), device_id_type=MESH).wait_recv()
        for cp in copies:
            cp.wait_send()
        acc = land_ref[0]
        for b in range(1, 8):
            acc = acc + land_ref[b]
        total_ref[me] = acc
        copies = []
        for rel in range(1, 8):
            to, _ = _peer(rel)
            cp = pltpu.make_async_remote_copy(src_ref=total_ref.at[me], dst_ref=total_ref.at[me], send_sem=send2.at[rel - 1],
                                              recv_sem=recv2.at[rel - 1], device_id=to, device_id_type=MESH)
            cp.start()
            copies.append(cp)
        for rel in range(1, 8):
            _, p = _peer(rel)
            pltpu.make_async_remote_copy(src_ref=total_ref.at[p], dst_ref=total_ref.at[p], send_sem=send2.at[rel - 1],
                                         recv_sem=recv2.at[rel - 1], device_id=(x, y, c), device_id_type=MESH).wait_recv()
        for cp in copies:
            cp.wait_send()

    return pl.pallas_call(
        body, name="allreduce_small",
        out_shape=[jax.ShapeDtypeStruct((8, rows, 128), F32), jax.ShapeDtypeStruct((8, rows, 128), F32)],
        in_specs=[pl.BlockSpec(memory_space=pltpu.VMEM)],
        out_specs=[pl.BlockSpec(memory_space=pltpu.VMEM)] * 2,
        scratch_shapes=[pltpu.SemaphoreType.DMA((7,))] * 4,
        compiler_params=pltpu.CompilerParams(vmem_limit_bytes=VMEM_LIMIT),
    )(pack)


def _scatter_grads(grads, views, shard_shapes):
    n = len(grads)

    def body(*refs):
        ins, outs = refs[:n], refs[n:2 * n]
        send_sems, recv_sems, local_sems = refs[2 * n:]
        x, y, c = _me()
        me = 4 * x + 2 * y + c
        started = []
        for a in range(n):
            mine = pltpu.make_async_copy(views[a](ins[a], me), outs[a].at[me], local_sems.at[a])
            mine.start()
            started.append(mine)
        sends = []
        for rel in range(1, 8):
            to, p = _peer(rel)
            for a in range(n):
                cp = pltpu.make_async_remote_copy(
                    src_ref=views[a](ins[a], p), dst_ref=outs[a].at[me], send_sem=send_sems.at[7 * a + rel - 1],
                    recv_sem=recv_sems.at[7 * a + rel - 1], device_id=to, device_id_type=MESH)
                cp.start()
                sends.append(cp)
        for rel in range(1, 8):
            _, p = _peer(rel)
            for a in range(n):
                pltpu.make_async_remote_copy(
                    src_ref=views[a](ins[a], p), dst_ref=outs[a].at[p], send_sem=send_sems.at[7 * a + rel - 1],
                    recv_sem=recv_sems.at[7 * a + rel - 1], device_id=(x, y, c), device_id_type=MESH).wait_recv()
        for cp in sends:
            cp.wait_send()
        for cp in started:
            cp.wait()

    return pl.pallas_call(
        body, name="scatter_grads",
        out_shape=[jax.ShapeDtypeStruct((NDEV,) + s, g.dtype) for s, g in zip(shard_shapes, grads)],
        in_specs=[pl.BlockSpec(memory_space=pl.ANY)] * n,
        out_specs=[pl.BlockSpec(memory_space=pl.ANY)] * n,
        scratch_shapes=[pltpu.SemaphoreType.DMA((7 * n,)), pltpu.SemaphoreType.DMA((7 * n,)),
                        pltpu.SemaphoreType.DMA((n,))],
        compiler_params=pltpu.CompilerParams(vmem_limit_bytes=VMEM_LIMIT),
    )(*grads)


def _modulated(x, prm_ref, sub, g_row):
    shift, scale = _row(prm_ref, 3 * sub), _row(prm_ref, 3 * sub + 1)
    g = _row(prm_ref, g_row)
    r = _rms_scale(x)
    n0 = x * r
    return (n0 * g) * (1.0 + scale) + shift, r, n0


def _ffn_forward(x, prm, win, wout, sub, g_row, name):
    T = x.shape[0]
    tm = min(T, TM_FFN)

    def body(x_ref, prm_ref, win_ref, wout_ref, xo_ref, f_ref):
        xv = x_ref[...]
        h, _, _ = _modulated(xv, prm_ref, sub, g_row)
        hb = h.astype(MXU_DTYPE)
        acc = None
        for j in range(4):
            a = _mm(hb, win_ref[0, j])
            b = _mm(hb, win_ref[1, j])
            s = (a * _sigmoid(a)) * b
            t = _mm(s, wout_ref[j])
            acc = t if acc is None else acc + t
        f_ref[...] = acc
        xo_ref[...] = xv + (0.5 * _row(prm_ref, 3 * sub + 2)) * acc

    tok = pl.BlockSpec((tm, D_MODEL), lambda i: (i, 0))
    return pl.pallas_call(
        body, name=name, grid=(T // tm,),
        out_shape=[jax.ShapeDtypeStruct((T, D_MODEL), F32)] * 2,
        in_specs=[tok, pl.BlockSpec(prm.shape, lambda i: (0, 0)),
                  pl.BlockSpec(win.shape, lambda i: (0, 0, 0, 0), pipeline_mode=pl.Buffered(1)),
                  pl.BlockSpec(wout.shape, lambda i: (0, 0, 0), pipeline_mode=pl.Buffered(1))],
        out_specs=[tok, tok],
        compiler_params=_params("arbitrary"),
    )(x, prm, win, wout)


def _ffn_backward(x, d, prm, win, wout, sub, g_row, name):
    T = x.shape[0]
    tm = min(T, TM_FFN)
    nt = T // tm

    def body(x_ref, d_ref, prm_ref, win_ref, wout_ref, dh_ref, dwin_ref, dwout_ref, acc_in, acc_out):
        i = pl.program_id(1)

        @pl.when(i == 0)
        def _():
            acc_in[...] = jnp.zeros_like(acc_in)
            acc_out[...] = jnp.zeros_like(acc_out)

        h, _, _ = _modulated(x_ref[...], prm_ref, sub, g_row)
        hb = h.astype(MXU_DTYPE)
        wa, wb, wo = win_ref[0, 0], win_ref[1, 0], wout_ref[0]
        a = _mm(hb, wa)
        b = _mm(hb, wb)
        sg = _sigmoid(a)
        si = a * sg
        dfs = ((0.5 * _row(prm_ref, 3 * sub + 2)) * d_ref[...]).astype(MXU_DTYPE)
        ds = _mm_nt(dfs, wo)
        acc_out[...] += _mm_tn(si * b, dfs)
        da = ds * b * (sg * (1.0 + a * (1.0 - sg)))
        db = ds * si
        acc_in[0] += _mm_tn(hb, da)
        acc_in[1] += _mm_tn(hb, db)
        dh_ref[0] = _mm_nt(da, wa) + _mm_nt(db, wb)

        @pl.when(i == nt - 1)
        def _():
            dwin_ref[0, 0] = acc_in[0].astype(WIRE_DTYPE)
            dwin_ref[1, 0] = acc_in[1].astype(WIRE_DTYPE)
            dwout_ref[0] = acc_out[...].astype(WIRE_DTYPE)

    tok = pl.BlockSpec((tm, D_MODEL), lambda j, i: (i, 0))
    return pl.pallas_call(
        body, name=name, grid=(4, nt),
        out_shape=[jax.ShapeDtypeStruct((4, T, D_MODEL), F32),
                   jax.ShapeDtypeStruct(win.shape, WIRE_DTYPE), jax.ShapeDtypeStruct(wout.shape, WIRE_DTYPE)],
        in_specs=[tok, tok, pl.BlockSpec(prm.shape, lambda j, i: (0, 0)),
                  pl.BlockSpec((2, 1, D_MODEL, FF_SHARD), lambda j, i: (0, j, 0, 0)),
                  pl.BlockSpec((1, FF_SHARD, D_MODEL), lambda j, i: (j, 0, 0))],
        out_specs=[pl.BlockSpec((1, tm, D_MODEL), lambda j, i: (j, i, 0)),
                   pl.BlockSpec((2, 1, D_MODEL, FF_SHARD), lambda j, i: (0, j, 0, 0)),
                   pl.BlockSpec((1, FF_SHARD, D_MODEL), lambda j, i: (j, 0, 0))],
        scratch_shapes=[pltpu.VMEM((2, D_MODEL, FF_SHARD), F32), pltpu.VMEM((FF_SHARD, D_MODEL), F32)],
        compiler_params=_params("arbitrary", "arbitrary"),
    )(x, d, prm, win, wout)


def _norm_backward(parts, x, d, f, prm, sub, g_row, gate_coef, name):
    T = x.shape[0]
    tm = min(T, TM_EW)
    P = parts.shape[0]

    def body(p_ref, x_ref, d_ref, f_ref, prm_ref, dx_ref, sums_ref):
        i = pl.program_id(0)
        dh = p_ref[0]
        for k in range(1, P):
            dh = dh + p_ref[k]
        xv, dv = x_ref[...], d_ref[...]
        scale, g = _row(prm_ref, 3 * sub + 1), _row(prm_ref, g_row)
        r = _rms_scale(xv)
        n0 = xv * r
        dn = dh * (1.0 + scale)
        dn0 = dn * g
        dx_ref[...] = dv + r * (dn0 - n0 * jnp.mean(dn0 * n0, axis=-1, keepdims=True))
        upd = jnp.concatenate([_colsum(dn * n0), _colsum(dh), _colsum(dh * (n0 * g)),
                               gate_coef * _colsum(dv * f_ref[...]), jnp.zeros((4, D_MODEL), F32)], axis=0)

        @pl.when(i == 0)
        def _():
            sums_ref[...] = upd

        @pl.when(i > 0)
        def _():
            sums_ref[...] += upd

    tok = pl.BlockSpec((tm, D_MODEL), lambda i: (i, 0))
    return pl.pallas_call(
        body, name=name, grid=(T // tm,),
        out_shape=[jax.ShapeDtypeStruct((T, D_MODEL), F32), jax.ShapeDtypeStruct((8, D_MODEL), F32)],
        in_specs=[pl.BlockSpec((P, tm, D_MODEL), lambda i: (0, i, 0)), tok, tok, tok,
                  pl.BlockSpec(prm.shape, lambda i: (0, 0))],
        out_specs=[tok, pl.BlockSpec((8, D_MODEL), lambda i: (0, 0))],
        compiler_params=_params("arbitrary"),
    )(parts, x, d, f, prm)


def _final_loss(x, target, prm):
    T = x.shape[0]
    tm = min(T, TM_EW)

    def body(x_ref, t_ref, prm_ref, dx_ref, sums_ref):
        i = pl.program_id(0)
        xv = x_ref[...]
        g = _row(prm_ref, ROW_G_FINAL)
        r = _rms_scale(xv)
        n0 = xv * r
        err = n0 * g - t_ref[...]
        dy = err / float(D_MODEL)
        dn0 = dy * g
        dx_ref[...] = r * (dn0 - n0 * jnp.mean(dn0 * n0, axis=-1, keepdims=True))
        loss = 0.5 * jnp.sum(jnp.mean(err * err, axis=-1, keepdims=True), axis=0, keepdims=True)
        upd = jnp.concatenate([_colsum(dy * n0), jnp.broadcast_to(loss, (1, D_MODEL)), jnp.zeros((6, D_MODEL), F32)], axis=0)

        @pl.when(i == 0)
        def _():
            sums_ref[...] = upd

        @pl.when(i > 0)
        def _():
            sums_ref[...] += upd

    tok = pl.BlockSpec((tm, D_MODEL), lambda i: (i, 0))
    return pl.pallas_call(
        body, name="final_loss", grid=(T // tm,),
        out_shape=[jax.ShapeDtypeStruct((T, D_MODEL), F32), jax.ShapeDtypeStruct((8, D_MODEL), F32)],
        in_specs=[tok, tok, pl.BlockSpec(prm.shape, lambda i: (0, 0))],
        out_specs=[tok, pl.BlockSpec((8, D_MODEL), lambda i: (0, 0))],
        compiler_params=_params("arbitrary"),
    )(x, target, prm)


def _ssm_discretise(lam_re_log, lam_im, log_dt):
    lr = -jnp.exp(lam_re_log)
    dt = jnp.exp(log_dt)
    mag = jnp.exp(lr * dt)
    ang = lam_im * dt
    ab_re = mag * jnp.cos(ang)
    ab_im = mag * jnp.sin(ang)
    num_re = ab_re - 1.0
    num_im = ab_im
    den = lr * lr + lam_im * lam_im
    f_re = (num_re * lr + num_im * lam_im) / den
    f_im = (num_im * lr - num_re * lam_im) / den
    return ab_re, ab_im, f_re, f_im


def _ssm_params_forward(lam_re_log, lam_im, log_dt):
    def body(a_ref, b_ref, c_ref, o0, o1, o2, o3):
        outs = _ssm_discretise(a_ref[...], b_ref[...], c_ref[...])
        for o, v in zip((o0, o1, o2, o3), outs):
            o[...] = v

    return pl.pallas_call(body, name="ssm_params_forward",
                          out_shape=[jax.ShapeDtypeStruct(lam_im.shape, F32)] * 4)(lam_re_log, lam_im, log_dt)


def _ssm_params_backward(lam_re_log, lam_im, log_dt, cot):
    def body(a_ref, b_ref, c_ref, g0, g1, g2, g3, o0, o1, o2):
        _, vjp = jax.vjp(_ssm_discretise, a_ref[...], b_ref[...], c_ref[...])
        d0, d1, d2 = vjp((g0[...], g1[...], g2[...], g3[...]))
        o0[...] = d0
        o1[...] = d1
        o2[...] = d2

    return pl.pallas_call(
        body, name="ssm_params_backward",
        out_shape=[jax.ShapeDtypeStruct(lam_im.shape, F32), jax.ShapeDtypeStruct(lam_im.shape, F32),
                   jax.ShapeDtypeStruct(log_dt.shape, F32)])(lam_re_log, lam_im, log_dt, *cot)


def _ssm_dense_forward(srow, b_dense, c_dense):
    def body(srow_ref, bd_ref, cd_ref, bb_ref, ct_ref):
        for j in range(SSM_BLOCKS):
            lanes = slice(j * SSM_BLOCK_STATE, (j + 1) * SSM_BLOCK_STATE)
            f_re, f_im = srow_ref[2:3, lanes], srow_ref[3:4, lanes]
            bb_ref[0, j] = (f_re * bd_ref[0, j] - f_im * bd_ref[1, j]).astype(MXU_DTYPE)
            bb_ref[1, j] = (f_re * bd_ref[1, j] + f_im * bd_ref[0, j]).astype(MXU_DTYPE)
            ct_ref[0, j] = cd_ref[0, j].astype(MXU_DTYPE)
            ct_ref[1, j] = cd_ref[1, j].astype(MXU_DTYPE)

    return pl.pallas_call(body, name="ssm_dense_forward",
                          out_shape=[jax.ShapeDtypeStruct(b_dense.shape, MXU_DTYPE),
                                     jax.ShapeDtypeStruct(c_dense.shape, MXU_DTYPE)],
                          compiler_params=pltpu.CompilerParams(vmem_limit_bytes=VMEM_LIMIT))(srow, b_dense, c_dense)


def _cmul(p, q):
    return p[0] * q[0] - p[1] * q[1], p[0] * q[1] + p[1] * q[0]


def _scan_coefficients(ar, ai, reverse):
    n = ar.shape[1]
    p = {1: (ar, ai)}
    p[2] = _cmul(p[1], p[1])
    p[3] = _cmul(p[2], p[1])
    p[4] = _cmul(p[2], p[2])
    p[5] = _cmul(p[4], p[1])
    p[6] = _cmul(p[4], p[2])
    p[7] = _cmul(p[4], p[3])
    p[8] = _cmul(p[4], p[4])
    rowi = lax.broadcasted_iota(jnp.int32, (SCAN_ROWS, n), 0)
    tiles = []
    for dstep in (1, 2, 4):
        keep = (rowi < SCAN_ROWS - dstep) if reverse else (rowi >= dstep)
        for part in p[dstep]:
            tiles.append(jnp.where(keep, jnp.broadcast_to(part, (SCAN_ROWS, n)), 0.0))
    for comp in (0, 1):
        t = jnp.zeros((SCAN_ROWS, n), F32)
        for rr in range(SCAN_ROWS):
            power = SCAN_ROWS - rr if reverse else rr + 1
            t = jnp.where(rowi == rr, jnp.broadcast_to(p[power][comp], (SCAN_ROWS, n)), t)
        tiles.append(t)
    return tiles


def _load_stack(stack_hbm, dst, sems, base):
    cols = stack_hbm.shape[2]
    cps = [pltpu.make_async_copy(stack_hbm.at[k], dst.at[:, pl.ds(k * cols, cols)], sems.at[base + k])
           for k in range(NDEV)]
    for cp in cps:
        cp.start()
    return cps


def _window_lanes():
    lane = lax.broadcasted_iota(jnp.int32, (1, POOL_WIDTH), 1)
    return jnp.where(lane < 128, 2.0, jnp.where(lane < 256, 4.0, jnp.where(lane < 384, 8.0, 16.0)))


def _gelu(y):
    return 0.5 * y * (1.0 + lax.erf(y * 0.7071067811865476))


def _gelu_grad(y):
    return 0.5 * (1.0 + lax.erf(y * 0.7071067811865476)) + y * jnp.exp(-0.5 * y * y) * 0.3989422804014327


def _mixer_forward(x, prm, w_in_s, w_pu_s, w_glu_s, w_su_s, w_out, pool_w, mvec, srow, bb, ct):
    T = x.shape[0]
    tm = min(T, TM_MIX)
    nt = T // tm
    n_tiles = tm // SCAN_ROWS

    def body(x_ref, prm_ref, w_in_h, w_pu_h, w_glu_h, w_su_h, w_out_h, pw_ref, mv_ref, srow_ref, bb, ct,
             x2_ref, mo_ref, z_ref, sre_ref, sim_ref, zp_ref, q_ref, yp_ref, yss_ref, vg_ref, ys_ref,
             w_in, w_pu, w_glu, w_su, w_o, coef, carry, hist, bu, sems):
        i = pl.program_id(0)

        @pl.when(i == 0)
        def _():
            cps = (_load_stack(w_in_h, w_in, sems, 0) + _load_stack(w_pu_h, w_pu, sems, 8)
                   + _load_stack(w_glu_h, w_glu, sems, 16) + _load_stack(w_su_h, w_su, sems, 24))
            cps.append(pltpu.make_async_copy(w_out_h, w_o, sems.at[32]))
            cps[-1].start()
            for j in range(SSM_BLOCKS):
                lanes = slice(j * SSM_BLOCK_STATE, (j + 1) * SSM_BLOCK_STATE)
                for k, t in enumerate(_scan_coefficients(srow_ref[0:1, lanes], srow_ref[1:2, lanes], False)):
                    coef[j, k] = t
            carry[...] = jnp.zeros_like(carry)
            hist[...] = jnp.zeros_like(hist)
            for cp in cps:
                cp.wait()

        xv = x_ref[...]
        h, _, _ = _modulated(xv, prm_ref, 1, ROW_G_MIX)
        z = _mm(h, w_in[...])
        z_ref[...] = z
        u_pool, u_ssm = z[:, 0:512], z[:, 512:1024]
        gl_pool, gl_ssm = z[:, 1024:2048], z[:, 2048:3072]

        ext = jnp.concatenate([hist[...], u_pool], axis=0)
        w2 = ext + pltpu.roll(ext, 1, 0)
        w4 = w2[:, 128:] + pltpu.roll(w2[:, 128:], 2, 0)
        w8 = w4[:, 128:] + pltpu.roll(w4[:, 128:], 4, 0)
        w16 = w8[:, 128:] + pltpu.roll(w8[:, 128:], 8, 0)
        wsum = jnp.concatenate([w2[POOL_HALO:, :128], w4[POOL_HALO:, :128], w8[POOL_HALO:, :128], w16[POOL_HALO:]], axis=1)
        hist[...] = u_pool[tm - POOL_HALO:, :]
        t1 = (lax.broadcasted_iota(jnp.int32, (tm, 1), 0) + (i * tm + 1)).astype(F32)
        zp = wsum / jnp.minimum(t1, _window_lanes()) - u_pool
        zp_ref[...] = zp
        q = jnp.concatenate([_mm(zp[:, k * 128:(k + 1) * 128], pw_ref[k]) for k in range(4)], axis=1)
        q = q + mv_ref[ROW_POOL_B:ROW_POOL_B + 1, 0:512]
        q_ref[...] = q
        y_pool = _mm(q * mv_ref[ROW_POOL_SCALE:ROW_POOL_SCALE + 1, 0:512], w_pu[...])
        yp_ref[...] = y_pool

        y_blocks = []
        for j in range(SSM_BLOCKS):
            lanes = pl.ds(j * SSM_BLOCK_STATE, SSM_BLOCK_STATE)
            ub = u_ssm[:, j * 128:(j + 1) * 128].astype(MXU_DTYPE)
            bu[0] = _mm(ub, bb[0, j])
            bu[1] = _mm(ub, bb[1, j])
            a1r, a1i, a2r, a2i, a4r, a4i, pr, pi = [coef[j, k] for k in range(8)]

            def step(tt, c, lanes=lanes, a1r=a1r, a1i=a1i, a2r=a2r, a2i=a2i, a4r=a4r, a4i=a4i, pr=pr, pi=pi):
                cr, ci = c
                rows = pl.ds(pl.multiple_of(tt * SCAN_ROWS, SCAN_ROWS), SCAN_ROWS)
                xr, xi = bu[0, rows, :], bu[1, rows, :]
                for dstep, kr, ki in ((1, a1r, a1i), (2, a2r, a2i), (4, a4r, a4i)):
                    sr, si = pltpu.roll(xr, dstep, 0), pltpu.roll(xi, dstep, 0)
                    xr, xi = xr + kr * sr - ki * si, xi + kr * si + ki * sr
                xr, xi = xr + pr * cr - pi * ci, xi + pr * ci + pi * cr
                sre_ref[rows, lanes] = xr
                sim_ref[rows, lanes] = xi
                return (jnp.broadcast_to(xr[SCAN_ROWS - 1:SCAN_ROWS, :], xr.shape),
                        jnp.broadcast_to(xi[SCAN_ROWS - 1:SCAN_ROWS, :], xi.shape))

            cr, ci = lax.fori_loop(0, n_tiles, step, (carry[j, 0], carry[j, 1]))
            carry[j, 0] = cr
            carry[j, 1] = ci
            y_blocks.append(_mm(sre_ref[:, lanes], ct[0, j]) - _mm(sim_ref[:, lanes], ct[1, j]))
        yss = jnp.concatenate(y_blocks, axis=1) + mv_ref[ROW_SSM_D:ROW_SSM_D + 1, 0:512] * u_ssm
        yss_ref[...] = yss
        vg = _mm(_gelu(yss), w_glu[...]) + mv_ref[ROW_B_GLU:ROW_B_GLU + 1, :]
        vg_ref[...] = vg
        y_ssm = _mm(vg[:, 0:512] * _sigmoid(vg[:, 512:1024]), w_su[...])
        ys_ref[...] = y_ssm

        merged = _sigmoid(gl_pool) * y_pool + _sigmoid(gl_ssm) * y_ssm
        mo = _mm(merged, w_o[...])
        mo_ref[...] = mo
        x2_ref[...] = xv + _row(prm_ref, 5) * mo

    def tok(width):
        return pl.BlockSpec((tm, width), lambda i: (i, 0))

    hbm = pl.BlockSpec(memory_space=pl.ANY)
    widths = (D_MODEL, D_MODEL, IN_WIDTH, N_STATE, N_STATE, 512, 512, D_MODEL, 512, D_MODEL, D_MODEL)
    return pl.pallas_call(
        body, name="mixer_forward", grid=(nt,),
        out_shape=[jax.ShapeDtypeStruct((T, w), F32) for w in widths],
        in_specs=[tok(D_MODEL), _resident(prm), hbm, hbm, hbm, hbm, hbm, _resident(pool_w), _resident(mvec),
                  _resident(srow), _resident(bb), _resident(ct)],
        out_specs=[tok(w) for w in widths],
        scratch_shapes=[
            pltpu.VMEM((D_MODEL, IN_WIDTH), MXU_DTYPE), pltpu.VMEM((512, D_MODEL), MXU_DTYPE),
            pltpu.VMEM((512, D_MODEL), MXU_DTYPE), pltpu.VMEM((512, D_MODEL), MXU_DTYPE),
            pltpu.VMEM((D_MODEL, D_MODEL), MXU_DTYPE),
            pltpu.VMEM((SSM_BLOCKS, 8, SCAN_ROWS, SSM_BLOCK_STATE), F32),
            pltpu.VMEM((SSM_BLOCKS, 2, SCAN_ROWS, SSM_BLOCK_STATE), F32),
            pltpu.VMEM((POOL_HALO, POOL_WIDTH), F32),
            pltpu.VMEM((2, tm, SSM_BLOCK_STATE), F32),
            pltpu.SemaphoreType.DMA((33,)),
        ],
        compiler_params=_params("arbitrary"),
    )(x, prm, w_in_s, w_pu_s, w_glu_s, w_su_s, w_out, pool_w, mvec, srow, bb, ct)


def _mixer_backward(d2, prm, saved, w_pu_s, w_glu_s, w_su_s, w_out, pool_w, mvec, srow, bb, ct):
    z, s_re, s_im, zp, q, y_pool, yss, vg, y_ssm = saved
    T = d2.shape[0]
    tm = min(T, TM_MIX_BWD)
    nt = T // tm
    n_tiles = tm // SCAN_ROWS

    def body(d_ref, prm_ref, z_ref, sre_ref, sim_ref, zp_ref, q_ref, yp_ref, yss_ref, vg_ref, ys_ref,
             w_pu_h, w_glu_h, w_su_h, w_out_h, pw_ref, mv_ref, srow_ref, bb, ct,
             dz_ref, dwo_h, dwpu_h, dwglu_h, dwsu_h, dpw_h, dbb_h, dct_h, vsum_h, da_h,
             w_pu, w_glu, w_su, w_o, pwb, coef, carry, hist, dre, lam,
             a_wo, a_wpu, a_wglu, a_wsu, a_pw, a_bb, a_ct, a_vs, a_da, sems):
        i = pl.program_id(0)
        tile = nt - 1 - i

        @pl.when(i == 0)
        def _():
            cps = (_load_stack(w_pu_h, w_pu, sems, 0) + _load_stack(w_glu_h, w_glu, sems, 8)
                   + _load_stack(w_su_h, w_su, sems, 16))
            cps.append(pltpu.make_async_copy(w_out_h, w_o, sems.at[24]))
            cps[-1].start()
            pwb[...] = pw_ref[...].astype(MXU_DTYPE)
            for j in range(SSM_BLOCKS):
                lanes = slice(j * SSM_BLOCK_STATE, (j + 1) * SSM_BLOCK_STATE)
                for k, t in enumerate(_scan_coefficients(srow_ref[0:1, lanes], srow_ref[1:2, lanes], True)):
                    coef[j, k] = t
            for acc in (carry, hist, a_wo, a_wpu, a_wglu, a_wsu, a_pw, a_bb, a_ct, a_vs, a_da):
                acc[...] = jnp.zeros_like(acc)
            for cp in cps:
                cp.wait()

        dv = d_ref[...]
        zt = z_ref[...]
        u_ssm, gl_pool, gl_ssm = zt[:, 512:1024], zt[:, 1024:2048], zt[:, 2048:3072]
        y_p, y_s = yp_ref[...], ys_ref[...]
        sgp, sgs = _sigmoid(gl_pool), _sigmoid(gl_ssm)
        dmo = (_row(prm_ref, 5) * dv).astype(MXU_DTYPE)
        a_wo[...] += _mm_tn(sgp * y_p + sgs * y_s, dmo)
        dmerged = _mm_nt(dmo, w_o[...])
        dy_pool = dmerged * sgp
        dgl_pool = dmerged * y_p * (sgp * (1.0 - sgp))
        dy_ssm = dmerged * sgs
        dgl_ssm = dmerged * y_s * (sgs * (1.0 - sgs))

        scale = mv_ref[ROW_POOL_SCALE:ROW_POOL_SCALE + 1, 0:512]
        qv, zpv = q_ref[...], zp_ref[...]
        a_wpu[...] += _mm_tn(qv * scale, dy_pool)
        dp = _mm_nt(dy_pool, w_pu[...])
        dq = dp * scale
        a_vs[0:1, 0:512] += _colsum(dp * qv)
        a_vs[1:2, 0:512] += _colsum(dq)
        dzp_blocks = []
        for k in range(4):
            lanes = slice(k * 128, (k + 1) * 128)
            dzp_blocks.append(_mm_nt(dq[:, lanes], pwb[k]))
            a_pw[k] += _mm_tn(zpv[:, lanes], dq[:, lanes])
        dzp = jnp.concatenate(dzp_blocks, axis=1)
        t1 = (lax.broadcasted_iota(jnp.int32, (tm, 1), 0) + (tile * tm + 1)).astype(F32)
        gs = dzp / jnp.minimum(t1, _window_lanes())
        n_ext = tm + POOL_HALO
        ext = jnp.concatenate([gs, hist[...]], axis=0)
        v2 = ext + pltpu.roll(ext, n_ext - 1, 0)
        v4 = v2[:, 128:] + pltpu.roll(v2[:, 128:], n_ext - 2, 0)
        v8 = v4[:, 128:] + pltpu.roll(v4[:, 128:], n_ext - 4, 0)
        v16 = v8[:, 128:] + pltpu.roll(v8[:, 128:], n_ext - 8, 0)
        msum = jnp.concatenate([v2[:tm, :128], v4[:tm, :128], v8[:tm, :128], v16[:tm]], axis=1)
        hist[...] = gs[0:POOL_HALO, :]
        du_pool = msum - dzp

        vgv = vg_ref[...]
        val, gate = vgv[:, 0:512], vgv[:, 512:1024]
        sgg = _sigmoid(gate)
        a_wsu[...] += _mm_tn(val * sgg, dy_ssm)
        do = _mm_nt(dy_ssm, w_su[...])
        dvg = jnp.concatenate([do * sgg, do * val * (sgg * (1.0 - sgg))], axis=1)
        a_vs[3:4, :] += _colsum(dvg)
        yv = yss_ref[...]
        a_wglu[...] += _mm_tn(_gelu(yv), dvg)
        dyss = _mm_nt(dvg, w_glu[...]) * _gelu_grad(yv)
        a_vs[2:3, 0:512] += _colsum(dyss * u_ssm)
        du_blocks = []
        for j in range(SSM_BLOCKS):
            lanes = pl.ds(j * SSM_BLOCK_STATE, SSM_BLOCK_STATE)
            in_lanes = slice(j * 128, (j + 1) * 128)
            dyb = dyss[:, in_lanes].astype(MXU_DTYPE)
            ub = u_ssm[:, in_lanes].astype(MXU_DTYPE)
            dre[0] = _mm_nt(dyb, ct[0, j])
            dre[1] = -_mm_nt(dyb, ct[1, j])
            a_ct[0, j] += _mm_tn(sre_ref[:, lanes], dyb)
            a_ct[1, j] -= _mm_tn(sim_ref[:, lanes], dyb)
            a1r, a1i, a2r, a2i, a4r, a4i, pr, pi = [coef[j, k] for k in range(8)]
            rowi = lax.broadcasted_iota(jnp.int32, (SCAN_ROWS, SSM_BLOCK_STATE), 0)

            def step(tt, c, lanes=lanes, a1r=a1r, a1i=a1i, a2r=a2r, a2i=a2i, a4r=a4r, a4i=a4i, pr=pr, pi=pi, rowi=rowi):
                cr, ci, acc_r, acc_i = c
                rows = pl.ds(pl.multiple_of((n_tiles - 1 - tt) * SCAN_ROWS, SCAN_ROWS), SCAN_ROWS)
                xr, xi = dre[0, rows, :], dre[1, rows, :]
                for dstep, kr, ki in ((1, a1r, a1i), (2, a2r, a2i), (4, a4r, a4i)):
                    sr, si = pltpu.roll(xr, SCAN_ROWS - dstep, 0), pltpu.roll(xi, SCAN_ROWS - dstep, 0)
                    xr, xi = xr + kr * sr + ki * si, xi + kr * si - ki * sr
                xr, xi = xr + pr * cr + pi * ci, xi + pr * ci - pi * cr
                lam[0, rows, :] = xr
                lam[1, rows, :] = xi
                nr = jnp.where(rowi == SCAN_ROWS - 1, cr, pltpu.roll(xr, SCAN_ROWS - 1, 0))
                ni = jnp.where(rowi == SCAN_ROWS - 1, ci, pltpu.roll(xi, SCAN_ROWS - 1, 0))
                s_r, s_i = sre_ref[rows, lanes], sim_ref[rows, lanes]
                acc_r = acc_r + nr * s_r + ni * s_i
                acc_i = acc_i + ni * s_r - nr * s_i
                return (jnp.broadcast_to(xr[0:1, :], xr.shape), jnp.broadcast_to(xi[0:1, :], xi.shape), acc_r, acc_i)

            cr, ci, acc_r, acc_i = lax.fori_loop(0, n_tiles, step, (carry[j, 0], carry[j, 1], a_da[0, j], a_da[1, j]))
            carry[j, 0] = cr
            carry[j, 1] = ci
            a_da[0, j] = acc_r
            a_da[1, j] = acc_i
            lr_b, li_b = lam[0].astype(MXU_DTYPE), lam[1].astype(MXU_DTYPE)
            a_bb[0, j] += _mm_tn(ub, lr_b)
            a_bb[1, j] += _mm_tn(ub, li_b)
            du_blocks.append(_mm_nt(lr_b, bb[0, j]) + _mm_nt(li_b, bb[1, j]))
        du_ssm = jnp.concatenate(du_blocks, axis=1) + dyss * mv_ref[ROW_SSM_D:ROW_SSM_D + 1, 0:512]
        dz_ref[...] = jnp.concatenate([du_pool, du_ssm, dgl_pool, dgl_ssm], axis=1)

        @pl.when(i == nt - 1)
        def _():
            outs = ((a_wo, dwo_h), (a_wpu, dwpu_h), (a_wglu, dwglu_h), (a_wsu, dwsu_h), (a_pw, dpw_h),
                    (a_bb, dbb_h), (a_ct, dct_h), (a_vs, vsum_h), (a_da, da_h))
            cps = [pltpu.make_async_copy(src, dst, sems.at[k]) for k, (src, dst) in enumerate(outs)]
            for cp in cps:
                cp.start()
            for cp in cps:
                cp.wait()

    def tok(width):
        return pl.BlockSpec((tm, width), lambda i: (nt - 1 - i, 0))

    hbm = pl.BlockSpec(memory_space=pl.ANY)
    acc_shapes = [(D_MODEL, D_MODEL), (512, D_MODEL), (512, D_MODEL), (512, D_MODEL), (4, 128, 128),
                  (2, SSM_BLOCKS, 128, SSM_BLOCK_STATE), (2, SSM_BLOCKS, SSM_BLOCK_STATE, 128), (8, D_MODEL),
                  (2, SSM_BLOCKS, SCAN_ROWS, SSM_BLOCK_STATE)]
    return pl.pallas_call(
        body, name="mixer_backward", grid=(nt,),
        out_shape=[jax.ShapeDtypeStruct((T, IN_WIDTH), F32)] + [jax.ShapeDtypeStruct(s, F32) for s in acc_shapes],
        in_specs=[tok(D_MODEL), _resident(prm), tok(IN_WIDTH), tok(N_STATE), tok(N_STATE), tok(512), tok(512),
                  tok(D_MODEL), tok(512), tok(D_MODEL), tok(D_MODEL), hbm, hbm, hbm, hbm, _resident(pool_w),
                  _resident(mvec), _resident(srow), _resident(bb), _resident(ct)],
        out_specs=[tok(IN_WIDTH)] + [hbm] * len(acc_shapes),
        scratch_shapes=[
            pltpu.VMEM((512, D_MODEL), MXU_DTYPE), pltpu.VMEM((512, D_MODEL), MXU_DTYPE),
            pltpu.VMEM((512, D_MODEL), MXU_DTYPE), pltpu.VMEM((D_MODEL, D_MODEL), MXU_DTYPE),
            pltpu.VMEM((4, 128, 128), MXU_DTYPE),
            pltpu.VMEM((SSM_BLOCKS, 8, SCAN_ROWS, SSM_BLOCK_STATE), F32),
            pltpu.VMEM((SSM_BLOCKS, 2, SCAN_ROWS, SSM_BLOCK_STATE), F32),
            pltpu.VMEM((POOL_HALO, POOL_WIDTH), F32),
            pltpu.VMEM((2, tm, SSM_BLOCK_STATE), F32), pltpu.VMEM((2, tm, SSM_BLOCK_STATE), F32),
        ] + [pltpu.VMEM(s, F32) for s in acc_shapes] + [pltpu.SemaphoreType.DMA((25,))],
        compiler_params=_params("arbitrary"),
    )(d2, prm, z, s_re, s_im, zp, q, y_pool, yss, vg, y_ssm, w_pu_s, w_glu_s, w_su_s, w_out, pool_w, mvec, srow,
      bb, ct)


def _mixer_in_backward(x, dz, prm, w_in_s):
    T = x.shape[0]
    tm = min(T, TM_MIX)
    nt = T // tm
    cols = IN_WIDTH // NDEV

    def body(x_ref, dz_ref, prm_ref, w_in_h, dh_ref, dw_ref, w_in, acc, sems):
        i = pl.program_id(0)

        @pl.when(i == 0)
        def _():
            cps = _load_stack(w_in_h, w_in, sems, 0)
            acc[...] = jnp.zeros_like(acc)
            for cp in cps:
                cp.wait()

        h, _, _ = _modulated(x_ref[...], prm_ref, 1, ROW_G_MIX)
        dzb = dz_ref[...].astype(MXU_DTYPE)
        dh_ref[0] = _mm_nt(dzb, w_in[...])
        acc[...] += _mm_tn(h, dzb)

        @pl.when(i == nt - 1)
        def _():
            for k in range(NDEV):
                dw_ref[k] = acc[:, k * cols:(k + 1) * cols].astype(WIRE_DTYPE)

    return pl.pallas_call(
        body, name="mixer_in_backward", grid=(nt,),
        out_shape=[jax.ShapeDtypeStruct((1, T, D_MODEL), F32), jax.ShapeDtypeStruct((NDEV, D_MODEL, cols), WIRE_DTYPE)],
        in_specs=[pl.BlockSpec((tm, D_MODEL), lambda i: (i, 0)), pl.BlockSpec((tm, IN_WIDTH), lambda i: (i, 0)),
                  pl.BlockSpec(prm.shape, lambda i: (0, 0)), pl.BlockSpec(memory_space=pl.ANY)],
        out_specs=[pl.BlockSpec((1, tm, D_MODEL), lambda i: (0, i, 0)),
                   pl.BlockSpec((NDEV, D_MODEL, cols), lambda i: (0, 0, 0))],
        scratch_shapes=[pltpu.VMEM((D_MODEL, IN_WIDTH), MXU_DTYPE), pltpu.VMEM((D_MODEL, IN_WIDTH), F32),
                        pltpu.SemaphoreType.DMA((8,))],
        compiler_params=_params("arbitrary"),
    )(x, dz, prm, w_in_s)


def _ssm_dense_backward(dbb, da, srow, b_dense):
    def body(dbb_ref, da_ref, srow_ref, bd_ref, db_ref, df_ref):
        df_re, df_im = [], []
        da_re = [_colsum(da_ref[0, j]) for j in range(SSM_BLOCKS)]
        da_im = [_colsum(da_ref[1, j]) for j in range(SSM_BLOCKS)]
        for j in range(SSM_BLOCKS):
            lanes = slice(j * SSM_BLOCK_STATE, (j + 1) * SSM_BLOCK_STATE)
            f_re, f_im = srow_ref[2:3, lanes], srow_ref[3:4, lanes]
            g_re, g_im = dbb_ref[0, j], dbb_ref[1, j]
            b_re, b_im = bd_ref[0, j], bd_ref[1, j]
            db_ref[0, j] = f_re * g_re + f_im * g_im
            db_ref[1, j] = f_re * g_im - f_im * g_re
            df_re.append(_colsum(g_re * b_re + g_im * b_im))
            df_im.append(_colsum(g_im * b_re - g_re * b_im))
        df_ref[...] = jnp.concatenate([jnp.concatenate(df_re, axis=1), jnp.concatenate(df_im, axis=1),
                                       jnp.concatenate(da_re, axis=1), jnp.concatenate(da_im, axis=1),
                                       jnp.zeros((4, N_STATE), F32)], axis=0)

    return pl.pallas_call(body, name="ssm_dense_backward",
                          out_shape=[jax.ShapeDtypeStruct(b_dense.shape, F32), jax.ShapeDtypeStruct((8, N_STATE), F32)],
                          compiler_params=pltpu.CompilerParams(vmem_limit_bytes=VMEM_LIMIT))(dbb, da, srow, b_dense)


def _adamw_update(w, g, m, v):
    m = ADAM_B1 * m + (1.0 - ADAM_B1) * g
    v = ADAM_B2 * v + (1.0 - ADAM_B2) * (g * g)
    m_hat = m / (1.0 - ADAM_B1 ** ADAM_STEP)
    v_hat = v / (1.0 - ADAM_B2 ** ADAM_STEP)
    delta = -ADAM_LR * (m_hat / (jnp.sqrt(v_hat) + ADAM_EPS) + ADAM_WD * w)
    return delta, m, v


def _adam_rows(shape):
    rows, cols = shape
    tr = rows
    while tr * cols * 4 > (1 << 20) and tr % 16 == 0:
        tr //= 2
    return tr


def _adam_sharded(w, m, v, land, name):
    R, C = w.shape
    tr = _adam_rows((R, C))

    def body(w_ref, m_ref, v_ref, land_ref, g_ref, d_ref, mo_ref, vo_ref):
        g = land_ref[0].astype(F32)
        for b in range(1, NDEV):
            g = g + land_ref[b].astype(F32)
        g_ref[...] = g
        d_ref[...], mo_ref[...], vo_ref[...] = _adamw_update(w_ref[...], g, m_ref[...], v_ref[...])

    blk = pl.BlockSpec((tr, C), lambda i: (i, 0))
    return pl.pallas_call(
        body, name=name, grid=(R // tr,),
        out_shape=[jax.ShapeDtypeStruct((R, C), F32)] * 4,
        in_specs=[blk, blk, blk, pl.BlockSpec((NDEV, tr, C), lambda i: (0, i, 0))],
        out_specs=[blk] * 4,
        compiler_params=_params("arbitrary"),
    )(w, m, v, land)


def _adam_ada(w, m, v, sc_all, dmod_cols):
    R, C = w.shape
    tr = 256

    def body(w_ref, m_ref, v_ref, sc_ref, dm_ref, g_ref, d_ref, mo_ref, vo_ref):
        g = _mm_tn(sc_ref[...], dm_ref[...])
        g_ref[...] = g
        d_ref[...], mo_ref[...], vo_ref[...] = _adamw_update(w_ref[...], g, m_ref[...], v_ref[...])

    blk = pl.BlockSpec((tr, C), lambda i: (i, 0))
    return pl.pallas_call(
        body, name="adam_w_ada", grid=(R // tr,),
        out_shape=[jax.ShapeDtypeStruct((R, C), F32)] * 4,
        in_specs=[blk, blk, blk, pl.BlockSpec((8, tr), lambda i: (0, i)), pl.BlockSpec((8, C), lambda i: (0, 0))],
        out_specs=[blk] * 4,
        compiler_params=_params("arbitrary"),
    )(w, m, v, sc_all, dmod_cols)


def _adam_small(w, g, m, v):
    def body(w_ref, g_ref, m_ref, v_ref, d_ref, mo_ref, vo_ref):
        d_ref[...], mo_ref[...], vo_ref[...] = _adamw_update(w_ref[...], g_ref[...], m_ref[...], v_ref[...])

    return pl.pallas_call(body, name="adam_small", out_shape=[jax.ShapeDtypeStruct(w.shape, F32)] * 3,
                          compiler_params=pltpu.CompilerParams(vmem_limit_bytes=VMEM_LIMIT))(w, g, m, v)


def _block_diag_in(b):
    bt = jnp.transpose(b, (0, 2, 1)).reshape(SSM_BLOCKS, 8, SSM_GROUP, SSM_STATE)
    eye = jnp.eye(8, dtype=bool)[None, :, None, :, None]
    return jnp.where(eye, bt[:, :, :, None, :], 0.0).reshape(SSM_BLOCKS, 128, SSM_BLOCK_STATE)


def _block_diag_out(c):
    ct = jnp.transpose(c, (0, 2, 1)).reshape(SSM_BLOCKS, 8, SSM_STATE, SSM_GROUP)
    eye = jnp.eye(8, dtype=bool)[None, :, None, :, None]
    return jnp.where(eye, ct[:, :, :, None, :], 0.0).reshape(SSM_BLOCKS, SSM_BLOCK_STATE, 128)


def _diag_blocks(dense, rows, cols):
    d5 = dense.reshape(SSM_BLOCKS, 8, rows, 8, cols)
    return jnp.stack([d5[:, a, :, a, :] for a in range(8)], axis=1).reshape(32, rows, cols)


def _pack_small(ada_vec, parts):
    rest = jnp.concatenate([parts[n].reshape(-1) for n, _ in SMALL_PARAMS])
    rest = jnp.pad(rest, (0, NDEV * REST_ROWS * 128 - SMALL_TOTAL)).reshape(NDEV, REST_ROWS, 128)
    return jnp.concatenate([ada_vec.reshape(NDEV, ADA_ROWS, 128), rest,
                            jnp.zeros((NDEV, PACK_ROWS - ADA_ROWS - REST_ROWS, 128), F32)], axis=1)


def _unpack_small(pack, shapes):
    ada_vec = pack[:, :ADA_ROWS].reshape(-1)
    rest = pack[:, ADA_ROWS:ADA_ROWS + REST_ROWS].reshape(-1)
    out, off = {}, 0
    for n, size in SMALL_PARAMS:
        out[n] = rest[off:off + size].reshape(shapes[n])
        off += size
    return ada_vec, out


WEIGHT_ORDER = ('w_ada', 'b_ada', 'g_ffn1', 'w_ffn1_in', 'w_ffn1_out', 'g_mix', 'w_in', 'pool_w', 'pool_b',
                'pool_scale', 'w_pool_up', 'ssm_lam_re_log', 'ssm_lam_im', 'ssm_log_dt', 'ssm_b_re', 'ssm_b_im',
                'ssm_c_re', 'ssm_c_im', 'ssm_d', 'w_glu', 'b_glu', 'w_ssm_up', 'w_out', 'g_ffn2', 'w_ffn2_in',
                'w_ffn2_out', 'g_final')
GATHERED = ('w_ffn1_in', 'w_ffn1_out', 'w_in', 'w_pool_up', 'w_glu', 'w_ssm_up', 'w_out', 'w_ffn2_in', 'w_ffn2_out')


def kernel(x, c, w_ada, b_ada, g_ffn1, w_ffn1_in, w_ffn1_out, g_mix, w_in, pool_w, pool_b, pool_scale, w_pool_up, ssm_lam_re_log, ssm_lam_im, ssm_log_dt, ssm_b_re, ssm_b_im, ssm_c_re, ssm_c_im, ssm_d, w_glu, b_glu, w_ssm_up, w_out, g_ffn2, w_ffn2_in, w_ffn2_out, g_final, loss_target, m_w_ada, m_b_ada, m_g_ffn1, m_w_ffn1_in, m_w_ffn1_out, m_g_mix, m_w_in, m_pool_w, m_pool_b, m_pool_scale, m_w_pool_up, m_ssm_lam_re_log, m_ssm_lam_im, m_ssm_log_dt, m_ssm_b_re, m_ssm_b_im, m_ssm_c_re, m_ssm_c_im, m_ssm_d, m_w_glu, m_b_glu, m_w_ssm_up, m_w_out, m_g_ffn2, m_w_ffn2_in, m_w_ffn2_out, m_g_final, v_w_ada, v_b_ada, v_g_ffn1, v_w_ffn1_in, v_w_ffn1_out, v_g_mix, v_w_in, v_pool_w, v_pool_b, v_pool_scale, v_w_pool_up, v_ssm_lam_re_log, v_ssm_lam_im, v_ssm_log_dt, v_ssm_b_re, v_ssm_b_im, v_ssm_c_re, v_ssm_c_im, v_ssm_d, v_w_glu, v_b_glu, v_w_ssm_up, v_w_out, v_g_ffn2, v_w_ffn2_in, v_w_ffn2_out, v_g_final):
    args = locals()
    W = {n: args[n] for n in WEIGHT_ORDER}
    M = {n: args["m_" + n] for n in WEIGHT_ORDER}
    V = {n: args["v_" + n] for n in WEIGHT_ORDER}
    shapes = {n: W[n].shape for n in WEIGHT_ORDER}
    xt, tgt = x[0], loss_target[0]

    stacks = dict(zip(GATHERED, _allgather_weights([W[n][0] for n in GATHERED])))
    win1 = stacks['w_ffn1_in'].reshape(2, 4, D_MODEL, FF_SHARD)
    win2 = stacks['w_ffn2_in'].reshape(2, 4, D_MODEL, FF_SHARD)
    wout1 = stacks['w_ffn1_out'].reshape(4, FF_SHARD, D_MODEL)
    wout2 = stacks['w_ffn2_out'].reshape(4, FF_SHARD, D_MODEL)
    w_out_full = stacks['w_out'].reshape(D_MODEL, D_MODEL)

    mod_cols, sc_all = _ada_forward(c, W['w_ada'][0], b_ada.reshape(NDEV, -1))
    prm = jnp.concatenate([mod_cols.reshape(9, D_MODEL), g_ffn1, g_mix, g_ffn2, g_final[None], jnp.zeros((3, D_MODEL), F32)], axis=0)
    pad512 = jnp.zeros((1, D_MODEL - 512), F32)
    mvec = jnp.concatenate([jnp.concatenate([pool_b, pad512], axis=1), jnp.concatenate([pool_scale, pad512], axis=1),
                            jnp.concatenate([ssm_d, pad512], axis=1), b_glu, jnp.zeros((4, D_MODEL), F32)], axis=0)
    log_dt_col = ssm_log_dt[0][:, None]
    coeffs = _ssm_params_forward(ssm_lam_re_log[0], ssm_lam_im[0], log_dt_col)
    srow = jnp.stack([t.reshape(N_STATE) for t in coeffs], axis=0)
    b_dense = jnp.stack([_block_diag_in(ssm_b_re[0]), _block_diag_in(ssm_b_im[0])], axis=0)
    c_dense = jnp.stack([_block_diag_out(ssm_c_re[0]), _block_diag_out(ssm_c_im[0])], axis=0)
    bb, ct = _ssm_dense_forward(srow, b_dense, c_dense)
    pw = pool_w[0]

    x1, f1 = _ffn_forward(xt, prm, win1, wout1, 0, ROW_G_FFN1, "ffn1_forward")
    x2, mo, *saved = _mixer_forward(x1, prm, stacks['w_in'], stacks['w_pool_up'], stacks['w_glu'], stacks['w_ssm_up'],
                                    w_out_full, pw, mvec, srow, bb, ct)
    x3, f3 = _ffn_forward(x2, prm, win2, wout2, 2, ROW_G_FFN2, "ffn2_forward")
    d3, fin = _final_loss(x3, tgt, prm)
    loss = lax.psum(fin[1, 0], ("x", "y", "c"))

    parts3, dwin2, dwout2 = _ffn_backward(x2, d3, prm, win2, wout2, 2, ROW_G_FFN2, "ffn2_backward")
    d2, sums3 = _norm_backward(parts3, x2, d3, f3, prm, 2, ROW_G_FFN2, 0.5, "ffn2_norm_backward")
    dz, dwo, dwpu, dwglu, dwsu, dpw, dbb, dct, vsum, da = _mixer_backward(
        d2, prm, saved, stacks['w_pool_up'], stacks['w_glu'], stacks['w_ssm_up'], w_out_full, pw, mvec, srow, bb, ct)
    parts2, dwin_mix = _mixer_in_backward(x1, dz, prm, stacks['w_in'])
    d1, sums2 = _norm_backward(parts2, x1, d2, mo, prm, 1, ROW_G_MIX, 1.0, "mixer_norm_backward")
    parts1, dwin1, dwout1 = _ffn_backward(xt, d1, prm, win1, wout1, 0, ROW_G_FFN1, "ffn1_backward")
    d0, sums1 = _norm_backward(parts1, xt, d1, f1, prm, 0, ROW_G_FFN1, 0.5, "ffn1_norm_backward")

    db_dense, df_rows = _ssm_dense_backward(dbb, da, srow, b_dense)
    cot = [df_rows[r].reshape(32, 64) for r in (2, 3, 0, 1)]
    d_lrl, d_li, d_ldt = _ssm_params_backward(ssm_lam_re_log[0], ssm_lam_im[0], log_dt_col, cot)
    small_grads = {
        'g_ffn1': sums1[0], 'g_mix': sums2[0], 'g_ffn2': sums3[0], 'g_final': fin[0], 'pool_w': dpw,
        'pool_b': vsum[1, :512], 'pool_scale': vsum[0, :512], 'ssm_lam_re_log': d_lrl, 'ssm_lam_im': d_li,
        'ssm_log_dt': d_ldt, 'ssm_b_re': jnp.transpose(_diag_blocks(db_dense[0], SSM_GROUP, SSM_STATE), (0, 2, 1)),
        'ssm_b_im': jnp.transpose(_diag_blocks(db_dense[1], SSM_GROUP, SSM_STATE), (0, 2, 1)),
        'ssm_c_re': jnp.transpose(_diag_blocks(dct[0], SSM_STATE, SSM_GROUP), (0, 2, 1)),
        'ssm_c_im': jnp.transpose(_diag_blocks(dct[1], SSM_STATE, SSM_GROUP), (0, 2, 1)),
        'ssm_d': vsum[2, :512], 'b_glu': vsum[3],
    }
    dmod = jnp.concatenate([sums1[1:4], sums2[1:4], sums3[1:4]], axis=0).reshape(-1)
    total, landed = _allreduce_small(_pack_small(dmod, small_grads))
    dmod_cols = landed[:, :ADA_ROWS].reshape(NDEV, ADA_ROWS * 128)

    def stacked(ref, p):
        return ref.at[p]

    def halves(ref, p):
        return ref.at[p // 4, p % 4]

    wire = lambda a: a.astype(WIRE_DTYPE)
    big = {
        'w_ffn1_in': (dwin1, halves), 'w_ffn1_out': (dwout1.reshape(NDEV, -1, D_MODEL), stacked),
        'w_in': (dwin_mix, stacked),
        'w_pool_up': (wire(dwpu.reshape(512, NDEV, 128).transpose(1, 0, 2)), stacked),
        'w_glu': (wire(dwglu.reshape(512, NDEV, 128).transpose(1, 0, 2)), stacked),
        'w_ssm_up': (wire(dwsu.reshape(512, NDEV, 128).transpose(1, 0, 2)), stacked),
        'w_out': (wire(dwo).reshape(NDEV, -1, D_MODEL), stacked),
        'w_ffn2_in': (dwin2, halves), 'w_ffn2_out': (dwout2.reshape(NDEV, -1, D_MODEL), stacked),
    }
    lands = dict(zip(GATHERED, _scatter_grads([big[n][0] for n in GATHERED], [big[n][1] for n in GATHERED],
                                              [W[n].shape[1:] for n in GATHERED])))

    grad, delta, new_m, new_v = {}, {}, {}, {}
    for n in GATHERED:
        res = _adam_sharded(W[n][0], M[n][0], V[n][0], lands[n], "adam_" + n)
        grad[n], delta[n], new_m[n], new_v[n] = [r[None] for r in res]
    res = _adam_ada(W['w_ada'][0], M['w_ada'][0], V['w_ada'][0], sc_all, dmod_cols)
    grad['w_ada'], delta['w_ada'], new_m['w_ada'], new_v['w_ada'] = [r[None] for r in res]

    flat = lambda t: t.reshape(NDEV * PACK_ROWS, 128)
    small_w = flat(_pack_small(b_ada.reshape(-1), W))
    small_m = flat(_pack_small(m_b_ada.reshape(-1), M))
    small_v = flat(_pack_small(v_b_ada.reshape(-1), V))
    res = _adam_small(small_w, flat(total), small_m, small_v)
    for dst, packed in zip((grad, delta, new_m, new_v), (total, *res)):
        ada_vec, rest = _unpack_small(packed.reshape(NDEV, PACK_ROWS, 128), shapes)
        dst.update(rest)
        dst['b_ada'] = ada_vec.reshape(shapes['b_ada'])

    return (loss, d0[None], *[grad[n] for n in WEIGHT_ORDER], *[delta[n] for n in WEIGHT_ORDER],
            *[new_m[n] for n in WEIGHT_ORDER], *[new_v[n] for n in WEIGHT_ORDER])
```

```python
import functools

import jax
import jax.numpy as jnp
from jax import lax
from jax.experimental import pallas as pl
from jax.experimental.pallas import tpu as pltpu

F32 = jnp.float32
MXU_DTYPE = jnp.bfloat16
WIRE_DTYPE = jnp.bfloat16

NDEV = 8
D_MODEL = 1024
D_FF = 2816
FF_SHARD = 2 * D_FF // NDEV
POOL_WIDTH = 512
POOL_GROUP = 128
SSM_WIDTH = 512
SSM_STATE = 64
SSM_GROUP = 16
SSM_BLOCKS = 4
SSM_BLOCK_STATE = 512
N_STATE = 2048
IN_WIDTH = 3072
EPS = 1e-6
ADAM_LR = 0.001
ADAM_B1 = 0.9
ADAM_B2 = 0.999
ADAM_EPS = 1e-08
ADAM_WD = 0.01
ADAM_STEP = 10

TM_FFN = 512
TM_MIX = 256
TM_MIX_BWD = 128
TM_EW = 512
SCAN_ROWS = 8
POOL_HALO = 16
VMEM_LIMIT = 60 * 1024 * 1024

ROW_G_FFN1, ROW_G_MIX, ROW_G_FFN2, ROW_G_FINAL = 9, 10, 11, 12
ROW_POOL_B, ROW_POOL_SCALE, ROW_SSM_D, ROW_B_GLU = 0, 1, 2, 3

SMALL_PARAMS = (
    ("g_ffn1", 1024), ("g_mix", 1024), ("g_ffn2", 1024), ("g_final", 1024), ("pool_w", 65536),
    ("pool_b", 512), ("pool_scale", 512), ("ssm_lam_re_log", 2048), ("ssm_lam_im", 2048),
    ("ssm_log_dt", 32), ("ssm_b_re", 32768), ("ssm_b_im", 32768), ("ssm_c_re", 32768),
    ("ssm_c_im", 32768), ("ssm_d", 512), ("b_glu", 1024),
)
SMALL_TOTAL = sum(n for _, n in SMALL_PARAMS)
ADA_ROWS = 9
REST_ROWS = 203
PACK_ROWS = 216
MESH = pl.DeviceIdType.MESH


def _mm(a, b):
    return jnp.dot(a.astype(MXU_DTYPE), b.astype(MXU_DTYPE), preferred_element_type=F32)


def _mm_nt(a, b):
    return lax.dot_general(a.astype(MXU_DTYPE), b.astype(MXU_DTYPE), (((1,), (1,)), ((), ())),
                           preferred_element_type=F32)


def _mm_tn(a, b):
    return lax.dot_general(a.astype(MXU_DTYPE), b.astype(MXU_DTYPE), (((0,), (0,)), ((), ())),
                           preferred_element_type=F32)


def _rms_scale(x):
    return lax.rsqrt(jnp.mean(x * x, axis=-1, keepdims=True) + EPS)


def _sigmoid(x):
    return jax.nn.sigmoid(x)


def _colsum(x):
    return jnp.sum(x, axis=0, keepdims=True)


def _row(ref, r):
    return ref[r:r + 1, :]


def _params(*sem):
    return pltpu.CompilerParams(dimension_semantics=sem, vmem_limit_bytes=VMEM_LIMIT)


def _resident(a):
    return pl.BlockSpec(a.shape, lambda *_: (0,) * a.ndim, pipeline_mode=pl.Buffered(1))


def _me():
    return lax.axis_index("x"), lax.axis_index("y"), lax.axis_index("c")


def _peer(rel):
    x, y, c = _me()
    px = 1 - x if rel & 4 else x
    py = 1 - y if rel & 2 else y
    pc = 1 - c if rel & 1 else c
    return (px, py, pc), 4 * px + 2 * py + pc


_HBM = pl.BlockSpec(memory_space=pl.ANY)


def _stacked(ref, p):
    return ref.at[p]


def _halves(ref, p):
    return ref.at[p // 4, p % 4]


class _Gather:
    def __init__(self, shards):
        self.operands = list(shards)
        self.n = len(shards)
        self.out_shape = [jax.ShapeDtypeStruct((NDEV,) + s.shape, s.dtype) for s in shards]
        self.scratch = [pltpu.SemaphoreType.DMA((7 * self.n,)), pltpu.SemaphoreType.DMA((7 * self.n,)),
                        pltpu.SemaphoreType.DMA((self.n,))]

    def plan(self, srcs, outs, sems):
        send_sems, recv_sems, local_sems = sems
        n = self.n
        x, y, c = _me()
        me = 4 * x + 2 * y + c
        here, sibling = (x, y, c), (x, y, 1 - c)
        chips = [(1 - x, y), (x, 1 - y), (1 - x, 1 - y)]

        def blk(px, py, pc):
            return 4 * px + 2 * py + pc

        def copy(a, k, block, to, src=None):
            return pltpu.make_async_remote_copy(
                src_ref=outs[a].at[block] if src is None else src, dst_ref=outs[a].at[block],
                send_sem=send_sems.at[7 * a + k], recv_sem=recv_sems.at[7 * a + k], device_id=to, device_id_type=MESH)

        def mine(a):
            return pltpu.make_async_copy(srcs[a], outs[a].at[me], local_sems.at[a])

        def first(a):
            return [copy(a, 0, me, sibling, src=srcs[a])] + [copy(a, 1 + j, me, (*chip, c), src=srcs[a])
                                                              for j, chip in enumerate(chips)]

        def start():
            for a in range(n):
                mine(a).start()
                for cp in first(a):
                    cp.start()

        def forward():
            for a in range(n):
                for j, chip in enumerate(chips):
                    copy(a, 1 + j, blk(*chip, c), here).wait_recv()
                    copy(a, 4 + j, blk(*chip, c), sibling).start()

        def finish():
            for a in range(n):
                copy(a, 0, blk(x, y, 1 - c), here).wait_recv()
                for j, chip in enumerate(chips):
                    copy(a, 4 + j, blk(*chip, 1 - c), here).wait_recv()
            for a in range(n):
                mine(a).wait()
                for cp in first(a):
                    cp.wait_send()
                for j, chip in enumerate(chips):
                    copy(a, 4 + j, blk(*chip, c), sibling).wait_send()

        return start, forward, finish


class _Scatter:
    def __init__(self, arrays, views, shard_shapes):
        self.operands = list(arrays)
        self.views = list(views)
        self.n = len(arrays)
        self.out_shape = [jax.ShapeDtypeStruct((NDEV,) + tuple(s), a.dtype) for s, a in zip(shard_shapes, arrays)]
        self.scratch = [pltpu.SemaphoreType.DMA((7 * self.n,)), pltpu.SemaphoreType.DMA((7 * self.n,)),
                        pltpu.SemaphoreType.DMA((self.n,))]

    def plan(self, srcs, outs, sems):
        send_sems, recv_sems, local_sems = sems
        n, views = self.n, self.views
        x, y, c = _me()
        me = 4 * x + 2 * y + c

        def mine(a):
            return pltpu.make_async_copy(views[a](srcs[a], me), outs[a].at[me], local_sems.at[a])

        def copy(a, rel, sending):
            to, p = _peer(rel)
            return pltpu.make_async_remote_copy(
                src_ref=views[a](srcs[a], p), dst_ref=outs[a].at[me if sending else p],
                send_sem=send_sems.at[7 * a + rel - 1], recv_sem=recv_sems.at[7 * a + rel - 1],
                device_id=to if sending else (x, y, c), device_id_type=MESH)

        def start():
            for a in range(n):
                mine(a).start()
            for rel in range(1, 8):
                for a in range(n):
                    copy(a, rel, True).start()

        def forward():
            pass

        def finish():
            for rel in range(1, 8):
                for a in range(n):
                    copy(a, rel, False).wait_recv()
            for rel in range(1, 8):
                for a in range(n):
                    copy(a, rel, True).wait_send()
            for a in range(n):
                mine(a).wait()

        return start, forward, finish


def _launch(body, name, out_shape, in_specs, out_specs, operands, scratch=(), grid=None, semantics=None,
            carry=None, steps=None):
    out_shape, in_specs, out_specs = list(out_shape), list(in_specs), list(out_specs)
    operands, scratch = list(operands), list(scratch)
    n_in, n_out, n_scr = len(in_specs), len(out_shape), len(scratch)
    kernel_body = body
    if carry is not None:
        k = carry.n

        def kernel_body(*refs):
            ins, cin = refs[:n_in], refs[n_in:n_in + k]
            outs, cout = refs[n_in + k:n_in + k + n_out], refs[n_in + k + n_out:n_in + 2 * k + n_out]
            rest = refs[n_in + 2 * k + n_out:]
            scr, csem = rest[:n_scr], rest[n_scr:]
            start, forward, finish = carry.plan(cin, cout, csem)
            if steps is None:
                start()
                body(*ins, *outs, *scr)
                forward()
                finish()
            else:
                pl.when(steps()[0])(start)
                body(*ins, *outs, *scr)
                pl.when(steps()[1])(forward)
                pl.when(steps()[2])(finish)

        in_specs += [_HBM] * k
        out_shape += carry.out_shape
        out_specs += [_HBM] * k
        operands += carry.operands
        scratch += carry.scratch
    kwargs = {} if grid is None else {"grid": grid}
    params = pltpu.CompilerParams(vmem_limit_bytes=VMEM_LIMIT) if semantics is None else _params(*semantics)
    return pl.pallas_call(kernel_body, name=name, out_shape=out_shape, in_specs=in_specs, out_specs=out_specs,
                          scratch_shapes=scratch, compiler_params=params, **kwargs)(*operands)


def _grid_steps(nt):
    def steps():
        i = pl.program_id(0)
        return i == 0, i == nt // 2, i == nt - 1
    return steps


def _cast_shards(shards):
    n = len(shards)

    def body(*refs):
        for a in range(n):
            refs[n + a][...] = refs[a][...].astype(WIRE_DTYPE)

    return pl.pallas_call(body, name="cast_shards",
                          out_shape=[jax.ShapeDtypeStruct(s.shape, WIRE_DTYPE) for s in shards],
                          compiler_params=pltpu.CompilerParams(vmem_limit_bytes=VMEM_LIMIT))(*shards)


def _ada_forward(c_row, w_ada, b_ada8, carry):
    cols = w_ada.shape[1]

    def body(c_ref, w_ref, b_ref, mod_ref, sc_ref, c_all, send_buf, recv_buf, send1, recv1, send2, recv2):
        x, y, c = _me()
        me = 4 * x + 2 * y + c
        rowi = lax.broadcasted_iota(jnp.int32, (8, D_MODEL), 0)
        c_all[me] = jnp.broadcast_to(c_ref[...], (8, D_MODEL))
        copies = []
        for rel in range(1, 8):
            to, _ = _peer(rel)
            cp = pltpu.make_async_remote_copy(src_ref=c_all.at[me], dst_ref=c_all.at[me], send_sem=send1.at[rel - 1],
                                              recv_sem=recv1.at[rel - 1], device_id=to, device_id_type=MESH)
            cp.start()
            copies.append(cp)
        for rel in range(1, 8):
            _, p = _peer(rel)
            pltpu.make_async_remote_copy(src_ref=c_all.at[p], dst_ref=c_all.at[p], send_sem=send1.at[rel - 1],
                                         recv_sem=recv1.at[rel - 1], device_id=(x, y, c), device_id_type=MESH).wait_recv()
        for cp in copies:
            cp.wait_send()
        cmat = jnp.zeros((8, D_MODEL), F32)
        for b in range(8):
            cmat = jnp.where(rowi == b, c_all[b], cmat)
        sc = cmat * _sigmoid(cmat)
        sc_ref[...] = sc
        modcols = _mm(sc, w_ref[...]) + b_ref[pl.ds(me, 1), :]
        for b in range(8):
            send_buf[b] = jnp.broadcast_to(modcols[b:b + 1, :], (8, cols))
        recv_buf[me] = send_buf[me]
        copies = []
        for rel in range(1, 8):
            to, p = _peer(rel)
            cp = pltpu.make_async_remote_copy(src_ref=send_buf.at[p], dst_ref=recv_buf.at[me], send_sem=send2.at[rel - 1],
                                              recv_sem=recv2.at[rel - 1], device_id=to, device_id_type=MESH)
            cp.start()
            copies.append(cp)
        for rel in range(1, 8):
            _, p = _peer(rel)
            pltpu.make_async_remote_copy(src_ref=send_buf.at[p], dst_ref=recv_buf.at[p], send_sem=send2.at[rel - 1],
                                         recv_sem=recv2.at[rel - 1], device_id=(x, y, c), device_id_type=MESH).wait_recv()
        for cp in copies:
            cp.wait_send()
        rowc = lax.broadcasted_iota(jnp.int32, (8, cols), 0)
        out = jnp.zeros((8, cols), F32)
        for k in range(8):
            out = jnp.where(rowc == k, recv_buf[k], out)
        mod_ref[...] = out

    return _launch(
        body, "ada_forward",
        out_shape=[jax.ShapeDtypeStruct((8, cols), F32), jax.ShapeDtypeStruct((8, D_MODEL), F32)],
        in_specs=[pl.BlockSpec(memory_space=pltpu.VMEM)] * 3,
        out_specs=[pl.BlockSpec(memory_space=pltpu.VMEM)] * 2,
        operands=(c_row, w_ada, b_ada8),
        scratch=[pltpu.VMEM((8, 8, D_MODEL), F32), pltpu.VMEM((8, 8, cols), F32), pltpu.VMEM((8, 8, cols), F32)]
        + [pltpu.SemaphoreType.DMA((7,))] * 4,
        carry=carry)


def _allreduce_small(pack):
    rows = pack.shape[1]

    def body(pack_ref, total_ref, land_ref, send1, recv1, send2, recv2):
        x, y, c = _me()
        me = 4 * x + 2 * y + c
        land_ref[me] = pack_ref[me]
        copies = []
        for rel in range(1, 8):
            to, p = _peer(rel)
            cp = pltpu.make_async_remote_copy(src_ref=pack_ref.at[p], dst_ref=land_ref.at[me], send_sem=send1.at[rel - 1],
                                              recv_sem=recv1.at[rel - 1], device_id=to, device_id_type=MESH)
            cp.start()
            copies.append(cp)
        for rel in range(1, 8):
            _, p = _peer(rel)
            pltpu.make_async_remote_copy(src_ref=pack_ref.at[p], dst_ref=land_ref.at[p], send_sem=send1.at[rel - 1],
                                         recv_sem=recv1.at[rel - 1], device_id=(x, y, c), device_id_type=MESH).wait_recv()
        for cp in copies:
            cp.wait_send()
        acc = land_ref[0]
        for b in range(1, 8):
            acc = acc + land_ref[b]
        total_ref[me] = acc
        copies = []
        for rel in range(1, 8):
            to, _ = _peer(rel)
            cp = pltpu.make_async_remote_copy(src_ref=total_ref.at[me], dst_ref=total_ref.at[me], send_sem=send2.at[rel - 1],
                                              recv_sem=recv2.at[rel - 1], device_id=to, device_id_type=MESH)
            cp.start()
            copies.append(cp)
        for rel in range(1, 8):
            _, p = _peer(rel)
            pltpu.make_async_remote_copy(src_ref=total_ref.at[p], dst_ref=total_ref.at[p], send_sem=send2.at[rel - 1],
                                         recv_sem=recv2.at[rel - 1], device_id=(x, y, c), device_id_type=MESH).wait_recv()
        for cp in copies:
            cp.wait_send()

    return pl.pallas_call(
        body, name="allreduce_small",
        out_shape=[jax.ShapeDtypeStruct((8, rows, 128), F32), jax.ShapeDtypeStruct((8, rows, 128), F32)],
        in_specs=[pl.BlockSpec(memory_space=pltpu.VMEM)],
        out_specs=[pl.BlockSpec(memory_space=pltpu.VMEM)] * 2,
        scratch_shapes=[pltpu.SemaphoreType.DMA((7,))] * 4,
        compiler_params=pltpu.CompilerParams(vmem_limit_bytes=VMEM_LIMIT),
    )(pack)


def _modulated(x, prm_ref, sub, g_row):
    shift, scale = _row(prm_ref, 3 * sub), _row(prm_ref, 3 * sub + 1)
    g = _row(prm_ref, g_row)
    r = _rms_scale(x)
    n0 = x * r
    return (n0 * g) * (1.0 + scale) + shift, r, n0


def _ffn_forward(x, prm, win, wout, sub, g_row, name, carry=None):
    T = x.shape[0]
    tm = min(T, TM_FFN)

    def body(x_ref, prm_ref, win_ref, wout_ref, xo_ref, f_ref):
        xv = x_ref[...]
        h, _, _ = _modulated(xv, prm_ref, sub, g_row)
        hb = h.astype(MXU_DTYPE)
        acc = None
        for j in range(4):
            a = _mm(hb, win_ref[0, j])
            b = _mm(hb, win_ref[1, j])
            s = (a * _sigmoid(a)) * b
            t = _mm(s, wout_ref[j])
            acc = t if acc is None else acc + t
        f_ref[...] = acc
        xo_ref[...] = xv + (0.5 * _row(prm_ref, 3 * sub + 2)) * acc

    tok = pl.BlockSpec((tm, D_MODEL), lambda i: (i, 0))
    return _launch(
        body, name, grid=(T // tm,), semantics=("arbitrary",),
        out_shape=[jax.ShapeDtypeStruct((T, D_MODEL), F32)] * 2,
        in_specs=[tok, _resident(prm), _resident(win), _resident(wout)],
        out_specs=[tok, tok], operands=(x, prm, win, wout), carry=carry, steps=_grid_steps(T // tm))


def _ffn_backward(x, d, prm, win, wout, sub, g_row, name, carry=None):
    T = x.shape[0]
    tm = min(T, TM_FFN)
    nt = T // tm

    def body(x_ref, d_ref, prm_ref, win_ref, wout_ref, dh_ref, dwin_ref, dwout_ref, acc_in, acc_out):
        i = pl.program_id(1)

        @pl.when(i == 0)
        def _():
            acc_in[...] = jnp.zeros_like(acc_in)
            acc_out[...] = jnp.zeros_like(acc_out)

        h, _, _ = _modulated(x_ref[...], prm_ref, sub, g_row)
        hb = h.astype(MXU_DTYPE)
        wa, wb, wo = win_ref[0, 0], win_ref[1, 0], wout_ref[0]
        a = _mm(hb, wa)
        b = _mm(hb, wb)
        sg = _sigmoid(a)
        si = a * sg
        dfs = ((0.5 * _row(prm_ref, 3 * sub + 2)) * d_ref[...]).astype(MXU_DTYPE)
        ds = _mm_nt(dfs, wo)
        acc_out[...] += _mm_tn(si * b, dfs)
        da = ds * b * (sg * (1.0 + a * (1.0 - sg)))
        db = ds * si
        acc_in[0] += _mm_tn(hb, da)
        acc_in[1] += _mm_tn(hb, db)
        dh_ref[0] = _mm_nt(da, wa) + _mm_nt(db, wb)

        @pl.when(i == nt - 1)
        def _():
            dwin_ref[0, 0] = acc_in[0].astype(WIRE_DTYPE)
            dwin_ref[1, 0] = acc_in[1].astype(WIRE_DTYPE)
            dwout_ref[0] = acc_out[...].astype(WIRE_DTYPE)

    def steps():
        j, i = pl.program_id(0), pl.program_id(1)
        return (j == 0) & (i == 0), (j == 2) & (i == 0), (j == 3) & (i == nt - 1)

    tok = pl.BlockSpec((tm, D_MODEL), lambda j, i: (i, 0))
    return _launch(
        body, name, grid=(4, nt), semantics=("arbitrary", "arbitrary"),
        out_shape=[jax.ShapeDtypeStruct((4, T, D_MODEL), F32),
                   jax.ShapeDtypeStruct(win.shape, WIRE_DTYPE), jax.ShapeDtypeStruct(wout.shape, WIRE_DTYPE)],
        in_specs=[tok, tok, _resident(prm),
                  pl.BlockSpec((2, 1, D_MODEL, FF_SHARD), lambda j, i: (0, j, 0, 0)),
                  pl.BlockSpec((1, FF_SHARD, D_MODEL), lambda j, i: (j, 0, 0))],
        out_specs=[pl.BlockSpec((1, tm, D_MODEL), lambda j, i: (j, i, 0)),
                   pl.BlockSpec((2, 1, D_MODEL, FF_SHARD), lambda j, i: (0, j, 0, 0)),
                   pl.BlockSpec((1, FF_SHARD, D_MODEL), lambda j, i: (j, 0, 0))],
        operands=(x, d, prm, win, wout),
        scratch=[pltpu.VMEM((2, D_MODEL, FF_SHARD), F32), pltpu.VMEM((FF_SHARD, D_MODEL), F32)],
        carry=carry, steps=steps)


def _norm_backward(parts, x, d, f, prm, sub, g_row, gate_coef, name, carry=None):
    T = x.shape[0]
    tm = min(T, TM_EW)
    P = parts.shape[0]

    def body(p_ref, x_ref, d_ref, f_ref, prm_ref, dx_ref, sums_ref):
        i = pl.program_id(0)
        dh = p_ref[0]
        for k in range(1, P):
            dh = dh + p_ref[k]
        xv, dv = x_ref[...], d_ref[...]
        scale, g = _row(prm_ref, 3 * sub + 1), _row(prm_ref, g_row)
        r = _rms_scale(xv)
        n0 = xv * r
        dn = dh * (1.0 + scale)
        dn0 = dn * g
        dx_ref[...] = dv + r * (dn0 - n0 * jnp.mean(dn0 * n0, axis=-1, keepdims=True))
        upd = jnp.concatenate([_colsum(dn * n0), _colsum(dh), _colsum(dh * (n0 * g)),
                               gate_coef * _colsum(dv * f_ref[...]), jnp.zeros((4, D_MODEL), F32)], axis=0)

        @pl.when(i == 0)
        def _():
            sums_ref[...] = upd

        @pl.when(i > 0)
        def _():
            sums_ref[...] += upd

    tok = pl.BlockSpec((tm, D_MODEL), lambda i: (i, 0))
    return _launch(
        body, name, grid=(T // tm,), semantics=("arbitrary",),
        out_shape=[jax.ShapeDtypeStruct((T, D_MODEL), F32), jax.ShapeDtypeStruct((8, D_MODEL), F32)],
        in_specs=[pl.BlockSpec((P, tm, D_MODEL), lambda i: (0, i, 0)), tok, tok, tok, _resident(prm)],
        out_specs=[tok, pl.BlockSpec((8, D_MODEL), lambda i: (0, 0))],
        operands=(parts, x, d, f, prm), carry=carry, steps=_grid_steps(T // tm))


def _final_loss(x, target, prm):
    T = x.shape[0]
    tm = min(T, TM_EW)

    def body(x_ref, t_ref, prm_ref, dx_ref, sums_ref):
        i = pl.program_id(0)
        xv = x_ref[...]
        g = _row(prm_ref, ROW_G_FINAL)
        r = _rms_scale(xv)
        n0 = xv * r
        err = n0 * g - t_ref[...]
        dy = err / float(D_MODEL)
        dn0 = dy * g
        dx_ref[...] = r * (dn0 - n0 * jnp.mean(dn0 * n0, axis=-1, keepdims=True))
        loss = 0.5 * jnp.sum(jnp.mean(err * err, axis=-1, keepdims=True), axis=0, keepdims=True)
        upd = jnp.concatenate([_colsum(dy * n0), jnp.broadcast_to(loss, (1, D_MODEL)), jnp.zeros((6, D_MODEL), F32)], axis=0)

        @pl.when(i == 0)
        def _():
            sums_ref[...] = upd

        @pl.when(i > 0)
        def _():
            sums_ref[...] += upd

    tok = pl.BlockSpec((tm, D_MODEL), lambda i: (i, 0))
    return pl.pallas_call(
        body, name="final_loss", grid=(T // tm,),
        out_shape=[jax.ShapeDtypeStruct((T, D_MODEL), F32), jax.ShapeDtypeStruct((8, D_MODEL), F32)],
        in_specs=[tok, tok, pl.BlockSpec(prm.shape, lambda i: (0, 0))],
        out_specs=[tok, pl.BlockSpec((8, D_MODEL), lambda i: (0, 0))],
        compiler_params=_params("arbitrary"),
    )(x, target, prm)


def _ssm_discretise(lam_re_log, lam_im, log_dt):
    lr = -jnp.exp(lam_re_log)
    dt = jnp.exp(log_dt)
    mag = jnp.exp(lr * dt)
    ang = lam_im * dt
    ab_re = mag * jnp.cos(ang)
    ab_im = mag * jnp.sin(ang)
    num_re = ab_re - 1.0
    num_im = ab_im
    den = lr * lr + lam_im * lam_im
    f_re = (num_re * lr + num_im * lam_im) / den
    f_im = (num_im * lr - num_re * lam_im) / den
    return ab_re, ab_im, f_re, f_im


def _ssm_params_forward(lam_re_log, lam_im, log_dt):
    def body(a_ref, b_ref, c_ref, o0, o1, o2, o3):
        outs = _ssm_discretise(a_ref[...], b_ref[...], c_ref[...])
        for o, v in zip((o0, o1, o2, o3), outs):
            o[...] = v

    return pl.pallas_call(body, name="ssm_params_forward",
                          out_shape=[jax.ShapeDtypeStruct(lam_im.shape, F32)] * 4)(lam_re_log, lam_im, log_dt)


def _ssm_params_backward(lam_re_log, lam_im, log_dt, cot):
    def body(a_ref, b_ref, c_ref, g0, g1, g2, g3, o0, o1, o2):
        _, vjp = jax.vjp(_ssm_discretise, a_ref[...], b_ref[...], c_ref[...])
        d0, d1, d2 = vjp((g0[...], g1[...], g2[...], g3[...]))
        o0[...] = d0
        o1[...] = d1
        o2[...] = d2

    return pl.pallas_call(
        body, name="ssm_params_backward",
        out_shape=[jax.ShapeDtypeStruct(lam_im.shape, F32), jax.ShapeDtypeStruct(lam_im.shape, F32),
                   jax.ShapeDtypeStruct(log_dt.shape, F32)])(lam_re_log, lam_im, log_dt, *cot)


def _ssm_dense_forward(srow, b_dense, c_dense):
    def body(srow_ref, bd_ref, cd_ref, bb_ref, ct_ref):
        for j in range(SSM_BLOCKS):
            lanes = slice(j * SSM_BLOCK_STATE, (j + 1) * SSM_BLOCK_STATE)
            f_re, f_im = srow_ref[2:3, lanes], srow_ref[3:4, lanes]
            bb_ref[0, j] = (f_re * bd_ref[0, j] - f_im * bd_ref[1, j]).astype(MXU_DTYPE)
            bb_ref[1, j] = (f_re * bd_ref[1, j] + f_im * bd_ref[0, j]).astype(MXU_DTYPE)
            ct_ref[0, j] = cd_ref[0, j].astype(MXU_DTYPE)
            ct_ref[1, j] = cd_ref[1, j].astype(MXU_DTYPE)

    return pl.pallas_call(body, name="ssm_dense_forward",
                          out_shape=[jax.ShapeDtypeStruct(b_dense.shape, MXU_DTYPE),
                                     jax.ShapeDtypeStruct(c_dense.shape, MXU_DTYPE)],
                          compiler_params=pltpu.CompilerParams(vmem_limit_bytes=VMEM_LIMIT))(srow, b_dense, c_dense)


def _cmul(p, q):
    return p[0] * q[0] - p[1] * q[1], p[0] * q[1] + p[1] * q[0]


def _scan_coefficients(ar, ai, reverse):
    n = ar.shape[1]
    p = {1: (ar, ai)}
    p[2] = _cmul(p[1], p[1])
    p[3] = _cmul(p[2], p[1])
    p[4] = _cmul(p[2], p[2])
    p[5] = _cmul(p[4], p[1])
    p[6] = _cmul(p[4], p[2])
    p[7] = _cmul(p[4], p[3])
    p[8] = _cmul(p[4], p[4])
    rowi = lax.broadcasted_iota(jnp.int32, (SCAN_ROWS, n), 0)
    tiles = []
    for dstep in (1, 2, 4):
        keep = (rowi < SCAN_ROWS - dstep) if reverse else (rowi >= dstep)
        for part in p[dstep]:
            tiles.append(jnp.where(keep, jnp.broadcast_to(part, (SCAN_ROWS, n)), 0.0))
    for comp in (0, 1):
        t = jnp.zeros((SCAN_ROWS, n), F32)
        for rr in range(SCAN_ROWS):
            power = SCAN_ROWS - rr if reverse else rr + 1
            t = jnp.where(rowi == rr, jnp.broadcast_to(p[power][comp], (SCAN_ROWS, n)), t)
        tiles.append(t)
    return tiles


def _load_stack(stack_hbm, dst, sems, base):
    cols = stack_hbm.shape[2]
    cps = [pltpu.make_async_copy(stack_hbm.at[k], dst.at[:, pl.ds(k * cols, cols)], sems.at[base + k])
           for k in range(NDEV)]
    for cp in cps:
        cp.start()
    return cps


def _window_lanes():
    lane = lax.broadcasted_iota(jnp.int32, (1, POOL_WIDTH), 1)
    return jnp.where(lane < 128, 2.0, jnp.where(lane < 256, 4.0, jnp.where(lane < 384, 8.0, 16.0)))


def _gelu(y):
    return 0.5 * y * (1.0 + lax.erf(y * 0.7071067811865476))


def _gelu_grad(y):
    return 0.5 * (1.0 + lax.erf(y * 0.7071067811865476)) + y * jnp.exp(-0.5 * y * y) * 0.3989422804014327


def _mixer_forward(x, prm, w_in_s, w_pu_s, w_glu_s, w_su_s, w_out, pool_w, mvec, srow, bb, ct, carry=None):
    T = x.shape[0]
    tm = min(T, TM_MIX)
    nt = T // tm
    n_tiles = tm // SCAN_ROWS

    def body(x_ref, prm_ref, w_in_h, w_pu_h, w_glu_h, w_su_h, w_out_h, pw_ref, mv_ref, srow_ref, bb, ct,
             x2_ref, mo_ref, z_ref, sre_ref, sim_ref, zp_ref, q_ref, yp_ref, yss_ref, vg_ref, ys_ref,
             w_in, w_pu, w_glu, w_su, w_o, coef, carry, hist, bu, sems):
        i = pl.program_id(0)

        @pl.when(i == 0)
        def _():
            cps = (_load_stack(w_in_h, w_in, sems, 0) + _load_stack(w_pu_h, w_pu, sems, 8)
                   + _load_stack(w_glu_h, w_glu, sems, 16) + _load_stack(w_su_h, w_su, sems, 24))
            cps.append(pltpu.make_async_copy(w_out_h, w_o, sems.at[32]))
            cps[-1].start()
            for j in range(SSM_BLOCKS):
                lanes = slice(j * SSM_BLOCK_STATE, (j + 1) * SSM_BLOCK_STATE)
                for k, t in enumerate(_scan_coefficients(srow_ref[0:1, lanes], srow_ref[1:2, lanes], False)):
                    coef[j, k] = t
            carry[...] = jnp.zeros_like(carry)
            hist[...] = jnp.zeros_like(hist)
            for cp in cps:
                cp.wait()

        xv = x_ref[...]
        h, _, _ = _modulated(xv, prm_ref, 1, ROW_G_MIX)
        z = _mm(h, w_in[...])
        z_ref[...] = z
        u_pool, u_ssm = z[:, 0:512], z[:, 512:1024]
        gl_pool, gl_ssm = z[:, 1024:2048], z[:, 2048:3072]

        ext = jnp.concatenate([hist[...], u_pool], axis=0)
        w2 = ext + pltpu.roll(ext, 1, 0)
        w4 = w2[:, 128:] + pltpu.roll(w2[:, 128:], 2, 0)
        w8 = w4[:, 128:] + pltpu.roll(w4[:, 128:], 4, 0)
        w16 = w8[:, 128:] + pltpu.roll(w8[:, 128:], 8, 0)
        wsum = jnp.concatenate([w2[POOL_HALO:, :128], w4[POOL_HALO:, :128], w8[POOL_HALO:, :128], w16[POOL_HALO:]], axis=1)
        hist[...] = u_pool[tm - POOL_HALO:, :]
        t1 = (lax.broadcasted_iota(jnp.int32, (tm, 1), 0) + (i * tm + 1)).astype(F32)
        zp = wsum / jnp.minimum(t1, _window_lanes()) - u_pool
        zp_ref[...] = zp
        q = jnp.concatenate([_mm(zp[:, k * 128:(k + 1) * 128], pw_ref[k]) for k in range(4)], axis=1)
        q = q + mv_ref[ROW_POOL_B:ROW_POOL_B + 1, 0:512]
        q_ref[...] = q
        y_pool = _mm(q * mv_ref[ROW_POOL_SCALE:ROW_POOL_SCALE + 1, 0:512], w_pu[...])
        yp_ref[...] = y_pool

        y_blocks = []
        for j in range(SSM_BLOCKS):
            lanes = pl.ds(j * SSM_BLOCK_STATE, SSM_BLOCK_STATE)
            ub = u_ssm[:, j * 128:(j + 1) * 128].astype(MXU_DTYPE)
            bu[0] = _mm(ub, bb[0, j])
            bu[1] = _mm(ub, bb[1, j])
            a1r, a1i, a2r, a2i, a4r, a4i, pr, pi = [coef[j, k] for k in range(8)]

            def step(tt, c, lanes=lanes, a1r=a1r, a1i=a1i, a2r=a2r, a2i=a2i, a4r=a4r, a4i=a4i, pr=pr, pi=pi):
                cr, ci = c
                rows = pl.ds(pl.multiple_of(tt * SCAN_ROWS, SCAN_ROWS), SCAN_ROWS)
                xr, xi = bu[0, rows, :], bu[1, rows, :]
                for dstep, kr, ki in ((1, a1r, a1i), (2, a2r, a2i), (4, a4r, a4i)):
                    sr, si = pltpu.roll(xr, dstep, 0), pltpu.roll(xi, dstep, 0)
                    xr, xi = xr + kr * sr - ki * si, xi + kr * si + ki * sr
                xr, xi = xr + pr * cr - pi * ci, xi + pr * ci + pi * cr
                sre_ref[rows, lanes] = xr
                sim_ref[rows, lanes] = xi
                return (jnp.broadcast_to(xr[SCAN_ROWS - 1:SCAN_ROWS, :], xr.shape),
                        jnp.broadcast_to(xi[SCAN_ROWS - 1:SCAN_ROWS, :], xi.shape))

            cr, ci = lax.fori_loop(0, n_tiles, step, (carry[j, 0], carry[j, 1]))
            carry[j, 0] = cr
            carry[j, 1] = ci
            y_blocks.append(_mm(sre_ref[:, lanes], ct[0, j]) - _mm(sim_ref[:, lanes], ct[1, j]))
        yss = jnp.concatenate(y_blocks, axis=1) + mv_ref[ROW_SSM_D:ROW_SSM_D + 1, 0:512] * u_ssm
        yss_ref[...] = yss
        vg = _mm(_gelu(yss), w_glu[...]) + mv_ref[ROW_B_GLU:ROW_B_GLU + 1, :]
        vg_ref[...] = vg
        y_ssm = _mm(vg[:, 0:512] * _sigmoid(vg[:, 512:1024]), w_su[...])
        ys_ref[...] = y_ssm

        merged = _sigmoid(gl_pool) * y_pool + _sigmoid(gl_ssm) * y_ssm
        mo = _mm(merged, w_o[...])
        mo_ref[...] = mo
        x2_ref[...] = xv + _row(prm_ref, 5) * mo

    def tok(width):
        return pl.BlockSpec((tm, width), lambda i: (i, 0))

    hbm = _HBM
    widths = (D_MODEL, D_MODEL, IN_WIDTH, N_STATE, N_STATE, 512, 512, D_MODEL, 512, D_MODEL, D_MODEL)
    return _launch(
        body, "mixer_forward", grid=(nt,), semantics=("arbitrary",), carry=carry, steps=_grid_steps(nt),
        out_shape=[jax.ShapeDtypeStruct((T, w), F32) for w in widths],
        in_specs=[tok(D_MODEL), _resident(prm), hbm, hbm, hbm, hbm, hbm, _resident(pool_w), _resident(mvec),
                  _resident(srow), _resident(bb), _resident(ct)],
        out_specs=[tok(w) for w in widths],
        operands=(x, prm, w_in_s, w_pu_s, w_glu_s, w_su_s, w_out, pool_w, mvec, srow, bb, ct),
        scratch=[
            pltpu.VMEM((D_MODEL, IN_WIDTH), MXU_DTYPE), pltpu.VMEM((512, D_MODEL), MXU_DTYPE),
            pltpu.VMEM((512, D_MODEL), MXU_DTYPE), pltpu.VMEM((512, D_MODEL), MXU_DTYPE),
            pltpu.VMEM((D_MODEL, D_MODEL), MXU_DTYPE),
            pltpu.VMEM((SSM_BLOCKS, 8, SCAN_ROWS, SSM_BLOCK_STATE), F32),
            pltpu.VMEM((SSM_BLOCKS, 2, SCAN_ROWS, SSM_BLOCK_STATE), F32),
            pltpu.VMEM((POOL_HALO, POOL_WIDTH), F32),
            pltpu.VMEM((2, tm, SSM_BLOCK_STATE), F32),
            pltpu.SemaphoreType.DMA((33,)),
        ])


def _mixer_backward(d2, prm, saved, w_pu_s, w_glu_s, w_su_s, w_out, pool_w, mvec, srow, bb, ct, carry=None):
    z, s_re, s_im, zp, q, y_pool, yss, vg, y_ssm = saved
    T = d2.shape[0]
    tm = min(T, TM_MIX_BWD)
    nt = T // tm
    n_tiles = tm // SCAN_ROWS

    def body(d_ref, prm_ref, z_ref, sre_ref, sim_ref, zp_ref, q_ref, yp_ref, yss_ref, vg_ref, ys_ref,
             w_pu_h, w_glu_h, w_su_h, w_out_h, pw_ref, mv_ref, srow_ref, bb, ct,
             dz_ref, dwo_h, dwpu_h, dwglu_h, dwsu_h, dpw_h, dbb_h, dct_h, vsum_h, da_h,
             w_pu, w_glu, w_su, w_o, pwb, coef, carry, hist, dre, lam,
             a_wo, a_wpu, a_wglu, a_wsu, a_pw, a_bb, a_ct, a_vs, a_da, st_wo, st_up, sems):
        i = pl.program_id(0)
        tile = nt - 1 - i

        @pl.when(i == 0)
        def _():
            cps = (_load_stack(w_pu_h, w_pu, sems, 0) + _load_stack(w_glu_h, w_glu, sems, 8)
                   + _load_stack(w_su_h, w_su, sems, 16))
            cps.append(pltpu.make_async_copy(w_out_h, w_o, sems.at[24]))
            cps[-1].start()
            pwb[...] = pw_ref[...].astype(MXU_DTYPE)
            for j in range(SSM_BLOCKS):
                lanes = slice(j * SSM_BLOCK_STATE, (j + 1) * SSM_BLOCK_STATE)
                for k, t in enumerate(_scan_coefficients(srow_ref[0:1, lanes], srow_ref[1:2, lanes], True)):
                    coef[j, k] = t
            for acc in (carry, hist, a_wo, a_wpu, a_wglu, a_wsu, a_pw, a_bb, a_ct, a_vs, a_da):
                acc[...] = jnp.zeros_like(acc)
            for cp in cps:
                cp.wait()

        dv = d_ref[...]
        zt = z_ref[...]
        u_ssm, gl_pool, gl_ssm = zt[:, 512:1024], zt[:, 1024:2048], zt[:, 2048:3072]
        y_p, y_s = yp_ref[...], ys_ref[...]
        sgp, sgs = _sigmoid(gl_pool), _sigmoid(gl_ssm)
        dmo = (_row(prm_ref, 5) * dv).astype(MXU_DTYPE)
        a_wo[...] += _mm_tn(sgp * y_p + sgs * y_s, dmo)
        dmerged = _mm_nt(dmo, w_o[...])
        dy_pool = dmerged * sgp
        dgl_pool = dmerged * y_p * (sgp * (1.0 - sgp))
        dy_ssm = dmerged * sgs
        dgl_ssm = dmerged * y_s * (sgs * (1.0 - sgs))

        scale = mv_ref[ROW_POOL_SCALE:ROW_POOL_SCALE + 1, 0:512]
        qv, zpv = q_ref[...], zp_ref[...]
        a_wpu[...] += _mm_tn(qv * scale, dy_pool)
        dp = _mm_nt(dy_pool, w_pu[...])
        dq = dp * scale
        a_vs[0:1, 0:512] += _colsum(dp * qv)
        a_vs[1:2, 0:512] += _colsum(dq)
        dzp_blocks = []
        for k in range(4):
            lanes = slice(k * 128, (k + 1) * 128)
            dzp_blocks.append(_mm_nt(dq[:, lanes], pwb[k]))
            a_pw[k] += _mm_tn(zpv[:, lanes], dq[:, lanes])
        dzp = jnp.concatenate(dzp_blocks, axis=1)
        t1 = (lax.broadcasted_iota(jnp.int32, (tm, 1), 0) + (tile * tm + 1)).astype(F32)
        gs = dzp / jnp.minimum(t1, _window_lanes())
        n_ext = tm + POOL_HALO
        ext = jnp.concatenate([gs, hist[...]], axis=0)
        v2 = ext + pltpu.roll(ext, n_ext - 1, 0)
        v4 = v2[:, 128:] + pltpu.roll(v2[:, 128:], n_ext - 2, 0)
        v8 = v4[:, 128:] + pltpu.roll(v4[:, 128:], n_ext - 4, 0)
        v16 = v8[:, 128:] + pltpu.roll(v8[:, 128:], n_ext - 8, 0)
        msum = jnp.concatenate([v2[:tm, :128], v4[:tm, :128], v8[:tm, :128], v16[:tm]], axis=1)
        hist[...] = gs[0:POOL_HALO, :]
        du_pool = msum - dzp

        vgv = vg_ref[...]
        val, gate = vgv[:, 0:512], vgv[:, 512:1024]
        sgg = _sigmoid(gate)
        a_wsu[...] += _mm_tn(val * sgg, dy_ssm)
        do = _mm_nt(dy_ssm, w_su[...])
        dvg = jnp.concatenate([do * sgg, do * val * (sgg * (1.0 - sgg))], axis=1)
        a_vs[3:4, :] += _colsum(dvg)
        yv = yss_ref[...]
        a_wglu[...] += _mm_tn(_gelu(yv), dvg)
        dyss = _mm_nt(dvg, w_glu[...]) * _gelu_grad(yv)
        a_vs[2:3, 0:512] += _colsum(dyss * u_ssm)
        du_blocks = []
        for j in range(SSM_BLOCKS):
            lanes = pl.ds(j * SSM_BLOCK_STATE, SSM_BLOCK_STATE)
            in_lanes = slice(j * 128, (j + 1) * 128)
            dyb = dyss[:, in_lanes].astype(MXU_DTYPE)
            ub = u_ssm[:, in_lanes].astype(MXU_DTYPE)
            dre[0] = _mm_nt(dyb, ct[0, j])
            dre[1] = -_mm_nt(dyb, ct[1, j])
            a_ct[0, j] += _mm_tn(sre_ref[:, lanes], dyb)
            a_ct[1, j] -= _mm_tn(sim_ref[:, lanes], dyb)
            a1r, a1i, a2r, a2i, a4r, a4i, pr, pi = [coef[j, k] for k in range(8)]
            rowi = lax.broadcasted_iota(jnp.int32, (SCAN_ROWS, SSM_BLOCK_STATE), 0)

            def step(tt, c, lanes=lanes, a1r=a1r, a1i=a1i, a2r=a2r, a2i=a2i, a4r=a4r, a4i=a4i, pr=pr, pi=pi, rowi=rowi):
                cr, ci, acc_r, acc_i = c
                rows = pl.ds(pl.multiple_of((n_tiles - 1 - tt) * SCAN_ROWS, SCAN_ROWS), SCAN_ROWS)
                xr, xi = dre[0, rows, :], dre[1, rows, :]
                for dstep, kr, ki in ((1, a1r, a1i), (2, a2r, a2i), (4, a4r, a4i)):
                    sr, si = pltpu.roll(xr, SCAN_ROWS - dstep, 0), pltpu.roll(xi, SCAN_ROWS - dstep, 0)
                    xr, xi = xr + kr * sr + ki * si, xi + kr * si - ki * sr
                xr, xi = xr + pr * cr + pi * ci, xi + pr * ci - pi * cr
                lam[0, rows, :] = xr
                lam[1, rows, :] = xi
                nr = jnp.where(rowi == SCAN_ROWS - 1, cr, pltpu.roll(xr, SCAN_ROWS - 1, 0))
                ni = jnp.where(rowi == SCAN_ROWS - 1, ci, pltpu.roll(xi, SCAN_ROWS - 1, 0))
                s_r, s_i = sre_ref[rows, lanes], sim_ref[rows, lanes]
                acc_r = acc_r + nr * s_r + ni * s_i
                acc_i = acc_i + ni * s_r - nr * s_i
                return (jnp.broadcast_to(xr[0:1, :], xr.shape), jnp.broadcast_to(xi[0:1, :], xi.shape), acc_r, acc_i)

            cr, ci, acc_r, acc_i = lax.fori_loop(0, n_tiles, step, (carry[j, 0], carry[j, 1], a_da[0, j], a_da[1, j]))
            carry[j, 0] = cr
            carry[j, 1] = ci
            a_da[0, j] = acc_r
            a_da[1, j] = acc_i
            lr_b, li_b = lam[0].astype(MXU_DTYPE), lam[1].astype(MXU_DTYPE)
            a_bb[0, j] += _mm_tn(ub, lr_b)
            a_bb[1, j] += _mm_tn(ub, li_b)
            du_blocks.append(_mm_nt(lr_b, bb[0, j]) + _mm_nt(li_b, bb[1, j]))
        du_ssm = jnp.concatenate(du_blocks, axis=1) + dyss * mv_ref[ROW_SSM_D:ROW_SSM_D + 1, 0:512]
        dz_ref[...] = jnp.concatenate([du_pool, du_ssm, dgl_pool, dgl_ssm], axis=1)

        @pl.when(i == nt - 1)
        def _():
            rows = D_MODEL // NDEV
            for k in range(NDEV):
                st_wo[k] = a_wo[k * rows:(k + 1) * rows, :].astype(WIRE_DTYPE)
                for a, acc in enumerate((a_wpu, a_wglu, a_wsu)):
                    st_up[a, k] = acc[:, k * 128:(k + 1) * 128].astype(WIRE_DTYPE)
            outs = ((st_wo, dwo_h), (st_up.at[0], dwpu_h), (st_up.at[1], dwglu_h), (st_up.at[2], dwsu_h),
                    (a_pw, dpw_h), (a_bb, dbb_h), (a_ct, dct_h), (a_vs, vsum_h), (a_da, da_h))
            cps = [pltpu.make_async_copy(src, dst, sems.at[k]) for k, (src, dst) in enumerate(outs)]
            for cp in cps:
                cp.start()
            for cp in cps:
                cp.wait()

    def tok(width):
        return pl.BlockSpec((tm, width), lambda i: (nt - 1 - i, 0))

    hbm = _HBM
    acc_shapes = [(D_MODEL, D_MODEL), (512, D_MODEL), (512, D_MODEL), (512, D_MODEL), (4, 128, 128),
                  (2, SSM_BLOCKS, 128, SSM_BLOCK_STATE), (2, SSM_BLOCKS, SSM_BLOCK_STATE, 128), (8, D_MODEL),
                  (2, SSM_BLOCKS, SCAN_ROWS, SSM_BLOCK_STATE)]
    stack_out = [jax.ShapeDtypeStruct((NDEV, D_MODEL // NDEV, D_MODEL), WIRE_DTYPE)] \
        + [jax.ShapeDtypeStruct((NDEV, 512, 128), WIRE_DTYPE)] * 3
    return _launch(
        body, "mixer_backward", grid=(nt,), semantics=("arbitrary",), carry=carry, steps=_grid_steps(nt),
        out_shape=[jax.ShapeDtypeStruct((T, IN_WIDTH), F32)] + stack_out
        + [jax.ShapeDtypeStruct(s, F32) for s in acc_shapes[4:]],
        in_specs=[tok(D_MODEL), _resident(prm), tok(IN_WIDTH), tok(N_STATE), tok(N_STATE), tok(512), tok(512),
                  tok(D_MODEL), tok(512), tok(D_MODEL), tok(D_MODEL), hbm, hbm, hbm, hbm, _resident(pool_w),
                  _resident(mvec), _resident(srow), _resident(bb), _resident(ct)],
        out_specs=[tok(IN_WIDTH)] + [hbm] * len(acc_shapes),
        operands=(d2, prm, z, s_re, s_im, zp, q, y_pool, yss, vg, y_ssm, w_pu_s, w_glu_s, w_su_s, w_out, pool_w, mvec,
                  srow, bb, ct),
        scratch=[
            pltpu.VMEM((512, D_MODEL), MXU_DTYPE), pltpu.VMEM((512, D_MODEL), MXU_DTYPE),
            pltpu.VMEM((512, D_MODEL), MXU_DTYPE), pltpu.VMEM((D_MODEL, D_MODEL), MXU_DTYPE),
            pltpu.VMEM((4, 128, 128), MXU_DTYPE),
            pltpu.VMEM((SSM_BLOCKS, 8, SCAN_ROWS, SSM_BLOCK_STATE), F32),
            pltpu.VMEM((SSM_BLOCKS, 2, SCAN_ROWS, SSM_BLOCK_STATE), F32),
            pltpu.VMEM((POOL_HALO, POOL_WIDTH), F32),
            pltpu.VMEM((2, tm, SSM_BLOCK_STATE), F32), pltpu.VMEM((2, tm, SSM_BLOCK_STATE), F32),
        ] + [pltpu.VMEM(s, F32) for s in acc_shapes]
        + [pltpu.VMEM((NDEV, D_MODEL // NDEV, D_MODEL), WIRE_DTYPE), pltpu.VMEM((3, NDEV, 512, 128), WIRE_DTYPE),
           pltpu.SemaphoreType.DMA((25,))])


def _mixer_in_backward(x, dz, prm, w_in_s):
    T = x.shape[0]
    tm = min(T, TM_MIX)
    nt = T // tm
    cols = IN_WIDTH // NDEV

    def body(x_ref, dz_ref, prm_ref, w_in_h, dh_ref, dw_ref, w_in, acc, sems):
        i = pl.program_id(0)

        @pl.when(i == 0)
        def _():
            cps = _load_stack(w_in_h, w_in, sems, 0)
            acc[...] = jnp.zeros_like(acc)
            for cp in cps:
                cp.wait()

        h, _, _ = _modulated(x_ref[...], prm_ref, 1, ROW_G_MIX)
        dzb = dz_ref[...].astype(MXU_DTYPE)
        dh_ref[0] = _mm_nt(dzb, w_in[...])
        acc[...] += _mm_tn(h, dzb)

        @pl.when(i == nt - 1)
        def _():
            for k in range(NDEV):
                dw_ref[k] = acc[:, k * cols:(k + 1) * cols].astype(WIRE_DTYPE)

    return pl.pallas_call(
        body, name="mixer_in_backward", grid=(nt,),
        out_shape=[jax.ShapeDtypeStruct((1, T, D_MODEL), F32), jax.ShapeDtypeStruct((NDEV, D_MODEL, cols), WIRE_DTYPE)],
        in_specs=[pl.BlockSpec((tm, D_MODEL), lambda i: (i, 0)), pl.BlockSpec((tm, IN_WIDTH), lambda i: (i, 0)),
                  pl.BlockSpec(prm.shape, lambda i: (0, 0)), pl.BlockSpec(memory_space=pl.ANY)],
        out_specs=[pl.BlockSpec((1, tm, D_MODEL), lambda i: (0, i, 0)),
                   pl.BlockSpec((NDEV, D_MODEL, cols), lambda i: (0, 0, 0))],
        scratch_shapes=[pltpu.VMEM((D_MODEL, IN_WIDTH), MXU_DTYPE), pltpu.VMEM((D_MODEL, IN_WIDTH), F32),
                        pltpu.SemaphoreType.DMA((8,))],
        compiler_params=_params("arbitrary"),
    )(x, dz, prm, w_in_s)


def _ssm_dense_backward(dbb, da, srow, b_dense):
    def body(dbb_ref, da_ref, srow_ref, bd_ref, db_ref, df_ref):
        df_re, df_im = [], []
        da_re = [_colsum(da_ref[0, j]) for j in range(SSM_BLOCKS)]
        da_im = [_colsum(da_ref[1, j]) for j in range(SSM_BLOCKS)]
        for j in range(SSM_BLOCKS):
            lanes = slice(j * SSM_BLOCK_STATE, (j + 1) * SSM_BLOCK_STATE)
            f_re, f_im = srow_ref[2:3, lanes], srow_ref[3:4, lanes]
            g_re, g_im = dbb_ref[0, j], dbb_ref[1, j]
            b_re, b_im = bd_ref[0, j], bd_ref[1, j]
            db_ref[0, j] = f_re * g_re + f_im * g_im
            db_ref[1, j] = f_re * g_im - f_im * g_re
            df_re.append(_colsum(g_re * b_re + g_im * b_im))
            df_im.append(_colsum(g_im * b_re - g_re * b_im))
        df_ref[...] = jnp.concatenate([jnp.concatenate(df_re, axis=1), jnp.concatenate(df_im, axis=1),
                                       jnp.concatenate(da_re, axis=1), jnp.concatenate(da_im, axis=1),
                                       jnp.zeros((4, N_STATE), F32)], axis=0)

    return pl.pallas_call(body, name="ssm_dense_backward",
                          out_shape=[jax.ShapeDtypeStruct(b_dense.shape, F32), jax.ShapeDtypeStruct((8, N_STATE), F32)],
                          compiler_params=pltpu.CompilerParams(vmem_limit_bytes=VMEM_LIMIT))(dbb, da, srow, b_dense)


def _adamw_update(w, g, m, v):
    m = ADAM_B1 * m + (1.0 - ADAM_B1) * g
    v = ADAM_B2 * v + (1.0 - ADAM_B2) * (g * g)
    m_hat = m / (1.0 - ADAM_B1 ** ADAM_STEP)
    v_hat = v / (1.0 - ADAM_B2 ** ADAM_STEP)
    delta = -ADAM_LR * (m_hat / (jnp.sqrt(v_hat) + ADAM_EPS) + ADAM_WD * w)
    return delta, m, v


def _adam_rows(shape):
    rows, cols = shape
    tr = rows
    while tr * cols * 4 > (1 << 20) and tr % 16 == 0:
        tr //= 2
    return tr


def _adam_sharded(w, m, v, land, name):
    R, C = w.shape
    tr = _adam_rows((R, C))

    def body(w_ref, m_ref, v_ref, land_ref, g_ref, d_ref, mo_ref, vo_ref):
        g = land_ref[0].astype(F32)
        for b in range(1, NDEV):
            g = g + land_ref[b].astype(F32)
        g_ref[...] = g
        d_ref[...], mo_ref[...], vo_ref[...] = _adamw_update(w_ref[...], g, m_ref[...], v_ref[...])

    blk = pl.BlockSpec((tr, C), lambda i: (i, 0))
    return pl.pallas_call(
        body, name=name, grid=(R // tr,),
        out_shape=[jax.ShapeDtypeStruct((R, C), F32)] * 4,
        in_specs=[blk, blk, blk, pl.BlockSpec((NDEV, tr, C), lambda i: (0, i, 0))],
        out_specs=[blk] * 4,
        compiler_params=_params("arbitrary"),
    )(w, m, v, land)


def _adam_ada(w, m, v, sc_all, dmod_cols):
    R, C = w.shape
    tr = 256

    def body(w_ref, m_ref, v_ref, sc_ref, dm_ref, g_ref, d_ref, mo_ref, vo_ref):
        g = _mm_tn(sc_ref[...], dm_ref[...])
        g_ref[...] = g
        d_ref[...], mo_ref[...], vo_ref[...] = _adamw_update(w_ref[...], g, m_ref[...], v_ref[...])

    blk = pl.BlockSpec((tr, C), lambda i: (i, 0))
    return pl.pallas_call(
        body, name="adam_w_ada", grid=(R // tr,),
        out_shape=[jax.ShapeDtypeStruct((R, C), F32)] * 4,
        in_specs=[blk, blk, blk, pl.BlockSpec((8, tr), lambda i: (0, i)), pl.BlockSpec((8, C), lambda i: (0, 0))],
        out_specs=[blk] * 4,
        compiler_params=_params("arbitrary"),
    )(w, m, v, sc_all, dmod_cols)


def _adam_small(w, g, m, v):
    def body(w_ref, g_ref, m_ref, v_ref, d_ref, mo_ref, vo_ref):
        d_ref[...], mo_ref[...], vo_ref[...] = _adamw_update(w_ref[...], g_ref[...], m_ref[...], v_ref[...])

    return pl.pallas_call(body, name="adam_small", out_shape=[jax.ShapeDtypeStruct(w.shape, F32)] * 3,
                          compiler_params=pltpu.CompilerParams(vmem_limit_bytes=VMEM_LIMIT))(w, g, m, v)


def _block_diag_in(b):
    bt = jnp.transpose(b, (0, 2, 1)).reshape(SSM_BLOCKS, 8, SSM_GROUP, SSM_STATE)
    eye = jnp.eye(8, dtype=bool)[None, :, None, :, None]
    return jnp.where(eye, bt[:, :, :, None, :], 0.0).reshape(SSM_BLOCKS, 128, SSM_BLOCK_STATE)


def _block_diag_out(c):
    ct = jnp.transpose(c, (0, 2, 1)).reshape(SSM_BLOCKS, 8, SSM_STATE, SSM_GROUP)
    eye = jnp.eye(8, dtype=bool)[None, :, None, :, None]
    return jnp.where(eye, ct[:, :, :, None, :], 0.0).reshape(SSM_BLOCKS, SSM_BLOCK_STATE, 128)


def _diag_blocks(dense, rows, cols):
    d5 = dense.reshape(SSM_BLOCKS, 8, rows, 8, cols)
    return jnp.stack([d5[:, a, :, a, :] for a in range(8)], axis=1).reshape(32, rows, cols)


def _pack_small(ada_vec, parts):
    rest = jnp.concatenate([parts[n].reshape(-1) for n, _ in SMALL_PARAMS])
    rest = jnp.pad(rest, (0, NDEV * REST_ROWS * 128 - SMALL_TOTAL)).reshape(NDEV, REST_ROWS, 128)
    return jnp.concatenate([ada_vec.reshape(NDEV, ADA_ROWS, 128), rest,
                            jnp.zeros((NDEV, PACK_ROWS - ADA_ROWS - REST_ROWS, 128), F32)], axis=1)


def _unpack_small(pack, shapes):
    ada_vec = pack[:, :ADA_ROWS].reshape(-1)
    rest = pack[:, ADA_ROWS:ADA_ROWS + REST_ROWS].reshape(-1)
    out, off = {}, 0
    for n, size in SMALL_PARAMS:
        out[n] = rest[off:off + size].reshape(shapes[n])
        off += size
    return ada_vec, out


WEIGHT_ORDER = ('w_ada', 'b_ada', 'g_ffn1', 'w_ffn1_in', 'w_ffn1_out', 'g_mix', 'w_in', 'pool_w', 'pool_b',
                'pool_scale', 'w_pool_up', 'ssm_lam_re_log', 'ssm_lam_im', 'ssm_log_dt', 'ssm_b_re', 'ssm_b_im',
                'ssm_c_re', 'ssm_c_im', 'ssm_d', 'w_glu', 'b_glu', 'w_ssm_up', 'w_out', 'g_ffn2', 'w_ffn2_in',
                'w_ffn2_out', 'g_final')
GATHERED = ('w_ffn1_in', 'w_ffn1_out', 'w_in', 'w_pool_up', 'w_glu', 'w_ssm_up', 'w_out', 'w_ffn2_in', 'w_ffn2_out')


def kernel(x, c, w_ada, b_ada, g_ffn1, w_ffn1_in, w_ffn1_out, g_mix, w_in, pool_w, pool_b, pool_scale, w_pool_up, ssm_lam_re_log, ssm_lam_im, ssm_log_dt, ssm_b_re, ssm_b_im, ssm_c_re, ssm_c_im, ssm_d, w_glu, b_glu, w_ssm_up, w_out, g_ffn2, w_ffn2_in, w_ffn2_out, g_final, loss_target, m_w_ada, m_b_ada, m_g_ffn1, m_w_ffn1_in, m_w_ffn1_out, m_g_mix, m_w_in, m_pool_w, m_pool_b, m_pool_scale, m_w_pool_up, m_ssm_lam_re_log, m_ssm_lam_im, m_ssm_log_dt, m_ssm_b_re, m_ssm_b_im, m_ssm_c_re, m_ssm_c_im, m_ssm_d, m_w_glu, m_b_glu, m_w_ssm_up, m_w_out, m_g_ffn2, m_w_ffn2_in, m_w_ffn2_out, m_g_final, v_w_ada, v_b_ada, v_g_ffn1, v_w_ffn1_in, v_w_ffn1_out, v_g_mix, v_w_in, v_pool_w, v_pool_b, v_pool_scale, v_w_pool_up, v_ssm_lam_re_log, v_ssm_lam_im, v_ssm_log_dt, v_ssm_b_re, v_ssm_b_im, v_ssm_c_re, v_ssm_c_im, v_ssm_d, v_w_glu, v_b_glu, v_w_ssm_up, v_w_out, v_g_ffn2, v_w_ffn2_in, v_w_ffn2_out, v_g_final):
    args = locals()
    W = {n: args[n] for n in WEIGHT_ORDER}
    M = {n: args["m_" + n] for n in WEIGHT_ORDER}
    V = {n: args["v_" + n] for n in WEIGHT_ORDER}
    shapes = {n: W[n].shape for n in WEIGHT_ORDER}
    xt, tgt = x[0], loss_target[0]

    shard = dict(zip(GATHERED, _cast_shards([W[n][0] for n in GATHERED])))
    stacks = {}

    def gather(names):
        return _Gather([shard[n] for n in names])

    def gathered(names, results):
        stacks.update(zip(names, results))

    ffn1_w, ffn2_w = ('w_ffn1_in', 'w_ffn1_out'), ('w_ffn2_in', 'w_ffn2_out')
    mix_w = ('w_in', 'w_pool_up', 'w_glu', 'w_ssm_up', 'w_out')
    mod_cols, sc_all, *res = _ada_forward(c, W['w_ada'][0], b_ada.reshape(NDEV, -1), gather(ffn1_w))
    gathered(ffn1_w, res)
    win1 = stacks['w_ffn1_in'].reshape(2, 4, D_MODEL, FF_SHARD)
    wout1 = stacks['w_ffn1_out'].reshape(4, FF_SHARD, D_MODEL)
    prm = jnp.concatenate([mod_cols.reshape(9, D_MODEL), g_ffn1, g_mix, g_ffn2, g_final[None], jnp.zeros((3, D_MODEL), F32)], axis=0)
    pad512 = jnp.zeros((1, D_MODEL - 512), F32)
    mvec = jnp.concatenate([jnp.concatenate([pool_b, pad512], axis=1), jnp.concatenate([pool_scale, pad512], axis=1),
                            jnp.concatenate([ssm_d, pad512], axis=1), b_glu, jnp.zeros((4, D_MODEL), F32)], axis=0)
    log_dt_col = ssm_log_dt[0][:, None]
    coeffs = _ssm_params_forward(ssm_lam_re_log[0], ssm_lam_im[0], log_dt_col)
    srow = jnp.stack([t.reshape(N_STATE) for t in coeffs], axis=0)
    b_dense = jnp.stack([_block_diag_in(ssm_b_re[0]), _block_diag_in(ssm_b_im[0])], axis=0)
    c_dense = jnp.stack([_block_diag_out(ssm_c_re[0]), _block_diag_out(ssm_c_im[0])], axis=0)
    bb, ct = _ssm_dense_forward(srow, b_dense, c_dense)
    pw = pool_w[0]

    x1, f1, *res = _ffn_forward(xt, prm, win1, wout1, 0, ROW_G_FFN1, "ffn1_forward", gather(mix_w))
    gathered(mix_w, res)
    w_out_full = stacks['w_out'].reshape(D_MODEL, D_MODEL)
    res = _mixer_forward(x1, prm, stacks['w_in'], stacks['w_pool_up'], stacks['w_glu'], stacks['w_ssm_up'],
                         w_out_full, pw, mvec, srow, bb, ct, gather(ffn2_w))
    x2, mo, saved = res[0], res[1], res[2:11]
    gathered(ffn2_w, res[11:])
    win2 = stacks['w_ffn2_in'].reshape(2, 4, D_MODEL, FF_SHARD)
    wout2 = stacks['w_ffn2_out'].reshape(4, FF_SHARD, D_MODEL)
    x3, f3 = _ffn_forward(x2, prm, win2, wout2, 2, ROW_G_FFN2, "ffn2_forward")
    d3, fin = _final_loss(x3, tgt, prm)
    loss = lax.psum(fin[1, 0], ("x", "y", "c"))

    lands = {}

    def scatter(grads):
        names = list(grads)
        return _Scatter([grads[n][0] for n in names], [grads[n][1] for n in names], [W[n].shape[1:] for n in names])

    def scattered(grads, results):
        lands.update(zip(grads, results))

    parts3, dwin2, dwout2 = _ffn_backward(x2, d3, prm, win2, wout2, 2, ROW_G_FFN2, "ffn2_backward")
    d2, sums3 = _norm_backward(parts3, x2, d3, f3, prm, 2, ROW_G_FFN2, 0.5, "ffn2_norm_backward")
    g_ffn2_w = {'w_ffn2_in': (dwin2, _halves), 'w_ffn2_out': (dwout2.reshape(NDEV, -1, D_MODEL), _stacked)}
    res = _mixer_backward(d2, prm, saved, stacks['w_pool_up'], stacks['w_glu'], stacks['w_ssm_up'], w_out_full, pw, mvec,
                          srow, bb, ct, scatter(g_ffn2_w))
    dz, dwo, dwpu, dwglu, dwsu, dpw, dbb, dct, vsum, da = res[:10]
    scattered(g_ffn2_w, res[10:])
    parts2, dwin_mix = _mixer_in_backward(x1, dz, prm, stacks['w_in'])
    d1, sums2 = _norm_backward(parts2, x1, d2, mo, prm, 1, ROW_G_MIX, 1.0, "mixer_norm_backward")
    g_mix_w = {'w_in': (dwin_mix, _stacked), 'w_pool_up': (dwpu, _stacked), 'w_glu': (dwglu, _stacked),
               'w_ssm_up': (dwsu, _stacked), 'w_out': (dwo, _stacked)}
    parts1, dwin1, dwout1, *res = _ffn_backward(xt, d1, prm, win1, wout1, 0, ROW_G_FFN1, "ffn1_backward", scatter(g_mix_w))
    scattered(g_mix_w, res)
    g_ffn1_w = {'w_ffn1_in': (dwin1, _halves), 'w_ffn1_out': (dwout1.reshape(NDEV, -1, D_MODEL), _stacked)}
    d0, sums1, *res = _norm_backward(parts1, xt, d1, f1, prm, 0, ROW_G_FFN1, 0.5, "ffn1_norm_backward", scatter(g_ffn1_w))
    scattered(g_ffn1_w, res)

    db_dense, df_rows = _ssm_dense_backward(dbb, da, srow, b_dense)
    cot = [df_rows[r].reshape(32, 64) for r in (2, 3, 0, 1)]
    d_lrl, d_li, d_ldt = _ssm_params_backward(ssm_lam_re_log[0], ssm_lam_im[0], log_dt_col, cot)
    small_grads = {
        'g_ffn1': sums1[0], 'g_mix': sums2[0], 'g_ffn2': sums3[0], 'g_final': fin[0], 'pool_w': dpw,
        'pool_b': vsum[1, :512], 'pool_scale': vsum[0, :512], 'ssm_lam_re_log': d_lrl, 'ssm_lam_im': d_li,
        'ssm_log_dt': d_ldt, 'ssm_b_re': jnp.transpose(_diag_blocks(db_dense[0], SSM_GROUP, SSM_STATE), (0, 2, 1)),
        'ssm_b_im': jnp.transpose(_diag_blocks(db_dense[1], SSM_GROUP, SSM_STATE), (0, 2, 1)),
        'ssm_c_re': jnp.transpose(_diag_blocks(dct[0], SSM_STATE, SSM_GROUP), (0, 2, 1)),
        'ssm_c_im': jnp.transpose(_diag_blocks(dct[1], SSM_STATE, SSM_GROUP), (0, 2, 1)),
        'ssm_d': vsum[2, :512], 'b_glu': vsum[3],
    }
    dmod = jnp.concatenate([sums1[1:4], sums2[1:4], sums3[1:4]], axis=0).reshape(-1)
    total, landed = _allreduce_small(_pack_small(dmod, small_grads))
    dmod_cols = landed[:, :ADA_ROWS].reshape(NDEV, ADA_ROWS * 128)

    grad, delta, new_m, new_v = {}, {}, {}, {}
    for n in GATHERED:
        res = _adam_sharded(W[n][0], M[n][0], V[n][0], lands[n], "adam_" + n)
        grad[n], delta[n], new_m[n], new_v[n] = [r[None] for r in res]
    res = _adam_ada(W['w_ada'][0], M['w_ada'][0], V['w_ada'][0], sc_all, dmod_cols)
    grad['w_ada'], delta['w_ada'], new_m['w_ada'], new_v['w_ada'] = [r[None] for r in res]

    flat = lambda t: t.reshape(NDEV * PACK_ROWS, 128)
    small_w = flat(_pack_small(b_ada.reshape(-1), W))
    small_m = flat(_pack_small(m_b_ada.reshape(-1), M))
    small_v = flat(_pack_small(v_b_ada.reshape(-1), V))
    res = _adam_small(small_w, flat(total), small_m, small_v)
    for dst, packed in zip((grad, delta, new_m, new_v), (total, *res)):
        ada_vec, rest = _unpack_small(packed.reshape(NDEV, PACK_ROWS, 128), shapes)
        dst.update(rest)
        dst['b_ada'] = ada_vec.reshape(shapes['b_ada'])

    return (loss, d0[None], *[grad[n] for n in WEIGHT_ORDER], *[delta[n] for n in WEIGHT_ORDER],
            *[new_m[n] for n in WEIGHT_ORDER], *[new_v[n] for n in WEIGHT_ORDER])
```

```python
import functools

import jax
import jax.numpy as jnp
from jax import lax
from jax.experimental import pallas as pl
from jax.experimental.pallas import tpu as pltpu

F32 = jnp.float32
MXU_DTYPE = jnp.bfloat16
WIRE_DTYPE = jnp.bfloat16
SAVE_DTYPE = jnp.bfloat16

NDEV = 8
D_MODEL = 1024
D_FF = 2816
FF_SHARD = 2 * D_FF // NDEV
POOL_WIDTH = 512
POOL_GROUP = 128
SSM_WIDTH = 512
SSM_STATE = 64
SSM_GROUP = 16
SSM_BLOCKS = 4
SSM_BLOCK_STATE = 512
N_STATE = 2048
IN_WIDTH = 3072
EPS = 1e-6
ADAM_LR = 0.001
ADAM_B1 = 0.9
ADAM_B2 = 0.999
ADAM_EPS = 1e-08
ADAM_WD = 0.01
ADAM_STEP = 10

TM_FFN = 512
TM_MIX = 256
TM_MIX_BWD = 256
TM_EW = 512
SCAN_ROWS = 8
POOL_HALO = 16
VMEM_LIMIT = 60 * 1024 * 1024

ROW_G_FFN1, ROW_G_MIX, ROW_G_FFN2, ROW_G_FINAL = 9, 10, 11, 12
ROW_POOL_B, ROW_POOL_SCALE, ROW_SSM_D, ROW_B_GLU = 0, 1, 2, 3

SMALL_PARAMS = (
    ("g_ffn1", 1024), ("g_mix", 1024), ("g_ffn2", 1024), ("g_final", 1024), ("pool_w", 65536),
    ("pool_b", 512), ("pool_scale", 512), ("ssm_lam_re_log", 2048), ("ssm_lam_im", 2048),
    ("ssm_log_dt", 32), ("ssm_b_re", 32768), ("ssm_b_im", 32768), ("ssm_c_re", 32768),
    ("ssm_c_im", 32768), ("ssm_d", 512), ("b_glu", 1024),
)
SMALL_TOTAL = sum(n for _, n in SMALL_PARAMS)
ADA_ROWS = 9
REST_ROWS = 203
PACK_ROWS = 216
MESH = pl.DeviceIdType.MESH


def _mm(a, b):
    return jnp.dot(a.astype(MXU_DTYPE), b.astype(MXU_DTYPE), preferred_element_type=F32)


def _mm_nt(a, b):
    return lax.dot_general(a.astype(MXU_DTYPE), b.astype(MXU_DTYPE), (((1,), (1,)), ((), ())),
                           preferred_element_type=F32)


def _mm_tn(a, b):
    return lax.dot_general(a.astype(MXU_DTYPE), b.astype(MXU_DTYPE), (((0,), (0,)), ((), ())),
                           preferred_element_type=F32)


def _rms_scale(x):
    return lax.rsqrt(jnp.mean(x * x, axis=-1, keepdims=True) + EPS)


def _sigmoid(x):
    return jax.nn.sigmoid(x)


def _colsum(x):
    return jnp.sum(x, axis=0, keepdims=True)


def _row(ref, r):
    return ref[r:r + 1, :]


def _params(*sem):
    return pltpu.CompilerParams(dimension_semantics=sem, vmem_limit_bytes=VMEM_LIMIT)


def _resident(a):
    return pl.BlockSpec(a.shape, lambda *_: (0,) * a.ndim, pipeline_mode=pl.Buffered(1))


def _me():
    return lax.axis_index("x"), lax.axis_index("y"), lax.axis_index("c")


def _peer(rel):
    x, y, c = _me()
    px = 1 - x if rel & 4 else x
    py = 1 - y if rel & 2 else y
    pc = 1 - c if rel & 1 else c
    return (px, py, pc), 4 * px + 2 * py + pc


_HBM = pl.BlockSpec(memory_space=pl.ANY)


def _stacked(ref, p):
    return ref.at[p]


def _halves(ref, p):
    return ref.at[p // 4, p % 4]


class _Gather:
    def __init__(self, shards):
        self.operands = list(shards)
        self.n = len(shards)
        self.out_shape = [jax.ShapeDtypeStruct((NDEV,) + s.shape, s.dtype) for s in shards]
        self.scratch = [pltpu.SemaphoreType.DMA((7 * self.n,)), pltpu.SemaphoreType.DMA((7 * self.n,)),
                        pltpu.SemaphoreType.DMA((self.n,))]

    def plan(self, srcs, outs, sems):
        send_sems, recv_sems, local_sems = sems
        n = self.n
        x, y, c = _me()
        me = 4 * x + 2 * y + c
        here, sibling = (x, y, c), (x, y, 1 - c)
        chips = [(1 - x, y), (x, 1 - y), (1 - x, 1 - y)]

        def blk(px, py, pc):
            return 4 * px + 2 * py + pc

        def copy(a, k, block, to, src=None):
            return pltpu.make_async_remote_copy(
                src_ref=outs[a].at[block] if src is None else src, dst_ref=outs[a].at[block],
                send_sem=send_sems.at[7 * a + k], recv_sem=recv_sems.at[7 * a + k], device_id=to, device_id_type=MESH)

        def mine(a):
            return pltpu.make_async_copy(srcs[a], outs[a].at[me], local_sems.at[a])

        def first(a):
            return [copy(a, 0, me, sibling, src=srcs[a])] + [copy(a, 1 + j, me, (*chip, c), src=srcs[a])
                                                              for j, chip in enumerate(chips)]

        def start():
            for a in range(n):
                mine(a).start()
                for cp in first(a):
                    cp.start()

        def forward():
            for a in range(n):
                for j, chip in enumerate(chips):
                    copy(a, 1 + j, blk(*chip, c), here).wait_recv()
                    copy(a, 4 + j, blk(*chip, c), sibling).start()

        def finish():
            for a in range(n):
                copy(a, 0, blk(x, y, 1 - c), here).wait_recv()
                for j, chip in enumerate(chips):
                    copy(a, 4 + j, blk(*chip, 1 - c), here).wait_recv()
            for a in range(n):
                mine(a).wait()
                for cp in first(a):
                    cp.wait_send()
                for j, chip in enumerate(chips):
                    copy(a, 4 + j, blk(*chip, c), sibling).wait_send()

        return start, forward, finish


class _Scatter:
    def __init__(self, arrays, views, shard_shapes):
        self.operands = list(arrays)
        self.views = list(views)
        self.n = len(arrays)
        self.out_shape = [jax.ShapeDtypeStruct((NDEV,) + tuple(s), a.dtype) for s, a in zip(shard_shapes, arrays)]
        self.scratch = [pltpu.SemaphoreType.DMA((7 * self.n,)), pltpu.SemaphoreType.DMA((7 * self.n,)),
                        pltpu.SemaphoreType.DMA((self.n,))]

    def plan(self, srcs, outs, sems):
        send_sems, recv_sems, local_sems = sems
        n, views = self.n, self.views
        x, y, c = _me()
        me = 4 * x + 2 * y + c

        def mine(a):
            return pltpu.make_async_copy(views[a](srcs[a], me), outs[a].at[me], local_sems.at[a])

        def copy(a, rel, sending):
            to, p = _peer(rel)
            return pltpu.make_async_remote_copy(
                src_ref=views[a](srcs[a], p), dst_ref=outs[a].at[me if sending else p],
                send_sem=send_sems.at[7 * a + rel - 1], recv_sem=recv_sems.at[7 * a + rel - 1],
                device_id=to if sending else (x, y, c), device_id_type=MESH)

        def start():
            for a in range(n):
                mine(a).start()
            for rel in range(1, 8):
                for a in range(n):
                    copy(a, rel, True).start()

        def forward():
            pass

        def finish():
            for rel in range(1, 8):
                for a in range(n):
                    copy(a, rel, False).wait_recv()
            for rel in range(1, 8):
                for a in range(n):
                    copy(a, rel, True).wait_send()
            for a in range(n):
                mine(a).wait()

        return start, forward, finish


def _launch(body, name, out_shape, in_specs, out_specs, operands, scratch=(), grid=None, semantics=None,
            carry=None, steps=None):
    out_shape, in_specs, out_specs = list(out_shape), list(in_specs), list(out_specs)
    operands, scratch = list(operands), list(scratch)
    n_in, n_out, n_scr = len(in_specs), len(out_shape), len(scratch)
    kernel_body = body
    if carry is not None:
        k = carry.n

        def kernel_body(*refs):
            ins, cin = refs[:n_in], refs[n_in:n_in + k]
            outs, cout = refs[n_in + k:n_in + k + n_out], refs[n_in + k + n_out:n_in + 2 * k + n_out]
            rest = refs[n_in + 2 * k + n_out:]
            scr, csem = rest[:n_scr], rest[n_scr:]
            start, forward, finish = carry.plan(cin, cout, csem)
            if steps is None:
                start()
                body(*ins, *outs, *scr)
                forward()
                finish()
            else:
                pl.when(steps()[0])(start)
                body(*ins, *outs, *scr)
                pl.when(steps()[1])(forward)
                pl.when(steps()[2])(finish)

        in_specs += [_HBM] * k
        out_shape += carry.out_shape
        out_specs += [_HBM] * k
        operands += carry.operands
        scratch += carry.scratch
    kwargs = {} if grid is None else {"grid": grid}
    params = pltpu.CompilerParams(vmem_limit_bytes=VMEM_LIMIT) if semantics is None else _params(*semantics)
    return pl.pallas_call(kernel_body, name=name, out_shape=out_shape, in_specs=in_specs, out_specs=out_specs,
                          scratch_shapes=scratch, compiler_params=params, **kwargs)(*operands)


def _grid_steps(nt):
    def steps():
        i = pl.program_id(0)
        return i == 0, i == nt // 2, i == nt - 1
    return steps


def _cast_shards(shards):
    n = len(shards)

    def body(*refs):
        for a in range(n):
            refs[n + a][...] = refs[a][...].astype(WIRE_DTYPE)

    return pl.pallas_call(body, name="cast_shards",
                          out_shape=[jax.ShapeDtypeStruct(s.shape, WIRE_DTYPE) for s in shards],
                          compiler_params=pltpu.CompilerParams(vmem_limit_bytes=VMEM_LIMIT))(*shards)


def _ada_forward(c_row, w_ada, b_ada8, carry):
    cols = w_ada.shape[1]

    def body(c_ref, w_ref, b_ref, mod_ref, sc_ref, c_all, send_buf, recv_buf, send1, recv1, send2, recv2):
        x, y, c = _me()
        me = 4 * x + 2 * y + c
        rowi = lax.broadcasted_iota(jnp.int32, (8, D_MODEL), 0)
        c_all[me] = jnp.broadcast_to(c_ref[...], (8, D_MODEL))
        copies = []
        for rel in range(1, 8):
            to, _ = _peer(rel)
            cp = pltpu.make_async_remote_copy(src_ref=c_all.at[me], dst_ref=c_all.at[me], send_sem=send1.at[rel - 1],
                                              recv_sem=recv1.at[rel - 1], device_id=to, device_id_type=MESH)
            cp.start()
            copies.append(cp)
        for rel in range(1, 8):
            _, p = _peer(rel)
            pltpu.make_async_remote_copy(src_ref=c_all.at[p], dst_ref=c_all.at[p], send_sem=send1.at[rel - 1],
                                         recv_sem=recv1.at[rel - 1], device_id=(x, y, c), device_id_type=MESH).wait_recv()
        for cp in copies:
            cp.wait_send()
        cmat = jnp.zeros((8, D_MODEL), F32)
        for b in range(8):
            cmat = jnp.where(rowi == b, c_all[b], cmat)
        sc = cmat * _sigmoid(cmat)
        sc_ref[...] = sc
        modcols = _mm(sc, w_ref[...]) + b_ref[pl.ds(me, 1), :]
        for b in range(8):
            send_buf[b] = jnp.broadcast_to(modcols[b:b + 1, :], (8, cols))
        recv_buf[me] = send_buf[me]
        copies = []
        for rel in range(1, 8):
            to, p = _peer(rel)
            cp = pltpu.make_async_remote_copy(src_ref=send_buf.at[p], dst_ref=recv_buf.at[me], send_sem=send2.at[rel - 1],
                                              recv_sem=recv2.at[rel - 1], device_id=to, device_id_type=MESH)
            cp.start()
            copies.append(cp)
        for rel in range(1, 8):
            _, p = _peer(rel)
            pltpu.make_async_remote_copy(src_ref=send_buf.at[p], dst_ref=recv_buf.at[p], send_sem=send2.at[rel - 1],
                                         recv_sem=recv2.at[rel - 1], device_id=(x, y, c), device_id_type=MESH).wait_recv()
        for cp in copies:
            cp.wait_send()
        rowc = lax.broadcasted_iota(jnp.int32, (8, cols), 0)
        out = jnp.zeros((8, cols), F32)
        for k in range(8):
            out = jnp.where(rowc == k, recv_buf[k], out)
        mod_ref[...] = out

    return _launch(
        body, "ada_forward",
        out_shape=[jax.ShapeDtypeStruct((8, cols), F32), jax.ShapeDtypeStruct((8, D_MODEL), F32)],
        in_specs=[pl.BlockSpec(memory_space=pltpu.VMEM)] * 3,
        out_specs=[pl.BlockSpec(memory_space=pltpu.VMEM)] * 2,
        operands=(c_row, w_ada, b_ada8),
        scratch=[pltpu.VMEM((8, 8, D_MODEL), F32), pltpu.VMEM((8, 8, cols), F32), pltpu.VMEM((8, 8, cols), F32)]
        + [pltpu.SemaphoreType.DMA((7,))] * 4,
        carry=carry)


def _allreduce_small(pack):
    rows = pack.shape[1]

    def body(pack_ref, total_ref, land_ref, send1, recv1, send2, recv2):
        x, y, c = _me()
        me = 4 * x + 2 * y + c
        land_ref[me] = pack_ref[me]
        copies = []
        for rel in range(1, 8):
            to, p = _peer(rel)
            cp = pltpu.make_async_remote_copy(src_ref=pack_ref.at[p], dst_ref=land_ref.at[me], send_sem=send1.at[rel - 1],
                                              recv_sem=recv1.at[rel - 1], device_id=to, device_id_type=MESH)
            cp.start()
            copies.append(cp)
        for rel in range(1, 8):
            _, p = _peer(rel)
            pltpu.make_async_remote_copy(src_ref=pack_ref.at[p], dst_ref=land_ref.at[p], send_sem=send1.at[rel - 1],
                                         recv_sem=recv1.at[rel - 1], device_id=(x, y, c), device_id_type=MESH).wait_recv()
        for cp in copies:
            cp.wait_send()
        acc = land_ref[0]
        for b in range(1, 8):
            acc = acc + land_ref[b]
        total_ref[me] = acc
        copies = []
        for rel in range(1, 8):
            to, _ = _peer(rel)
            cp = pltpu.make_async_remote_copy(src_ref=total_ref.at[me], dst_ref=total_ref.at[me], send_sem=send2.at[rel - 1],
                                              recv_sem=recv2.at[rel - 1], device_id=to, device_id_type=MESH)
            cp.start()
            copies.append(cp)
        for rel in range(1, 8):
            _, p = _peer(rel)
            pltpu.make_async_remote_copy(src_ref=total_ref.at[p], dst_ref=total_ref.at[p], send_sem=send2.at[rel - 1],
                                         recv_sem=recv2.at[rel - 1], device_id=(x, y, c), device_id_type=MESH).wait_recv()
        for cp in copies:
            cp.wait_send()

    return pl.pallas_call(
        body, name="allreduce_small",
        out_shape=[jax.ShapeDtypeStruct((8, rows, 128), F32), jax.ShapeDtypeStruct((8, rows, 128), F32)],
        in_specs=[pl.BlockSpec(memory_space=pltpu.VMEM)],
        out_specs=[pl.BlockSpec(memory_space=pltpu.VMEM)] * 2,
        scratch_shapes=[pltpu.SemaphoreType.DMA((7,))] * 4,
        compiler_params=pltpu.CompilerParams(vmem_limit_bytes=VMEM_LIMIT),
    )(pack)


def _modulated(x, prm_ref, sub, g_row):
    shift, scale = _row(prm_ref, 3 * sub), _row(prm_ref, 3 * sub + 1)
    g = _row(prm_ref, g_row)
    r = _rms_scale(x)
    n0 = x * r
    return (n0 * g) * (1.0 + scale) + shift, r, n0


def _ffn_forward(x, prm, win, wout, sub, g_row, name, carry=None):
    T = x.shape[0]
    tm = min(T, TM_FFN)

    def body(x_ref, prm_ref, win_ref, wout_ref, xo_ref, f_ref, ab_ref):
        xv = x_ref[...]
        h, _, _ = _modulated(xv, prm_ref, sub, g_row)
        hb = h.astype(MXU_DTYPE)
        acc = None
        for j in range(4):
            a = _mm(hb, win_ref[0, j])
            b = _mm(hb, win_ref[1, j])
            ab_ref[0, j] = a.astype(SAVE_DTYPE)
            ab_ref[1, j] = b.astype(SAVE_DTYPE)
            s = (a * _sigmoid(a)) * b
            t = _mm(s, wout_ref[j])
            acc = t if acc is None else acc + t
        f_ref[...] = acc.astype(SAVE_DTYPE)
        xo_ref[...] = xv + (0.5 * _row(prm_ref, 3 * sub + 2)) * acc

    tok = pl.BlockSpec((tm, D_MODEL), lambda i: (i, 0))
    return _launch(
        body, name, grid=(T // tm,), semantics=("arbitrary",),
        out_shape=[jax.ShapeDtypeStruct((T, D_MODEL), F32), jax.ShapeDtypeStruct((T, D_MODEL), SAVE_DTYPE),
                   jax.ShapeDtypeStruct((2, 4, T, FF_SHARD), SAVE_DTYPE)],
        in_specs=[tok, _resident(prm), _resident(win), _resident(wout)],
        out_specs=[tok, tok, pl.BlockSpec((2, 4, tm, FF_SHARD), lambda i: (0, 0, i, 0))],
        operands=(x, prm, win, wout), carry=carry, steps=_grid_steps(T // tm))


def _ffn_backward(x, d, ab, prm, win, wout, sub, g_row, name, carry=None):
    T = x.shape[0]
    tm = min(T, TM_FFN)
    nt = T // tm

    def body(x_ref, d_ref, ab_ref, prm_ref, win_ref, wout_ref, dh_ref, dwin_ref, dwout_ref, acc_in, acc_out):
        i = pl.program_id(1)

        @pl.when(i == 0)
        def _():
            acc_in[...] = jnp.zeros_like(acc_in)
            acc_out[...] = jnp.zeros_like(acc_out)

        h, _, _ = _modulated(x_ref[...], prm_ref, sub, g_row)
        hb = h.astype(MXU_DTYPE)
        wa, wb, wo = win_ref[0, 0], win_ref[1, 0], wout_ref[0]
        a = ab_ref[0, 0].astype(F32)
        b = ab_ref[1, 0].astype(F32)
        sg = _sigmoid(a)
        si = a * sg
        dfs = ((0.5 * _row(prm_ref, 3 * sub + 2)) * d_ref[...]).astype(MXU_DTYPE)
        ds = _mm_nt(dfs, wo)
        acc_out[...] += _mm_tn(si * b, dfs)
        da = ds * b * (sg * (1.0 + a * (1.0 - sg)))
        db = ds * si
        acc_in[0] += _mm_tn(hb, da)
        acc_in[1] += _mm_tn(hb, db)
        dh_ref[0] = (_mm_nt(da, wa) + _mm_nt(db, wb)).astype(SAVE_DTYPE)

        @pl.when(i == nt - 1)
        def _():
            dwin_ref[0, 0] = acc_in[0].astype(WIRE_DTYPE)
            dwin_ref[1, 0] = acc_in[1].astype(WIRE_DTYPE)
            dwout_ref[0] = acc_out[...].astype(WIRE_DTYPE)

    def steps():
        j, i = pl.program_id(0), pl.program_id(1)
        return (j == 0) & (i == 0), (j == 2) & (i == 0), (j == 3) & (i == nt - 1)

    tok = pl.BlockSpec((tm, D_MODEL), lambda j, i: (i, 0))
    return _launch(
        body, name, grid=(4, nt), semantics=("arbitrary", "arbitrary"),
        out_shape=[jax.ShapeDtypeStruct((4, T, D_MODEL), SAVE_DTYPE),
                   jax.ShapeDtypeStruct(win.shape, WIRE_DTYPE), jax.ShapeDtypeStruct(wout.shape, WIRE_DTYPE)],
        in_specs=[tok, tok, pl.BlockSpec((2, 1, tm, FF_SHARD), lambda j, i: (0, j, i, 0)), _resident(prm),
                  pl.BlockSpec((2, 1, D_MODEL, FF_SHARD), lambda j, i: (0, j, 0, 0)),
                  pl.BlockSpec((1, FF_SHARD, D_MODEL), lambda j, i: (j, 0, 0))],
        out_specs=[pl.BlockSpec((1, tm, D_MODEL), lambda j, i: (j, i, 0)),
                   pl.BlockSpec((2, 1, D_MODEL, FF_SHARD), lambda j, i: (0, j, 0, 0)),
                   pl.BlockSpec((1, FF_SHARD, D_MODEL), lambda j, i: (j, 0, 0))],
        operands=(x, d, ab, prm, win, wout),
        scratch=[pltpu.VMEM((2, D_MODEL, FF_SHARD), F32), pltpu.VMEM((FF_SHARD, D_MODEL), F32)],
        carry=carry, steps=steps)


def _norm_backward(parts, x, d, f, prm, sub, g_row, gate_coef, name, carry=None):
    T = x.shape[0]
    tm = min(T, TM_EW)
    P = parts.shape[0]

    def body(p_ref, x_ref, d_ref, f_ref, prm_ref, dx_ref, sums_ref):
        i = pl.program_id(0)
        dh = p_ref[0].astype(F32)
        for k in range(1, P):
            dh = dh + p_ref[k].astype(F32)
        xv, dv = x_ref[...], d_ref[...]
        scale, g = _row(prm_ref, 3 * sub + 1), _row(prm_ref, g_row)
        r = _rms_scale(xv)
        n0 = xv * r
        dn = dh * (1.0 + scale)
        dn0 = dn * g
        dx_ref[...] = dv + r * (dn0 - n0 * jnp.mean(dn0 * n0, axis=-1, keepdims=True))
        upd = jnp.concatenate([_colsum(dn * n0), _colsum(dh), _colsum(dh * (n0 * g)),
                               gate_coef * _colsum(dv * f_ref[...].astype(F32)), jnp.zeros((4, D_MODEL), F32)], axis=0)

        @pl.when(i == 0)
        def _():
            sums_ref[...] = upd

        @pl.when(i > 0)
        def _():
            sums_ref[...] += upd

    tok = pl.BlockSpec((tm, D_MODEL), lambda i: (i, 0))
    return _launch(
        body, name, grid=(T // tm,), semantics=("arbitrary",),
        out_shape=[jax.ShapeDtypeStruct((T, D_MODEL), F32), jax.ShapeDtypeStruct((8, D_MODEL), F32)],
        in_specs=[pl.BlockSpec((P, tm, D_MODEL), lambda i: (0, i, 0)), tok, tok, tok, _resident(prm)],
        out_specs=[tok, pl.BlockSpec((8, D_MODEL), lambda i: (0, 0))],
        operands=(parts, x, d, f, prm), carry=carry, steps=_grid_steps(T // tm))


def _final_loss(x, target, prm):
    T = x.shape[0]
    tm = min(T, TM_EW)

    def body(x_ref, t_ref, prm_ref, dx_ref, sums_ref):
        i = pl.program_id(0)
        xv = x_ref[...]
        g = _row(prm_ref, ROW_G_FINAL)
        r = _rms_scale(xv)
        n0 = xv * r
        err = n0 * g - t_ref[...]
        dy = err / float(D_MODEL)
        dn0 = dy * g
        dx_ref[...] = r * (dn0 - n0 * jnp.mean(dn0 * n0, axis=-1, keepdims=True))
        loss = 0.5 * jnp.sum(jnp.mean(err * err, axis=-1, keepdims=True), axis=0, keepdims=True)
        upd = jnp.concatenate([_colsum(dy * n0), jnp.broadcast_to(loss, (1, D_MODEL)), jnp.zeros((6, D_MODEL), F32)], axis=0)

        @pl.when(i == 0)
        def _():
            sums_ref[...] = upd

        @pl.when(i > 0)
        def _():
            sums_ref[...] += upd

    tok = pl.BlockSpec((tm, D_MODEL), lambda i: (i, 0))
    return pl.pallas_call(
        body, name="final_loss", grid=(T // tm,),
        out_shape=[jax.ShapeDtypeStruct((T, D_MODEL), F32), jax.ShapeDtypeStruct((8, D_MODEL), F32)],
        in_specs=[tok, tok, pl.BlockSpec(prm.shape, lambda i: (0, 0))],
        out_specs=[tok, pl.BlockSpec((8, D_MODEL), lambda i: (0, 0))],
        compiler_params=_params("arbitrary"),
    )(x, target, prm)


def _ssm_discretise(lam_re_log, lam_im, log_dt):
    lr = -jnp.exp(lam_re_log)
    dt = jnp.exp(log_dt)
    mag = jnp.exp(lr * dt)
    ang = lam_im * dt
    ab_re = mag * jnp.cos(ang)
    ab_im = mag * jnp.sin(ang)
    num_re = ab_re - 1.0
    num_im = ab_im
    den = lr * lr + lam_im * lam_im
    f_re = (num_re * lr + num_im * lam_im) / den
    f_im = (num_im * lr - num_re * lam_im) / den
    return ab_re, ab_im, f_re, f_im


def _ssm_params_forward(lam_re_log, lam_im, log_dt):
    def body(a_ref, b_ref, c_ref, o0, o1, o2, o3):
        outs = _ssm_discretise(a_ref[...], b_ref[...], c_ref[...])
        for o, v in zip((o0, o1, o2, o3), outs):
            o[...] = v

    return pl.pallas_call(body, name="ssm_params_forward",
                          out_shape=[jax.ShapeDtypeStruct(lam_im.shape, F32)] * 4)(lam_re_log, lam_im, log_dt)


def _ssm_params_backward(lam_re_log, lam_im, log_dt, cot):
    def body(a_ref, b_ref, c_ref, g0, g1, g2, g3, o0, o1, o2):
        _, vjp = jax.vjp(_ssm_discretise, a_ref[...], b_ref[...], c_ref[...])
        d0, d1, d2 = vjp((g0[...], g1[...], g2[...], g3[...]))
        o0[...] = d0
        o1[...] = d1
        o2[...] = d2

    return pl.pallas_call(
        body, name="ssm_params_backward",
        out_shape=[jax.ShapeDtypeStruct(lam_im.shape, F32), jax.ShapeDtypeStruct(lam_im.shape, F32),
                   jax.ShapeDtypeStruct(log_dt.shape, F32)])(lam_re_log, lam_im, log_dt, *cot)


def _ssm_dense_forward(srow, b_dense, c_dense):
    def body(srow_ref, bd_ref, cd_ref, bb_ref, ct_ref):
        for j in range(SSM_BLOCKS):
            lanes = slice(j * SSM_BLOCK_STATE, (j + 1) * SSM_BLOCK_STATE)
            f_re, f_im = srow_ref[2:3, lanes], srow_ref[3:4, lanes]
            bb_ref[0, j] = (f_re * bd_ref[0, j] - f_im * bd_ref[1, j]).astype(MXU_DTYPE)
            bb_ref[1, j] = (f_re * bd_ref[1, j] + f_im * bd_ref[0, j]).astype(MXU_DTYPE)
            ct_ref[0, j] = cd_ref[0, j].astype(MXU_DTYPE)
            ct_ref[1, j] = cd_ref[1, j].astype(MXU_DTYPE)

    return pl.pallas_call(body, name="ssm_dense_forward",
                          out_shape=[jax.ShapeDtypeStruct(b_dense.shape, MXU_DTYPE),
                                     jax.ShapeDtypeStruct(c_dense.shape, MXU_DTYPE)],
                          compiler_params=pltpu.CompilerParams(vmem_limit_bytes=VMEM_LIMIT))(srow, b_dense, c_dense)


def _cmul(p, q):
    return p[0] * q[0] - p[1] * q[1], p[0] * q[1] + p[1] * q[0]


def _scan_coefficients(ar, ai, reverse):
    n = ar.shape[1]
    p = {1: (ar, ai)}
    p[2] = _cmul(p[1], p[1])
    p[3] = _cmul(p[2], p[1])
    p[4] = _cmul(p[2], p[2])
    p[5] = _cmul(p[4], p[1])
    p[6] = _cmul(p[4], p[2])
    p[7] = _cmul(p[4], p[3])
    p[8] = _cmul(p[4], p[4])
    rowi = lax.broadcasted_iota(jnp.int32, (SCAN_ROWS, n), 0)
    tiles = []
    for dstep in (1, 2, 4):
        keep = (rowi < SCAN_ROWS - dstep) if reverse else (rowi >= dstep)
        for part in p[dstep]:
            tiles.append(jnp.where(keep, jnp.broadcast_to(part, (SCAN_ROWS, n)), 0.0))
    for comp in (0, 1):
        t = jnp.zeros((SCAN_ROWS, n), F32)
        for rr in range(SCAN_ROWS):
            power = SCAN_ROWS - rr if reverse else rr + 1
            t = jnp.where(rowi == rr, jnp.broadcast_to(p[power][comp], (SCAN_ROWS, n)), t)
        tiles.append(t)
    return tiles


def _load_stack(stack_hbm, dst, sems, base):
    cols = stack_hbm.shape[2]
    cps = [pltpu.make_async_copy(stack_hbm.at[k], dst.at[:, pl.ds(k * cols, cols)], sems.at[base + k])
           for k in range(NDEV)]
    for cp in cps:
        cp.start()
    return cps


def _window_lanes():
    lane = lax.broadcasted_iota(jnp.int32, (1, POOL_WIDTH), 1)
    return jnp.where(lane < 128, 2.0, jnp.where(lane < 256, 4.0, jnp.where(lane < 384, 8.0, 16.0)))


def _gelu(y):
    return 0.5 * y * (1.0 + lax.erf(y * 0.7071067811865476))


def _gelu_grad(y):
    return 0.5 * (1.0 + lax.erf(y * 0.7071067811865476)) + y * jnp.exp(-0.5 * y * y) * 0.3989422804014327


def _mixer_forward(x, prm, w_in_s, w_pu_s, w_glu_s, w_su_s, w_out, pool_w, mvec, srow, bb, ct, carry=None):
    T = x.shape[0]
    tm = min(T, TM_MIX)
    nt = T // tm
    n_tiles = tm // SCAN_ROWS

    def body(x_ref, prm_ref, w_in_h, w_pu_h, w_glu_h, w_su_h, w_out_h, pw_ref, mv_ref, srow_ref, bb, ct,
             x2_ref, mo_ref, z_ref, sre_ref, sim_ref, zp_ref, q_ref, yp_ref, yss_ref, vg_ref, ys_ref,
             w_in, w_pu, w_glu, w_su, w_o, coef, carry, hist, bu, sems):
        i = pl.program_id(0)

        @pl.when(i == 0)
        def _():
            cps = (_load_stack(w_in_h, w_in, sems, 0) + _load_stack(w_pu_h, w_pu, sems, 8)
                   + _load_stack(w_glu_h, w_glu, sems, 16) + _load_stack(w_su_h, w_su, sems, 24))
            cps.append(pltpu.make_async_copy(w_out_h, w_o, sems.at[32]))
            cps[-1].start()
            for j in range(SSM_BLOCKS):
                lanes = slice(j * SSM_BLOCK_STATE, (j + 1) * SSM_BLOCK_STATE)
                for k, t in enumerate(_scan_coefficients(srow_ref[0:1, lanes], srow_ref[1:2, lanes], False)):
                    coef[j, k] = t
            carry[...] = jnp.zeros_like(carry)
            hist[...] = jnp.zeros_like(hist)
            for cp in cps:
                cp.wait()

        xv = x_ref[...]
        h, _, _ = _modulated(xv, prm_ref, 1, ROW_G_MIX)
        z = _mm(h, w_in[...])
        z_ref[...] = z.astype(SAVE_DTYPE)
        u_pool, u_ssm = z[:, 0:512], z[:, 512:1024]
        gl_pool, gl_ssm = z[:, 1024:2048], z[:, 2048:3072]

        ext = jnp.concatenate([hist[...], u_pool], axis=0)
        w2 = ext + pltpu.roll(ext, 1, 0)
        w4 = w2[:, 128:] + pltpu.roll(w2[:, 128:], 2, 0)
        w8 = w4[:, 128:] + pltpu.roll(w4[:, 128:], 4, 0)
        w16 = w8[:, 128:] + pltpu.roll(w8[:, 128:], 8, 0)
        wsum = jnp.concatenate([w2[POOL_HALO:, :128], w4[POOL_HALO:, :128], w8[POOL_HALO:, :128], w16[POOL_HALO:]], axis=1)
        hist[...] = u_pool[tm - POOL_HALO:, :]
        t1 = (lax.broadcasted_iota(jnp.int32, (tm, 1), 0) + (i * tm + 1)).astype(F32)
        zp = wsum / jnp.minimum(t1, _window_lanes()) - u_pool
        zp_ref[...] = zp.astype(SAVE_DTYPE)
        q = jnp.concatenate([_mm(zp[:, k * 128:(k + 1) * 128], pw_ref[k]) for k in range(4)], axis=1)
        q = q + mv_ref[ROW_POOL_B:ROW_POOL_B + 1, 0:512]
        q_ref[...] = q.astype(SAVE_DTYPE)
        y_pool = _mm(q * mv_ref[ROW_POOL_SCALE:ROW_POOL_SCALE + 1, 0:512], w_pu[...])
        yp_ref[...] = y_pool.astype(SAVE_DTYPE)

        y_blocks = []
        for j in range(SSM_BLOCKS):
            lanes = pl.ds(j * SSM_BLOCK_STATE, SSM_BLOCK_STATE)
            ub = u_ssm[:, j * 128:(j + 1) * 128].astype(MXU_DTYPE)
            bu[0] = _mm(ub, bb[0, j])
            bu[1] = _mm(ub, bb[1, j])
            a1r, a1i, a2r, a2i, a4r, a4i, pr, pi = [coef[j, k] for k in range(8)]

            def step(tt, c, lanes=lanes, a1r=a1r, a1i=a1i, a2r=a2r, a2i=a2i, a4r=a4r, a4i=a4i, pr=pr, pi=pi):
                cr, ci = c
                rows = pl.ds(pl.multiple_of(tt * SCAN_ROWS, SCAN_ROWS), SCAN_ROWS)
                xr, xi = bu[0, rows, :], bu[1, rows, :]
                for dstep, kr, ki in ((1, a1r, a1i), (2, a2r, a2i), (4, a4r, a4i)):
                    sr, si = pltpu.roll(xr, dstep, 0), pltpu.roll(xi, dstep, 0)
                    xr, xi = xr + kr * sr - ki * si, xi + kr * si + ki * sr
                xr, xi = xr + pr * cr - pi * ci, xi + pr * ci + pi * cr
                sre_ref[rows, lanes] = xr
                sim_ref[rows, lanes] = xi
                return (jnp.broadcast_to(xr[SCAN_ROWS - 1:SCAN_ROWS, :], xr.shape),
                        jnp.broadcast_to(xi[SCAN_ROWS - 1:SCAN_ROWS, :], xi.shape))

            cr, ci = lax.fori_loop(0, n_tiles, step, (carry[j, 0], carry[j, 1]))
            carry[j, 0] = cr
            carry[j, 1] = ci
            y_blocks.append(_mm(sre_ref[:, lanes], ct[0, j]) - _mm(sim_ref[:, lanes], ct[1, j]))
        yss = jnp.concatenate(y_blocks, axis=1) + mv_ref[ROW_SSM_D:ROW_SSM_D + 1, 0:512] * u_ssm
        yss_ref[...] = yss.astype(SAVE_DTYPE)
        vg = _mm(_gelu(yss), w_glu[...]) + mv_ref[ROW_B_GLU:ROW_B_GLU + 1, :]
        vg_ref[...] = vg.astype(SAVE_DTYPE)
        y_ssm = _mm(vg[:, 0:512] * _sigmoid(vg[:, 512:1024]), w_su[...])
        ys_ref[...] = y_ssm.astype(SAVE_DTYPE)

        merged = _sigmoid(gl_pool) * y_pool + _sigmoid(gl_ssm) * y_ssm
        mo = _mm(merged, w_o[...])
        mo_ref[...] = mo.astype(SAVE_DTYPE)
        x2_ref[...] = xv + _row(prm_ref, 5) * mo

    def tok(width):
        return pl.BlockSpec((tm, width), lambda i: (i, 0))

    hbm = _HBM
    widths = (D_MODEL, D_MODEL, IN_WIDTH, N_STATE, N_STATE, 512, 512, D_MODEL, 512, D_MODEL, D_MODEL)
    dtypes = (F32, SAVE_DTYPE, SAVE_DTYPE, F32, F32) + (SAVE_DTYPE,) * 6
    return _launch(
        body, "mixer_forward", grid=(nt,), semantics=("arbitrary",), carry=carry, steps=_grid_steps(nt),
        out_shape=[jax.ShapeDtypeStruct((T, w), dt) for w, dt in zip(widths, dtypes)],
        in_specs=[tok(D_MODEL), _resident(prm), hbm, hbm, hbm, hbm, hbm, _resident(pool_w), _resident(mvec),
                  _resident(srow), _resident(bb), _resident(ct)],
        out_specs=[tok(w) for w in widths],
        operands=(x, prm, w_in_s, w_pu_s, w_glu_s, w_su_s, w_out, pool_w, mvec, srow, bb, ct),
        scratch=[
            pltpu.VMEM((D_MODEL, IN_WIDTH), MXU_DTYPE), pltpu.VMEM((512, D_MODEL), MXU_DTYPE),
            pltpu.VMEM((512, D_MODEL), MXU_DTYPE), pltpu.VMEM((512, D_MODEL), MXU_DTYPE),
            pltpu.VMEM((D_MODEL, D_MODEL), MXU_DTYPE),
            pltpu.VMEM((SSM_BLOCKS, 8, SCAN_ROWS, SSM_BLOCK_STATE), F32),
            pltpu.VMEM((SSM_BLOCKS, 2, SCAN_ROWS, SSM_BLOCK_STATE), F32),
            pltpu.VMEM((POOL_HALO, POOL_WIDTH), F32),
            pltpu.VMEM((2, tm, SSM_BLOCK_STATE), F32),
            pltpu.SemaphoreType.DMA((33,)),
        ])


def _mixer_backward(d2, prm, saved, w_pu_s, w_glu_s, w_su_s, w_out, pool_w, mvec, srow, bb, ct, carry=None):
    z, s_re, s_im, zp, q, y_pool, yss, vg, y_ssm = saved
    T = d2.shape[0]
    tm = min(T, TM_MIX_BWD)
    nt = T // tm
    n_tiles = tm // SCAN_ROWS

    def body(d_ref, prm_ref, z_ref, sre_ref, sim_ref, zp_ref, q_ref, yp_ref, yss_ref, vg_ref, ys_ref,
             w_pu_h, w_glu_h, w_su_h, w_out_h, pw_ref, mv_ref, srow_ref, bb, ct,
             dz_ref, dwo_h, dwpu_h, dwglu_h, dwsu_h, dpw_h, dbb_h, dct_h, vsum_h, da_h,
             w_pu, w_glu, w_su, w_o, pwb, coef, carry, hist, dre, lam,
             a_wo, a_wpu, a_wglu, a_wsu, a_pw, a_bb, a_ct, a_vs, a_da, st_wo, st_up, sems):
        i = pl.program_id(0)
        tile = nt - 1 - i

        @pl.when(i == 0)
        def _():
            cps = (_load_stack(w_pu_h, w_pu, sems, 0) + _load_stack(w_glu_h, w_glu, sems, 8)
                   + _load_stack(w_su_h, w_su, sems, 16))
            cps.append(pltpu.make_async_copy(w_out_h, w_o, sems.at[24]))
            cps[-1].start()
            pwb[...] = pw_ref[...].astype(MXU_DTYPE)
            for j in range(SSM_BLOCKS):
                lanes = slice(j * SSM_BLOCK_STATE, (j + 1) * SSM_BLOCK_STATE)
                for k, t in enumerate(_scan_coefficients(srow_ref[0:1, lanes], srow_ref[1:2, lanes], True)):
                    coef[j, k] = t
            for acc in (carry, hist, a_wo, a_wpu, a_wglu, a_wsu, a_pw, a_bb, a_ct, a_vs, a_da):
                acc[...] = jnp.zeros_like(acc)
            for cp in cps:
                cp.wait()

        dv = d_ref[...]
        zt = z_ref[...].astype(F32)
        u_ssm, gl_pool, gl_ssm = zt[:, 512:1024], zt[:, 1024:2048], zt[:, 2048:3072]
        y_p, y_s = yp_ref[...].astype(F32), ys_ref[...].astype(F32)
        sgp, sgs = _sigmoid(gl_pool), _sigmoid(gl_ssm)
        dmo = (_row(prm_ref, 5) * dv).astype(MXU_DTYPE)
        a_wo[...] += _mm_tn(sgp * y_p + sgs * y_s, dmo)
        dmerged = _mm_nt(dmo, w_o[...])
        dy_pool = dmerged * sgp
        dgl_pool = dmerged * y_p * (sgp * (1.0 - sgp))
        dy_ssm = dmerged * sgs
        dgl_ssm = dmerged * y_s * (sgs * (1.0 - sgs))

        scale = mv_ref[ROW_POOL_SCALE:ROW_POOL_SCALE + 1, 0:512]
        qv, zpv = q_ref[...].astype(F32), zp_ref[...]
        a_wpu[...] += _mm_tn(qv * scale, dy_pool)
        dp = _mm_nt(dy_pool, w_pu[...])
        dq = dp * scale
        a_vs[0:1, 0:512] += _colsum(dp * qv)
        a_vs[1:2, 0:512] += _colsum(dq)
        dzp_blocks = []
        for k in range(4):
            lanes = slice(k * 128, (k + 1) * 128)
            dzp_blocks.append(_mm_nt(dq[:, lanes], pwb[k]))
            a_pw[k] += _mm_tn(zpv[:, lanes], dq[:, lanes])
        dzp = jnp.concatenate(dzp_blocks, axis=1)
        t1 = (lax.broadcasted_iota(jnp.int32, (tm, 1), 0) + (tile * tm + 1)).astype(F32)
        gs = dzp / jnp.minimum(t1, _window_lanes())
        n_ext = tm + POOL_HALO
        ext = jnp.concatenate([gs, hist[...]], axis=0)
        v2 = ext + pltpu.roll(ext, n_ext - 1, 0)
        v4 = v2[:, 128:] + pltpu.roll(v2[:, 128:], n_ext - 2, 0)
        v8 = v4[:, 128:] + pltpu.roll(v4[:, 128:], n_ext - 4, 0)
        v16 = v8[:, 128:] + pltpu.roll(v8[:, 128:], n_ext - 8, 0)
        msum = jnp.concatenate([v2[:tm, :128], v4[:tm, :128], v8[:tm, :128], v16[:tm]], axis=1)
        hist[...] = gs[0:POOL_HALO, :]
        du_pool = msum - dzp

        vgv = vg_ref[...].astype(F32)
        val, gate = vgv[:, 0:512], vgv[:, 512:1024]
        sgg = _sigmoid(gate)
        a_wsu[...] += _mm_tn(val * sgg, dy_ssm)
        do = _mm_nt(dy_ssm, w_su[...])
        dvg = jnp.concatenate([do * sgg, do * val * (sgg * (1.0 - sgg))], axis=1)
        a_vs[3:4, :] += _colsum(dvg)
        yv = yss_ref[...].astype(F32)
        a_wglu[...] += _mm_tn(_gelu(yv), dvg)
        dyss = _mm_nt(dvg, w_glu[...]) * _gelu_grad(yv)
        a_vs[2:3, 0:512] += _colsum(dyss * u_ssm)
        du_blocks = []
        for j in range(SSM_BLOCKS):
            lanes = pl.ds(j * SSM_BLOCK_STATE, SSM_BLOCK_STATE)
            in_lanes = slice(j * 128, (j + 1) * 128)
            dyb = dyss[:, in_lanes].astype(MXU_DTYPE)
            ub = u_ssm[:, in_lanes].astype(MXU_DTYPE)
            dre[0] = _mm_nt(dyb, ct[0, j])
            dre[1] = -_mm_nt(dyb, ct[1, j])
            a_ct[0, j] += _mm_tn(sre_ref[:, lanes], dyb)
            a_ct[1, j] -= _mm_tn(sim_ref[:, lanes], dyb)
            a1r, a1i, a2r, a2i, a4r, a4i, pr, pi = [coef[j, k] for k in range(8)]
            rowi = lax.broadcasted_iota(jnp.int32, (SCAN_ROWS, SSM_BLOCK_STATE), 0)

            def step(tt, c, lanes=lanes, a1r=a1r, a1i=a1i, a2r=a2r, a2i=a2i, a4r=a4r, a4i=a4i, pr=pr, pi=pi, rowi=rowi):
                cr, ci, acc_r, acc_i = c
                rows = pl.ds(pl.multiple_of((n_tiles - 1 - tt) * SCAN_ROWS, SCAN_ROWS), SCAN_ROWS)
                xr, xi = dre[0, rows, :], dre[1, rows, :]
                for dstep, kr, ki in ((1, a1r, a1i), (2, a2r, a2i), (4, a4r, a4i)):
                    sr, si = pltpu.roll(xr, SCAN_ROWS - dstep, 0), pltpu.roll(xi, SCAN_ROWS - dstep, 0)
                    xr, xi = xr + kr * sr + ki * si, xi + kr * si - ki * sr
                xr, xi = xr + pr * cr + pi * ci, xi + pr * ci - pi * cr
                lam[0, rows, :] = xr
                lam[1, rows, :] = xi
                nr = jnp.where(rowi == SCAN_ROWS - 1, cr, pltpu.roll(xr, SCAN_ROWS - 1, 0))
                ni = jnp.where(rowi == SCAN_ROWS - 1, ci, pltpu.roll(xi, SCAN_ROWS - 1, 0))
                s_r, s_i = sre_ref[rows, lanes], sim_ref[rows, lanes]
                acc_r = acc_r + nr * s_r + ni * s_i
                acc_i = acc_i + ni * s_r - nr * s_i
                return (jnp.broadcast_to(xr[0:1, :], xr.shape), jnp.broadcast_to(xi[0:1, :], xi.shape), acc_r, acc_i)

            cr, ci, acc_r, acc_i = lax.fori_loop(0, n_tiles, step, (carry[j, 0], carry[j, 1], a_da[0, j], a_da[1, j]))
            carry[j, 0] = cr
            carry[j, 1] = ci
            a_da[0, j] = acc_r
            a_da[1, j] = acc_i
            lr_b, li_b = lam[0].astype(MXU_DTYPE), lam[1].astype(MXU_DTYPE)
            a_bb[0, j] += _mm_tn(ub, lr_b)
            a_bb[1, j] += _mm_tn(ub, li_b)
            du_blocks.append(_mm_nt(lr_b, bb[0, j]) + _mm_nt(li_b, bb[1, j]))
        du_ssm = jnp.concatenate(du_blocks, axis=1) + dyss * mv_ref[ROW_SSM_D:ROW_SSM_D + 1, 0:512]
        dz_ref[...] = jnp.concatenate([du_pool, du_ssm, dgl_pool, dgl_ssm], axis=1).astype(SAVE_DTYPE)

        @pl.when(i == nt - 1)
        def _():
            rows = D_MODEL // NDEV
            for k in range(NDEV):
                st_wo[k] = a_wo[k * rows:(k + 1) * rows, :].astype(WIRE_DTYPE)
                for a, acc in enumerate((a_wpu, a_wglu, a_wsu)):
                    st_up[a, k] = acc[:, k * 128:(k + 1) * 128].astype(WIRE_DTYPE)
            outs = ((st_wo, dwo_h), (st_up.at[0], dwpu_h), (st_up.at[1], dwglu_h), (st_up.at[2], dwsu_h),
                    (a_pw, dpw_h), (a_bb, dbb_h), (a_ct, dct_h), (a_vs, vsum_h), (a_da, da_h))
            cps = [pltpu.make_async_copy(src, dst, sems.at[k]) for k, (src, dst) in enumerate(outs)]
            for cp in cps:
                cp.start()
            for cp in cps:
                cp.wait()

    def tok(width):
        return pl.BlockSpec((tm, width), lambda i: (nt - 1 - i, 0))

    hbm = _HBM
    acc_shapes = [(D_MODEL, D_MODEL), (512, D_MODEL), (512, D_MODEL), (512, D_MODEL), (4, 128, 128),
                  (2, SSM_BLOCKS, 128, SSM_BLOCK_STATE), (2, SSM_BLOCKS, SSM_BLOCK_STATE, 128), (8, D_MODEL),
                  (2, SSM_BLOCKS, SCAN_ROWS, SSM_BLOCK_STATE)]
    stack_out = [jax.ShapeDtypeStruct((NDEV, D_MODEL // NDEV, D_MODEL), WIRE_DTYPE)] \
        + [jax.ShapeDtypeStruct((NDEV, 512, 128), WIRE_DTYPE)] * 3
    return _launch(
        body, "mixer_backward", grid=(nt,), semantics=("arbitrary",), carry=carry, steps=_grid_steps(nt),
        out_shape=[jax.ShapeDtypeStruct((T, IN_WIDTH), SAVE_DTYPE)] + stack_out
        + [jax.ShapeDtypeStruct(s, F32) for s in acc_shapes[4:]],
        in_specs=[tok(D_MODEL), _resident(prm), tok(IN_WIDTH), tok(N_STATE), tok(N_STATE), tok(512), tok(512),
                  tok(D_MODEL), tok(512), tok(D_MODEL), tok(D_MODEL), hbm, hbm, hbm, hbm, _resident(pool_w),
                  _resident(mvec), _resident(srow), _resident(bb), _resident(ct)],
        out_specs=[tok(IN_WIDTH)] + [hbm] * len(acc_shapes),
        operands=(d2, prm, z, s_re, s_im, zp, q, y_pool, yss, vg, y_ssm, w_pu_s, w_glu_s, w_su_s, w_out, pool_w, mvec,
                  srow, bb, ct),
        scratch=[
            pltpu.VMEM((512, D_MODEL), MXU_DTYPE), pltpu.VMEM((512, D_MODEL), MXU_DTYPE),
            pltpu.VMEM((512, D_MODEL), MXU_DTYPE), pltpu.VMEM((D_MODEL, D_MODEL), MXU_DTYPE),
            pltpu.VMEM((4, 128, 128), MXU_DTYPE),
            pltpu.VMEM((SSM_BLOCKS, 8, SCAN_ROWS, SSM_BLOCK_STATE), F32),
            pltpu.VMEM((SSM_BLOCKS, 2, SCAN_ROWS, SSM_BLOCK_STATE), F32),
            pltpu.VMEM((POOL_HALO, POOL_WIDTH), F32),
            pltpu.VMEM((2, tm, SSM_BLOCK_STATE), F32), pltpu.VMEM((2, tm, SSM_BLOCK_STATE), F32),
        ] + [pltpu.VMEM(s, F32) for s in acc_shapes]
        + [pltpu.VMEM((NDEV, D_MODEL // NDEV, D_MODEL), WIRE_DTYPE), pltpu.VMEM((3, NDEV, 512, 128), WIRE_DTYPE),
           pltpu.SemaphoreType.DMA((25,))])


def _mixer_in_backward(x, dz, prm, w_in_s):
    T = x.shape[0]
    tm = min(T, TM_MIX)
    nt = T // tm
    cols = IN_WIDTH // NDEV

    def body(x_ref, dz_ref, prm_ref, w_in_h, dh_ref, dw_ref, w_in, acc, sems):
        i = pl.program_id(0)

        @pl.when(i == 0)
        def _():
            cps = _load_stack(w_in_h, w_in, sems, 0)
            acc[...] = jnp.zeros_like(acc)
            for cp in cps:
                cp.wait()

        h, _, _ = _modulated(x_ref[...], prm_ref, 1, ROW_G_MIX)
        dzb = dz_ref[...].astype(MXU_DTYPE)
        dh_ref[0] = _mm_nt(dzb, w_in[...]).astype(SAVE_DTYPE)
        acc[...] += _mm_tn(h, dzb)

        @pl.when(i == nt - 1)
        def _():
            for k in range(NDEV):
                dw_ref[k] = acc[:, k * cols:(k + 1) * cols].astype(WIRE_DTYPE)

    return pl.pallas_call(
        body, name="mixer_in_backward", grid=(nt,),
        out_shape=[jax.ShapeDtypeStruct((1, T, D_MODEL), SAVE_DTYPE), jax.ShapeDtypeStruct((NDEV, D_MODEL, cols), WIRE_DTYPE)],
        in_specs=[pl.BlockSpec((tm, D_MODEL), lambda i: (i, 0)), pl.BlockSpec((tm, IN_WIDTH), lambda i: (i, 0)),
                  pl.BlockSpec(prm.shape, lambda i: (0, 0)), pl.BlockSpec(memory_space=pl.ANY)],
        out_specs=[pl.BlockSpec((1, tm, D_MODEL), lambda i: (0, i, 0)),
                   pl.BlockSpec((NDEV, D_MODEL, cols), lambda i: (0, 0, 0))],
        scratch_shapes=[pltpu.VMEM((D_MODEL, IN_WIDTH), MXU_DTYPE), pltpu.VMEM((D_MODEL, IN_WIDTH), F32),
                        pltpu.SemaphoreType.DMA((8,))],
        compiler_params=_params("arbitrary"),
    )(x, dz, prm, w_in_s)


def _ssm_dense_backward(dbb, da, srow, b_dense):
    def body(dbb_ref, da_ref, srow_ref, bd_ref, db_ref, df_ref):
        df_re, df_im = [], []
        da_re = [_colsum(da_ref[0, j]) for j in range(SSM_BLOCKS)]
        da_im = [_colsum(da_ref[1, j]) for j in range(SSM_BLOCKS)]
        for j in range(SSM_BLOCKS):
            lanes = slice(j * SSM_BLOCK_STATE, (j + 1) * SSM_BLOCK_STATE)
            f_re, f_im = srow_ref[2:3, lanes], srow_ref[3:4, lanes]
            g_re, g_im = dbb_ref[0, j], dbb_ref[1, j]
            b_re, b_im = bd_ref[0, j], bd_ref[1, j]
            db_ref[0, j] = f_re * g_re + f_im * g_im
            db_ref[1, j] = f_re * g_im - f_im * g_re
            df_re.append(_colsum(g_re * b_re + g_im * b_im))
            df_im.append(_colsum(g_im * b_re - g_re * b_im))
        df_ref[...] = jnp.concatenate([jnp.concatenate(df_re, axis=1), jnp.concatenate(df_im, axis=1),
                                       jnp.concatenate(da_re, axis=1), jnp.concatenate(da_im, axis=1),
                                       jnp.zeros((4, N_STATE), F32)], axis=0)

    return pl.pallas_call(body, name="ssm_dense_backward",
                          out_shape=[jax.ShapeDtypeStruct(b_dense.shape, F32), jax.ShapeDtypeStruct((8, N_STATE), F32)],
                          compiler_params=pltpu.CompilerParams(vmem_limit_bytes=VMEM_LIMIT))(dbb, da, srow, b_dense)


def _adamw_update(w, g, m, v):
    m = ADAM_B1 * m + (1.0 - ADAM_B1) * g
    v = ADAM_B2 * v + (1.0 - ADAM_B2) * (g * g)
    m_hat = m / (1.0 - ADAM_B1 ** ADAM_STEP)
    v_hat = v / (1.0 - ADAM_B2 ** ADAM_STEP)
    delta = -ADAM_LR * (m_hat / (jnp.sqrt(v_hat) + ADAM_EPS) + ADAM_WD * w)
    return delta, m, v


def _adam_rows(shape):
    rows, cols = shape
    tr = rows
    while tr * cols * 4 > (1 << 20) and tr % 16 == 0:
        tr //= 2
    return tr


def _adam_sharded(w, m, v, land, name):
    R, C = w.shape
    tr = _adam_rows((R, C))

    def body(w_ref, m_ref, v_ref, land_ref, g_ref, d_ref, mo_ref, vo_ref):
        g = land_ref[0].astype(F32)
        for b in range(1, NDEV):
            g = g + land_ref[b].astype(F32)
        g_ref[...] = g
        d_ref[...], mo_ref[...], vo_ref[...] = _adamw_update(w_ref[...], g, m_ref[...], v_ref[...])

    blk = pl.BlockSpec((tr, C), lambda i: (i, 0))
    return pl.pallas_call(
        body, name=name, grid=(R // tr,),
        out_shape=[jax.ShapeDtypeStruct((R, C), F32)] * 4,
        in_specs=[blk, blk, blk, pl.BlockSpec((NDEV, tr, C), lambda i: (0, i, 0))],
        out_specs=[blk] * 4,
        compiler_params=_params("arbitrary"),
    )(w, m, v, land)


def _adam_ada(w, m, v, sc_all, dmod_cols):
    R, C = w.shape
    tr = 256

    def body(w_ref, m_ref, v_ref, sc_ref, dm_ref, g_ref, d_ref, mo_ref, vo_ref):
        g = _mm_tn(sc_ref[...], dm_ref[...])
        g_ref[...] = g
        d_ref[...], mo_ref[...], vo_ref[...] = _adamw_update(w_ref[...], g, m_ref[...], v_ref[...])

    blk = pl.BlockSpec((tr, C), lambda i: (i, 0))
    return pl.pallas_call(
        body, name="adam_w_ada", grid=(R // tr,),
        out_shape=[jax.ShapeDtypeStruct((R, C), F32)] * 4,
        in_specs=[blk, blk, blk, pl.BlockSpec((8, tr), lambda i: (0, i)), pl.BlockSpec((8, C), lambda i: (0, 0))],
        out_specs=[blk] * 4,
        compiler_params=_params("arbitrary"),
    )(w, m, v, sc_all, dmod_cols)


def _adam_small(w, g, m, v):
    def body(w_ref, g_ref, m_ref, v_ref, d_ref, mo_ref, vo_ref):
        d_ref[...], mo_ref[...], vo_ref[...] = _adamw_update(w_ref[...], g_ref[...], m_ref[...], v_ref[...])

    return pl.pallas_call(body, name="adam_small", out_shape=[jax.ShapeDtypeStruct(w.shape, F32)] * 3,
                          compiler_params=pltpu.CompilerParams(vmem_limit_bytes=VMEM_LIMIT))(w, g, m, v)


def _block_diag_in(b):
    bt = jnp.transpose(b, (0, 2, 1)).reshape(SSM_BLOCKS, 8, SSM_GROUP, SSM_STATE)
    eye = jnp.eye(8, dtype=bool)[None, :, None, :, None]
    return jnp.where(eye, bt[:, :, :, None, :], 0.0).reshape(SSM_BLOCKS, 128, SSM_BLOCK_STATE)


def _block_diag_out(c):
    ct = jnp.transpose(c, (0, 2, 1)).reshape(SSM_BLOCKS, 8, SSM_STATE, SSM_GROUP)
    eye = jnp.eye(8, dtype=bool)[None, :, None, :, None]
    return jnp.where(eye, ct[:, :, :, None, :], 0.0).reshape(SSM_BLOCKS, SSM_BLOCK_STATE, 128)


def _diag_blocks(dense, rows, cols):
    d5 = dense.reshape(SSM_BLOCKS, 8, rows, 8, cols)
    return jnp.stack([d5[:, a, :, a, :] for a in range(8)], axis=1).reshape(32, rows, cols)


def _pack_small(ada_vec, parts):
    rest = jnp.concatenate([parts[n].reshape(-1) for n, _ in SMALL_PARAMS])
    rest = jnp.pad(rest, (0, NDEV * REST_ROWS * 128 - SMALL_TOTAL)).reshape(NDEV, REST_ROWS, 128)
    return jnp.concatenate([ada_vec.reshape(NDEV, ADA_ROWS, 128), rest,
                            jnp.zeros((NDEV, PACK_ROWS - ADA_ROWS - REST_ROWS, 128), F32)], axis=1)


def _unpack_small(pack, shapes):
    ada_vec = pack[:, :ADA_ROWS].reshape(-1)
    rest = pack[:, ADA_ROWS:ADA_ROWS + REST_ROWS].reshape(-1)
    out, off = {}, 0
    for n, size in SMALL_PARAMS:
        out[n] = rest[off:off + size].reshape(shapes[n])
        off += size
    return ada_vec, out


WEIGHT_ORDER = ('w_ada', 'b_ada', 'g_ffn1', 'w_ffn1_in', 'w_ffn1_out', 'g_mix', 'w_in', 'pool_w', 'pool_b',
                'pool_scale', 'w_pool_up', 'ssm_lam_re_log', 'ssm_lam_im', 'ssm_log_dt', 'ssm_b_re', 'ssm_b_im',
                'ssm_c_re', 'ssm_c_im', 'ssm_d', 'w_glu', 'b_glu', 'w_ssm_up', 'w_out', 'g_ffn2', 'w_ffn2_in',
                'w_ffn2_out', 'g_final')
GATHERED = ('w_ffn1_in', 'w_ffn1_out', 'w_in', 'w_pool_up', 'w_glu', 'w_ssm_up', 'w_out', 'w_ffn2_in', 'w_ffn2_out')


def kernel(x, c, w_ada, b_ada, g_ffn1, w_ffn1_in, w_ffn1_out, g_mix, w_in, pool_w, pool_b, pool_scale, w_pool_up, ssm_lam_re_log, ssm_lam_im, ssm_log_dt, ssm_b_re, ssm_b_im, ssm_c_re, ssm_c_im, ssm_d, w_glu, b_glu, w_ssm_up, w_out, g_ffn2, w_ffn2_in, w_ffn2_out, g_final, loss_target, m_w_ada, m_b_ada, m_g_ffn1, m_w_ffn1_in, m_w_ffn1_out, m_g_mix, m_w_in, m_pool_w, m_pool_b, m_pool_scale, m_w_pool_up, m_ssm_lam_re_log, m_ssm_lam_im, m_ssm_log_dt, m_ssm_b_re, m_ssm_b_im, m_ssm_c_re, m_ssm_c_im, m_ssm_d, m_w_glu, m_b_glu, m_w_ssm_up, m_w_out, m_g_ffn2, m_w_ffn2_in, m_w_ffn2_out, m_g_final, v_w_ada, v_b_ada, v_g_ffn1, v_w_ffn1_in, v_w_ffn1_out, v_g_mix, v_w_in, v_pool_w, v_pool_b, v_pool_scale, v_w_pool_up, v_ssm_lam_re_log, v_ssm_lam_im, v_ssm_log_dt, v_ssm_b_re, v_ssm_b_im, v_ssm_c_re, v_ssm_c_im, v_ssm_d, v_w_glu, v_b_glu, v_w_ssm_up, v_w_out, v_g_ffn2, v_w_ffn2_in, v_w_ffn2_out, v_g_final):
    args = locals()
    W = {n: args[n] for n in WEIGHT_ORDER}
    M = {n: args["m_" + n] for n in WEIGHT_ORDER}
    V = {n: args["v_" + n] for n in WEIGHT_ORDER}
    shapes = {n: W[n].shape for n in WEIGHT_ORDER}
    xt, tgt = x[0], loss_target[0]

    shard = dict(zip(GATHERED, _cast_shards([W[n][0] for n in GATHERED])))
    stacks = {}

    def gather(names):
        return _Gather([shard[n] for n in names])

    def gathered(names, results):
        stacks.update(zip(names, results))

    ffn1_w, ffn2_w = ('w_ffn1_in', 'w_ffn1_out'), ('w_ffn2_in', 'w_ffn2_out')
    mix_w = ('w_in', 'w_pool_up', 'w_glu', 'w_ssm_up', 'w_out')
    mod_cols, sc_all, *res = _ada_forward(c, W['w_ada'][0], b_ada.reshape(NDEV, -1), gather(ffn1_w))
    gathered(ffn1_w, res)
    win1 = stacks['w_ffn1_in'].reshape(2, 4, D_MODEL, FF_SHARD)
    wout1 = stacks['w_ffn1_out'].reshape(4, FF_SHARD, D_MODEL)
    prm = jnp.concatenate([mod_cols.reshape(9, D_MODEL), g_ffn1, g_mix, g_ffn2, g_final[None], jnp.zeros((3, D_MODEL), F32)], axis=0)
    pad512 = jnp.zeros((1, D_MODEL - 512), F32)
    mvec = jnp.concatenate([jnp.concatenate([pool_b, pad512], axis=1), jnp.concatenate([pool_scale, pad512], axis=1),
                            jnp.concatenate([ssm_d, pad512], axis=1), b_glu, jnp.zeros((4, D_MODEL), F32)], axis=0)
    log_dt_col = ssm_log_dt[0][:, None]
    coeffs = _ssm_params_forward(ssm_lam_re_log[0], ssm_lam_im[0], log_dt_col)
    srow = jnp.stack([t.reshape(N_STATE) for t in coeffs], axis=0)
    b_dense = jnp.stack([_block_diag_in(ssm_b_re[0]), _block_diag_in(ssm_b_im[0])], axis=0)
    c_dense = jnp.stack([_block_diag_out(ssm_c_re[0]), _block_diag_out(ssm_c_im[0])], axis=0)
    bb, ct = _ssm_dense_forward(srow, b_dense, c_dense)
    pw = pool_w[0]

    x1, f1, ab1, *res = _ffn_forward(xt, prm, win1, wout1, 0, ROW_G_FFN1, "ffn1_forward", gather(mix_w))
    gathered(mix_w, res)
    w_out_full = stacks['w_out'].reshape(D_MODEL, D_MODEL)
    res = _mixer_forward(x1, prm, stacks['w_in'], stacks['w_pool_up'], stacks['w_glu'], stacks['w_ssm_up'],
                         w_out_full, pw, mvec, srow, bb, ct, gather(ffn2_w))
    x2, mo, saved = res[0], res[1], res[2:11]
    gathered(ffn2_w, res[11:])
    win2 = stacks['w_ffn2_in'].reshape(2, 4, D_MODEL, FF_SHARD)
    wout2 = stacks['w_ffn2_out'].reshape(4, FF_SHARD, D_MODEL)
    x3, f3, ab3 = _ffn_forward(x2, prm, win2, wout2, 2, ROW_G_FFN2, "ffn2_forward")
    d3, fin = _final_loss(x3, tgt, prm)
    loss = lax.psum(fin[1, 0], ("x", "y", "c"))

    lands = {}

    def scatter(grads):
        names = list(grads)
        return _Scatter([grads[n][0] for n in names], [grads[n][1] for n in names], [W[n].shape[1:] for n in names])

    def scattered(grads, results):
        lands.update(zip(grads, results))

    parts3, dwin2, dwout2 = _ffn_backward(x2, d3, ab3, prm, win2, wout2, 2, ROW_G_FFN2, "ffn2_backward")
    d2, sums3 = _norm_backward(parts3, x2, d3, f3, prm, 2, ROW_G_FFN2, 0.5, "ffn2_norm_backward")
    g_ffn2_w = {'w_ffn2_in': (dwin2, _halves), 'w_ffn2_out': (dwout2.reshape(NDEV, -1, D_MODEL), _stacked)}
    res = _mixer_backward(d2, prm, saved, stacks['w_pool_up'], stacks['w_glu'], stacks['w_ssm_up'], w_out_full, pw, mvec,
                          srow, bb, ct, scatter(g_ffn2_w))
    dz, dwo, dwpu, dwglu, dwsu, dpw, dbb, dct, vsum, da = res[:10]
    scattered(g_ffn2_w, res[10:])
    parts2, dwin_mix = _mixer_in_backward(x1, dz, prm, stacks['w_in'])
    d1, sums2 = _norm_backward(parts2, x1, d2, mo, prm, 1, ROW_G_MIX, 1.0, "mixer_norm_backward")
    g_mix_w = {'w_in': (dwin_mix, _stacked), 'w_pool_up': (dwpu, _stacked), 'w_glu': (dwglu, _stacked),
               'w_ssm_up': (dwsu, _stacked), 'w_out': (dwo, _stacked)}
    parts1, dwin1, dwout1, *res = _ffn_backward(xt, d1, ab1, prm, win1, wout1, 0, ROW_G_FFN1, "ffn1_backward",
                                                scatter(g_mix_w))
    scattered(g_mix_w, res)
    g_ffn1_w = {'w_ffn1_in': (dwin1, _halves), 'w_ffn1_out': (dwout1.reshape(NDEV, -1, D_MODEL), _stacked)}
    d0, sums1, *res = _norm_backward(parts1, xt, d1, f1, prm, 0, ROW_G_FFN1, 0.5, "ffn1_norm_backward", scatter(g_ffn1_w))
    scattered(g_ffn1_w, res)

    db_dense, df_rows = _ssm_dense_backward(dbb, da, srow, b_dense)
    cot = [df_rows[r].reshape(32, 64) for r in (2, 3, 0, 1)]
    d_lrl, d_li, d_ldt = _ssm_params_backward(ssm_lam_re_log[0], ssm_lam_im[0], log_dt_col, cot)
    small_grads = {
        'g_ffn1': sums1[0], 'g_mix': sums2[0], 'g_ffn2': sums3[0], 'g_final': fin[0], 'pool_w': dpw,
        'pool_b': vsum[1, :512], 'pool_scale': vsum[0, :512], 'ssm_lam_re_log': d_lrl, 'ssm_lam_im': d_li,
        'ssm_log_dt': d_ldt, 'ssm_b_re': jnp.transpose(_diag_blocks(db_dense[0], SSM_GROUP, SSM_STATE), (0, 2, 1)),
        'ssm_b_im': jnp.transpose(_diag_blocks(db_dense[1], SSM_GROUP, SSM_STATE), (0, 2, 1)),
        'ssm_c_re': jnp.transpose(_diag_blocks(dct[0], SSM_STATE, SSM_GROUP), (0, 2, 1)),
        'ssm_c_im': jnp.transpose(_diag_blocks(dct[1], SSM_STATE, SSM_GROUP), (0, 2, 1)),
        'ssm_d': vsum[2, :512], 'b_glu': vsum[3],
    }
    dmod = jnp.concatenate([sums1[1:4], sums2[1:4], sums3[1:4]], axis=0).reshape(-1)
    total, landed = _allreduce_small(_pack_small(dmod, small_grads))
    dmod_cols = landed[:, :ADA_ROWS].reshape(NDEV, ADA_ROWS * 128)

    grad, delta, new_m, new_v = {}, {}, {}, {}
    for n in GATHERED:
        res = _adam_sharded(W[n][0], M[n][0], V[n][0], lands[n], "adam_" + n)
        grad[n], delta[n], new_m[n], new_v[n] = [r[None] for r in res]
    res = _adam_ada(W['w_ada'][0], M['w_ada'][0], V['w_ada'][0], sc_all, dmod_cols)
    grad['w_ada'], delta['w_ada'], new_m['w_ada'], new_v['w_ada'] = [r[None] for r in res]

    flat = lambda t: t.reshape(NDEV * PACK_ROWS, 128)
    small_w = flat(_pack_small(b_ada.reshape(-1), W))
    small_m = flat(_pack_small(m_b_ada.reshape(-1), M))
    small_v = flat(_pack_small(v_b_ada.reshape(-1), V))
    res = _adam_small(small_w, flat(total), small_m, small_v)
    for dst, packed in zip((grad, delta, new_m, new_v), (total, *res)):
        ada_vec, rest = _unpack_small(packed.reshape(NDEV, PACK_ROWS, 128), shapes)
        dst.update(rest)
        dst['b_ada'] = ada_vec.reshape(shapes['b_ada'])

    return (loss, d0[None], *[grad[n] for n in WEIGHT_ORDER], *[delta[n] for n in WEIGHT_ORDER],
            *[new_m[n] for n in WEIGHT_ORDER], *[new_v[n] for n in WEIGHT_ORDER])
```

```python
import functools

import jax
import jax.numpy as jnp
from jax import lax
from jax.experimental import pallas as pl
from jax.experimental.pallas import tpu as pltpu

F32 = jnp.float32
MXU_DTYPE = jnp.bfloat16
WIRE_DTYPE = jnp.bfloat16
SAVE_DTYPE = jnp.bfloat16

NDEV = 8
D_MODEL = 1024
D_FF = 2816
FF_SHARD = 2 * D_FF // NDEV
POOL_WIDTH = 512
POOL_GROUP = 128
SSM_WIDTH = 512
SSM_STATE = 64
SSM_GROUP = 16
SSM_BLOCKS = 4
SSM_BLOCK_STATE = 512
N_STATE = 2048
IN_WIDTH = 3072
EPS = 1e-6
ADAM_LR = 0.001
ADAM_B1 = 0.9
ADAM_B2 = 0.999
ADAM_EPS = 1e-08
ADAM_WD = 0.01
ADAM_STEP = 10

TM_FFN = 512
TM_MIX = 256
TM_MIX_BWD = 256
TM_EW = 512
SCAN_ROWS = 8
POOL_HALO = 16
VMEM_LIMIT = 60 * 1024 * 1024

ROW_G_FFN1, ROW_G_MIX, ROW_G_FFN2, ROW_G_FINAL = 9, 10, 11, 12
ROW_POOL_B, ROW_POOL_SCALE, ROW_SSM_D, ROW_B_GLU = 0, 1, 2, 3

SMALL_PARAMS = (
    ("g_ffn1", 1024), ("g_mix", 1024), ("g_ffn2", 1024), ("g_final", 1024), ("pool_w", 65536),
    ("pool_b", 512), ("pool_scale", 512), ("ssm_lam_re_log", 2048), ("ssm_lam_im", 2048),
    ("ssm_log_dt", 32), ("ssm_b_re", 32768), ("ssm_b_im", 32768), ("ssm_c_re", 32768),
    ("ssm_c_im", 32768), ("ssm_d", 512), ("b_glu", 1024),
)
SMALL_TOTAL = sum(n for _, n in SMALL_PARAMS)
ADA_ROWS = 9
REST_ROWS = 203
PACK_ROWS = 216
MESH = pl.DeviceIdType.MESH


def _mm(a, b):
    return jnp.dot(a.astype(MXU_DTYPE), b.astype(MXU_DTYPE), preferred_element_type=F32)


def _mm_nt(a, b):
    return lax.dot_general(a.astype(MXU_DTYPE), b.astype(MXU_DTYPE), (((1,), (1,)), ((), ())),
                           preferred_element_type=F32)


def _mm_tn(a, b):
    return lax.dot_general(a.astype(MXU_DTYPE), b.astype(MXU_DTYPE), (((0,), (0,)), ((), ())),
                           preferred_element_type=F32)


def _rms_scale(x):
    return lax.rsqrt(jnp.mean(x * x, axis=-1, keepdims=True) + EPS)


def _sigmoid(x):
    return jax.nn.sigmoid(x)


def _colsum(x):
    return jnp.sum(x, axis=0, keepdims=True)


def _row(ref, r):
    return ref[r:r + 1, :]


def _params(*sem):
    return pltpu.CompilerParams(dimension_semantics=sem, vmem_limit_bytes=VMEM_LIMIT)


def _resident(a):
    return pl.BlockSpec(a.shape, lambda *_: (0,) * a.ndim, pipeline_mode=pl.Buffered(1))


def _me():
    return lax.axis_index("x"), lax.axis_index("y"), lax.axis_index("c")


def _peer(rel):
    x, y, c = _me()
    px = 1 - x if rel & 4 else x
    py = 1 - y if rel & 2 else y
    pc = 1 - c if rel & 1 else c
    return (px, py, pc), 4 * px + 2 * py + pc


_HBM = pl.BlockSpec(memory_space=pl.ANY)
_HBM_ONLY = pl.BlockSpec(memory_space=pltpu.HBM)


def _stacked(ref, p):
    return ref.at[p]


def _halves(ref, p):
    return ref.at[p // 4, p % 4]


class _Gather:
    def __init__(self, shards):
        self.operands = list(shards)
        self.n = len(shards)
        self.out_shape = [jax.ShapeDtypeStruct((NDEV,) + s.shape, s.dtype) for s in shards]
        self.scratch = [pltpu.SemaphoreType.DMA((7 * self.n,)), pltpu.SemaphoreType.DMA((7 * self.n,)),
                        pltpu.SemaphoreType.DMA((self.n,))]

    def plan(self, srcs, outs, sems):
        send_sems, recv_sems, local_sems = sems
        n = self.n
        x, y, c = _me()
        me = 4 * x + 2 * y + c
        here, sibling = (x, y, c), (x, y, 1 - c)
        chips = [(1 - x, y), (x, 1 - y), (1 - x, 1 - y)]

        def blk(px, py, pc):
            return 4 * px + 2 * py + pc

        def copy(a, k, block, to, src=None):
            return pltpu.make_async_remote_copy(
                src_ref=outs[a].at[block] if src is None else src, dst_ref=outs[a].at[block],
                send_sem=send_sems.at[7 * a + k], recv_sem=recv_sems.at[7 * a + k], device_id=to, device_id_type=MESH)

        def mine(a):
            return pltpu.make_async_copy(srcs[a], outs[a].at[me], local_sems.at[a])

        def first(a):
            return [copy(a, 0, me, sibling, src=srcs[a])] + [copy(a, 1 + j, me, (*chip, c), src=srcs[a])
                                                              for j, chip in enumerate(chips)]

        def start():
            for a in range(n):
                mine(a).start()
                for cp in first(a):
                    cp.start()

        def forward():
            for a in range(n):
                for j, chip in enumerate(chips):
                    copy(a, 1 + j, blk(*chip, c), here).wait_recv()
                    copy(a, 4 + j, blk(*chip, c), sibling).start()

        def finish():
            for a in range(n):
                copy(a, 0, blk(x, y, 1 - c), here).wait_recv()
                for j, chip in enumerate(chips):
                    copy(a, 4 + j, blk(*chip, 1 - c), here).wait_recv()
            for a in range(n):
                mine(a).wait()
                for cp in first(a):
                    cp.wait_send()
                for j, chip in enumerate(chips):
                    copy(a, 4 + j, blk(*chip, c), sibling).wait_send()

        return start, forward, finish


class _Scatter:
    def __init__(self, arrays, views, shard_shapes):
        self.operands = list(arrays)
        self.views = list(views)
        self.n = len(arrays)
        self.out_shape = [jax.ShapeDtypeStruct((NDEV,) + tuple(s), a.dtype) for s, a in zip(shard_shapes, arrays)]
        self.scratch = [pltpu.SemaphoreType.DMA((7 * self.n,)), pltpu.SemaphoreType.DMA((7 * self.n,)),
                        pltpu.SemaphoreType.DMA((self.n,))]

    def plan(self, srcs, outs, sems):
        send_sems, recv_sems, local_sems = sems
        n, views = self.n, self.views
        x, y, c = _me()
        me = 4 * x + 2 * y + c

        def mine(a):
            return pltpu.make_async_copy(views[a](srcs[a], me), outs[a].at[me], local_sems.at[a])

        def copy(a, rel, sending):
            to, p = _peer(rel)
            return pltpu.make_async_remote_copy(
                src_ref=views[a](srcs[a], p), dst_ref=outs[a].at[me if sending else p],
                send_sem=send_sems.at[7 * a + rel - 1], recv_sem=recv_sems.at[7 * a + rel - 1],
                device_id=to if sending else (x, y, c), device_id_type=MESH)

        def start():
            for a in range(n):
                mine(a).start()
            for rel in range(1, 8):
                for a in range(n):
                    copy(a, rel, True).start()

        def forward():
            pass

        def finish():
            for rel in range(1, 8):
                for a in range(n):
                    copy(a, rel, False).wait_recv()
            for rel in range(1, 8):
                for a in range(n):
                    copy(a, rel, True).wait_send()
            for a in range(n):
                mine(a).wait()

        return start, forward, finish


def _launch(body, name, out_shape, in_specs, out_specs, operands, scratch=(), grid=None, semantics=None,
            carry=None, steps=None):
    out_shape, in_specs, out_specs = list(out_shape), list(in_specs), list(out_specs)
    operands, scratch = list(operands), list(scratch)
    n_in, n_out, n_scr = len(in_specs), len(out_shape), len(scratch)
    kernel_body = body
    if carry is not None:
        k = carry.n

        def kernel_body(*refs):
            ins, cin = refs[:n_in], refs[n_in:n_in + k]
            outs, cout = refs[n_in + k:n_in + k + n_out], refs[n_in + k + n_out:n_in + 2 * k + n_out]
            rest = refs[n_in + 2 * k + n_out:]
            scr, csem = rest[:n_scr], rest[n_scr:]
            start, forward, finish = carry.plan(cin, cout, csem)
            if steps is None:
                start()
                body(*ins, *outs, *scr)
                forward()
                finish()
            else:
                pl.when(steps()[0])(start)
                body(*ins, *outs, *scr)
                pl.when(steps()[1])(forward)
                pl.when(steps()[2])(finish)

        in_specs += [_HBM] * k
        out_shape += carry.out_shape
        out_specs += [_HBM] * k
        operands += carry.operands
        scratch += carry.scratch
    kwargs = {} if grid is None else {"grid": grid}
    params = pltpu.CompilerParams(vmem_limit_bytes=VMEM_LIMIT) if semantics is None else _params(*semantics)
    return pl.pallas_call(kernel_body, name=name, out_shape=out_shape, in_specs=in_specs, out_specs=out_specs,
                          scratch_shapes=scratch, compiler_params=params, **kwargs)(*operands)


def _grid_steps(nt):
    def steps():
        i = pl.program_id(0)
        return i == 0, i == nt // 2, i == nt - 1
    return steps


def _cast_shards(shards):
    n = len(shards)

    def body(*refs):
        for a in range(n):
            refs[n + a][...] = refs[a][...].astype(WIRE_DTYPE)

    return pl.pallas_call(body, name="cast_shards",
                          out_shape=[jax.ShapeDtypeStruct(s.shape, WIRE_DTYPE) for s in shards],
                          compiler_params=pltpu.CompilerParams(vmem_limit_bytes=VMEM_LIMIT))(*shards)


_SEM = pl.BlockSpec(memory_space=pltpu.SEMAPHORE)
_DATAFLOW = pltpu.SideEffectType.DATAFLOW_SIDE_EFFECTING


def _split_copy(arrays, views, landing, send_sems, recv_sems, a, rel):
    to, p = _peer(rel)
    x, y, c = _me()
    return pltpu.make_async_remote_copy(
        src_ref=views[a](arrays[a], p), dst_ref=landing[a].at[4 * x + 2 * y + c],
        send_sem=send_sems.at[NDEV * a + rel], recv_sem=recv_sems.at[NDEV * a + rel], device_id=to, device_id_type=MESH)


def _scatter_start(arrays, views, shard_shapes):
    n = len(arrays)
    landing = [pltpu.with_memory_space_constraint(lax.empty((NDEV,) + tuple(s), a.dtype), pltpu.HBM)
               for s, a in zip(shard_shapes, arrays)]
    arrays = [pltpu.with_memory_space_constraint(a, pltpu.HBM) for a in arrays]

    def body(*refs):
        ins, land = refs[:n], refs[n:2 * n]
        send_sems, recv_sems = refs[2 * n], refs[2 * n + 1]
        token = refs[-1]
        for rel in range(NDEV):
            for a in range(n):
                _split_copy(ins, views, land, send_sems, recv_sems, a, rel).start()
        token[...] = jnp.zeros_like(token)

    res = pl.pallas_call(
        body, name="scatter_start",
        out_shape=[pltpu.SemaphoreType.DMA((NDEV * n,)), pltpu.SemaphoreType.DMA((NDEV * n,))]
        + [pltpu.HBM(a.shape, a.dtype) for a in arrays] + [pltpu.HBM(l.shape, l.dtype) for l in landing]
        + [jax.ShapeDtypeStruct((8, 128), F32)],
        in_specs=[_HBM_ONLY] * (2 * n),
        out_specs=[_SEM, _SEM] + [_HBM_ONLY] * (2 * n) + [pl.BlockSpec(memory_space=pltpu.VMEM)],
        input_output_aliases={i: 2 + i for i in range(2 * n)},
        compiler_params=pltpu.CompilerParams(has_side_effects=_DATAFLOW),
    )(*arrays, *landing)
    return res[0], res[1], res[2:2 + n], res[2 + n:2 + 2 * n], res[-1]


def _scatter_wait(send_sems, recv_sems, arrays, landing, views, after):
    n = len(arrays)

    def body(*refs):
        ins, land = refs[:n], refs[n:2 * n]
        send, recv = refs[2 * n], refs[2 * n + 1]
        for rel in range(NDEV):
            for a in range(n):
                cp = _split_copy(ins, views, land, send, recv, a, rel)
                cp.wait_send()
                cp.wait_recv()

    res = pl.pallas_call(
        body, name="scatter_wait",
        out_shape=[pltpu.HBM(a.shape, a.dtype) for a in arrays] + [pltpu.HBM(l.shape, l.dtype) for l in landing],
        in_specs=[_HBM_ONLY] * (2 * n) + [_SEM, _SEM] + [_HBM] * len(after),
        out_specs=[_HBM_ONLY] * (2 * n),
        input_output_aliases={i: i for i in range(2 * n)},
        compiler_params=pltpu.CompilerParams(has_side_effects=_DATAFLOW),
    )(*arrays, *landing, send_sems, recv_sems, *after)
    return res[n:]


def _ada_forward(c_row, w_ada, b_ada8, carry):
    cols = w_ada.shape[1]

    def body(c_ref, w_ref, b_ref, mod_ref, sc_ref, c_all, send_buf, recv_buf, send1, recv1, send2, recv2):
        x, y, c = _me()
        me = 4 * x + 2 * y + c
        rowi = lax.broadcasted_iota(jnp.int32, (8, D_MODEL), 0)
        c_all[me] = jnp.broadcast_to(c_ref[...], (8, D_MODEL))
        copies = []
        for rel in range(1, 8):
            to, _ = _peer(rel)
            cp = pltpu.make_async_remote_copy(src_ref=c_all.at[me], dst_ref=c_all.at[me], send_sem=send1.at[rel - 1],
                                              recv_sem=recv1.at[rel - 1], device_id=to, device_id_type=MESH)
            cp.start()
            copies.append(cp)
        for rel in range(1, 8):
            _, p = _peer(rel)
            pltpu.make_async_remote_copy(src_ref=c_all.at[p], dst_ref=c_all.at[p], send_sem=send1.at[rel - 1],
                                         recv_sem=recv1.at[rel - 1], device_id=(x, y, c), device_id_type=MESH).wait_recv()
        for cp in copies:
            cp.wait_send()
        cmat = jnp.zeros((8, D_MODEL), F32)
        for b in range(8):
            cmat = jnp.where(rowi == b, c_all[b], cmat)
        sc = cmat * _sigmoid(cmat)
        sc_ref[...] = sc
        modcols = _mm(sc, w_ref[...]) + b_ref[pl.ds(me, 1), :]
        for b in range(8):
            send_buf[b] = jnp.broadcast_to(modcols[b:b + 1, :], (8, cols))
        recv_buf[me] = send_buf[me]
        copies = []
        for rel in range(1, 8):
            to, p = _peer(rel)
            cp = pltpu.make_async_remote_copy(src_ref=send_buf.at[p], dst_ref=recv_buf.at[me], send_sem=send2.at[rel - 1],
                                              recv_sem=recv2.at[rel - 1], device_id=to, device_id_type=MESH)
            cp.start()
            copies.append(cp)
        for rel in range(1, 8):
            _, p = _peer(rel)
            pltpu.make_async_remote_copy(src_ref=send_buf.at[p], dst_ref=recv_buf.at[p], send_sem=send2.at[rel - 1],
                                         recv_sem=recv2.at[rel - 1], device_id=(x, y, c), device_id_type=MESH).wait_recv()
        for cp in copies:
            cp.wait_send()
        rowc = lax.broadcasted_iota(jnp.int32, (8, cols), 0)
        out = jnp.zeros((8, cols), F32)
        for k in range(8):
            out = jnp.where(rowc == k, recv_buf[k], out)
        mod_ref[...] = out

    return _launch(
        body, "ada_forward",
        out_shape=[jax.ShapeDtypeStruct((8, cols), F32), jax.ShapeDtypeStruct((8, D_MODEL), F32)],
        in_specs=[pl.BlockSpec(memory_space=pltpu.VMEM)] * 3,
        out_specs=[pl.BlockSpec(memory_space=pltpu.VMEM)] * 2,
        operands=(c_row, w_ada, b_ada8),
        scratch=[pltpu.VMEM((8, 8, D_MODEL), F32), pltpu.VMEM((8, 8, cols), F32), pltpu.VMEM((8, 8, cols), F32)]
        + [pltpu.SemaphoreType.DMA((7,))] * 4,
        carry=carry)


def _allreduce_small(pack):
    rows = pack.shape[1]

    def body(pack_ref, total_ref, land_ref, send1, recv1, send2, recv2):
        x, y, c = _me()
        me = 4 * x + 2 * y + c
        land_ref[me] = pack_ref[me]
        copies = []
        for rel in range(1, 8):
            to, p = _peer(rel)
            cp = pltpu.make_async_remote_copy(src_ref=pack_ref.at[p], dst_ref=land_ref.at[me], send_sem=send1.at[rel - 1],
                                              recv_sem=recv1.at[rel - 1], device_id=to, device_id_type=MESH)
            cp.start()
            copies.append(cp)
        for rel in range(1, 8):
            _, p = _peer(rel)
            pltpu.make_async_remote_copy(src_ref=pack_ref.at[p], dst_ref=land_ref.at[p], send_sem=send1.at[rel - 1],
                                         recv_sem=recv1.at[rel - 1], device_id=(x, y, c), device_id_type=MESH).wait_recv()
        for cp in copies:
            cp.wait_send()
        acc = land_ref[0]
        for b in range(1, 8):
            acc = acc + land_ref[b]
        total_ref[me] = acc
        copies = []
        for rel in range(1, 8):
            to, _ = _peer(rel)
            cp = pltpu.make_async_remote_copy(src_ref=total_ref.at[me], dst_ref=total_ref.at[me], send_sem=send2.at[rel - 1],
                                              recv_sem=recv2.at[rel - 1], device_id=to, device_id_type=MESH)
            cp.start()
            copies.append(cp)
        for rel in range(1, 8):
            _, p = _peer(rel)
            pltpu.make_async_remote_copy(src_ref=total_ref.at[p], dst_ref=total_ref.at[p], send_sem=send2.at[rel - 1],
                                         recv_sem=recv2.at[rel - 1], device_id=(x, y, c), device_id_type=MESH).wait_recv()
        for cp in copies:
            cp.wait_send()

    return pl.pallas_call(
        body, name="allreduce_small",
        out_shape=[jax.ShapeDtypeStruct((8, rows, 128), F32), jax.ShapeDtypeStruct((8, rows, 128), F32)],
        in_specs=[pl.BlockSpec(memory_space=pltpu.VMEM)],
        out_specs=[pl.BlockSpec(memory_space=pltpu.VMEM)] * 2,
        scratch_shapes=[pltpu.SemaphoreType.DMA((7,))] * 4,
        compiler_params=pltpu.CompilerParams(vmem_limit_bytes=VMEM_LIMIT),
    )(pack)


def _modulated(x, prm_ref, sub, g_row):
    shift, scale = _row(prm_ref, 3 * sub), _row(prm_ref, 3 * sub + 1)
    g = _row(prm_ref, g_row)
    r = _rms_scale(x)
    n0 = x * r
    return (n0 * g) * (1.0 + scale) + shift, r, n0


def _ffn_forward(x, prm, win, wout, sub, g_row, name, carry=None):
    T = x.shape[0]
    tm = min(T, TM_FFN)

    def body(x_ref, prm_ref, win_ref, wout_ref, xo_ref, f_ref, ab_ref):
        xv = x_ref[...]
        h, _, _ = _modulated(xv, prm_ref, sub, g_row)
        hb = h.astype(MXU_DTYPE)
        acc = None
        for j in range(4):
            a = _mm(hb, win_ref[0, j])
            b = _mm(hb, win_ref[1, j])
            ab_ref[0, j] = a.astype(SAVE_DTYPE)
            ab_ref[1, j] = b.astype(SAVE_DTYPE)
            s = (a * _sigmoid(a)) * b
            t = _mm(s, wout_ref[j])
            acc = t if acc is None else acc + t
        f_ref[...] = acc.astype(SAVE_DTYPE)
        xo_ref[...] = xv + (0.5 * _row(prm_ref, 3 * sub + 2)) * acc

    tok = pl.BlockSpec((tm, D_MODEL), lambda i: (i, 0))
    return _launch(
        body, name, grid=(T // tm,), semantics=("arbitrary",),
        out_shape=[jax.ShapeDtypeStruct((T, D_MODEL), F32), jax.ShapeDtypeStruct((T, D_MODEL), SAVE_DTYPE),
                   jax.ShapeDtypeStruct((2, 4, T, FF_SHARD), SAVE_DTYPE)],
        in_specs=[tok, _resident(prm), _resident(win), _resident(wout)],
        out_specs=[tok, tok, pl.BlockSpec((2, 4, tm, FF_SHARD), lambda i: (0, 0, i, 0))],
        operands=(x, prm, win, wout), carry=carry, steps=_grid_steps(T // tm))


def _ffn_backward(x, d, ab, prm, win, wout, sub, g_row, name, carry=None):
    T = x.shape[0]
    tm = min(T, TM_FFN)
    nt = T // tm

    def body(x_ref, d_ref, ab_ref, prm_ref, win_ref, wout_ref, dh_ref, dwin_ref, dwout_ref, acc_in, acc_out):
        i = pl.program_id(1)

        @pl.when(i == 0)
        def _():
            acc_in[...] = jnp.zeros_like(acc_in)
            acc_out[...] = jnp.zeros_like(acc_out)

        h, _, _ = _modulated(x_ref[...], prm_ref, sub, g_row)
        hb = h.astype(MXU_DTYPE)
        wa, wb, wo = win_ref[0, 0], win_ref[1, 0], wout_ref[0]
        a = ab_ref[0, 0].astype(F32)
        b = ab_ref[1, 0].astype(F32)
        sg = _sigmoid(a)
        si = a * sg
        dfs = ((0.5 * _row(prm_ref, 3 * sub + 2)) * d_ref[...]).astype(MXU_DTYPE)
        ds = _mm_nt(dfs, wo)
        acc_out[...] += _mm_tn(si * b, dfs)
        da = ds * b * (sg * (1.0 + a * (1.0 - sg)))
        db = ds * si
        acc_in[0] += _mm_tn(hb, da)
        acc_in[1] += _mm_tn(hb, db)
        dh_ref[0] = (_mm_nt(da, wa) + _mm_nt(db, wb)).astype(SAVE_DTYPE)

        @pl.when(i == nt - 1)
        def _():
            dwin_ref[0, 0] = acc_in[0].astype(WIRE_DTYPE)
            dwin_ref[1, 0] = acc_in[1].astype(WIRE_DTYPE)
            dwout_ref[0] = acc_out[...].astype(WIRE_DTYPE)

    def steps():
        j, i = pl.program_id(0), pl.program_id(1)
        return (j == 0) & (i == 0), (j == 2) & (i == 0), (j == 3) & (i == nt - 1)

    tok = pl.BlockSpec((tm, D_MODEL), lambda j, i: (i, 0))
    return _launch(
        body, name, grid=(4, nt), semantics=("arbitrary", "arbitrary"),
        out_shape=[jax.ShapeDtypeStruct((4, T, D_MODEL), SAVE_DTYPE),
                   jax.ShapeDtypeStruct(win.shape, WIRE_DTYPE), jax.ShapeDtypeStruct(wout.shape, WIRE_DTYPE)],
        in_specs=[tok, tok, pl.BlockSpec((2, 1, tm, FF_SHARD), lambda j, i: (0, j, i, 0)), _resident(prm),
                  pl.BlockSpec((2, 1, D_MODEL, FF_SHARD), lambda j, i: (0, j, 0, 0)),
                  pl.BlockSpec((1, FF_SHARD, D_MODEL), lambda j, i: (j, 0, 0))],
        out_specs=[pl.BlockSpec((1, tm, D_MODEL), lambda j, i: (j, i, 0)),
                   pl.BlockSpec((2, 1, D_MODEL, FF_SHARD), lambda j, i: (0, j, 0, 0)),
                   pl.BlockSpec((1, FF_SHARD, D_MODEL), lambda j, i: (j, 0, 0))],
        operands=(x, d, ab, prm, win, wout),
        scratch=[pltpu.VMEM((2, D_MODEL, FF_SHARD), F32), pltpu.VMEM((FF_SHARD, D_MODEL), F32)],
        carry=carry, steps=steps)


def _norm_backward(parts, x, d, f, prm, sub, g_row, gate_coef, name, carry=None):
    T = x.shape[0]
    tm = min(T, TM_EW)
    P = parts.shape[0]

    def body(p_ref, x_ref, d_ref, f_ref, prm_ref, dx_ref, sums_ref):
        i = pl.program_id(0)
        dh = p_ref[0].astype(F32)
        for k in range(1, P):
            dh = dh + p_ref[k].astype(F32)
        xv, dv = x_ref[...], d_ref[...]
        scale, g = _row(prm_ref, 3 * sub + 1), _row(prm_ref, g_row)
        r = _rms_scale(xv)
        n0 = xv * r
        dn = dh * (1.0 + scale)
        dn0 = dn * g
        dx_ref[...] = dv + r * (dn0 - n0 * jnp.mean(dn0 * n0, axis=-1, keepdims=True))
        upd = jnp.concatenate([_colsum(dn * n0), _colsum(dh), _colsum(dh * (n0 * g)),
                               gate_coef * _colsum(dv * f_ref[...].astype(F32)), jnp.zeros((4, D_MODEL), F32)], axis=0)

        @pl.when(i == 0)
        def _():
            sums_ref[...] = upd

        @pl.when(i > 0)
        def _():
            sums_ref[...] += upd

    tok = pl.BlockSpec((tm, D_MODEL), lambda i: (i, 0))
    return _launch(
        body, name, grid=(T // tm,), semantics=("arbitrary",),
        out_shape=[jax.ShapeDtypeStruct((T, D_MODEL), F32), jax.ShapeDtypeStruct((8, D_MODEL), F32)],
        in_specs=[pl.BlockSpec((P, tm, D_MODEL), lambda i: (0, i, 0)), tok, tok, tok, _resident(prm)],
        out_specs=[tok, pl.BlockSpec((8, D_MODEL), lambda i: (0, 0))],
        operands=(parts, x, d, f, prm), carry=carry, steps=_grid_steps(T // tm))


def _final_loss(x, target, prm):
    T = x.shape[0]
    tm = min(T, TM_EW)

    def body(x_ref, t_ref, prm_ref, dx_ref, sums_ref):
        i = pl.program_id(0)
        xv = x_ref[...]
        g = _row(prm_ref, ROW_G_FINAL)
        r = _rms_scale(xv)
        n0 = xv * r
        err = n0 * g - t_ref[...]
        dy = err / float(D_MODEL)
        dn0 = dy * g
        dx_ref[...] = r * (dn0 - n0 * jnp.mean(dn0 * n0, axis=-1, keepdims=True))
        loss = 0.5 * jnp.sum(jnp.mean(err * err, axis=-1, keepdims=True), axis=0, keepdims=True)
        upd = jnp.concatenate([_colsum(dy * n0), jnp.broadcast_to(loss, (1, D_MODEL)), jnp.zeros((6, D_MODEL), F32)], axis=0)

        @pl.when(i == 0)
        def _():
            sums_ref[...] = upd

        @pl.when(i > 0)
        def _():
            sums_ref[...] += upd

    tok = pl.BlockSpec((tm, D_MODEL), lambda i: (i, 0))
    return pl.pallas_call(
        body, name="final_loss", grid=(T // tm,),
        out_shape=[jax.ShapeDtypeStruct((T, D_MODEL), F32), jax.ShapeDtypeStruct((8, D_MODEL), F32)],
        in_specs=[tok, tok, pl.BlockSpec(prm.shape, lambda i: (0, 0))],
        out_specs=[tok, pl.BlockSpec((8, D_MODEL), lambda i: (0, 0))],
        compiler_params=_params("arbitrary"),
    )(x, target, prm)


def _ssm_discretise(lam_re_log, lam_im, log_dt):
    lr = -jnp.exp(lam_re_log)
    dt = jnp.exp(log_dt)
    mag = jnp.exp(lr * dt)
    ang = lam_im * dt
    ab_re = mag * jnp.cos(ang)
    ab_im = mag * jnp.sin(ang)
    num_re = ab_re - 1.0
    num_im = ab_im
    den = lr * lr + lam_im * lam_im
    f_re = (num_re * lr + num_im * lam_im) / den
    f_im = (num_im * lr - num_re * lam_im) / den
    return ab_re, ab_im, f_re, f_im


def _ssm_params_forward(lam_re_log, lam_im, log_dt):
    def body(a_ref, b_ref, c_ref, o0, o1, o2, o3):
        outs = _ssm_discretise(a_ref[...], b_ref[...], c_ref[...])
        for o, v in zip((o0, o1, o2, o3), outs):
            o[...] = v

    return pl.pallas_call(body, name="ssm_params_forward",
                          out_shape=[jax.ShapeDtypeStruct(lam_im.shape, F32)] * 4)(lam_re_log, lam_im, log_dt)


def _ssm_params_backward(lam_re_log, lam_im, log_dt, cot):
    def body(a_ref, b_ref, c_ref, g0, g1, g2, g3, o0, o1, o2):
        _, vjp = jax.vjp(_ssm_discretise, a_ref[...], b_ref[...], c_ref[...])
        d0, d1, d2 = vjp((g0[...], g1[...], g2[...], g3[...]))
        o0[...] = d0
        o1[...] = d1
        o2[...] = d2

    return pl.pallas_call(
        body, name="ssm_params_backward",
        out_shape=[jax.ShapeDtypeStruct(lam_im.shape, F32), jax.ShapeDtypeStruct(lam_im.shape, F32),
                   jax.ShapeDtypeStruct(log_dt.shape, F32)])(lam_re_log, lam_im, log_dt, *cot)


def _ssm_dense_forward(srow, b_dense, c_dense):
    def body(srow_ref, bd_ref, cd_ref, bb_ref, ct_ref):
        for j in range(SSM_BLOCKS):
            lanes = slice(j * SSM_BLOCK_STATE, (j + 1) * SSM_BLOCK_STATE)
            f_re, f_im = srow_ref[2:3, lanes], srow_ref[3:4, lanes]
            bb_ref[0, j] = (f_re * bd_ref[0, j] - f_im * bd_ref[1, j]).astype(MXU_DTYPE)
            bb_ref[1, j] = (f_re * bd_ref[1, j] + f_im * bd_ref[0, j]).astype(MXU_DTYPE)
            ct_ref[0, j] = cd_ref[0, j].astype(MXU_DTYPE)
            ct_ref[1, j] = cd_ref[1, j].astype(MXU_DTYPE)

    return pl.pallas_call(body, name="ssm_dense_forward",
                          out_shape=[jax.ShapeDtypeStruct(b_dense.shape, MXU_DTYPE),
                                     jax.ShapeDtypeStruct(c_dense.shape, MXU_DTYPE)],
                          compiler_params=pltpu.CompilerParams(vmem_limit_bytes=VMEM_LIMIT))(srow, b_dense, c_dense)


def _cmul(p, q):
    return p[0] * q[0] - p[1] * q[1], p[0] * q[1] + p[1] * q[0]


def _scan_coefficients(ar, ai, reverse):
    n = ar.shape[1]
    p = {1: (ar, ai)}
    p[2] = _cmul(p[1], p[1])
    p[3] = _cmul(p[2], p[1])
    p[4] = _cmul(p[2], p[2])
    p[5] = _cmul(p[4], p[1])
    p[6] = _cmul(p[4], p[2])
    p[7] = _cmul(p[4], p[3])
    p[8] = _cmul(p[4], p[4])
    rowi = lax.broadcasted_iota(jnp.int32, (SCAN_ROWS, n), 0)
    tiles = []
    for dstep in (1, 2, 4):
        keep = (rowi < SCAN_ROWS - dstep) if reverse else (rowi >= dstep)
        for part in p[dstep]:
            tiles.append(jnp.where(keep, jnp.broadcast_to(part, (SCAN_ROWS, n)), 0.0))
    for comp in (0, 1):
        t = jnp.zeros((SCAN_ROWS, n), F32)
        for rr in range(SCAN_ROWS):
            power = SCAN_ROWS - rr if reverse else rr + 1
            t = jnp.where(rowi == rr, jnp.broadcast_to(p[power][comp], (SCAN_ROWS, n)), t)
        tiles.append(t)
    return tiles


def _load_stack(stack_hbm, dst, sems, base):
    cols = stack_hbm.shape[2]
    cps = [pltpu.make_async_copy(stack_hbm.at[k], dst.at[:, pl.ds(k * cols, cols)], sems.at[base + k])
           for k in range(NDEV)]
    for cp in cps:
        cp.start()
    return cps


def _window_lanes():
    lane = lax.broadcasted_iota(jnp.int32, (1, POOL_WIDTH), 1)
    return jnp.where(lane < 128, 2.0, jnp.where(lane < 256, 4.0, jnp.where(lane < 384, 8.0, 16.0)))


def _gelu(y):
    return 0.5 * y * (1.0 + lax.erf(y * 0.7071067811865476))


def _gelu_grad(y):
    return 0.5 * (1.0 + lax.erf(y * 0.7071067811865476)) + y * jnp.exp(-0.5 * y * y) * 0.3989422804014327


def _mixer_forward(x, prm, w_in_s, w_pu_s, w_glu_s, w_su_s, w_out, pool_w, mvec, srow, bb, ct, carry=None):
    T = x.shape[0]
    tm = min(T, TM_MIX)
    nt = T // tm
    n_tiles = tm // SCAN_ROWS

    def body(x_ref, prm_ref, w_in_h, w_pu_h, w_glu_h, w_su_h, w_out_h, pw_ref, mv_ref, srow_ref, bb, ct,
             x2_ref, mo_ref, z_ref, sre_ref, sim_ref, zp_ref, q_ref, yp_ref, yss_ref, vg_ref, ys_ref,
             w_in, w_pu, w_glu, w_su, w_o, coef, carry, hist, bu, sems):
        i = pl.program_id(0)

        @pl.when(i == 0)
        def _():
            cps = (_load_stack(w_in_h, w_in, sems, 0) + _load_stack(w_pu_h, w_pu, sems, 8)
                   + _load_stack(w_glu_h, w_glu, sems, 16) + _load_stack(w_su_h, w_su, sems, 24))
            cps.append(pltpu.make_async_copy(w_out_h, w_o, sems.at[32]))
            cps[-1].start()
            for j in range(SSM_BLOCKS):
                lanes = slice(j * SSM_BLOCK_STATE, (j + 1) * SSM_BLOCK_STATE)
                for k, t in enumerate(_scan_coefficients(srow_ref[0:1, lanes], srow_ref[1:2, lanes], False)):
                    coef[j, k] = t
            carry[...] = jnp.zeros_like(carry)
            hist[...] = jnp.zeros_like(hist)
            for cp in cps:
                cp.wait()

        xv = x_ref[...]
        h, _, _ = _modulated(xv, prm_ref, 1, ROW_G_MIX)
        z = _mm(h, w_in[...])
        z_ref[...] = z.astype(SAVE_DTYPE)
        u_pool, u_ssm = z[:, 0:512], z[:, 512:1024]
        gl_pool, gl_ssm = z[:, 1024:2048], z[:, 2048:3072]

        ext = jnp.concatenate([hist[...], u_pool], axis=0)
        w2 = ext + pltpu.roll(ext, 1, 0)
        w4 = w2[:, 128:] + pltpu.roll(w2[:, 128:], 2, 0)
        w8 = w4[:, 128:] + pltpu.roll(w4[:, 128:], 4, 0)
        w16 = w8[:, 128:] + pltpu.roll(w8[:, 128:], 8, 0)
        wsum = jnp.concatenate([w2[POOL_HALO:, :128], w4[POOL_HALO:, :128], w8[POOL_HALO:, :128], w16[POOL_HALO:]], axis=1)
        hist[...] = u_pool[tm - POOL_HALO:, :]
        t1 = (lax.broadcasted_iota(jnp.int32, (tm, 1), 0) + (i * tm + 1)).astype(F32)
        zp = wsum / jnp.minimum(t1, _window_lanes()) - u_pool
        zp_ref[...] = zp.astype(SAVE_DTYPE)
        q = jnp.concatenate([_mm(zp[:, k * 128:(k + 1) * 128], pw_ref[k]) for k in range(4)], axis=1)
        q = q + mv_ref[ROW_POOL_B:ROW_POOL_B + 1, 0:512]
        q_ref[...] = q.astype(SAVE_DTYPE)
        y_pool = _mm(q * mv_ref[ROW_POOL_SCALE:ROW_POOL_SCALE + 1, 0:512], w_pu[...])
        yp_ref[...] = y_pool.astype(SAVE_DTYPE)

        y_blocks = []
        for j in range(SSM_BLOCKS):
            lanes = pl.ds(j * SSM_BLOCK_STATE, SSM_BLOCK_STATE)
            ub = u_ssm[:, j * 128:(j + 1) * 128].astype(MXU_DTYPE)
            bu[0] = _mm(ub, bb[0, j])
            bu[1] = _mm(ub, bb[1, j])
            a1r, a1i, a2r, a2i, a4r, a4i, pr, pi = [coef[j, k] for k in range(8)]

            def step(tt, c, lanes=lanes, a1r=a1r, a1i=a1i, a2r=a2r, a2i=a2i, a4r=a4r, a4i=a4i, pr=pr, pi=pi):
                cr, ci = c
                rows = pl.ds(pl.multiple_of(tt * SCAN_ROWS, SCAN_ROWS), SCAN_ROWS)
                xr, xi = bu[0, rows, :], bu[1, rows, :]
                for dstep, kr, ki in ((1, a1r, a1i), (2, a2r, a2i), (4, a4r, a4i)):
                    sr, si = pltpu.roll(xr, dstep, 0), pltpu.roll(xi, dstep, 0)
                    xr, xi = xr + kr * sr - ki * si, xi + kr * si + ki * sr
                xr, xi = xr + pr * cr - pi * ci, xi + pr * ci + pi * cr
                sre_ref[rows, lanes] = xr
                sim_ref[rows, lanes] = xi
                return (jnp.broadcast_to(xr[SCAN_ROWS - 1:SCAN_ROWS, :], xr.shape),
                        jnp.broadcast_to(xi[SCAN_ROWS - 1:SCAN_ROWS, :], xi.shape))

            cr, ci = lax.fori_loop(0, n_tiles, step, (carry[j, 0], carry[j, 1]))
            carry[j, 0] = cr
            carry[j, 1] = ci
            y_blocks.append(_mm(sre_ref[:, lanes], ct[0, j]) - _mm(sim_ref[:, lanes], ct[1, j]))
        yss = jnp.concatenate(y_blocks, axis=1) + mv_ref[ROW_SSM_D:ROW_SSM_D + 1, 0:512] * u_ssm
        yss_ref[...] = yss.astype(SAVE_DTYPE)
        vg = _mm(_gelu(yss), w_glu[...]) + mv_ref[ROW_B_GLU:ROW_B_GLU + 1, :]
        vg_ref[...] = vg.astype(SAVE_DTYPE)
        y_ssm = _mm(vg[:, 0:512] * _sigmoid(vg[:, 512:1024]), w_su[...])
        ys_ref[...] = y_ssm.astype(SAVE_DTYPE)

        merged = _sigmoid(gl_pool) * y_pool + _sigmoid(gl_ssm) * y_ssm
        mo = _mm(merged, w_o[...])
        mo_ref[...] = mo.astype(SAVE_DTYPE)
        x2_ref[...] = xv + _row(prm_ref, 5) * mo

    def tok(width):
        return pl.BlockSpec((tm, width), lambda i: (i, 0))

    hbm = _HBM
    widths = (D_MODEL, D_MODEL, IN_WIDTH, N_STATE, N_STATE, 512, 512, D_MODEL, 512, D_MODEL, D_MODEL)
    dtypes = (F32, SAVE_DTYPE, SAVE_DTYPE, F32, F32) + (SAVE_DTYPE,) * 6
    return _launch(
        body, "mixer_forward", grid=(nt,), semantics=("arbitrary",), carry=carry, steps=_grid_steps(nt),
        out_shape=[jax.ShapeDtypeStruct((T, w), dt) for w, dt in zip(widths, dtypes)],
        in_specs=[tok(D_MODEL), _resident(prm), hbm, hbm, hbm, hbm, hbm, _resident(pool_w), _resident(mvec),
                  _resident(srow), _resident(bb), _resident(ct)],
        out_specs=[tok(w) for w in widths],
        operands=(x, prm, w_in_s, w_pu_s, w_glu_s, w_su_s, w_out, pool_w, mvec, srow, bb, ct),
        scratch=[
            pltpu.VMEM((D_MODEL, IN_WIDTH), MXU_DTYPE), pltpu.VMEM((512, D_MODEL), MXU_DTYPE),
            pltpu.VMEM((512, D_MODEL), MXU_DTYPE), pltpu.VMEM((512, D_MODEL), MXU_DTYPE),
            pltpu.VMEM((D_MODEL, D_MODEL), MXU_DTYPE),
            pltpu.VMEM((SSM_BLOCKS, 8, SCAN_ROWS, SSM_BLOCK_STATE), F32),
            pltpu.VMEM((SSM_BLOCKS, 2, SCAN_ROWS, SSM_BLOCK_STATE), F32),
            pltpu.VMEM((POOL_HALO, POOL_WIDTH), F32),
            pltpu.VMEM((2, tm, SSM_BLOCK_STATE), F32),
            pltpu.SemaphoreType.DMA((33,)),
        ])


def _mixer_backward(d2, prm, saved, w_pu_s, w_glu_s, w_su_s, w_out, pool_w, mvec, srow, bb, ct, carry=None):
    z, s_re, s_im, zp, q, y_pool, yss, vg, y_ssm = saved
    T = d2.shape[0]
    tm = min(T, TM_MIX_BWD)
    nt = T // tm
    n_tiles = tm // SCAN_ROWS

    def body(d_ref, prm_ref, z_ref, sre_ref, sim_ref, zp_ref, q_ref, yp_ref, yss_ref, vg_ref, ys_ref,
             w_pu_h, w_glu_h, w_su_h, w_out_h, pw_ref, mv_ref, srow_ref, bb, ct,
             dz_ref, dwo_h, dwpu_h, dwglu_h, dwsu_h, dpw_h, dbb_h, dct_h, vsum_h, da_h,
             w_pu, w_glu, w_su, w_o, pwb, coef, carry, hist, dre, lam,
             a_wo, a_wpu, a_wglu, a_wsu, a_pw, a_bb, a_ct, a_vs, a_da, st_wo, st_up, sems):
        i = pl.program_id(0)
        tile = nt - 1 - i

        @pl.when(i == 0)
        def _():
            cps = (_load_stack(w_pu_h, w_pu, sems, 0) + _load_stack(w_glu_h, w_glu, sems, 8)
                   + _load_stack(w_su_h, w_su, sems, 16))
            cps.append(pltpu.make_async_copy(w_out_h, w_o, sems.at[24]))
            cps[-1].start()
            pwb[...] = pw_ref[...].astype(MXU_DTYPE)
            for j in range(SSM_BLOCKS):
                lanes = slice(j * SSM_BLOCK_STATE, (j + 1) * SSM_BLOCK_STATE)
                for k, t in enumerate(_scan_coefficients(srow_ref[0:1, lanes], srow_ref[1:2, lanes], True)):
                    coef[j, k] = t
            for acc in (carry, hist, a_wo, a_wpu, a_wglu, a_wsu, a_pw, a_bb, a_ct, a_vs, a_da):
                acc[...] = jnp.zeros_like(acc)
            for cp in cps:
                cp.wait()

        dv = d_ref[...]
        zt = z_ref[...].astype(F32)
        u_ssm, gl_pool, gl_ssm = zt[:, 512:1024], zt[:, 1024:2048], zt[:, 2048:3072]
        y_p, y_s = yp_ref[...].astype(F32), ys_ref[...].astype(F32)
        sgp, sgs = _sigmoid(gl_pool), _sigmoid(gl_ssm)
        dmo = (_row(prm_ref, 5) * dv).astype(MXU_DTYPE)
        a_wo[...] += _mm_tn(sgp * y_p + sgs * y_s, dmo)
        dmerged = _mm_nt(dmo, w_o[...])
        dy_pool = dmerged * sgp
        dgl_pool = dmerged * y_p * (sgp * (1.0 - sgp))
        dy_ssm = dmerged * sgs
        dgl_ssm = dmerged * y_s * (sgs * (1.0 - sgs))

        scale = mv_ref[ROW_POOL_SCALE:ROW_POOL_SCALE + 1, 0:512]
        qv, zpv = q_ref[...].astype(F32), zp_ref[...]
        a_wpu[...] += _mm_tn(qv * scale, dy_pool)
        dp = _mm_nt(dy_pool, w_pu[...])
        dq = dp * scale
        a_vs[0:1, 0:512] += _colsum(dp * qv)
        a_vs[1:2, 0:512] += _colsum(dq)
        dzp_blocks = []
        for k in range(4):
            lanes = slice(k * 128, (k + 1) * 128)
            dzp_blocks.append(_mm_nt(dq[:, lanes], pwb[k]))
            a_pw[k] += _mm_tn(zpv[:, lanes], dq[:, lanes])
        dzp = jnp.concatenate(dzp_blocks, axis=1)
        t1 = (lax.broadcasted_iota(jnp.int32, (tm, 1), 0) + (tile * tm + 1)).astype(F32)
        gs = dzp / jnp.minimum(t1, _window_lanes())
        n_ext = tm + POOL_HALO
        ext = jnp.concatenate([gs, hist[...]], axis=0)
        v2 = ext + pltpu.roll(ext, n_ext - 1, 0)
        v4 = v2[:, 128:] + pltpu.roll(v2[:, 128:], n_ext - 2, 0)
        v8 = v4[:, 128:] + pltpu.roll(v4[:, 128:], n_ext - 4, 0)
        v16 = v8[:, 128:] + pltpu.roll(v8[:, 128:], n_ext - 8, 0)
        msum = jnp.concatenate([v2[:tm, :128], v4[:tm, :128], v8[:tm, :128], v16[:tm]], axis=1)
        hist[...] = gs[0:POOL_HALO, :]
        du_pool = msum - dzp

        vgv = vg_ref[...].astype(F32)
        val, gate = vgv[:, 0:512], vgv[:, 512:1024]
        sgg = _sigmoid(gate)
        a_wsu[...] += _mm_tn(val * sgg, dy_ssm)
        do = _mm_nt(dy_ssm, w_su[...])
        dvg = jnp.concatenate([do * sgg, do * val * (sgg * (1.0 - sgg))], axis=1)
        a_vs[3:4, :] += _colsum(dvg)
        yv = yss_ref[...].astype(F32)
        a_wglu[...] += _mm_tn(_gelu(yv), dvg)
        dyss = _mm_nt(dvg, w_glu[...]) * _gelu_grad(yv)
        a_vs[2:3, 0:512] += _colsum(dyss * u_ssm)
        du_blocks = []
        for j in range(SSM_BLOCKS):
            lanes = pl.ds(j * SSM_BLOCK_STATE, SSM_BLOCK_STATE)
            in_lanes = slice(j * 128, (j + 1) * 128)
            dyb = dyss[:, in_lanes].astype(MXU_DTYPE)
            ub = u_ssm[:, in_lanes].astype(MXU_DTYPE)
            dre[0] = _mm_nt(dyb, ct[0, j])
            dre[1] = -_mm_nt(dyb, ct[1, j])
            a_ct[0, j] += _mm_tn(sre_ref[:, lanes], dyb)
            a_ct[1, j] -= _mm_tn(sim_ref[:, lanes], dyb)
            a1r, a1i, a2r, a2i, a4r, a4i, pr, pi = [coef[j, k] for k in range(8)]
            rowi = lax.broadcasted_iota(jnp.int32, (SCAN_ROWS, SSM_BLOCK_STATE), 0)

            def step(tt, c, lanes=lanes, a1r=a1r, a1i=a1i, a2r=a2r, a2i=a2i, a4r=a4r, a4i=a4i, pr=pr, pi=pi, rowi=rowi):
                cr, ci, acc_r, acc_i = c
                rows = pl.ds(pl.multiple_of((n_tiles - 1 - tt) * SCAN_ROWS, SCAN_ROWS), SCAN_ROWS)
                xr, xi = dre[0, rows, :], dre[1, rows, :]
                for dstep, kr, ki in ((1, a1r, a1i), (2, a2r, a2i), (4, a4r, a4i)):
                    sr, si = pltpu.roll(xr, SCAN_ROWS - dstep, 0), pltpu.roll(xi, SCAN_ROWS - dstep, 0)
                    xr, xi = xr + kr * sr + ki * si, xi + kr * si - ki * sr
                xr, xi = xr + pr * cr + pi * ci, xi + pr * ci - pi * cr
                lam[0, rows, :] = xr
                lam[1, rows, :] = xi
                nr = jnp.where(rowi == SCAN_ROWS - 1, cr, pltpu.roll(xr, SCAN_ROWS - 1, 0))
                ni = jnp.where(rowi == SCAN_ROWS - 1, ci, pltpu.roll(xi, SCAN_ROWS - 1, 0))
                s_r, s_i = sre_ref[rows, lanes], sim_ref[rows, lanes]
                acc_r = acc_r + nr * s_r + ni * s_i
                acc_i = acc_i + ni * s_r - nr * s_i
                return (jnp.broadcast_to(xr[0:1, :], xr.shape), jnp.broadcast_to(xi[0:1, :], xi.shape), acc_r, acc_i)

            cr, ci, acc_r, acc_i = lax.fori_loop(0, n_tiles, step, (carry[j, 0], carry[j, 1], a_da[0, j], a_da[1, j]))
            carry[j, 0] = cr
            carry[j, 1] = ci
            a_da[0, j] = acc_r
            a_da[1, j] = acc_i
            lr_b, li_b = lam[0].astype(MXU_DTYPE), lam[1].astype(MXU_DTYPE)
            a_bb[0, j] += _mm_tn(ub, lr_b)
            a_bb[1, j] += _mm_tn(ub, li_b)
            du_blocks.append(_mm_nt(lr_b, bb[0, j]) + _mm_nt(li_b, bb[1, j]))
        du_ssm = jnp.concatenate(du_blocks, axis=1) + dyss * mv_ref[ROW_SSM_D:ROW_SSM_D + 1, 0:512]
        dz_ref[...] = jnp.concatenate([du_pool, du_ssm, dgl_pool, dgl_ssm], axis=1).astype(SAVE_DTYPE)

        @pl.when(i == nt - 1)
        def _():
            rows = D_MODEL // NDEV
            for k in range(NDEV):
                st_wo[k] = a_wo[k * rows:(k + 1) * rows, :].astype(WIRE_DTYPE)
                for a, acc in enumerate((a_wpu, a_wglu, a_wsu)):
                    st_up[a, k] = acc[:, k * 128:(k + 1) * 128].astype(WIRE_DTYPE)
            outs = ((st_wo, dwo_h), (st_up.at[0], dwpu_h), (st_up.at[1], dwglu_h), (st_up.at[2], dwsu_h),
                    (a_pw, dpw_h), (a_bb, dbb_h), (a_ct, dct_h), (a_vs, vsum_h), (a_da, da_h))
            cps = [pltpu.make_async_copy(src, dst, sems.at[k]) for k, (src, dst) in enumerate(outs)]
            for cp in cps:
                cp.start()
            for cp in cps:
                cp.wait()

    def tok(width):
        return pl.BlockSpec((tm, width), lambda i: (nt - 1 - i, 0))

    hbm = _HBM
    acc_shapes = [(D_MODEL, D_MODEL), (512, D_MODEL), (512, D_MODEL), (512, D_MODEL), (4, 128, 128),
                  (2, SSM_BLOCKS, 128, SSM_BLOCK_STATE), (2, SSM_BLOCKS, SSM_BLOCK_STATE, 128), (8, D_MODEL),
                  (2, SSM_BLOCKS, SCAN_ROWS, SSM_BLOCK_STATE)]
    stack_out = [jax.ShapeDtypeStruct((NDEV, D_MODEL // NDEV, D_MODEL), WIRE_DTYPE)] \
        + [jax.ShapeDtypeStruct((NDEV, 512, 128), WIRE_DTYPE)] * 3
    return _launch(
        body, "mixer_backward", grid=(nt,), semantics=("arbitrary",), carry=carry, steps=_grid_steps(nt),
        out_shape=[jax.ShapeDtypeStruct((T, IN_WIDTH), SAVE_DTYPE)] + stack_out
        + [jax.ShapeDtypeStruct(s, F32) for s in acc_shapes[4:]],
        in_specs=[tok(D_MODEL), _resident(prm), tok(IN_WIDTH), tok(N_STATE), tok(N_STATE), tok(512), tok(512),
                  tok(D_MODEL), tok(512), tok(D_MODEL), tok(D_MODEL), hbm, hbm, hbm, hbm, _resident(pool_w),
                  _resident(mvec), _resident(srow), _resident(bb), _resident(ct)],
        out_specs=[tok(IN_WIDTH)] + [hbm] * len(acc_shapes),
        operands=(d2, prm, z, s_re, s_im, zp, q, y_pool, yss, vg, y_ssm, w_pu_s, w_glu_s, w_su_s, w_out, pool_w, mvec,
                  srow, bb, ct),
        scratch=[
            pltpu.VMEM((512, D_MODEL), MXU_DTYPE), pltpu.VMEM((512, D_MODEL), MXU_DTYPE),
            pltpu.VMEM((512, D_MODEL), MXU_DTYPE), pltpu.VMEM((D_MODEL, D_MODEL), MXU_DTYPE),
            pltpu.VMEM((4, 128, 128), MXU_DTYPE),
            pltpu.VMEM((SSM_BLOCKS, 8, SCAN_ROWS, SSM_BLOCK_STATE), F32),
            pltpu.VMEM((SSM_BLOCKS, 2, SCAN_ROWS, SSM_BLOCK_STATE), F32),
            pltpu.VMEM((POOL_HALO, POOL_WIDTH), F32),
            pltpu.VMEM((2, tm, SSM_BLOCK_STATE), F32), pltpu.VMEM((2, tm, SSM_BLOCK_STATE), F32),
        ] + [pltpu.VMEM(s, F32) for s in acc_shapes]
        + [pltpu.VMEM((NDEV, D_MODEL // NDEV, D_MODEL), WIRE_DTYPE), pltpu.VMEM((3, NDEV, 512, 128), WIRE_DTYPE),
           pltpu.SemaphoreType.DMA((25,))])


def _mixer_in_backward(x, dz, prm, w_in_s):
    T = x.shape[0]
    tm = min(T, TM_MIX)
    nt = T // tm
    cols = IN_WIDTH // NDEV

    def body(x_ref, dz_ref, prm_ref, w_in_h, dh_ref, dw_ref, w_in, acc, sems):
        i = pl.program_id(0)

        @pl.when(i == 0)
        def _():
            cps = _load_stack(w_in_h, w_in, sems, 0)
            acc[...] = jnp.zeros_like(acc)
            for cp in cps:
                cp.wait()

        h, _, _ = _modulated(x_ref[...], prm_ref, 1, ROW_G_MIX)
        dzb = dz_ref[...].astype(MXU_DTYPE)
        dh_ref[0] = _mm_nt(dzb, w_in[...]).astype(SAVE_DTYPE)
        acc[...] += _mm_tn(h, dzb)

        @pl.when(i == nt - 1)
        def _():
            for k in range(NDEV):
                dw_ref[k] = acc[:, k * cols:(k + 1) * cols].astype(WIRE_DTYPE)

    return pl.pallas_call(
        body, name="mixer_in_backward", grid=(nt,),
        out_shape=[jax.ShapeDtypeStruct((1, T, D_MODEL), SAVE_DTYPE), jax.ShapeDtypeStruct((NDEV, D_MODEL, cols), WIRE_DTYPE)],
        in_specs=[pl.BlockSpec((tm, D_MODEL), lambda i: (i, 0)), pl.BlockSpec((tm, IN_WIDTH), lambda i: (i, 0)),
                  pl.BlockSpec(prm.shape, lambda i: (0, 0)), pl.BlockSpec(memory_space=pl.ANY)],
        out_specs=[pl.BlockSpec((1, tm, D_MODEL), lambda i: (0, i, 0)),
                   pl.BlockSpec((NDEV, D_MODEL, cols), lambda i: (0, 0, 0))],
        scratch_shapes=[pltpu.VMEM((D_MODEL, IN_WIDTH), MXU_DTYPE), pltpu.VMEM((D_MODEL, IN_WIDTH), F32),
                        pltpu.SemaphoreType.DMA((8,))],
        compiler_params=_params("arbitrary"),
    )(x, dz, prm, w_in_s)


def _ssm_dense_backward(dbb, da, srow, b_dense):
    def body(dbb_ref, da_ref, srow_ref, bd_ref, db_ref, df_ref):
        df_re, df_im = [], []
        da_re = [_colsum(da_ref[0, j]) for j in range(SSM_BLOCKS)]
        da_im = [_colsum(da_ref[1, j]) for j in range(SSM_BLOCKS)]
        for j in range(SSM_BLOCKS):
            lanes = slice(j * SSM_BLOCK_STATE, (j + 1) * SSM_BLOCK_STATE)
            f_re, f_im = srow_ref[2:3, lanes], srow_ref[3:4, lanes]
            g_re, g_im = dbb_ref[0, j], dbb_ref[1, j]
            b_re, b_im = bd_ref[0, j], bd_ref[1, j]
            db_ref[0, j] = f_re * g_re + f_im * g_im
            db_ref[1, j] = f_re * g_im - f_im * g_re
            df_re.append(_colsum(g_re * b_re + g_im * b_im))
            df_im.append(_colsum(g_im * b_re - g_re * b_im))
        df_ref[...] = jnp.concatenate([jnp.concatenate(df_re, axis=1), jnp.concatenate(df_im, axis=1),
                                       jnp.concatenate(da_re, axis=1), jnp.concatenate(da_im, axis=1),
                                       jnp.zeros((4, N_STATE), F32)], axis=0)

    return pl.pallas_call(body, name="ssm_dense_backward",
                          out_shape=[jax.ShapeDtypeStruct(b_dense.shape, F32), jax.ShapeDtypeStruct((8, N_STATE), F32)],
                          compiler_params=pltpu.CompilerParams(vmem_limit_bytes=VMEM_LIMIT))(dbb, da, srow, b_dense)


def _adamw_update(w, g, m, v):
    m = ADAM_B1 * m + (1.0 - ADAM_B1) * g
    v = ADAM_B2 * v + (1.0 - ADAM_B2) * (g * g)
    m_hat = m / (1.0 - ADAM_B1 ** ADAM_STEP)
    v_hat = v / (1.0 - ADAM_B2 ** ADAM_STEP)
    delta = -ADAM_LR * (m_hat / (jnp.sqrt(v_hat) + ADAM_EPS) + ADAM_WD * w)
    return delta, m, v


def _adam_rows(shape):
    rows, cols = shape
    tr = rows
    while tr * cols * 4 > (1 << 20) and tr % 16 == 0:
        tr //= 2
    return tr


def _adam_sharded(w, m, v, land, name):
    R, C = w.shape
    tr = _adam_rows((R, C))

    def body(w_ref, m_ref, v_ref, land_ref, g_ref, d_ref, mo_ref, vo_ref):
        g = land_ref[0].astype(F32)
        for b in range(1, NDEV):
            g = g + land_ref[b].astype(F32)
        g_ref[...] = g
        d_ref[...], mo_ref[...], vo_ref[...] = _adamw_update(w_ref[...], g, m_ref[...], v_ref[...])

    blk = pl.BlockSpec((tr, C), lambda i: (i, 0))
    return pl.pallas_call(
        body, name=name, grid=(R // tr,),
        out_shape=[jax.ShapeDtypeStruct((R, C), F32)] * 4,
        in_specs=[blk, blk, blk, pl.BlockSpec((NDEV, tr, C), lambda i: (0, i, 0))],
        out_specs=[blk] * 4,
        compiler_params=_params("arbitrary"),
    )(w, m, v, land)


def _adam_ada(w, m, v, sc_all, dmod_cols):
    R, C = w.shape
    tr = 256

    def body(w_ref, m_ref, v_ref, sc_ref, dm_ref, g_ref, d_ref, mo_ref, vo_ref):
        g = _mm_tn(sc_ref[...], dm_ref[...])
        g_ref[...] = g
        d_ref[...], mo_ref[...], vo_ref[...] = _adamw_update(w_ref[...], g, m_ref[...], v_ref[...])

    blk = pl.BlockSpec((tr, C), lambda i: (i, 0))
    return pl.pallas_call(
        body, name="adam_w_ada", grid=(R // tr,),
        out_shape=[jax.ShapeDtypeStruct((R, C), F32)] * 4,
        in_specs=[blk, blk, blk, pl.BlockSpec((8, tr), lambda i: (0, i)), pl.BlockSpec((8, C), lambda i: (0, 0))],
        out_specs=[blk] * 4,
        compiler_params=_params("arbitrary"),
    )(w, m, v, sc_all, dmod_cols)


def _adam_small(w, g, m, v):
    def body(w_ref, g_ref, m_ref, v_ref, d_ref, mo_ref, vo_ref):
        d_ref[...], mo_ref[...], vo_ref[...] = _adamw_update(w_ref[...], g_ref[...], m_ref[...], v_ref[...])

    return pl.pallas_call(body, name="adam_small", out_shape=[jax.ShapeDtypeStruct(w.shape, F32)] * 3,
                          compiler_params=pltpu.CompilerParams(vmem_limit_bytes=VMEM_LIMIT))(w, g, m, v)


def _block_diag_in(b):
    bt = jnp.transpose(b, (0, 2, 1)).reshape(SSM_BLOCKS, 8, SSM_GROUP, SSM_STATE)
    eye = jnp.eye(8, dtype=bool)[None, :, None, :, None]
    return jnp.where(eye, bt[:, :, :, None, :], 0.0).reshape(SSM_BLOCKS, 128, SSM_BLOCK_STATE)


def _block_diag_out(c):
    ct = jnp.transpose(c, (0, 2, 1)).reshape(SSM_BLOCKS, 8, SSM_STATE, SSM_GROUP)
    eye = jnp.eye(8, dtype=bool)[None, :, None, :, None]
    return jnp.where(eye, ct[:, :, :, None, :], 0.0).reshape(SSM_BLOCKS, SSM_BLOCK_STATE, 128)


def _diag_blocks(dense, rows, cols):
    d5 = dense.reshape(SSM_BLOCKS, 8, rows, 8, cols)
    return jnp.stack([d5[:, a, :, a, :] for a in range(8)], axis=1).reshape(32, rows, cols)


def _pack_small(ada_vec, parts):
    rest = jnp.concatenate([parts[n].reshape(-1) for n, _ in SMALL_PARAMS])
    rest = jnp.pad(rest, (0, NDEV * REST_ROWS * 128 - SMALL_TOTAL)).reshape(NDEV, REST_ROWS, 128)
    return jnp.concatenate([ada_vec.reshape(NDEV, ADA_ROWS, 128), rest,
                            jnp.zeros((NDEV, PACK_ROWS - ADA_ROWS - REST_ROWS, 128), F32)], axis=1)


def _unpack_small(pack, shapes):
    ada_vec = pack[:, :ADA_ROWS].reshape(-1)
    rest = pack[:, ADA_ROWS:ADA_ROWS + REST_ROWS].reshape(-1)
    out, off = {}, 0
    for n, size in SMALL_PARAMS:
        out[n] = rest[off:off + size].reshape(shapes[n])
        off += size
    return ada_vec, out


WEIGHT_ORDER = ('w_ada', 'b_ada', 'g_ffn1', 'w_ffn1_in', 'w_ffn1_out', 'g_mix', 'w_in', 'pool_w', 'pool_b',
                'pool_scale', 'w_pool_up', 'ssm_lam_re_log', 'ssm_lam_im', 'ssm_log_dt', 'ssm_b_re', 'ssm_b_im',
                'ssm_c_re', 'ssm_c_im', 'ssm_d', 'w_glu', 'b_glu', 'w_ssm_up', 'w_out', 'g_ffn2', 'w_ffn2_in',
                'w_ffn2_out', 'g_final')
GATHERED = ('w_ffn1_in', 'w_ffn1_out', 'w_in', 'w_pool_up', 'w_glu', 'w_ssm_up', 'w_out', 'w_ffn2_in', 'w_ffn2_out')


def kernel(x, c, w_ada, b_ada, g_ffn1, w_ffn1_in, w_ffn1_out, g_mix, w_in, pool_w, pool_b, pool_scale, w_pool_up, ssm_lam_re_log, ssm_lam_im, ssm_log_dt, ssm_b_re, ssm_b_im, ssm_c_re, ssm_c_im, ssm_d, w_glu, b_glu, w_ssm_up, w_out, g_ffn2, w_ffn2_in, w_ffn2_out, g_final, loss_target, m_w_ada, m_b_ada, m_g_ffn1, m_w_ffn1_in, m_w_ffn1_out, m_g_mix, m_w_in, m_pool_w, m_pool_b, m_pool_scale, m_w_pool_up, m_ssm_lam_re_log, m_ssm_lam_im, m_ssm_log_dt, m_ssm_b_re, m_ssm_b_im, m_ssm_c_re, m_ssm_c_im, m_ssm_d, m_w_glu, m_b_glu, m_w_ssm_up, m_w_out, m_g_ffn2, m_w_ffn2_in, m_w_ffn2_out, m_g_final, v_w_ada, v_b_ada, v_g_ffn1, v_w_ffn1_in, v_w_ffn1_out, v_g_mix, v_w_in, v_pool_w, v_pool_b, v_pool_scale, v_w_pool_up, v_ssm_lam_re_log, v_ssm_lam_im, v_ssm_log_dt, v_ssm_b_re, v_ssm_b_im, v_ssm_c_re, v_ssm_c_im, v_ssm_d, v_w_glu, v_b_glu, v_w_ssm_up, v_w_out, v_g_ffn2, v_w_ffn2_in, v_w_ffn2_out, v_g_final):
    args = locals()
    W = {n: args[n] for n in WEIGHT_ORDER}
    M = {n: args["m_" + n] for n in WEIGHT_ORDER}
    V = {n: args["v_" + n] for n in WEIGHT_ORDER}
    shapes = {n: W[n].shape for n in WEIGHT_ORDER}
    xt, tgt = x[0], loss_target[0]

    shard = dict(zip(GATHERED, _cast_shards([W[n][0] for n in GATHERED])))
    stacks = {}

    def gather(names):
        return _Gather([shard[n] for n in names])

    def gathered(names, results):
        stacks.update(zip(names, results))

    ffn1_w, ffn2_w = ('w_ffn1_in', 'w_ffn1_out'), ('w_ffn2_in', 'w_ffn2_out')
    mix_w = ('w_in', 'w_pool_up', 'w_glu', 'w_ssm_up', 'w_out')
    mod_cols, sc_all, *res = _ada_forward(c, W['w_ada'][0], b_ada.reshape(NDEV, -1), gather(ffn1_w))
    gathered(ffn1_w, res)
    win1 = stacks['w_ffn1_in'].reshape(2, 4, D_MODEL, FF_SHARD)
    wout1 = stacks['w_ffn1_out'].reshape(4, FF_SHARD, D_MODEL)
    prm = jnp.concatenate([mod_cols.reshape(9, D_MODEL), g_ffn1, g_mix, g_ffn2, g_final[None], jnp.zeros((3, D_MODEL), F32)], axis=0)
    pad512 = jnp.zeros((1, D_MODEL - 512), F32)
    mvec = jnp.concatenate([jnp.concatenate([pool_b, pad512], axis=1), jnp.concatenate([pool_scale, pad512], axis=1),
                            jnp.concatenate([ssm_d, pad512], axis=1), b_glu, jnp.zeros((4, D_MODEL), F32)], axis=0)
    log_dt_col = ssm_log_dt[0][:, None]
    coeffs = _ssm_params_forward(ssm_lam_re_log[0], ssm_lam_im[0], log_dt_col)
    srow = jnp.stack([t.reshape(N_STATE) for t in coeffs], axis=0)
    b_dense = jnp.stack([_block_diag_in(ssm_b_re[0]), _block_diag_in(ssm_b_im[0])], axis=0)
    c_dense = jnp.stack([_block_diag_out(ssm_c_re[0]), _block_diag_out(ssm_c_im[0])], axis=0)
    bb, ct = _ssm_dense_forward(srow, b_dense, c_dense)
    pw = pool_w[0]

    x1, f1, ab1, *res = _ffn_forward(xt, prm, win1, wout1, 0, ROW_G_FFN1, "ffn1_forward", gather(mix_w))
    gathered(mix_w, res)
    w_out_full = stacks['w_out'].reshape(D_MODEL, D_MODEL)
    res = _mixer_forward(x1, prm, stacks['w_in'], stacks['w_pool_up'], stacks['w_glu'], stacks['w_ssm_up'],
                         w_out_full, pw, mvec, srow, bb, ct, gather(ffn2_w))
    x2, mo, saved = res[0], res[1], res[2:11]
    gathered(ffn2_w, res[11:])
    win2 = stacks['w_ffn2_in'].reshape(2, 4, D_MODEL, FF_SHARD)
    wout2 = stacks['w_ffn2_out'].reshape(4, FF_SHARD, D_MODEL)
    x3, f3, ab3 = _ffn_forward(x2, prm, win2, wout2, 2, ROW_G_FFN2, "ffn2_forward")
    d3, fin = _final_loss(x3, tgt, prm)
    loss = lax.psum(fin[1, 0], ("x", "y", "c"))

    lands = {}

    def scatter(grads):
        names = list(grads)
        return _Scatter([grads[n][0] for n in names], [grads[n][1] for n in names], [W[n].shape[1:] for n in names])

    def scattered(grads, results):
        lands.update(zip(grads, results))

    parts3, dwin2, dwout2 = _ffn_backward(x2, d3, ab3, prm, win2, wout2, 2, ROW_G_FFN2, "ffn2_backward")
    d2, sums3 = _norm_backward(parts3, x2, d3, f3, prm, 2, ROW_G_FFN2, 0.5, "ffn2_norm_backward")
    g_ffn2_w = {'w_ffn2_in': (dwin2, _halves), 'w_ffn2_out': (dwout2.reshape(NDEV, -1, D_MODEL), _stacked)}
    res = _mixer_backward(d2, prm, saved, stacks['w_pool_up'], stacks['w_glu'], stacks['w_ssm_up'], w_out_full, pw, mvec,
                          srow, bb, ct, scatter(g_ffn2_w))
    dz, dwo, dwpu, dwglu, dwsu, dpw, dbb, dct, vsum, da = res[:10]
    scattered(g_ffn2_w, res[10:])
    parts2, dwin_mix = _mixer_in_backward(x1, dz, prm, stacks['w_in'])
    d1, sums2 = _norm_backward(parts2, x1, d2, mo, prm, 1, ROW_G_MIX, 1.0, "mixer_norm_backward")
    g_mix_w = {'w_in': (dwin_mix, _stacked), 'w_pool_up': (dwpu, _stacked), 'w_glu': (dwglu, _stacked),
               'w_ssm_up': (dwsu, _stacked), 'w_out': (dwo, _stacked)}
    parts1, dwin1, dwout1, *res = _ffn_backward(xt, d1, ab1, prm, win1, wout1, 0, ROW_G_FFN1, "ffn1_backward",
                                                scatter(g_mix_w))
    scattered(g_mix_w, res)
    g_ffn1_w = {'w_ffn1_in': (dwin1, _halves), 'w_ffn1_out': (dwout1.reshape(NDEV, -1, D_MODEL), _stacked)}
    last_views = [g_ffn1_w[n][1] for n in ffn1_w]
    send_sems, recv_sems, last_src, last_land, token = _scatter_start(
        [g_ffn1_w[n][0] for n in ffn1_w], last_views, [W[n].shape[1:] for n in ffn1_w])
    d0, sums1 = _norm_backward(parts1, xt, d1, f1, prm + token[0:1, 0:1], 0, ROW_G_FFN1, 0.5, "ffn1_norm_backward")

    db_dense, df_rows = _ssm_dense_backward(dbb, da, srow, b_dense)
    cot = [df_rows[r].reshape(32, 64) for r in (2, 3, 0, 1)]
    d_lrl, d_li, d_ldt = _ssm_params_backward(ssm_lam_re_log[0], ssm_lam_im[0], log_dt_col, cot)
    small_grads = {
        'g_ffn1': sums1[0], 'g_mix': sums2[0], 'g_ffn2': sums3[0], 'g_final': fin[0], 'pool_w': dpw,
        'pool_b': vsum[1, :512], 'pool_scale': vsum[0, :512], 'ssm_lam_re_log': d_lrl, 'ssm_lam_im': d_li,
        'ssm_log_dt': d_ldt, 'ssm_b_re': jnp.transpose(_diag_blocks(db_dense[0], SSM_GROUP, SSM_STATE), (0, 2, 1)),
        'ssm_b_im': jnp.transpose(_diag_blocks(db_dense[1], SSM_GROUP, SSM_STATE), (0, 2, 1)),
        'ssm_c_re': jnp.transpose(_diag_blocks(dct[0], SSM_STATE, SSM_GROUP), (0, 2, 1)),
        'ssm_c_im': jnp.transpose(_diag_blocks(dct[1], SSM_STATE, SSM_GROUP), (0, 2, 1)),
        'ssm_d': vsum[2, :512], 'b_glu': vsum[3],
    }
    dmod = jnp.concatenate([sums1[1:4], sums2[1:4], sums3[1:4]], axis=0).reshape(-1)
    total, landed = _allreduce_small(_pack_small(dmod, small_grads))
    dmod_cols = landed[:, :ADA_ROWS].reshape(NDEV, ADA_ROWS * 128)

    grad, delta, new_m, new_v = {}, {}, {}, {}

    def adam_sharded(n):
        res = _adam_sharded(W[n][0], M[n][0], V[n][0], lands[n], "adam_" + n)
        grad[n], delta[n], new_m[n], new_v[n] = [r[None] for r in res]
        return res[3]

    done = [adam_sharded(n) for n in GATHERED if n not in ffn1_w]
    res = _adam_ada(W['w_ada'][0], M['w_ada'][0], V['w_ada'][0], sc_all, dmod_cols)
    grad['w_ada'], delta['w_ada'], new_m['w_ada'], new_v['w_ada'] = [r[None] for r in res]
    done.append(res[3])

    flat = lambda t: t.reshape(NDEV * PACK_ROWS, 128)
    small_w = flat(_pack_small(b_ada.reshape(-1), W))
    small_m = flat(_pack_small(m_b_ada.reshape(-1), M))
    small_v = flat(_pack_small(v_b_ada.reshape(-1), V))
    res = _adam_small(small_w, flat(total), small_m, small_v)
    done.append(res[2])
    for dst, packed in zip((grad, delta, new_m, new_v), (total, *res)):
        ada_vec, rest = _unpack_small(packed.reshape(NDEV, PACK_ROWS, 128), shapes)
        dst.update(rest)
        dst['b_ada'] = ada_vec.reshape(shapes['b_ada'])

    lands.update(zip(ffn1_w, _scatter_wait(send_sems, recv_sems, last_src, last_land, last_views, done)))
    for n in ffn1_w:
        adam_sharded(n)

    return (loss, d0[None], *[grad[n] for n in WEIGHT_ORDER], *[delta[n] for n in WEIGHT_ORDER],
            *[new_m[n] for n in WEIGHT_ORDER], *[new_v[n] for n in WEIGHT_ORDER])
```

```python
import functools

import jax
import jax.numpy as jnp
from jax import lax
from jax.experimental import pallas as pl
from jax.experimental.pallas import tpu as pltpu

F32 = jnp.float32
MXU_DTYPE = jnp.bfloat16
WIRE_DTYPE = jnp.bfloat16
SAVE_DTYPE = jnp.bfloat16

NDEV = 8
D_MODEL = 1024
D_FF = 2816
FF_SHARD = 2 * D_FF // NDEV
POOL_WIDTH = 512
POOL_GROUP = 128
SSM_WIDTH = 512
SSM_STATE = 64
SSM_GROUP = 16
SSM_BLOCKS = 4
SSM_BLOCK_STATE = 512
N_STATE = 2048
IN_WIDTH = 3072
EPS = 1e-6
ADAM_LR = 0.001
ADAM_B1 = 0.9
ADAM_B2 = 0.999
ADAM_EPS = 1e-08
ADAM_WD = 0.01
ADAM_STEP = 10

TM_FFN = 512
FFN_BWD_CHUNK = 256
TM_MIX = 256
TM_MIX_BWD = 256
TM_EW = 512
SCAN_ROWS = 8
POOL_HALO = 16
VMEM_LIMIT = 60 * 1024 * 1024

ROW_G_FFN1, ROW_G_MIX, ROW_G_FFN2, ROW_G_FINAL = 9, 10, 11, 12
ROW_POOL_B, ROW_POOL_SCALE, ROW_SSM_D, ROW_B_GLU = 0, 1, 2, 3

SMALL_PARAMS = (
    ("g_ffn1", 1024), ("g_mix", 1024), ("g_ffn2", 1024), ("g_final", 1024), ("pool_w", 65536),
    ("pool_b", 512), ("pool_scale", 512), ("ssm_lam_re_log", 2048), ("ssm_lam_im", 2048),
    ("ssm_log_dt", 32), ("ssm_b_re", 32768), ("ssm_b_im", 32768), ("ssm_c_re", 32768),
    ("ssm_c_im", 32768), ("ssm_d", 512), ("b_glu", 1024),
)
SMALL_TOTAL = sum(n for _, n in SMALL_PARAMS)
ADA_ROWS = 9
REST_ROWS = 203
PACK_ROWS = 216
MESH = pl.DeviceIdType.MESH


def _mm(a, b):
    return jnp.dot(a.astype(MXU_DTYPE), b.astype(MXU_DTYPE), preferred_element_type=F32)


def _mm_nt(a, b):
    return lax.dot_general(a.astype(MXU_DTYPE), b.astype(MXU_DTYPE), (((1,), (1,)), ((), ())),
                           preferred_element_type=F32)


def _mm_tn(a, b):
    return lax.dot_general(a.astype(MXU_DTYPE), b.astype(MXU_DTYPE), (((0,), (0,)), ((), ())),
                           preferred_element_type=F32)


def _rms_scale(x):
    return lax.rsqrt(jnp.mean(x * x, axis=-1, keepdims=True) + EPS)


def _sigmoid(x):
    return jax.nn.sigmoid(x)


def _colsum(x):
    return jnp.sum(x, axis=0, keepdims=True)


def _row(ref, r):
    return ref[r:r + 1, :]


def _params(*sem):
    return pltpu.CompilerParams(dimension_semantics=sem, vmem_limit_bytes=VMEM_LIMIT)


def _resident(a):
    return pl.BlockSpec(a.shape, lambda *_: (0,) * a.ndim, pipeline_mode=pl.Buffered(1))


def _me():
    return lax.axis_index("x"), lax.axis_index("y"), lax.axis_index("c")


def _peer(rel):
    x, y, c = _me()
    px = 1 - x if rel & 4 else x
    py = 1 - y if rel & 2 else y
    pc = 1 - c if rel & 1 else c
    return (px, py, pc), 4 * px + 2 * py + pc


_HBM = pl.BlockSpec(memory_space=pl.ANY)
_HBM_ONLY = pl.BlockSpec(memory_space=pltpu.HBM)


def _stacked(ref, p):
    return ref.at[p]


def _halves(ref, p):
    return ref.at[p // 4, p % 4]


class _Gather:
    def __init__(self, shards):
        self.operands = list(shards)
        self.n = len(shards)
        self.out_shape = [jax.ShapeDtypeStruct((NDEV,) + s.shape, s.dtype) for s in shards]
        self.scratch = [pltpu.SemaphoreType.DMA((7 * self.n,)), pltpu.SemaphoreType.DMA((7 * self.n,)),
                        pltpu.SemaphoreType.DMA((self.n,))]

    def plan(self, srcs, outs, sems):
        send_sems, recv_sems, local_sems = sems
        n = self.n
        x, y, c = _me()
        me = 4 * x + 2 * y + c
        here, sibling = (x, y, c), (x, y, 1 - c)
        chips = [(1 - x, y), (x, 1 - y), (1 - x, 1 - y)]

        def blk(px, py, pc):
            return 4 * px + 2 * py + pc

        def copy(a, k, block, to, src=None):
            return pltpu.make_async_remote_copy(
                src_ref=outs[a].at[block] if src is None else src, dst_ref=outs[a].at[block],
                send_sem=send_sems.at[7 * a + k], recv_sem=recv_sems.at[7 * a + k], device_id=to, device_id_type=MESH)

        def mine(a):
            return pltpu.make_async_copy(srcs[a], outs[a].at[me], local_sems.at[a])

        def first(a):
            return [copy(a, 0, me, sibling, src=srcs[a])] + [copy(a, 1 + j, me, (*chip, c), src=srcs[a])
                                                              for j, chip in enumerate(chips)]

        def start():
            for a in range(n):
                mine(a).start()
                for cp in first(a):
                    cp.start()

        def forward():
            for a in range(n):
                for j, chip in enumerate(chips):
                    copy(a, 1 + j, blk(*chip, c), here).wait_recv()
                    copy(a, 4 + j, blk(*chip, c), sibling).start()

        def finish():
            for a in range(n):
                copy(a, 0, blk(x, y, 1 - c), here).wait_recv()
                for j, chip in enumerate(chips):
                    copy(a, 4 + j, blk(*chip, 1 - c), here).wait_recv()
            for a in range(n):
                mine(a).wait()
                for cp in first(a):
                    cp.wait_send()
                for j, chip in enumerate(chips):
                    copy(a, 4 + j, blk(*chip, c), sibling).wait_send()

        return start, forward, finish


class _Scatter:
    def __init__(self, arrays, views, shard_shapes):
        self.operands = list(arrays)
        self.views = list(views)
        self.n = len(arrays)
        self.out_shape = [jax.ShapeDtypeStruct((NDEV,) + tuple(s), a.dtype) for s, a in zip(shard_shapes, arrays)]
        self.scratch = [pltpu.SemaphoreType.DMA((7 * self.n,)), pltpu.SemaphoreType.DMA((7 * self.n,)),
                        pltpu.SemaphoreType.DMA((self.n,))]

    def plan(self, srcs, outs, sems):
        send_sems, recv_sems, local_sems = sems
        n, views = self.n, self.views
        x, y, c = _me()
        me = 4 * x + 2 * y + c

        def mine(a):
            return pltpu.make_async_copy(views[a](srcs[a], me), outs[a].at[me], local_sems.at[a])

        def copy(a, rel, sending):
            to, p = _peer(rel)
            return pltpu.make_async_remote_copy(
                src_ref=views[a](srcs[a], p), dst_ref=outs[a].at[me if sending else p],
                send_sem=send_sems.at[7 * a + rel - 1], recv_sem=recv_sems.at[7 * a + rel - 1],
                device_id=to if sending else (x, y, c), device_id_type=MESH)

        def start():
            for a in range(n):
                mine(a).start()
            for rel in range(1, 8):
                for a in range(n):
                    copy(a, rel, True).start()

        def forward():
            pass

        def finish():
            for rel in range(1, 8):
                for a in range(n):
                    copy(a, rel, False).wait_recv()
            for rel in range(1, 8):
                for a in range(n):
                    copy(a, rel, True).wait_send()
            for a in range(n):
                mine(a).wait()

        return start, forward, finish


def _launch(body, name, out_shape, in_specs, out_specs, operands, scratch=(), grid=None, semantics=None,
            carry=None, steps=None):
    out_shape, in_specs, out_specs = list(out_shape), list(in_specs), list(out_specs)
    operands, scratch = list(operands), list(scratch)
    n_in, n_out, n_scr = len(in_specs), len(out_shape), len(scratch)
    kernel_body = body
    if carry is not None:
        k = carry.n

        def kernel_body(*refs):
            ins, cin = refs[:n_in], refs[n_in:n_in + k]
            outs, cout = refs[n_in + k:n_in + k + n_out], refs[n_in + k + n_out:n_in + 2 * k + n_out]
            rest = refs[n_in + 2 * k + n_out:]
            scr, csem = rest[:n_scr], rest[n_scr:]
            start, forward, finish = carry.plan(cin, cout, csem)
            if steps is None:
                start()
                body(*ins, *outs, *scr)
                forward()
                finish()
            else:
                pl.when(steps()[0])(start)
                body(*ins, *outs, *scr)
                pl.when(steps()[1])(forward)
                pl.when(steps()[2])(finish)

        in_specs += [_HBM] * k
        out_shape += carry.out_shape
        out_specs += [_HBM] * k
        operands += carry.operands
        scratch += carry.scratch
    kwargs = {} if grid is None else {"grid": grid}
    params = pltpu.CompilerParams(vmem_limit_bytes=VMEM_LIMIT) if semantics is None else _params(*semantics)
    return pl.pallas_call(kernel_body, name=name, out_shape=out_shape, in_specs=in_specs, out_specs=out_specs,
                          scratch_shapes=scratch, compiler_params=params, **kwargs)(*operands)


def _grid_steps(nt):
    def steps():
        i = pl.program_id(0)
        return i == 0, i == nt // 2, i == nt - 1
    return steps


def _cast_shards(shards):
    n = len(shards)

    def body(*refs):
        for a in range(n):
            refs[n + a][...] = refs[a][...].astype(WIRE_DTYPE)

    return pl.pallas_call(body, name="cast_shards",
                          out_shape=[jax.ShapeDtypeStruct(s.shape, WIRE_DTYPE) for s in shards],
                          compiler_params=pltpu.CompilerParams(vmem_limit_bytes=VMEM_LIMIT))(*shards)


_SEM = pl.BlockSpec(memory_space=pltpu.SEMAPHORE)
_DATAFLOW = pltpu.SideEffectType.DATAFLOW_SIDE_EFFECTING


def _split_copy(arrays, views, landing, send_sems, recv_sems, a, rel):
    to, p = _peer(rel)
    x, y, c = _me()
    return pltpu.make_async_remote_copy(
        src_ref=views[a](arrays[a], p), dst_ref=landing[a].at[4 * x + 2 * y + c],
        send_sem=send_sems.at[NDEV * a + rel], recv_sem=recv_sems.at[NDEV * a + rel], device_id=to, device_id_type=MESH)


def _scatter_start(arrays, views, shard_shapes, after):
    n = len(arrays)
    landing = [pltpu.with_memory_space_constraint(lax.empty((NDEV,) + tuple(s), a.dtype), pltpu.HBM)
               for s, a in zip(shard_shapes, arrays)]
    arrays = [pltpu.with_memory_space_constraint(a, pltpu.HBM) for a in arrays]

    def body(*refs):
        ins, land = refs[:n], refs[n:2 * n]
        send_sems, recv_sems = refs[2 * n + len(after)], refs[2 * n + len(after) + 1]
        token = refs[-1]
        for rel in range(NDEV):
            for a in range(n):
                _split_copy(ins, views, land, send_sems, recv_sems, a, rel).start()
        token[...] = jnp.zeros_like(token)

    res = pl.pallas_call(
        body, name="scatter_start",
        out_shape=[pltpu.SemaphoreType.DMA((NDEV * n,)), pltpu.SemaphoreType.DMA((NDEV * n,))]
        + [pltpu.HBM(a.shape, a.dtype) for a in arrays] + [pltpu.HBM(l.shape, l.dtype) for l in landing]
        + [jax.ShapeDtypeStruct((8, 128), F32)],
        in_specs=[_HBM_ONLY] * (2 * n) + [_HBM] * len(after),
        out_specs=[_SEM, _SEM] + [_HBM_ONLY] * (2 * n) + [pl.BlockSpec(memory_space=pltpu.VMEM)],
        input_output_aliases={i: 2 + i for i in range(2 * n)},
        compiler_params=pltpu.CompilerParams(has_side_effects=_DATAFLOW),
    )(*arrays, *landing, *after)
    return res[0], res[1], res[2:2 + n], res[2 + n:2 + 2 * n], res[-1]


def _scatter_wait(send_sems, recv_sems, arrays, landing, views, after):
    n = len(arrays)

    def body(*refs):
        ins, land = refs[:n], refs[n:2 * n]
        send, recv = refs[2 * n], refs[2 * n + 1]
        for rel in range(NDEV):
            for a in range(n):
                cp = _split_copy(ins, views, land, send, recv, a, rel)
                cp.wait_send()
                cp.wait_recv()

    res = pl.pallas_call(
        body, name="scatter_wait",
        out_shape=[pltpu.HBM(a.shape, a.dtype) for a in arrays] + [pltpu.HBM(l.shape, l.dtype) for l in landing],
        in_specs=[_HBM_ONLY] * (2 * n) + [_SEM, _SEM] + [_HBM] * len(after),
        out_specs=[_HBM_ONLY] * (2 * n),
        input_output_aliases={i: i for i in range(2 * n)},
        compiler_params=pltpu.CompilerParams(has_side_effects=_DATAFLOW),
    )(*arrays, *landing, send_sems, recv_sems, *after)
    return res[n:]


def _ada_forward(c_row, w_ada, b_ada8, carry):
    cols = w_ada.shape[1]

    def body(c_ref, w_ref, b_ref, mod_ref, sc_ref, c_all, send_buf, recv_buf, send1, recv1, send2, recv2):
        x, y, c = _me()
        me = 4 * x + 2 * y + c
        rowi = lax.broadcasted_iota(jnp.int32, (8, D_MODEL), 0)
        c_all[me] = jnp.broadcast_to(c_ref[...], (8, D_MODEL))
        copies = []
        for rel in range(1, 8):
            to, _ = _peer(rel)
            cp = pltpu.make_async_remote_copy(src_ref=c_all.at[me], dst_ref=c_all.at[me], send_sem=send1.at[rel - 1],
                                              recv_sem=recv1.at[rel - 1], device_id=to, device_id_type=MESH)
            cp.start()
            copies.append(cp)
        for rel in range(1, 8):
            _, p = _peer(rel)
            pltpu.make_async_remote_copy(src_ref=c_all.at[p], dst_ref=c_all.at[p], send_sem=send1.at[rel - 1],
                                         recv_sem=recv1.at[rel - 1], device_id=(x, y, c), device_id_type=MESH).wait_recv()
        for cp in copies:
            cp.wait_send()
        cmat = jnp.zeros((8, D_MODEL), F32)
        for b in range(8):
            cmat = jnp.where(rowi == b, c_all[b], cmat)
        sc = cmat * _sigmoid(cmat)
        sc_ref[...] = sc
        modcols = _mm(sc, w_ref[...]) + b_ref[pl.ds(me, 1), :]
        for b in range(8):
            send_buf[b] = jnp.broadcast_to(modcols[b:b + 1, :], (8, cols))
        recv_buf[me] = send_buf[me]
        copies = []
        for rel in range(1, 8):
            to, p = _peer(rel)
            cp = pltpu.make_async_remote_copy(src_ref=send_buf.at[p], dst_ref=recv_buf.at[me], send_sem=send2.at[rel - 1],
                                              recv_sem=recv2.at[rel - 1], device_id=to, device_id_type=MESH)
            cp.start()
            copies.append(cp)
        for rel in range(1, 8):
            _, p = _peer(rel)
            pltpu.make_async_remote_copy(src_ref=send_buf.at[p], dst_ref=recv_buf.at[p], send_sem=send2.at[rel - 1],
                                         recv_sem=recv2.at[rel - 1], device_id=(x, y, c), device_id_type=MESH).wait_recv()
        for cp in copies:
            cp.wait_send()
        rowc = lax.broadcasted_iota(jnp.int32, (8, cols), 0)
        out = jnp.zeros((8, cols), F32)
        for k in range(8):
            out = jnp.where(rowc == k, recv_buf[k], out)
        mod_ref[...] = out

    return _launch(
        body, "ada_forward",
        out_shape=[jax.ShapeDtypeStruct((8, cols), F32), jax.ShapeDtypeStruct((8, D_MODEL), F32)],
        in_specs=[pl.BlockSpec(memory_space=pltpu.VMEM)] * 3,
        out_specs=[pl.BlockSpec(memory_space=pltpu.VMEM)] * 2,
        operands=(c_row, w_ada, b_ada8),
        scratch=[pltpu.VMEM((8, 8, D_MODEL), F32), pltpu.VMEM((8, 8, cols), F32), pltpu.VMEM((8, 8, cols), F32)]
        + [pltpu.SemaphoreType.DMA((7,))] * 4,
        carry=carry)


def _allreduce_small(pack):
    rows = pack.shape[1]

    def body(pack_ref, total_ref, land_ref, send1, recv1, send2, recv2):
        x, y, c = _me()
        me = 4 * x + 2 * y + c
        land_ref[me] = pack_ref[me]
        copies = []
        for rel in range(1, 8):
            to, p = _peer(rel)
            cp = pltpu.make_async_remote_copy(src_ref=pack_ref.at[p], dst_ref=land_ref.at[me], send_sem=send1.at[rel - 1],
                                              recv_sem=recv1.at[rel - 1], device_id=to, device_id_type=MESH)
            cp.start()
            copies.append(cp)
        for rel in range(1, 8):
            _, p = _peer(rel)
            pltpu.make_async_remote_copy(src_ref=pack_ref.at[p], dst_ref=land_ref.at[p], send_sem=send1.at[rel - 1],
                                         recv_sem=recv1.at[rel - 1], device_id=(x, y, c), device_id_type=MESH).wait_recv()
        for cp in copies:
            cp.wait_send()
        acc = land_ref[0]
        for b in range(1, 8):
            acc = acc + land_ref[b]
        total_ref[me] = acc
        copies = []
        for rel in range(1, 8):
            to, _ = _peer(rel)
            cp = pltpu.make_async_remote_copy(src_ref=total_ref.at[me], dst_ref=total_ref.at[me], send_sem=send2.at[rel - 1],
                                              recv_sem=recv2.at[rel - 1], device_id=to, device_id_type=MESH)
            cp.start()
            copies.append(cp)
        for rel in range(1, 8):
            _, p = _peer(rel)
            pltpu.make_async_remote_copy(src_ref=total_ref.at[p], dst_ref=total_ref.at[p], send_sem=send2.at[rel - 1],
                                         recv_sem=recv2.at[rel - 1], device_id=(x, y, c), device_id_type=MESH).wait_recv()
        for cp in copies:
            cp.wait_send()

    return pl.pallas_call(
        body, name="allreduce_small",
        out_shape=[jax.ShapeDtypeStruct((8, rows, 128), F32), jax.ShapeDtypeStruct((8, rows, 128), F32)],
        in_specs=[pl.BlockSpec(memory_space=pltpu.VMEM)],
        out_specs=[pl.BlockSpec(memory_space=pltpu.VMEM)] * 2,
        scratch_shapes=[pltpu.SemaphoreType.DMA((7,))] * 4,
        compiler_params=pltpu.CompilerParams(vmem_limit_bytes=VMEM_LIMIT),
    )(pack)


def _modulated(x, prm_ref, sub, g_row):
    shift, scale = _row(prm_ref, 3 * sub), _row(prm_ref, 3 * sub + 1)
    g = _row(prm_ref, g_row)
    r = _rms_scale(x)
    n0 = x * r
    return (n0 * g) * (1.0 + scale) + shift, r, n0


def _ffn_forward(x, prm, win, wout, sub, g_row, name, carry=None):
    T = x.shape[0]
    tm = min(T, TM_FFN)

    def body(x_ref, prm_ref, win_ref, wout_ref, xo_ref, f_ref, ab_ref):
        xv = x_ref[...]
        h, _, _ = _modulated(xv, prm_ref, sub, g_row)
        hb = h.astype(MXU_DTYPE)
        acc = None
        for j in range(4):
            a = _mm(hb, win_ref[0, j])
            b = _mm(hb, win_ref[1, j])
            ab_ref[0, j] = a.astype(SAVE_DTYPE)
            ab_ref[1, j] = b.astype(SAVE_DTYPE)
            s = (a * _sigmoid(a)) * b
            t = _mm(s, wout_ref[j])
            acc = t if acc is None else acc + t
        f_ref[...] = acc.astype(SAVE_DTYPE)
        xo_ref[...] = xv + (0.5 * _row(prm_ref, 3 * sub + 2)) * acc

    tok = pl.BlockSpec((tm, D_MODEL), lambda i: (i, 0))
    return _launch(
        body, name, grid=(T // tm,), semantics=("arbitrary",),
        out_shape=[jax.ShapeDtypeStruct((T, D_MODEL), F32), jax.ShapeDtypeStruct((T, D_MODEL), SAVE_DTYPE),
                   jax.ShapeDtypeStruct((2, 4, T, FF_SHARD), SAVE_DTYPE)],
        in_specs=[tok, _resident(prm), _resident(win), _resident(wout)],
        out_specs=[tok, tok, pl.BlockSpec((2, 4, tm, FF_SHARD), lambda i: (0, 0, i, 0))],
        operands=(x, prm, win, wout), carry=carry, steps=_grid_steps(T // tm))


def _ffn_backward(x, d, ab, prm, win, wout, sub, g_row, name, carry=None):
    T = x.shape[0]
    tm = min(T, TM_FFN)
    nt = T // tm
    chunk = min(tm, FFN_BWD_CHUNK)

    def body(x_ref, d_ref, ab_ref, prm_ref, win_ref, wout_ref, dh_ref, dwin_ref, dwout_ref, acc_in, acc_out):
        i = pl.program_id(1)

        @pl.when(i == 0)
        def _():
            acc_in[...] = jnp.zeros_like(acc_in)
            acc_out[...] = jnp.zeros_like(acc_out)

        wa, wb, wo = win_ref[0, 0], win_ref[1, 0], wout_ref[0]
        half_gate = 0.5 * _row(prm_ref, 3 * sub + 2)
        das, dbs, ss, hbs, dfss = [], [], [], [], []
        for ck in range(tm // chunk):
            rows = slice(ck * chunk, (ck + 1) * chunk)
            h, _, _ = _modulated(x_ref[rows, :], prm_ref, sub, g_row)
            hbs.append(h.astype(MXU_DTYPE))
            a = ab_ref[0, 0, rows, :].astype(F32)
            b = ab_ref[1, 0, rows, :].astype(F32)
            sg = _sigmoid(a)
            si = a * sg
            dfs = (half_gate * d_ref[rows, :]).astype(MXU_DTYPE)
            ds = _mm_nt(dfs, wo)
            da = (ds * b * (sg * (1.0 + a * (1.0 - sg)))).astype(MXU_DTYPE)
            db = (ds * si).astype(MXU_DTYPE)
            dh_ref[0, rows, :] = (_mm_nt(da, wa) + _mm_nt(db, wb)).astype(SAVE_DTYPE)
            das.append(da)
            dbs.append(db)
            ss.append((si * b).astype(MXU_DTYPE))
            dfss.append(dfs)
        cat = (lambda v: v[0]) if len(das) == 1 else (lambda v: jnp.concatenate(v, axis=0))
        hb = cat(hbs)
        acc_out[...] += _mm_tn(cat(ss), cat(dfss))
        acc_in[0] += _mm_tn(hb, cat(das))
        acc_in[1] += _mm_tn(hb, cat(dbs))

        @pl.when(i == nt - 1)
        def _():
            dwin_ref[0, 0] = acc_in[0].astype(WIRE_DTYPE)
            dwin_ref[1, 0] = acc_in[1].astype(WIRE_DTYPE)
            dwout_ref[0] = acc_out[...].astype(WIRE_DTYPE)

    def steps():
        j, i = pl.program_id(0), pl.program_id(1)
        return (j == 0) & (i == 0), (j == 2) & (i == 0), (j == 3) & (i == nt - 1)

    tok = pl.BlockSpec((tm, D_MODEL), lambda j, i: (i, 0))
    return _launch(
        body, name, grid=(4, nt), semantics=("arbitrary", "arbitrary"),
        out_shape=[jax.ShapeDtypeStruct((4, T, D_MODEL), SAVE_DTYPE),
                   jax.ShapeDtypeStruct(win.shape, WIRE_DTYPE), jax.ShapeDtypeStruct(wout.shape, WIRE_DTYPE)],
        in_specs=[tok, tok, pl.BlockSpec((2, 1, tm, FF_SHARD), lambda j, i: (0, j, i, 0)), _resident(prm),
                  pl.BlockSpec((2, 1, D_MODEL, FF_SHARD), lambda j, i: (0, j, 0, 0)),
                  pl.BlockSpec((1, FF_SHARD, D_MODEL), lambda j, i: (j, 0, 0))],
        out_specs=[pl.BlockSpec((1, tm, D_MODEL), lambda j, i: (j, i, 0)),
                   pl.BlockSpec((2, 1, D_MODEL, FF_SHARD), lambda j, i: (0, j, 0, 0)),
                   pl.BlockSpec((1, FF_SHARD, D_MODEL), lambda j, i: (j, 0, 0))],
        operands=(x, d, ab, prm, win, wout),
        scratch=[pltpu.VMEM((2, D_MODEL, FF_SHARD), F32), pltpu.VMEM((FF_SHARD, D_MODEL), F32)],
        carry=carry, steps=steps)


def _norm_backward(parts, x, d, f, prm, sub, g_row, gate_coef, name, carry=None):
    T = x.shape[0]
    tm = min(T, TM_EW)
    P = parts.shape[0]

    def body(p_ref, x_ref, d_ref, f_ref, prm_ref, dx_ref, sums_ref):
        i = pl.program_id(0)
        dh = p_ref[0].astype(F32)
        for k in range(1, P):
            dh = dh + p_ref[k].astype(F32)
        xv, dv = x_ref[...], d_ref[...]
        scale, g = _row(prm_ref, 3 * sub + 1), _row(prm_ref, g_row)
        r = _rms_scale(xv)
        n0 = xv * r
        dn = dh * (1.0 + scale)
        dn0 = dn * g
        dx_ref[...] = dv + r * (dn0 - n0 * jnp.mean(dn0 * n0, axis=-1, keepdims=True))
        upd = jnp.concatenate([_colsum(dn * n0), _colsum(dh), _colsum(dh * (n0 * g)),
                               gate_coef * _colsum(dv * f_ref[...].astype(F32)), jnp.zeros((4, D_MODEL), F32)], axis=0)

        @pl.when(i == 0)
        def _():
            sums_ref[...] = upd

        @pl.when(i > 0)
        def _():
            sums_ref[...] += upd

    tok = pl.BlockSpec((tm, D_MODEL), lambda i: (i, 0))
    return _launch(
        body, name, grid=(T // tm,), semantics=("arbitrary",),
        out_shape=[jax.ShapeDtypeStruct((T, D_MODEL), F32), jax.ShapeDtypeStruct((8, D_MODEL), F32)],
        in_specs=[pl.BlockSpec((P, tm, D_MODEL), lambda i: (0, i, 0)), tok, tok, tok, _resident(prm)],
        out_specs=[tok, pl.BlockSpec((8, D_MODEL), lambda i: (0, 0))],
        operands=(parts, x, d, f, prm), carry=carry, steps=_grid_steps(T // tm))


def _final_loss(x, target, prm):
    T = x.shape[0]
    tm = min(T, TM_EW)

    def body(x_ref, t_ref, prm_ref, dx_ref, sums_ref):
        i = pl.program_id(0)
        xv = x_ref[...]
        g = _row(prm_ref, ROW_G_FINAL)
        r = _rms_scale(xv)
        n0 = xv * r
        err = n0 * g - t_ref[...]
        dy = err / float(D_MODEL)
        dn0 = dy * g
        dx_ref[...] = r * (dn0 - n0 * jnp.mean(dn0 * n0, axis=-1, keepdims=True))
        loss = 0.5 * jnp.sum(jnp.mean(err * err, axis=-1, keepdims=True), axis=0, keepdims=True)
        upd = jnp.concatenate([_colsum(dy * n0), jnp.broadcast_to(loss, (1, D_MODEL)), jnp.zeros((6, D_MODEL), F32)], axis=0)

        @pl.when(i == 0)
        def _():
            sums_ref[...] = upd

        @pl.when(i > 0)
        def _():
            sums_ref[...] += upd

    tok = pl.BlockSpec((tm, D_MODEL), lambda i: (i, 0))
    return pl.pallas_call(
        body, name="final_loss", grid=(T // tm,),
        out_shape=[jax.ShapeDtypeStruct((T, D_MODEL), F32), jax.ShapeDtypeStruct((8, D_MODEL), F32)],
        in_specs=[tok, tok, pl.BlockSpec(prm.shape, lambda i: (0, 0))],
        out_specs=[tok, pl.BlockSpec((8, D_MODEL), lambda i: (0, 0))],
        compiler_params=_params("arbitrary"),
    )(x, target, prm)


def _ssm_discretise(lam_re_log, lam_im, log_dt):
    lr = -jnp.exp(lam_re_log)
    dt = jnp.exp(log_dt)
    mag = jnp.exp(lr * dt)
    ang = lam_im * dt
    ab_re = mag * jnp.cos(ang)
    ab_im = mag * jnp.sin(ang)
    num_re = ab_re - 1.0
    num_im = ab_im
    den = lr * lr + lam_im * lam_im
    f_re = (num_re * lr + num_im * lam_im) / den
    f_im = (num_im * lr - num_re * lam_im) / den
    return ab_re, ab_im, f_re, f_im


def _ssm_params_forward(lam_re_log, lam_im, log_dt):
    def body(a_ref, b_ref, c_ref, o0, o1, o2, o3):
        outs = _ssm_discretise(a_ref[...], b_ref[...], c_ref[...])
        for o, v in zip((o0, o1, o2, o3), outs):
            o[...] = v

    return pl.pallas_call(body, name="ssm_params_forward",
                          out_shape=[jax.ShapeDtypeStruct(lam_im.shape, F32)] * 4)(lam_re_log, lam_im, log_dt)


def _ssm_params_backward(lam_re_log, lam_im, log_dt, cot):
    def body(a_ref, b_ref, c_ref, g0, g1, g2, g3, o0, o1, o2):
        _, vjp = jax.vjp(_ssm_discretise, a_ref[...], b_ref[...], c_ref[...])
        d0, d1, d2 = vjp((g0[...], g1[...], g2[...], g3[...]))
        o0[...] = d0
        o1[...] = d1
        o2[...] = d2

    return pl.pallas_call(
        body, name="ssm_params_backward",
        out_shape=[jax.ShapeDtypeStruct(lam_im.shape, F32), jax.ShapeDtypeStruct(lam_im.shape, F32),
                   jax.ShapeDtypeStruct(log_dt.shape, F32)])(lam_re_log, lam_im, log_dt, *cot)


def _ssm_dense_forward(srow, b_dense, c_dense):
    def body(srow_ref, bd_ref, cd_ref, bb_ref, ct_ref):
        for j in range(SSM_BLOCKS):
            lanes = slice(j * SSM_BLOCK_STATE, (j + 1) * SSM_BLOCK_STATE)
            f_re, f_im = srow_ref[2:3, lanes], srow_ref[3:4, lanes]
            bb_ref[0, j] = (f_re * bd_ref[0, j] - f_im * bd_ref[1, j]).astype(MXU_DTYPE)
            bb_ref[1, j] = (f_re * bd_ref[1, j] + f_im * bd_ref[0, j]).astype(MXU_DTYPE)
            ct_ref[0, j] = cd_ref[0, j].astype(MXU_DTYPE)
            ct_ref[1, j] = cd_ref[1, j].astype(MXU_DTYPE)

    return pl.pallas_call(body, name="ssm_dense_forward",
                          out_shape=[jax.ShapeDtypeStruct(b_dense.shape, MXU_DTYPE),
                                     jax.ShapeDtypeStruct(c_dense.shape, MXU_DTYPE)],
                          compiler_params=pltpu.CompilerParams(vmem_limit_bytes=VMEM_LIMIT))(srow, b_dense, c_dense)


def _cmul(p, q):
    return p[0] * q[0] - p[1] * q[1], p[0] * q[1] + p[1] * q[0]


def _scan_coefficients(ar, ai, reverse):
    n = ar.shape[1]
    p = {1: (ar, ai)}
    p[2] = _cmul(p[1], p[1])
    p[3] = _cmul(p[2], p[1])
    p[4] = _cmul(p[2], p[2])
    p[5] = _cmul(p[4], p[1])
    p[6] = _cmul(p[4], p[2])
    p[7] = _cmul(p[4], p[3])
    p[8] = _cmul(p[4], p[4])
    rowi = lax.broadcasted_iota(jnp.int32, (SCAN_ROWS, n), 0)
    tiles = []
    for dstep in (1, 2, 4):
        keep = (rowi < SCAN_ROWS - dstep) if reverse else (rowi >= dstep)
        for part in p[dstep]:
            tiles.append(jnp.where(keep, jnp.broadcast_to(part, (SCAN_ROWS, n)), 0.0))
    for comp in (0, 1):
        t = jnp.zeros((SCAN_ROWS, n), F32)
        for rr in range(SCAN_ROWS):
            power = SCAN_ROWS - rr if reverse else rr + 1
            t = jnp.where(rowi == rr, jnp.broadcast_to(p[power][comp], (SCAN_ROWS, n)), t)
        tiles.append(t)
    return tiles


def _load_stack(stack_hbm, dst, sems, base):
    cols = stack_hbm.shape[2]
    cps = [pltpu.make_async_copy(stack_hbm.at[k], dst.at[:, pl.ds(k * cols, cols)], sems.at[base + k])
           for k in range(NDEV)]
    for cp in cps:
        cp.start()
    return cps


def _window_lanes():
    lane = lax.broadcasted_iota(jnp.int32, (1, POOL_WIDTH), 1)
    return jnp.where(lane < 128, 2.0, jnp.where(lane < 256, 4.0, jnp.where(lane < 384, 8.0, 16.0)))


def _gelu(y):
    return 0.5 * y * (1.0 + lax.erf(y * 0.7071067811865476))


def _gelu_grad(y):
    return 0.5 * (1.0 + lax.erf(y * 0.7071067811865476)) + y * jnp.exp(-0.5 * y * y) * 0.3989422804014327


def _mixer_forward(x, prm, w_in_s, w_pu_s, w_glu_s, w_su_s, w_out, pool_w, mvec, srow, bb, ct, carry=None):
    T = x.shape[0]
    tm = min(T, TM_MIX)
    nt = T // tm
    n_tiles = tm // SCAN_ROWS

    def body(x_ref, prm_ref, w_in_h, w_pu_h, w_glu_h, w_su_h, w_out_h, pw_ref, mv_ref, srow_ref, bb, ct,
             x2_ref, mo_ref, z_ref, sre_ref, sim_ref, zp_ref, q_ref, yp_ref, yss_ref, vg_ref, ys_ref,
             w_in, w_pu, w_glu, w_su, w_o, coef, carry, hist, bu, sems):
        i = pl.program_id(0)

        @pl.when(i == 0)
        def _():
            cps = (_load_stack(w_in_h, w_in, sems, 0) + _load_stack(w_pu_h, w_pu, sems, 8)
                   + _load_stack(w_glu_h, w_glu, sems, 16) + _load_stack(w_su_h, w_su, sems, 24))
            cps.append(pltpu.make_async_copy(w_out_h, w_o, sems.at[32]))
            cps[-1].start()
            for j in range(SSM_BLOCKS):
                lanes = slice(j * SSM_BLOCK_STATE, (j + 1) * SSM_BLOCK_STATE)
                for k, t in enumerate(_scan_coefficients(srow_ref[0:1, lanes], srow_ref[1:2, lanes], False)):
                    coef[j, k] = t
            carry[...] = jnp.zeros_like(carry)
            hist[...] = jnp.zeros_like(hist)
            for cp in cps:
                cp.wait()

        xv = x_ref[...]
        h, _, _ = _modulated(xv, prm_ref, 1, ROW_G_MIX)
        z = _mm(h, w_in[...])
        z_ref[...] = z.astype(SAVE_DTYPE)
        u_pool, u_ssm = z[:, 0:512], z[:, 512:1024]
        gl_pool, gl_ssm = z[:, 1024:2048], z[:, 2048:3072]

        ext = jnp.concatenate([hist[...], u_pool], axis=0)
        w2 = ext + pltpu.roll(ext, 1, 0)
        w4 = w2[:, 128:] + pltpu.roll(w2[:, 128:], 2, 0)
        w8 = w4[:, 128:] + pltpu.roll(w4[:, 128:], 4, 0)
        w16 = w8[:, 128:] + pltpu.roll(w8[:, 128:], 8, 0)
        wsum = jnp.concatenate([w2[POOL_HALO:, :128], w4[POOL_HALO:, :128], w8[POOL_HALO:, :128], w16[POOL_HALO:]], axis=1)
        hist[...] = u_pool[tm - POOL_HALO:, :]
        t1 = (lax.broadcasted_iota(jnp.int32, (tm, 1), 0) + (i * tm + 1)).astype(F32)
        zp = wsum / jnp.minimum(t1, _window_lanes()) - u_pool
        zp_ref[...] = zp.astype(SAVE_DTYPE)
        q = jnp.concatenate([_mm(zp[:, k * 128:(k + 1) * 128], pw_ref[k]) for k in range(4)], axis=1)
        q = q + mv_ref[ROW_POOL_B:ROW_POOL_B + 1, 0:512]
        q_ref[...] = q.astype(SAVE_DTYPE)
        y_pool = _mm(q * mv_ref[ROW_POOL_SCALE:ROW_POOL_SCALE + 1, 0:512], w_pu[...])
        yp_ref[...] = y_pool.astype(SAVE_DTYPE)

        y_blocks = []
        for j in range(SSM_BLOCKS):
            lanes = pl.ds(j * SSM_BLOCK_STATE, SSM_BLOCK_STATE)
            ub = u_ssm[:, j * 128:(j + 1) * 128].astype(MXU_DTYPE)
            bu[0] = _mm(ub, bb[0, j])
            bu[1] = _mm(ub, bb[1, j])
            a1r, a1i, a2r, a2i, a4r, a4i, pr, pi = [coef[j, k] for k in range(8)]

            def step(tt, c, lanes=lanes, a1r=a1r, a1i=a1i, a2r=a2r, a2i=a2i, a4r=a4r, a4i=a4i, pr=pr, pi=pi):
                cr, ci = c
                rows = pl.ds(pl.multiple_of(tt * SCAN_ROWS, SCAN_ROWS), SCAN_ROWS)
                xr, xi = bu[0, rows, :], bu[1, rows, :]
                for dstep, kr, ki in ((1, a1r, a1i), (2, a2r, a2i), (4, a4r, a4i)):
                    sr, si = pltpu.roll(xr, dstep, 0), pltpu.roll(xi, dstep, 0)
                    xr, xi = xr + kr * sr - ki * si, xi + kr * si + ki * sr
                xr, xi = xr + pr * cr - pi * ci, xi + pr * ci + pi * cr
                sre_ref[rows, lanes] = xr
                sim_ref[rows, lanes] = xi
                return (jnp.broadcast_to(xr[SCAN_ROWS - 1:SCAN_ROWS, :], xr.shape),
                        jnp.broadcast_to(xi[SCAN_ROWS - 1:SCAN_ROWS, :], xi.shape))

            cr, ci = lax.fori_loop(0, n_tiles, step, (carry[j, 0], carry[j, 1]))
            carry[j, 0] = cr
            carry[j, 1] = ci
            y_blocks.append(_mm(sre_ref[:, lanes], ct[0, j]) - _mm(sim_ref[:, lanes], ct[1, j]))
        yss = jnp.concatenate(y_blocks, axis=1) + mv_ref[ROW_SSM_D:ROW_SSM_D + 1, 0:512] * u_ssm
        yss_ref[...] = yss.astype(SAVE_DTYPE)
        vg = _mm(_gelu(yss), w_glu[...]) + mv_ref[ROW_B_GLU:ROW_B_GLU + 1, :]
        vg_ref[...] = vg.astype(SAVE_DTYPE)
        y_ssm = _mm(vg[:, 0:512] * _sigmoid(vg[:, 512:1024]), w_su[...])
        ys_ref[...] = y_ssm.astype(SAVE_DTYPE)

        merged = _sigmoid(gl_pool) * y_pool + _sigmoid(gl_ssm) * y_ssm
        mo = _mm(merged, w_o[...])
        mo_ref[...] = mo.astype(SAVE_DTYPE)
        x2_ref[...] = xv + _row(prm_ref, 5) * mo

    def tok(width):
        return pl.BlockSpec((tm, width), lambda i: (i, 0))

    hbm = _HBM
    widths = (D_MODEL, D_MODEL, IN_WIDTH, N_STATE, N_STATE, 512, 512, D_MODEL, 512, D_MODEL, D_MODEL)
    dtypes = (F32, SAVE_DTYPE, SAVE_DTYPE, F32, F32) + (SAVE_DTYPE,) * 6
    return _launch(
        body, "mixer_forward", grid=(nt,), semantics=("arbitrary",), carry=carry, steps=_grid_steps(nt),
        out_shape=[jax.ShapeDtypeStruct((T, w), dt) for w, dt in zip(widths, dtypes)],
        in_specs=[tok(D_MODEL), _resident(prm), hbm, hbm, hbm, hbm, hbm, _resident(pool_w), _resident(mvec),
                  _resident(srow), _resident(bb), _resident(ct)],
        out_specs=[tok(w) for w in widths],
        operands=(x, prm, w_in_s, w_pu_s, w_glu_s, w_su_s, w_out, pool_w, mvec, srow, bb, ct),
        scratch=[
            pltpu.VMEM((D_MODEL, IN_WIDTH), MXU_DTYPE), pltpu.VMEM((512, D_MODEL), MXU_DTYPE),
            pltpu.VMEM((512, D_MODEL), MXU_DTYPE), pltpu.VMEM((512, D_MODEL), MXU_DTYPE),
            pltpu.VMEM((D_MODEL, D_MODEL), MXU_DTYPE),
            pltpu.VMEM((SSM_BLOCKS, 8, SCAN_ROWS, SSM_BLOCK_STATE), F32),
            pltpu.VMEM((SSM_BLOCKS, 2, SCAN_ROWS, SSM_BLOCK_STATE), F32),
            pltpu.VMEM((POOL_HALO, POOL_WIDTH), F32),
            pltpu.VMEM((2, tm, SSM_BLOCK_STATE), F32),
            pltpu.SemaphoreType.DMA((33,)),
        ])


def _mixer_backward(d2, prm, saved, w_pu_s, w_glu_s, w_su_s, w_out, pool_w, mvec, srow, bb, ct, carry=None):
    z, s_re, s_im, zp, q, y_pool, yss, vg, y_ssm = saved
    T = d2.shape[0]
    tm = min(T, TM_MIX_BWD)
    nt = T // tm
    n_tiles = tm // SCAN_ROWS

    def body(d_ref, prm_ref, z_ref, sre_ref, sim_ref, zp_ref, q_ref, yp_ref, yss_ref, vg_ref, ys_ref,
             w_pu_h, w_glu_h, w_su_h, w_out_h, pw_ref, mv_ref, srow_ref, bb, ct,
             dz_ref, dwo_h, dwpu_h, dwglu_h, dwsu_h, dpw_h, dbb_h, dct_h, vsum_h, da_h,
             w_pu, w_glu, w_su, w_o, pwb, coef, carry, hist, dre, lam,
             a_wo, a_wpu, a_wglu, a_wsu, a_pw, a_bb, a_ct, a_vs, a_da, st_wo, st_up, sems):
        i = pl.program_id(0)
        tile = nt - 1 - i

        @pl.when(i == 0)
        def _():
            cps = (_load_stack(w_pu_h, w_pu, sems, 0) + _load_stack(w_glu_h, w_glu, sems, 8)
                   + _load_stack(w_su_h, w_su, sems, 16))
            cps.append(pltpu.make_async_copy(w_out_h, w_o, sems.at[24]))
            cps[-1].start()
            pwb[...] = pw_ref[...].astype(MXU_DTYPE)
            for j in range(SSM_BLOCKS):
                lanes = slice(j * SSM_BLOCK_STATE, (j + 1) * SSM_BLOCK_STATE)
                for k, t in enumerate(_scan_coefficients(srow_ref[0:1, lanes], srow_ref[1:2, lanes], True)):
                    coef[j, k] = t
            for acc in (carry, hist, a_wo, a_wpu, a_wglu, a_wsu, a_pw, a_bb, a_ct, a_vs, a_da):
                acc[...] = jnp.zeros_like(acc)
            for cp in cps:
                cp.wait()

        dv = d_ref[...]
        zt = z_ref[...].astype(F32)
        u_ssm, gl_pool, gl_ssm = zt[:, 512:1024], zt[:, 1024:2048], zt[:, 2048:3072]
        y_p, y_s = yp_ref[...].astype(F32), ys_ref[...].astype(F32)
        sgp, sgs = _sigmoid(gl_pool), _sigmoid(gl_ssm)
        dmo = (_row(prm_ref, 5) * dv).astype(MXU_DTYPE)
        a_wo[...] += _mm_tn(sgp * y_p + sgs * y_s, dmo)
        dmerged = _mm_nt(dmo, w_o[...])
        dy_pool = dmerged * sgp
        dgl_pool = dmerged * y_p * (sgp * (1.0 - sgp))
        dy_ssm = dmerged * sgs
        dgl_ssm = dmerged * y_s * (sgs * (1.0 - sgs))

        scale = mv_ref[ROW_POOL_SCALE:ROW_POOL_SCALE + 1, 0:512]
        qv, zpv = q_ref[...].astype(F32), zp_ref[...]
        a_wpu[...] += _mm_tn(qv * scale, dy_pool)
        dp = _mm_nt(dy_pool, w_pu[...])
        dq = dp * scale
        a_vs[0:1, 0:512] += _colsum(dp * qv)
        a_vs[1:2, 0:512] += _colsum(dq)
        dzp_blocks = []
        for k in range(4):
            lanes = slice(k * 128, (k + 1) * 128)
            dzp_blocks.append(_mm_nt(dq[:, lanes], pwb[k]))
            a_pw[k] += _mm_tn(zpv[:, lanes], dq[:, lanes])
        dzp = jnp.concatenate(dzp_blocks, axis=1)
        t1 = (lax.broadcasted_iota(jnp.int32, (tm, 1), 0) + (tile * tm + 1)).astype(F32)
        gs = dzp / jnp.minimum(t1, _window_lanes())
        n_ext = tm + POOL_HALO
        ext = jnp.concatenate([gs, hist[...]], axis=0)
        v2 = ext + pltpu.roll(ext, n_ext - 1, 0)
        v4 = v2[:, 128:] + pltpu.roll(v2[:, 128:], n_ext - 2, 0)
        v8 = v4[:, 128:] + pltpu.roll(v4[:, 128:], n_ext - 4, 0)
        v16 = v8[:, 128:] + pltpu.roll(v8[:, 128:], n_ext - 8, 0)
        msum = jnp.concatenate([v2[:tm, :128], v4[:tm, :128], v8[:tm, :128], v16[:tm]], axis=1)
        hist[...] = gs[0:POOL_HALO, :]
        du_pool = msum - dzp

        vgv = vg_ref[...].astype(F32)
        val, gate = vgv[:, 0:512], vgv[:, 512:1024]
        sgg = _sigmoid(gate)
        a_wsu[...] += _mm_tn(val * sgg, dy_ssm)
        do = _mm_nt(dy_ssm, w_su[...])
        dvg = jnp.concatenate([do * sgg, do * val * (sgg * (1.0 - sgg))], axis=1)
        a_vs[3:4, :] += _colsum(dvg)
        yv = yss_ref[...].astype(F32)
        a_wglu[...] += _mm_tn(_gelu(yv), dvg)
        dyss = _mm_nt(dvg, w_glu[...]) * _gelu_grad(yv)
        a_vs[2:3, 0:512] += _colsum(dyss * u_ssm)
        du_blocks = []
        for j in range(SSM_BLOCKS):
            lanes = pl.ds(j * SSM_BLOCK_STATE, SSM_BLOCK_STATE)
            in_lanes = slice(j * 128, (j + 1) * 128)
            dyb = dyss[:, in_lanes].astype(MXU_DTYPE)
            ub = u_ssm[:, in_lanes].astype(MXU_DTYPE)
            dre[0] = _mm_nt(dyb, ct[0, j])
            dre[1] = -_mm_nt(dyb, ct[1, j])
            a_ct[0, j] += _mm_tn(sre_ref[:, lanes], dyb)
            a_ct[1, j] -= _mm_tn(sim_ref[:, lanes], dyb)
            a1r, a1i, a2r, a2i, a4r, a4i, pr, pi = [coef[j, k] for k in range(8)]
            rowi = lax.broadcasted_iota(jnp.int32, (SCAN_ROWS, SSM_BLOCK_STATE), 0)

            def step(tt, c, lanes=lanes, a1r=a1r, a1i=a1i, a2r=a2r, a2i=a2i, a4r=a4r, a4i=a4i, pr=pr, pi=pi, rowi=rowi):
                cr, ci, acc_r, acc_i = c
                rows = pl.ds(pl.multiple_of((n_tiles - 1 - tt) * SCAN_ROWS, SCAN_ROWS), SCAN_ROWS)
                xr, xi = dre[0, rows, :], dre[1, rows, :]
                for dstep, kr, ki in ((1, a1r, a1i), (2, a2r, a2i), (4, a4r, a4i)):
                    sr, si = pltpu.roll(xr, SCAN_ROWS - dstep, 0), pltpu.roll(xi, SCAN_ROWS - dstep, 0)
                    xr, xi = xr + kr * sr + ki * si, xi + kr * si - ki * sr
                xr, xi = xr + pr * cr + pi * ci, xi + pr * ci - pi * cr
                lam[0, rows, :] = xr
                lam[1, rows, :] = xi
                nr = jnp.where(rowi == SCAN_ROWS - 1, cr, pltpu.roll(xr, SCAN_ROWS - 1, 0))
                ni = jnp.where(rowi == SCAN_ROWS - 1, ci, pltpu.roll(xi, SCAN_ROWS - 1, 0))
                s_r, s_i = sre_ref[rows, lanes], sim_ref[rows, lanes]
                acc_r = acc_r + nr * s_r + ni * s_i
                acc_i = acc_i + ni * s_r - nr * s_i
                return (jnp.broadcast_to(xr[0:1, :], xr.shape), jnp.broadcast_to(xi[0:1, :], xi.shape), acc_r, acc_i)

            cr, ci, acc_r, acc_i = lax.fori_loop(0, n_tiles, step, (carry[j, 0], carry[j, 1], a_da[0, j], a_da[1, j]))
            carry[j, 0] = cr
            carry[j, 1] = ci
            a_da[0, j] = acc_r
            a_da[1, j] = acc_i
            lr_b, li_b = lam[0].astype(MXU_DTYPE), lam[1].astype(MXU_DTYPE)
            a_bb[0, j] += _mm_tn(ub, lr_b)
            a_bb[1, j] += _mm_tn(ub, li_b)
            du_blocks.append(_mm_nt(lr_b, bb[0, j]) + _mm_nt(li_b, bb[1, j]))
        du_ssm = jnp.concatenate(du_blocks, axis=1) + dyss * mv_ref[ROW_SSM_D:ROW_SSM_D + 1, 0:512]
        dz_ref[...] = jnp.concatenate([du_pool, du_ssm, dgl_pool, dgl_ssm], axis=1).astype(SAVE_DTYPE)

        @pl.when(i == nt - 1)
        def _():
            rows = D_MODEL // NDEV
            for k in range(NDEV):
                st_wo[k] = a_wo[k * rows:(k + 1) * rows, :].astype(WIRE_DTYPE)
                for a, acc in enumerate((a_wpu, a_wglu, a_wsu)):
                    st_up[a, k] = acc[:, k * 128:(k + 1) * 128].astype(WIRE_DTYPE)
            outs = ((st_wo, dwo_h), (st_up.at[0], dwpu_h), (st_up.at[1], dwglu_h), (st_up.at[2], dwsu_h),
                    (a_pw, dpw_h), (a_bb, dbb_h), (a_ct, dct_h), (a_vs, vsum_h), (a_da, da_h))
            cps = [pltpu.make_async_copy(src, dst, sems.at[k]) for k, (src, dst) in enumerate(outs)]
            for cp in cps:
                cp.start()
            for cp in cps:
                cp.wait()

    def tok(width):
        return pl.BlockSpec((tm, width), lambda i: (nt - 1 - i, 0))

    hbm = _HBM
    acc_shapes = [(D_MODEL, D_MODEL), (512, D_MODEL), (512, D_MODEL), (512, D_MODEL), (4, 128, 128),
                  (2, SSM_BLOCKS, 128, SSM_BLOCK_STATE), (2, SSM_BLOCKS, SSM_BLOCK_STATE, 128), (8, D_MODEL),
                  (2, SSM_BLOCKS, SCAN_ROWS, SSM_BLOCK_STATE)]
    stack_out = [jax.ShapeDtypeStruct((NDEV, D_MODEL // NDEV, D_MODEL), WIRE_DTYPE)] \
        + [jax.ShapeDtypeStruct((NDEV, 512, 128), WIRE_DTYPE)] * 3
    return _launch(
        body, "mixer_backward", grid=(nt,), semantics=("arbitrary",), carry=carry, steps=_grid_steps(nt),
        out_shape=[jax.ShapeDtypeStruct((T, IN_WIDTH), SAVE_DTYPE)] + stack_out
        + [jax.ShapeDtypeStruct(s, F32) for s in acc_shapes[4:]],
        in_specs=[tok(D_MODEL), _resident(prm), tok(IN_WIDTH), tok(N_STATE), tok(N_STATE), tok(512), tok(512),
                  tok(D_MODEL), tok(512), tok(D_MODEL), tok(D_MODEL), hbm, hbm, hbm, hbm, _resident(pool_w),
                  _resident(mvec), _resident(srow), _resident(bb), _resident(ct)],
        out_specs=[tok(IN_WIDTH)] + [hbm] * len(acc_shapes),
        operands=(d2, prm, z, s_re, s_im, zp, q, y_pool, yss, vg, y_ssm, w_pu_s, w_glu_s, w_su_s, w_out, pool_w, mvec,
                  srow, bb, ct),
        scratch=[
            pltpu.VMEM((512, D_MODEL), MXU_DTYPE), pltpu.VMEM((512, D_MODEL), MXU_DTYPE),
            pltpu.VMEM((512, D_MODEL), MXU_DTYPE), pltpu.VMEM((D_MODEL, D_MODEL), MXU_DTYPE),
            pltpu.VMEM((4, 128, 128), MXU_DTYPE),
            pltpu.VMEM((SSM_BLOCKS, 8, SCAN_ROWS, SSM_BLOCK_STATE), F32),
            pltpu.VMEM((SSM_BLOCKS, 2, SCAN_ROWS, SSM_BLOCK_STATE), F32),
            pltpu.VMEM((POOL_HALO, POOL_WIDTH), F32),
            pltpu.VMEM((2, tm, SSM_BLOCK_STATE), F32), pltpu.VMEM((2, tm, SSM_BLOCK_STATE), F32),
        ] + [pltpu.VMEM(s, F32) for s in acc_shapes]
        + [pltpu.VMEM((NDEV, D_MODEL // NDEV, D_MODEL), WIRE_DTYPE), pltpu.VMEM((3, NDEV, 512, 128), WIRE_DTYPE),
           pltpu.SemaphoreType.DMA((25,))])


def _mixer_in_backward(x, dz, prm, w_in_s):
    T = x.shape[0]
    tm = min(T, TM_MIX)
    nt = T // tm
    cols = IN_WIDTH // NDEV

    def body(x_ref, dz_ref, prm_ref, w_in_h, dh_ref, dw_ref, w_in, acc, sems):
        i = pl.program_id(0)

        @pl.when(i == 0)
        def _():
            cps = _load_stack(w_in_h, w_in, sems, 0)
            acc[...] = jnp.zeros_like(acc)
            for cp in cps:
                cp.wait()

        h, _, _ = _modulated(x_ref[...], prm_ref, 1, ROW_G_MIX)
        dzb = dz_ref[...].astype(MXU_DTYPE)
        dh_ref[0] = _mm_nt(dzb, w_in[...]).astype(SAVE_DTYPE)
        acc[...] += _mm_tn(h, dzb)

        @pl.when(i == nt - 1)
        def _():
            for k in range(NDEV):
                dw_ref[k] = acc[:, k * cols:(k + 1) * cols].astype(WIRE_DTYPE)

    return pl.pallas_call(
        body, name="mixer_in_backward", grid=(nt,),
        out_shape=[jax.ShapeDtypeStruct((1, T, D_MODEL), SAVE_DTYPE), jax.ShapeDtypeStruct((NDEV, D_MODEL, cols), WIRE_DTYPE)],
        in_specs=[pl.BlockSpec((tm, D_MODEL), lambda i: (i, 0)), pl.BlockSpec((tm, IN_WIDTH), lambda i: (i, 0)),
                  pl.BlockSpec(prm.shape, lambda i: (0, 0)), pl.BlockSpec(memory_space=pl.ANY)],
        out_specs=[pl.BlockSpec((1, tm, D_MODEL), lambda i: (0, i, 0)),
                   pl.BlockSpec((NDEV, D_MODEL, cols), lambda i: (0, 0, 0))],
        scratch_shapes=[pltpu.VMEM((D_MODEL, IN_WIDTH), MXU_DTYPE), pltpu.VMEM((D_MODEL, IN_WIDTH), F32),
                        pltpu.SemaphoreType.DMA((8,))],
        compiler_params=_params("arbitrary"),
    )(x, dz, prm, w_in_s)


def _ssm_dense_backward(dbb, da, srow, b_dense):
    def body(dbb_ref, da_ref, srow_ref, bd_ref, db_ref, df_ref):
        df_re, df_im = [], []
        da_re = [_colsum(da_ref[0, j]) for j in range(SSM_BLOCKS)]
        da_im = [_colsum(da_ref[1, j]) for j in range(SSM_BLOCKS)]
        for j in range(SSM_BLOCKS):
            lanes = slice(j * SSM_BLOCK_STATE, (j + 1) * SSM_BLOCK_STATE)
            f_re, f_im = srow_ref[2:3, lanes], srow_ref[3:4, lanes]
            g_re, g_im = dbb_ref[0, j], dbb_ref[1, j]
            b_re, b_im = bd_ref[0, j], bd_ref[1, j]
            db_ref[0, j] = f_re * g_re + f_im * g_im
            db_ref[1, j] = f_re * g_im - f_im * g_re
            df_re.append(_colsum(g_re * b_re + g_im * b_im))
            df_im.append(_colsum(g_im * b_re - g_re * b_im))
        df_ref[...] = jnp.concatenate([jnp.concatenate(df_re, axis=1), jnp.concatenate(df_im, axis=1),
                                       jnp.concatenate(da_re, axis=1), jnp.concatenate(da_im, axis=1),
                                       jnp.zeros((4, N_STATE), F32)], axis=0)

    return pl.pallas_call(body, name="ssm_dense_backward",
                          out_shape=[jax.ShapeDtypeStruct(b_dense.shape, F32), jax.ShapeDtypeStruct((8, N_STATE), F32)],
                          compiler_params=pltpu.CompilerParams(vmem_limit_bytes=VMEM_LIMIT))(dbb, da, srow, b_dense)


def _adamw_update(w, g, m, v):
    m = ADAM_B1 * m + (1.0 - ADAM_B1) * g
    v = ADAM_B2 * v + (1.0 - ADAM_B2) * (g * g)
    m_hat = m / (1.0 - ADAM_B1 ** ADAM_STEP)
    v_hat = v / (1.0 - ADAM_B2 ** ADAM_STEP)
    delta = -ADAM_LR * (m_hat / (jnp.sqrt(v_hat) + ADAM_EPS) + ADAM_WD * w)
    return delta, m, v


def _adam_rows(shape):
    rows, cols = shape
    tr = rows
    while tr * cols * 4 > (1 << 20) and tr % 16 == 0:
        tr //= 2
    return tr


def _adam_sharded(w, m, v, land, order, name):
    R, C = w.shape
    tr = _adam_rows((R, C))

    def body(w_ref, m_ref, v_ref, land_ref, order_ref, g_ref, d_ref, mo_ref, vo_ref):
        g = land_ref[0].astype(F32)
        for b in range(1, NDEV):
            g = g + land_ref[b].astype(F32)
        g_ref[...] = g
        d_ref[...], mo_ref[...], vo_ref[...] = _adamw_update(w_ref[...], g, m_ref[...], v_ref[...])

    blk = pl.BlockSpec((tr, C), lambda i: (i, 0))
    return pl.pallas_call(
        body, name=name, grid=(R // tr,),
        out_shape=[jax.ShapeDtypeStruct((R, C), F32)] * 4,
        in_specs=[blk, blk, blk, pl.BlockSpec((NDEV, tr, C), lambda i: (0, i, 0)), _HBM],
        out_specs=[blk] * 4,
        compiler_params=_params("arbitrary"),
    )(w, m, v, land, order)


def _adam_ada(w, m, v, sc_all, dmod_cols):
    R, C = w.shape
    tr = 256

    def body(w_ref, m_ref, v_ref, sc_ref, dm_ref, g_ref, d_ref, mo_ref, vo_ref):
        g = _mm_tn(sc_ref[...], dm_ref[...])
        g_ref[...] = g
        d_ref[...], mo_ref[...], vo_ref[...] = _adamw_update(w_ref[...], g, m_ref[...], v_ref[...])

    blk = pl.BlockSpec((tr, C), lambda i: (i, 0))
    return pl.pallas_call(
        body, name="adam_w_ada", grid=(R // tr,),
        out_shape=[jax.ShapeDtypeStruct((R, C), F32)] * 4,
        in_specs=[blk, blk, blk, pl.BlockSpec((8, tr), lambda i: (0, i)), pl.BlockSpec((8, C), lambda i: (0, 0))],
        out_specs=[blk] * 4,
        compiler_params=_params("arbitrary"),
    )(w, m, v, sc_all, dmod_cols)


def _adam_small(w, g, m, v):
    def body(w_ref, g_ref, m_ref, v_ref, d_ref, mo_ref, vo_ref):
        d_ref[...], mo_ref[...], vo_ref[...] = _adamw_update(w_ref[...], g_ref[...], m_ref[...], v_ref[...])

    return pl.pallas_call(body, name="adam_small", out_shape=[jax.ShapeDtypeStruct(w.shape, F32)] * 3,
                          compiler_params=pltpu.CompilerParams(vmem_limit_bytes=VMEM_LIMIT))(w, g, m, v)


def _block_diag_in(b):
    bt = jnp.transpose(b, (0, 2, 1)).reshape(SSM_BLOCKS, 8, SSM_GROUP, SSM_STATE)
    eye = jnp.eye(8, dtype=bool)[None, :, None, :, None]
    return jnp.where(eye, bt[:, :, :, None, :], 0.0).reshape(SSM_BLOCKS, 128, SSM_BLOCK_STATE)


def _block_diag_out(c):
    ct = jnp.transpose(c, (0, 2, 1)).reshape(SSM_BLOCKS, 8, SSM_STATE, SSM_GROUP)
    eye = jnp.eye(8, dtype=bool)[None, :, None, :, None]
    return jnp.where(eye, ct[:, :, :, None, :], 0.0).reshape(SSM_BLOCKS, SSM_BLOCK_STATE, 128)


def _diag_blocks(dense, rows, cols):
    d5 = dense.reshape(SSM_BLOCKS, 8, rows, 8, cols)
    return jnp.stack([d5[:, a, :, a, :] for a in range(8)], axis=1).reshape(32, rows, cols)


def _pack_small(ada_vec, parts):
    rest = jnp.concatenate([parts[n].reshape(-1) for n, _ in SMALL_PARAMS])
    rest = jnp.pad(rest, (0, NDEV * REST_ROWS * 128 - SMALL_TOTAL)).reshape(NDEV, REST_ROWS, 128)
    return jnp.concatenate([ada_vec.reshape(NDEV, ADA_ROWS, 128), rest,
                            jnp.zeros((NDEV, PACK_ROWS - ADA_ROWS - REST_ROWS, 128), F32)], axis=1)


def _unpack_small(pack, shapes):
    ada_vec = pack[:, :ADA_ROWS].reshape(-1)
    rest = pack[:, ADA_ROWS:ADA_ROWS + REST_ROWS].reshape(-1)
    out, off = {}, 0
    for n, size in SMALL_PARAMS:
        out[n] = rest[off:off + size].reshape(shapes[n])
        off += size
    return ada_vec, out


WEIGHT_ORDER = ('w_ada', 'b_ada', 'g_ffn1', 'w_ffn1_in', 'w_ffn1_out', 'g_mix', 'w_in', 'pool_w', 'pool_b',
                'pool_scale', 'w_pool_up', 'ssm_lam_re_log', 'ssm_lam_im', 'ssm_log_dt', 'ssm_b_re', 'ssm_b_im',
                'ssm_c_re', 'ssm_c_im', 'ssm_d', 'w_glu', 'b_glu', 'w_ssm_up', 'w_out', 'g_ffn2', 'w_ffn2_in',
                'w_ffn2_out', 'g_final')
GATHERED = ('w_ffn1_in', 'w_ffn1_out', 'w_in', 'w_pool_up', 'w_glu', 'w_ssm_up', 'w_out', 'w_ffn2_in', 'w_ffn2_out')


def kernel(x, c, w_ada, b_ada, g_ffn1, w_ffn1_in, w_ffn1_out, g_mix, w_in, pool_w, pool_b, pool_scale, w_pool_up, ssm_lam_re_log, ssm_lam_im, ssm_log_dt, ssm_b_re, ssm_b_im, ssm_c_re, ssm_c_im, ssm_d, w_glu, b_glu, w_ssm_up, w_out, g_ffn2, w_ffn2_in, w_ffn2_out, g_final, loss_target, m_w_ada, m_b_ada, m_g_ffn1, m_w_ffn1_in, m_w_ffn1_out, m_g_mix, m_w_in, m_pool_w, m_pool_b, m_pool_scale, m_w_pool_up, m_ssm_lam_re_log, m_ssm_lam_im, m_ssm_log_dt, m_ssm_b_re, m_ssm_b_im, m_ssm_c_re, m_ssm_c_im, m_ssm_d, m_w_glu, m_b_glu, m_w_ssm_up, m_w_out, m_g_ffn2, m_w_ffn2_in, m_w_ffn2_out, m_g_final, v_w_ada, v_b_ada, v_g_ffn1, v_w_ffn1_in, v_w_ffn1_out, v_g_mix, v_w_in, v_pool_w, v_pool_b, v_pool_scale, v_w_pool_up, v_ssm_lam_re_log, v_ssm_lam_im, v_ssm_log_dt, v_ssm_b_re, v_ssm_b_im, v_ssm_c_re, v_ssm_c_im, v_ssm_d, v_w_glu, v_b_glu, v_w_ssm_up, v_w_out, v_g_ffn2, v_w_ffn2_in, v_w_ffn2_out, v_g_final):
    args = locals()
    W = {n: args[n] for n in WEIGHT_ORDER}
    M = {n: args["m_" + n] for n in WEIGHT_ORDER}
    V = {n: args["v_" + n] for n in WEIGHT_ORDER}
    shapes = {n: W[n].shape for n in WEIGHT_ORDER}
    xt, tgt = x[0], loss_target[0]

    shard = dict(zip(GATHERED, _cast_shards([W[n][0] for n in GATHERED])))
    stacks = {}

    def gather(names):
        return _Gather([shard[n] for n in names])

    def gathered(names, results):
        stacks.update(zip(names, results))

    ffn1_w, ffn2_w = ('w_ffn1_in', 'w_ffn1_out'), ('w_ffn2_in', 'w_ffn2_out')
    mix_w = ('w_in', 'w_pool_up', 'w_glu', 'w_ssm_up', 'w_out')
    mod_cols, sc_all, *res = _ada_forward(c, W['w_ada'][0], b_ada.reshape(NDEV, -1), gather(ffn1_w))
    gathered(ffn1_w, res)
    win1 = stacks['w_ffn1_in'].reshape(2, 4, D_MODEL, FF_SHARD)
    wout1 = stacks['w_ffn1_out'].reshape(4, FF_SHARD, D_MODEL)
    prm = jnp.concatenate([mod_cols.reshape(9, D_MODEL), g_ffn1, g_mix, g_ffn2, g_final[None], jnp.zeros((3, D_MODEL), F32)], axis=0)
    pad512 = jnp.zeros((1, D_MODEL - 512), F32)
    mvec = jnp.concatenate([jnp.concatenate([pool_b, pad512], axis=1), jnp.concatenate([pool_scale, pad512], axis=1),
                            jnp.concatenate([ssm_d, pad512], axis=1), b_glu, jnp.zeros((4, D_MODEL), F32)], axis=0)
    log_dt_col = ssm_log_dt[0][:, None]
    coeffs = _ssm_params_forward(ssm_lam_re_log[0], ssm_lam_im[0], log_dt_col)
    srow = jnp.stack([t.reshape(N_STATE) for t in coeffs], axis=0)
    b_dense = jnp.stack([_block_diag_in(ssm_b_re[0]), _block_diag_in(ssm_b_im[0])], axis=0)
    c_dense = jnp.stack([_block_diag_out(ssm_c_re[0]), _block_diag_out(ssm_c_im[0])], axis=0)
    bb, ct = _ssm_dense_forward(srow, b_dense, c_dense)
    pw = pool_w[0]

    x1, f1, ab1, *res = _ffn_forward(xt, prm, win1, wout1, 0, ROW_G_FFN1, "ffn1_forward", gather(mix_w))
    gathered(mix_w, res)
    w_out_full = stacks['w_out'].reshape(D_MODEL, D_MODEL)
    res = _mixer_forward(x1, prm, stacks['w_in'], stacks['w_pool_up'], stacks['w_glu'], stacks['w_ssm_up'],
                         w_out_full, pw, mvec, srow, bb, ct, gather(ffn2_w))
    x2, mo, saved = res[0], res[1], res[2:11]
    gathered(ffn2_w, res[11:])
    win2 = stacks['w_ffn2_in'].reshape(2, 4, D_MODEL, FF_SHARD)
    wout2 = stacks['w_ffn2_out'].reshape(4, FF_SHARD, D_MODEL)
    x3, f3, ab3 = _ffn_forward(x2, prm, win2, wout2, 2, ROW_G_FFN2, "ffn2_forward")
    d3, fin = _final_loss(x3, tgt, prm)
    loss = lax.psum(fin[1, 0], ("x", "y", "c"))

    lands = {}

    def scatter(grads):
        names = list(grads)
        return _Scatter([grads[n][0] for n in names], [grads[n][1] for n in names], [W[n].shape[1:] for n in names])

    def scattered(grads, results):
        lands.update(zip(grads, results))

    parts3, dwin2, dwout2 = _ffn_backward(x2, d3, ab3, prm, win2, wout2, 2, ROW_G_FFN2, "ffn2_backward")
    d2, sums3 = _norm_backward(parts3, x2, d3, f3, prm, 2, ROW_G_FFN2, 0.5, "ffn2_norm_backward")
    g_ffn2_w = {'w_ffn2_in': (dwin2, _halves), 'w_ffn2_out': (dwout2.reshape(NDEV, -1, D_MODEL), _stacked)}
    res = _mixer_backward(d2, prm, saved, stacks['w_pool_up'], stacks['w_glu'], stacks['w_ssm_up'], w_out_full, pw, mvec,
                          srow, bb, ct, scatter(g_ffn2_w))
    dz, dwo, dwpu, dwglu, dwsu, dpw, dbb, dct, vsum, da = res[:10]
    scattered(g_ffn2_w, res[10:])
    parts2, dwin_mix = _mixer_in_backward(x1, dz, prm, stacks['w_in'])
    d1, sums2 = _norm_backward(parts2, x1, d2, mo, prm, 1, ROW_G_MIX, 1.0, "mixer_norm_backward")
    g_mix_w = {'w_in': (dwin_mix, _stacked), 'w_pool_up': (dwpu, _stacked), 'w_glu': (dwglu, _stacked),
               'w_ssm_up': (dwsu, _stacked), 'w_out': (dwo, _stacked)}
    parts1, dwin1, dwout1, *res = _ffn_backward(xt, d1, ab1, prm, win1, wout1, 0, ROW_G_FFN1, "ffn1_backward",
                                                scatter(g_mix_w))
    scattered(g_mix_w, res)
    d0, sums1 = _norm_backward(parts1, xt, d1, f1, prm, 0, ROW_G_FFN1, 0.5, "ffn1_norm_backward")

    db_dense, df_rows = _ssm_dense_backward(dbb, da, srow, b_dense)
    cot = [df_rows[r].reshape(32, 64) for r in (2, 3, 0, 1)]
    d_lrl, d_li, d_ldt = _ssm_params_backward(ssm_lam_re_log[0], ssm_lam_im[0], log_dt_col, cot)
    small_grads = {
        'g_ffn1': sums1[0], 'g_mix': sums2[0], 'g_ffn2': sums3[0], 'g_final': fin[0], 'pool_w': dpw,
        'pool_b': vsum[1, :512], 'pool_scale': vsum[0, :512], 'ssm_lam_re_log': d_lrl, 'ssm_lam_im': d_li,
        'ssm_log_dt': d_ldt, 'ssm_b_re': jnp.transpose(_diag_blocks(db_dense[0], SSM_GROUP, SSM_STATE), (0, 2, 1)),
        'ssm_b_im': jnp.transpose(_diag_blocks(db_dense[1], SSM_GROUP, SSM_STATE), (0, 2, 1)),
        'ssm_c_re': jnp.transpose(_diag_blocks(dct[0], SSM_STATE, SSM_GROUP), (0, 2, 1)),
        'ssm_c_im': jnp.transpose(_diag_blocks(dct[1], SSM_STATE, SSM_GROUP), (0, 2, 1)),
        'ssm_d': vsum[2, :512], 'b_glu': vsum[3],
    }
    dmod = jnp.concatenate([sums1[1:4], sums2[1:4], sums3[1:4]], axis=0).reshape(-1)
    total, landed = _allreduce_small(_pack_small(dmod, small_grads))
    dmod_cols = landed[:, :ADA_ROWS].reshape(NDEV, ADA_ROWS * 128)

    g_ffn1_w = {'w_ffn1_in': (dwin1, _halves), 'w_ffn1_out': (dwout1.reshape(NDEV, -1, D_MODEL), _stacked)}
    last_views = [g_ffn1_w[n][1] for n in ffn1_w]
    send_sems, recv_sems, last_src, last_land, token = _scatter_start(
        [g_ffn1_w[n][0] for n in ffn1_w], last_views, [W[n].shape[1:] for n in ffn1_w], [total])
    total = total + token[0:1, 0:1]

    grad, delta, new_m, new_v = {}, {}, {}, {}

    def adam_sharded(n):
        res = _adam_sharded(W[n][0], M[n][0], V[n][0], lands[n], token, "adam_" + n)
        grad[n], delta[n], new_m[n], new_v[n] = [r[None] for r in res]
        return res[3]

    done = [adam_sharded(n) for n in GATHERED if n not in ffn1_w]
    res = _adam_ada(W['w_ada'][0], M['w_ada'][0], V['w_ada'][0], sc_all, dmod_cols + token[0:1, 0:1])
    grad['w_ada'], delta['w_ada'], new_m['w_ada'], new_v['w_ada'] = [r[None] for r in res]
    done.append(res[3])

    flat = lambda t: t.reshape(NDEV * PACK_ROWS, 128)
    small_w = flat(_pack_small(b_ada.reshape(-1), W))
    small_m = flat(_pack_small(m_b_ada.reshape(-1), M))
    small_v = flat(_pack_small(v_b_ada.reshape(-1), V))
    res = _adam_small(small_w, flat(total), small_m, small_v)
    done.append(res[2])
    for dst, packed in zip((grad, delta, new_m, new_v), (total, *res)):
        ada_vec, rest = _unpack_small(packed.reshape(NDEV, PACK_ROWS, 128), shapes)
        dst.update(rest)
        dst['b_ada'] = ada_vec.reshape(shapes['b_ada'])

    lands.update(zip(ffn1_w, _scatter_wait(send_sems, recv_sems, last_src, last_land, last_views, done)))
    for n in ffn1_w:
        adam_sharded(n)

    return (loss, d0[None], *[grad[n] for n in WEIGHT_ORDER], *[delta[n] for n in WEIGHT_ORDER],
            *[new_m[n] for n in WEIGHT_ORDER], *[new_v[n] for n in WEIGHT_ORDER])
```

```python
import functools

import jax
import jax.numpy as jnp
from jax import lax
from jax.experimental import pallas as pl
from jax.experimental.pallas import tpu as pltpu

F32 = jnp.float32
MXU_DTYPE = jnp.bfloat16
WIRE_DTYPE = jnp.bfloat16
SAVE_DTYPE = jnp.bfloat16

NDEV = 8
D_MODEL = 1024
D_FF = 2816
FF_SHARD = 2 * D_FF // NDEV
POOL_WIDTH = 512
POOL_GROUP = 128
SSM_WIDTH = 512
SSM_STATE = 64
SSM_GROUP = 16
SSM_BLOCKS = 4
SSM_BLOCK_STATE = 512
N_STATE = 2048
IN_WIDTH = 3072
EPS = 1e-6
ADAM_LR = 0.001
ADAM_B1 = 0.9
ADAM_B2 = 0.999
ADAM_EPS = 1e-08
ADAM_WD = 0.01
ADAM_STEP = 10

TM_FFN = 512
FFN_BWD_CHUNK = 256
TM_MIX = 256
TM_MIX_BWD = 256
TM_EW = 512
SCAN_ROWS = 8
POOL_HALO = 16
VMEM_LIMIT = 60 * 1024 * 1024

ROW_G_FFN1, ROW_G_MIX, ROW_G_FFN2, ROW_G_FINAL = 9, 10, 11, 12
ROW_POOL_B, ROW_POOL_SCALE, ROW_SSM_D, ROW_B_GLU = 0, 1, 2, 3

SMALL_PARAMS = (
    ("g_ffn1", 1024), ("g_mix", 1024), ("g_ffn2", 1024), ("g_final", 1024), ("pool_w", 65536),
    ("pool_b", 512), ("pool_scale", 512), ("ssm_lam_re_log", 2048), ("ssm_lam_im", 2048),
    ("ssm_log_dt", 32), ("ssm_b_re", 32768), ("ssm_b_im", 32768), ("ssm_c_re", 32768),
    ("ssm_c_im", 32768), ("ssm_d", 512), ("b_glu", 1024),
)
SMALL_TOTAL = sum(n for _, n in SMALL_PARAMS)
ADA_ROWS = 9
REST_ROWS = 203
PACK_ROWS = 216
MESH = pl.DeviceIdType.MESH


def _mm(a, b):
    return jnp.dot(a.astype(MXU_DTYPE), b.astype(MXU_DTYPE), preferred_element_type=F32)


def _mm_nt(a, b):
    return lax.dot_general(a.astype(MXU_DTYPE), b.astype(MXU_DTYPE), (((1,), (1,)), ((), ())),
                           preferred_element_type=F32)


def _mm_tn(a, b):
    return lax.dot_general(a.astype(MXU_DTYPE), b.astype(MXU_DTYPE), (((0,), (0,)), ((), ())),
                           preferred_element_type=F32)


def _rms_scale(x):
    return lax.rsqrt(jnp.mean(x * x, axis=-1, keepdims=True) + EPS)


def _sigmoid(x):
    return jax.nn.sigmoid(x)


def _colsum(x):
    return jnp.sum(x, axis=0, keepdims=True)


def _row(ref, r):
    return ref[r:r + 1, :]


def _params(*sem):
    return pltpu.CompilerParams(dimension_semantics=sem, vmem_limit_bytes=VMEM_LIMIT)


def _resident(a):
    return pl.BlockSpec(a.shape, lambda *_: (0,) * a.ndim, pipeline_mode=pl.Buffered(1))


def _me():
    return lax.axis_index("x"), lax.axis_index("y"), lax.axis_index("c")


def _peer(rel):
    x, y, c = _me()
    px = 1 - x if rel & 4 else x
    py = 1 - y if rel & 2 else y
    pc = 1 - c if rel & 1 else c
    return (px, py, pc), 4 * px + 2 * py + pc


_HBM = pl.BlockSpec(memory_space=pl.ANY)
_HBM_ONLY = pl.BlockSpec(memory_space=pltpu.HBM)


def _stacked(ref, p):
    return ref.at[p]


def _halves(ref, p):
    return ref.at[p // 4, p % 4]


class _Gather:
    def __init__(self, shards):
        self.operands = list(shards)
        self.n = len(shards)
        self.out_shape = [jax.ShapeDtypeStruct((NDEV,) + s.shape, s.dtype) for s in shards]
        self.scratch = [pltpu.SemaphoreType.DMA((7 * self.n,)), pltpu.SemaphoreType.DMA((7 * self.n,)),
                        pltpu.SemaphoreType.DMA((self.n,))]

    def plan(self, srcs, outs, sems):
        send_sems, recv_sems, local_sems = sems
        n = self.n
        x, y, c = _me()
        me = 4 * x + 2 * y + c
        here, sibling = (x, y, c), (x, y, 1 - c)
        chips = [(1 - x, y), (x, 1 - y), (1 - x, 1 - y)]

        def blk(px, py, pc):
            return 4 * px + 2 * py + pc

        def copy(a, k, block, to, src=None):
            return pltpu.make_async_remote_copy(
                src_ref=outs[a].at[block] if src is None else src, dst_ref=outs[a].at[block],
                send_sem=send_sems.at[7 * a + k], recv_sem=recv_sems.at[7 * a + k], device_id=to, device_id_type=MESH)

        def mine(a):
            return pltpu.make_async_copy(srcs[a], outs[a].at[me], local_sems.at[a])

        def first(a):
            return [copy(a, 0, me, sibling, src=srcs[a])] + [copy(a, 1 + j, me, (*chip, c), src=srcs[a])
                                                              for j, chip in enumerate(chips)]

        def start():
            for a in range(n):
                mine(a).start()
                for cp in first(a):
                    cp.start()

        def forward():
            for a in range(n):
                for j, chip in enumerate(chips):
                    copy(a, 1 + j, blk(*chip, c), here).wait_recv()
                    copy(a, 4 + j, blk(*chip, c), sibling).start()

        def finish():
            for a in range(n):
                copy(a, 0, blk(x, y, 1 - c), here).wait_recv()
                for j, chip in enumerate(chips):
                    copy(a, 4 + j, blk(*chip, 1 - c), here).wait_recv()
            for a in range(n):
                mine(a).wait()
                for cp in first(a):
                    cp.wait_send()
                for j, chip in enumerate(chips):
                    copy(a, 4 + j, blk(*chip, c), sibling).wait_send()

        return start, forward, finish


class _Scatter:
    def __init__(self, arrays, views, shard_shapes):
        self.operands = list(arrays)
        self.views = list(views)
        self.n = len(arrays)
        self.out_shape = [jax.ShapeDtypeStruct((NDEV,) + tuple(s), a.dtype) for s, a in zip(shard_shapes, arrays)]
        self.scratch = [pltpu.SemaphoreType.DMA((7 * self.n,)), pltpu.SemaphoreType.DMA((7 * self.n,)),
                        pltpu.SemaphoreType.DMA((self.n,))]

    def plan(self, srcs, outs, sems):
        send_sems, recv_sems, local_sems = sems
        n, views = self.n, self.views
        x, y, c = _me()
        me = 4 * x + 2 * y + c

        def mine(a):
            return pltpu.make_async_copy(views[a](srcs[a], me), outs[a].at[me], local_sems.at[a])

        def copy(a, rel, sending):
            to, p = _peer(rel)
            return pltpu.make_async_remote_copy(
                src_ref=views[a](srcs[a], p), dst_ref=outs[a].at[me if sending else p],
                send_sem=send_sems.at[7 * a + rel - 1], recv_sem=recv_sems.at[7 * a + rel - 1],
                device_id=to if sending else (x, y, c), device_id_type=MESH)

        def start():
            for a in range(n):
                mine(a).start()
            for rel in range(1, 8):
                for a in range(n):
                    copy(a, rel, True).start()

        def forward():
            pass

        def finish():
            for rel in range(1, 8):
                for a in range(n):
                    copy(a, rel, False).wait_recv()
            for rel in range(1, 8):
                for a in range(n):
                    copy(a, rel, True).wait_send()
            for a in range(n):
                mine(a).wait()

        return start, forward, finish


def _launch(body, name, out_shape, in_specs, out_specs, operands, scratch=(), grid=None, semantics=None,
            carry=None, steps=None):
    out_shape, in_specs, out_specs = list(out_shape), list(in_specs), list(out_specs)
    operands, scratch = list(operands), list(scratch)
    n_in, n_out, n_scr = len(in_specs), len(out_shape), len(scratch)
    kernel_body = body
    if carry is not None:
        k = carry.n

        def kernel_body(*refs):
            ins, cin = refs[:n_in], refs[n_in:n_in + k]
            outs, cout = refs[n_in + k:n_in + k + n_out], refs[n_in + k + n_out:n_in + 2 * k + n_out]
            rest = refs[n_in + 2 * k + n_out:]
            scr, csem = rest[:n_scr], rest[n_scr:]
            start, forward, finish = carry.plan(cin, cout, csem)
            if steps is None:
                start()
                body(*ins, *outs, *scr)
                forward()
                finish()
            else:
                pl.when(steps()[0])(start)
                body(*ins, *outs, *scr)
                pl.when(steps()[1])(forward)
                pl.when(steps()[2])(finish)

        in_specs += [_HBM] * k
        out_shape += carry.out_shape
        out_specs += [_HBM] * k
        operands += carry.operands
        scratch += carry.scratch
    kwargs = {} if grid is None else {"grid": grid}
    params = pltpu.CompilerParams(vmem_limit_bytes=VMEM_LIMIT) if semantics is None else _params(*semantics)
    return pl.pallas_call(kernel_body, name=name, out_shape=out_shape, in_specs=in_specs, out_specs=out_specs,
                          scratch_shapes=scratch, compiler_params=params, **kwargs)(*operands)


def _grid_steps(nt):
    def steps():
        i = pl.program_id(0)
        return i == 0, i == nt // 2, i == nt - 1
    return steps


def _cast_shards(shards):
    n = len(shards)

    def body(*refs):
        for a in range(n):
            refs[n + a][...] = refs[a][...].astype(WIRE_DTYPE)

    return pl.pallas_call(body, name="cast_shards",
                          out_shape=[jax.ShapeDtypeStruct(s.shape, WIRE_DTYPE) for s in shards],
                          compiler_params=pltpu.CompilerParams(vmem_limit_bytes=VMEM_LIMIT))(*shards)


_SEM = pl.BlockSpec(memory_space=pltpu.SEMAPHORE)
_DATAFLOW = pltpu.SideEffectType.DATAFLOW_SIDE_EFFECTING


def _split_copy(arrays, views, landing, send_sems, recv_sems, a, rel):
    to, p = _peer(rel)
    x, y, c = _me()
    return pltpu.make_async_remote_copy(
        src_ref=views[a](arrays[a], p), dst_ref=landing[a].at[4 * x + 2 * y + c],
        send_sem=send_sems.at[NDEV * a + rel], recv_sem=recv_sems.at[NDEV * a + rel], device_id=to, device_id_type=MESH)


def _scatter_start(arrays, views, shard_shapes, after):
    n = len(arrays)
    landing = [pltpu.with_memory_space_constraint(lax.empty((NDEV,) + tuple(s), a.dtype), pltpu.HBM)
               for s, a in zip(shard_shapes, arrays)]
    arrays = [pltpu.with_memory_space_constraint(a, pltpu.HBM) for a in arrays]

    def body(*refs):
        ins, land = refs[:n], refs[n:2 * n]
        send_sems, recv_sems = refs[2 * n + len(after)], refs[2 * n + len(after) + 1]
        token = refs[-1]
        for rel in range(NDEV):
            for a in range(n):
                _split_copy(ins, views, land, send_sems, recv_sems, a, rel).start()
        token[...] = jnp.zeros_like(token)

    res = pl.pallas_call(
        body, name="scatter_start",
        out_shape=[pltpu.SemaphoreType.DMA((NDEV * n,)), pltpu.SemaphoreType.DMA((NDEV * n,))]
        + [pltpu.HBM(a.shape, a.dtype) for a in arrays] + [pltpu.HBM(l.shape, l.dtype) for l in landing]
        + [jax.ShapeDtypeStruct((8, 128), F32)],
        in_specs=[_HBM_ONLY] * (2 * n) + [_HBM] * len(after),
        out_specs=[_SEM, _SEM] + [_HBM_ONLY] * (2 * n) + [pl.BlockSpec(memory_space=pltpu.VMEM)],
        input_output_aliases={i: 2 + i for i in range(2 * n)},
        compiler_params=pltpu.CompilerParams(has_side_effects=_DATAFLOW),
    )(*arrays, *landing, *after)
    return res[0], res[1], res[2:2 + n], res[2 + n:2 + 2 * n], res[-1]


def _scatter_wait(send_sems, recv_sems, arrays, landing, views, after):
    n = len(arrays)

    def body(*refs):
        ins, land = refs[:n], refs[n:2 * n]
        send, recv = refs[2 * n], refs[2 * n + 1]
        for rel in range(NDEV):
            for a in range(n):
                cp = _split_copy(ins, views, land, send, recv, a, rel)
                cp.wait_send()
                cp.wait_recv()

    res = pl.pallas_call(
        body, name="scatter_wait",
        out_shape=[pltpu.HBM(a.shape, a.dtype) for a in arrays] + [pltpu.HBM(l.shape, l.dtype) for l in landing],
        in_specs=[_HBM_ONLY] * (2 * n) + [_SEM, _SEM] + [_HBM] * len(after),
        out_specs=[_HBM_ONLY] * (2 * n),
        input_output_aliases={i: i for i in range(2 * n)},
        compiler_params=pltpu.CompilerParams(has_side_effects=_DATAFLOW),
    )(*arrays, *landing, send_sems, recv_sems, *after)
    return res[n:]


def _ada_forward(c_row, w_ada, b_ada8, carry):
    cols = w_ada.shape[1]

    def body(c_ref, w_ref, b_ref, mod_ref, sc_ref, c_all, send_buf, recv_buf, send1, recv1, send2, recv2):
        x, y, c = _me()
        me = 4 * x + 2 * y + c
        rowi = lax.broadcasted_iota(jnp.int32, (8, D_MODEL), 0)
        c_all[me] = jnp.broadcast_to(c_ref[...], (8, D_MODEL))
        copies = []
        for rel in range(1, 8):
            to, _ = _peer(rel)
            cp = pltpu.make_async_remote_copy(src_ref=c_all.at[me], dst_ref=c_all.at[me], send_sem=send1.at[rel - 1],
                                              recv_sem=recv1.at[rel - 1], device_id=to, device_id_type=MESH)
            cp.start()
            copies.append(cp)
        for rel in range(1, 8):
            _, p = _peer(rel)
            pltpu.make_async_remote_copy(src_ref=c_all.at[p], dst_ref=c_all.at[p], send_sem=send1.at[rel - 1],
                                         recv_sem=recv1.at[rel - 1], device_id=(x, y, c), device_id_type=MESH).wait_recv()
        for cp in copies:
            cp.wait_send()
        cmat = jnp.zeros((8, D_MODEL), F32)
        for b in range(8):
            cmat = jnp.where(rowi == b, c_all[b], cmat)
        sc = cmat * _sigmoid(cmat)
        sc_ref[...] = sc
        modcols = _mm(sc, w_ref[...]) + b_ref[pl.ds(me, 1), :]
        for b in range(8):
            send_buf[b] = jnp.broadcast_to(modcols[b:b + 1, :], (8, cols))
        recv_buf[me] = send_buf[me]
        copies = []
        for rel in range(1, 8):
            to, p = _peer(rel)
            cp = pltpu.make_async_remote_copy(src_ref=send_buf.at[p], dst_ref=recv_buf.at[me], send_sem=send2.at[rel - 1],
                                              recv_sem=recv2.at[rel - 1], device_id=to, device_id_type=MESH)
            cp.start()
            copies.append(cp)
        for rel in range(1, 8):
            _, p = _peer(rel)
            pltpu.make_async_remote_copy(src_ref=send_buf.at[p], dst_ref=recv_buf.at[p], send_sem=send2.at[rel - 1],
                                         recv_sem=recv2.at[rel - 1], device_id=(x, y, c), device_id_type=MESH).wait_recv()
        for cp in copies:
            cp.wait_send()
        rowc = lax.broadcasted_iota(jnp.int32, (8, cols), 0)
        out = jnp.zeros((8, cols), F32)
        for k in range(8):
            out = jnp.where(rowc == k, recv_buf[k], out)
        mod_ref[...] = out

    return _launch(
        body, "ada_forward",
        out_shape=[jax.ShapeDtypeStruct((8, cols), F32), jax.ShapeDtypeStruct((8, D_MODEL), F32)],
        in_specs=[pl.BlockSpec(memory_space=pltpu.VMEM)] * 3,
        out_specs=[pl.BlockSpec(memory_space=pltpu.VMEM)] * 2,
        operands=(c_row, w_ada, b_ada8),
        scratch=[pltpu.VMEM((8, 8, D_MODEL), F32), pltpu.VMEM((8, 8, cols), F32), pltpu.VMEM((8, 8, cols), F32)]
        + [pltpu.SemaphoreType.DMA((7,))] * 4,
        carry=carry)


def _allreduce_small(pack):
    rows = pack.shape[1]

    def body(pack_ref, total_ref, land_ref, send1, recv1, send2, recv2):
        x, y, c = _me()
        me = 4 * x + 2 * y + c
        land_ref[me] = pack_ref[me]
        copies = []
        for rel in range(1, 8):
            to, p = _peer(rel)
            cp = pltpu.make_async_remote_copy(src_ref=pack_ref.at[p], dst_ref=land_ref.at[me], send_sem=send1.at[rel - 1],
                                              recv_sem=recv1.at[rel - 1], device_id=to, device_id_type=MESH)
            cp.start()
            copies.append(cp)
        for rel in range(1, 8):
            _, p = _peer(rel)
            pltpu.make_async_remote_copy(src_ref=pack_ref.at[p], dst_ref=land_ref.at[p], send_sem=send1.at[rel - 1],
                                         recv_sem=recv1.at[rel - 1], device_id=(x, y, c), device_id_type=MESH).wait_recv()
        for cp in copies:
            cp.wait_send()
        acc = land_ref[0]
        for b in range(1, 8):
            acc = acc + land_ref[b]
        total_ref[me] = acc
        copies = []
        for rel in range(1, 8):
            to, _ = _peer(rel)
            cp = pltpu.make_async_remote_copy(src_ref=total_ref.at[me], dst_ref=total_ref.at[me], send_sem=send2.at[rel - 1],
                                              recv_sem=recv2.at[rel - 1], device_id=to, device_id_type=MESH)
            cp.start()
            copies.append(cp)
        for rel in range(1, 8):
            _, p = _peer(rel)
            pltpu.make_async_remote_copy(src_ref=total_ref.at[p], dst_ref=total_ref.at[p], send_sem=send2.at[rel - 1],
                                         recv_sem=recv2.at[rel - 1], device_id=(x, y, c), device_id_type=MESH).wait_recv()
        for cp in copies:
            cp.wait_send()

    return pl.pallas_call(
        body, name="allreduce_small",
        out_shape=[jax.ShapeDtypeStruct((8, rows, 128), F32), jax.ShapeDtypeStruct((8, rows, 128), F32)],
        in_specs=[pl.BlockSpec(memory_space=pltpu.VMEM)],
        out_specs=[pl.BlockSpec(memory_space=pltpu.VMEM)] * 2,
        scratch_shapes=[pltpu.SemaphoreType.DMA((7,))] * 4,
        compiler_params=pltpu.CompilerParams(vmem_limit_bytes=VMEM_LIMIT),
    )(pack)


def _modulated(x, prm_ref, sub, g_row):
    shift, scale = _row(prm_ref, 3 * sub), _row(prm_ref, 3 * sub + 1)
    g = _row(prm_ref, g_row)
    r = _rms_scale(x)
    n0 = x * r
    return (n0 * g) * (1.0 + scale) + shift, r, n0


def _ffn_forward(x, prm, win, wout, sub, g_row, name, carry=None):
    T = x.shape[0]
    tm = min(T, TM_FFN)

    def body(x_ref, prm_ref, win_ref, wout_ref, xo_ref, f_ref, ab_ref):
        xv = x_ref[...]
        h, _, _ = _modulated(xv, prm_ref, sub, g_row)
        hb = h.astype(MXU_DTYPE)
        acc = None
        for j in range(4):
            a = _mm(hb, win_ref[0, j])
            b = _mm(hb, win_ref[1, j])
            ab_ref[0, j] = a.astype(SAVE_DTYPE)
            ab_ref[1, j] = b.astype(SAVE_DTYPE)
            s = (a * _sigmoid(a)) * b
            t = _mm(s, wout_ref[j])
            acc = t if acc is None else acc + t
        f_ref[...] = acc.astype(SAVE_DTYPE)
        xo_ref[...] = xv + (0.5 * _row(prm_ref, 3 * sub + 2)) * acc

    tok = pl.BlockSpec((tm, D_MODEL), lambda i: (i, 0))
    return _launch(
        body, name, grid=(T // tm,), semantics=("arbitrary",),
        out_shape=[jax.ShapeDtypeStruct((T, D_MODEL), F32), jax.ShapeDtypeStruct((T, D_MODEL), SAVE_DTYPE),
                   jax.ShapeDtypeStruct((2, 4, T, FF_SHARD), SAVE_DTYPE)],
        in_specs=[tok, _resident(prm), _resident(win), _resident(wout)],
        out_specs=[tok, tok, pl.BlockSpec((2, 4, tm, FF_SHARD), lambda i: (0, 0, i, 0))],
        operands=(x, prm, win, wout), carry=carry, steps=_grid_steps(T // tm))


def _ffn_hidden(x, prm, win, sub, g_row, name, carry=None):
    T = x.shape[0]
    tm = min(T, TM_FFN)

    def body(x_ref, prm_ref, win_ref, ab_ref, s_ref):
        h, _, _ = _modulated(x_ref[...], prm_ref, sub, g_row)
        hb = h.astype(MXU_DTYPE)
        for j in range(4):
            a = _mm(hb, win_ref[0, j])
            b = _mm(hb, win_ref[1, j])
            ab_ref[0, j] = a.astype(SAVE_DTYPE)
            ab_ref[1, j] = b.astype(SAVE_DTYPE)
            s_ref[j] = ((a * _sigmoid(a)) * b).astype(MXU_DTYPE)

    return _launch(
        body, name, grid=(T // tm,), semantics=("arbitrary",),
        out_shape=[jax.ShapeDtypeStruct((2, 4, T, FF_SHARD), SAVE_DTYPE), jax.ShapeDtypeStruct((4, T, FF_SHARD), MXU_DTYPE)],
        in_specs=[pl.BlockSpec((tm, D_MODEL), lambda i: (i, 0)), _resident(prm), _resident(win)],
        out_specs=[pl.BlockSpec((2, 4, tm, FF_SHARD), lambda i: (0, 0, i, 0)),
                   pl.BlockSpec((4, tm, FF_SHARD), lambda i: (0, i, 0))],
        operands=(x, prm, win), carry=carry, steps=_grid_steps(T // tm))


def _ffn_out(x, s, prm, wout, sub, name, carry=None):
    T = x.shape[0]
    tm = min(T, TM_FFN)

    def body(x_ref, s_ref, prm_ref, wout_ref, xo_ref, f_ref):
        acc = None
        for j in range(4):
            t = _mm(s_ref[j], wout_ref[j])
            acc = t if acc is None else acc + t
        f_ref[...] = acc.astype(SAVE_DTYPE)
        xo_ref[...] = x_ref[...] + (0.5 * _row(prm_ref, 3 * sub + 2)) * acc

    tok = pl.BlockSpec((tm, D_MODEL), lambda i: (i, 0))
    return _launch(
        body, name, grid=(T // tm,), semantics=("arbitrary",),
        out_shape=[jax.ShapeDtypeStruct((T, D_MODEL), F32), jax.ShapeDtypeStruct((T, D_MODEL), SAVE_DTYPE)],
        in_specs=[tok, pl.BlockSpec((4, tm, FF_SHARD), lambda i: (0, i, 0)), _resident(prm), _resident(wout)],
        out_specs=[tok, tok], operands=(x, s, prm, wout), carry=carry, steps=_grid_steps(T // tm))


def _ffn_backward(x, d, ab, prm, win, wout, sub, g_row, name, carry=None):
    T = x.shape[0]
    tm = min(T, TM_FFN)
    nt = T // tm
    chunk = min(tm, FFN_BWD_CHUNK)

    def body(x_ref, d_ref, ab_ref, prm_ref, win_ref, wout_ref, dh_ref, dwin_ref, dwout_ref, acc_in, acc_out):
        i = pl.program_id(1)

        @pl.when(i == 0)
        def _():
            acc_in[...] = jnp.zeros_like(acc_in)
            acc_out[...] = jnp.zeros_like(acc_out)

        wa, wb, wo = win_ref[0, 0], win_ref[1, 0], wout_ref[0]
        half_gate = 0.5 * _row(prm_ref, 3 * sub + 2)
        das, dbs, ss, hbs, dfss = [], [], [], [], []
        for ck in range(tm // chunk):
            rows = slice(ck * chunk, (ck + 1) * chunk)
            h, _, _ = _modulated(x_ref[rows, :], prm_ref, sub, g_row)
            hbs.append(h.astype(MXU_DTYPE))
            a = ab_ref[0, 0, rows, :].astype(F32)
            b = ab_ref[1, 0, rows, :].astype(F32)
            sg = _sigmoid(a)
            si = a * sg
            dfs = (half_gate * d_ref[rows, :]).astype(MXU_DTYPE)
            ds = _mm_nt(dfs, wo)
            da = (ds * b * (sg * (1.0 + a * (1.0 - sg)))).astype(MXU_DTYPE)
            db = (ds * si).astype(MXU_DTYPE)
            dh_ref[0, rows, :] = (_mm_nt(da, wa) + _mm_nt(db, wb)).astype(SAVE_DTYPE)
            das.append(da)
            dbs.append(db)
            ss.append((si * b).astype(MXU_DTYPE))
            dfss.append(dfs)
        cat = (lambda v: v[0]) if len(das) == 1 else (lambda v: jnp.concatenate(v, axis=0))
        hb = cat(hbs)
        acc_out[...] += _mm_tn(cat(ss), cat(dfss))
        acc_in[0] += _mm_tn(hb, cat(das))
        acc_in[1] += _mm_tn(hb, cat(dbs))

        @pl.when(i == nt - 1)
        def _():
            dwin_ref[0, 0] = acc_in[0].astype(WIRE_DTYPE)
            dwin_ref[1, 0] = acc_in[1].astype(WIRE_DTYPE)
            dwout_ref[0] = acc_out[...].astype(WIRE_DTYPE)

    def steps():
        j, i = pl.program_id(0), pl.program_id(1)
        return (j == 0) & (i == 0), (j == 2) & (i == 0), (j == 3) & (i == nt - 1)

    tok = pl.BlockSpec((tm, D_MODEL), lambda j, i: (i, 0))
    return _launch(
        body, name, grid=(4, nt), semantics=("arbitrary", "arbitrary"),
        out_shape=[jax.ShapeDtypeStruct((4, T, D_MODEL), SAVE_DTYPE),
                   jax.ShapeDtypeStruct(win.shape, WIRE_DTYPE), jax.ShapeDtypeStruct(wout.shape, WIRE_DTYPE)],
        in_specs=[tok, tok, pl.BlockSpec((2, 1, tm, FF_SHARD), lambda j, i: (0, j, i, 0)), _resident(prm),
                  pl.BlockSpec((2, 1, D_MODEL, FF_SHARD), lambda j, i: (0, j, 0, 0)),
                  pl.BlockSpec((1, FF_SHARD, D_MODEL), lambda j, i: (j, 0, 0))],
        out_specs=[pl.BlockSpec((1, tm, D_MODEL), lambda j, i: (j, i, 0)),
                   pl.BlockSpec((2, 1, D_MODEL, FF_SHARD), lambda j, i: (0, j, 0, 0)),
                   pl.BlockSpec((1, FF_SHARD, D_MODEL), lambda j, i: (j, 0, 0))],
        operands=(x, d, ab, prm, win, wout),
        scratch=[pltpu.VMEM((2, D_MODEL, FF_SHARD), F32), pltpu.VMEM((FF_SHARD, D_MODEL), F32)],
        carry=carry, steps=steps)


def _norm_backward(parts, x, d, f, prm, sub, g_row, gate_coef, name, carry=None):
    T = x.shape[0]
    tm = min(T, TM_EW)
    P = parts.shape[0]

    def body(p_ref, x_ref, d_ref, f_ref, prm_ref, dx_ref, sums_ref):
        i = pl.program_id(0)
        dh = p_ref[0].astype(F32)
        for k in range(1, P):
            dh = dh + p_ref[k].astype(F32)
        xv, dv = x_ref[...], d_ref[...]
        scale, g = _row(prm_ref, 3 * sub + 1), _row(prm_ref, g_row)
        r = _rms_scale(xv)
        n0 = xv * r
        dn = dh * (1.0 + scale)
        dn0 = dn * g
        dx_ref[...] = dv + r * (dn0 - n0 * jnp.mean(dn0 * n0, axis=-1, keepdims=True))
        upd = jnp.concatenate([_colsum(dn * n0), _colsum(dh), _colsum(dh * (n0 * g)),
                               gate_coef * _colsum(dv * f_ref[...].astype(F32)), jnp.zeros((4, D_MODEL), F32)], axis=0)

        @pl.when(i == 0)
        def _():
            sums_ref[...] = upd

        @pl.when(i > 0)
        def _():
            sums_ref[...] += upd

    tok = pl.BlockSpec((tm, D_MODEL), lambda i: (i, 0))
    return _launch(
        body, name, grid=(T // tm,), semantics=("arbitrary",),
        out_shape=[jax.ShapeDtypeStruct((T, D_MODEL), F32), jax.ShapeDtypeStruct((8, D_MODEL), F32)],
        in_specs=[pl.BlockSpec((P, tm, D_MODEL), lambda i: (0, i, 0)), tok, tok, tok, _resident(prm)],
        out_specs=[tok, pl.BlockSpec((8, D_MODEL), lambda i: (0, 0))],
        operands=(parts, x, d, f, prm), carry=carry, steps=_grid_steps(T // tm))


def _final_loss(x, target, prm):
    T = x.shape[0]
    tm = min(T, TM_EW)

    def body(x_ref, t_ref, prm_ref, dx_ref, sums_ref):
        i = pl.program_id(0)
        xv = x_ref[...]
        g = _row(prm_ref, ROW_G_FINAL)
        r = _rms_scale(xv)
        n0 = xv * r
        err = n0 * g - t_ref[...]
        dy = err / float(D_MODEL)
        dn0 = dy * g
        dx_ref[...] = r * (dn0 - n0 * jnp.mean(dn0 * n0, axis=-1, keepdims=True))
        loss = 0.5 * jnp.sum(jnp.mean(err * err, axis=-1, keepdims=True), axis=0, keepdims=True)
        upd = jnp.concatenate([_colsum(dy * n0), jnp.broadcast_to(loss, (1, D_MODEL)), jnp.zeros((6, D_MODEL), F32)], axis=0)

        @pl.when(i == 0)
        def _():
            sums_ref[...] = upd

        @pl.when(i > 0)
        def _():
            sums_ref[...] += upd

    tok = pl.BlockSpec((tm, D_MODEL), lambda i: (i, 0))
    return pl.pallas_call(
        body, name="final_loss", grid=(T // tm,),
        out_shape=[jax.ShapeDtypeStruct((T, D_MODEL), F32), jax.ShapeDtypeStruct((8, D_MODEL), F32)],
        in_specs=[tok, tok, pl.BlockSpec(prm.shape, lambda i: (0, 0))],
        out_specs=[tok, pl.BlockSpec((8, D_MODEL), lambda i: (0, 0))],
        compiler_params=_params("arbitrary"),
    )(x, target, prm)


def _ssm_discretise(lam_re_log, lam_im, log_dt):
    lr = -jnp.exp(lam_re_log)
    dt = jnp.exp(log_dt)
    mag = jnp.exp(lr * dt)
    ang = lam_im * dt
    ab_re = mag * jnp.cos(ang)
    ab_im = mag * jnp.sin(ang)
    num_re = ab_re - 1.0
    num_im = ab_im
    den = lr * lr + lam_im * lam_im
    f_re = (num_re * lr + num_im * lam_im) / den
    f_im = (num_im * lr - num_re * lam_im) / den
    return ab_re, ab_im, f_re, f_im


def _ssm_params_forward(lam_re_log, lam_im, log_dt):
    def body(a_ref, b_ref, c_ref, o0, o1, o2, o3):
        outs = _ssm_discretise(a_ref[...], b_ref[...], c_ref[...])
        for o, v in zip((o0, o1, o2, o3), outs):
            o[...] = v

    return pl.pallas_call(body, name="ssm_params_forward",
                          out_shape=[jax.ShapeDtypeStruct(lam_im.shape, F32)] * 4)(lam_re_log, lam_im, log_dt)


def _ssm_params_backward(lam_re_log, lam_im, log_dt, cot):
    def body(a_ref, b_ref, c_ref, g0, g1, g2, g3, o0, o1, o2):
        _, vjp = jax.vjp(_ssm_discretise, a_ref[...], b_ref[...], c_ref[...])
        d0, d1, d2 = vjp((g0[...], g1[...], g2[...], g3[...]))
        o0[...] = d0
        o1[...] = d1
        o2[...] = d2

    return pl.pallas_call(
        body, name="ssm_params_backward",
        out_shape=[jax.ShapeDtypeStruct(lam_im.shape, F32), jax.ShapeDtypeStruct(lam_im.shape, F32),
                   jax.ShapeDtypeStruct(log_dt.shape, F32)])(lam_re_log, lam_im, log_dt, *cot)


def _ssm_dense_forward(srow, b_dense, c_dense):
    def body(srow_ref, bd_ref, cd_ref, bb_ref, ct_ref):
        for j in range(SSM_BLOCKS):
            lanes = slice(j * SSM_BLOCK_STATE, (j + 1) * SSM_BLOCK_STATE)
            f_re, f_im = srow_ref[2:3, lanes], srow_ref[3:4, lanes]
            bb_ref[0, j] = (f_re * bd_ref[0, j] - f_im * bd_ref[1, j]).astype(MXU_DTYPE)
            bb_ref[1, j] = (f_re * bd_ref[1, j] + f_im * bd_ref[0, j]).astype(MXU_DTYPE)
            ct_ref[0, j] = cd_ref[0, j].astype(MXU_DTYPE)
            ct_ref[1, j] = cd_ref[1, j].astype(MXU_DTYPE)

    return pl.pallas_call(body, name="ssm_dense_forward",
                          out_shape=[jax.ShapeDtypeStruct(b_dense.shape, MXU_DTYPE),
                                     jax.ShapeDtypeStruct(c_dense.shape, MXU_DTYPE)],
                          compiler_params=pltpu.CompilerParams(vmem_limit_bytes=VMEM_LIMIT))(srow, b_dense, c_dense)


def _cmul(p, q):
    return p[0] * q[0] - p[1] * q[1], p[0] * q[1] + p[1] * q[0]


def _scan_coefficients(ar, ai, reverse):
    n = ar.shape[1]
    p = {1: (ar, ai)}
    p[2] = _cmul(p[1], p[1])
    p[3] = _cmul(p[2], p[1])
    p[4] = _cmul(p[2], p[2])
    p[5] = _cmul(p[4], p[1])
    p[6] = _cmul(p[4], p[2])
    p[7] = _cmul(p[4], p[3])
    p[8] = _cmul(p[4], p[4])
    rowi = lax.broadcasted_iota(jnp.int32, (SCAN_ROWS, n), 0)
    tiles = []
    for dstep in (1, 2, 4):
        keep = (rowi < SCAN_ROWS - dstep) if reverse else (rowi >= dstep)
        for part in p[dstep]:
            tiles.append(jnp.where(keep, jnp.broadcast_to(part, (SCAN_ROWS, n)), 0.0))
    for comp in (0, 1):
        t = jnp.zeros((SCAN_ROWS, n), F32)
        for rr in range(SCAN_ROWS):
            power = SCAN_ROWS - rr if reverse else rr + 1
            t = jnp.where(rowi == rr, jnp.broadcast_to(p[power][comp], (SCAN_ROWS, n)), t)
        tiles.append(t)
    return tiles


def _load_stack(stack_hbm, dst, sems, base):
    cols = stack_hbm.shape[2]
    cps = [pltpu.make_async_copy(stack_hbm.at[k], dst.at[:, pl.ds(k * cols, cols)], sems.at[base + k])
           for k in range(NDEV)]
    for cp in cps:
        cp.start()
    return cps


def _window_lanes():
    lane = lax.broadcasted_iota(jnp.int32, (1, POOL_WIDTH), 1)
    return jnp.where(lane < 128, 2.0, jnp.where(lane < 256, 4.0, jnp.where(lane < 384, 8.0, 16.0)))


def _gelu(y):
    return 0.5 * y * (1.0 + lax.erf(y * 0.7071067811865476))


def _gelu_grad(y):
    return 0.5 * (1.0 + lax.erf(y * 0.7071067811865476)) + y * jnp.exp(-0.5 * y * y) * 0.3989422804014327


def _mixer_forward(x, prm, w_in_s, w_pu_s, w_glu_s, w_su_s, w_out, pool_w, mvec, srow, bb, ct, carry=None):
    T = x.shape[0]
    tm = min(T, TM_MIX)
    nt = T // tm
    n_tiles = tm // SCAN_ROWS

    def body(x_ref, prm_ref, w_in_h, w_pu_h, w_glu_h, w_su_h, w_out_h, pw_ref, mv_ref, srow_ref, bb, ct,
             x2_ref, mo_ref, z_ref, sre_ref, sim_ref, zp_ref, q_ref, yp_ref, yss_ref, vg_ref, ys_ref,
             w_in, w_pu, w_glu, w_su, w_o, coef, carry, hist, bu, sems):
        i = pl.program_id(0)

        @pl.when(i == 0)
        def _():
            cps = (_load_stack(w_in_h, w_in, sems, 0) + _load_stack(w_pu_h, w_pu, sems, 8)
                   + _load_stack(w_glu_h, w_glu, sems, 16) + _load_stack(w_su_h, w_su, sems, 24))
            cps.append(pltpu.make_async_copy(w_out_h, w_o, sems.at[32]))
            cps[-1].start()
            for j in range(SSM_BLOCKS):
                lanes = slice(j * SSM_BLOCK_STATE, (j + 1) * SSM_BLOCK_STATE)
                for k, t in enumerate(_scan_coefficients(srow_ref[0:1, lanes], srow_ref[1:2, lanes], False)):
                    coef[j, k] = t
            carry[...] = jnp.zeros_like(carry)
            hist[...] = jnp.zeros_like(hist)
            for cp in cps:
                cp.wait()

        xv = x_ref[...]
        h, _, _ = _modulated(xv, prm_ref, 1, ROW_G_MIX)
        z = _mm(h, w_in[...])
        z_ref[...] = z.astype(SAVE_DTYPE)
        u_pool, u_ssm = z[:, 0:512], z[:, 512:1024]
        gl_pool, gl_ssm = z[:, 1024:2048], z[:, 2048:3072]

        ext = jnp.concatenate([hist[...], u_pool], axis=0)
        w2 = ext + pltpu.roll(ext, 1, 0)
        w4 = w2[:, 128:] + pltpu.roll(w2[:, 128:], 2, 0)
        w8 = w4[:, 128:] + pltpu.roll(w4[:, 128:], 4, 0)
        w16 = w8[:, 128:] + pltpu.roll(w8[:, 128:], 8, 0)
        wsum = jnp.concatenate([w2[POOL_HALO:, :128], w4[POOL_HALO:, :128], w8[POOL_HALO:, :128], w16[POOL_HALO:]], axis=1)
        hist[...] = u_pool[tm - POOL_HALO:, :]
        t1 = (lax.broadcasted_iota(jnp.int32, (tm, 1), 0) + (i * tm + 1)).astype(F32)
        zp = wsum / jnp.minimum(t1, _window_lanes()) - u_pool
        zp_ref[...] = zp.astype(SAVE_DTYPE)
        q = jnp.concatenate([_mm(zp[:, k * 128:(k + 1) * 128], pw_ref[k]) for k in range(4)], axis=1)
        q = q + mv_ref[ROW_POOL_B:ROW_POOL_B + 1, 0:512]
        q_ref[...] = q.astype(SAVE_DTYPE)
        y_pool = _mm(q * mv_ref[ROW_POOL_SCALE:ROW_POOL_SCALE + 1, 0:512], w_pu[...])
        yp_ref[...] = y_pool.astype(SAVE_DTYPE)

        y_blocks = []
        for j in range(SSM_BLOCKS):
            lanes = pl.ds(j * SSM_BLOCK_STATE, SSM_BLOCK_STATE)
            ub = u_ssm[:, j * 128:(j + 1) * 128].astype(MXU_DTYPE)
            bu[0] = _mm(ub, bb[0, j])
            bu[1] = _mm(ub, bb[1, j])
            a1r, a1i, a2r, a2i, a4r, a4i, pr, pi = [coef[j, k] for k in range(8)]

            def step(tt, c, lanes=lanes, a1r=a1r, a1i=a1i, a2r=a2r, a2i=a2i, a4r=a4r, a4i=a4i, pr=pr, pi=pi):
                cr, ci = c
                rows = pl.ds(pl.multiple_of(tt * SCAN_ROWS, SCAN_ROWS), SCAN_ROWS)
                xr, xi = bu[0, rows, :], bu[1, rows, :]
                for dstep, kr, ki in ((1, a1r, a1i), (2, a2r, a2i), (4, a4r, a4i)):
                    sr, si = pltpu.roll(xr, dstep, 0), pltpu.roll(xi, dstep, 0)
                    xr, xi = xr + kr * sr - ki * si, xi + kr * si + ki * sr
                xr, xi = xr + pr * cr - pi * ci, xi + pr * ci + pi * cr
                sre_ref[rows, lanes] = xr
                sim_ref[rows, lanes] = xi
                return (jnp.broadcast_to(xr[SCAN_ROWS - 1:SCAN_ROWS, :], xr.shape),
                        jnp.broadcast_to(xi[SCAN_ROWS - 1:SCAN_ROWS, :], xi.shape))

            cr, ci = lax.fori_loop(0, n_tiles, step, (carry[j, 0], carry[j, 1]))
            carry[j, 0] = cr
            carry[j, 1] = ci
            y_blocks.append(_mm(sre_ref[:, lanes], ct[0, j]) - _mm(sim_ref[:, lanes], ct[1, j]))
        yss = jnp.concatenate(y_blocks, axis=1) + mv_ref[ROW_SSM_D:ROW_SSM_D + 1, 0:512] * u_ssm
        yss_ref[...] = yss.astype(SAVE_DTYPE)
        vg = _mm(_gelu(yss), w_glu[...]) + mv_ref[ROW_B_GLU:ROW_B_GLU + 1, :]
        vg_ref[...] = vg.astype(SAVE_DTYPE)
        y_ssm = _mm(vg[:, 0:512] * _sigmoid(vg[:, 512:1024]), w_su[...])
        ys_ref[...] = y_ssm.astype(SAVE_DTYPE)

        merged = _sigmoid(gl_pool) * y_pool + _sigmoid(gl_ssm) * y_ssm
        mo = _mm(merged, w_o[...])
        mo_ref[...] = mo.astype(SAVE_DTYPE)
        x2_ref[...] = xv + _row(prm_ref, 5) * mo

    def tok(width):
        return pl.BlockSpec((tm, width), lambda i: (i, 0))

    hbm = _HBM
    widths = (D_MODEL, D_MODEL, IN_WIDTH, N_STATE, N_STATE, 512, 512, D_MODEL, 512, D_MODEL, D_MODEL)
    dtypes = (F32, SAVE_DTYPE, SAVE_DTYPE, F32, F32) + (SAVE_DTYPE,) * 6
    return _launch(
        body, "mixer_forward", grid=(nt,), semantics=("arbitrary",), carry=carry, steps=_grid_steps(nt),
        out_shape=[jax.ShapeDtypeStruct((T, w), dt) for w, dt in zip(widths, dtypes)],
        in_specs=[tok(D_MODEL), _resident(prm), hbm, hbm, hbm, hbm, hbm, _resident(pool_w), _resident(mvec),
                  _resident(srow), _resident(bb), _resident(ct)],
        out_specs=[tok(w) for w in widths],
        operands=(x, prm, w_in_s, w_pu_s, w_glu_s, w_su_s, w_out, pool_w, mvec, srow, bb, ct),
        scratch=[
            pltpu.VMEM((D_MODEL, IN_WIDTH), MXU_DTYPE), pltpu.VMEM((512, D_MODEL), MXU_DTYPE),
            pltpu.VMEM((512, D_MODEL), MXU_DTYPE), pltpu.VMEM((512, D_MODEL), MXU_DTYPE),
            pltpu.VMEM((D_MODEL, D_MODEL), MXU_DTYPE),
            pltpu.VMEM((SSM_BLOCKS, 8, SCAN_ROWS, SSM_BLOCK_STATE), F32),
            pltpu.VMEM((SSM_BLOCKS, 2, SCAN_ROWS, SSM_BLOCK_STATE), F32),
            pltpu.VMEM((POOL_HALO, POOL_WIDTH), F32),
            pltpu.VMEM((2, tm, SSM_BLOCK_STATE), F32),
            pltpu.SemaphoreType.DMA((33,)),
        ])


def _mixer_backward(d2, prm, saved, w_pu_s, w_glu_s, w_su_s, w_out, pool_w, mvec, srow, bb, ct, carry=None):
    z, s_re, s_im, zp, q, y_pool, yss, vg, y_ssm = saved
    T = d2.shape[0]
    tm = min(T, TM_MIX_BWD)
    nt = T // tm
    n_tiles = tm // SCAN_ROWS

    def body(d_ref, prm_ref, z_ref, sre_ref, sim_ref, zp_ref, q_ref, yp_ref, yss_ref, vg_ref, ys_ref,
             w_pu_h, w_glu_h, w_su_h, w_out_h, pw_ref, mv_ref, srow_ref, bb, ct,
             dz_ref, dwo_h, dwpu_h, dwglu_h, dwsu_h, dpw_h, dbb_h, dct_h, vsum_h, da_h,
             w_pu, w_glu, w_su, w_o, pwb, coef, carry, hist, dre, lam,
             a_wo, a_wpu, a_wglu, a_wsu, a_pw, a_bb, a_ct, a_vs, a_da, st_wo, st_up, sems):
        i = pl.program_id(0)
        tile = nt - 1 - i

        @pl.when(i == 0)
        def _():
            cps = (_load_stack(w_pu_h, w_pu, sems, 0) + _load_stack(w_glu_h, w_glu, sems, 8)
                   + _load_stack(w_su_h, w_su, sems, 16))
            cps.append(pltpu.make_async_copy(w_out_h, w_o, sems.at[24]))
            cps[-1].start()
            pwb[...] = pw_ref[...].astype(MXU_DTYPE)
            for j in range(SSM_BLOCKS):
                lanes = slice(j * SSM_BLOCK_STATE, (j + 1) * SSM_BLOCK_STATE)
                for k, t in enumerate(_scan_coefficients(srow_ref[0:1, lanes], srow_ref[1:2, lanes], True)):
                    coef[j, k] = t
            for acc in (carry, hist, a_wo, a_wpu, a_wglu, a_wsu, a_pw, a_bb, a_ct, a_vs, a_da):
                acc[...] = jnp.zeros_like(acc)
            for cp in cps:
                cp.wait()

        dv = d_ref[...]
        zt = z_ref[...].astype(F32)
        u_ssm, gl_pool, gl_ssm = zt[:, 512:1024], zt[:, 1024:2048], zt[:, 2048:3072]
        y_p, y_s = yp_ref[...].astype(F32), ys_ref[...].astype(F32)
        sgp, sgs = _sigmoid(gl_pool), _sigmoid(gl_ssm)
        dmo = (_row(prm_ref, 5) * dv).astype(MXU_DTYPE)
        a_wo[...] += _mm_tn(sgp * y_p + sgs * y_s, dmo)
        dmerged = _mm_nt(dmo, w_o[...])
        dy_pool = dmerged * sgp
        dgl_pool = dmerged * y_p * (sgp * (1.0 - sgp))
        dy_ssm = dmerged * sgs
        dgl_ssm = dmerged * y_s * (sgs * (1.0 - sgs))

        scale = mv_ref[ROW_POOL_SCALE:ROW_POOL_SCALE + 1, 0:512]
        qv, zpv = q_ref[...].astype(F32), zp_ref[...]
        a_wpu[...] += _mm_tn(qv * scale, dy_pool)
        dp = _mm_nt(dy_pool, w_pu[...])
        dq = dp * scale
        a_vs[0:1, 0:512] += _colsum(dp * qv)
        a_vs[1:2, 0:512] += _colsum(dq)
        dzp_blocks = []
        for k in range(4):
            lanes = slice(k * 128, (k + 1) * 128)
            dzp_blocks.append(_mm_nt(dq[:, lanes], pwb[k]))
            a_pw[k] += _mm_tn(zpv[:, lanes], dq[:, lanes])
        dzp = jnp.concatenate(dzp_blocks, axis=1)
        t1 = (lax.broadcasted_iota(jnp.int32, (tm, 1), 0) + (tile * tm + 1)).astype(F32)
        gs = dzp / jnp.minimum(t1, _window_lanes())
        n_ext = tm + POOL_HALO
        ext = jnp.concatenate([gs, hist[...]], axis=0)
        v2 = ext + pltpu.roll(ext, n_ext - 1, 0)
        v4 = v2[:, 128:] + pltpu.roll(v2[:, 128:], n_ext - 2, 0)
        v8 = v4[:, 128:] + pltpu.roll(v4[:, 128:], n_ext - 4, 0)
        v16 = v8[:, 128:] + pltpu.roll(v8[:, 128:], n_ext - 8, 0)
        msum = jnp.concatenate([v2[:tm, :128], v4[:tm, :128], v8[:tm, :128], v16[:tm]], axis=1)
        hist[...] = gs[0:POOL_HALO, :]
        du_pool = msum - dzp

        vgv = vg_ref[...].astype(F32)
        val, gate = vgv[:, 0:512], vgv[:, 512:1024]
        sgg = _sigmoid(gate)
        a_wsu[...] += _mm_tn(val * sgg, dy_ssm)
        do = _mm_nt(dy_ssm, w_su[...])
        dvg = jnp.concatenate([do * sgg, do * val * (sgg * (1.0 - sgg))], axis=1)
        a_vs[3:4, :] += _colsum(dvg)
        yv = yss_ref[...].astype(F32)
        a_wglu[...] += _mm_tn(_gelu(yv), dvg)
        dyss = _mm_nt(dvg, w_glu[...]) * _gelu_grad(yv)
        a_vs[2:3, 0:512] += _colsum(dyss * u_ssm)
        du_blocks = []
        for j in range(SSM_BLOCKS):
            lanes = pl.ds(j * SSM_BLOCK_STATE, SSM_BLOCK_STATE)
            in_lanes = slice(j * 128, (j + 1) * 128)
            dyb = dyss[:, in_lanes].astype(MXU_DTYPE)
            ub = u_ssm[:, in_lanes].astype(MXU_DTYPE)
            dre[0] = _mm_nt(dyb, ct[0, j])
            dre[1] = -_mm_nt(dyb, ct[1, j])
            a_ct[0, j] += _mm_tn(sre_ref[:, lanes], dyb)
            a_ct[1, j] -= _mm_tn(sim_ref[:, lanes], dyb)
            a1r, a1i, a2r, a2i, a4r, a4i, pr, pi = [coef[j, k] for k in range(8)]
            rowi = lax.broadcasted_iota(jnp.int32, (SCAN_ROWS, SSM_BLOCK_STATE), 0)

            def step(tt, c, lanes=lanes, a1r=a1r, a1i=a1i, a2r=a2r, a2i=a2i, a4r=a4r, a4i=a4i, pr=pr, pi=pi, rowi=rowi):
                cr, ci, acc_r, acc_i = c
                rows = pl.ds(pl.multiple_of((n_tiles - 1 - tt) * SCAN_ROWS, SCAN_ROWS), SCAN_ROWS)
                xr, xi = dre[0, rows, :], dre[1, rows, :]
                for dstep, kr, ki in ((1, a1r, a1i), (2, a2r, a2i), (4, a4r, a4i)):
                    sr, si = pltpu.roll(xr, SCAN_ROWS - dstep, 0), pltpu.roll(xi, SCAN_ROWS - dstep, 0)
                    xr, xi = xr + kr * sr + ki * si, xi + kr * si - ki * sr
                xr, xi = xr + pr * cr + pi * ci, xi + pr * ci - pi * cr
                lam[0, rows, :] = xr
                lam[1, rows, :] = xi
                nr = jnp.where(rowi == SCAN_ROWS - 1, cr, pltpu.roll(xr, SCAN_ROWS - 1, 0))
                ni = jnp.where(rowi == SCAN_ROWS - 1, ci, pltpu.roll(xi, SCAN_ROWS - 1, 0))
                s_r, s_i = sre_ref[rows, lanes], sim_ref[rows, lanes]
                acc_r = acc_r + nr * s_r + ni * s_i
                acc_i = acc_i + ni * s_r - nr * s_i
                return (jnp.broadcast_to(xr[0:1, :], xr.shape), jnp.broadcast_to(xi[0:1, :], xi.shape), acc_r, acc_i)

            cr, ci, acc_r, acc_i = lax.fori_loop(0, n_tiles, step, (carry[j, 0], carry[j, 1], a_da[0, j], a_da[1, j]))
            carry[j, 0] = cr
            carry[j, 1] = ci
            a_da[0, j] = acc_r
            a_da[1, j] = acc_i
            lr_b, li_b = lam[0].astype(MXU_DTYPE), lam[1].astype(MXU_DTYPE)
            a_bb[0, j] += _mm_tn(ub, lr_b)
            a_bb[1, j] += _mm_tn(ub, li_b)
            du_blocks.append(_mm_nt(lr_b, bb[0, j]) + _mm_nt(li_b, bb[1, j]))
        du_ssm = jnp.concatenate(du_blocks, axis=1) + dyss * mv_ref[ROW_SSM_D:ROW_SSM_D + 1, 0:512]
        dz_ref[...] = jnp.concatenate([du_pool, du_ssm, dgl_pool, dgl_ssm], axis=1).astype(SAVE_DTYPE)

        @pl.when(i == nt - 1)
        def _():
            rows = D_MODEL // NDEV
            for k in range(NDEV):
                st_wo[k] = a_wo[k * rows:(k + 1) * rows, :].astype(WIRE_DTYPE)
                for a, acc in enumerate((a_wpu, a_wglu, a_wsu)):
                    st_up[a, k] = acc[:, k * 128:(k + 1) * 128].astype(WIRE_DTYPE)
            outs = ((st_wo, dwo_h), (st_up.at[0], dwpu_h), (st_up.at[1], dwglu_h), (st_up.at[2], dwsu_h),
                    (a_pw, dpw_h), (a_bb, dbb_h), (a_ct, dct_h), (a_vs, vsum_h), (a_da, da_h))
            cps = [pltpu.make_async_copy(src, dst, sems.at[k]) for k, (src, dst) in enumerate(outs)]
            for cp in cps:
                cp.start()
            for cp in cps:
                cp.wait()

    def tok(width):
        return pl.BlockSpec((tm, width), lambda i: (nt - 1 - i, 0))

    hbm = _HBM
    acc_shapes = [(D_MODEL, D_MODEL), (512, D_MODEL), (512, D_MODEL), (512, D_MODEL), (4, 128, 128),
                  (2, SSM_BLOCKS, 128, SSM_BLOCK_STATE), (2, SSM_BLOCKS, SSM_BLOCK_STATE, 128), (8, D_MODEL),
                  (2, SSM_BLOCKS, SCAN_ROWS, SSM_BLOCK_STATE)]
    stack_out = [jax.ShapeDtypeStruct((NDEV, D_MODEL // NDEV, D_MODEL), WIRE_DTYPE)] \
        + [jax.ShapeDtypeStruct((NDEV, 512, 128), WIRE_DTYPE)] * 3
    return _launch(
        body, "mixer_backward", grid=(nt,), semantics=("arbitrary",), carry=carry, steps=_grid_steps(nt),
        out_shape=[jax.ShapeDtypeStruct((T, IN_WIDTH), SAVE_DTYPE)] + stack_out
        + [jax.ShapeDtypeStruct(s, F32) for s in acc_shapes[4:]],
        in_specs=[tok(D_MODEL), _resident(prm), tok(IN_WIDTH), tok(N_STATE), tok(N_STATE), tok(512), tok(512),
                  tok(D_MODEL), tok(512), tok(D_MODEL), tok(D_MODEL), hbm, hbm, hbm, hbm, _resident(pool_w),
                  _resident(mvec), _resident(srow), _resident(bb), _resident(ct)],
        out_specs=[tok(IN_WIDTH)] + [hbm] * len(acc_shapes),
        operands=(d2, prm, z, s_re, s_im, zp, q, y_pool, yss, vg, y_ssm, w_pu_s, w_glu_s, w_su_s, w_out, pool_w, mvec,
                  srow, bb, ct),
        scratch=[
            pltpu.VMEM((512, D_MODEL), MXU_DTYPE), pltpu.VMEM((512, D_MODEL), MXU_DTYPE),
            pltpu.VMEM((512, D_MODEL), MXU_DTYPE), pltpu.VMEM((D_MODEL, D_MODEL), MXU_DTYPE),
            pltpu.VMEM((4, 128, 128), MXU_DTYPE),
            pltpu.VMEM((SSM_BLOCKS, 8, SCAN_ROWS, SSM_BLOCK_STATE), F32),
            pltpu.VMEM((SSM_BLOCKS, 2, SCAN_ROWS, SSM_BLOCK_STATE), F32),
            pltpu.VMEM((POOL_HALO, POOL_WIDTH), F32),
            pltpu.VMEM((2, tm, SSM_BLOCK_STATE), F32), pltpu.VMEM((2, tm, SSM_BLOCK_STATE), F32),
        ] + [pltpu.VMEM(s, F32) for s in acc_shapes]
        + [pltpu.VMEM((NDEV, D_MODEL // NDEV, D_MODEL), WIRE_DTYPE), pltpu.VMEM((3, NDEV, 512, 128), WIRE_DTYPE),
           pltpu.SemaphoreType.DMA((25,))])


def _mixer_in_backward(x, dz, prm, w_in_s):
    T = x.shape[0]
    tm = min(T, TM_MIX)
    nt = T // tm
    cols = IN_WIDTH // NDEV

    def body(x_ref, dz_ref, prm_ref, w_in_h, dh_ref, dw_ref, w_in, acc, sems):
        i = pl.program_id(0)

        @pl.when(i == 0)
        def _():
            cps = _load_stack(w_in_h, w_in, sems, 0)
            acc[...] = jnp.zeros_like(acc)
            for cp in cps:
                cp.wait()

        h, _, _ = _modulated(x_ref[...], prm_ref, 1, ROW_G_MIX)
        dzb = dz_ref[...].astype(MXU_DTYPE)
        dh_ref[0] = _mm_nt(dzb, w_in[...]).astype(SAVE_DTYPE)
        acc[...] += _mm_tn(h, dzb)

        @pl.when(i == nt - 1)
        def _():
            for k in range(NDEV):
                dw_ref[k] = acc[:, k * cols:(k + 1) * cols].astype(WIRE_DTYPE)

    return pl.pallas_call(
        body, name="mixer_in_backward", grid=(nt,),
        out_shape=[jax.ShapeDtypeStruct((1, T, D_MODEL), SAVE_DTYPE), jax.ShapeDtypeStruct((NDEV, D_MODEL, cols), WIRE_DTYPE)],
        in_specs=[pl.BlockSpec((tm, D_MODEL), lambda i: (i, 0)), pl.BlockSpec((tm, IN_WIDTH), lambda i: (i, 0)),
                  pl.BlockSpec(prm.shape, lambda i: (0, 0)), pl.BlockSpec(memory_space=pl.ANY)],
        out_specs=[pl.BlockSpec((1, tm, D_MODEL), lambda i: (0, i, 0)),
                   pl.BlockSpec((NDEV, D_MODEL, cols), lambda i: (0, 0, 0))],
        scratch_shapes=[pltpu.VMEM((D_MODEL, IN_WIDTH), MXU_DTYPE), pltpu.VMEM((D_MODEL, IN_WIDTH), F32),
                        pltpu.SemaphoreType.DMA((8,))],
        compiler_params=_params("arbitrary"),
    )(x, dz, prm, w_in_s)


def _ssm_dense_backward(dbb, da, srow, b_dense):
    def body(dbb_ref, da_ref, srow_ref, bd_ref, db_ref, df_ref):
        df_re, df_im = [], []
        da_re = [_colsum(da_ref[0, j]) for j in range(SSM_BLOCKS)]
        da_im = [_colsum(da_ref[1, j]) for j in range(SSM_BLOCKS)]
        for j in range(SSM_BLOCKS):
            lanes = slice(j * SSM_BLOCK_STATE, (j + 1) * SSM_BLOCK_STATE)
            f_re, f_im = srow_ref[2:3, lanes], srow_ref[3:4, lanes]
            g_re, g_im = dbb_ref[0, j], dbb_ref[1, j]
            b_re, b_im = bd_ref[0, j], bd_ref[1, j]
            db_ref[0, j] = f_re * g_re + f_im * g_im
            db_ref[1, j] = f_re * g_im - f_im * g_re
            df_re.append(_colsum(g_re * b_re + g_im * b_im))
            df_im.append(_colsum(g_im * b_re - g_re * b_im))
        df_ref[...] = jnp.concatenate([jnp.concatenate(df_re, axis=1), jnp.concatenate(df_im, axis=1),
                                       jnp.concatenate(da_re, axis=1), jnp.concatenate(da_im, axis=1),
                                       jnp.zeros((4, N_STATE), F32)], axis=0)

    return pl.pallas_call(body, name="ssm_dense_backward",
                          out_shape=[jax.ShapeDtypeStruct(b_dense.shape, F32), jax.ShapeDtypeStruct((8, N_STATE), F32)],
                          compiler_params=pltpu.CompilerParams(vmem_limit_bytes=VMEM_LIMIT))(dbb, da, srow, b_dense)


def _adamw_update(w, g, m, v):
    m = ADAM_B1 * m + (1.0 - ADAM_B1) * g
    v = ADAM_B2 * v + (1.0 - ADAM_B2) * (g * g)
    m_hat = m / (1.0 - ADAM_B1 ** ADAM_STEP)
    v_hat = v / (1.0 - ADAM_B2 ** ADAM_STEP)
    delta = -ADAM_LR * (m_hat / (jnp.sqrt(v_hat) + ADAM_EPS) + ADAM_WD * w)
    return delta, m, v


def _adam_rows(shape):
    rows, cols = shape
    tr = rows
    while tr * cols * 4 > (1 << 20) and tr % 16 == 0:
        tr //= 2
    return tr


def _adam_sharded(w, m, v, land, order, name):
    R, C = w.shape
    tr = _adam_rows((R, C))

    def body(w_ref, m_ref, v_ref, land_ref, order_ref, g_ref, d_ref, mo_ref, vo_ref):
        g = land_ref[0].astype(F32)
        for b in range(1, NDEV):
            g = g + land_ref[b].astype(F32)
        g_ref[...] = g
        d_ref[...], mo_ref[...], vo_ref[...] = _adamw_update(w_ref[...], g, m_ref[...], v_ref[...])

    blk = pl.BlockSpec((tr, C), lambda i: (i, 0))
    return pl.pallas_call(
        body, name=name, grid=(R // tr,),
        out_shape=[jax.ShapeDtypeStruct((R, C), F32)] * 4,
        in_specs=[blk, blk, blk, pl.BlockSpec((NDEV, tr, C), lambda i: (0, i, 0)), _HBM],
        out_specs=[blk] * 4,
        compiler_params=_params("arbitrary"),
    )(w, m, v, land, order)


def _adam_ada(w, m, v, sc_all, dmod_cols):
    R, C = w.shape
    tr = 256

    def body(w_ref, m_ref, v_ref, sc_ref, dm_ref, g_ref, d_ref, mo_ref, vo_ref):
        g = _mm_tn(sc_ref[...], dm_ref[...])
        g_ref[...] = g
        d_ref[...], mo_ref[...], vo_ref[...] = _adamw_update(w_ref[...], g, m_ref[...], v_ref[...])

    blk = pl.BlockSpec((tr, C), lambda i: (i, 0))
    return pl.pallas_call(
        body, name="adam_w_ada", grid=(R // tr,),
        out_shape=[jax.ShapeDtypeStruct((R, C), F32)] * 4,
        in_specs=[blk, blk, blk, pl.BlockSpec((8, tr), lambda i: (0, i)), pl.BlockSpec((8, C), lambda i: (0, 0))],
        out_specs=[blk] * 4,
        compiler_params=_params("arbitrary"),
    )(w, m, v, sc_all, dmod_cols)


def _adam_small(w, g, m, v):
    def body(w_ref, g_ref, m_ref, v_ref, d_ref, mo_ref, vo_ref):
        d_ref[...], mo_ref[...], vo_ref[...] = _adamw_update(w_ref[...], g_ref[...], m_ref[...], v_ref[...])

    return pl.pallas_call(body, name="adam_small", out_shape=[jax.ShapeDtypeStruct(w.shape, F32)] * 3,
                          compiler_params=pltpu.CompilerParams(vmem_limit_bytes=VMEM_LIMIT))(w, g, m, v)


def _block_diag_in(b):
    bt = jnp.transpose(b, (0, 2, 1)).reshape(SSM_BLOCKS, 8, SSM_GROUP, SSM_STATE)
    eye = jnp.eye(8, dtype=bool)[None, :, None, :, None]
    return jnp.where(eye, bt[:, :, :, None, :], 0.0).reshape(SSM_BLOCKS, 128, SSM_BLOCK_STATE)


def _block_diag_out(c):
    ct = jnp.transpose(c, (0, 2, 1)).reshape(SSM_BLOCKS, 8, SSM_STATE, SSM_GROUP)
    eye = jnp.eye(8, dtype=bool)[None, :, None, :, None]
    return jnp.where(eye, ct[:, :, :, None, :], 0.0).reshape(SSM_BLOCKS, SSM_BLOCK_STATE, 128)


def _diag_blocks(dense, rows, cols):
    d5 = dense.reshape(SSM_BLOCKS, 8, rows, 8, cols)
    return jnp.stack([d5[:, a, :, a, :] for a in range(8)], axis=1).reshape(32, rows, cols)


def _pack_small(ada_vec, parts):
    rest = jnp.concatenate([parts[n].reshape(-1) for n, _ in SMALL_PARAMS])
    rest = jnp.pad(rest, (0, NDEV * REST_ROWS * 128 - SMALL_TOTAL)).reshape(NDEV, REST_ROWS, 128)
    return jnp.concatenate([ada_vec.reshape(NDEV, ADA_ROWS, 128), rest,
                            jnp.zeros((NDEV, PACK_ROWS - ADA_ROWS - REST_ROWS, 128), F32)], axis=1)


def _unpack_small(pack, shapes):
    ada_vec = pack[:, :ADA_ROWS].reshape(-1)
    rest = pack[:, ADA_ROWS:ADA_ROWS + REST_ROWS].reshape(-1)
    out, off = {}, 0
    for n, size in SMALL_PARAMS:
        out[n] = rest[off:off + size].reshape(shapes[n])
        off += size
    return ada_vec, out


WEIGHT_ORDER = ('w_ada', 'b_ada', 'g_ffn1', 'w_ffn1_in', 'w_ffn1_out', 'g_mix', 'w_in', 'pool_w', 'pool_b',
                'pool_scale', 'w_pool_up', 'ssm_lam_re_log', 'ssm_lam_im', 'ssm_log_dt', 'ssm_b_re', 'ssm_b_im',
                'ssm_c_re', 'ssm_c_im', 'ssm_d', 'w_glu', 'b_glu', 'w_ssm_up', 'w_out', 'g_ffn2', 'w_ffn2_in',
                'w_ffn2_out', 'g_final')
GATHERED = ('w_ffn1_in', 'w_ffn1_out', 'w_in', 'w_pool_up', 'w_glu', 'w_ssm_up', 'w_out', 'w_ffn2_in', 'w_ffn2_out')


def kernel(x, c, w_ada, b_ada, g_ffn1, w_ffn1_in, w_ffn1_out, g_mix, w_in, pool_w, pool_b, pool_scale, w_pool_up, ssm_lam_re_log, ssm_lam_im, ssm_log_dt, ssm_b_re, ssm_b_im, ssm_c_re, ssm_c_im, ssm_d, w_glu, b_glu, w_ssm_up, w_out, g_ffn2, w_ffn2_in, w_ffn2_out, g_final, loss_target, m_w_ada, m_b_ada, m_g_ffn1, m_w_ffn1_in, m_w_ffn1_out, m_g_mix, m_w_in, m_pool_w, m_pool_b, m_pool_scale, m_w_pool_up, m_ssm_lam_re_log, m_ssm_lam_im, m_ssm_log_dt, m_ssm_b_re, m_ssm_b_im, m_ssm_c_re, m_ssm_c_im, m_ssm_d, m_w_glu, m_b_glu, m_w_ssm_up, m_w_out, m_g_ffn2, m_w_ffn2_in, m_w_ffn2_out, m_g_final, v_w_ada, v_b_ada, v_g_ffn1, v_w_ffn1_in, v_w_ffn1_out, v_g_mix, v_w_in, v_pool_w, v_pool_b, v_pool_scale, v_w_pool_up, v_ssm_lam_re_log, v_ssm_lam_im, v_ssm_log_dt, v_ssm_b_re, v_ssm_b_im, v_ssm_c_re, v_ssm_c_im, v_ssm_d, v_w_glu, v_b_glu, v_w_ssm_up, v_w_out, v_g_ffn2, v_w_ffn2_in, v_w_ffn2_out, v_g_final):
    args = locals()
    W = {n: args[n] for n in WEIGHT_ORDER}
    M = {n: args["m_" + n] for n in WEIGHT_ORDER}
    V = {n: args["v_" + n] for n in WEIGHT_ORDER}
    shapes = {n: W[n].shape for n in WEIGHT_ORDER}
    xt, tgt = x[0], loss_target[0]

    shard = dict(zip(GATHERED, _cast_shards([W[n][0] for n in GATHERED])))
    stacks = {}

    def gather(names):
        return _Gather([shard[n] for n in names])

    def gathered(names, results):
        stacks.update(zip(names, results))

    ffn1_w, ffn2_w = ('w_ffn1_in', 'w_ffn1_out'), ('w_ffn2_in', 'w_ffn2_out')
    mix_w = ('w_in', 'w_pool_up', 'w_glu', 'w_ssm_up', 'w_out')
    mod_cols, sc_all, *res = _ada_forward(c, W['w_ada'][0], b_ada.reshape(NDEV, -1), gather(ffn1_w[:1]))
    gathered(ffn1_w[:1], res)
    win1 = stacks['w_ffn1_in'].reshape(2, 4, D_MODEL, FF_SHARD)
    prm = jnp.concatenate([mod_cols.reshape(9, D_MODEL), g_ffn1, g_mix, g_ffn2, g_final[None], jnp.zeros((3, D_MODEL), F32)], axis=0)
    pad512 = jnp.zeros((1, D_MODEL - 512), F32)
    mvec = jnp.concatenate([jnp.concatenate([pool_b, pad512], axis=1), jnp.concatenate([pool_scale, pad512], axis=1),
                            jnp.concatenate([ssm_d, pad512], axis=1), b_glu, jnp.zeros((4, D_MODEL), F32)], axis=0)
    log_dt_col = ssm_log_dt[0][:, None]
    coeffs = _ssm_params_forward(ssm_lam_re_log[0], ssm_lam_im[0], log_dt_col)
    srow = jnp.stack([t.reshape(N_STATE) for t in coeffs], axis=0)
    b_dense = jnp.stack([_block_diag_in(ssm_b_re[0]), _block_diag_in(ssm_b_im[0])], axis=0)
    c_dense = jnp.stack([_block_diag_out(ssm_c_re[0]), _block_diag_out(ssm_c_im[0])], axis=0)
    bb, ct = _ssm_dense_forward(srow, b_dense, c_dense)
    pw = pool_w[0]

    late_w = ffn1_w[1:] + mix_w
    ab1, s1, *res = _ffn_hidden(xt, prm, win1, 0, ROW_G_FFN1, "ffn1_hidden", gather(late_w))
    gathered(late_w, res)
    wout1 = stacks['w_ffn1_out'].reshape(4, FF_SHARD, D_MODEL)
    x1, f1 = _ffn_out(xt, s1, prm, wout1, 0, "ffn1_out")
    w_out_full = stacks['w_out'].reshape(D_MODEL, D_MODEL)
    res = _mixer_forward(x1, prm, stacks['w_in'], stacks['w_pool_up'], stacks['w_glu'], stacks['w_ssm_up'],
                         w_out_full, pw, mvec, srow, bb, ct, gather(ffn2_w))
    x2, mo, saved = res[0], res[1], res[2:11]
    gathered(ffn2_w, res[11:])
    win2 = stacks['w_ffn2_in'].reshape(2, 4, D_MODEL, FF_SHARD)
    wout2 = stacks['w_ffn2_out'].reshape(4, FF_SHARD, D_MODEL)
    x3, f3, ab3 = _ffn_forward(x2, prm, win2, wout2, 2, ROW_G_FFN2, "ffn2_forward")
    d3, fin = _final_loss(x3, tgt, prm)
    loss = lax.psum(fin[1, 0], ("x", "y", "c"))

    lands = {}

    def scatter(grads):
        names = list(grads)
        return _Scatter([grads[n][0] for n in names], [grads[n][1] for n in names], [W[n].shape[1:] for n in names])

    def scattered(grads, results):
        lands.update(zip(grads, results))

    parts3, dwin2, dwout2 = _ffn_backward(x2, d3, ab3, prm, win2, wout2, 2, ROW_G_FFN2, "ffn2_backward")
    d2, sums3 = _norm_backward(parts3, x2, d3, f3, prm, 2, ROW_G_FFN2, 0.5, "ffn2_norm_backward")
    g_ffn2_w = {'w_ffn2_in': (dwin2, _halves), 'w_ffn2_out': (dwout2.reshape(NDEV, -1, D_MODEL), _stacked)}
    res = _mixer_backward(d2, prm, saved, stacks['w_pool_up'], stacks['w_glu'], stacks['w_ssm_up'], w_out_full, pw, mvec,
                          srow, bb, ct, scatter(g_ffn2_w))
    dz, dwo, dwpu, dwglu, dwsu, dpw, dbb, dct, vsum, da = res[:10]
    scattered(g_ffn2_w, res[10:])
    parts2, dwin_mix = _mixer_in_backward(x1, dz, prm, stacks['w_in'])
    d1, sums2 = _norm_backward(parts2, x1, d2, mo, prm, 1, ROW_G_MIX, 1.0, "mixer_norm_backward")
    g_mix_w = {'w_in': (dwin_mix, _stacked), 'w_pool_up': (dwpu, _stacked), 'w_glu': (dwglu, _stacked),
               'w_ssm_up': (dwsu, _stacked), 'w_out': (dwo, _stacked)}
    parts1, dwin1, dwout1, *res = _ffn_backward(xt, d1, ab1, prm, win1, wout1, 0, ROW_G_FFN1, "ffn1_backward",
                                                scatter(g_mix_w))
    scattered(g_mix_w, res)
    d0, sums1 = _norm_backward(parts1, xt, d1, f1, prm, 0, ROW_G_FFN1, 0.5, "ffn1_norm_backward")

    db_dense, df_rows = _ssm_dense_backward(dbb, da, srow, b_dense)
    cot = [df_rows[r].reshape(32, 64) for r in (2, 3, 0, 1)]
    d_lrl, d_li, d_ldt = _ssm_params_backward(ssm_lam_re_log[0], ssm_lam_im[0], log_dt_col, cot)
    small_grads = {
        'g_ffn1': sums1[0], 'g_mix': sums2[0], 'g_ffn2': sums3[0], 'g_final': fin[0], 'pool_w': dpw,
        'pool_b': vsum[1, :512], 'pool_scale': vsum[0, :512], 'ssm_lam_re_log': d_lrl, 'ssm_lam_im': d_li,
        'ssm_log_dt': d_ldt, 'ssm_b_re': jnp.transpose(_diag_blocks(db_dense[0], SSM_GROUP, SSM_STATE), (0, 2, 1)),
        'ssm_b_im': jnp.transpose(_diag_blocks(db_dense[1], SSM_GROUP, SSM_STATE), (0, 2, 1)),
        'ssm_c_re': jnp.transpose(_diag_blocks(dct[0], SSM_STATE, SSM_GROUP), (0, 2, 1)),
        'ssm_c_im': jnp.transpose(_diag_blocks(dct[1], SSM_STATE, SSM_GROUP), (0, 2, 1)),
        'ssm_d': vsum[2, :512], 'b_glu': vsum[3],
    }
    dmod = jnp.concatenate([sums1[1:4], sums2[1:4], sums3[1:4]], axis=0).reshape(-1)
    total, landed = _allreduce_small(_pack_small(dmod, small_grads))
    dmod_cols = landed[:, :ADA_ROWS].reshape(NDEV, ADA_ROWS * 128)

    g_ffn1_w = {'w_ffn1_in': (dwin1, _halves), 'w_ffn1_out': (dwout1.reshape(NDEV, -1, D_MODEL), _stacked)}
    last_views = [g_ffn1_w[n][1] for n in ffn1_w]
    send_sems, recv_sems, last_src, last_land, token = _scatter_start(
        [g_ffn1_w[n][0] for n in ffn1_w], last_views, [W[n].shape[1:] for n in ffn1_w], [total])
    total = total + token[0:1, 0:1]

    grad, delta, new_m, new_v = {}, {}, {}, {}

    def adam_sharded(n):
        res = _adam_sharded(W[n][0], M[n][0], V[n][0], lands[n], token, "adam_" + n)
        grad[n], delta[n], new_m[n], new_v[n] = [r[None] for r in res]
        return res[3]

    done = [adam_sharded(n) for n in GATHERED if n not in ffn1_w]
    res = _adam_ada(W['w_ada'][0], M['w_ada'][0], V['w_ada'][0], sc_all, dmod_cols + token[0:1, 0:1])
    grad['w_ada'], delta['w_ada'], new_m['w_ada'], new_v['w_ada'] = [r[None] for r in res]
    done.append(res[3])

    flat = lambda t: t.reshape(NDEV * PACK_ROWS, 128)
    small_w = flat(_pack_small(b_ada.reshape(-1), W))
    small_m = flat(_pack_small(m_b_ada.reshape(-1), M))
    small_v = flat(_pack_small(v_b_ada.reshape(-1), V))
    res = _adam_small(small_w, flat(total), small_m, small_v)
    done.append(res[2])
    for dst, packed in zip((grad, delta, new_m, new_v), (total, *res)):
        ada_vec, rest = _unpack_small(packed.reshape(NDEV, PACK_ROWS, 128), shapes)
        dst.update(rest)
        dst['b_ada'] = ada_vec.reshape(shapes['b_ada'])

    lands.update(zip(ffn1_w, _scatter_wait(send_sems, recv_sems, last_src, last_land, last_views, done)))
    for n in ffn1_w:
        adam_sharded(n)

    return (loss, d0[None], *[grad[n] for n in WEIGHT_ORDER], *[delta[n] for n in WEIGHT_ORDER],
            *[new_m[n] for n in WEIGHT_ORDER], *[new_v[n] for n in WEIGHT_ORDER])
```

```python
import functools

import jax
import jax.numpy as jnp
from jax import lax
from jax.experimental import pallas as pl
from jax.experimental.pallas import tpu as pltpu

F32 = jnp.float32
MXU_DTYPE = jnp.bfloat16
WIRE_DTYPE = jnp.bfloat16
SAVE_DTYPE = jnp.bfloat16

NDEV = 8
D_MODEL = 1024
D_FF = 2816
FF_SHARD = 2 * D_FF // NDEV
POOL_WIDTH = 512
POOL_GROUP = 128
SSM_WIDTH = 512
SSM_STATE = 64
SSM_GROUP = 16
SSM_BLOCKS = 4
SSM_BLOCK_STATE = 512
N_STATE = 2048
IN_WIDTH = 3072
EPS = 1e-6
ADAM_LR = 0.001
ADAM_B1 = 0.9
ADAM_B2 = 0.999
ADAM_EPS = 1e-08
ADAM_WD = 0.01
ADAM_STEP = 10

TM_FFN = 512
FFN_BWD_CHUNK = 256
TM_MIX = 256
TM_MIX_BWD = 256
TM_EW = 512
SCAN_ROWS = 8
POOL_HALO = 16
VMEM_LIMIT = 60 * 1024 * 1024

ROW_G_FFN1, ROW_G_MIX, ROW_G_FFN2, ROW_G_FINAL = 9, 10, 11, 12
ROW_POOL_B, ROW_POOL_SCALE, ROW_SSM_D, ROW_B_GLU = 0, 1, 2, 3

SMALL_PARAMS = (
    ("g_ffn1", 1024), ("g_mix", 1024), ("g_ffn2", 1024), ("g_final", 1024), ("pool_w", 65536),
    ("pool_b", 512), ("pool_scale", 512), ("ssm_lam_re_log", 2048), ("ssm_lam_im", 2048),
    ("ssm_log_dt", 32), ("ssm_b_re", 32768), ("ssm_b_im", 32768), ("ssm_c_re", 32768),
    ("ssm_c_im", 32768), ("ssm_d", 512), ("b_glu", 1024),
)
SMALL_TOTAL = sum(n for _, n in SMALL_PARAMS)
ADA_ROWS = 9
REST_ROWS = 203
PACK_ROWS = 216
MESH = pl.DeviceIdType.MESH


def _mm(a, b):
    return jnp.dot(a.astype(MXU_DTYPE), b.astype(MXU_DTYPE), preferred_element_type=F32)


def _mm_nt(a, b):
    return lax.dot_general(a.astype(MXU_DTYPE), b.astype(MXU_DTYPE), (((1,), (1,)), ((), ())),
                           preferred_element_type=F32)


def _mm_tn(a, b):
    return lax.dot_general(a.astype(MXU_DTYPE), b.astype(MXU_DTYPE), (((0,), (0,)), ((), ())),
                           preferred_element_type=F32)


def _rms_scale(x):
    return lax.rsqrt(jnp.mean(x * x, axis=-1, keepdims=True) + EPS)


def _sigmoid(x):
    return jax.nn.sigmoid(x)


def _colsum(x):
    return jnp.sum(x, axis=0, keepdims=True)


def _row(ref, r):
    return ref[r:r + 1, :]


def _params(*sem):
    return pltpu.CompilerParams(dimension_semantics=sem, vmem_limit_bytes=VMEM_LIMIT)


def _resident(a):
    return pl.BlockSpec(a.shape, lambda *_: (0,) * a.ndim, pipeline_mode=pl.Buffered(1))


def _me():
    return lax.axis_index("x"), lax.axis_index("y"), lax.axis_index("c")


def _peer(rel):
    x, y, c = _me()
    px = 1 - x if rel & 4 else x
    py = 1 - y if rel & 2 else y
    pc = 1 - c if rel & 1 else c
    return (px, py, pc), 4 * px + 2 * py + pc


_HBM = pl.BlockSpec(memory_space=pl.ANY)
_HBM_ONLY = pl.BlockSpec(memory_space=pltpu.HBM)


def _stacked(ref, p):
    return ref.at[p]


def _halves(ref, p):
    return ref.at[p // 4, p % 4]


class _Gather:
    def __init__(self, shards):
        self.operands = list(shards)
        self.n = len(shards)
        self.out_shape = [jax.ShapeDtypeStruct((NDEV,) + s.shape, s.dtype) for s in shards]
        self.scratch = [pltpu.SemaphoreType.DMA((7 * self.n,)), pltpu.SemaphoreType.DMA((7 * self.n,)),
                        pltpu.SemaphoreType.DMA((self.n,))]

    def plan(self, srcs, outs, sems):
        send_sems, recv_sems, local_sems = sems
        n = self.n
        x, y, c = _me()
        me = 4 * x + 2 * y + c
        here, sibling = (x, y, c), (x, y, 1 - c)
        chips = [(1 - x, y), (x, 1 - y), (1 - x, 1 - y)]

        def blk(px, py, pc):
            return 4 * px + 2 * py + pc

        def copy(a, k, block, to, src=None):
            return pltpu.make_async_remote_copy(
                src_ref=outs[a].at[block] if src is None else src, dst_ref=outs[a].at[block],
                send_sem=send_sems.at[7 * a + k], recv_sem=recv_sems.at[7 * a + k], device_id=to, device_id_type=MESH)

        def mine(a):
            return pltpu.make_async_copy(srcs[a], outs[a].at[me], local_sems.at[a])

        def first(a):
            return [copy(a, 0, me, sibling, src=srcs[a])] + [copy(a, 1 + j, me, (*chip, c), src=srcs[a])
                                                              for j, chip in enumerate(chips)]

        def start():
            for a in range(n):
                mine(a).start()
                for cp in first(a):
                    cp.start()

        def forward():
            for a in range(n):
                for j, chip in enumerate(chips):
                    copy(a, 1 + j, blk(*chip, c), here).wait_recv()
                    copy(a, 4 + j, blk(*chip, c), sibling).start()

        def finish():
            for a in range(n):
                copy(a, 0, blk(x, y, 1 - c), here).wait_recv()
                for j, chip in enumerate(chips):
                    copy(a, 4 + j, blk(*chip, 1 - c), here).wait_recv()
            for a in range(n):
                mine(a).wait()
                for cp in first(a):
                    cp.wait_send()
                for j, chip in enumerate(chips):
                    copy(a, 4 + j, blk(*chip, c), sibling).wait_send()

        return start, forward, finish


class _Scatter:
    def __init__(self, arrays, views, shard_shapes):
        self.operands = list(arrays)
        self.views = list(views)
        self.n = len(arrays)
        self.out_shape = [jax.ShapeDtypeStruct((NDEV,) + tuple(s), a.dtype) for s, a in zip(shard_shapes, arrays)]
        self.scratch = [pltpu.SemaphoreType.DMA((7 * self.n,)), pltpu.SemaphoreType.DMA((7 * self.n,)),
                        pltpu.SemaphoreType.DMA((self.n,))]

    def plan(self, srcs, outs, sems):
        send_sems, recv_sems, local_sems = sems
        n, views = self.n, self.views
        x, y, c = _me()
        me = 4 * x + 2 * y + c

        def mine(a):
            return pltpu.make_async_copy(views[a](srcs[a], me), outs[a].at[me], local_sems.at[a])

        def copy(a, rel, sending):
            to, p = _peer(rel)
            return pltpu.make_async_remote_copy(
                src_ref=views[a](srcs[a], p), dst_ref=outs[a].at[me if sending else p],
                send_sem=send_sems.at[7 * a + rel - 1], recv_sem=recv_sems.at[7 * a + rel - 1],
                device_id=to if sending else (x, y, c), device_id_type=MESH)

        def start():
            for a in range(n):
                mine(a).start()
            for rel in range(1, 8):
                for a in range(n):
                    copy(a, rel, True).start()

        def forward():
            pass

        def finish():
            for rel in range(1, 8):
                for a in range(n):
                    copy(a, rel, False).wait_recv()
            for rel in range(1, 8):
                for a in range(n):
                    copy(a, rel, True).wait_send()
            for a in range(n):
                mine(a).wait()

        return start, forward, finish


def _launch(body, name, out_shape, in_specs, out_specs, operands, scratch=(), grid=None, semantics=None,
            carry=None, steps=None):
    out_shape, in_specs, out_specs = list(out_shape), list(in_specs), list(out_specs)
    operands, scratch = list(operands), list(scratch)
    n_in, n_out, n_scr = len(in_specs), len(out_shape), len(scratch)
    kernel_body = body
    if carry is not None:
        k = carry.n

        def kernel_body(*refs):
            ins, cin = refs[:n_in], refs[n_in:n_in + k]
            outs, cout = refs[n_in + k:n_in + k + n_out], refs[n_in + k + n_out:n_in + 2 * k + n_out]
            rest = refs[n_in + 2 * k + n_out:]
            scr, csem = rest[:n_scr], rest[n_scr:]
            start, forward, finish = carry.plan(cin, cout, csem)
            if steps is None:
                start()
                body(*ins, *outs, *scr)
                forward()
                finish()
            else:
                pl.when(steps()[0])(start)
                body(*ins, *outs, *scr)
                pl.when(steps()[1])(forward)
                pl.when(steps()[2])(finish)

        in_specs += [_HBM] * k
        out_shape += carry.out_shape
        out_specs += [_HBM] * k
        operands += carry.operands
        scratch += carry.scratch
    kwargs = {} if grid is None else {"grid": grid}
    params = pltpu.CompilerParams(vmem_limit_bytes=VMEM_LIMIT) if semantics is None else _params(*semantics)
    return pl.pallas_call(kernel_body, name=name, out_shape=out_shape, in_specs=in_specs, out_specs=out_specs,
                          scratch_shapes=scratch, compiler_params=params, **kwargs)(*operands)


def _grid_steps(nt):
    def steps():
        i = pl.program_id(0)
        return i == 0, i == nt // 2, i == nt - 1
    return steps


def _cast_shards(shards):
    n = len(shards)

    def body(*refs):
        for a in range(n):
            refs[n + a][...] = refs[a][...].astype(WIRE_DTYPE)

    return pl.pallas_call(body, name="cast_shards",
                          out_shape=[jax.ShapeDtypeStruct(s.shape, WIRE_DTYPE) for s in shards],
                          compiler_params=pltpu.CompilerParams(vmem_limit_bytes=VMEM_LIMIT))(*shards)


_SEM = pl.BlockSpec(memory_space=pltpu.SEMAPHORE)
_DATAFLOW = pltpu.SideEffectType.DATAFLOW_SIDE_EFFECTING


def _split_copy(arrays, views, landing, send_sems, recv_sems, a, rel):
    to, p = _peer(rel)
    x, y, c = _me()
    return pltpu.make_async_remote_copy(
        src_ref=views[a](arrays[a], p), dst_ref=landing[a].at[4 * x + 2 * y + c],
        send_sem=send_sems.at[NDEV * a + rel], recv_sem=recv_sems.at[NDEV * a + rel], device_id=to, device_id_type=MESH)


def _scatter_start(arrays, views, shard_shapes, after):
    n = len(arrays)
    landing = [pltpu.with_memory_space_constraint(lax.empty((NDEV,) + tuple(s), a.dtype), pltpu.HBM)
               for s, a in zip(shard_shapes, arrays)]
    arrays = [pltpu.with_memory_space_constraint(a, pltpu.HBM) for a in arrays]

    def body(*refs):
        ins, land = refs[:n], refs[n:2 * n]
        send_sems, recv_sems = refs[2 * n + len(after)], refs[2 * n + len(after) + 1]
        token = refs[-1]
        for rel in range(NDEV):
            for a in range(n):
                _split_copy(ins, views, land, send_sems, recv_sems, a, rel).start()
        token[...] = jnp.zeros_like(token)

    res = pl.pallas_call(
        body, name="scatter_start",
        out_shape=[pltpu.SemaphoreType.DMA((NDEV * n,)), pltpu.SemaphoreType.DMA((NDEV * n,))]
        + [pltpu.HBM(a.shape, a.dtype) for a in arrays] + [pltpu.HBM(l.shape, l.dtype) for l in landing]
        + [jax.ShapeDtypeStruct((8, 128), F32)],
        in_specs=[_HBM_ONLY] * (2 * n) + [_HBM] * len(after),
        out_specs=[_SEM, _SEM] + [_HBM_ONLY] * (2 * n) + [pl.BlockSpec(memory_space=pltpu.VMEM)],
        input_output_aliases={i: 2 + i for i in range(2 * n)},
        compiler_params=pltpu.CompilerParams(has_side_effects=_DATAFLOW),
    )(*arrays, *landing, *after)
    return res[0], res[1], res[2:2 + n], res[2 + n:2 + 2 * n], res[-1]


def _scatter_wait(send_sems, recv_sems, arrays, landing, views, after):
    n = len(arrays)

    def body(*refs):
        ins, land = refs[:n], refs[n:2 * n]
        send, recv = refs[2 * n], refs[2 * n + 1]
        for rel in range(NDEV):
            for a in range(n):
                cp = _split_copy(ins, views, land, send, recv, a, rel)
                cp.wait_send()
                cp.wait_recv()

    res = pl.pallas_call(
        body, name="scatter_wait",
        out_shape=[pltpu.HBM(a.shape, a.dtype) for a in arrays] + [pltpu.HBM(l.shape, l.dtype) for l in landing],
        in_specs=[_HBM_ONLY] * (2 * n) + [_SEM, _SEM] + [_HBM] * len(after),
        out_specs=[_HBM_ONLY] * (2 * n),
        input_output_aliases={i: i for i in range(2 * n)},
        compiler_params=pltpu.CompilerParams(has_side_effects=_DATAFLOW),
    )(*arrays, *landing, send_sems, recv_sems, *after)
    return res[n:]


def _ada_forward(c_row, w_ada, b_ada8, carry):
    cols = w_ada.shape[1]

    def body(c_ref, w_ref, b_ref, mod_ref, sc_ref, c_all, send_buf, recv_buf, send1, recv1, send2, recv2):
        x, y, c = _me()
        me = 4 * x + 2 * y + c
        rowi = lax.broadcasted_iota(jnp.int32, (8, D_MODEL), 0)
        c_all[me] = jnp.broadcast_to(c_ref[...], (8, D_MODEL))
        copies = []
        for rel in range(1, 8):
            to, _ = _peer(rel)
            cp = pltpu.make_async_remote_copy(src_ref=c_all.at[me], dst_ref=c_all.at[me], send_sem=send1.at[rel - 1],
                                              recv_sem=recv1.at[rel - 1], device_id=to, device_id_type=MESH)
            cp.start()
            copies.append(cp)
        for rel in range(1, 8):
            _, p = _peer(rel)
            pltpu.make_async_remote_copy(src_ref=c_all.at[p], dst_ref=c_all.at[p], send_sem=send1.at[rel - 1],
                                         recv_sem=recv1.at[rel - 1], device_id=(x, y, c), device_id_type=MESH).wait_recv()
        for cp in copies:
            cp.wait_send()
        cmat = jnp.zeros((8, D_MODEL), F32)
        for b in range(8):
            cmat = jnp.where(rowi == b, c_all[b], cmat)
        sc = cmat * _sigmoid(cmat)
        sc_ref[...] = sc
        modcols = _mm(sc, w_ref[...]) + b_ref[pl.ds(me, 1), :]
        for b in range(8):
            send_buf[b] = jnp.broadcast_to(modcols[b:b + 1, :], (8, cols))
        recv_buf[me] = send_buf[me]
        copies = []
        for rel in range(1, 8):
            to, p = _peer(rel)
            cp = pltpu.make_async_remote_copy(src_ref=send_buf.at[p], dst_ref=recv_buf.at[me], send_sem=send2.at[rel - 1],
                                              recv_sem=recv2.at[rel - 1], device_id=to, device_id_type=MESH)
            cp.start()
            copies.append(cp)
        for rel in range(1, 8):
            _, p = _peer(rel)
            pltpu.make_async_remote_copy(src_ref=send_buf.at[p], dst_ref=recv_buf.at[p], send_sem=send2.at[rel - 1],
                                         recv_sem=recv2.at[rel - 1], device_id=(x, y, c), device_id_type=MESH).wait_recv()
        for cp in copies:
            cp.wait_send()
        rowc = lax.broadcasted_iota(jnp.int32, (8, cols), 0)
        out = jnp.zeros((8, cols), F32)
        for k in range(8):
            out = jnp.where(rowc == k, recv_buf[k], out)
        mod_ref[...] = out

    return _launch(
        body, "ada_forward",
        out_shape=[jax.ShapeDtypeStruct((8, cols), F32), jax.ShapeDtypeStruct((8, D_MODEL), F32)],
        in_specs=[pl.BlockSpec(memory_space=pltpu.VMEM)] * 3,
        out_specs=[pl.BlockSpec(memory_space=pltpu.VMEM)] * 2,
        operands=(c_row, w_ada, b_ada8),
        scratch=[pltpu.VMEM((8, 8, D_MODEL), F32), pltpu.VMEM((8, 8, cols), F32), pltpu.VMEM((8, 8, cols), F32)]
        + [pltpu.SemaphoreType.DMA((7,))] * 4,
        carry=carry)


def _allreduce_small(pack):
    rows = pack.shape[1]

    def body(pack_ref, total_ref, land_ref, send1, recv1, send2, recv2):
        x, y, c = _me()
        me = 4 * x + 2 * y + c
        land_ref[me] = pack_ref[me]
        copies = []
        for rel in range(1, 8):
            to, p = _peer(rel)
            cp = pltpu.make_async_remote_copy(src_ref=pack_ref.at[p], dst_ref=land_ref.at[me], send_sem=send1.at[rel - 1],
                                              recv_sem=recv1.at[rel - 1], device_id=to, device_id_type=MESH)
            cp.start()
            copies.append(cp)
        for rel in range(1, 8):
            _, p = _peer(rel)
            pltpu.make_async_remote_copy(src_ref=pack_ref.at[p], dst_ref=land_ref.at[p], send_sem=send1.at[rel - 1],
                                         recv_sem=recv1.at[rel - 1], device_id=(x, y, c), device_id_type=MESH).wait_recv()
        for cp in copies:
            cp.wait_send()
        acc = land_ref[0]
        for b in range(1, 8):
            acc = acc + land_ref[b]
        total_ref[me] = acc
        copies = []
        for rel in range(1, 8):
            to, _ = _peer(rel)
            cp = pltpu.make_async_remote_copy(src_ref=total_ref.at[me], dst_ref=total_ref.at[me], send_sem=send2.at[rel - 1],
                                              recv_sem=recv2.at[rel - 1], device_id=to, device_id_type=MESH)
            cp.start()
            copies.append(cp)
        for rel in range(1, 8):
            _, p = _peer(rel)
            pltpu.make_async_remote_copy(src_ref=total_ref.at[p], dst_ref=total_ref.at[p], send_sem=send2.at[rel - 1],
                                         recv_sem=recv2.at[rel - 1], device_id=(x, y, c), device_id_type=MESH).wait_recv()
        for cp in copies:
            cp.wait_send()

    return pl.pallas_call(
        body, name="allreduce_small",
        out_shape=[jax.ShapeDtypeStruct((8, rows, 128), F32), jax.ShapeDtypeStruct((8, rows, 128), F32)],
        in_specs=[pl.BlockSpec(memory_space=pltpu.VMEM)],
        out_specs=[pl.BlockSpec(memory_space=pltpu.VMEM)] * 2,
        scratch_shapes=[pltpu.SemaphoreType.DMA((7,))] * 4,
        compiler_params=pltpu.CompilerParams(vmem_limit_bytes=VMEM_LIMIT),
    )(pack)


def _modulated(x, prm_ref, sub, g_row):
    shift, scale = _row(prm_ref, 3 * sub), _row(prm_ref, 3 * sub + 1)
    g = _row(prm_ref, g_row)
    r = _rms_scale(x)
    n0 = x * r
    return (n0 * g) * (1.0 + scale) + shift, r, n0


def _ffn_forward(x, prm, win, wout, sub, g_row, name, carry=None):
    T = x.shape[0]
    tm = min(T, TM_FFN)

    def body(x_ref, prm_ref, win_ref, wout_ref, xo_ref, f_ref, ab_ref):
        xv = x_ref[...]
        h, _, _ = _modulated(xv, prm_ref, sub, g_row)
        hb = h.astype(MXU_DTYPE)
        acc = None
        for j in range(4):
            a = _mm_nt(hb, win_ref[0, j])
            b = _mm_nt(hb, win_ref[1, j])
            ab_ref[0, j] = a.astype(SAVE_DTYPE)
            ab_ref[1, j] = b.astype(SAVE_DTYPE)
            s = (a * _sigmoid(a)) * b
            t = _mm(s, wout_ref[j])
            acc = t if acc is None else acc + t
        f_ref[...] = acc.astype(SAVE_DTYPE)
        xo_ref[...] = xv + (0.5 * _row(prm_ref, 3 * sub + 2)) * acc

    tok = pl.BlockSpec((tm, D_MODEL), lambda i: (i, 0))
    return _launch(
        body, name, grid=(T // tm,), semantics=("arbitrary",),
        out_shape=[jax.ShapeDtypeStruct((T, D_MODEL), F32), jax.ShapeDtypeStruct((T, D_MODEL), SAVE_DTYPE),
                   jax.ShapeDtypeStruct((2, 4, T, FF_SHARD), SAVE_DTYPE)],
        in_specs=[tok, _resident(prm), _resident(win), _resident(wout)],
        out_specs=[tok, tok, pl.BlockSpec((2, 4, tm, FF_SHARD), lambda i: (0, 0, i, 0))],
        operands=(x, prm, win, wout), carry=carry, steps=_grid_steps(T // tm))


def _ffn_hidden(x, prm, win, sub, g_row, name, carry=None):
    T = x.shape[0]
    tm = min(T, TM_FFN)

    def body(x_ref, prm_ref, win_ref, ab_ref, s_ref):
        h, _, _ = _modulated(x_ref[...], prm_ref, sub, g_row)
        hb = h.astype(MXU_DTYPE)
        for j in range(4):
            a = _mm_nt(hb, win_ref[0, j])
            b = _mm_nt(hb, win_ref[1, j])
            ab_ref[0, j] = a.astype(SAVE_DTYPE)
            ab_ref[1, j] = b.astype(SAVE_DTYPE)
            s_ref[j] = ((a * _sigmoid(a)) * b).astype(MXU_DTYPE)

    return _launch(
        body, name, grid=(T // tm,), semantics=("arbitrary",),
        out_shape=[jax.ShapeDtypeStruct((2, 4, T, FF_SHARD), SAVE_DTYPE), jax.ShapeDtypeStruct((4, T, FF_SHARD), MXU_DTYPE)],
        in_specs=[pl.BlockSpec((tm, D_MODEL), lambda i: (i, 0)), _resident(prm), _resident(win)],
        out_specs=[pl.BlockSpec((2, 4, tm, FF_SHARD), lambda i: (0, 0, i, 0)),
                   pl.BlockSpec((4, tm, FF_SHARD), lambda i: (0, i, 0))],
        operands=(x, prm, win), carry=carry, steps=_grid_steps(T // tm))


def _ffn_out(x, s, prm, wout, sub, name, carry=None):
    T = x.shape[0]
    tm = min(T, TM_FFN)

    def body(x_ref, s_ref, prm_ref, wout_ref, xo_ref, f_ref):
        acc = None
        for j in range(4):
            t = _mm(s_ref[j], wout_ref[j])
            acc = t if acc is None else acc + t
        f_ref[...] = acc.astype(SAVE_DTYPE)
        xo_ref[...] = x_ref[...] + (0.5 * _row(prm_ref, 3 * sub + 2)) * acc

    tok = pl.BlockSpec((tm, D_MODEL), lambda i: (i, 0))
    return _launch(
        body, name, grid=(T // tm,), semantics=("arbitrary",),
        out_shape=[jax.ShapeDtypeStruct((T, D_MODEL), F32), jax.ShapeDtypeStruct((T, D_MODEL), SAVE_DTYPE)],
        in_specs=[tok, pl.BlockSpec((4, tm, FF_SHARD), lambda i: (0, i, 0)), _resident(prm), _resident(wout)],
        out_specs=[tok, tok], operands=(x, s, prm, wout), carry=carry, steps=_grid_steps(T // tm))


def _ffn_backward(x, d, ab, prm, win, wout, sub, g_row, name, carry=None):
    T = x.shape[0]
    tm = min(T, TM_FFN)
    nt = T // tm
    chunk = min(tm, FFN_BWD_CHUNK)

    def body(x_ref, d_ref, ab_ref, prm_ref, win_ref, wout_ref, dh_ref, dwin_ref, dwout_ref, acc_in, acc_out):
        i = pl.program_id(1)

        @pl.when(i == 0)
        def _():
            acc_in[...] = jnp.zeros_like(acc_in)
            acc_out[...] = jnp.zeros_like(acc_out)

        wa, wb, wo = win_ref[0, 0], win_ref[1, 0], wout_ref[0]
        half_gate = 0.5 * _row(prm_ref, 3 * sub + 2)
        das, dbs, ss, hbs, dfss = [], [], [], [], []
        for ck in range(tm // chunk):
            rows = slice(ck * chunk, (ck + 1) * chunk)
            h, _, _ = _modulated(x_ref[rows, :], prm_ref, sub, g_row)
            hbs.append(h.astype(MXU_DTYPE))
            a = ab_ref[0, 0, rows, :].astype(F32)
            b = ab_ref[1, 0, rows, :].astype(F32)
            sg = _sigmoid(a)
            si = a * sg
            dfs = (half_gate * d_ref[rows, :]).astype(MXU_DTYPE)
            ds = _mm_nt(dfs, wo)
            da = (ds * b * (sg * (1.0 + a * (1.0 - sg)))).astype(MXU_DTYPE)
            db = (ds * si).astype(MXU_DTYPE)
            dh_ref[0, rows, :] = (_mm(da, wa) + _mm(db, wb)).astype(SAVE_DTYPE)
            das.append(da)
            dbs.append(db)
            ss.append((si * b).astype(MXU_DTYPE))
            dfss.append(dfs)
        cat = (lambda v: v[0]) if len(das) == 1 else (lambda v: jnp.concatenate(v, axis=0))
        hb = cat(hbs)
        acc_out[...] += _mm_tn(cat(ss), cat(dfss))
        acc_in[0] += _mm_tn(cat(das), hb)
        acc_in[1] += _mm_tn(cat(dbs), hb)

        @pl.when(i == nt - 1)
        def _():
            dwin_ref[0, 0] = acc_in[0].astype(WIRE_DTYPE)
            dwin_ref[1, 0] = acc_in[1].astype(WIRE_DTYPE)
            dwout_ref[0] = acc_out[...].astype(WIRE_DTYPE)

    def steps():
        j, i = pl.program_id(0), pl.program_id(1)
        return (j == 0) & (i == 0), (j == 2) & (i == 0), (j == 3) & (i == nt - 1)

    tok = pl.BlockSpec((tm, D_MODEL), lambda j, i: (i, 0))
    return _launch(
        body, name, grid=(4, nt), semantics=("arbitrary", "arbitrary"),
        out_shape=[jax.ShapeDtypeStruct((4, T, D_MODEL), SAVE_DTYPE),
                   jax.ShapeDtypeStruct(win.shape, WIRE_DTYPE), jax.ShapeDtypeStruct(wout.shape, WIRE_DTYPE)],
        in_specs=[tok, tok, pl.BlockSpec((2, 1, tm, FF_SHARD), lambda j, i: (0, j, i, 0)), _resident(prm),
                  pl.BlockSpec((2, 1, FF_SHARD, D_MODEL), lambda j, i: (0, j, 0, 0)),
                  pl.BlockSpec((1, FF_SHARD, D_MODEL), lambda j, i: (j, 0, 0))],
        out_specs=[pl.BlockSpec((1, tm, D_MODEL), lambda j, i: (j, i, 0)),
                   pl.BlockSpec((2, 1, FF_SHARD, D_MODEL), lambda j, i: (0, j, 0, 0)),
                   pl.BlockSpec((1, FF_SHARD, D_MODEL), lambda j, i: (j, 0, 0))],
        operands=(x, d, ab, prm, win, wout),
        scratch=[pltpu.VMEM((2, FF_SHARD, D_MODEL), F32), pltpu.VMEM((FF_SHARD, D_MODEL), F32)],
        carry=carry, steps=steps)


def _norm_backward(parts, x, d, f, prm, sub, g_row, gate_coef, name, carry=None):
    T = x.shape[0]
    tm = min(T, TM_EW)
    P = parts.shape[0]

    def body(p_ref, x_ref, d_ref, f_ref, prm_ref, dx_ref, sums_ref):
        i = pl.program_id(0)
        dh = p_ref[0].astype(F32)
        for k in range(1, P):
            dh = dh + p_ref[k].astype(F32)
        xv, dv = x_ref[...], d_ref[...]
        scale, g = _row(prm_ref, 3 * sub + 1), _row(prm_ref, g_row)
        r = _rms_scale(xv)
        n0 = xv * r
        dn = dh * (1.0 + scale)
        dn0 = dn * g
        dx_ref[...] = dv + r * (dn0 - n0 * jnp.mean(dn0 * n0, axis=-1, keepdims=True))
        upd = jnp.concatenate([_colsum(dn * n0), _colsum(dh), _colsum(dh * (n0 * g)),
                               gate_coef * _colsum(dv * f_ref[...].astype(F32)), jnp.zeros((4, D_MODEL), F32)], axis=0)

        @pl.when(i == 0)
        def _():
            sums_ref[...] = upd

        @pl.when(i > 0)
        def _():
            sums_ref[...] += upd

    tok = pl.BlockSpec((tm, D_MODEL), lambda i: (i, 0))
    return _launch(
        body, name, grid=(T // tm,), semantics=("arbitrary",),
        out_shape=[jax.ShapeDtypeStruct((T, D_MODEL), F32), jax.ShapeDtypeStruct((8, D_MODEL), F32)],
        in_specs=[pl.BlockSpec((P, tm, D_MODEL), lambda i: (0, i, 0)), tok, tok, tok, _resident(prm)],
        out_specs=[tok, pl.BlockSpec((8, D_MODEL), lambda i: (0, 0))],
        operands=(parts, x, d, f, prm), carry=carry, steps=_grid_steps(T // tm))


def _final_loss(x, target, prm):
    T = x.shape[0]
    tm = min(T, TM_EW)

    def body(x_ref, t_ref, prm_ref, dx_ref, sums_ref):
        i = pl.program_id(0)
        xv = x_ref[...]
        g = _row(prm_ref, ROW_G_FINAL)
        r = _rms_scale(xv)
        n0 = xv * r
        err = n0 * g - t_ref[...]
        dy = err / float(D_MODEL)
        dn0 = dy * g
        dx_ref[...] = r * (dn0 - n0 * jnp.mean(dn0 * n0, axis=-1, keepdims=True))
        loss = 0.5 * jnp.sum(jnp.mean(err * err, axis=-1, keepdims=True), axis=0, keepdims=True)
        upd = jnp.concatenate([_colsum(dy * n0), jnp.broadcast_to(loss, (1, D_MODEL)), jnp.zeros((6, D_MODEL), F32)], axis=0)

        @pl.when(i == 0)
        def _():
            sums_ref[...] = upd

        @pl.when(i > 0)
        def _():
            sums_ref[...] += upd

    tok = pl.BlockSpec((tm, D_MODEL), lambda i: (i, 0))
    return pl.pallas_call(
        body, name="final_loss", grid=(T // tm,),
        out_shape=[jax.ShapeDtypeStruct((T, D_MODEL), F32), jax.ShapeDtypeStruct((8, D_MODEL), F32)],
        in_specs=[tok, tok, pl.BlockSpec(prm.shape, lambda i: (0, 0))],
        out_specs=[tok, pl.BlockSpec((8, D_MODEL), lambda i: (0, 0))],
        compiler_params=_params("arbitrary"),
    )(x, target, prm)


def _ssm_discretise(lam_re_log, lam_im, log_dt):
    lr = -jnp.exp(lam_re_log)
    dt = jnp.exp(log_dt)
    mag = jnp.exp(lr * dt)
    ang = lam_im * dt
    ab_re = mag * jnp.cos(ang)
    ab_im = mag * jnp.sin(ang)
    num_re = ab_re - 1.0
    num_im = ab_im
    den = lr * lr + lam_im * lam_im
    f_re = (num_re * lr + num_im * lam_im) / den
    f_im = (num_im * lr - num_re * lam_im) / den
    return ab_re, ab_im, f_re, f_im


def _ssm_params_forward(lam_re_log, lam_im, log_dt):
    def body(a_ref, b_ref, c_ref, o0, o1, o2, o3):
        outs = _ssm_discretise(a_ref[...], b_ref[...], c_ref[...])
        for o, v in zip((o0, o1, o2, o3), outs):
            o[...] = v

    return pl.pallas_call(body, name="ssm_params_forward",
                          out_shape=[jax.ShapeDtypeStruct(lam_im.shape, F32)] * 4)(lam_re_log, lam_im, log_dt)


def _ssm_params_backward(lam_re_log, lam_im, log_dt, cot):
    def body(a_ref, b_ref, c_ref, g0, g1, g2, g3, o0, o1, o2):
        _, vjp = jax.vjp(_ssm_discretise, a_ref[...], b_ref[...], c_ref[...])
        d0, d1, d2 = vjp((g0[...], g1[...], g2[...], g3[...]))
        o0[...] = d0
        o1[...] = d1
        o2[...] = d2

    return pl.pallas_call(
        body, name="ssm_params_backward",
        out_shape=[jax.ShapeDtypeStruct(lam_im.shape, F32), jax.ShapeDtypeStruct(lam_im.shape, F32),
                   jax.ShapeDtypeStruct(log_dt.shape, F32)])(lam_re_log, lam_im, log_dt, *cot)


def _ssm_dense_forward(srow, b_dense, c_dense):
    def body(srow_ref, bd_ref, cd_ref, bb_ref, ct_ref):
        for j in range(SSM_BLOCKS):
            lanes = slice(j * SSM_BLOCK_STATE, (j + 1) * SSM_BLOCK_STATE)
            f_re, f_im = srow_ref[2:3, lanes], srow_ref[3:4, lanes]
            bb_ref[0, j] = (f_re * bd_ref[0, j] - f_im * bd_ref[1, j]).astype(MXU_DTYPE)
            bb_ref[1, j] = (f_re * bd_ref[1, j] + f_im * bd_ref[0, j]).astype(MXU_DTYPE)
            ct_ref[0, j] = cd_ref[0, j].astype(MXU_DTYPE)
            ct_ref[1, j] = cd_ref[1, j].astype(MXU_DTYPE)

    return pl.pallas_call(body, name="ssm_dense_forward",
                          out_shape=[jax.ShapeDtypeStruct(b_dense.shape, MXU_DTYPE),
                                     jax.ShapeDtypeStruct(c_dense.shape, MXU_DTYPE)],
                          compiler_params=pltpu.CompilerParams(vmem_limit_bytes=VMEM_LIMIT))(srow, b_dense, c_dense)


def _cmul(p, q):
    return p[0] * q[0] - p[1] * q[1], p[0] * q[1] + p[1] * q[0]


def _scan_coefficients(ar, ai, reverse):
    n = ar.shape[1]
    p = {1: (ar, ai)}
    p[2] = _cmul(p[1], p[1])
    p[3] = _cmul(p[2], p[1])
    p[4] = _cmul(p[2], p[2])
    p[5] = _cmul(p[4], p[1])
    p[6] = _cmul(p[4], p[2])
    p[7] = _cmul(p[4], p[3])
    p[8] = _cmul(p[4], p[4])
    rowi = lax.broadcasted_iota(jnp.int32, (SCAN_ROWS, n), 0)
    tiles = []
    for dstep in (1, 2, 4):
        keep = (rowi < SCAN_ROWS - dstep) if reverse else (rowi >= dstep)
        for part in p[dstep]:
            tiles.append(jnp.where(keep, jnp.broadcast_to(part, (SCAN_ROWS, n)), 0.0))
    for comp in (0, 1):
        t = jnp.zeros((SCAN_ROWS, n), F32)
        for rr in range(SCAN_ROWS):
            power = SCAN_ROWS - rr if reverse else rr + 1
            t = jnp.where(rowi == rr, jnp.broadcast_to(p[power][comp], (SCAN_ROWS, n)), t)
        tiles.append(t)
    return tiles


def _load_stack(stack_hbm, dst, sems, base):
    cols = stack_hbm.shape[2]
    cps = [pltpu.make_async_copy(stack_hbm.at[k], dst.at[:, pl.ds(k * cols, cols)], sems.at[base + k])
           for k in range(NDEV)]
    for cp in cps:
        cp.start()
    return cps


def _window_lanes():
    lane = lax.broadcasted_iota(jnp.int32, (1, POOL_WIDTH), 1)
    return jnp.where(lane < 128, 2.0, jnp.where(lane < 256, 4.0, jnp.where(lane < 384, 8.0, 16.0)))


def _gelu(y):
    return 0.5 * y * (1.0 + lax.erf(y * 0.7071067811865476))


def _gelu_grad(y):
    return 0.5 * (1.0 + lax.erf(y * 0.7071067811865476)) + y * jnp.exp(-0.5 * y * y) * 0.3989422804014327


def _mixer_forward(x, prm, w_in_s, w_pu_s, w_glu_s, w_su_s, w_out, pool_w, mvec, srow, bb, ct, carry=None):
    T = x.shape[0]
    tm = min(T, TM_MIX)
    nt = T // tm
    n_tiles = tm // SCAN_ROWS

    def body(x_ref, prm_ref, w_in_h, w_pu_h, w_glu_h, w_su_h, w_out_h, pw_ref, mv_ref, srow_ref, bb, ct,
             x2_ref, mo_ref, z_ref, sre_ref, sim_ref, zp_ref, q_ref, yp_ref, yss_ref, vg_ref, ys_ref,
             w_in, w_pu, w_glu, w_su, w_o, coef, carry, hist, bu, sems):
        i = pl.program_id(0)

        @pl.when(i == 0)
        def _():
            cps = (_load_stack(w_in_h, w_in, sems, 0) + _load_stack(w_pu_h, w_pu, sems, 8)
                   + _load_stack(w_glu_h, w_glu, sems, 16) + _load_stack(w_su_h, w_su, sems, 24))
            cps.append(pltpu.make_async_copy(w_out_h, w_o, sems.at[32]))
            cps[-1].start()
            for j in range(SSM_BLOCKS):
                lanes = slice(j * SSM_BLOCK_STATE, (j + 1) * SSM_BLOCK_STATE)
                for k, t in enumerate(_scan_coefficients(srow_ref[0:1, lanes], srow_ref[1:2, lanes], False)):
                    coef[j, k] = t
            carry[...] = jnp.zeros_like(carry)
            hist[...] = jnp.zeros_like(hist)
            for cp in cps:
                cp.wait()

        xv = x_ref[...]
        h, _, _ = _modulated(xv, prm_ref, 1, ROW_G_MIX)
        z = _mm(h, w_in[...])
        z_ref[...] = z.astype(SAVE_DTYPE)
        u_pool, u_ssm = z[:, 0:512], z[:, 512:1024]
        gl_pool, gl_ssm = z[:, 1024:2048], z[:, 2048:3072]

        ext = jnp.concatenate([hist[...], u_pool], axis=0)
        w2 = ext + pltpu.roll(ext, 1, 0)
        w4 = w2[:, 128:] + pltpu.roll(w2[:, 128:], 2, 0)
        w8 = w4[:, 128:] + pltpu.roll(w4[:, 128:], 4, 0)
        w16 = w8[:, 128:] + pltpu.roll(w8[:, 128:], 8, 0)
        wsum = jnp.concatenate([w2[POOL_HALO:, :128], w4[POOL_HALO:, :128], w8[POOL_HALO:, :128], w16[POOL_HALO:]], axis=1)
        hist[...] = u_pool[tm - POOL_HALO:, :]
        t1 = (lax.broadcasted_iota(jnp.int32, (tm, 1), 0) + (i * tm + 1)).astype(F32)
        zp = wsum / jnp.minimum(t1, _window_lanes()) - u_pool
        zp_ref[...] = zp.astype(SAVE_DTYPE)
        q = jnp.concatenate([_mm(zp[:, k * 128:(k + 1) * 128], pw_ref[k]) for k in range(4)], axis=1)
        q = q + mv_ref[ROW_POOL_B:ROW_POOL_B + 1, 0:512]
        q_ref[...] = q.astype(SAVE_DTYPE)
        y_pool = _mm(q * mv_ref[ROW_POOL_SCALE:ROW_POOL_SCALE + 1, 0:512], w_pu[...])
        yp_ref[...] = y_pool.astype(SAVE_DTYPE)

        y_blocks = []
        for j in range(SSM_BLOCKS):
            lanes = pl.ds(j * SSM_BLOCK_STATE, SSM_BLOCK_STATE)
            ub = u_ssm[:, j * 128:(j + 1) * 128].astype(MXU_DTYPE)
            bu[0] = _mm(ub, bb[0, j])
            bu[1] = _mm(ub, bb[1, j])
            a1r, a1i, a2r, a2i, a4r, a4i, pr, pi = [coef[j, k] for k in range(8)]

            def step(tt, c, lanes=lanes, a1r=a1r, a1i=a1i, a2r=a2r, a2i=a2i, a4r=a4r, a4i=a4i, pr=pr, pi=pi):
                cr, ci = c
                rows = pl.ds(pl.multiple_of(tt * SCAN_ROWS, SCAN_ROWS), SCAN_ROWS)
                xr, xi = bu[0, rows, :], bu[1, rows, :]
                for dstep, kr, ki in ((1, a1r, a1i), (2, a2r, a2i), (4, a4r, a4i)):
                    sr, si = pltpu.roll(xr, dstep, 0), pltpu.roll(xi, dstep, 0)
                    xr, xi = xr + kr * sr - ki * si, xi + kr * si + ki * sr
                xr, xi = xr + pr * cr - pi * ci, xi + pr * ci + pi * cr
                sre_ref[rows, lanes] = xr
                sim_ref[rows, lanes] = xi
                return (jnp.broadcast_to(xr[SCAN_ROWS - 1:SCAN_ROWS, :], xr.shape),
                        jnp.broadcast_to(xi[SCAN_ROWS - 1:SCAN_ROWS, :], xi.shape))

            cr, ci = lax.fori_loop(0, n_tiles, step, (carry[j, 0], carry[j, 1]))
            carry[j, 0] = cr
            carry[j, 1] = ci
            y_blocks.append(_mm(sre_ref[:, lanes], ct[0, j]) - _mm(sim_ref[:, lanes], ct[1, j]))
        yss = jnp.concatenate(y_blocks, axis=1) + mv_ref[ROW_SSM_D:ROW_SSM_D + 1, 0:512] * u_ssm
        yss_ref[...] = yss.astype(SAVE_DTYPE)
        vg = _mm(_gelu(yss), w_glu[...]) + mv_ref[ROW_B_GLU:ROW_B_GLU + 1, :]
        vg_ref[...] = vg.astype(SAVE_DTYPE)
        y_ssm = _mm(vg[:, 0:512] * _sigmoid(vg[:, 512:1024]), w_su[...])
        ys_ref[...] = y_ssm.astype(SAVE_DTYPE)

        merged = _sigmoid(gl_pool) * y_pool + _sigmoid(gl_ssm) * y_ssm
        mo = _mm(merged, w_o[...])
        mo_ref[...] = mo.astype(SAVE_DTYPE)
        x2_ref[...] = xv + _row(prm_ref, 5) * mo

    def tok(width):
        return pl.BlockSpec((tm, width), lambda i: (i, 0))

    hbm = _HBM
    widths = (D_MODEL, D_MODEL, IN_WIDTH, N_STATE, N_STATE, 512, 512, D_MODEL, 512, D_MODEL, D_MODEL)
    dtypes = (F32, SAVE_DTYPE, SAVE_DTYPE, F32, F32) + (SAVE_DTYPE,) * 6
    return _launch(
        body, "mixer_forward", grid=(nt,), semantics=("arbitrary",), carry=carry, steps=_grid_steps(nt),
        out_shape=[jax.ShapeDtypeStruct((T, w), dt) for w, dt in zip(widths, dtypes)],
        in_specs=[tok(D_MODEL), _resident(prm), hbm, hbm, hbm, hbm, hbm, _resident(pool_w), _resident(mvec),
                  _resident(srow), _resident(bb), _resident(ct)],
        out_specs=[tok(w) for w in widths],
        operands=(x, prm, w_in_s, w_pu_s, w_glu_s, w_su_s, w_out, pool_w, mvec, srow, bb, ct),
        scratch=[
            pltpu.VMEM((D_MODEL, IN_WIDTH), MXU_DTYPE), pltpu.VMEM((512, D_MODEL), MXU_DTYPE),
            pltpu.VMEM((512, D_MODEL), MXU_DTYPE), pltpu.VMEM((512, D_MODEL), MXU_DTYPE),
            pltpu.VMEM((D_MODEL, D_MODEL), MXU_DTYPE),
            pltpu.VMEM((SSM_BLOCKS, 8, SCAN_ROWS, SSM_BLOCK_STATE), F32),
            pltpu.VMEM((SSM_BLOCKS, 2, SCAN_ROWS, SSM_BLOCK_STATE), F32),
            pltpu.VMEM((POOL_HALO, POOL_WIDTH), F32),
            pltpu.VMEM((2, tm, SSM_BLOCK_STATE), F32),
            pltpu.SemaphoreType.DMA((33,)),
        ])


def _mixer_backward(d2, prm, saved, w_pu_s, w_glu_s, w_su_s, w_out, pool_w, mvec, srow, bb, ct, carry=None):
    z, s_re, s_im, zp, q, y_pool, yss, vg, y_ssm = saved
    T = d2.shape[0]
    tm = min(T, TM_MIX_BWD)
    nt = T // tm
    n_tiles = tm // SCAN_ROWS

    def body(d_ref, prm_ref, z_ref, sre_ref, sim_ref, zp_ref, q_ref, yp_ref, yss_ref, vg_ref, ys_ref,
             w_pu_h, w_glu_h, w_su_h, w_out_h, pw_ref, mv_ref, srow_ref, bb, ct,
             dz_ref, dwo_h, dwpu_h, dwglu_h, dwsu_h, dpw_h, dbb_h, dct_h, vsum_h, da_h,
             w_pu, w_glu, w_su, w_o, pwb, coef, carry, hist, dre, lam,
             a_wo, a_wpu, a_wglu, a_wsu, a_pw, a_bb, a_ct, a_vs, a_da, st_wo, st_up, sems):
        i = pl.program_id(0)
        tile = nt - 1 - i

        @pl.when(i == 0)
        def _():
            cps = (_load_stack(w_pu_h, w_pu, sems, 0) + _load_stack(w_glu_h, w_glu, sems, 8)
                   + _load_stack(w_su_h, w_su, sems, 16))
            cps.append(pltpu.make_async_copy(w_out_h, w_o, sems.at[24]))
            cps[-1].start()
            pwb[...] = pw_ref[...].astype(MXU_DTYPE)
            for j in range(SSM_BLOCKS):
                lanes = slice(j * SSM_BLOCK_STATE, (j + 1) * SSM_BLOCK_STATE)
                for k, t in enumerate(_scan_coefficients(srow_ref[0:1, lanes], srow_ref[1:2, lanes], True)):
                    coef[j, k] = t
            for acc in (carry, hist, a_wo, a_wpu, a_wglu, a_wsu, a_pw, a_bb, a_ct, a_vs, a_da):
                acc[...] = jnp.zeros_like(acc)
            for cp in cps:
                cp.wait()

        dv = d_ref[...]
        zt = z_ref[...].astype(F32)
        u_ssm, gl_pool, gl_ssm = zt[:, 512:1024], zt[:, 1024:2048], zt[:, 2048:3072]
        y_p, y_s = yp_ref[...].astype(F32), ys_ref[...].astype(F32)
        sgp, sgs = _sigmoid(gl_pool), _sigmoid(gl_ssm)
        dmo = (_row(prm_ref, 5) * dv).astype(MXU_DTYPE)
        a_wo[...] += _mm_tn(sgp * y_p + sgs * y_s, dmo)
        dmerged = _mm_nt(dmo, w_o[...])
        dy_pool = dmerged * sgp
        dgl_pool = dmerged * y_p * (sgp * (1.0 - sgp))
        dy_ssm = dmerged * sgs
        dgl_ssm = dmerged * y_s * (sgs * (1.0 - sgs))

        scale = mv_ref[ROW_POOL_SCALE:ROW_POOL_SCALE + 1, 0:512]
        qv, zpv = q_ref[...].astype(F32), zp_ref[...]
        a_wpu[...] += _mm_tn(qv * scale, dy_pool)
        dp = _mm_nt(dy_pool, w_pu[...])
        dq = dp * scale
        a_vs[0:1, 0:512] += _colsum(dp * qv)
        a_vs[1:2, 0:512] += _colsum(dq)
        dzp_blocks = []
        for k in range(4):
            lanes = slice(k * 128, (k + 1) * 128)
            dzp_blocks.append(_mm_nt(dq[:, lanes], pwb[k]))
            a_pw[k] += _mm_tn(zpv[:, lanes], dq[:, lanes])
        dzp = jnp.concatenate(dzp_blocks, axis=1)
        t1 = (lax.broadcasted_iota(jnp.int32, (tm, 1), 0) + (tile * tm + 1)).astype(F32)
        gs = dzp / jnp.minimum(t1, _window_lanes())
        n_ext = tm + POOL_HALO
        ext = jnp.concatenate([gs, hist[...]], axis=0)
        v2 = ext + pltpu.roll(ext, n_ext - 1, 0)
        v4 = v2[:, 128:] + pltpu.roll(v2[:, 128:], n_ext - 2, 0)
        v8 = v4[:, 128:] + pltpu.roll(v4[:, 128:], n_ext - 4, 0)
        v16 = v8[:, 128:] + pltpu.roll(v8[:, 128:], n_ext - 8, 0)
        msum = jnp.concatenate([v2[:tm, :128], v4[:tm, :128], v8[:tm, :128], v16[:tm]], axis=1)
        hist[...] = gs[0:POOL_HALO, :]
        du_pool = msum - dzp

        vgv = vg_ref[...].astype(F32)
        val, gate = vgv[:, 0:512], vgv[:, 512:1024]
        sgg = _sigmoid(gate)
        a_wsu[...] += _mm_tn(val * sgg, dy_ssm)
        do = _mm_nt(dy_ssm, w_su[...])
        dvg = jnp.concatenate([do * sgg, do * val * (sgg * (1.0 - sgg))], axis=1)
        a_vs[3:4, :] += _colsum(dvg)
        yv = yss_ref[...].astype(F32)
        a_wglu[...] += _mm_tn(_gelu(yv), dvg)
        dyss = _mm_nt(dvg, w_glu[...]) * _gelu_grad(yv)
        a_vs[2:3, 0:512] += _colsum(dyss * u_ssm)
        du_blocks = []
        for j in range(SSM_BLOCKS):
            lanes = pl.ds(j * SSM_BLOCK_STATE, SSM_BLOCK_STATE)
            in_lanes = slice(j * 128, (j + 1) * 128)
            dyb = dyss[:, in_lanes].astype(MXU_DTYPE)
            ub = u_ssm[:, in_lanes].astype(MXU_DTYPE)
            dre[0] = _mm_nt(dyb, ct[0, j])
            dre[1] = -_mm_nt(dyb, ct[1, j])
            a_ct[0, j] += _mm_tn(sre_ref[:, lanes], dyb)
            a_ct[1, j] -= _mm_tn(sim_ref[:, lanes], dyb)
            a1r, a1i, a2r, a2i, a4r, a4i, pr, pi = [coef[j, k] for k in range(8)]
            rowi = lax.broadcasted_iota(jnp.int32, (SCAN_ROWS, SSM_BLOCK_STATE), 0)

            def step(tt, c, lanes=lanes, a1r=a1r, a1i=a1i, a2r=a2r, a2i=a2i, a4r=a4r, a4i=a4i, pr=pr, pi=pi, rowi=rowi):
                cr, ci, acc_r, acc_i = c
                rows = pl.ds(pl.multiple_of((n_tiles - 1 - tt) * SCAN_ROWS, SCAN_ROWS), SCAN_ROWS)
                xr, xi = dre[0, rows, :], dre[1, rows, :]
                for dstep, kr, ki in ((1, a1r, a1i), (2, a2r, a2i), (4, a4r, a4i)):
                    sr, si = pltpu.roll(xr, SCAN_ROWS - dstep, 0), pltpu.roll(xi, SCAN_ROWS - dstep, 0)
                    xr, xi = xr + kr * sr + ki * si, xi + kr * si - ki * sr
                xr, xi = xr + pr * cr + pi * ci, xi + pr * ci - pi * cr
                lam[0, rows, :] = xr
                lam[1, rows, :] = xi
                nr = jnp.where(rowi == SCAN_ROWS - 1, cr, pltpu.roll(xr, SCAN_ROWS - 1, 0))
                ni = jnp.where(rowi == SCAN_ROWS - 1, ci, pltpu.roll(xi, SCAN_ROWS - 1, 0))
                s_r, s_i = sre_ref[rows, lanes], sim_ref[rows, lanes]
                acc_r = acc_r + nr * s_r + ni * s_i
                acc_i = acc_i + ni * s_r - nr * s_i
                return (jnp.broadcast_to(xr[0:1, :], xr.shape), jnp.broadcast_to(xi[0:1, :], xi.shape), acc_r, acc_i)

            cr, ci, acc_r, acc_i = lax.fori_loop(0, n_tiles, step, (carry[j, 0], carry[j, 1], a_da[0, j], a_da[1, j]))
            carry[j, 0] = cr
            carry[j, 1] = ci
            a_da[0, j] = acc_r
            a_da[1, j] = acc_i
            lr_b, li_b = lam[0].astype(MXU_DTYPE), lam[1].astype(MXU_DTYPE)
            a_bb[0, j] += _mm_tn(ub, lr_b)
            a_bb[1, j] += _mm_tn(ub, li_b)
            du_blocks.append(_mm_nt(lr_b, bb[0, j]) + _mm_nt(li_b, bb[1, j]))
        du_ssm = jnp.concatenate(du_blocks, axis=1) + dyss * mv_ref[ROW_SSM_D:ROW_SSM_D + 1, 0:512]
        dz_ref[...] = jnp.concatenate([du_pool, du_ssm, dgl_pool, dgl_ssm], axis=1).astype(SAVE_DTYPE)

        @pl.when(i == nt - 1)
        def _():
            rows = D_MODEL // NDEV
            for k in range(NDEV):
                st_wo[k] = a_wo[k * rows:(k + 1) * rows, :].astype(WIRE_DTYPE)
                for a, acc in enumerate((a_wpu, a_wglu, a_wsu)):
                    st_up[a, k] = acc[:, k * 128:(k + 1) * 128].astype(WIRE_DTYPE)
            outs = ((st_wo, dwo_h), (st_up.at[0], dwpu_h), (st_up.at[1], dwglu_h), (st_up.at[2], dwsu_h),
                    (a_pw, dpw_h), (a_bb, dbb_h), (a_ct, dct_h), (a_vs, vsum_h), (a_da, da_h))
            cps = [pltpu.make_async_copy(src, dst, sems.at[k]) for k, (src, dst) in enumerate(outs)]
            for cp in cps:
                cp.start()
            for cp in cps:
                cp.wait()

    def tok(width):
        return pl.BlockSpec((tm, width), lambda i: (nt - 1 - i, 0))

    hbm = _HBM
    acc_shapes = [(D_MODEL, D_MODEL), (512, D_MODEL), (512, D_MODEL), (512, D_MODEL), (4, 128, 128),
                  (2, SSM_BLOCKS, 128, SSM_BLOCK_STATE), (2, SSM_BLOCKS, SSM_BLOCK_STATE, 128), (8, D_MODEL),
                  (2, SSM_BLOCKS, SCAN_ROWS, SSM_BLOCK_STATE)]
    stack_out = [jax.ShapeDtypeStruct((NDEV, D_MODEL // NDEV, D_MODEL), WIRE_DTYPE)] \
        + [jax.ShapeDtypeStruct((NDEV, 512, 128), WIRE_DTYPE)] * 3
    return _launch(
        body, "mixer_backward", grid=(nt,), semantics=("arbitrary",), carry=carry, steps=_grid_steps(nt),
        out_shape=[jax.ShapeDtypeStruct((T, IN_WIDTH), SAVE_DTYPE)] + stack_out
        + [jax.ShapeDtypeStruct(s, F32) for s in acc_shapes[4:]],
        in_specs=[tok(D_MODEL), _resident(prm), tok(IN_WIDTH), tok(N_STATE), tok(N_STATE), tok(512), tok(512),
                  tok(D_MODEL), tok(512), tok(D_MODEL), tok(D_MODEL), hbm, hbm, hbm, hbm, _resident(pool_w),
                  _resident(mvec), _resident(srow), _resident(bb), _resident(ct)],
        out_specs=[tok(IN_WIDTH)] + [hbm] * len(acc_shapes),
        operands=(d2, prm, z, s_re, s_im, zp, q, y_pool, yss, vg, y_ssm, w_pu_s, w_glu_s, w_su_s, w_out, pool_w, mvec,
                  srow, bb, ct),
        scratch=[
            pltpu.VMEM((512, D_MODEL), MXU_DTYPE), pltpu.VMEM((512, D_MODEL), MXU_DTYPE),
            pltpu.VMEM((512, D_MODEL), MXU_DTYPE), pltpu.VMEM((D_MODEL, D_MODEL), MXU_DTYPE),
            pltpu.VMEM((4, 128, 128), MXU_DTYPE),
            pltpu.VMEM((SSM_BLOCKS, 8, SCAN_ROWS, SSM_BLOCK_STATE), F32),
            pltpu.VMEM((SSM_BLOCKS, 2, SCAN_ROWS, SSM_BLOCK_STATE), F32),
            pltpu.VMEM((POOL_HALO, POOL_WIDTH), F32),
            pltpu.VMEM((2, tm, SSM_BLOCK_STATE), F32), pltpu.VMEM((2, tm, SSM_BLOCK_STATE), F32),
        ] + [pltpu.VMEM(s, F32) for s in acc_shapes]
        + [pltpu.VMEM((NDEV, D_MODEL // NDEV, D_MODEL), WIRE_DTYPE), pltpu.VMEM((3, NDEV, 512, 128), WIRE_DTYPE),
           pltpu.SemaphoreType.DMA((25,))])


def _mixer_in_backward(x, dz, prm, w_in_s):
    T = x.shape[0]
    tm = min(T, TM_MIX)
    nt = T // tm
    cols = IN_WIDTH // NDEV

    def body(x_ref, dz_ref, prm_ref, w_in_h, dh_ref, dw_ref, w_in, acc, sems):
        i = pl.program_id(0)

        @pl.when(i == 0)
        def _():
            cps = _load_stack(w_in_h, w_in, sems, 0)
            acc[...] = jnp.zeros_like(acc)
            for cp in cps:
                cp.wait()

        h, _, _ = _modulated(x_ref[...], prm_ref, 1, ROW_G_MIX)
        dzb = dz_ref[...].astype(MXU_DTYPE)
        dh_ref[0] = _mm_nt(dzb, w_in[...]).astype(SAVE_DTYPE)
        acc[...] += _mm_tn(h, dzb)

        @pl.when(i == nt - 1)
        def _():
            for k in range(NDEV):
                dw_ref[k] = acc[:, k * cols:(k + 1) * cols].astype(WIRE_DTYPE)

    return pl.pallas_call(
        body, name="mixer_in_backward", grid=(nt,),
        out_shape=[jax.ShapeDtypeStruct((1, T, D_MODEL), SAVE_DTYPE), jax.ShapeDtypeStruct((NDEV, D_MODEL, cols), WIRE_DTYPE)],
        in_specs=[pl.BlockSpec((tm, D_MODEL), lambda i: (i, 0)), pl.BlockSpec((tm, IN_WIDTH), lambda i: (i, 0)),
                  pl.BlockSpec(prm.shape, lambda i: (0, 0)), pl.BlockSpec(memory_space=pl.ANY)],
        out_specs=[pl.BlockSpec((1, tm, D_MODEL), lambda i: (0, i, 0)),
                   pl.BlockSpec((NDEV, D_MODEL, cols), lambda i: (0, 0, 0))],
        scratch_shapes=[pltpu.VMEM((D_MODEL, IN_WIDTH), MXU_DTYPE), pltpu.VMEM((D_MODEL, IN_WIDTH), F32),
                        pltpu.SemaphoreType.DMA((8,))],
        compiler_params=_params("arbitrary"),
    )(x, dz, prm, w_in_s)


def _ssm_dense_backward(dbb, da, srow, b_dense):
    def body(dbb_ref, da_ref, srow_ref, bd_ref, db_ref, df_ref):
        df_re, df_im = [], []
        da_re = [_colsum(da_ref[0, j]) for j in range(SSM_BLOCKS)]
        da_im = [_colsum(da_ref[1, j]) for j in range(SSM_BLOCKS)]
        for j in range(SSM_BLOCKS):
            lanes = slice(j * SSM_BLOCK_STATE, (j + 1) * SSM_BLOCK_STATE)
            f_re, f_im = srow_ref[2:3, lanes], srow_ref[3:4, lanes]
            g_re, g_im = dbb_ref[0, j], dbb_ref[1, j]
            b_re, b_im = bd_ref[0, j], bd_ref[1, j]
            db_ref[0, j] = f_re * g_re + f_im * g_im
            db_ref[1, j] = f_re * g_im - f_im * g_re
            df_re.append(_colsum(g_re * b_re + g_im * b_im))
            df_im.append(_colsum(g_im * b_re - g_re * b_im))
        df_ref[...] = jnp.concatenate([jnp.concatenate(df_re, axis=1), jnp.concatenate(df_im, axis=1),
                                       jnp.concatenate(da_re, axis=1), jnp.concatenate(da_im, axis=1),
                                       jnp.zeros((4, N_STATE), F32)], axis=0)

    return pl.pallas_call(body, name="ssm_dense_backward",
                          out_shape=[jax.ShapeDtypeStruct(b_dense.shape, F32), jax.ShapeDtypeStruct((8, N_STATE), F32)],
                          compiler_params=pltpu.CompilerParams(vmem_limit_bytes=VMEM_LIMIT))(dbb, da, srow, b_dense)


def _adamw_update(w, g, m, v):
    m = ADAM_B1 * m + (1.0 - ADAM_B1) * g
    v = ADAM_B2 * v + (1.0 - ADAM_B2) * (g * g)
    m_hat = m / (1.0 - ADAM_B1 ** ADAM_STEP)
    v_hat = v / (1.0 - ADAM_B2 ** ADAM_STEP)
    delta = -ADAM_LR * (m_hat / (jnp.sqrt(v_hat) + ADAM_EPS) + ADAM_WD * w)
    return delta, m, v


def _adam_rows(shape):
    rows, cols = shape
    tr = rows
    while tr * cols * 4 > (1 << 20) and tr % 16 == 0:
        tr //= 2
    return tr


def _adam_sharded(w, m, v, land, order, name):
    R, C = w.shape
    tr = _adam_rows((R, C))

    def body(w_ref, m_ref, v_ref, land_ref, order_ref, g_ref, d_ref, mo_ref, vo_ref):
        g = land_ref[0].astype(F32)
        for b in range(1, NDEV):
            g = g + land_ref[b].astype(F32)
        g_ref[...] = g
        d_ref[...], mo_ref[...], vo_ref[...] = _adamw_update(w_ref[...], g, m_ref[...], v_ref[...])

    blk = pl.BlockSpec((tr, C), lambda i: (i, 0))
    return pl.pallas_call(
        body, name=name, grid=(R // tr,),
        out_shape=[jax.ShapeDtypeStruct((R, C), F32)] * 4,
        in_specs=[blk, blk, blk, pl.BlockSpec((NDEV, tr, C), lambda i: (0, i, 0)), _HBM],
        out_specs=[blk] * 4,
        compiler_params=_params("arbitrary"),
    )(w, m, v, land, order)


def _adam_ada(w, m, v, sc_all, dmod_cols):
    R, C = w.shape
    tr = 256

    def body(w_ref, m_ref, v_ref, sc_ref, dm_ref, g_ref, d_ref, mo_ref, vo_ref):
        g = _mm_tn(sc_ref[...], dm_ref[...])
        g_ref[...] = g
        d_ref[...], mo_ref[...], vo_ref[...] = _adamw_update(w_ref[...], g, m_ref[...], v_ref[...])

    blk = pl.BlockSpec((tr, C), lambda i: (i, 0))
    return pl.pallas_call(
        body, name="adam_w_ada", grid=(R // tr,),
        out_shape=[jax.ShapeDtypeStruct((R, C), F32)] * 4,
        in_specs=[blk, blk, blk, pl.BlockSpec((8, tr), lambda i: (0, i)), pl.BlockSpec((8, C), lambda i: (0, 0))],
        out_specs=[blk] * 4,
        compiler_params=_params("arbitrary"),
    )(w, m, v, sc_all, dmod_cols)


def _adam_small(w, g, m, v):
    def body(w_ref, g_ref, m_ref, v_ref, d_ref, mo_ref, vo_ref):
        d_ref[...], mo_ref[...], vo_ref[...] = _adamw_update(w_ref[...], g_ref[...], m_ref[...], v_ref[...])

    return pl.pallas_call(body, name="adam_small", out_shape=[jax.ShapeDtypeStruct(w.shape, F32)] * 3,
                          compiler_params=pltpu.CompilerParams(vmem_limit_bytes=VMEM_LIMIT))(w, g, m, v)


def _block_diag_in(b):
    bt = jnp.transpose(b, (0, 2, 1)).reshape(SSM_BLOCKS, 8, SSM_GROUP, SSM_STATE)
    eye = jnp.eye(8, dtype=bool)[None, :, None, :, None]
    return jnp.where(eye, bt[:, :, :, None, :], 0.0).reshape(SSM_BLOCKS, 128, SSM_BLOCK_STATE)


def _block_diag_out(c):
    ct = jnp.transpose(c, (0, 2, 1)).reshape(SSM_BLOCKS, 8, SSM_STATE, SSM_GROUP)
    eye = jnp.eye(8, dtype=bool)[None, :, None, :, None]
    return jnp.where(eye, ct[:, :, :, None, :], 0.0).reshape(SSM_BLOCKS, SSM_BLOCK_STATE, 128)


def _diag_blocks(dense, rows, cols):
    d5 = dense.reshape(SSM_BLOCKS, 8, rows, 8, cols)
    return jnp.stack([d5[:, a, :, a, :] for a in range(8)], axis=1).reshape(32, rows, cols)


def _pack_small(ada_vec, parts):
    rest = jnp.concatenate([parts[n].reshape(-1) for n, _ in SMALL_PARAMS])
    rest = jnp.pad(rest, (0, NDEV * REST_ROWS * 128 - SMALL_TOTAL)).reshape(NDEV, REST_ROWS, 128)
    return jnp.concatenate([ada_vec.reshape(NDEV, ADA_ROWS, 128), rest,
                            jnp.zeros((NDEV, PACK_ROWS - ADA_ROWS - REST_ROWS, 128), F32)], axis=1)


def _unpack_small(pack, shapes):
    ada_vec = pack[:, :ADA_ROWS].reshape(-1)
    rest = pack[:, ADA_ROWS:ADA_ROWS + REST_ROWS].reshape(-1)
    out, off = {}, 0
    for n, size in SMALL_PARAMS:
        out[n] = rest[off:off + size].reshape(shapes[n])
        off += size
    return ada_vec, out


WEIGHT_ORDER = ('w_ada', 'b_ada', 'g_ffn1', 'w_ffn1_in', 'w_ffn1_out', 'g_mix', 'w_in', 'pool_w', 'pool_b',
                'pool_scale', 'w_pool_up', 'ssm_lam_re_log', 'ssm_lam_im', 'ssm_log_dt', 'ssm_b_re', 'ssm_b_im',
                'ssm_c_re', 'ssm_c_im', 'ssm_d', 'w_glu', 'b_glu', 'w_ssm_up', 'w_out', 'g_ffn2', 'w_ffn2_in',
                'w_ffn2_out', 'g_final')
GATHERED = ('w_ffn1_in', 'w_ffn1_out', 'w_in', 'w_pool_up', 'w_glu', 'w_ssm_up', 'w_out', 'w_ffn2_in', 'w_ffn2_out')
TRANSPOSED = ('w_ffn1_in', 'w_ffn2_in')


def kernel(x, c, w_ada, b_ada, g_ffn1, w_ffn1_in, w_ffn1_out, g_mix, w_in, pool_w, pool_b, pool_scale, w_pool_up, ssm_lam_re_log, ssm_lam_im, ssm_log_dt, ssm_b_re, ssm_b_im, ssm_c_re, ssm_c_im, ssm_d, w_glu, b_glu, w_ssm_up, w_out, g_ffn2, w_ffn2_in, w_ffn2_out, g_final, loss_target, m_w_ada, m_b_ada, m_g_ffn1, m_w_ffn1_in, m_w_ffn1_out, m_g_mix, m_w_in, m_pool_w, m_pool_b, m_pool_scale, m_w_pool_up, m_ssm_lam_re_log, m_ssm_lam_im, m_ssm_log_dt, m_ssm_b_re, m_ssm_b_im, m_ssm_c_re, m_ssm_c_im, m_ssm_d, m_w_glu, m_b_glu, m_w_ssm_up, m_w_out, m_g_ffn2, m_w_ffn2_in, m_w_ffn2_out, m_g_final, v_w_ada, v_b_ada, v_g_ffn1, v_w_ffn1_in, v_w_ffn1_out, v_g_mix, v_w_in, v_pool_w, v_pool_b, v_pool_scale, v_w_pool_up, v_ssm_lam_re_log, v_ssm_lam_im, v_ssm_log_dt, v_ssm_b_re, v_ssm_b_im, v_ssm_c_re, v_ssm_c_im, v_ssm_d, v_w_glu, v_b_glu, v_w_ssm_up, v_w_out, v_g_ffn2, v_w_ffn2_in, v_w_ffn2_out, v_g_final):
    args = locals()
    W = {n: args[n] for n in WEIGHT_ORDER}
    M = {n: args["m_" + n] for n in WEIGHT_ORDER}
    V = {n: args["v_" + n] for n in WEIGHT_ORDER}
    shapes = {n: W[n].shape for n in WEIGHT_ORDER}
    xt, tgt = x[0], loss_target[0]

    def local(tree, n):
        return jnp.swapaxes(tree[n][0], 0, 1) if n in TRANSPOSED else tree[n][0]

    def as_output(n, a):
        return (jnp.swapaxes(a, 0, 1) if n in TRANSPOSED else a)[None]

    shard = dict(zip(GATHERED, _cast_shards([local(W, n) for n in GATHERED])))
    stacks = {}

    def gather(names):
        return _Gather([shard[n] for n in names])

    def gathered(names, results):
        stacks.update(zip(names, results))

    ffn1_w, ffn2_w = ('w_ffn1_in', 'w_ffn1_out'), ('w_ffn2_in', 'w_ffn2_out')
    mix_w = ('w_in', 'w_pool_up', 'w_glu', 'w_ssm_up', 'w_out')
    mod_cols, sc_all, *res = _ada_forward(c, W['w_ada'][0], b_ada.reshape(NDEV, -1), gather(ffn1_w[:1]))
    gathered(ffn1_w[:1], res)
    win1 = stacks['w_ffn1_in'].reshape(2, 4, FF_SHARD, D_MODEL)
    prm = jnp.concatenate([mod_cols.reshape(9, D_MODEL), g_ffn1, g_mix, g_ffn2, g_final[None], jnp.zeros((3, D_MODEL), F32)], axis=0)
    pad512 = jnp.zeros((1, D_MODEL - 512), F32)
    mvec = jnp.concatenate([jnp.concatenate([pool_b, pad512], axis=1), jnp.concatenate([pool_scale, pad512], axis=1),
                            jnp.concatenate([ssm_d, pad512], axis=1), b_glu, jnp.zeros((4, D_MODEL), F32)], axis=0)
    log_dt_col = ssm_log_dt[0][:, None]
    coeffs = _ssm_params_forward(ssm_lam_re_log[0], ssm_lam_im[0], log_dt_col)
    srow = jnp.stack([t.reshape(N_STATE) for t in coeffs], axis=0)
    b_dense = jnp.stack([_block_diag_in(ssm_b_re[0]), _block_diag_in(ssm_b_im[0])], axis=0)
    c_dense = jnp.stack([_block_diag_out(ssm_c_re[0]), _block_diag_out(ssm_c_im[0])], axis=0)
    bb, ct = _ssm_dense_forward(srow, b_dense, c_dense)
    pw = pool_w[0]

    next_w = ffn1_w[1:] + mix_w[:1]
    ab1, s1, *res = _ffn_hidden(xt, prm, win1, 0, ROW_G_FFN1, "ffn1_hidden", gather(next_w))
    gathered(next_w, res)
    wout1 = stacks['w_ffn1_out'].reshape(4, FF_SHARD, D_MODEL)
    x1, f1, *res = _ffn_out(xt, s1, prm, wout1, 0, "ffn1_out", gather(mix_w[1:]))
    gathered(mix_w[1:], res)
    w_out_full = stacks['w_out'].reshape(D_MODEL, D_MODEL)
    res = _mixer_forward(x1, prm, stacks['w_in'], stacks['w_pool_up'], stacks['w_glu'], stacks['w_ssm_up'],
                         w_out_full, pw, mvec, srow, bb, ct, gather(ffn2_w))
    x2, mo, saved = res[0], res[1], res[2:11]
    gathered(ffn2_w, res[11:])
    win2 = stacks['w_ffn2_in'].reshape(2, 4, FF_SHARD, D_MODEL)
    wout2 = stacks['w_ffn2_out'].reshape(4, FF_SHARD, D_MODEL)
    x3, f3, ab3 = _ffn_forward(x2, prm, win2, wout2, 2, ROW_G_FFN2, "ffn2_forward")
    d3, fin = _final_loss(x3, tgt, prm)
    loss = lax.psum(fin[1, 0], ("x", "y", "c"))

    lands = {}

    def scatter(grads):
        names = list(grads)
        return _Scatter([grads[n][0] for n in names], [grads[n][1] for n in names], [local(W, n).shape for n in names])

    def scattered(grads, results):
        lands.update(zip(grads, results))

    parts3, dwin2, dwout2 = _ffn_backward(x2, d3, ab3, prm, win2, wout2, 2, ROW_G_FFN2, "ffn2_backward")
    d2, sums3 = _norm_backward(parts3, x2, d3, f3, prm, 2, ROW_G_FFN2, 0.5, "ffn2_norm_backward")
    g_ffn2_w = {'w_ffn2_in': (dwin2, _halves), 'w_ffn2_out': (dwout2.reshape(NDEV, -1, D_MODEL), _stacked)}
    res = _mixer_backward(d2, prm, saved, stacks['w_pool_up'], stacks['w_glu'], stacks['w_ssm_up'], w_out_full, pw, mvec,
                          srow, bb, ct, scatter(g_ffn2_w))
    dz, dwo, dwpu, dwglu, dwsu, dpw, dbb, dct, vsum, da = res[:10]
    scattered(g_ffn2_w, res[10:])
    parts2, dwin_mix = _mixer_in_backward(x1, dz, prm, stacks['w_in'])
    d1, sums2 = _norm_backward(parts2, x1, d2, mo, prm, 1, ROW_G_MIX, 1.0, "mixer_norm_backward")
    g_mix_w = {'w_in': (dwin_mix, _stacked), 'w_pool_up': (dwpu, _stacked), 'w_glu': (dwglu, _stacked),
               'w_ssm_up': (dwsu, _stacked), 'w_out': (dwo, _stacked)}
    parts1, dwin1, dwout1, *res = _ffn_backward(xt, d1, ab1, prm, win1, wout1, 0, ROW_G_FFN1, "ffn1_backward",
                                                scatter(g_mix_w))
    scattered(g_mix_w, res)
    d0, sums1 = _norm_backward(parts1, xt, d1, f1, prm, 0, ROW_G_FFN1, 0.5, "ffn1_norm_backward")

    db_dense, df_rows = _ssm_dense_backward(dbb, da, srow, b_dense)
    cot = [df_rows[r].reshape(32, 64) for r in (2, 3, 0, 1)]
    d_lrl, d_li, d_ldt = _ssm_params_backward(ssm_lam_re_log[0], ssm_lam_im[0], log_dt_col, cot)
    small_grads = {
        'g_ffn1': sums1[0], 'g_mix': sums2[0], 'g_ffn2': sums3[0], 'g_final': fin[0], 'pool_w': dpw,
        'pool_b': vsum[1, :512], 'pool_scale': vsum[0, :512], 'ssm_lam_re_log': d_lrl, 'ssm_lam_im': d_li,
        'ssm_log_dt': d_ldt, 'ssm_b_re': jnp.transpose(_diag_blocks(db_dense[0], SSM_GROUP, SSM_STATE), (0, 2, 1)),
        'ssm_b_im': jnp.transpose(_diag_blocks(db_dense[1], SSM_GROUP, SSM_STATE), (0, 2, 1)),
        'ssm_c_re': jnp.transpose(_diag_blocks(dct[0], SSM_STATE, SSM_GROUP), (0, 2, 1)),
        'ssm_c_im': jnp.transpose(_diag_blocks(dct[1], SSM_STATE, SSM_GROUP), (0, 2, 1)),
        'ssm_d': vsum[2, :512], 'b_glu': vsum[3],
    }
    dmod = jnp.concatenate([sums1[1:4], sums2[1:4], sums3[1:4]], axis=0).reshape(-1)
    total, landed = _allreduce_small(_pack_small(dmod, small_grads))
    dmod_cols = landed[:, :ADA_ROWS].reshape(NDEV, ADA_ROWS * 128)

    g_ffn1_w = {'w_ffn1_in': (dwin1, _halves), 'w_ffn1_out': (dwout1.reshape(NDEV, -1, D_MODEL), _stacked)}
    last_views = [g_ffn1_w[n][1] for n in ffn1_w]
    send_sems, recv_sems, last_src, last_land, token = _scatter_start(
        [g_ffn1_w[n][0] for n in ffn1_w], last_views, [local(W, n).shape for n in ffn1_w], [total])
    total = total + token[0:1, 0:1]

    grad, delta, new_m, new_v = {}, {}, {}, {}

    def adam_sharded(n):
        res = _adam_sharded(local(W, n), local(M, n), local(V, n), lands[n], token, "adam_" + n)
        grad[n], delta[n], new_m[n], new_v[n] = [as_output(n, r) for r in res]
        return res[3]

    done = [adam_sharded(n) for n in GATHERED if n not in ffn1_w]
    res = _adam_ada(W['w_ada'][0], M['w_ada'][0], V['w_ada'][0], sc_all, dmod_cols + token[0:1, 0:1])
    grad['w_ada'], delta['w_ada'], new_m['w_ada'], new_v['w_ada'] = [r[None] for r in res]
    done.append(res[3])

    flat = lambda t: t.reshape(NDEV * PACK_ROWS, 128)
    small_w = flat(_pack_small(b_ada.reshape(-1), W))
    small_m = flat(_pack_small(m_b_ada.reshape(-1), M))
    small_v = flat(_pack_small(v_b_ada.reshape(-1), V))
    res = _adam_small(small_w, flat(total), small_m, small_v)
    done.append(res[2])
    for dst, packed in zip((grad, delta, new_m, new_v), (total, *res)):
        ada_vec, rest = _unpack_small(packed.reshape(NDEV, PACK_ROWS, 128), shapes)
        dst.update(rest)
        dst['b_ada'] = ada_vec.reshape(shapes['b_ada'])

    lands.update(zip(ffn1_w, _scatter_wait(send_sems, recv_sems, last_src, last_land, last_views, done)))
    for n in ffn1_w:
        adam_sharded(n)

    return (loss, d0[None], *[grad[n] for n in WEIGHT_ORDER], *[delta[n] for n in WEIGHT_ORDER],
            *[new_m[n] for n in WEIGHT_ORDER], *[new_v[n] for n in WEIGHT_ORDER])
```

```python
import functools

import jax
import jax.numpy as jnp
from jax import lax
from jax.experimental import pallas as pl
from jax.experimental.pallas import tpu as pltpu

F32 = jnp.float32
MXU_DTYPE = jnp.bfloat16
WIRE_DTYPE = jnp.bfloat16
SAVE_DTYPE = jnp.bfloat16

NDEV = 8
D_MODEL = 1024
D_FF = 2816
FF_SHARD = 2 * D_FF // NDEV
POOL_WIDTH = 512
POOL_GROUP = 128
SSM_WIDTH = 512
SSM_STATE = 64
SSM_GROUP = 16
SSM_BLOCKS = 4
SSM_BLOCK_STATE = 512
N_STATE = 2048
IN_WIDTH = 3072
EPS = 1e-6
ADAM_LR = 0.001
ADAM_B1 = 0.9
ADAM_B2 = 0.999
ADAM_EPS = 1e-08
ADAM_WD = 0.01
ADAM_STEP = 10

TM_FFN = 512
FFN_BWD_CHUNK = 256
TM_MIX = 256
TM_MIX_BWD = 256
TM_EW = 512
SCAN_ROWS = 8
POOL_HALO = 16
VMEM_LIMIT = 60 * 1024 * 1024

ROW_G_FFN1, ROW_G_MIX, ROW_G_FFN2, ROW_G_FINAL = 9, 10, 11, 12
ROW_POOL_B, ROW_POOL_SCALE, ROW_SSM_D, ROW_B_GLU = 0, 1, 2, 3

SMALL_PARAMS = (
    ("g_ffn1", 1024), ("g_mix", 1024), ("g_ffn2", 1024), ("g_final", 1024), ("pool_w", 65536),
    ("pool_b", 512), ("pool_scale", 512), ("ssm_lam_re_log", 2048), ("ssm_lam_im", 2048),
    ("ssm_log_dt", 32), ("ssm_b_re", 32768), ("ssm_b_im", 32768), ("ssm_c_re", 32768),
    ("ssm_c_im", 32768), ("ssm_d", 512), ("b_glu", 1024),
)
SMALL_TOTAL = sum(n for _, n in SMALL_PARAMS)
ADA_ROWS = 9
REST_ROWS = 203
PACK_ROWS = 216
MESH = pl.DeviceIdType.MESH


def _mm(a, b):
    return jnp.dot(a.astype(MXU_DTYPE), b.astype(MXU_DTYPE), preferred_element_type=F32)


def _mm_nt(a, b):
    return lax.dot_general(a.astype(MXU_DTYPE), b.astype(MXU_DTYPE), (((1,), (1,)), ((), ())),
                           preferred_element_type=F32)


def _mm_tn(a, b):
    return lax.dot_general(a.astype(MXU_DTYPE), b.astype(MXU_DTYPE), (((0,), (0,)), ((), ())),
                           preferred_element_type=F32)


def _rms_scale(x):
    return lax.rsqrt(jnp.mean(x * x, axis=-1, keepdims=True) + EPS)


def _sigmoid(x):
    return jax.nn.sigmoid(x)


def _colsum(x):
    return jnp.sum(x, axis=0, keepdims=True)


def _row(ref, r):
    return ref[r:r + 1, :]


def _params(*sem):
    return pltpu.CompilerParams(dimension_semantics=sem, vmem_limit_bytes=VMEM_LIMIT)


def _resident(a):
    return pl.BlockSpec(a.shape, lambda *_: (0,) * a.ndim, pipeline_mode=pl.Buffered(1))


def _me():
    return lax.axis_index("x"), lax.axis_index("y"), lax.axis_index("c")


def _peer(rel):
    x, y, c = _me()
    px = 1 - x if rel & 4 else x
    py = 1 - y if rel & 2 else y
    pc = 1 - c if rel & 1 else c
    return (px, py, pc), 4 * px + 2 * py + pc


_HBM = pl.BlockSpec(memory_space=pl.ANY)
_HBM_ONLY = pl.BlockSpec(memory_space=pltpu.HBM)


def _stacked(ref, p):
    return ref.at[p]


def _halves(ref, p):
    return ref.at[p // 4, p % 4]


class _Gather:
    def __init__(self, shards):
        self.operands = list(shards)
        self.n = len(shards)
        self.out_shape = [jax.ShapeDtypeStruct((NDEV,) + s.shape, s.dtype) for s in shards]
        self.scratch = [pltpu.SemaphoreType.DMA((7 * self.n,)), pltpu.SemaphoreType.DMA((7 * self.n,)),
                        pltpu.SemaphoreType.DMA((self.n,))]

    def plan(self, srcs, outs, sems):
        send_sems, recv_sems, local_sems = sems
        n = self.n
        x, y, c = _me()
        me = 4 * x + 2 * y + c
        here, sibling = (x, y, c), (x, y, 1 - c)
        chips = [(1 - x, y), (x, 1 - y), (1 - x, 1 - y)]

        def blk(px, py, pc):
            return 4 * px + 2 * py + pc

        def copy(a, k, block, to, src=None):
            return pltpu.make_async_remote_copy(
                src_ref=outs[a].at[block] if src is None else src, dst_ref=outs[a].at[block],
                send_sem=send_sems.at[7 * a + k], recv_sem=recv_sems.at[7 * a + k], device_id=to, device_id_type=MESH)

        def mine(a):
            return pltpu.make_async_copy(srcs[a], outs[a].at[me], local_sems.at[a])

        def first(a):
            return [copy(a, 0, me, sibling, src=srcs[a])] + [copy(a, 1 + j, me, (*chip, c), src=srcs[a])
                                                              for j, chip in enumerate(chips)]

        def start():
            for a in range(n):
                mine(a).start()
                for cp in first(a):
                    cp.start()

        def forward():
            for a in range(n):
                for j, chip in enumerate(chips):
                    copy(a, 1 + j, blk(*chip, c), here).wait_recv()
                    copy(a, 4 + j, blk(*chip, c), sibling).start()

        def finish():
            for a in range(n):
                copy(a, 0, blk(x, y, 1 - c), here).wait_recv()
                for j, chip in enumerate(chips):
                    copy(a, 4 + j, blk(*chip, 1 - c), here).wait_recv()
            for a in range(n):
                mine(a).wait()
                for cp in first(a):
                    cp.wait_send()
                for j, chip in enumerate(chips):
                    copy(a, 4 + j, blk(*chip, c), sibling).wait_send()

        return start, forward, finish


class _Scatter:
    def __init__(self, arrays, views, shard_shapes):
        self.operands = list(arrays)
        self.views = list(views)
        self.n = len(arrays)
        self.out_shape = [jax.ShapeDtypeStruct((NDEV,) + tuple(s), a.dtype) for s, a in zip(shard_shapes, arrays)]
        self.scratch = [pltpu.SemaphoreType.DMA((7 * self.n,)), pltpu.SemaphoreType.DMA((7 * self.n,)),
                        pltpu.SemaphoreType.DMA((self.n,))]

    def plan(self, srcs, outs, sems):
        send_sems, recv_sems, local_sems = sems
        n, views = self.n, self.views
        x, y, c = _me()
        me = 4 * x + 2 * y + c

        def mine(a):
            return pltpu.make_async_copy(views[a](srcs[a], me), outs[a].at[me], local_sems.at[a])

        def copy(a, rel, sending):
            to, p = _peer(rel)
            return pltpu.make_async_remote_copy(
                src_ref=views[a](srcs[a], p), dst_ref=outs[a].at[me if sending else p],
                send_sem=send_sems.at[7 * a + rel - 1], recv_sem=recv_sems.at[7 * a + rel - 1],
                device_id=to if sending else (x, y, c), device_id_type=MESH)

        def start():
            for a in range(n):
                mine(a).start()
            for rel in range(1, 8):
                for a in range(n):
                    copy(a, rel, True).start()

        def forward():
            pass

        def finish():
            for rel in range(1, 8):
                for a in range(n):
                    copy(a, rel, False).wait_recv()
            for rel in range(1, 8):
                for a in range(n):
                    copy(a, rel, True).wait_send()
            for a in range(n):
                mine(a).wait()

        return start, forward, finish


def _launch(body, name, out_shape, in_specs, out_specs, operands, scratch=(), grid=None, semantics=None,
            carry=None, steps=None):
    out_shape, in_specs, out_specs = list(out_shape), list(in_specs), list(out_specs)
    operands, scratch = list(operands), list(scratch)
    n_in, n_out, n_scr = len(in_specs), len(out_shape), len(scratch)
    kernel_body = body
    if carry is not None:
        k = carry.n

        def kernel_body(*refs):
            ins, cin = refs[:n_in], refs[n_in:n_in + k]
            outs, cout = refs[n_in + k:n_in + k + n_out], refs[n_in + k + n_out:n_in + 2 * k + n_out]
            rest = refs[n_in + 2 * k + n_out:]
            scr, csem = rest[:n_scr], rest[n_scr:]
            start, forward, finish = carry.plan(cin, cout, csem)
            if steps is None:
                start()
                body(*ins, *outs, *scr)
                forward()
                finish()
            else:
                pl.when(steps()[0])(start)
                pl.when(steps()[1])(forward)
                body(*ins, *outs, *scr)
                pl.when(steps()[2])(finish)

        in_specs += [_HBM] * k
        out_shape += carry.out_shape
        out_specs += [_HBM] * k
        operands += carry.operands
        scratch += carry.scratch
    kwargs = {} if grid is None else {"grid": grid}
    params = pltpu.CompilerParams(vmem_limit_bytes=VMEM_LIMIT) if semantics is None else _params(*semantics)
    return pl.pallas_call(kernel_body, name=name, out_shape=out_shape, in_specs=in_specs, out_specs=out_specs,
                          scratch_shapes=scratch, compiler_params=params, **kwargs)(*operands)


def _grid_steps(nt):
    def steps():
        i = pl.program_id(0)
        return i == 0, i == nt - 1, i == nt - 1
    return steps


def _cast_shards(shards):
    n = len(shards)

    def body(*refs):
        for a in range(n):
            refs[n + a][...] = refs[a][...].astype(WIRE_DTYPE)

    return pl.pallas_call(body, name="cast_shards",
                          out_shape=[jax.ShapeDtypeStruct(s.shape, WIRE_DTYPE) for s in shards],
                          compiler_params=pltpu.CompilerParams(vmem_limit_bytes=VMEM_LIMIT))(*shards)


_SEM = pl.BlockSpec(memory_space=pltpu.SEMAPHORE)
_DATAFLOW = pltpu.SideEffectType.DATAFLOW_SIDE_EFFECTING


def _split_copy(arrays, views, landing, send_sems, recv_sems, a, rel):
    to, p = _peer(rel)
    x, y, c = _me()
    return pltpu.make_async_remote_copy(
        src_ref=views[a](arrays[a], p), dst_ref=landing[a].at[4 * x + 2 * y + c],
        send_sem=send_sems.at[NDEV * a + rel], recv_sem=recv_sems.at[NDEV * a + rel], device_id=to, device_id_type=MESH)


def _scatter_start(arrays, views, shard_shapes, after):
    n = len(arrays)
    landing = [pltpu.with_memory_space_constraint(lax.empty((NDEV,) + tuple(s), a.dtype), pltpu.HBM)
               for s, a in zip(shard_shapes, arrays)]
    arrays = [pltpu.with_memory_space_constraint(a, pltpu.HBM) for a in arrays]

    def body(*refs):
        ins, land = refs[:n], refs[n:2 * n]
        send_sems, recv_sems = refs[2 * n + len(after)], refs[2 * n + len(after) + 1]
        token = refs[-1]
        for rel in range(NDEV):
            for a in range(n):
                _split_copy(ins, views, land, send_sems, recv_sems, a, rel).start()
        token[...] = jnp.zeros_like(token)

    res = pl.pallas_call(
        body, name="scatter_start",
        out_shape=[pltpu.SemaphoreType.DMA((NDEV * n,)), pltpu.SemaphoreType.DMA((NDEV * n,))]
        + [pltpu.HBM(a.shape, a.dtype) for a in arrays] + [pltpu.HBM(l.shape, l.dtype) for l in landing]
        + [jax.ShapeDtypeStruct((8, 128), F32)],
        in_specs=[_HBM_ONLY] * (2 * n) + [_HBM] * len(after),
        out_specs=[_SEM, _SEM] + [_HBM_ONLY] * (2 * n) + [pl.BlockSpec(memory_space=pltpu.VMEM)],
        input_output_aliases={i: 2 + i for i in range(2 * n)},
        compiler_params=pltpu.CompilerParams(has_side_effects=_DATAFLOW),
    )(*arrays, *landing, *after)
    return res[0], res[1], res[2:2 + n], res[2 + n:2 + 2 * n], res[-1]


def _scatter_wait(send_sems, recv_sems, arrays, landing, views, after):
    n = len(arrays)

    def body(*refs):
        ins, land = refs[:n], refs[n:2 * n]
        send, recv = refs[2 * n], refs[2 * n + 1]
        for rel in range(NDEV):
            for a in range(n):
                cp = _split_copy(ins, views, land, send, recv, a, rel)
                cp.wait_send()
                cp.wait_recv()

    res = pl.pallas_call(
        body, name="scatter_wait",
        out_shape=[pltpu.HBM(a.shape, a.dtype) for a in arrays] + [pltpu.HBM(l.shape, l.dtype) for l in landing],
        in_specs=[_HBM_ONLY] * (2 * n) + [_SEM, _SEM] + [_HBM] * len(after),
        out_specs=[_HBM_ONLY] * (2 * n),
        input_output_aliases={i: i for i in range(2 * n)},
        compiler_params=pltpu.CompilerParams(has_side_effects=_DATAFLOW),
    )(*arrays, *landing, send_sems, recv_sems, *after)
    return res[n:]


def _ada_forward(c_row, w_ada, b_ada8, carry):
    cols = w_ada.shape[1]

    def body(c_ref, w_ref, b_ref, mod_ref, sc_ref, c_all, send_buf, recv_buf, send1, recv1, send2, recv2):
        x, y, c = _me()
        me = 4 * x + 2 * y + c
        rowi = lax.broadcasted_iota(jnp.int32, (8, D_MODEL), 0)
        c_all[me] = jnp.broadcast_to(c_ref[...], (8, D_MODEL))
        copies = []
        for rel in range(1, 8):
            to, _ = _peer(rel)
            cp = pltpu.make_async_remote_copy(src_ref=c_all.at[me], dst_ref=c_all.at[me], send_sem=send1.at[rel - 1],
                                              recv_sem=recv1.at[rel - 1], device_id=to, device_id_type=MESH)
            cp.start()
            copies.append(cp)
        for rel in range(1, 8):
            _, p = _peer(rel)
            pltpu.make_async_remote_copy(src_ref=c_all.at[p], dst_ref=c_all.at[p], send_sem=send1.at[rel - 1],
                                         recv_sem=recv1.at[rel - 1], device_id=(x, y, c), device_id_type=MESH).wait_recv()
        for cp in copies:
            cp.wait_send()
        cmat = jnp.zeros((8, D_MODEL), F32)
        for b in range(8):
            cmat = jnp.where(rowi == b, c_all[b], cmat)
        sc = cmat * _sigmoid(cmat)
        sc_ref[...] = sc
        modcols = _mm(sc, w_ref[...]) + b_ref[pl.ds(me, 1), :]
        for b in range(8):
            send_buf[b] = jnp.broadcast_to(modcols[b:b + 1, :], (8, cols))
        recv_buf[me] = send_buf[me]
        copies = []
        for rel in range(1, 8):
            to, p = _peer(rel)
            cp = pltpu.make_async_remote_copy(src_ref=send_buf.at[p], dst_ref=recv_buf.at[me], send_sem=send2.at[rel - 1],
                                              recv_sem=recv2.at[rel - 1], device_id=to, device_id_type=MESH)
            cp.start()
            copies.append(cp)
        for rel in range(1, 8):
            _, p = _peer(rel)
            pltpu.make_async_remote_copy(src_ref=send_buf.at[p], dst_ref=recv_buf.at[p], send_sem=send2.at[rel - 1],
                                         recv_sem=recv2.at[rel - 1], device_id=(x, y, c), device_id_type=MESH).wait_recv()
        for cp in copies:
            cp.wait_send()
        rowc = lax.broadcasted_iota(jnp.int32, (8, cols), 0)
        out = jnp.zeros((8, cols), F32)
        for k in range(8):
            out = jnp.where(rowc == k, recv_buf[k], out)
        mod_ref[...] = out

    return _launch(
        body, "ada_forward",
        out_shape=[jax.ShapeDtypeStruct((8, cols), F32), jax.ShapeDtypeStruct((8, D_MODEL), F32)],
        in_specs=[pl.BlockSpec(memory_space=pltpu.VMEM)] * 3,
        out_specs=[pl.BlockSpec(memory_space=pltpu.VMEM)] * 2,
        operands=(c_row, w_ada, b_ada8),
        scratch=[pltpu.VMEM((8, 8, D_MODEL), F32), pltpu.VMEM((8, 8, cols), F32), pltpu.VMEM((8, 8, cols), F32)]
        + [pltpu.SemaphoreType.DMA((7,))] * 4,
        carry=carry)


def _allreduce_small(pack):
    rows = pack.shape[1]

    def body(pack_ref, total_ref, land_ref, send1, recv1, send2, recv2):
        x, y, c = _me()
        me = 4 * x + 2 * y + c
        land_ref[me] = pack_ref[me]
        copies = []
        for rel in range(1, 8):
            to, p = _peer(rel)
            cp = pltpu.make_async_remote_copy(src_ref=pack_ref.at[p], dst_ref=land_ref.at[me], send_sem=send1.at[rel - 1],
                                              recv_sem=recv1.at[rel - 1], device_id=to, device_id_type=MESH)
            cp.start()
            copies.append(cp)
        for rel in range(1, 8):
            _, p = _peer(rel)
            pltpu.make_async_remote_copy(src_ref=pack_ref.at[p], dst_ref=land_ref.at[p], send_sem=send1.at[rel - 1],
                                         recv_sem=recv1.at[rel - 1], device_id=(x, y, c), device_id_type=MESH).wait_recv()
        for cp in copies:
            cp.wait_send()
        acc = land_ref[0]
        for b in range(1, 8):
            acc = acc + land_ref[b]
        total_ref[me] = acc
        copies = []
        for rel in range(1, 8):
            to, _ = _peer(rel)
            cp = pltpu.make_async_remote_copy(src_ref=total_ref.at[me], dst_ref=total_ref.at[me], send_sem=send2.at[rel - 1],
                                              recv_sem=recv2.at[rel - 1], device_id=to, device_id_type=MESH)
            cp.start()
            copies.append(cp)
        for rel in range(1, 8):
            _, p = _peer(rel)
            pltpu.make_async_remote_copy(src_ref=total_ref.at[p], dst_ref=total_ref.at[p], send_sem=send2.at[rel - 1],
                                         recv_sem=recv2.at[rel - 1], device_id=(x, y, c), device_id_type=MESH).wait_recv()
        for cp in copies:
            cp.wait_send()

    return pl.pallas_call(
        body, name="allreduce_small",
        out_shape=[jax.ShapeDtypeStruct((8, rows, 128), F32), jax.ShapeDtypeStruct((8, rows, 128), F32)],
        in_specs=[pl.BlockSpec(memory_space=pltpu.VMEM)],
        out_specs=[pl.BlockSpec(memory_space=pltpu.VMEM)] * 2,
        scratch_shapes=[pltpu.SemaphoreType.DMA((7,))] * 4,
        compiler_params=pltpu.CompilerParams(vmem_limit_bytes=VMEM_LIMIT),
    )(pack)


def _modulated(x, prm_ref, sub, g_row):
    shift, scale = _row(prm_ref, 3 * sub), _row(prm_ref, 3 * sub + 1)
    g = _row(prm_ref, g_row)
    r = _rms_scale(x)
    n0 = x * r
    return (n0 * g) * (1.0 + scale) + shift, r, n0


def _ffn_forward(x, prm, win, wout, sub, g_row, name, carry=None):
    T = x.shape[0]
    tm = min(T, TM_FFN)

    def body(x_ref, prm_ref, win_ref, wout_ref, xo_ref, f_ref, ab_ref):
        xv = x_ref[...]
        h, _, _ = _modulated(xv, prm_ref, sub, g_row)
        hb = h.astype(MXU_DTYPE)
        acc = None
        for j in range(4):
            a = _mm_nt(hb, win_ref[0, j])
            b = _mm_nt(hb, win_ref[1, j])
            ab_ref[0, j] = a.astype(SAVE_DTYPE)
            ab_ref[1, j] = b.astype(SAVE_DTYPE)
            s = (a * _sigmoid(a)) * b
            t = _mm(s, wout_ref[j])
            acc = t if acc is None else acc + t
        f_ref[...] = acc.astype(SAVE_DTYPE)
        xo_ref[...] = xv + (0.5 * _row(prm_ref, 3 * sub + 2)) * acc

    tok = pl.BlockSpec((tm, D_MODEL), lambda i: (i, 0))
    return _launch(
        body, name, grid=(T // tm,), semantics=("arbitrary",),
        out_shape=[jax.ShapeDtypeStruct((T, D_MODEL), F32), jax.ShapeDtypeStruct((T, D_MODEL), SAVE_DTYPE),
                   jax.ShapeDtypeStruct((2, 4, T, FF_SHARD), SAVE_DTYPE)],
        in_specs=[tok, _resident(prm), _resident(win), _resident(wout)],
        out_specs=[tok, tok, pl.BlockSpec((2, 4, tm, FF_SHARD), lambda i: (0, 0, i, 0))],
        operands=(x, prm, win, wout), carry=carry, steps=_grid_steps(T // tm))


def _ffn_hidden(x, prm, win, sub, g_row, name, carry=None):
    T = x.shape[0]
    tm = min(T, TM_FFN)

    def body(x_ref, prm_ref, win_ref, ab_ref, s_ref):
        h, _, _ = _modulated(x_ref[...], prm_ref, sub, g_row)
        hb = h.astype(MXU_DTYPE)
        for j in range(4):
            a = _mm_nt(hb, win_ref[0, j])
            b = _mm_nt(hb, win_ref[1, j])
            ab_ref[0, j] = a.astype(SAVE_DTYPE)
            ab_ref[1, j] = b.astype(SAVE_DTYPE)
            s_ref[j] = ((a * _sigmoid(a)) * b).astype(MXU_DTYPE)

    return _launch(
        body, name, grid=(T // tm,), semantics=("arbitrary",),
        out_shape=[jax.ShapeDtypeStruct((2, 4, T, FF_SHARD), SAVE_DTYPE), jax.ShapeDtypeStruct((4, T, FF_SHARD), MXU_DTYPE)],
        in_specs=[pl.BlockSpec((tm, D_MODEL), lambda i: (i, 0)), _resident(prm), _resident(win)],
        out_specs=[pl.BlockSpec((2, 4, tm, FF_SHARD), lambda i: (0, 0, i, 0)),
                   pl.BlockSpec((4, tm, FF_SHARD), lambda i: (0, i, 0))],
        operands=(x, prm, win), carry=carry, steps=_grid_steps(T // tm))


def _ffn_out(x, s, prm, wout, sub, name, carry=None):
    T = x.shape[0]
    tm = min(T, TM_FFN)

    def body(x_ref, s_ref, prm_ref, wout_ref, xo_ref, f_ref):
        acc = None
        for j in range(4):
            t = _mm(s_ref[j], wout_ref[j])
            acc = t if acc is None else acc + t
        f_ref[...] = acc.astype(SAVE_DTYPE)
        xo_ref[...] = x_ref[...] + (0.5 * _row(prm_ref, 3 * sub + 2)) * acc

    tok = pl.BlockSpec((tm, D_MODEL), lambda i: (i, 0))
    return _launch(
        body, name, grid=(T // tm,), semantics=("arbitrary",),
        out_shape=[jax.ShapeDtypeStruct((T, D_MODEL), F32), jax.ShapeDtypeStruct((T, D_MODEL), SAVE_DTYPE)],
        in_specs=[tok, pl.BlockSpec((4, tm, FF_SHARD), lambda i: (0, i, 0)), _resident(prm), _resident(wout)],
        out_specs=[tok, tok], operands=(x, s, prm, wout), carry=carry, steps=_grid_steps(T // tm))


def _ffn_backward(x, d, ab, prm, win, wout, sub, g_row, name, carry=None):
    T = x.shape[0]
    tm = min(T, TM_FFN)
    nt = T // tm
    chunk = min(tm, FFN_BWD_CHUNK)

    def body(x_ref, d_ref, ab_ref, prm_ref, win_ref, wout_ref, dh_ref, dwin_ref, dwout_ref, acc_in, acc_out):
        i = pl.program_id(1)

        @pl.when(i == 0)
        def _():
            acc_in[...] = jnp.zeros_like(acc_in)
            acc_out[...] = jnp.zeros_like(acc_out)

        wa, wb, wo = win_ref[0, 0], win_ref[1, 0], wout_ref[0]
        half_gate = 0.5 * _row(prm_ref, 3 * sub + 2)
        das, dbs, ss, hbs, dfss = [], [], [], [], []
        for ck in range(tm // chunk):
            rows = slice(ck * chunk, (ck + 1) * chunk)
            h, _, _ = _modulated(x_ref[rows, :], prm_ref, sub, g_row)
            hbs.append(h.astype(MXU_DTYPE))
            a = ab_ref[0, 0, rows, :].astype(F32)
            b = ab_ref[1, 0, rows, :].astype(F32)
            sg = _sigmoid(a)
            si = a * sg
            dfs = (half_gate * d_ref[rows, :]).astype(MXU_DTYPE)
            ds = _mm_nt(dfs, wo)
            da = (ds * b * (sg * (1.0 + a * (1.0 - sg)))).astype(MXU_DTYPE)
            db = (ds * si).astype(MXU_DTYPE)
            dh_ref[0, rows, :] = (_mm(da, wa) + _mm(db, wb)).astype(SAVE_DTYPE)
            das.append(da)
            dbs.append(db)
            ss.append((si * b).astype(MXU_DTYPE))
            dfss.append(dfs)
        cat = (lambda v: v[0]) if len(das) == 1 else (lambda v: jnp.concatenate(v, axis=0))
        hb = cat(hbs)
        acc_out[...] += _mm_tn(cat(ss), cat(dfss))
        acc_in[0] += _mm_tn(cat(das), hb)
        acc_in[1] += _mm_tn(cat(dbs), hb)

        @pl.when(i == nt - 1)
        def _():
            dwin_ref[0, 0] = acc_in[0].astype(WIRE_DTYPE)
            dwin_ref[1, 0] = acc_in[1].astype(WIRE_DTYPE)
            dwout_ref[0] = acc_out[...].astype(WIRE_DTYPE)

    def steps():
        j, i = pl.program_id(0), pl.program_id(1)
        return (j == 0) & (i == 0), (j == 2) & (i == 0), (j == 3) & (i == nt - 1)

    tok = pl.BlockSpec((tm, D_MODEL), lambda j, i: (i, 0))
    return _launch(
        body, name, grid=(4, nt), semantics=("arbitrary", "arbitrary"),
        out_shape=[jax.ShapeDtypeStruct((4, T, D_MODEL), SAVE_DTYPE),
                   jax.ShapeDtypeStruct(win.shape, WIRE_DTYPE), jax.ShapeDtypeStruct(wout.shape, WIRE_DTYPE)],
        in_specs=[tok, tok, pl.BlockSpec((2, 1, tm, FF_SHARD), lambda j, i: (0, j, i, 0)), _resident(prm),
                  pl.BlockSpec((2, 1, FF_SHARD, D_MODEL), lambda j, i: (0, j, 0, 0)),
                  pl.BlockSpec((1, FF_SHARD, D_MODEL), lambda j, i: (j, 0, 0))],
        out_specs=[pl.BlockSpec((1, tm, D_MODEL), lambda j, i: (j, i, 0)),
                   pl.BlockSpec((2, 1, FF_SHARD, D_MODEL), lambda j, i: (0, j, 0, 0)),
                   pl.BlockSpec((1, FF_SHARD, D_MODEL), lambda j, i: (j, 0, 0))],
        operands=(x, d, ab, prm, win, wout),
        scratch=[pltpu.VMEM((2, FF_SHARD, D_MODEL), F32), pltpu.VMEM((FF_SHARD, D_MODEL), F32)],
        carry=carry, steps=steps)


def _norm_backward(parts, x, d, f, prm, sub, g_row, gate_coef, name, carry=None):
    T = x.shape[0]
    tm = min(T, TM_EW)
    P = parts.shape[0]

    def body(p_ref, x_ref, d_ref, f_ref, prm_ref, dx_ref, sums_ref):
        i = pl.program_id(0)
        dh = p_ref[0].astype(F32)
        for k in range(1, P):
            dh = dh + p_ref[k].astype(F32)
        xv, dv = x_ref[...], d_ref[...]
        scale, g = _row(prm_ref, 3 * sub + 1), _row(prm_ref, g_row)
        r = _rms_scale(xv)
        n0 = xv * r
        dn = dh * (1.0 + scale)
        dn0 = dn * g
        dx_ref[...] = dv + r * (dn0 - n0 * jnp.mean(dn0 * n0, axis=-1, keepdims=True))
        upd = jnp.concatenate([_colsum(dn * n0), _colsum(dh), _colsum(dh * (n0 * g)),
                               gate_coef * _colsum(dv * f_ref[...].astype(F32)), jnp.zeros((4, D_MODEL), F32)], axis=0)

        @pl.when(i == 0)
        def _():
            sums_ref[...] = upd

        @pl.when(i > 0)
        def _():
            sums_ref[...] += upd

    tok = pl.BlockSpec((tm, D_MODEL), lambda i: (i, 0))
    return _launch(
        body, name, grid=(T // tm,), semantics=("arbitrary",),
        out_shape=[jax.ShapeDtypeStruct((T, D_MODEL), F32), jax.ShapeDtypeStruct((8, D_MODEL), F32)],
        in_specs=[pl.BlockSpec((P, tm, D_MODEL), lambda i: (0, i, 0)), tok, tok, tok, _resident(prm)],
        out_specs=[tok, pl.BlockSpec((8, D_MODEL), lambda i: (0, 0))],
        operands=(parts, x, d, f, prm), carry=carry, steps=_grid_steps(T // tm))


def _final_loss(x, target, prm):
    T = x.shape[0]
    tm = min(T, TM_EW)

    def body(x_ref, t_ref, prm_ref, dx_ref, sums_ref):
        i = pl.program_id(0)
        xv = x_ref[...]
        g = _row(prm_ref, ROW_G_FINAL)
        r = _rms_scale(xv)
        n0 = xv * r
        err = n0 * g - t_ref[...]
        dy = err / float(D_MODEL)
        dn0 = dy * g
        dx_ref[...] = r * (dn0 - n0 * jnp.mean(dn0 * n0, axis=-1, keepdims=True))
        loss = 0.5 * jnp.sum(jnp.mean(err * err, axis=-1, keepdims=True), axis=0, keepdims=True)
        upd = jnp.concatenate([_colsum(dy * n0), jnp.broadcast_to(loss, (1, D_MODEL)), jnp.zeros((6, D_MODEL), F32)], axis=0)

        @pl.when(i == 0)
        def _():
            sums_ref[...] = upd

        @pl.when(i > 0)
        def _():
            sums_ref[...] += upd

    tok = pl.BlockSpec((tm, D_MODEL), lambda i: (i, 0))
    return pl.pallas_call(
        body, name="final_loss", grid=(T // tm,),
        out_shape=[jax.ShapeDtypeStruct((T, D_MODEL), F32), jax.ShapeDtypeStruct((8, D_MODEL), F32)],
        in_specs=[tok, tok, pl.BlockSpec(prm.shape, lambda i: (0, 0))],
        out_specs=[tok, pl.BlockSpec((8, D_MODEL), lambda i: (0, 0))],
        compiler_params=_params("arbitrary"),
    )(x, target, prm)


def _ssm_discretise(lam_re_log, lam_im, log_dt):
    lr = -jnp.exp(lam_re_log)
    dt = jnp.exp(log_dt)
    mag = jnp.exp(lr * dt)
    ang = lam_im * dt
    ab_re = mag * jnp.cos(ang)
    ab_im = mag * jnp.sin(ang)
    num_re = ab_re - 1.0
    num_im = ab_im
    den = lr * lr + lam_im * lam_im
    f_re = (num_re * lr + num_im * lam_im) / den
    f_im = (num_im * lr - num_re * lam_im) / den
    return ab_re, ab_im, f_re, f_im


def _ssm_params_forward(lam_re_log, lam_im, log_dt):
    def body(a_ref, b_ref, c_ref, o0, o1, o2, o3):
        outs = _ssm_discretise(a_ref[...], b_ref[...], c_ref[...])
        for o, v in zip((o0, o1, o2, o3), outs):
            o[...] = v

    return pl.pallas_call(body, name="ssm_params_forward",
                          out_shape=[jax.ShapeDtypeStruct(lam_im.shape, F32)] * 4)(lam_re_log, lam_im, log_dt)


def _ssm_params_backward(lam_re_log, lam_im, log_dt, cot):
    def body(a_ref, b_ref, c_ref, g0, g1, g2, g3, o0, o1, o2):
        _, vjp = jax.vjp(_ssm_discretise, a_ref[...], b_ref[...], c_ref[...])
        d0, d1, d2 = vjp((g0[...], g1[...], g2[...], g3[...]))
        o0[...] = d0
        o1[...] = d1
        o2[...] = d2

    return pl.pallas_call(
        body, name="ssm_params_backward",
        out_shape=[jax.ShapeDtypeStruct(lam_im.shape, F32), jax.ShapeDtypeStruct(lam_im.shape, F32),
                   jax.ShapeDtypeStruct(log_dt.shape, F32)])(lam_re_log, lam_im, log_dt, *cot)


def _ssm_dense_forward(srow, b_dense, c_dense):
    def body(srow_ref, bd_ref, cd_ref, bb_ref, ct_ref):
        for j in range(SSM_BLOCKS):
            lanes = slice(j * SSM_BLOCK_STATE, (j + 1) * SSM_BLOCK_STATE)
            f_re, f_im = srow_ref[2:3, lanes], srow_ref[3:4, lanes]
            bb_ref[0, j] = (f_re * bd_ref[0, j] - f_im * bd_ref[1, j]).astype(MXU_DTYPE)
            bb_ref[1, j] = (f_re * bd_ref[1, j] + f_im * bd_ref[0, j]).astype(MXU_DTYPE)
            ct_ref[0, j] = cd_ref[0, j].astype(MXU_DTYPE)
            ct_ref[1, j] = cd_ref[1, j].astype(MXU_DTYPE)

    return pl.pallas_call(body, name="ssm_dense_forward",
                          out_shape=[jax.ShapeDtypeStruct(b_dense.shape, MXU_DTYPE),
                                     jax.ShapeDtypeStruct(c_dense.shape, MXU_DTYPE)],
                          compiler_params=pltpu.CompilerParams(vmem_limit_bytes=VMEM_LIMIT))(srow, b_dense, c_dense)


def _cmul(p, q):
    return p[0] * q[0] - p[1] * q[1], p[0] * q[1] + p[1] * q[0]


def _scan_coefficients(ar, ai, reverse):
    n = ar.shape[1]
    p = {1: (ar, ai)}
    p[2] = _cmul(p[1], p[1])
    p[3] = _cmul(p[2], p[1])
    p[4] = _cmul(p[2], p[2])
    p[5] = _cmul(p[4], p[1])
    p[6] = _cmul(p[4], p[2])
    p[7] = _cmul(p[4], p[3])
    p[8] = _cmul(p[4], p[4])
    rowi = lax.broadcasted_iota(jnp.int32, (SCAN_ROWS, n), 0)
    tiles = []
    for dstep in (1, 2, 4):
        keep = (rowi < SCAN_ROWS - dstep) if reverse else (rowi >= dstep)
        for part in p[dstep]:
            tiles.append(jnp.where(keep, jnp.broadcast_to(part, (SCAN_ROWS, n)), 0.0))
    for comp in (0, 1):
        t = jnp.zeros((SCAN_ROWS, n), F32)
        for rr in range(SCAN_ROWS):
            power = SCAN_ROWS - rr if reverse else rr + 1
            t = jnp.where(rowi == rr, jnp.broadcast_to(p[power][comp], (SCAN_ROWS, n)), t)
        tiles.append(t)
    return tiles


def _load_stack(stack_hbm, dst, sems, base):
    cols = stack_hbm.shape[2]
    cps = [pltpu.make_async_copy(stack_hbm.at[k], dst.at[:, pl.ds(k * cols, cols)], sems.at[base + k])
           for k in range(NDEV)]
    for cp in cps:
        cp.start()
    return cps


def _window_lanes():
    lane = lax.broadcasted_iota(jnp.int32, (1, POOL_WIDTH), 1)
    return jnp.where(lane < 128, 2.0, jnp.where(lane < 256, 4.0, jnp.where(lane < 384, 8.0, 16.0)))


def _gelu(y):
    return 0.5 * y * (1.0 + lax.erf(y * 0.7071067811865476))


def _gelu_grad(y):
    return 0.5 * (1.0 + lax.erf(y * 0.7071067811865476)) + y * jnp.exp(-0.5 * y * y) * 0.3989422804014327


def _mixer_forward(x, prm, w_in_s, w_pu_s, w_glu_s, w_su_s, w_out, pool_w, mvec, srow, bb, ct, carry=None):
    T = x.shape[0]
    tm = min(T, TM_MIX)
    nt = T // tm
    n_tiles = tm // SCAN_ROWS

    def body(x_ref, prm_ref, w_in_h, w_pu_h, w_glu_h, w_su_h, w_out_h, pw_ref, mv_ref, srow_ref, bb, ct,
             x2_ref, mo_ref, z_ref, sre_ref, sim_ref, zp_ref, q_ref, yp_ref, yss_ref, vg_ref, ys_ref,
             w_in, w_pu, w_glu, w_su, w_o, coef, carry, hist, bu, sems):
        i = pl.program_id(0)

        @pl.when(i == 0)
        def _():
            cps = (_load_stack(w_in_h, w_in, sems, 0) + _load_stack(w_pu_h, w_pu, sems, 8)
                   + _load_stack(w_glu_h, w_glu, sems, 16) + _load_stack(w_su_h, w_su, sems, 24))
            cps.append(pltpu.make_async_copy(w_out_h, w_o, sems.at[32]))
            cps[-1].start()
            for j in range(SSM_BLOCKS):
                lanes = slice(j * SSM_BLOCK_STATE, (j + 1) * SSM_BLOCK_STATE)
                for k, t in enumerate(_scan_coefficients(srow_ref[0:1, lanes], srow_ref[1:2, lanes], False)):
                    coef[j, k] = t
            carry[...] = jnp.zeros_like(carry)
            hist[...] = jnp.zeros_like(hist)
            for cp in cps:
                cp.wait()

        xv = x_ref[...]
        h, _, _ = _modulated(xv, prm_ref, 1, ROW_G_MIX)
        z = _mm(h, w_in[...])
        z_ref[...] = z.astype(SAVE_DTYPE)
        u_pool, u_ssm = z[:, 0:512], z[:, 512:1024]
        gl_pool, gl_ssm = z[:, 1024:2048], z[:, 2048:3072]

        ext = jnp.concatenate([hist[...], u_pool], axis=0)
        w2 = ext + pltpu.roll(ext, 1, 0)
        w4 = w2[:, 128:] + pltpu.roll(w2[:, 128:], 2, 0)
        w8 = w4[:, 128:] + pltpu.roll(w4[:, 128:], 4, 0)
        w16 = w8[:, 128:] + pltpu.roll(w8[:, 128:], 8, 0)
        wsum = jnp.concatenate([w2[POOL_HALO:, :128], w4[POOL_HALO:, :128], w8[POOL_HALO:, :128], w16[POOL_HALO:]], axis=1)
        hist[...] = u_pool[tm - POOL_HALO:, :]
        t1 = (lax.broadcasted_iota(jnp.int32, (tm, 1), 0) + (i * tm + 1)).astype(F32)
        zp = wsum / jnp.minimum(t1, _window_lanes()) - u_pool
        zp_ref[...] = zp.astype(SAVE_DTYPE)
        q = jnp.concatenate([_mm(zp[:, k * 128:(k + 1) * 128], pw_ref[k]) for k in range(4)], axis=1)
        q = q + mv_ref[ROW_POOL_B:ROW_POOL_B + 1, 0:512]
        q_ref[...] = q.astype(SAVE_DTYPE)
        y_pool = _mm(q * mv_ref[ROW_POOL_SCALE:ROW_POOL_SCALE + 1, 0:512], w_pu[...])
        yp_ref[...] = y_pool.astype(SAVE_DTYPE)

        y_blocks = []
        for j in range(SSM_BLOCKS):
            lanes = pl.ds(j * SSM_BLOCK_STATE, SSM_BLOCK_STATE)
            ub = u_ssm[:, j * 128:(j + 1) * 128].astype(MXU_DTYPE)
            bu[0] = _mm(ub, bb[0, j])
            bu[1] = _mm(ub, bb[1, j])
            a1r, a1i, a2r, a2i, a4r, a4i, pr, pi = [coef[j, k] for k in range(8)]

            def step(tt, c, lanes=lanes, a1r=a1r, a1i=a1i, a2r=a2r, a2i=a2i, a4r=a4r, a4i=a4i, pr=pr, pi=pi):
                cr, ci = c
                rows = pl.ds(pl.multiple_of(tt * SCAN_ROWS, SCAN_ROWS), SCAN_ROWS)
                xr, xi = bu[0, rows, :], bu[1, rows, :]
                for dstep, kr, ki in ((1, a1r, a1i), (2, a2r, a2i), (4, a4r, a4i)):
                    sr, si = pltpu.roll(xr, dstep, 0), pltpu.roll(xi, dstep, 0)
                    xr, xi = xr + kr * sr - ki * si, xi + kr * si + ki * sr
                xr, xi = xr + pr * cr - pi * ci, xi + pr * ci + pi * cr
                sre_ref[rows, lanes] = xr
                sim_ref[rows, lanes] = xi
                return (jnp.broadcast_to(xr[SCAN_ROWS - 1:SCAN_ROWS, :], xr.shape),
                        jnp.broadcast_to(xi[SCAN_ROWS - 1:SCAN_ROWS, :], xi.shape))

            cr, ci = lax.fori_loop(0, n_tiles, step, (carry[j, 0], carry[j, 1]))
            carry[j, 0] = cr
            carry[j, 1] = ci
            y_blocks.append(_mm(sre_ref[:, lanes], ct[0, j]) - _mm(sim_ref[:, lanes], ct[1, j]))
        yss = jnp.concatenate(y_blocks, axis=1) + mv_ref[ROW_SSM_D:ROW_SSM_D + 1, 0:512] * u_ssm
        yss_ref[...] = yss.astype(SAVE_DTYPE)
        vg = _mm(_gelu(yss), w_glu[...]) + mv_ref[ROW_B_GLU:ROW_B_GLU + 1, :]
        vg_ref[...] = vg.astype(SAVE_DTYPE)
        y_ssm = _mm(vg[:, 0:512] * _sigmoid(vg[:, 512:1024]), w_su[...])
        ys_ref[...] = y_ssm.astype(SAVE_DTYPE)

        merged = _sigmoid(gl_pool) * y_pool + _sigmoid(gl_ssm) * y_ssm
        mo = _mm(merged, w_o[...])
        mo_ref[...] = mo.astype(SAVE_DTYPE)
        x2_ref[...] = xv + _row(prm_ref, 5) * mo

    def tok(width):
        return pl.BlockSpec((tm, width), lambda i: (i, 0))

    hbm = _HBM
    widths = (D_MODEL, D_MODEL, IN_WIDTH, N_STATE, N_STATE, 512, 512, D_MODEL, 512, D_MODEL, D_MODEL)
    dtypes = (F32, SAVE_DTYPE, SAVE_DTYPE, F32, F32) + (SAVE_DTYPE,) * 6
    return _launch(
        body, "mixer_forward", grid=(nt,), semantics=("arbitrary",), carry=carry, steps=_grid_steps(nt),
        out_shape=[jax.ShapeDtypeStruct((T, w), dt) for w, dt in zip(widths, dtypes)],
        in_specs=[tok(D_MODEL), _resident(prm), hbm, hbm, hbm, hbm, hbm, _resident(pool_w), _resident(mvec),
                  _resident(srow), _resident(bb), _resident(ct)],
        out_specs=[tok(w) for w in widths],
        operands=(x, prm, w_in_s, w_pu_s, w_glu_s, w_su_s, w_out, pool_w, mvec, srow, bb, ct),
        scratch=[
            pltpu.VMEM((D_MODEL, IN_WIDTH), MXU_DTYPE), pltpu.VMEM((512, D_MODEL), MXU_DTYPE),
            pltpu.VMEM((512, D_MODEL), MXU_DTYPE), pltpu.VMEM((512, D_MODEL), MXU_DTYPE),
            pltpu.VMEM((D_MODEL, D_MODEL), MXU_DTYPE),
            pltpu.VMEM((SSM_BLOCKS, 8, SCAN_ROWS, SSM_BLOCK_STATE), F32),
            pltpu.VMEM((SSM_BLOCKS, 2, SCAN_ROWS, SSM_BLOCK_STATE), F32),
            pltpu.VMEM((POOL_HALO, POOL_WIDTH), F32),
            pltpu.VMEM((2, tm, SSM_BLOCK_STATE), F32),
            pltpu.SemaphoreType.DMA((33,)),
        ])


def _mixer_backward(d2, prm, saved, w_pu_s, w_glu_s, w_su_s, w_out, pool_w, mvec, srow, bb, ct, carry=None):
    z, s_re, s_im, zp, q, y_pool, yss, vg, y_ssm = saved
    T = d2.shape[0]
    tm = min(T, TM_MIX_BWD)
    nt = T // tm
    n_tiles = tm // SCAN_ROWS

    def body(d_ref, prm_ref, z_ref, sre_ref, sim_ref, zp_ref, q_ref, yp_ref, yss_ref, vg_ref, ys_ref,
             w_pu_h, w_glu_h, w_su_h, w_out_h, pw_ref, mv_ref, srow_ref, bb, ct,
             dz_ref, dwo_h, dwpu_h, dwglu_h, dwsu_h, dpw_h, dbb_h, dct_h, vsum_h, da_h,
             w_pu, w_glu, w_su, w_o, pwb, coef, carry, hist, dre, lam,
             a_wo, a_wpu, a_wglu, a_wsu, a_pw, a_bb, a_ct, a_vs, a_da, st_wo, st_up, sems):
        i = pl.program_id(0)
        tile = nt - 1 - i

        @pl.when(i == 0)
        def _():
            cps = (_load_stack(w_pu_h, w_pu, sems, 0) + _load_stack(w_glu_h, w_glu, sems, 8)
                   + _load_stack(w_su_h, w_su, sems, 16))
            cps.append(pltpu.make_async_copy(w_out_h, w_o, sems.at[24]))
            cps[-1].start()
            pwb[...] = pw_ref[...].astype(MXU_DTYPE)
            for j in range(SSM_BLOCKS):
                lanes = slice(j * SSM_BLOCK_STATE, (j + 1) * SSM_BLOCK_STATE)
                for k, t in enumerate(_scan_coefficients(srow_ref[0:1, lanes], srow_ref[1:2, lanes], True)):
                    coef[j, k] = t
            for acc in (carry, hist, a_wo, a_wpu, a_wglu, a_wsu, a_pw, a_bb, a_ct, a_vs, a_da):
                acc[...] = jnp.zeros_like(acc)
            for cp in cps:
                cp.wait()

        dv = d_ref[...]
        zt = z_ref[...].astype(F32)
        u_ssm, gl_pool, gl_ssm = zt[:, 512:1024], zt[:, 1024:2048], zt[:, 2048:3072]
        y_p, y_s = yp_ref[...].astype(F32), ys_ref[...].astype(F32)
        sgp, sgs = _sigmoid(gl_pool), _sigmoid(gl_ssm)
        dmo = (_row(prm_ref, 5) * dv).astype(MXU_DTYPE)
        a_wo[...] += _mm_tn(sgp * y_p + sgs * y_s, dmo)
        dmerged = _mm_nt(dmo, w_o[...])
        dy_pool = dmerged * sgp
        dgl_pool = dmerged * y_p * (sgp * (1.0 - sgp))
        dy_ssm = dmerged * sgs
        dgl_ssm = dmerged * y_s * (sgs * (1.0 - sgs))

        scale = mv_ref[ROW_POOL_SCALE:ROW_POOL_SCALE + 1, 0:512]
        qv, zpv = q_ref[...].astype(F32), zp_ref[...]
        a_wpu[...] += _mm_tn(qv * scale, dy_pool)
        dp = _mm_nt(dy_pool, w_pu[...])
        dq = dp * scale
        a_vs[0:1, 0:512] += _colsum(dp * qv)
        a_vs[1:2, 0:512] += _colsum(dq)
        dzp_blocks = []
        for k in range(4):
            lanes = slice(k * 128, (k + 1) * 128)
            dzp_blocks.append(_mm_nt(dq[:, lanes], pwb[k]))
            a_pw[k] += _mm_tn(zpv[:, lanes], dq[:, lanes])
        dzp = jnp.concatenate(dzp_blocks, axis=1)
        t1 = (lax.broadcasted_iota(jnp.int32, (tm, 1), 0) + (tile * tm + 1)).astype(F32)
        gs = dzp / jnp.minimum(t1, _window_lanes())
        n_ext = tm + POOL_HALO
        ext = jnp.concatenate([gs, hist[...]], axis=0)
        v2 = ext + pltpu.roll(ext, n_ext - 1, 0)
        v4 = v2[:, 128:] + pltpu.roll(v2[:, 128:], n_ext - 2, 0)
        v8 = v4[:, 128:] + pltpu.roll(v4[:, 128:], n_ext - 4, 0)
        v16 = v8[:, 128:] + pltpu.roll(v8[:, 128:], n_ext - 8, 0)
        msum = jnp.concatenate([v2[:tm, :128], v4[:tm, :128], v8[:tm, :128], v16[:tm]], axis=1)
        hist[...] = gs[0:POOL_HALO, :]
        du_pool = msum - dzp

        vgv = vg_ref[...].astype(F32)
        val, gate = vgv[:, 0:512], vgv[:, 512:1024]
        sgg = _sigmoid(gate)
        a_wsu[...] += _mm_tn(val * sgg, dy_ssm)
        do = _mm_nt(dy_ssm, w_su[...])
        dvg = jnp.concatenate([do * sgg, do * val * (sgg * (1.0 - sgg))], axis=1)
        a_vs[3:4, :] += _colsum(dvg)
        yv = yss_ref[...].astype(F32)
        a_wglu[...] += _mm_tn(_gelu(yv), dvg)
        dyss = _mm_nt(dvg, w_glu[...]) * _gelu_grad(yv)
        a_vs[2:3, 0:512] += _colsum(dyss * u_ssm)
        du_blocks = []
        for j in range(SSM_BLOCKS):
            lanes = pl.ds(j * SSM_BLOCK_STATE, SSM_BLOCK_STATE)
            in_lanes = slice(j * 128, (j + 1) * 128)
            dyb = dyss[:, in_lanes].astype(MXU_DTYPE)
            ub = u_ssm[:, in_lanes].astype(MXU_DTYPE)
            dre[0] = _mm_nt(dyb, ct[0, j])
            dre[1] = -_mm_nt(dyb, ct[1, j])
            a_ct[0, j] += _mm_tn(sre_ref[:, lanes], dyb)
            a_ct[1, j] -= _mm_tn(sim_ref[:, lanes], dyb)
            a1r, a1i, a2r, a2i, a4r, a4i, pr, pi = [coef[j, k] for k in range(8)]
            rowi = lax.broadcasted_iota(jnp.int32, (SCAN_ROWS, SSM_BLOCK_STATE), 0)

            def step(tt, c, lanes=lanes, a1r=a1r, a1i=a1i, a2r=a2r, a2i=a2i, a4r=a4r, a4i=a4i, pr=pr, pi=pi, rowi=rowi):
                cr, ci, acc_r, acc_i = c
                rows = pl.ds(pl.multiple_of((n_tiles - 1 - tt) * SCAN_ROWS, SCAN_ROWS), SCAN_ROWS)
                xr, xi = dre[0, rows, :], dre[1, rows, :]
                for dstep, kr, ki in ((1, a1r, a1i), (2, a2r, a2i), (4, a4r, a4i)):
                    sr, si = pltpu.roll(xr, SCAN_ROWS - dstep, 0), pltpu.roll(xi, SCAN_ROWS - dstep, 0)
                    xr, xi = xr + kr * sr + ki * si, xi + kr * si - ki * sr
                xr, xi = xr + pr * cr + pi * ci, xi + pr * ci - pi * cr
                lam[0, rows, :] = xr
                lam[1, rows, :] = xi
                nr = jnp.where(rowi == SCAN_ROWS - 1, cr, pltpu.roll(xr, SCAN_ROWS - 1, 0))
                ni = jnp.where(rowi == SCAN_ROWS - 1, ci, pltpu.roll(xi, SCAN_ROWS - 1, 0))
                s_r, s_i = sre_ref[rows, lanes], sim_ref[rows, lanes]
                acc_r = acc_r + nr * s_r + ni * s_i
                acc_i = acc_i + ni * s_r - nr * s_i
                return (jnp.broadcast_to(xr[0:1, :], xr.shape), jnp.broadcast_to(xi[0:1, :], xi.shape), acc_r, acc_i)

            cr, ci, acc_r, acc_i = lax.fori_loop(0, n_tiles, step, (carry[j, 0], carry[j, 1], a_da[0, j], a_da[1, j]))
            carry[j, 0] = cr
            carry[j, 1] = ci
            a_da[0, j] = acc_r
            a_da[1, j] = acc_i
            lr_b, li_b = lam[0].astype(MXU_DTYPE), lam[1].astype(MXU_DTYPE)
            a_bb[0, j] += _mm_tn(ub, lr_b)
            a_bb[1, j] += _mm_tn(ub, li_b)
            du_blocks.append(_mm_nt(lr_b, bb[0, j]) + _mm_nt(li_b, bb[1, j]))
        du_ssm = jnp.concatenate(du_blocks, axis=1) + dyss * mv_ref[ROW_SSM_D:ROW_SSM_D + 1, 0:512]
        dz_ref[...] = jnp.concatenate([du_pool, du_ssm, dgl_pool, dgl_ssm], axis=1).astype(SAVE_DTYPE)

        @pl.when(i == nt - 1)
        def _():
            rows = D_MODEL // NDEV
            for k in range(NDEV):
                st_wo[k] = a_wo[k * rows:(k + 1) * rows, :].astype(WIRE_DTYPE)
                for a, acc in enumerate((a_wpu, a_wglu, a_wsu)):
                    st_up[a, k] = acc[:, k * 128:(k + 1) * 128].astype(WIRE_DTYPE)
            outs = ((st_wo, dwo_h), (st_up.at[0], dwpu_h), (st_up.at[1], dwglu_h), (st_up.at[2], dwsu_h),
                    (a_pw, dpw_h), (a_bb, dbb_h), (a_ct, dct_h), (a_vs, vsum_h), (a_da, da_h))
            cps = [pltpu.make_async_copy(src, dst, sems.at[k]) for k, (src, dst) in enumerate(outs)]
            for cp in cps:
                cp.start()
            for cp in cps:
                cp.wait()

    def tok(width):
        return pl.BlockSpec((tm, width), lambda i: (nt - 1 - i, 0))

    hbm = _HBM
    acc_shapes = [(D_MODEL, D_MODEL), (512, D_MODEL), (512, D_MODEL), (512, D_MODEL), (4, 128, 128),
                  (2, SSM_BLOCKS, 128, SSM_BLOCK_STATE), (2, SSM_BLOCKS, SSM_BLOCK_STATE, 128), (8, D_MODEL),
                  (2, SSM_BLOCKS, SCAN_ROWS, SSM_BLOCK_STATE)]
    stack_out = [jax.ShapeDtypeStruct((NDEV, D_MODEL // NDEV, D_MODEL), WIRE_DTYPE)] \
        + [jax.ShapeDtypeStruct((NDEV, 512, 128), WIRE_DTYPE)] * 3
    return _launch(
        body, "mixer_backward", grid=(nt,), semantics=("arbitrary",), carry=carry, steps=_grid_steps(nt),
        out_shape=[jax.ShapeDtypeStruct((T, IN_WIDTH), SAVE_DTYPE)] + stack_out
        + [jax.ShapeDtypeStruct(s, F32) for s in acc_shapes[4:]],
        in_specs=[tok(D_MODEL), _resident(prm), tok(IN_WIDTH), tok(N_STATE), tok(N_STATE), tok(512), tok(512),
                  tok(D_MODEL), tok(512), tok(D_MODEL), tok(D_MODEL), hbm, hbm, hbm, hbm, _resident(pool_w),
                  _resident(mvec), _resident(srow), _resident(bb), _resident(ct)],
        out_specs=[tok(IN_WIDTH)] + [hbm] * len(acc_shapes),
        operands=(d2, prm, z, s_re, s_im, zp, q, y_pool, yss, vg, y_ssm, w_pu_s, w_glu_s, w_su_s, w_out, pool_w, mvec,
                  srow, bb, ct),
        scratch=[
            pltpu.VMEM((512, D_MODEL), MXU_DTYPE), pltpu.VMEM((512, D_MODEL), MXU_DTYPE),
            pltpu.VMEM((512, D_MODEL), MXU_DTYPE), pltpu.VMEM((D_MODEL, D_MODEL), MXU_DTYPE),
            pltpu.VMEM((4, 128, 128), MXU_DTYPE),
            pltpu.VMEM((SSM_BLOCKS, 8, SCAN_ROWS, SSM_BLOCK_STATE), F32),
            pltpu.VMEM((SSM_BLOCKS, 2, SCAN_ROWS, SSM_BLOCK_STATE), F32),
            pltpu.VMEM((POOL_HALO, POOL_WIDTH), F32),
            pltpu.VMEM((2, tm, SSM_BLOCK_STATE), F32), pltpu.VMEM((2, tm, SSM_BLOCK_STATE), F32),
        ] + [pltpu.VMEM(s, F32) for s in acc_shapes]
        + [pltpu.VMEM((NDEV, D_MODEL // NDEV, D_MODEL), WIRE_DTYPE), pltpu.VMEM((3, NDEV, 512, 128), WIRE_DTYPE),
           pltpu.SemaphoreType.DMA((25,))])


def _mixer_in_backward(x, dz, prm, w_in_s):
    T = x.shape[0]
    tm = min(T, TM_MIX)
    nt = T // tm
    cols = IN_WIDTH // NDEV

    def body(x_ref, dz_ref, prm_ref, w_in_h, dh_ref, dw_ref, w_in, acc, sems):
        i = pl.program_id(0)

        @pl.when(i == 0)
        def _():
            cps = _load_stack(w_in_h, w_in, sems, 0)
            acc[...] = jnp.zeros_like(acc)
            for cp in cps:
                cp.wait()

        h, _, _ = _modulated(x_ref[...], prm_ref, 1, ROW_G_MIX)
        dzb = dz_ref[...].astype(MXU_DTYPE)
        dh_ref[0] = _mm_nt(dzb, w_in[...]).astype(SAVE_DTYPE)
        acc[...] += _mm_tn(h, dzb)

        @pl.when(i == nt - 1)
        def _():
            for k in range(NDEV):
                dw_ref[k] = acc[:, k * cols:(k + 1) * cols].astype(WIRE_DTYPE)

    return pl.pallas_call(
        body, name="mixer_in_backward", grid=(nt,),
        out_shape=[jax.ShapeDtypeStruct((1, T, D_MODEL), SAVE_DTYPE), jax.ShapeDtypeStruct((NDEV, D_MODEL, cols), WIRE_DTYPE)],
        in_specs=[pl.BlockSpec((tm, D_MODEL), lambda i: (i, 0)), pl.BlockSpec((tm, IN_WIDTH), lambda i: (i, 0)),
                  pl.BlockSpec(prm.shape, lambda i: (0, 0)), pl.BlockSpec(memory_space=pl.ANY)],
        out_specs=[pl.BlockSpec((1, tm, D_MODEL), lambda i: (0, i, 0)),
                   pl.BlockSpec((NDEV, D_MODEL, cols), lambda i: (0, 0, 0))],
        scratch_shapes=[pltpu.VMEM((D_MODEL, IN_WIDTH), MXU_DTYPE), pltpu.VMEM((D_MODEL, IN_WIDTH), F32),
                        pltpu.SemaphoreType.DMA((8,))],
        compiler_params=_params("arbitrary"),
    )(x, dz, prm, w_in_s)


def _ssm_dense_backward(dbb, da, srow, b_dense):
    def body(dbb_ref, da_ref, srow_ref, bd_ref, db_ref, df_ref):
        df_re, df_im = [], []
        da_re = [_colsum(da_ref[0, j]) for j in range(SSM_BLOCKS)]
        da_im = [_colsum(da_ref[1, j]) for j in range(SSM_BLOCKS)]
        for j in range(SSM_BLOCKS):
            lanes = slice(j * SSM_BLOCK_STATE, (j + 1) * SSM_BLOCK_STATE)
            f_re, f_im = srow_ref[2:3, lanes], srow_ref[3:4, lanes]
            g_re, g_im = dbb_ref[0, j], dbb_ref[1, j]
            b_re, b_im = bd_ref[0, j], bd_ref[1, j]
            db_ref[0, j] = f_re * g_re + f_im * g_im
            db_ref[1, j] = f_re * g_im - f_im * g_re
            df_re.append(_colsum(g_re * b_re + g_im * b_im))
            df_im.append(_colsum(g_im * b_re - g_re * b_im))
        df_ref[...] = jnp.concatenate([jnp.concatenate(df_re, axis=1), jnp.concatenate(df_im, axis=1),
                                       jnp.concatenate(da_re, axis=1), jnp.concatenate(da_im, axis=1),
                                       jnp.zeros((4, N_STATE), F32)], axis=0)

    return pl.pallas_call(body, name="ssm_dense_backward",
                          out_shape=[jax.ShapeDtypeStruct(b_dense.shape, F32), jax.ShapeDtypeStruct((8, N_STATE), F32)],
                          compiler_params=pltpu.CompilerParams(vmem_limit_bytes=VMEM_LIMIT))(dbb, da, srow, b_dense)


def _adamw_update(w, g, m, v):
    m = ADAM_B1 * m + (1.0 - ADAM_B1) * g
    v = ADAM_B2 * v + (1.0 - ADAM_B2) * (g * g)
    m_hat = m / (1.0 - ADAM_B1 ** ADAM_STEP)
    v_hat = v / (1.0 - ADAM_B2 ** ADAM_STEP)
    delta = -ADAM_LR * (m_hat / (jnp.sqrt(v_hat) + ADAM_EPS) + ADAM_WD * w)
    return delta, m, v


def _adam_rows(shape):
    rows, cols = shape
    tr = rows
    while tr * cols * 4 > (1 << 20) and tr % 16 == 0:
        tr //= 2
    return tr


def _adam_sharded(w, m, v, land, order, name):
    R, C = w.shape
    tr = _adam_rows((R, C))

    def body(w_ref, m_ref, v_ref, land_ref, order_ref, g_ref, d_ref, mo_ref, vo_ref):
        g = land_ref[0].astype(F32)
        for b in range(1, NDEV):
            g = g + land_ref[b].astype(F32)
        g_ref[...] = g
        d_ref[...], mo_ref[...], vo_ref[...] = _adamw_update(w_ref[...], g, m_ref[...], v_ref[...])

    blk = pl.BlockSpec((tr, C), lambda i: (i, 0))
    return pl.pallas_call(
        body, name=name, grid=(R // tr,),
        out_shape=[jax.ShapeDtypeStruct((R, C), F32)] * 4,
        in_specs=[blk, blk, blk, pl.BlockSpec((NDEV, tr, C), lambda i: (0, i, 0)), _HBM],
        out_specs=[blk] * 4,
        compiler_params=_params("arbitrary"),
    )(w, m, v, land, order)


def _adam_ada(w, m, v, sc_all, dmod_cols):
    R, C = w.shape
    tr = 256

    def body(w_ref, m_ref, v_ref, sc_ref, dm_ref, g_ref, d_ref, mo_ref, vo_ref):
        g = _mm_tn(sc_ref[...], dm_ref[...])
        g_ref[...] = g
        d_ref[...], mo_ref[...], vo_ref[...] = _adamw_update(w_ref[...], g, m_ref[...], v_ref[...])

    blk = pl.BlockSpec((tr, C), lambda i: (i, 0))
    return pl.pallas_call(
        body, name="adam_w_ada", grid=(R // tr,),
        out_shape=[jax.ShapeDtypeStruct((R, C), F32)] * 4,
        in_specs=[blk, blk, blk, pl.BlockSpec((8, tr), lambda i: (0, i)), pl.BlockSpec((8, C), lambda i: (0, 0))],
        out_specs=[blk] * 4,
        compiler_params=_params("arbitrary"),
    )(w, m, v, sc_all, dmod_cols)


def _adam_small(w, g, m, v):
    def body(w_ref, g_ref, m_ref, v_ref, d_ref, mo_ref, vo_ref):
        d_ref[...], mo_ref[...], vo_ref[...] = _adamw_update(w_ref[...], g_ref[...], m_ref[...], v_ref[...])

    return pl.pallas_call(body, name="adam_small", out_shape=[jax.ShapeDtypeStruct(w.shape, F32)] * 3,
                          compiler_params=pltpu.CompilerParams(vmem_limit_bytes=VMEM_LIMIT))(w, g, m, v)


def _block_diag_in(b):
    bt = jnp.transpose(b, (0, 2, 1)).reshape(SSM_BLOCKS, 8, SSM_GROUP, SSM_STATE)
    eye = jnp.eye(8, dtype=bool)[None, :, None, :, None]
    return jnp.where(eye, bt[:, :, :, None, :], 0.0).reshape(SSM_BLOCKS, 128, SSM_BLOCK_STATE)


def _block_diag_out(c):
    ct = jnp.transpose(c, (0, 2, 1)).reshape(SSM_BLOCKS, 8, SSM_STATE, SSM_GROUP)
    eye = jnp.eye(8, dtype=bool)[None, :, None, :, None]
    return jnp.where(eye, ct[:, :, :, None, :], 0.0).reshape(SSM_BLOCKS, SSM_BLOCK_STATE, 128)


def _diag_blocks(dense, rows, cols):
    d5 = dense.reshape(SSM_BLOCKS, 8, rows, 8, cols)
    return jnp.stack([d5[:, a, :, a, :] for a in range(8)], axis=1).reshape(32, rows, cols)


def _pack_small(ada_vec, parts):
    rest = jnp.concatenate([parts[n].reshape(-1) for n, _ in SMALL_PARAMS])
    rest = jnp.pad(rest, (0, NDEV * REST_ROWS * 128 - SMALL_TOTAL)).reshape(NDEV, REST_ROWS, 128)
    return jnp.concatenate([ada_vec.reshape(NDEV, ADA_ROWS, 128), rest,
                            jnp.zeros((NDEV, PACK_ROWS - ADA_ROWS - REST_ROWS, 128), F32)], axis=1)


def _unpack_small(pack, shapes):
    ada_vec = pack[:, :ADA_ROWS].reshape(-1)
    rest = pack[:, ADA_ROWS:ADA_ROWS + REST_ROWS].reshape(-1)
    out, off = {}, 0
    for n, size in SMALL_PARAMS:
        out[n] = rest[off:off + size].reshape(shapes[n])
        off += size
    return ada_vec, out


WEIGHT_ORDER = ('w_ada', 'b_ada', 'g_ffn1', 'w_ffn1_in', 'w_ffn1_out', 'g_mix', 'w_in', 'pool_w', 'pool_b',
                'pool_scale', 'w_pool_up', 'ssm_lam_re_log', 'ssm_lam_im', 'ssm_log_dt', 'ssm_b_re', 'ssm_b_im',
                'ssm_c_re', 'ssm_c_im', 'ssm_d', 'w_glu', 'b_glu', 'w_ssm_up', 'w_out', 'g_ffn2', 'w_ffn2_in',
                'w_ffn2_out', 'g_final')
GATHERED = ('w_ffn1_in', 'w_ffn1_out', 'w_in', 'w_pool_up', 'w_glu', 'w_ssm_up', 'w_out', 'w_ffn2_in', 'w_ffn2_out')
TRANSPOSED = ('w_ffn1_in', 'w_ffn2_in')


def kernel(x, c, w_ada, b_ada, g_ffn1, w_ffn1_in, w_ffn1_out, g_mix, w_in, pool_w, pool_b, pool_scale, w_pool_up, ssm_lam_re_log, ssm_lam_im, ssm_log_dt, ssm_b_re, ssm_b_im, ssm_c_re, ssm_c_im, ssm_d, w_glu, b_glu, w_ssm_up, w_out, g_ffn2, w_ffn2_in, w_ffn2_out, g_final, loss_target, m_w_ada, m_b_ada, m_g_ffn1, m_w_ffn1_in, m_w_ffn1_out, m_g_mix, m_w_in, m_pool_w, m_pool_b, m_pool_scale, m_w_pool_up, m_ssm_lam_re_log, m_ssm_lam_im, m_ssm_log_dt, m_ssm_b_re, m_ssm_b_im, m_ssm_c_re, m_ssm_c_im, m_ssm_d, m_w_glu, m_b_glu, m_w_ssm_up, m_w_out, m_g_ffn2, m_w_ffn2_in, m_w_ffn2_out, m_g_final, v_w_ada, v_b_ada, v_g_ffn1, v_w_ffn1_in, v_w_ffn1_out, v_g_mix, v_w_in, v_pool_w, v_pool_b, v_pool_scale, v_w_pool_up, v_ssm_lam_re_log, v_ssm_lam_im, v_ssm_log_dt, v_ssm_b_re, v_ssm_b_im, v_ssm_c_re, v_ssm_c_im, v_ssm_d, v_w_glu, v_b_glu, v_w_ssm_up, v_w_out, v_g_ffn2, v_w_ffn2_in, v_w_ffn2_out, v_g_final):
    args = locals()
    W = {n: args[n] for n in WEIGHT_ORDER}
    M = {n: args["m_" + n] for n in WEIGHT_ORDER}
    V = {n: args["v_" + n] for n in WEIGHT_ORDER}
    shapes = {n: W[n].shape for n in WEIGHT_ORDER}
    xt, tgt = x[0], loss_target[0]

    def local(tree, n):
        return jnp.swapaxes(tree[n][0], 0, 1) if n in TRANSPOSED else tree[n][0]

    def as_output(n, a):
        return (jnp.swapaxes(a, 0, 1) if n in TRANSPOSED else a)[None]

    shard = dict(zip(GATHERED, _cast_shards([local(W, n) for n in GATHERED])))
    stacks = {}

    def gather(names):
        return _Gather([shard[n] for n in names])

    def gathered(names, results):
        stacks.update(zip(names, results))

    ffn1_w, ffn2_w = ('w_ffn1_in', 'w_ffn1_out'), ('w_ffn2_in', 'w_ffn2_out')
    mix_w = ('w_in', 'w_pool_up', 'w_glu', 'w_ssm_up', 'w_out')
    mod_cols, sc_all, *res = _ada_forward(c, W['w_ada'][0], b_ada.reshape(NDEV, -1), gather(ffn1_w[:1]))
    gathered(ffn1_w[:1], res)
    win1 = stacks['w_ffn1_in'].reshape(2, 4, FF_SHARD, D_MODEL)
    prm = jnp.concatenate([mod_cols.reshape(9, D_MODEL), g_ffn1, g_mix, g_ffn2, g_final[None], jnp.zeros((3, D_MODEL), F32)], axis=0)
    pad512 = jnp.zeros((1, D_MODEL - 512), F32)
    mvec = jnp.concatenate([jnp.concatenate([pool_b, pad512], axis=1), jnp.concatenate([pool_scale, pad512], axis=1),
                            jnp.concatenate([ssm_d, pad512], axis=1), b_glu, jnp.zeros((4, D_MODEL), F32)], axis=0)
    log_dt_col = ssm_log_dt[0][:, None]
    coeffs = _ssm_params_forward(ssm_lam_re_log[0], ssm_lam_im[0], log_dt_col)
    srow = jnp.stack([t.reshape(N_STATE) for t in coeffs], axis=0)
    b_dense = jnp.stack([_block_diag_in(ssm_b_re[0]), _block_diag_in(ssm_b_im[0])], axis=0)
    c_dense = jnp.stack([_block_diag_out(ssm_c_re[0]), _block_diag_out(ssm_c_im[0])], axis=0)
    bb, ct = _ssm_dense_forward(srow, b_dense, c_dense)
    pw = pool_w[0]

    next_w = ffn1_w[1:] + mix_w[:1]
    ab1, s1, *res = _ffn_hidden(xt, prm, win1, 0, ROW_G_FFN1, "ffn1_hidden", gather(next_w))
    gathered(next_w, res)
    wout1 = stacks['w_ffn1_out'].reshape(4, FF_SHARD, D_MODEL)
    x1, f1, *res = _ffn_out(xt, s1, prm, wout1, 0, "ffn1_out", gather(mix_w[1:]))
    gathered(mix_w[1:], res)
    w_out_full = stacks['w_out'].reshape(D_MODEL, D_MODEL)
    res = _mixer_forward(x1, prm, stacks['w_in'], stacks['w_pool_up'], stacks['w_glu'], stacks['w_ssm_up'],
                         w_out_full, pw, mvec, srow, bb, ct, gather(ffn2_w))
    x2, mo, saved = res[0], res[1], res[2:11]
    gathered(ffn2_w, res[11:])
    win2 = stacks['w_ffn2_in'].reshape(2, 4, FF_SHARD, D_MODEL)
    wout2 = stacks['w_ffn2_out'].reshape(4, FF_SHARD, D_MODEL)
    x3, f3, ab3 = _ffn_forward(x2, prm, win2, wout2, 2, ROW_G_FFN2, "ffn2_forward")
    d3, fin = _final_loss(x3, tgt, prm)
    loss = lax.psum(fin[1, 0], ("x", "y", "c"))

    lands = {}

    def scatter(grads):
        names = list(grads)
        return _Scatter([grads[n][0] for n in names], [grads[n][1] for n in names], [local(W, n).shape for n in names])

    def scattered(grads, results):
        lands.update(zip(grads, results))

    parts3, dwin2, dwout2 = _ffn_backward(x2, d3, ab3, prm, win2, wout2, 2, ROW_G_FFN2, "ffn2_backward")
    d2, sums3 = _norm_backward(parts3, x2, d3, f3, prm, 2, ROW_G_FFN2, 0.5, "ffn2_norm_backward")
    g_ffn2_w = {'w_ffn2_in': (dwin2, _halves), 'w_ffn2_out': (dwout2.reshape(NDEV, -1, D_MODEL), _stacked)}
    res = _mixer_backward(d2, prm, saved, stacks['w_pool_up'], stacks['w_glu'], stacks['w_ssm_up'], w_out_full, pw, mvec,
                          srow, bb, ct, scatter(g_ffn2_w))
    dz, dwo, dwpu, dwglu, dwsu, dpw, dbb, dct, vsum, da = res[:10]
    scattered(g_ffn2_w, res[10:])
    parts2, dwin_mix = _mixer_in_backward(x1, dz, prm, stacks['w_in'])
    d1, sums2 = _norm_backward(parts2, x1, d2, mo, prm, 1, ROW_G_MIX, 1.0, "mixer_norm_backward")
    g_mix_w = {'w_in': (dwin_mix, _stacked), 'w_pool_up': (dwpu, _stacked), 'w_glu': (dwglu, _stacked),
               'w_ssm_up': (dwsu, _stacked), 'w_out': (dwo, _stacked)}
    parts1, dwin1, dwout1, *res = _ffn_backward(xt, d1, ab1, prm, win1, wout1, 0, ROW_G_FFN1, "ffn1_backward",
                                                scatter(g_mix_w))
    scattered(g_mix_w, res)
    d0, sums1 = _norm_backward(parts1, xt, d1, f1, prm, 0, ROW_G_FFN1, 0.5, "ffn1_norm_backward")

    db_dense, df_rows = _ssm_dense_backward(dbb, da, srow, b_dense)
    cot = [df_rows[r].reshape(32, 64) for r in (2, 3, 0, 1)]
    d_lrl, d_li, d_ldt = _ssm_params_backward(ssm_lam_re_log[0], ssm_lam_im[0], log_dt_col, cot)
    small_grads = {
        'g_ffn1': sums1[0], 'g_mix': sums2[0], 'g_ffn2': sums3[0], 'g_final': fin[0], 'pool_w': dpw,
        'pool_b': vsum[1, :512], 'pool_scale': vsum[0, :512], 'ssm_lam_re_log': d_lrl, 'ssm_lam_im': d_li,
        'ssm_log_dt': d_ldt, 'ssm_b_re': jnp.transpose(_diag_blocks(db_dense[0], SSM_GROUP, SSM_STATE), (0, 2, 1)),
        'ssm_b_im': jnp.transpose(_diag_blocks(db_dense[1], SSM_GROUP, SSM_STATE), (0, 2, 1)),
        'ssm_c_re': jnp.transpose(_diag_blocks(dct[0], SSM_STATE, SSM_GROUP), (0, 2, 1)),
        'ssm_c_im': jnp.transpose(_diag_blocks(dct[1], SSM_STATE, SSM_GROUP), (0, 2, 1)),
        'ssm_d': vsum[2, :512], 'b_glu': vsum[3],
    }
    dmod = jnp.concatenate([sums1[1:4], sums2[1:4], sums3[1:4]], axis=0).reshape(-1)
    total, landed = _allreduce_small(_pack_small(dmod, small_grads))
    dmod_cols = landed[:, :ADA_ROWS].reshape(NDEV, ADA_ROWS * 128)

    g_ffn1_w = {'w_ffn1_in': (dwin1, _halves), 'w_ffn1_out': (dwout1.reshape(NDEV, -1, D_MODEL), _stacked)}
    last_views = [g_ffn1_w[n][1] for n in ffn1_w]
    send_sems, recv_sems, last_src, last_land, token = _scatter_start(
        [g_ffn1_w[n][0] for n in ffn1_w], last_views, [local(W, n).shape for n in ffn1_w], [total])
    total = total + token[0:1, 0:1]

    grad, delta, new_m, new_v = {}, {}, {}, {}

    def adam_sharded(n):
        res = _adam_sharded(local(W, n), local(M, n), local(V, n), lands[n], token, "adam_" + n)
        grad[n], delta[n], new_m[n], new_v[n] = [as_output(n, r) for r in res]
        return res[3]

    done = [adam_sharded(n) for n in GATHERED if n not in ffn1_w]
    res = _adam_ada(W['w_ada'][0], M['w_ada'][0], V['w_ada'][0], sc_all, dmod_cols + token[0:1, 0:1])
    grad['w_ada'], delta['w_ada'], new_m['w_ada'], new_v['w_ada'] = [r[None] for r in res]
    done.append(res[3])

    flat = lambda t: t.reshape(NDEV * PACK_ROWS, 128)
    small_w = flat(_pack_small(b_ada.reshape(-1), W))
    small_m = flat(_pack_small(m_b_ada.reshape(-1), M))
    small_v = flat(_pack_small(v_b_ada.reshape(-1), V))
    res = _adam_small(small_w, flat(total), small_m, small_v)
    done.append(res[2])
    for dst, packed in zip((grad, delta, new_m, new_v), (total, *res)):
        ada_vec, rest = _unpack_small(packed.reshape(NDEV, PACK_ROWS, 128), shapes)
        dst.update(rest)
        dst['b_ada'] = ada_vec.reshape(shapes['b_ada'])
    small_names = [n for n, _ in SMALL_PARAMS] + ['b_ada']
    firsts = [dst[n].reshape(-1)[0] for dst in (grad, delta, new_m, new_v) for n in small_names]
    done.append(jnp.full((8, 128), functools.reduce(jnp.add, firsts, loss), F32))

    lands.update(zip(ffn1_w, _scatter_wait(send_sems, recv_sems, last_src, last_land, last_views, done)))
    for n in ffn1_w:
        adam_sharded(n)

    return (loss, d0[None], *[grad[n] for n in WEIGHT_ORDER], *[delta[n] for n in WEIGHT_ORDER],
            *[new_m[n] for n in WEIGHT_ORDER], *[new_v[n] for n in WEIGHT_ORDER])
```

```python
import functools

import jax
import jax.numpy as jnp
from jax import lax
from jax.experimental import pallas as pl
from jax.experimental.pallas import tpu as pltpu

F32 = jnp.float32
MXU_DTYPE = jnp.bfloat16
WIRE_DTYPE = jnp.bfloat16
SAVE_DTYPE = jnp.bfloat16

NDEV = 8
D_MODEL = 1024
D_FF = 2816
FF_SHARD = 2 * D_FF // NDEV
POOL_WIDTH = 512
POOL_GROUP = 128
SSM_WIDTH = 512
SSM_STATE = 64
SSM_GROUP = 16
SSM_BLOCKS = 4
SSM_BLOCK_STATE = 512
N_STATE = 2048
IN_WIDTH = 3072
EPS = 1e-6
ADAM_LR = 0.001
ADAM_B1 = 0.9
ADAM_B2 = 0.999
ADAM_EPS = 1e-08
ADAM_WD = 0.01
ADAM_STEP = 10

TM_FFN = 512
FFN_BWD_CHUNK = 256
TM_MIX = 256
TM_MIX_BWD = 256
TM_EW = 512
SCAN_ROWS = 8
POOL_HALO = 16
VMEM_LIMIT = 60 * 1024 * 1024

ROW_G_FFN1, ROW_G_MIX, ROW_G_FFN2, ROW_G_FINAL = 9, 10, 11, 12
ROW_POOL_B, ROW_POOL_SCALE, ROW_SSM_D, ROW_B_GLU = 0, 1, 2, 3

SMALL_PARAMS = (
    ("g_ffn1", 1024), ("g_mix", 1024), ("g_ffn2", 1024), ("g_final", 1024), ("pool_w", 65536),
    ("pool_b", 512), ("pool_scale", 512), ("ssm_lam_re_log", 2048), ("ssm_lam_im", 2048),
    ("ssm_log_dt", 32), ("ssm_b_re", 32768), ("ssm_b_im", 32768), ("ssm_c_re", 32768),
    ("ssm_c_im", 32768), ("ssm_d", 512), ("b_glu", 1024),
)
SMALL_TOTAL = sum(n for _, n in SMALL_PARAMS)
ADA_ROWS = 9
REST_ROWS = 203
PACK_ROWS = 216
MESH = pl.DeviceIdType.MESH


def _mm(a, b):
    return jnp.dot(a.astype(MXU_DTYPE), b.astype(MXU_DTYPE), preferred_element_type=F32)


def _mm_nt(a, b):
    return lax.dot_general(a.astype(MXU_DTYPE), b.astype(MXU_DTYPE), (((1,), (1,)), ((), ())),
                           preferred_element_type=F32)


def _mm_tn(a, b):
    return lax.dot_general(a.astype(MXU_DTYPE), b.astype(MXU_DTYPE), (((0,), (0,)), ((), ())),
                           preferred_element_type=F32)


def _rms_scale(x):
    return lax.rsqrt(jnp.mean(x * x, axis=-1, keepdims=True) + EPS)


def _sigmoid(x):
    return jax.nn.sigmoid(x)


def _colsum(x):
    return jnp.sum(x, axis=0, keepdims=True)


def _row(ref, r):
    return ref[r:r + 1, :]


def _params(*sem):
    return pltpu.CompilerParams(dimension_semantics=sem, vmem_limit_bytes=VMEM_LIMIT)


def _resident(a):
    return pl.BlockSpec(a.shape, lambda *_: (0,) * a.ndim, pipeline_mode=pl.Buffered(1))


def _me():
    return lax.axis_index("x"), lax.axis_index("y"), lax.axis_index("c")


def _peer(rel):
    x, y, c = _me()
    px = 1 - x if rel & 4 else x
    py = 1 - y if rel & 2 else y
    pc = 1 - c if rel & 1 else c
    return (px, py, pc), 4 * px + 2 * py + pc


_HBM = pl.BlockSpec(memory_space=pl.ANY)
_HBM_ONLY = pl.BlockSpec(memory_space=pltpu.HBM)


def _stacked(ref, p):
    return ref.at[p]


def _halves(ref, p):
    return ref.at[p // 4, p % 4]


class _Gather:
    def __init__(self, shards):
        self.operands = list(shards)
        self.n = len(shards)
        self.out_shape = [jax.ShapeDtypeStruct((NDEV,) + s.shape, s.dtype) for s in shards]
        self.scratch = [pltpu.SemaphoreType.DMA((7 * self.n,)), pltpu.SemaphoreType.DMA((7 * self.n,)),
                        pltpu.SemaphoreType.DMA((self.n,))]

    def plan(self, srcs, outs, sems):
        send_sems, recv_sems, local_sems = sems
        n = self.n
        x, y, c = _me()
        me = 4 * x + 2 * y + c
        here, sibling = (x, y, c), (x, y, 1 - c)
        chips = [(1 - x, y), (x, 1 - y), (1 - x, 1 - y)]

        def blk(px, py, pc):
            return 4 * px + 2 * py + pc

        def copy(a, k, block, to, src=None):
            return pltpu.make_async_remote_copy(
                src_ref=outs[a].at[block] if src is None else src, dst_ref=outs[a].at[block],
                send_sem=send_sems.at[7 * a + k], recv_sem=recv_sems.at[7 * a + k], device_id=to, device_id_type=MESH)

        def mine(a):
            return pltpu.make_async_copy(srcs[a], outs[a].at[me], local_sems.at[a])

        def first(a):
            return [copy(a, 0, me, sibling, src=srcs[a])] + [copy(a, 1 + j, me, (*chip, c), src=srcs[a])
                                                              for j, chip in enumerate(chips)]

        def start():
            for a in range(n):
                mine(a).start()
                for cp in first(a):
                    cp.start()

        def forward():
            for a in range(n):
                for j, chip in enumerate(chips):
                    copy(a, 1 + j, blk(*chip, c), here).wait_recv()
                    copy(a, 4 + j, blk(*chip, c), sibling).start()

        def finish():
            for a in range(n):
                copy(a, 0, blk(x, y, 1 - c), here).wait_recv()
                for j, chip in enumerate(chips):
                    copy(a, 4 + j, blk(*chip, 1 - c), here).wait_recv()
            for a in range(n):
                mine(a).wait()
                for cp in first(a):
                    cp.wait_send()
                for j, chip in enumerate(chips):
                    copy(a, 4 + j, blk(*chip, c), sibling).wait_send()

        return start, forward, finish


class _Scatter:
    def __init__(self, arrays, views, shard_shapes):
        self.operands = list(arrays)
        self.views = list(views)
        self.n = len(arrays)
        self.out_shape = [jax.ShapeDtypeStruct((NDEV,) + tuple(s), a.dtype) for s, a in zip(shard_shapes, arrays)]
        self.scratch = [pltpu.SemaphoreType.DMA((7 * self.n,)), pltpu.SemaphoreType.DMA((7 * self.n,)),
                        pltpu.SemaphoreType.DMA((self.n,))]

    def plan(self, srcs, outs, sems):
        send_sems, recv_sems, local_sems = sems
        n, views = self.n, self.views
        x, y, c = _me()
        me = 4 * x + 2 * y + c

        def mine(a):
            return pltpu.make_async_copy(views[a](srcs[a], me), outs[a].at[me], local_sems.at[a])

        def copy(a, rel, sending):
            to, p = _peer(rel)
            return pltpu.make_async_remote_copy(
                src_ref=views[a](srcs[a], p), dst_ref=outs[a].at[me if sending else p],
                send_sem=send_sems.at[7 * a + rel - 1], recv_sem=recv_sems.at[7 * a + rel - 1],
                device_id=to if sending else (x, y, c), device_id_type=MESH)

        def start():
            for a in range(n):
                mine(a).start()
            for rel in range(1, 8):
                for a in range(n):
                    copy(a, rel, True).start()

        def forward():
            pass

        def finish():
            for rel in range(1, 8):
                for a in range(n):
                    copy(a, rel, False).wait_recv()
            for rel in range(1, 8):
                for a in range(n):
                    copy(a, rel, True).wait_send()
            for a in range(n):
                mine(a).wait()

        return start, forward, finish


def _launch(body, name, out_shape, in_specs, out_specs, operands, scratch=(), grid=None, semantics=None,
            carry=None, steps=None):
    out_shape, in_specs, out_specs = list(out_shape), list(in_specs), list(out_specs)
    operands, scratch = list(operands), list(scratch)
    n_in, n_out, n_scr = len(in_specs), len(out_shape), len(scratch)
    kernel_body = body
    if carry is not None:
        k = carry.n

        def kernel_body(*refs):
            ins, cin = refs[:n_in], refs[n_in:n_in + k]
            outs, cout = refs[n_in + k:n_in + k + n_out], refs[n_in + k + n_out:n_in + 2 * k + n_out]
            rest = refs[n_in + 2 * k + n_out:]
            scr, csem = rest[:n_scr], rest[n_scr:]
            start, forward, finish = carry.plan(cin, cout, csem)
            if steps is None:
                start()
                body(*ins, *outs, *scr)
                forward()
                finish()
            else:
                pl.when(steps()[0])(start)
                pl.when(steps()[1])(forward)
                body(*ins, *outs, *scr)
                pl.when(steps()[2])(finish)

        in_specs += [_HBM] * k
        out_shape += carry.out_shape
        out_specs += [_HBM] * k
        operands += carry.operands
        scratch += carry.scratch
    kwargs = {} if grid is None else {"grid": grid}
    params = pltpu.CompilerParams(vmem_limit_bytes=VMEM_LIMIT) if semantics is None else _params(*semantics)
    return pl.pallas_call(kernel_body, name=name, out_shape=out_shape, in_specs=in_specs, out_specs=out_specs,
                          scratch_shapes=scratch, compiler_params=params, **kwargs)(*operands)


def _grid_steps(nt):
    def steps():
        i = pl.program_id(0)
        return i == 0, i == nt - 1, i == nt - 1
    return steps


def _cast_shards(shards):
    n = len(shards)

    def body(*refs):
        for a in range(n):
            refs[n + a][...] = refs[a][...].astype(WIRE_DTYPE)

    return pl.pallas_call(body, name="cast_shards",
                          out_shape=[jax.ShapeDtypeStruct(s.shape, WIRE_DTYPE) for s in shards],
                          compiler_params=pltpu.CompilerParams(vmem_limit_bytes=VMEM_LIMIT))(*shards)


_SEM = pl.BlockSpec(memory_space=pltpu.SEMAPHORE)
_DATAFLOW = pltpu.SideEffectType.DATAFLOW_SIDE_EFFECTING


def _split_copy(arrays, views, landing, send_sems, recv_sems, a, rel):
    to, p = _peer(rel)
    x, y, c = _me()
    return pltpu.make_async_remote_copy(
        src_ref=views[a](arrays[a], p), dst_ref=landing[a].at[4 * x + 2 * y + c],
        send_sem=send_sems.at[NDEV * a + rel], recv_sem=recv_sems.at[NDEV * a + rel], device_id=to, device_id_type=MESH)


def _scatter_start(arrays, views, shard_shapes, after):
    n = len(arrays)
    landing = [pltpu.with_memory_space_constraint(lax.empty((NDEV,) + tuple(s), a.dtype), pltpu.HBM)
               for s, a in zip(shard_shapes, arrays)]
    arrays = [pltpu.with_memory_space_constraint(a, pltpu.HBM) for a in arrays]

    def body(*refs):
        ins, land = refs[:n], refs[n:2 * n]
        send_sems, recv_sems = refs[2 * n + len(after)], refs[2 * n + len(after) + 1]
        token = refs[-1]
        for rel in range(NDEV):
            for a in range(n):
                _split_copy(ins, views, land, send_sems, recv_sems, a, rel).start()
        token[...] = jnp.zeros_like(token)

    res = pl.pallas_call(
        body, name="scatter_start",
        out_shape=[pltpu.SemaphoreType.DMA((NDEV * n,)), pltpu.SemaphoreType.DMA((NDEV * n,))]
        + [pltpu.HBM(a.shape, a.dtype) for a in arrays] + [pltpu.HBM(l.shape, l.dtype) for l in landing]
        + [jax.ShapeDtypeStruct((8, 128), F32)],
        in_specs=[_HBM_ONLY] * (2 * n) + [_HBM] * len(after),
        out_specs=[_SEM, _SEM] + [_HBM_ONLY] * (2 * n) + [pl.BlockSpec(memory_space=pltpu.VMEM)],
        input_output_aliases={i: 2 + i for i in range(2 * n)},
        compiler_params=pltpu.CompilerParams(has_side_effects=_DATAFLOW),
    )(*arrays, *landing, *after)
    return res[0], res[1], res[2:2 + n], res[2 + n:2 + 2 * n], res[-1]


def _scatter_wait(send_sems, recv_sems, arrays, landing, views, after):
    n = len(arrays)

    def body(*refs):
        ins, land = refs[:n], refs[n:2 * n]
        send, recv = refs[2 * n], refs[2 * n + 1]
        for rel in range(NDEV):
            for a in range(n):
                cp = _split_copy(ins, views, land, send, recv, a, rel)
                cp.wait_send()
                cp.wait_recv()

    res = pl.pallas_call(
        body, name="scatter_wait",
        out_shape=[pltpu.HBM(a.shape, a.dtype) for a in arrays] + [pltpu.HBM(l.shape, l.dtype) for l in landing],
        in_specs=[_HBM_ONLY] * (2 * n) + [_SEM, _SEM] + [_HBM] * len(after),
        out_specs=[_HBM_ONLY] * (2 * n),
        input_output_aliases={i: i for i in range(2 * n)},
        compiler_params=pltpu.CompilerParams(has_side_effects=_DATAFLOW),
    )(*arrays, *landing, send_sems, recv_sems, *after)
    return res[n:]


def _ada_forward(c_row, w_ada, b_ada8, carry):
    cols = w_ada.shape[1]

    def body(c_ref, w_ref, b_ref, mod_ref, sc_ref, c_all, send_buf, recv_buf, send1, recv1, send2, recv2):
        x, y, c = _me()
        me = 4 * x + 2 * y + c
        rowi = lax.broadcasted_iota(jnp.int32, (8, D_MODEL), 0)
        c_all[me] = jnp.broadcast_to(c_ref[...], (8, D_MODEL))
        copies = []
        for rel in range(1, 8):
            to, _ = _peer(rel)
            cp = pltpu.make_async_remote_copy(src_ref=c_all.at[me], dst_ref=c_all.at[me], send_sem=send1.at[rel - 1],
                                              recv_sem=recv1.at[rel - 1], device_id=to, device_id_type=MESH)
            cp.start()
            copies.append(cp)
        for rel in range(1, 8):
            _, p = _peer(rel)
            pltpu.make_async_remote_copy(src_ref=c_all.at[p], dst_ref=c_all.at[p], send_sem=send1.at[rel - 1],
                                         recv_sem=recv1.at[rel - 1], device_id=(x, y, c), device_id_type=MESH).wait_recv()
        for cp in copies:
            cp.wait_send()
        cmat = jnp.zeros((8, D_MODEL), F32)
        for b in range(8):
            cmat = jnp.where(rowi == b, c_all[b], cmat)
        sc = cmat * _sigmoid(cmat)
        sc_ref[...] = sc
        modcols = _mm(sc, w_ref[...]) + b_ref[pl.ds(me, 1), :]
        for b in range(8):
            send_buf[b] = jnp.broadcast_to(modcols[b:b + 1, :], (8, cols))
        recv_buf[me] = send_buf[me]
        copies = []
        for rel in range(1, 8):
            to, p = _peer(rel)
            cp = pltpu.make_async_remote_copy(src_ref=send_buf.at[p], dst_ref=recv_buf.at[me], send_sem=send2.at[rel - 1],
                                              recv_sem=recv2.at[rel - 1], device_id=to, device_id_type=MESH)
            cp.start()
            copies.append(cp)
        for rel in range(1, 8):
            _, p = _peer(rel)
            pltpu.make_async_remote_copy(src_ref=send_buf.at[p], dst_ref=recv_buf.at[p], send_sem=send2.at[rel - 1],
                                         recv_sem=recv2.at[rel - 1], device_id=(x, y, c), device_id_type=MESH).wait_recv()
        for cp in copies:
            cp.wait_send()
        rowc = lax.broadcasted_iota(jnp.int32, (8, cols), 0)
        out = jnp.zeros((8, cols), F32)
        for k in range(8):
            out = jnp.where(rowc == k, recv_buf[k], out)
        mod_ref[...] = out

    return _launch(
        body, "ada_forward",
        out_shape=[jax.ShapeDtypeStruct((8, cols), F32), jax.ShapeDtypeStruct((8, D_MODEL), F32)],
        in_specs=[pl.BlockSpec(memory_space=pltpu.VMEM)] * 3,
        out_specs=[pl.BlockSpec(memory_space=pltpu.VMEM)] * 2,
        operands=(c_row, w_ada, b_ada8),
        scratch=[pltpu.VMEM((8, 8, D_MODEL), F32), pltpu.VMEM((8, 8, cols), F32), pltpu.VMEM((8, 8, cols), F32)]
        + [pltpu.SemaphoreType.DMA((7,))] * 4,
        carry=carry)


def _allreduce_small(pack):
    rows = pack.shape[1]

    def body(pack_ref, total_ref, land_ref, send1, recv1, send2, recv2):
        x, y, c = _me()
        me = 4 * x + 2 * y + c
        land_ref[me] = pack_ref[me]
        copies = []
        for rel in range(1, 8):
            to, p = _peer(rel)
            cp = pltpu.make_async_remote_copy(src_ref=pack_ref.at[p], dst_ref=land_ref.at[me], send_sem=send1.at[rel - 1],
                                              recv_sem=recv1.at[rel - 1], device_id=to, device_id_type=MESH)
            cp.start()
            copies.append(cp)
        for rel in range(1, 8):
            _, p = _peer(rel)
            pltpu.make_async_remote_copy(src_ref=pack_ref.at[p], dst_ref=land_ref.at[p], send_sem=send1.at[rel - 1],
                                         recv_sem=recv1.at[rel - 1], device_id=(x, y, c), device_id_type=MESH).wait_recv()
        for cp in copies:
            cp.wait_send()
        acc = land_ref[0]
        for b in range(1, 8):
            acc = acc + land_ref[b]
        total_ref[me] = acc
        copies = []
        for rel in range(1, 8):
            to, _ = _peer(rel)
            cp = pltpu.make_async_remote_copy(src_ref=total_ref.at[me], dst_ref=total_ref.at[me], send_sem=send2.at[rel - 1],
                                              recv_sem=recv2.at[rel - 1], device_id=to, device_id_type=MESH)
            cp.start()
            copies.append(cp)
        for rel in range(1, 8):
            _, p = _peer(rel)
            pltpu.make_async_remote_copy(src_ref=total_ref.at[p], dst_ref=total_ref.at[p], send_sem=send2.at[rel - 1],
                                         recv_sem=recv2.at[rel - 1], device_id=(x, y, c), device_id_type=MESH).wait_recv()
        for cp in copies:
            cp.wait_send()

    return pl.pallas_call(
        body, name="allreduce_small",
        out_shape=[jax.ShapeDtypeStruct((8, rows, 128), F32), jax.ShapeDtypeStruct((8, rows, 128), F32)],
        in_specs=[pl.BlockSpec(memory_space=pltpu.VMEM)],
        out_specs=[pl.BlockSpec(memory_space=pltpu.VMEM)] * 2,
        scratch_shapes=[pltpu.SemaphoreType.DMA((7,))] * 4,
        compiler_params=pltpu.CompilerParams(vmem_limit_bytes=VMEM_LIMIT),
    )(pack)


def _modulated(x, prm_ref, sub, g_row):
    shift, scale = _row(prm_ref, 3 * sub), _row(prm_ref, 3 * sub + 1)
    g = _row(prm_ref, g_row)
    r = _rms_scale(x)
    n0 = x * r
    return (n0 * g) * (1.0 + scale) + shift, r, n0


def _ffn_forward(x, prm, win, wout, sub, g_row, name, carry=None):
    T = x.shape[0]
    tm = min(T, TM_FFN)

    def body(x_ref, prm_ref, win_ref, wout_ref, xo_ref, f_ref, ab_ref):
        xv = x_ref[...]
        h, _, _ = _modulated(xv, prm_ref, sub, g_row)
        hb = h.astype(MXU_DTYPE)
        acc = None
        for j in range(4):
            a = _mm_nt(hb, win_ref[0, j])
            b = _mm_nt(hb, win_ref[1, j])
            ab_ref[0, j] = a.astype(SAVE_DTYPE)
            ab_ref[1, j] = b.astype(SAVE_DTYPE)
            s = (a * _sigmoid(a)) * b
            t = _mm(s, wout_ref[j])
            acc = t if acc is None else acc + t
        f_ref[...] = acc.astype(SAVE_DTYPE)
        xo_ref[...] = xv + (0.5 * _row(prm_ref, 3 * sub + 2)) * acc

    tok = pl.BlockSpec((tm, D_MODEL), lambda i: (i, 0))
    return _launch(
        body, name, grid=(T // tm,), semantics=("arbitrary",),
        out_shape=[jax.ShapeDtypeStruct((T, D_MODEL), F32), jax.ShapeDtypeStruct((T, D_MODEL), SAVE_DTYPE),
                   jax.ShapeDtypeStruct((2, 4, T, FF_SHARD), SAVE_DTYPE)],
        in_specs=[tok, _resident(prm), _resident(win), _resident(wout)],
        out_specs=[tok, tok, pl.BlockSpec((2, 4, tm, FF_SHARD), lambda i: (0, 0, i, 0))],
        operands=(x, prm, win, wout), carry=carry, steps=_grid_steps(T // tm))


def _ffn_hidden(x, prm, win, sub, g_row, name, carry=None):
    T = x.shape[0]
    tm = min(T, TM_FFN)

    def body(x_ref, prm_ref, win_ref, ab_ref, s_ref):
        h, _, _ = _modulated(x_ref[...], prm_ref, sub, g_row)
        hb = h.astype(MXU_DTYPE)
        for j in range(4):
            a = _mm_nt(hb, win_ref[0, j])
            b = _mm_nt(hb, win_ref[1, j])
            ab_ref[0, j] = a.astype(SAVE_DTYPE)
            ab_ref[1, j] = b.astype(SAVE_DTYPE)
            s_ref[j] = ((a * _sigmoid(a)) * b).astype(MXU_DTYPE)

    return _launch(
        body, name, grid=(T // tm,), semantics=("arbitrary",),
        out_shape=[jax.ShapeDtypeStruct((2, 4, T, FF_SHARD), SAVE_DTYPE), jax.ShapeDtypeStruct((4, T, FF_SHARD), MXU_DTYPE)],
        in_specs=[pl.BlockSpec((tm, D_MODEL), lambda i: (i, 0)), _resident(prm), _resident(win)],
        out_specs=[pl.BlockSpec((2, 4, tm, FF_SHARD), lambda i: (0, 0, i, 0)),
                   pl.BlockSpec((4, tm, FF_SHARD), lambda i: (0, i, 0))],
        operands=(x, prm, win), carry=carry, steps=_grid_steps(T // tm))


def _ffn_out(x, s, prm, wout, sub, name, carry=None):
    T = x.shape[0]
    tm = min(T, TM_FFN)

    def body(x_ref, s_ref, prm_ref, wout_ref, xo_ref, f_ref):
        acc = None
        for j in range(4):
            t = _mm(s_ref[j], wout_ref[j])
            acc = t if acc is None else acc + t
        f_ref[...] = acc.astype(SAVE_DTYPE)
        xo_ref[...] = x_ref[...] + (0.5 * _row(prm_ref, 3 * sub + 2)) * acc

    tok = pl.BlockSpec((tm, D_MODEL), lambda i: (i, 0))
    return _launch(
        body, name, grid=(T // tm,), semantics=("arbitrary",),
        out_shape=[jax.ShapeDtypeStruct((T, D_MODEL), F32), jax.ShapeDtypeStruct((T, D_MODEL), SAVE_DTYPE)],
        in_specs=[tok, pl.BlockSpec((4, tm, FF_SHARD), lambda i: (0, i, 0)), _resident(prm), _resident(wout)],
        out_specs=[tok, tok], operands=(x, s, prm, wout), carry=carry, steps=_grid_steps(T // tm))


def _ffn_backward(x, d, ab, prm, win, wout, sub, g_row, name, carry=None):
    T = x.shape[0]
    tm = min(T, TM_FFN)
    nt = T // tm
    chunk = min(tm, FFN_BWD_CHUNK)

    def body(x_ref, d_ref, ab_ref, prm_ref, win_ref, wout_ref, dh_ref, dwin_ref, dwout_ref, acc_in, acc_out):
        i = pl.program_id(1)

        @pl.when(i == 0)
        def _():
            acc_in[...] = jnp.zeros_like(acc_in)
            acc_out[...] = jnp.zeros_like(acc_out)

        wa, wb, wo = win_ref[0, 0], win_ref[1, 0], wout_ref[0]
        half_gate = 0.5 * _row(prm_ref, 3 * sub + 2)
        das, dbs, ss, hbs, dfss = [], [], [], [], []
        for ck in range(tm // chunk):
            rows = slice(ck * chunk, (ck + 1) * chunk)
            h, _, _ = _modulated(x_ref[rows, :], prm_ref, sub, g_row)
            hbs.append(h.astype(MXU_DTYPE))
            a = ab_ref[0, 0, rows, :].astype(F32)
            b = ab_ref[1, 0, rows, :].astype(F32)
            sg = _sigmoid(a)
            si = a * sg
            dfs = (half_gate * d_ref[rows, :]).astype(MXU_DTYPE)
            ds = _mm_nt(dfs, wo)
            da = (ds * b * (sg * (1.0 + a * (1.0 - sg)))).astype(MXU_DTYPE)
            db = (ds * si).astype(MXU_DTYPE)
            dh_ref[0, rows, :] = (_mm(da, wa) + _mm(db, wb)).astype(SAVE_DTYPE)
            das.append(da)
            dbs.append(db)
            ss.append((si * b).astype(MXU_DTYPE))
            dfss.append(dfs)
        cat = (lambda v: v[0]) if len(das) == 1 else (lambda v: jnp.concatenate(v, axis=0))
        hb = cat(hbs)
        acc_out[...] += _mm_tn(cat(ss), cat(dfss))
        acc_in[0] += _mm_tn(cat(das), hb)
        acc_in[1] += _mm_tn(cat(dbs), hb)

        @pl.when(i == nt - 1)
        def _():
            dwin_ref[0, 0] = acc_in[0].astype(WIRE_DTYPE)
            dwin_ref[1, 0] = acc_in[1].astype(WIRE_DTYPE)
            dwout_ref[0] = acc_out[...].astype(WIRE_DTYPE)

    def steps():
        j, i = pl.program_id(0), pl.program_id(1)
        return (j == 0) & (i == 0), (j == 2) & (i == 0), (j == 3) & (i == nt - 1)

    tok = pl.BlockSpec((tm, D_MODEL), lambda j, i: (i, 0))
    return _launch(
        body, name, grid=(4, nt), semantics=("arbitrary", "arbitrary"),
        out_shape=[jax.ShapeDtypeStruct((4, T, D_MODEL), SAVE_DTYPE),
                   jax.ShapeDtypeStruct(win.shape, WIRE_DTYPE), jax.ShapeDtypeStruct(wout.shape, WIRE_DTYPE)],
        in_specs=[tok, tok, pl.BlockSpec((2, 1, tm, FF_SHARD), lambda j, i: (0, j, i, 0)), _resident(prm),
                  pl.BlockSpec((2, 1, FF_SHARD, D_MODEL), lambda j, i: (0, j, 0, 0)),
                  pl.BlockSpec((1, FF_SHARD, D_MODEL), lambda j, i: (j, 0, 0))],
        out_specs=[pl.BlockSpec((1, tm, D_MODEL), lambda j, i: (j, i, 0)),
                   pl.BlockSpec((2, 1, FF_SHARD, D_MODEL), lambda j, i: (0, j, 0, 0)),
                   pl.BlockSpec((1, FF_SHARD, D_MODEL), lambda j, i: (j, 0, 0))],
        operands=(x, d, ab, prm, win, wout),
        scratch=[pltpu.VMEM((2, FF_SHARD, D_MODEL), F32), pltpu.VMEM((FF_SHARD, D_MODEL), F32)],
        carry=carry, steps=steps)


def _norm_backward(parts, x, d, f, prm, sub, g_row, gate_coef, name, carry=None):
    T = x.shape[0]
    tm = min(T, TM_EW)
    P = parts.shape[0]

    def body(p_ref, x_ref, d_ref, f_ref, prm_ref, dx_ref, sums_ref):
        i = pl.program_id(0)
        dh = p_ref[0].astype(F32)
        for k in range(1, P):
            dh = dh + p_ref[k].astype(F32)
        xv, dv = x_ref[...], d_ref[...]
        scale, g = _row(prm_ref, 3 * sub + 1), _row(prm_ref, g_row)
        r = _rms_scale(xv)
        n0 = xv * r
        dn = dh * (1.0 + scale)
        dn0 = dn * g
        dx_ref[...] = dv + r * (dn0 - n0 * jnp.mean(dn0 * n0, axis=-1, keepdims=True))
        upd = jnp.concatenate([_colsum(dn * n0), _colsum(dh), _colsum(dh * (n0 * g)),
                               gate_coef * _colsum(dv * f_ref[...].astype(F32)), jnp.zeros((4, D_MODEL), F32)], axis=0)

        @pl.when(i == 0)
        def _():
            sums_ref[...] = upd

        @pl.when(i > 0)
        def _():
            sums_ref[...] += upd

    tok = pl.BlockSpec((tm, D_MODEL), lambda i: (i, 0))
    return _launch(
        body, name, grid=(T // tm,), semantics=("arbitrary",),
        out_shape=[jax.ShapeDtypeStruct((T, D_MODEL), F32), jax.ShapeDtypeStruct((8, D_MODEL), F32)],
        in_specs=[pl.BlockSpec((P, tm, D_MODEL), lambda i: (0, i, 0)), tok, tok, tok, _resident(prm)],
        out_specs=[tok, pl.BlockSpec((8, D_MODEL), lambda i: (0, 0))],
        operands=(parts, x, d, f, prm), carry=carry, steps=_grid_steps(T // tm))


def _final_loss(x, target, prm):
    T = x.shape[0]
    tm = min(T, TM_EW)

    def body(x_ref, t_ref, prm_ref, dx_ref, sums_ref):
        i = pl.program_id(0)
        xv = x_ref[...]
        g = _row(prm_ref, ROW_G_FINAL)
        r = _rms_scale(xv)
        n0 = xv * r
        err = n0 * g - t_ref[...]
        dy = err / float(D_MODEL)
        dn0 = dy * g
        dx_ref[...] = r * (dn0 - n0 * jnp.mean(dn0 * n0, axis=-1, keepdims=True))
        loss = 0.5 * jnp.sum(jnp.mean(err * err, axis=-1, keepdims=True), axis=0, keepdims=True)
        upd = jnp.concatenate([_colsum(dy * n0), jnp.broadcast_to(loss, (1, D_MODEL)), jnp.zeros((6, D_MODEL), F32)], axis=0)

        @pl.when(i == 0)
        def _():
            sums_ref[...] = upd

        @pl.when(i > 0)
        def _():
            sums_ref[...] += upd

    tok = pl.BlockSpec((tm, D_MODEL), lambda i: (i, 0))
    return pl.pallas_call(
        body, name="final_loss", grid=(T // tm,),
        out_shape=[jax.ShapeDtypeStruct((T, D_MODEL), F32), jax.ShapeDtypeStruct((8, D_MODEL), F32)],
        in_specs=[tok, tok, pl.BlockSpec(prm.shape, lambda i: (0, 0))],
        out_specs=[tok, pl.BlockSpec((8, D_MODEL), lambda i: (0, 0))],
        compiler_params=_params("arbitrary"),
    )(x, target, prm)


def _ssm_discretise(lam_re_log, lam_im, log_dt):
    lr = -jnp.exp(lam_re_log)
    dt = jnp.exp(log_dt)
    mag = jnp.exp(lr * dt)
    ang = lam_im * dt
    ab_re = mag * jnp.cos(ang)
    ab_im = mag * jnp.sin(ang)
    num_re = ab_re - 1.0
    num_im = ab_im
    den = lr * lr + lam_im * lam_im
    f_re = (num_re * lr + num_im * lam_im) / den
    f_im = (num_im * lr - num_re * lam_im) / den
    return ab_re, ab_im, f_re, f_im


def _ssm_params_forward(lam_re_log, lam_im, log_dt):
    def body(a_ref, b_ref, c_ref, o0, o1, o2, o3):
        outs = _ssm_discretise(a_ref[...], b_ref[...], c_ref[...])
        for o, v in zip((o0, o1, o2, o3), outs):
            o[...] = v

    return pl.pallas_call(body, name="ssm_params_forward",
                          out_shape=[jax.ShapeDtypeStruct(lam_im.shape, F32)] * 4)(lam_re_log, lam_im, log_dt)


def _ssm_params_backward(lam_re_log, lam_im, log_dt, cot):
    def body(a_ref, b_ref, c_ref, g0, g1, g2, g3, o0, o1, o2):
        _, vjp = jax.vjp(_ssm_discretise, a_ref[...], b_ref[...], c_ref[...])
        d0, d1, d2 = vjp((g0[...], g1[...], g2[...], g3[...]))
        o0[...] = d0
        o1[...] = d1
        o2[...] = d2

    return pl.pallas_call(
        body, name="ssm_params_backward",
        out_shape=[jax.ShapeDtypeStruct(lam_im.shape, F32), jax.ShapeDtypeStruct(lam_im.shape, F32),
                   jax.ShapeDtypeStruct(log_dt.shape, F32)])(lam_re_log, lam_im, log_dt, *cot)


def _ssm_dense_forward(srow, b_dense, c_dense):
    def body(srow_ref, bd_ref, cd_ref, bb_ref, ct_ref):
        for j in range(SSM_BLOCKS):
            lanes = slice(j * SSM_BLOCK_STATE, (j + 1) * SSM_BLOCK_STATE)
            f_re, f_im = srow_ref[2:3, lanes], srow_ref[3:4, lanes]
            bb_ref[0, j] = (f_re * bd_ref[0, j] - f_im * bd_ref[1, j]).astype(MXU_DTYPE)
            bb_ref[1, j] = (f_re * bd_ref[1, j] + f_im * bd_ref[0, j]).astype(MXU_DTYPE)
            ct_ref[0, j] = cd_ref[0, j].astype(MXU_DTYPE)
            ct_ref[1, j] = cd_ref[1, j].astype(MXU_DTYPE)

    return pl.pallas_call(body, name="ssm_dense_forward",
                          out_shape=[jax.ShapeDtypeStruct(b_dense.shape, MXU_DTYPE),
                                     jax.ShapeDtypeStruct(c_dense.shape, MXU_DTYPE)],
                          compiler_params=pltpu.CompilerParams(vmem_limit_bytes=VMEM_LIMIT))(srow, b_dense, c_dense)


def _cmul(p, q):
    return p[0] * q[0] - p[1] * q[1], p[0] * q[1] + p[1] * q[0]


def _scan_coefficients(ar, ai, reverse):
    n = ar.shape[1]
    p = {1: (ar, ai)}
    p[2] = _cmul(p[1], p[1])
    p[3] = _cmul(p[2], p[1])
    p[4] = _cmul(p[2], p[2])
    p[5] = _cmul(p[4], p[1])
    p[6] = _cmul(p[4], p[2])
    p[7] = _cmul(p[4], p[3])
    p[8] = _cmul(p[4], p[4])
    rowi = lax.broadcasted_iota(jnp.int32, (SCAN_ROWS, n), 0)
    tiles = []
    for dstep in (1, 2, 4):
        keep = (rowi < SCAN_ROWS - dstep) if reverse else (rowi >= dstep)
        for part in p[dstep]:
            tiles.append(jnp.where(keep, jnp.broadcast_to(part, (SCAN_ROWS, n)), 0.0))
    for comp in (0, 1):
        t = jnp.zeros((SCAN_ROWS, n), F32)
        for rr in range(SCAN_ROWS):
            power = SCAN_ROWS - rr if reverse else rr + 1
            t = jnp.where(rowi == rr, jnp.broadcast_to(p[power][comp], (SCAN_ROWS, n)), t)
        tiles.append(t)
    return tiles


def _load_stack(stack_hbm, dst, sems, base):
    cols = stack_hbm.shape[2]
    cps = [pltpu.make_async_copy(stack_hbm.at[k], dst.at[:, pl.ds(k * cols, cols)], sems.at[base + k])
           for k in range(NDEV)]
    for cp in cps:
        cp.start()
    return cps


def _window_lanes():
    lane = lax.broadcasted_iota(jnp.int32, (1, POOL_WIDTH), 1)
    return jnp.where(lane < 128, 2.0, jnp.where(lane < 256, 4.0, jnp.where(lane < 384, 8.0, 16.0)))


def _gelu(y):
    return 0.5 * y * (1.0 + lax.erf(y * 0.7071067811865476))


def _gelu_grad(y):
    return 0.5 * (1.0 + lax.erf(y * 0.7071067811865476)) + y * jnp.exp(-0.5 * y * y) * 0.3989422804014327


def _mixer_forward(x, prm, w_in_s, w_pu_s, w_glu_s, w_su_s, w_out, pool_w, mvec, srow, bb, ct, carry=None):
    T = x.shape[0]
    tm = min(T, TM_MIX)
    nt = T // tm
    n_tiles = tm // SCAN_ROWS

    def body(x_ref, prm_ref, w_in_h, w_pu_h, w_glu_h, w_su_h, w_out_h, pw_ref, mv_ref, srow_ref, bb, ct,
             x2_ref, mo_ref, z_ref, sre_ref, sim_ref, zp_ref, q_ref, yp_ref, yss_ref, vg_ref, ys_ref,
             w_in, w_pu, w_glu, w_su, w_o, coef, carry, hist, bu, sems):
        i = pl.program_id(0)

        @pl.when(i == 0)
        def _():
            cps = (_load_stack(w_in_h, w_in, sems, 0) + _load_stack(w_pu_h, w_pu, sems, 8)
                   + _load_stack(w_glu_h, w_glu, sems, 16) + _load_stack(w_su_h, w_su, sems, 24))
            cps.append(pltpu.make_async_copy(w_out_h, w_o, sems.at[32]))
            cps[-1].start()
            for j in range(SSM_BLOCKS):
                lanes = slice(j * SSM_BLOCK_STATE, (j + 1) * SSM_BLOCK_STATE)
                for k, t in enumerate(_scan_coefficients(srow_ref[0:1, lanes], srow_ref[1:2, lanes], False)):
                    coef[j, k] = t
            carry[...] = jnp.zeros_like(carry)
            hist[...] = jnp.zeros_like(hist)
            for cp in cps:
                cp.wait()

        xv = x_ref[...]
        h, _, _ = _modulated(xv, prm_ref, 1, ROW_G_MIX)
        z = _mm(h, w_in[...])
        z_ref[...] = z.astype(SAVE_DTYPE)
        u_pool, u_ssm = z[:, 0:512], z[:, 512:1024]
        gl_pool, gl_ssm = z[:, 1024:2048], z[:, 2048:3072]

        ext = jnp.concatenate([hist[...], u_pool], axis=0)
        w2 = ext + pltpu.roll(ext, 1, 0)
        w4 = w2[:, 128:] + pltpu.roll(w2[:, 128:], 2, 0)
        w8 = w4[:, 128:] + pltpu.roll(w4[:, 128:], 4, 0)
        w16 = w8[:, 128:] + pltpu.roll(w8[:, 128:], 8, 0)
        wsum = jnp.concatenate([w2[POOL_HALO:, :128], w4[POOL_HALO:, :128], w8[POOL_HALO:, :128], w16[POOL_HALO:]], axis=1)
        hist[...] = u_pool[tm - POOL_HALO:, :]
        t1 = (lax.broadcasted_iota(jnp.int32, (tm, 1), 0) + (i * tm + 1)).astype(F32)
        zp = wsum / jnp.minimum(t1, _window_lanes()) - u_pool
        zp_ref[...] = zp.astype(SAVE_DTYPE)
        q = jnp.concatenate([_mm(zp[:, k * 128:(k + 1) * 128], pw_ref[k]) for k in range(4)], axis=1)
        q = q + mv_ref[ROW_POOL_B:ROW_POOL_B + 1, 0:512]
        q_ref[...] = q.astype(SAVE_DTYPE)
        y_pool = _mm(q * mv_ref[ROW_POOL_SCALE:ROW_POOL_SCALE + 1, 0:512], w_pu[...])
        yp_ref[...] = y_pool.astype(SAVE_DTYPE)

        y_blocks = []
        for j in range(SSM_BLOCKS):
            lanes = pl.ds(j * SSM_BLOCK_STATE, SSM_BLOCK_STATE)
            ub = u_ssm[:, j * 128:(j + 1) * 128].astype(MXU_DTYPE)
            bu[0] = _mm(ub, bb[0, j])
            bu[1] = _mm(ub, bb[1, j])
            a1r, a1i, a2r, a2i, a4r, a4i, pr, pi = [coef[j, k] for k in range(8)]

            def step(tt, c, lanes=lanes, a1r=a1r, a1i=a1i, a2r=a2r, a2i=a2i, a4r=a4r, a4i=a4i, pr=pr, pi=pi):
                cr, ci = c
                rows = pl.ds(pl.multiple_of(tt * SCAN_ROWS, SCAN_ROWS), SCAN_ROWS)
                xr, xi = bu[0, rows, :], bu[1, rows, :]
                for dstep, kr, ki in ((1, a1r, a1i), (2, a2r, a2i), (4, a4r, a4i)):
                    sr, si = pltpu.roll(xr, dstep, 0), pltpu.roll(xi, dstep, 0)
                    xr, xi = xr + kr * sr - ki * si, xi + kr * si + ki * sr
                xr, xi = xr + pr * cr - pi * ci, xi + pr * ci + pi * cr
                sre_ref[rows, lanes] = xr
                sim_ref[rows, lanes] = xi
                return (jnp.broadcast_to(xr[SCAN_ROWS - 1:SCAN_ROWS, :], xr.shape),
                        jnp.broadcast_to(xi[SCAN_ROWS - 1:SCAN_ROWS, :], xi.shape))

            cr, ci = lax.fori_loop(0, n_tiles, step, (carry[j, 0], carry[j, 1]))
            carry[j, 0] = cr
            carry[j, 1] = ci
            y_blocks.append(_mm(sre_ref[:, lanes], ct[0, j]) - _mm(sim_ref[:, lanes], ct[1, j]))
        yss = jnp.concatenate(y_blocks, axis=1) + mv_ref[ROW_SSM_D:ROW_SSM_D + 1, 0:512] * u_ssm
        yss_ref[...] = yss.astype(SAVE_DTYPE)
        vg = _mm(_gelu(yss), w_glu[...]) + mv_ref[ROW_B_GLU:ROW_B_GLU + 1, :]
        vg_ref[...] = vg.astype(SAVE_DTYPE)
        y_ssm = _mm(vg[:, 0:512] * _sigmoid(vg[:, 512:1024]), w_su[...])
        ys_ref[...] = y_ssm.astype(SAVE_DTYPE)

        merged = _sigmoid(gl_pool) * y_pool + _sigmoid(gl_ssm) * y_ssm
        mo = _mm(merged, w_o[...])
        mo_ref[...] = mo.astype(SAVE_DTYPE)
        x2_ref[...] = xv + _row(prm_ref, 5) * mo

    def tok(width):
        return pl.BlockSpec((tm, width), lambda i: (i, 0))

    hbm = _HBM
    widths = (D_MODEL, D_MODEL, IN_WIDTH, N_STATE, N_STATE, 512, 512, D_MODEL, 512, D_MODEL, D_MODEL)
    dtypes = (F32, SAVE_DTYPE, SAVE_DTYPE, F32, F32) + (SAVE_DTYPE,) * 6
    return _launch(
        body, "mixer_forward", grid=(nt,), semantics=("arbitrary",), carry=carry, steps=_grid_steps(nt),
        out_shape=[jax.ShapeDtypeStruct((T, w), dt) for w, dt in zip(widths, dtypes)],
        in_specs=[tok(D_MODEL), _resident(prm), hbm, hbm, hbm, hbm, hbm, _resident(pool_w), _resident(mvec),
                  _resident(srow), _resident(bb), _resident(ct)],
        out_specs=[tok(w) for w in widths],
        operands=(x, prm, w_in_s, w_pu_s, w_glu_s, w_su_s, w_out, pool_w, mvec, srow, bb, ct),
        scratch=[
            pltpu.VMEM((D_MODEL, IN_WIDTH), MXU_DTYPE), pltpu.VMEM((512, D_MODEL), MXU_DTYPE),
            pltpu.VMEM((512, D_MODEL), MXU_DTYPE), pltpu.VMEM((512, D_MODEL), MXU_DTYPE),
            pltpu.VMEM((D_MODEL, D_MODEL), MXU_DTYPE),
            pltpu.VMEM((SSM_BLOCKS, 8, SCAN_ROWS, SSM_BLOCK_STATE), F32),
            pltpu.VMEM((SSM_BLOCKS, 2, SCAN_ROWS, SSM_BLOCK_STATE), F32),
            pltpu.VMEM((POOL_HALO, POOL_WIDTH), F32),
            pltpu.VMEM((2, tm, SSM_BLOCK_STATE), F32),
            pltpu.SemaphoreType.DMA((33,)),
        ])


def _mixer_backward(d2, prm, saved, w_pu_s, w_glu_s, w_su_s, w_out, pool_w, mvec, srow, bb, ct, carry=None):
    z, s_re, s_im, zp, q, y_pool, yss, vg, y_ssm = saved
    T = d2.shape[0]
    tm = min(T, TM_MIX_BWD)
    nt = T // tm
    n_tiles = tm // SCAN_ROWS

    def body(d_ref, prm_ref, z_ref, sre_ref, sim_ref, zp_ref, q_ref, yp_ref, yss_ref, vg_ref, ys_ref,
             w_pu_h, w_glu_h, w_su_h, w_out_h, pw_ref, mv_ref, srow_ref, bb, ct,
             dz_ref, dwo_h, dwpu_h, dwglu_h, dwsu_h, dpw_h, dbb_h, dct_h, vsum_h, da_h,
             w_pu, w_glu, w_su, w_o, pwb, coef, carry, hist, dre, lam,
             a_wo, a_wpu, a_wglu, a_wsu, a_pw, a_bb, a_ct, a_vs, a_da, st_wo, st_up, sems):
        i = pl.program_id(0)
        tile = nt - 1 - i

        @pl.when(i == 0)
        def _():
            cps = (_load_stack(w_pu_h, w_pu, sems, 0) + _load_stack(w_glu_h, w_glu, sems, 8)
                   + _load_stack(w_su_h, w_su, sems, 16))
            cps.append(pltpu.make_async_copy(w_out_h, w_o, sems.at[24]))
            cps[-1].start()
            pwb[...] = pw_ref[...].astype(MXU_DTYPE)
            for j in range(SSM_BLOCKS):
                lanes = slice(j * SSM_BLOCK_STATE, (j + 1) * SSM_BLOCK_STATE)
                for k, t in enumerate(_scan_coefficients(srow_ref[0:1, lanes], srow_ref[1:2, lanes], True)):
                    coef[j, k] = t
            for acc in (carry, hist, a_wo, a_wpu, a_wglu, a_wsu, a_pw, a_bb, a_ct, a_vs, a_da):
                acc[...] = jnp.zeros_like(acc)
            for cp in cps:
                cp.wait()

        dv = d_ref[...]
        zt = z_ref[...].astype(F32)
        u_ssm, gl_pool, gl_ssm = zt[:, 512:1024], zt[:, 1024:2048], zt[:, 2048:3072]
        y_p, y_s = yp_ref[...].astype(F32), ys_ref[...].astype(F32)
        sgp, sgs = _sigmoid(gl_pool), _sigmoid(gl_ssm)
        dmo = (_row(prm_ref, 5) * dv).astype(MXU_DTYPE)
        a_wo[...] += _mm_tn(sgp * y_p + sgs * y_s, dmo)
        dmerged = _mm_nt(dmo, w_o[...])
        dy_pool = dmerged * sgp
        dgl_pool = dmerged * y_p * (sgp * (1.0 - sgp))
        dy_ssm = dmerged * sgs
        dgl_ssm = dmerged * y_s * (sgs * (1.0 - sgs))

        scale = mv_ref[ROW_POOL_SCALE:ROW_POOL_SCALE + 1, 0:512]
        qv, zpv = q_ref[...].astype(F32), zp_ref[...]
        a_wpu[...] += _mm_tn(qv * scale, dy_pool)
        dp = _mm_nt(dy_pool, w_pu[...])
        dq = dp * scale
        a_vs[0:1, 0:512] += _colsum(dp * qv)
        a_vs[1:2, 0:512] += _colsum(dq)
        dzp_blocks = []
        for k in range(4):
            lanes = slice(k * 128, (k + 1) * 128)
            dzp_blocks.append(_mm_nt(dq[:, lanes], pwb[k]))
            a_pw[k] += _mm_tn(zpv[:, lanes], dq[:, lanes])
        dzp = jnp.concatenate(dzp_blocks, axis=1)
        t1 = (lax.broadcasted_iota(jnp.int32, (tm, 1), 0) + (tile * tm + 1)).astype(F32)
        gs = dzp / jnp.minimum(t1, _window_lanes())
        n_ext = tm + POOL_HALO
        ext = jnp.concatenate([gs, hist[...]], axis=0)
        v2 = ext + pltpu.roll(ext, n_ext - 1, 0)
        v4 = v2[:, 128:] + pltpu.roll(v2[:, 128:], n_ext - 2, 0)
        v8 = v4[:, 128:] + pltpu.roll(v4[:, 128:], n_ext - 4, 0)
        v16 = v8[:, 128:] + pltpu.roll(v8[:, 128:], n_ext - 8, 0)
        msum = jnp.concatenate([v2[:tm, :128], v4[:tm, :128], v8[:tm, :128], v16[:tm]], axis=1)
        hist[...] = gs[0:POOL_HALO, :]
        du_pool = msum - dzp

        vgv = vg_ref[...].astype(F32)
        val, gate = vgv[:, 0:512], vgv[:, 512:1024]
        sgg = _sigmoid(gate)
        a_wsu[...] += _mm_tn(val * sgg, dy_ssm)
        do = _mm_nt(dy_ssm, w_su[...])
        dvg = jnp.concatenate([do * sgg, do * val * (sgg * (1.0 - sgg))], axis=1)
        a_vs[3:4, :] += _colsum(dvg)
        yv = yss_ref[...].astype(F32)
        a_wglu[...] += _mm_tn(_gelu(yv), dvg)
        dyss = _mm_nt(dvg, w_glu[...]) * _gelu_grad(yv)
        a_vs[2:3, 0:512] += _colsum(dyss * u_ssm)
        du_blocks = []
        for j in range(SSM_BLOCKS):
            lanes = pl.ds(j * SSM_BLOCK_STATE, SSM_BLOCK_STATE)
            in_lanes = slice(j * 128, (j + 1) * 128)
            dyb = dyss[:, in_lanes].astype(MXU_DTYPE)
            ub = u_ssm[:, in_lanes].astype(MXU_DTYPE)
            dre[0] = _mm_nt(dyb, ct[0, j])
            dre[1] = -_mm_nt(dyb, ct[1, j])
            a_ct[0, j] += _mm_tn(sre_ref[:, lanes], dyb)
            a_ct[1, j] -= _mm_tn(sim_ref[:, lanes], dyb)
            a1r, a1i, a2r, a2i, a4r, a4i, pr, pi = [coef[j, k] for k in range(8)]
            rowi = lax.broadcasted_iota(jnp.int32, (SCAN_ROWS, SSM_BLOCK_STATE), 0)

            def step(tt, c, lanes=lanes, a1r=a1r, a1i=a1i, a2r=a2r, a2i=a2i, a4r=a4r, a4i=a4i, pr=pr, pi=pi, rowi=rowi):
                cr, ci, acc_r, acc_i = c
                rows = pl.ds(pl.multiple_of((n_tiles - 1 - tt) * SCAN_ROWS, SCAN_ROWS), SCAN_ROWS)
                xr, xi = dre[0, rows, :], dre[1, rows, :]
                for dstep, kr, ki in ((1, a1r, a1i), (2, a2r, a2i), (4, a4r, a4i)):
                    sr, si = pltpu.roll(xr, SCAN_ROWS - dstep, 0), pltpu.roll(xi, SCAN_ROWS - dstep, 0)
                    xr, xi = xr + kr * sr + ki * si, xi + kr * si - ki * sr
                xr, xi = xr + pr * cr + pi * ci, xi + pr * ci - pi * cr
                lam[0, rows, :] = xr
                lam[1, rows, :] = xi
                nr = jnp.where(rowi == SCAN_ROWS - 1, cr, pltpu.roll(xr, SCAN_ROWS - 1, 0))
                ni = jnp.where(rowi == SCAN_ROWS - 1, ci, pltpu.roll(xi, SCAN_ROWS - 1, 0))
                s_r, s_i = sre_ref[rows, lanes], sim_ref[rows, lanes]
                acc_r = acc_r + nr * s_r + ni * s_i
                acc_i = acc_i + ni * s_r - nr * s_i
                return (jnp.broadcast_to(xr[0:1, :], xr.shape), jnp.broadcast_to(xi[0:1, :], xi.shape), acc_r, acc_i)

            cr, ci, acc_r, acc_i = lax.fori_loop(0, n_tiles, step, (carry[j, 0], carry[j, 1], a_da[0, j], a_da[1, j]))
            carry[j, 0] = cr
            carry[j, 1] = ci
            a_da[0, j] = acc_r
            a_da[1, j] = acc_i
            lr_b, li_b = lam[0].astype(MXU_DTYPE), lam[1].astype(MXU_DTYPE)
            a_bb[0, j] += _mm_tn(ub, lr_b)
            a_bb[1, j] += _mm_tn(ub, li_b)
            du_blocks.append(_mm_nt(lr_b, bb[0, j]) + _mm_nt(li_b, bb[1, j]))
        du_ssm = jnp.concatenate(du_blocks, axis=1) + dyss * mv_ref[ROW_SSM_D:ROW_SSM_D + 1, 0:512]
        dz_ref[...] = jnp.concatenate([du_pool, du_ssm, dgl_pool, dgl_ssm], axis=1).astype(SAVE_DTYPE)

        @pl.when(i == nt - 1)
        def _():
            rows = D_MODEL // NDEV
            for k in range(NDEV):
                st_wo[k] = a_wo[k * rows:(k + 1) * rows, :].astype(WIRE_DTYPE)
                for a, acc in enumerate((a_wpu, a_wglu, a_wsu)):
                    st_up[a, k] = acc[:, k * 128:(k + 1) * 128].astype(WIRE_DTYPE)
            outs = ((st_wo, dwo_h), (st_up.at[0], dwpu_h), (st_up.at[1], dwglu_h), (st_up.at[2], dwsu_h),
                    (a_pw, dpw_h), (a_bb, dbb_h), (a_ct, dct_h), (a_vs, vsum_h), (a_da, da_h))
            cps = [pltpu.make_async_copy(src, dst, sems.at[k]) for k, (src, dst) in enumerate(outs)]
            for cp in cps:
                cp.start()
            for cp in cps:
                cp.wait()

    def tok(width):
        return pl.BlockSpec((tm, width), lambda i: (nt - 1 - i, 0))

    hbm = _HBM
    acc_shapes = [(D_MODEL, D_MODEL), (512, D_MODEL), (512, D_MODEL), (512, D_MODEL), (4, 128, 128),
                  (2, SSM_BLOCKS, 128, SSM_BLOCK_STATE), (2, SSM_BLOCKS, SSM_BLOCK_STATE, 128), (8, D_MODEL),
                  (2, SSM_BLOCKS, SCAN_ROWS, SSM_BLOCK_STATE)]
    stack_out = [jax.ShapeDtypeStruct((NDEV, D_MODEL // NDEV, D_MODEL), WIRE_DTYPE)] \
        + [jax.ShapeDtypeStruct((NDEV, 512, 128), WIRE_DTYPE)] * 3
    return _launch(
        body, "mixer_backward", grid=(nt,), semantics=("arbitrary",), carry=carry, steps=_grid_steps(nt),
        out_shape=[jax.ShapeDtypeStruct((T, IN_WIDTH), SAVE_DTYPE)] + stack_out
        + [jax.ShapeDtypeStruct(s, F32) for s in acc_shapes[4:]],
        in_specs=[tok(D_MODEL), _resident(prm), tok(IN_WIDTH), tok(N_STATE), tok(N_STATE), tok(512), tok(512),
                  tok(D_MODEL), tok(512), tok(D_MODEL), tok(D_MODEL), hbm, hbm, hbm, hbm, _resident(pool_w),
                  _resident(mvec), _resident(srow), _resident(bb), _resident(ct)],
        out_specs=[tok(IN_WIDTH)] + [hbm] * len(acc_shapes),
        operands=(d2, prm, z, s_re, s_im, zp, q, y_pool, yss, vg, y_ssm, w_pu_s, w_glu_s, w_su_s, w_out, pool_w, mvec,
                  srow, bb, ct),
        scratch=[
            pltpu.VMEM((512, D_MODEL), MXU_DTYPE), pltpu.VMEM((512, D_MODEL), MXU_DTYPE),
            pltpu.VMEM((512, D_MODEL), MXU_DTYPE), pltpu.VMEM((D_MODEL, D_MODEL), MXU_DTYPE),
            pltpu.VMEM((4, 128, 128), MXU_DTYPE),
            pltpu.VMEM((SSM_BLOCKS, 8, SCAN_ROWS, SSM_BLOCK_STATE), F32),
            pltpu.VMEM((SSM_BLOCKS, 2, SCAN_ROWS, SSM_BLOCK_STATE), F32),
            pltpu.VMEM((POOL_HALO, POOL_WIDTH), F32),
            pltpu.VMEM((2, tm, SSM_BLOCK_STATE), F32), pltpu.VMEM((2, tm, SSM_BLOCK_STATE), F32),
        ] + [pltpu.VMEM(s, F32) for s in acc_shapes]
        + [pltpu.VMEM((NDEV, D_MODEL // NDEV, D_MODEL), WIRE_DTYPE), pltpu.VMEM((3, NDEV, 512, 128), WIRE_DTYPE),
           pltpu.SemaphoreType.DMA((25,))])


def _mixer_in_backward(x, dz, prm, w_in_s):
    T = x.shape[0]
    tm = min(T, TM_MIX)
    nt = T // tm
    cols = IN_WIDTH // NDEV

    def body(x_ref, dz_ref, prm_ref, w_in_h, dh_ref, dw_ref, w_in, acc, sems):
        i = pl.program_id(0)

        @pl.when(i == 0)
        def _():
            cps = _load_stack(w_in_h, w_in, sems, 0)
            acc[...] = jnp.zeros_like(acc)
            for cp in cps:
                cp.wait()

        h, _, _ = _modulated(x_ref[...], prm_ref, 1, ROW_G_MIX)
        dzb = dz_ref[...].astype(MXU_DTYPE)
        dh_ref[0] = _mm_nt(dzb, w_in[...]).astype(SAVE_DTYPE)
        acc[...] += _mm_tn(h, dzb)

        @pl.when(i == nt - 1)
        def _():
            for k in range(NDEV):
                dw_ref[k] = acc[:, k * cols:(k + 1) * cols].astype(WIRE_DTYPE)

    return pl.pallas_call(
        body, name="mixer_in_backward", grid=(nt,),
        out_shape=[jax.ShapeDtypeStruct((1, T, D_MODEL), SAVE_DTYPE), jax.ShapeDtypeStruct((NDEV, D_MODEL, cols), WIRE_DTYPE)],
        in_specs=[pl.BlockSpec((tm, D_MODEL), lambda i: (i, 0)), pl.BlockSpec((tm, IN_WIDTH), lambda i: (i, 0)),
                  pl.BlockSpec(prm.shape, lambda i: (0, 0)), pl.BlockSpec(memory_space=pl.ANY)],
        out_specs=[pl.BlockSpec((1, tm, D_MODEL), lambda i: (0, i, 0)),
                   pl.BlockSpec((NDEV, D_MODEL, cols), lambda i: (0, 0, 0))],
        scratch_shapes=[pltpu.VMEM((D_MODEL, IN_WIDTH), MXU_DTYPE), pltpu.VMEM((D_MODEL, IN_WIDTH), F32),
                        pltpu.SemaphoreType.DMA((8,))],
        compiler_params=_params("arbitrary"),
    )(x, dz, prm, w_in_s)


def _ssm_dense_backward(dbb, da, srow, b_dense):
    def body(dbb_ref, da_ref, srow_ref, bd_ref, db_ref, df_ref):
        df_re, df_im = [], []
        da_re = [_colsum(da_ref[0, j]) for j in range(SSM_BLOCKS)]
        da_im = [_colsum(da_ref[1, j]) for j in range(SSM_BLOCKS)]
        for j in range(SSM_BLOCKS):
            lanes = slice(j * SSM_BLOCK_STATE, (j + 1) * SSM_BLOCK_STATE)
            f_re, f_im = srow_ref[2:3, lanes], srow_ref[3:4, lanes]
            g_re, g_im = dbb_ref[0, j], dbb_ref[1, j]
            b_re, b_im = bd_ref[0, j], bd_ref[1, j]
            db_ref[0, j] = f_re * g_re + f_im * g_im
            db_ref[1, j] = f_re * g_im - f_im * g_re
            df_re.append(_colsum(g_re * b_re + g_im * b_im))
            df_im.append(_colsum(g_im * b_re - g_re * b_im))
        df_ref[...] = jnp.concatenate([jnp.concatenate(df_re, axis=1), jnp.concatenate(df_im, axis=1),
                                       jnp.concatenate(da_re, axis=1), jnp.concatenate(da_im, axis=1),
                                       jnp.zeros((4, N_STATE), F32)], axis=0)

    return pl.pallas_call(body, name="ssm_dense_backward",
                          out_shape=[jax.ShapeDtypeStruct(b_dense.shape, F32), jax.ShapeDtypeStruct((8, N_STATE), F32)],
                          compiler_params=pltpu.CompilerParams(vmem_limit_bytes=VMEM_LIMIT))(dbb, da, srow, b_dense)


def _adamw_update(w, g, m, v):
    m = ADAM_B1 * m + (1.0 - ADAM_B1) * g
    v = ADAM_B2 * v + (1.0 - ADAM_B2) * (g * g)
    m_hat = m / (1.0 - ADAM_B1 ** ADAM_STEP)
    v_hat = v / (1.0 - ADAM_B2 ** ADAM_STEP)
    delta = -ADAM_LR * (m_hat / (jnp.sqrt(v_hat) + ADAM_EPS) + ADAM_WD * w)
    return delta, m, v


def _adam_rows(shape):
    rows, cols = shape
    tr = rows
    while tr * cols * 4 > (1 << 20) and tr % 16 == 0:
        tr //= 2
    return tr


def _adam_sharded(w, m, v, land, order, name):
    R, C = w.shape
    tr = _adam_rows((R, C))

    def body(w_ref, m_ref, v_ref, land_ref, order_ref, g_ref, d_ref, mo_ref, vo_ref):
        g = land_ref[0].astype(F32)
        for b in range(1, NDEV):
            g = g + land_ref[b].astype(F32)
        g_ref[...] = g
        d_ref[...], mo_ref[...], vo_ref[...] = _adamw_update(w_ref[...], g, m_ref[...], v_ref[...])

    blk = pl.BlockSpec((tr, C), lambda i: (i, 0))
    return pl.pallas_call(
        body, name=name, grid=(R // tr,),
        out_shape=[jax.ShapeDtypeStruct((R, C), F32)] * 4,
        in_specs=[blk, blk, blk, pl.BlockSpec((NDEV, tr, C), lambda i: (0, i, 0)), _HBM],
        out_specs=[blk] * 4,
        compiler_params=_params("arbitrary"),
    )(w, m, v, land, order)


def _adam_ada(w, m, v, sc_all, dmod_cols):
    R, C = w.shape
    tr = 256

    def body(w_ref, m_ref, v_ref, sc_ref, dm_ref, g_ref, d_ref, mo_ref, vo_ref):
        g = _mm_tn(sc_ref[...], dm_ref[...])
        g_ref[...] = g
        d_ref[...], mo_ref[...], vo_ref[...] = _adamw_update(w_ref[...], g, m_ref[...], v_ref[...])

    blk = pl.BlockSpec((tr, C), lambda i: (i, 0))
    return pl.pallas_call(
        body, name="adam_w_ada", grid=(R // tr,),
        out_shape=[jax.ShapeDtypeStruct((R, C), F32)] * 4,
        in_specs=[blk, blk, blk, pl.BlockSpec((8, tr), lambda i: (0, i)), pl.BlockSpec((8, C), lambda i: (0, 0))],
        out_specs=[blk] * 4,
        compiler_params=_params("arbitrary"),
    )(w, m, v, sc_all, dmod_cols)


def _adam_small(w, g, m, v):
    def body(w_ref, g_ref, m_ref, v_ref, d_ref, mo_ref, vo_ref):
        d_ref[...], mo_ref[...], vo_ref[...] = _adamw_update(w_ref[...], g_ref[...], m_ref[...], v_ref[...])

    return pl.pallas_call(body, name="adam_small", out_shape=[jax.ShapeDtypeStruct(w.shape, F32)] * 3,
                          compiler_params=pltpu.CompilerParams(vmem_limit_bytes=VMEM_LIMIT))(w, g, m, v)


def _block_diag_in(b):
    bt = jnp.transpose(b, (0, 2, 1)).reshape(SSM_BLOCKS, 8, SSM_GROUP, SSM_STATE)
    eye = jnp.eye(8, dtype=bool)[None, :, None, :, None]
    return jnp.where(eye, bt[:, :, :, None, :], 0.0).reshape(SSM_BLOCKS, 128, SSM_BLOCK_STATE)


def _block_diag_out(c):
    ct = jnp.transpose(c, (0, 2, 1)).reshape(SSM_BLOCKS, 8, SSM_STATE, SSM_GROUP)
    eye = jnp.eye(8, dtype=bool)[None, :, None, :, None]
    return jnp.where(eye, ct[:, :, :, None, :], 0.0).reshape(SSM_BLOCKS, SSM_BLOCK_STATE, 128)


def _diag_blocks(dense, rows, cols):
    d5 = dense.reshape(SSM_BLOCKS, 8, rows, 8, cols)
    return jnp.stack([d5[:, a, :, a, :] for a in range(8)], axis=1).reshape(32, rows, cols)


def _pack_small(ada_vec, parts):
    rest = jnp.concatenate([parts[n].reshape(-1) for n, _ in SMALL_PARAMS])
    rest = jnp.pad(rest, (0, NDEV * REST_ROWS * 128 - SMALL_TOTAL)).reshape(NDEV, REST_ROWS, 128)
    return jnp.concatenate([ada_vec.reshape(NDEV, ADA_ROWS, 128), rest,
                            jnp.zeros((NDEV, PACK_ROWS - ADA_ROWS - REST_ROWS, 128), F32)], axis=1)


def _unpack_small(pack, shapes):
    ada_vec = pack[:, :ADA_ROWS].reshape(-1)
    rest = pack[:, ADA_ROWS:ADA_ROWS + REST_ROWS].reshape(-1)
    out, off = {}, 0
    for n, size in SMALL_PARAMS:
        out[n] = rest[off:off + size].reshape(shapes[n])
        off += size
    return ada_vec, out


WEIGHT_ORDER = ('w_ada', 'b_ada', 'g_ffn1', 'w_ffn1_in', 'w_ffn1_out', 'g_mix', 'w_in', 'pool_w', 'pool_b',
                'pool_scale', 'w_pool_up', 'ssm_lam_re_log', 'ssm_lam_im', 'ssm_log_dt', 'ssm_b_re', 'ssm_b_im',
                'ssm_c_re', 'ssm_c_im', 'ssm_d', 'w_glu', 'b_glu', 'w_ssm_up', 'w_out', 'g_ffn2', 'w_ffn2_in',
                'w_ffn2_out', 'g_final')
GATHERED = ('w_ffn1_in', 'w_ffn1_out', 'w_in', 'w_pool_up', 'w_glu', 'w_ssm_up', 'w_out', 'w_ffn2_in', 'w_ffn2_out')
TRANSPOSED = ('w_ffn1_in', 'w_ffn2_in')


def kernel(x, c, w_ada, b_ada, g_ffn1, w_ffn1_in, w_ffn1_out, g_mix, w_in, pool_w, pool_b, pool_scale, w_pool_up, ssm_lam_re_log, ssm_lam_im, ssm_log_dt, ssm_b_re, ssm_b_im, ssm_c_re, ssm_c_im, ssm_d, w_glu, b_glu, w_ssm_up, w_out, g_ffn2, w_ffn2_in, w_ffn2_out, g_final, loss_target, m_w_ada, m_b_ada, m_g_ffn1, m_w_ffn1_in, m_w_ffn1_out, m_g_mix, m_w_in, m_pool_w, m_pool_b, m_pool_scale, m_w_pool_up, m_ssm_lam_re_log, m_ssm_lam_im, m_ssm_log_dt, m_ssm_b_re, m_ssm_b_im, m_ssm_c_re, m_ssm_c_im, m_ssm_d, m_w_glu, m_b_glu, m_w_ssm_up, m_w_out, m_g_ffn2, m_w_ffn2_in, m_w_ffn2_out, m_g_final, v_w_ada, v_b_ada, v_g_ffn1, v_w_ffn1_in, v_w_ffn1_out, v_g_mix, v_w_in, v_pool_w, v_pool_b, v_pool_scale, v_w_pool_up, v_ssm_lam_re_log, v_ssm_lam_im, v_ssm_log_dt, v_ssm_b_re, v_ssm_b_im, v_ssm_c_re, v_ssm_c_im, v_ssm_d, v_w_glu, v_b_glu, v_w_ssm_up, v_w_out, v_g_ffn2, v_w_ffn2_in, v_w_ffn2_out, v_g_final):
    args = locals()
    W = {n: args[n] for n in WEIGHT_ORDER}
    M = {n: args["m_" + n] for n in WEIGHT_ORDER}
    V = {n: args["v_" + n] for n in WEIGHT_ORDER}
    shapes = {n: W[n].shape for n in WEIGHT_ORDER}
    xt, tgt = x[0], loss_target[0]

    def local(tree, n):
        return jnp.swapaxes(tree[n][0], 0, 1) if n in TRANSPOSED else tree[n][0]

    def as_output(n, a):
        return (jnp.swapaxes(a, 0, 1) if n in TRANSPOSED else a)[None]

    shard = dict(zip(GATHERED, _cast_shards([local(W, n) for n in GATHERED])))
    stacks = {}

    def gather(names):
        return _Gather([shard[n] for n in names])

    def gathered(names, results):
        stacks.update(zip(names, results))

    ffn1_w, ffn2_w = ('w_ffn1_in', 'w_ffn1_out'), ('w_ffn2_in', 'w_ffn2_out')
    mix_w = ('w_in', 'w_pool_up', 'w_glu', 'w_ssm_up', 'w_out')
    mod_cols, sc_all, *res = _ada_forward(c, W['w_ada'][0], b_ada.reshape(NDEV, -1), gather(ffn1_w[:1]))
    gathered(ffn1_w[:1], res)
    win1 = stacks['w_ffn1_in'].reshape(2, 4, FF_SHARD, D_MODEL)
    prm = jnp.concatenate([mod_cols.reshape(9, D_MODEL), g_ffn1, g_mix, g_ffn2, g_final[None], jnp.zeros((3, D_MODEL), F32)], axis=0)
    pad512 = jnp.zeros((1, D_MODEL - 512), F32)
    mvec = jnp.concatenate([jnp.concatenate([pool_b, pad512], axis=1), jnp.concatenate([pool_scale, pad512], axis=1),
                            jnp.concatenate([ssm_d, pad512], axis=1), b_glu, jnp.zeros((4, D_MODEL), F32)], axis=0)
    log_dt_col = ssm_log_dt[0][:, None]
    coeffs = _ssm_params_forward(ssm_lam_re_log[0], ssm_lam_im[0], log_dt_col)
    srow = jnp.stack([t.reshape(N_STATE) for t in coeffs], axis=0)
    b_dense = jnp.stack([_block_diag_in(ssm_b_re[0]), _block_diag_in(ssm_b_im[0])], axis=0)
    c_dense = jnp.stack([_block_diag_out(ssm_c_re[0]), _block_diag_out(ssm_c_im[0])], axis=0)
    bb, ct = _ssm_dense_forward(srow, b_dense, c_dense)
    pw = pool_w[0]

    next_w = ffn1_w[1:] + mix_w[:1]
    ab1, s1, *res = _ffn_hidden(xt, prm, win1, 0, ROW_G_FFN1, "ffn1_hidden", gather(next_w))
    gathered(next_w, res)
    wout1 = stacks['w_ffn1_out'].reshape(4, FF_SHARD, D_MODEL)
    x1, f1, *res = _ffn_out(xt, s1, prm, wout1, 0, "ffn1_out", gather(mix_w[1:]))
    gathered(mix_w[1:], res)
    w_out_full = stacks['w_out'].reshape(D_MODEL, D_MODEL)
    res = _mixer_forward(x1, prm, stacks['w_in'], stacks['w_pool_up'], stacks['w_glu'], stacks['w_ssm_up'],
                         w_out_full, pw, mvec, srow, bb, ct, gather(ffn2_w))
    x2, mo, saved = res[0], res[1], res[2:11]
    gathered(ffn2_w, res[11:])
    win2 = stacks['w_ffn2_in'].reshape(2, 4, FF_SHARD, D_MODEL)
    wout2 = stacks['w_ffn2_out'].reshape(4, FF_SHARD, D_MODEL)
    x3, f3, ab3 = _ffn_forward(x2, prm, win2, wout2, 2, ROW_G_FFN2, "ffn2_forward")
    d3, fin = _final_loss(x3, tgt, prm)
    loss = lax.psum(fin[1, 0], ("x", "y", "c"))

    lands = {}

    def scatter(grads):
        names = list(grads)
        return _Scatter([grads[n][0] for n in names], [grads[n][1] for n in names], [local(W, n).shape for n in names])

    def scattered(grads, results):
        lands.update(zip(grads, results))

    parts3, dwin2, dwout2 = _ffn_backward(x2, d3, ab3, prm, win2, wout2, 2, ROW_G_FFN2, "ffn2_backward")
    d2, sums3 = _norm_backward(parts3, x2, d3, f3, prm, 2, ROW_G_FFN2, 0.5, "ffn2_norm_backward")
    g_ffn2_w = {'w_ffn2_in': (dwin2, _halves), 'w_ffn2_out': (dwout2.reshape(NDEV, -1, D_MODEL), _stacked)}
    res = _mixer_backward(d2, prm, saved, stacks['w_pool_up'], stacks['w_glu'], stacks['w_ssm_up'], w_out_full, pw, mvec,
                          srow, bb, ct, scatter(g_ffn2_w))
    dz, dwo, dwpu, dwglu, dwsu, dpw, dbb, dct, vsum, da = res[:10]
    scattered(g_ffn2_w, res[10:])
    parts2, dwin_mix = _mixer_in_backward(x1, dz, prm, stacks['w_in'])
    d1, sums2 = _norm_backward(parts2, x1, d2, mo, prm, 1, ROW_G_MIX, 1.0, "mixer_norm_backward")
    g_mix_w = {'w_in': (dwin_mix, _stacked), 'w_pool_up': (dwpu, _stacked), 'w_glu': (dwglu, _stacked),
               'w_ssm_up': (dwsu, _stacked), 'w_out': (dwo, _stacked)}
    parts1, dwin1, dwout1, *res = _ffn_backward(xt, d1, ab1, prm, win1, wout1, 0, ROW_G_FFN1, "ffn1_backward",
                                                scatter(g_mix_w))
    scattered(g_mix_w, res)
    d0, sums1 = _norm_backward(parts1, xt, d1, f1, prm, 0, ROW_G_FFN1, 0.5, "ffn1_norm_backward")

    db_dense, df_rows = _ssm_dense_backward(dbb, da, srow, b_dense)
    cot = [df_rows[r].reshape(32, 64) for r in (2, 3, 0, 1)]
    d_lrl, d_li, d_ldt = _ssm_params_backward(ssm_lam_re_log[0], ssm_lam_im[0], log_dt_col, cot)
    small_grads = {
        'g_ffn1': sums1[0], 'g_mix': sums2[0], 'g_ffn2': sums3[0], 'g_final': fin[0], 'pool_w': dpw,
        'pool_b': vsum[1, :512], 'pool_scale': vsum[0, :512], 'ssm_lam_re_log': d_lrl, 'ssm_lam_im': d_li,
        'ssm_log_dt': d_ldt, 'ssm_b_re': jnp.transpose(_diag_blocks(db_dense[0], SSM_GROUP, SSM_STATE), (0, 2, 1)),
        'ssm_b_im': jnp.transpose(_diag_blocks(db_dense[1], SSM_GROUP, SSM_STATE), (0, 2, 1)),
        'ssm_c_re': jnp.transpose(_diag_blocks(dct[0], SSM_STATE, SSM_GROUP), (0, 2, 1)),
        'ssm_c_im': jnp.transpose(_diag_blocks(dct[1], SSM_STATE, SSM_GROUP), (0, 2, 1)),
        'ssm_d': vsum[2, :512], 'b_glu': vsum[3],
    }
    dmod = jnp.concatenate([sums1[1:4], sums2[1:4], sums3[1:4]], axis=0).reshape(-1)
    total, landed = _allreduce_small(_pack_small(dmod, small_grads))
    dmod_cols = landed[:, :ADA_ROWS].reshape(NDEV, ADA_ROWS * 128)

    g_ffn1_w = {'w_ffn1_in': (dwin1, _halves), 'w_ffn1_out': (dwout1.reshape(NDEV, -1, D_MODEL), _stacked)}
    last_views = [g_ffn1_w[n][1] for n in ffn1_w]
    send_sems, recv_sems, last_src, last_land, token = _scatter_start(
        [g_ffn1_w[n][0] for n in ffn1_w], last_views, [local(W, n).shape for n in ffn1_w],
        [total, jnp.full((8, 128), loss, F32)])
    total = total + token[0:1, 0:1]

    grad, delta, new_m, new_v = {}, {}, {}, {}

    def adam_sharded(n):
        res = _adam_sharded(local(W, n), local(M, n), local(V, n), lands[n], token, "adam_" + n)
        grad[n], delta[n], new_m[n], new_v[n] = [as_output(n, r) for r in res]
        return res[3]

    done = [adam_sharded(n) for n in GATHERED if n not in ffn1_w]
    res = _adam_ada(W['w_ada'][0], M['w_ada'][0], V['w_ada'][0], sc_all, dmod_cols + token[0:1, 0:1])
    grad['w_ada'], delta['w_ada'], new_m['w_ada'], new_v['w_ada'] = [r[None] for r in res]
    done.append(res[3])

    flat = lambda t: t.reshape(NDEV * PACK_ROWS, 128)
    small_w = flat(_pack_small(b_ada.reshape(-1), W))
    small_m = flat(_pack_small(m_b_ada.reshape(-1), M))
    small_v = flat(_pack_small(v_b_ada.reshape(-1), V))
    res = _adam_small(small_w, flat(total), small_m, small_v)
    done.append(res[2])
    for dst, packed in zip((grad, delta, new_m, new_v), (total, *res)):
        ada_vec, rest = _unpack_small(packed.reshape(NDEV, PACK_ROWS, 128), shapes)
        dst.update(rest)
        dst['b_ada'] = ada_vec.reshape(shapes['b_ada'])
    small_names = [n for n, _ in SMALL_PARAMS] + ['b_ada']
    firsts = [dst[n].reshape(-1)[0] for dst in (grad, delta, new_m, new_v) for n in small_names]
    done.append(jnp.full((8, 128), functools.reduce(jnp.add, firsts), F32))

    lands.update(zip(ffn1_w, _scatter_wait(send_sems, recv_sems, last_src, last_land, last_views, done)))
    for n in ffn1_w:
        adam_sharded(n)

    return (loss, d0[None], *[grad[n] for n in WEIGHT_ORDER], *[delta[n] for n in WEIGHT_ORDER],
            *[new_m[n] for n in WEIGHT_ORDER], *[new_v[n] for n in WEIGHT_ORDER])
```

```python
import functools

import jax
import jax.numpy as jnp
from jax import lax
from jax.experimental import pallas as pl
from jax.experimental.pallas import tpu as pltpu

F32 = jnp.float32
MXU_DTYPE = jnp.bfloat16
WIRE_DTYPE = jnp.bfloat16
SAVE_DTYPE = jnp.bfloat16

NDEV = 8
D_MODEL = 1024
D_FF = 2816
FF_SHARD = 2 * D_FF // NDEV
POOL_WIDTH = 512
POOL_GROUP = 128
SSM_WIDTH = 512
SSM_STATE = 64
SSM_GROUP = 16
SSM_BLOCKS = 4
SSM_BLOCK_STATE = 512
N_STATE = 2048
IN_WIDTH = 3072
EPS = 1e-6
ADAM_LR = 0.001
ADAM_B1 = 0.9
ADAM_B2 = 0.999
ADAM_EPS = 1e-08
ADAM_WD = 0.01
ADAM_STEP = 10

TM_FFN = 512
FFN_BWD_CHUNK = 256
TM_MIX = 256
TM_MIX_BWD = 256
TM_EW = 512
SCAN_ROWS = 8
POOL_HALO = 16
VMEM_LIMIT = 60 * 1024 * 1024

ROW_G_FFN1, ROW_G_MIX, ROW_G_FFN2, ROW_G_FINAL = 9, 10, 11, 12
ROW_POOL_B, ROW_POOL_SCALE, ROW_SSM_D, ROW_B_GLU = 0, 1, 2, 3

SMALL_EARLY = (
    ("g_mix", 1024), ("g_ffn2", 1024), ("g_final", 1024), ("pool_w", 65536),
    ("pool_b", 512), ("pool_scale", 512), ("ssm_lam_re_log", 2048), ("ssm_lam_im", 2048),
    ("ssm_log_dt", 32), ("ssm_b_re", 32768), ("ssm_b_im", 32768), ("ssm_c_re", 32768),
    ("ssm_c_im", 32768), ("ssm_d", 512), ("b_glu", 1024),
)
SMALL_LATE = (("g_ffn1", 1024),)
ADA_ROWS = 9
MESH = pl.DeviceIdType.MESH


def _pack_rows(params, with_ada):
    rest = -(-sum(n for _, n in params) // (NDEV * 128))
    return rest, -(-(rest + (ADA_ROWS if with_ada else 0)) // 8) * 8


def _mm(a, b):
    return jnp.dot(a.astype(MXU_DTYPE), b.astype(MXU_DTYPE), preferred_element_type=F32)


def _mm_nt(a, b):
    return lax.dot_general(a.astype(MXU_DTYPE), b.astype(MXU_DTYPE), (((1,), (1,)), ((), ())),
                           preferred_element_type=F32)


def _mm_tn(a, b):
    return lax.dot_general(a.astype(MXU_DTYPE), b.astype(MXU_DTYPE), (((0,), (0,)), ((), ())),
                           preferred_element_type=F32)


def _rms_scale(x):
    return lax.rsqrt(jnp.mean(x * x, axis=-1, keepdims=True) + EPS)


def _sigmoid(x):
    return jax.nn.sigmoid(x)


def _colsum(x):
    return jnp.sum(x, axis=0, keepdims=True)


def _row(ref, r):
    return ref[r:r + 1, :]


def _params(*sem):
    return pltpu.CompilerParams(dimension_semantics=sem, vmem_limit_bytes=VMEM_LIMIT)


def _resident(a):
    return pl.BlockSpec(a.shape, lambda *_: (0,) * a.ndim, pipeline_mode=pl.Buffered(1))


def _me():
    return lax.axis_index("x"), lax.axis_index("y"), lax.axis_index("c")


def _peer(rel):
    x, y, c = _me()
    px = 1 - x if rel & 4 else x
    py = 1 - y if rel & 2 else y
    pc = 1 - c if rel & 1 else c
    return (px, py, pc), 4 * px + 2 * py + pc


_HBM = pl.BlockSpec(memory_space=pl.ANY)
_HBM_ONLY = pl.BlockSpec(memory_space=pltpu.HBM)


def _stacked(ref, p):
    return ref.at[p]


def _halves(ref, p):
    return ref.at[p // 4, p % 4]


class _Gather:
    def __init__(self, shards):
        self.operands = list(shards)
        self.n = len(shards)
        self.out_shape = [jax.ShapeDtypeStruct((NDEV,) + s.shape, s.dtype) for s in shards]
        self.scratch = [pltpu.SemaphoreType.DMA((7 * self.n,)), pltpu.SemaphoreType.DMA((7 * self.n,)),
                        pltpu.SemaphoreType.DMA((self.n,))]

    def plan(self, srcs, outs, sems):
        send_sems, recv_sems, local_sems = sems
        n = self.n
        x, y, c = _me()
        me = 4 * x + 2 * y + c
        here, sibling = (x, y, c), (x, y, 1 - c)
        chips = [(1 - x, y), (x, 1 - y), (1 - x, 1 - y)]

        def blk(px, py, pc):
            return 4 * px + 2 * py + pc

        def copy(a, k, block, to, src=None):
            return pltpu.make_async_remote_copy(
                src_ref=outs[a].at[block] if src is None else src, dst_ref=outs[a].at[block],
                send_sem=send_sems.at[7 * a + k], recv_sem=recv_sems.at[7 * a + k], device_id=to, device_id_type=MESH)

        def mine(a):
            return pltpu.make_async_copy(srcs[a], outs[a].at[me], local_sems.at[a])

        def first(a):
            return [copy(a, 0, me, sibling, src=srcs[a])] + [copy(a, 1 + j, me, (*chip, c), src=srcs[a])
                                                              for j, chip in enumerate(chips)]

        def start():
            for a in range(n):
                mine(a).start()
                for cp in first(a):
                    cp.start()

        def forward():
            for a in range(n):
                for j, chip in enumerate(chips):
                    copy(a, 1 + j, blk(*chip, c), here).wait_recv()
                    copy(a, 4 + j, blk(*chip, c), sibling).start()

        def finish():
            for a in range(n):
                copy(a, 0, blk(x, y, 1 - c), here).wait_recv()
                for j, chip in enumerate(chips):
                    copy(a, 4 + j, blk(*chip, 1 - c), here).wait_recv()
            for a in range(n):
                mine(a).wait()
                for cp in first(a):
                    cp.wait_send()
                for j, chip in enumerate(chips):
                    copy(a, 4 + j, blk(*chip, c), sibling).wait_send()

        return start, forward, finish


class _Scatter:
    def __init__(self, arrays, views, shard_shapes):
        self.operands = list(arrays)
        self.views = list(views)
        self.n = len(arrays)
        self.out_shape = [jax.ShapeDtypeStruct((NDEV,) + tuple(s), a.dtype) for s, a in zip(shard_shapes, arrays)]
        self.scratch = [pltpu.SemaphoreType.DMA((7 * self.n,)), pltpu.SemaphoreType.DMA((7 * self.n,)),
                        pltpu.SemaphoreType.DMA((self.n,))]

    def plan(self, srcs, outs, sems):
        send_sems, recv_sems, local_sems = sems
        n, views = self.n, self.views
        x, y, c = _me()
        me = 4 * x + 2 * y + c

        def mine(a):
            return pltpu.make_async_copy(views[a](srcs[a], me), outs[a].at[me], local_sems.at[a])

        def copy(a, rel, sending):
            to, p = _peer(rel)
            return pltpu.make_async_remote_copy(
                src_ref=views[a](srcs[a], p), dst_ref=outs[a].at[me if sending else p],
                send_sem=send_sems.at[7 * a + rel - 1], recv_sem=recv_sems.at[7 * a + rel - 1],
                device_id=to if sending else (x, y, c), device_id_type=MESH)

        def start():
            for a in range(n):
                mine(a).start()
            for rel in range(1, 8):
                for a in range(n):
                    copy(a, rel, True).start()

        def forward():
            pass

        def finish():
            for rel in range(1, 8):
                for a in range(n):
                    copy(a, rel, False).wait_recv()
            for rel in range(1, 8):
                for a in range(n):
                    copy(a, rel, True).wait_send()
            for a in range(n):
                mine(a).wait()

        return start, forward, finish


def _launch(body, name, out_shape, in_specs, out_specs, operands, scratch=(), grid=None, semantics=None,
            carry=None, steps=None):
    out_shape, in_specs, out_specs = list(out_shape), list(in_specs), list(out_specs)
    operands, scratch = list(operands), list(scratch)
    n_in, n_out, n_scr = len(in_specs), len(out_shape), len(scratch)
    kernel_body = body
    if carry is not None:
        k = carry.n

        def kernel_body(*refs):
            ins, cin = refs[:n_in], refs[n_in:n_in + k]
            outs, cout = refs[n_in + k:n_in + k + n_out], refs[n_in + k + n_out:n_in + 2 * k + n_out]
            rest = refs[n_in + 2 * k + n_out:]
            scr, csem = rest[:n_scr], rest[n_scr:]
            start, forward, finish = carry.plan(cin, cout, csem)
            if steps is None:
                start()
                body(*ins, *outs, *scr)
                forward()
                finish()
            else:
                pl.when(steps()[0])(start)
                pl.when(steps()[1])(forward)
                body(*ins, *outs, *scr)
                pl.when(steps()[2])(finish)

        in_specs += [_HBM] * k
        out_shape += carry.out_shape
        out_specs += [_HBM] * k
        operands += carry.operands
        scratch += carry.scratch
    kwargs = {} if grid is None else {"grid": grid}
    params = pltpu.CompilerParams(vmem_limit_bytes=VMEM_LIMIT) if semantics is None else _params(*semantics)
    return pl.pallas_call(kernel_body, name=name, out_shape=out_shape, in_specs=in_specs, out_specs=out_specs,
                          scratch_shapes=scratch, compiler_params=params, **kwargs)(*operands)


def _grid_steps(nt):
    def steps():
        i = pl.program_id(0)
        return i == 0, i == nt - 1, i == nt - 1
    return steps


def _cast_shards(shards):
    n = len(shards)

    def body(*refs):
        for a in range(n):
            refs[n + a][...] = refs[a][...].astype(WIRE_DTYPE)

    return pl.pallas_call(body, name="cast_shards",
                          out_shape=[jax.ShapeDtypeStruct(s.shape, WIRE_DTYPE) for s in shards],
                          compiler_params=pltpu.CompilerParams(vmem_limit_bytes=VMEM_LIMIT))(*shards)


_SEM = pl.BlockSpec(memory_space=pltpu.SEMAPHORE)
_DATAFLOW = pltpu.SideEffectType.DATAFLOW_SIDE_EFFECTING


def _split_copy(arrays, views, landing, send_sems, recv_sems, a, rel):
    to, p = _peer(rel)
    x, y, c = _me()
    return pltpu.make_async_remote_copy(
        src_ref=views[a](arrays[a], p), dst_ref=landing[a].at[4 * x + 2 * y + c],
        send_sem=send_sems.at[NDEV * a + rel], recv_sem=recv_sems.at[NDEV * a + rel], device_id=to, device_id_type=MESH)


def _scatter_start(arrays, views, shard_shapes, after):
    n = len(arrays)
    landing = [pltpu.with_memory_space_constraint(lax.empty((NDEV,) + tuple(s), a.dtype), pltpu.HBM)
               for s, a in zip(shard_shapes, arrays)]
    arrays = [pltpu.with_memory_space_constraint(a, pltpu.HBM) for a in arrays]

    def body(*refs):
        ins, land = refs[:n], refs[n:2 * n]
        send_sems, recv_sems = refs[2 * n + len(after)], refs[2 * n + len(after) + 1]
        token = refs[-1]
        for rel in range(NDEV):
            for a in range(n):
                _split_copy(ins, views, land, send_sems, recv_sems, a, rel).start()
        token[...] = jnp.zeros_like(token)

    res = pl.pallas_call(
        body, name="scatter_start",
        out_shape=[pltpu.SemaphoreType.DMA((NDEV * n,)), pltpu.SemaphoreType.DMA((NDEV * n,))]
        + [pltpu.HBM(a.shape, a.dtype) for a in arrays] + [pltpu.HBM(l.shape, l.dtype) for l in landing]
        + [jax.ShapeDtypeStruct((8, 128), F32)],
        in_specs=[_HBM_ONLY] * (2 * n) + [_HBM] * len(after),
        out_specs=[_SEM, _SEM] + [_HBM_ONLY] * (2 * n) + [pl.BlockSpec(memory_space=pltpu.VMEM)],
        input_output_aliases={i: 2 + i for i in range(2 * n)},
        compiler_params=pltpu.CompilerParams(has_side_effects=_DATAFLOW),
    )(*arrays, *landing, *after)
    return res[0], res[1], res[2:2 + n], res[2 + n:2 + 2 * n], res[-1]


def _scatter_wait(send_sems, recv_sems, arrays, landing, views, after):
    n = len(arrays)

    def body(*refs):
        ins, land = refs[:n], refs[n:2 * n]
        send, recv = refs[2 * n], refs[2 * n + 1]
        for rel in range(NDEV):
            for a in range(n):
                cp = _split_copy(ins, views, land, send, recv, a, rel)
                cp.wait_send()
                cp.wait_recv()

    res = pl.pallas_call(
        body, name="scatter_wait",
        out_shape=[pltpu.HBM(a.shape, a.dtype) for a in arrays] + [pltpu.HBM(l.shape, l.dtype) for l in landing],
        in_specs=[_HBM_ONLY] * (2 * n) + [_SEM, _SEM] + [_HBM] * len(after),
        out_specs=[_HBM_ONLY] * (2 * n),
        input_output_aliases={i: i for i in range(2 * n)},
        compiler_params=pltpu.CompilerParams(has_side_effects=_DATAFLOW),
    )(*arrays, *landing, send_sems, recv_sems, *after)
    return res[n:]


def _ada_forward(c_row, w_ada, b_ada8, carry):
    cols = w_ada.shape[1]

    def body(c_ref, w_ref, b_ref, mod_ref, sc_ref, c_all, send_buf, recv_buf, send1, recv1, send2, recv2):
        x, y, c = _me()
        me = 4 * x + 2 * y + c
        rowi = lax.broadcasted_iota(jnp.int32, (8, D_MODEL), 0)
        c_all[me] = jnp.broadcast_to(c_ref[...], (8, D_MODEL))
        copies = []
        for rel in range(1, 8):
            to, _ = _peer(rel)
            cp = pltpu.make_async_remote_copy(src_ref=c_all.at[me], dst_ref=c_all.at[me], send_sem=send1.at[rel - 1],
                                              recv_sem=recv1.at[rel - 1], device_id=to, device_id_type=MESH)
            cp.start()
            copies.append(cp)
        for rel in range(1, 8):
            _, p = _peer(rel)
            pltpu.make_async_remote_copy(src_ref=c_all.at[p], dst_ref=c_all.at[p], send_sem=send1.at[rel - 1],
                                         recv_sem=recv1.at[rel - 1], device_id=(x, y, c), device_id_type=MESH).wait_recv()
        for cp in copies:
            cp.wait_send()
        cmat = jnp.zeros((8, D_MODEL), F32)
        for b in range(8):
            cmat = jnp.where(rowi == b, c_all[b], cmat)
        sc = cmat * _sigmoid(cmat)
        sc_ref[...] = sc
        modcols = _mm(sc, w_ref[...]) + b_ref[pl.ds(me, 1), :]
        for b in range(8):
            send_buf[b] = jnp.broadcast_to(modcols[b:b + 1, :], (8, cols))
        recv_buf[me] = send_buf[me]
        copies = []
        for rel in range(1, 8):
            to, p = _peer(rel)
            cp = pltpu.make_async_remote_copy(src_ref=send_buf.at[p], dst_ref=recv_buf.at[me], send_sem=send2.at[rel - 1],
                                              recv_sem=recv2.at[rel - 1], device_id=to, device_id_type=MESH)
            cp.start()
            copies.append(cp)
        for rel in range(1, 8):
            _, p = _peer(rel)
            pltpu.make_async_remote_copy(src_ref=send_buf.at[p], dst_ref=recv_buf.at[p], send_sem=send2.at[rel - 1],
                                         recv_sem=recv2.at[rel - 1], device_id=(x, y, c), device_id_type=MESH).wait_recv()
        for cp in copies:
            cp.wait_send()
        rowc = lax.broadcasted_iota(jnp.int32, (8, cols), 0)
        out = jnp.zeros((8, cols), F32)
        for k in range(8):
            out = jnp.where(rowc == k, recv_buf[k], out)
        mod_ref[...] = out

    return _launch(
        body, "ada_forward",
        out_shape=[jax.ShapeDtypeStruct((8, cols), F32), jax.ShapeDtypeStruct((8, D_MODEL), F32)],
        in_specs=[pl.BlockSpec(memory_space=pltpu.VMEM)] * 3,
        out_specs=[pl.BlockSpec(memory_space=pltpu.VMEM)] * 2,
        operands=(c_row, w_ada, b_ada8),
        scratch=[pltpu.VMEM((8, 8, D_MODEL), F32), pltpu.VMEM((8, 8, cols), F32), pltpu.VMEM((8, 8, cols), F32)]
        + [pltpu.SemaphoreType.DMA((7,))] * 4,
        carry=carry)


def _allreduce_small(pack, order, name):
    rows = pack.shape[1]

    def body(pack_ref, order_ref, total_ref, land_ref, send1, recv1, send2, recv2):
        x, y, c = _me()
        me = 4 * x + 2 * y + c
        land_ref[me] = pack_ref[me]
        copies = []
        for rel in range(1, 8):
            to, p = _peer(rel)
            cp = pltpu.make_async_remote_copy(src_ref=pack_ref.at[p], dst_ref=land_ref.at[me], send_sem=send1.at[rel - 1],
                                              recv_sem=recv1.at[rel - 1], device_id=to, device_id_type=MESH)
            cp.start()
            copies.append(cp)
        for rel in range(1, 8):
            _, p = _peer(rel)
            pltpu.make_async_remote_copy(src_ref=pack_ref.at[p], dst_ref=land_ref.at[p], send_sem=send1.at[rel - 1],
                                         recv_sem=recv1.at[rel - 1], device_id=(x, y, c), device_id_type=MESH).wait_recv()
        for cp in copies:
            cp.wait_send()
        acc = land_ref[0]
        for b in range(1, 8):
            acc = acc + land_ref[b]
        total_ref[me] = acc
        copies = []
        for rel in range(1, 8):
            to, _ = _peer(rel)
            cp = pltpu.make_async_remote_copy(src_ref=total_ref.at[me], dst_ref=total_ref.at[me], send_sem=send2.at[rel - 1],
                                              recv_sem=recv2.at[rel - 1], device_id=to, device_id_type=MESH)
            cp.start()
            copies.append(cp)
        for rel in range(1, 8):
            _, p = _peer(rel)
            pltpu.make_async_remote_copy(src_ref=total_ref.at[p], dst_ref=total_ref.at[p], send_sem=send2.at[rel - 1],
                                         recv_sem=recv2.at[rel - 1], device_id=(x, y, c), device_id_type=MESH).wait_recv()
        for cp in copies:
            cp.wait_send()

    return pl.pallas_call(
        body, name=name,
        out_shape=[jax.ShapeDtypeStruct((8, rows, 128), F32), jax.ShapeDtypeStruct((8, rows, 128), F32)],
        in_specs=[pl.BlockSpec(memory_space=pltpu.VMEM), _HBM],
        out_specs=[pl.BlockSpec(memory_space=pltpu.VMEM)] * 2,
        scratch_shapes=[pltpu.SemaphoreType.DMA((7,))] * 4,
        compiler_params=pltpu.CompilerParams(vmem_limit_bytes=VMEM_LIMIT),
    )(pack, order)


def _modulated(x, prm_ref, sub, g_row):
    shift, scale = _row(prm_ref, 3 * sub), _row(prm_ref, 3 * sub + 1)
    g = _row(prm_ref, g_row)
    r = _rms_scale(x)
    n0 = x * r
    return (n0 * g) * (1.0 + scale) + shift, r, n0


def _ffn_forward(x, prm, win, wout, sub, g_row, name, carry=None):
    T = x.shape[0]
    tm = min(T, TM_FFN)

    def body(x_ref, prm_ref, win_ref, wout_ref, xo_ref, f_ref, ab_ref):
        xv = x_ref[...]
        h, _, _ = _modulated(xv, prm_ref, sub, g_row)
        hb = h.astype(MXU_DTYPE)
        acc = None
        for j in range(4):
            a = _mm_nt(hb, win_ref[0, j])
            b = _mm_nt(hb, win_ref[1, j])
            ab_ref[0, j] = a.astype(SAVE_DTYPE)
            ab_ref[1, j] = b.astype(SAVE_DTYPE)
            s = (a * _sigmoid(a)) * b
            t = _mm(s, wout_ref[j])
            acc = t if acc is None else acc + t
        f_ref[...] = acc.astype(SAVE_DTYPE)
        xo_ref[...] = xv + (0.5 * _row(prm_ref, 3 * sub + 2)) * acc

    tok = pl.BlockSpec((tm, D_MODEL), lambda i: (i, 0))
    return _launch(
        body, name, grid=(T // tm,), semantics=("arbitrary",),
        out_shape=[jax.ShapeDtypeStruct((T, D_MODEL), F32), jax.ShapeDtypeStruct((T, D_MODEL), SAVE_DTYPE),
                   jax.ShapeDtypeStruct((2, 4, T, FF_SHARD), SAVE_DTYPE)],
        in_specs=[tok, _resident(prm), _resident(win), _resident(wout)],
        out_specs=[tok, tok, pl.BlockSpec((2, 4, tm, FF_SHARD), lambda i: (0, 0, i, 0))],
        operands=(x, prm, win, wout), carry=carry, steps=_grid_steps(T // tm))


def _ffn_hidden(x, prm, win, sub, g_row, name, carry=None):
    T = x.shape[0]
    tm = min(T, TM_FFN)

    def body(x_ref, prm_ref, win_ref, ab_ref, s_ref):
        h, _, _ = _modulated(x_ref[...], prm_ref, sub, g_row)
        hb = h.astype(MXU_DTYPE)
        for j in range(4):
            a = _mm_nt(hb, win_ref[0, j])
            b = _mm_nt(hb, win_ref[1, j])
            ab_ref[0, j] = a.astype(SAVE_DTYPE)
            ab_ref[1, j] = b.astype(SAVE_DTYPE)
            s_ref[j] = ((a * _sigmoid(a)) * b).astype(MXU_DTYPE)

    return _launch(
        body, name, grid=(T // tm,), semantics=("arbitrary",),
        out_shape=[jax.ShapeDtypeStruct((2, 4, T, FF_SHARD), SAVE_DTYPE), jax.ShapeDtypeStruct((4, T, FF_SHARD), MXU_DTYPE)],
        in_specs=[pl.BlockSpec((tm, D_MODEL), lambda i: (i, 0)), _resident(prm), _resident(win)],
        out_specs=[pl.BlockSpec((2, 4, tm, FF_SHARD), lambda i: (0, 0, i, 0)),
                   pl.BlockSpec((4, tm, FF_SHARD), lambda i: (0, i, 0))],
        operands=(x, prm, win), carry=carry, steps=_grid_steps(T // tm))


def _ffn_out(x, s, prm, wout, sub, name, carry=None):
    T = x.shape[0]
    tm = min(T, TM_FFN)

    def body(x_ref, s_ref, prm_ref, wout_ref, xo_ref, f_ref):
        acc = None
        for j in range(4):
            t = _mm(s_ref[j], wout_ref[j])
            acc = t if acc is None else acc + t
        f_ref[...] = acc.astype(SAVE_DTYPE)
        xo_ref[...] = x_ref[...] + (0.5 * _row(prm_ref, 3 * sub + 2)) * acc

    tok = pl.BlockSpec((tm, D_MODEL), lambda i: (i, 0))
    return _launch(
        body, name, grid=(T // tm,), semantics=("arbitrary",),
        out_shape=[jax.ShapeDtypeStruct((T, D_MODEL), F32), jax.ShapeDtypeStruct((T, D_MODEL), SAVE_DTYPE)],
        in_specs=[tok, pl.BlockSpec((4, tm, FF_SHARD), lambda i: (0, i, 0)), _resident(prm), _resident(wout)],
        out_specs=[tok, tok], operands=(x, s, prm, wout), carry=carry, steps=_grid_steps(T // tm))


def _ffn_backward(x, d, ab, prm, win, wout, order, sub, g_row, name, carry=None):
    T = x.shape[0]
    tm = min(T, TM_FFN)
    nt = T // tm
    chunk = min(tm, FFN_BWD_CHUNK)

    def body(x_ref, d_ref, ab_ref, prm_ref, win_ref, wout_ref, order_ref, dh_ref, dwin_ref, dwout_ref, acc_in, acc_out):
        i = pl.program_id(1)

        @pl.when(i == 0)
        def _():
            acc_in[...] = jnp.zeros_like(acc_in)
            acc_out[...] = jnp.zeros_like(acc_out)

        wa, wb, wo = win_ref[0, 0], win_ref[1, 0], wout_ref[0]
        half_gate = 0.5 * _row(prm_ref, 3 * sub + 2)
        das, dbs, ss, hbs, dfss = [], [], [], [], []
        for ck in range(tm // chunk):
            rows = slice(ck * chunk, (ck + 1) * chunk)
            h, _, _ = _modulated(x_ref[rows, :], prm_ref, sub, g_row)
            hbs.append(h.astype(MXU_DTYPE))
            a = ab_ref[0, 0, rows, :].astype(F32)
            b = ab_ref[1, 0, rows, :].astype(F32)
            sg = _sigmoid(a)
            si = a * sg
            dfs = (half_gate * d_ref[rows, :]).astype(MXU_DTYPE)
            ds = _mm_nt(dfs, wo)
            da = (ds * b * (sg * (1.0 + a * (1.0 - sg)))).astype(MXU_DTYPE)
            db = (ds * si).astype(MXU_DTYPE)
            dh_ref[0, rows, :] = (_mm(da, wa) + _mm(db, wb)).astype(SAVE_DTYPE)
            das.append(da)
            dbs.append(db)
            ss.append((si * b).astype(MXU_DTYPE))
            dfss.append(dfs)
        cat = (lambda v: v[0]) if len(das) == 1 else (lambda v: jnp.concatenate(v, axis=0))
        hb = cat(hbs)
        acc_out[...] += _mm_tn(cat(ss), cat(dfss))
        acc_in[0] += _mm_tn(cat(das), hb)
        acc_in[1] += _mm_tn(cat(dbs), hb)

        @pl.when(i == nt - 1)
        def _():
            dwin_ref[0, 0] = acc_in[0].astype(WIRE_DTYPE)
            dwin_ref[1, 0] = acc_in[1].astype(WIRE_DTYPE)
            dwout_ref[0] = acc_out[...].astype(WIRE_DTYPE)

    def steps():
        j, i = pl.program_id(0), pl.program_id(1)
        return (j == 0) & (i == 0), (j == 2) & (i == 0), (j == 3) & (i == nt - 1)

    tok = pl.BlockSpec((tm, D_MODEL), lambda j, i: (i, 0))
    return _launch(
        body, name, grid=(4, nt), semantics=("arbitrary", "arbitrary"),
        out_shape=[jax.ShapeDtypeStruct((4, T, D_MODEL), SAVE_DTYPE),
                   jax.ShapeDtypeStruct(win.shape, WIRE_DTYPE), jax.ShapeDtypeStruct(wout.shape, WIRE_DTYPE)],
        in_specs=[tok, tok, pl.BlockSpec((2, 1, tm, FF_SHARD), lambda j, i: (0, j, i, 0)), _resident(prm),
                  pl.BlockSpec((2, 1, FF_SHARD, D_MODEL), lambda j, i: (0, j, 0, 0)),
                  pl.BlockSpec((1, FF_SHARD, D_MODEL), lambda j, i: (j, 0, 0)), _HBM],
        out_specs=[pl.BlockSpec((1, tm, D_MODEL), lambda j, i: (j, i, 0)),
                   pl.BlockSpec((2, 1, FF_SHARD, D_MODEL), lambda j, i: (0, j, 0, 0)),
                   pl.BlockSpec((1, FF_SHARD, D_MODEL), lambda j, i: (j, 0, 0))],
        operands=(x, d, ab, prm, win, wout, order),
        scratch=[pltpu.VMEM((2, FF_SHARD, D_MODEL), F32), pltpu.VMEM((FF_SHARD, D_MODEL), F32)],
        carry=carry, steps=steps)


def _norm_backward(parts, x, d, f, prm, sub, g_row, gate_coef, name, carry=None):
    T = x.shape[0]
    tm = min(T, TM_EW)
    P = parts.shape[0]

    def body(p_ref, x_ref, d_ref, f_ref, prm_ref, dx_ref, sums_ref):
        i = pl.program_id(0)
        dh = p_ref[0].astype(F32)
        for k in range(1, P):
            dh = dh + p_ref[k].astype(F32)
        xv, dv = x_ref[...], d_ref[...]
        scale, g = _row(prm_ref, 3 * sub + 1), _row(prm_ref, g_row)
        r = _rms_scale(xv)
        n0 = xv * r
        dn = dh * (1.0 + scale)
        dn0 = dn * g
        dx_ref[...] = dv + r * (dn0 - n0 * jnp.mean(dn0 * n0, axis=-1, keepdims=True))
        upd = jnp.concatenate([_colsum(dn * n0), _colsum(dh), _colsum(dh * (n0 * g)),
                               gate_coef * _colsum(dv * f_ref[...].astype(F32)), jnp.zeros((4, D_MODEL), F32)], axis=0)

        @pl.when(i == 0)
        def _():
            sums_ref[...] = upd

        @pl.when(i > 0)
        def _():
            sums_ref[...] += upd

    tok = pl.BlockSpec((tm, D_MODEL), lambda i: (i, 0))
    return _launch(
        body, name, grid=(T // tm,), semantics=("arbitrary",),
        out_shape=[jax.ShapeDtypeStruct((T, D_MODEL), F32), jax.ShapeDtypeStruct((8, D_MODEL), F32)],
        in_specs=[pl.BlockSpec((P, tm, D_MODEL), lambda i: (0, i, 0)), tok, tok, tok, _resident(prm)],
        out_specs=[tok, pl.BlockSpec((8, D_MODEL), lambda i: (0, 0))],
        operands=(parts, x, d, f, prm), carry=carry, steps=_grid_steps(T // tm))


def _final_loss(x, target, prm):
    T = x.shape[0]
    tm = min(T, TM_EW)

    def body(x_ref, t_ref, prm_ref, dx_ref, sums_ref):
        i = pl.program_id(0)
        xv = x_ref[...]
        g = _row(prm_ref, ROW_G_FINAL)
        r = _rms_scale(xv)
        n0 = xv * r
        err = n0 * g - t_ref[...]
        dy = err / float(D_MODEL)
        dn0 = dy * g
        dx_ref[...] = r * (dn0 - n0 * jnp.mean(dn0 * n0, axis=-1, keepdims=True))
        loss = 0.5 * jnp.sum(jnp.mean(err * err, axis=-1, keepdims=True), axis=0, keepdims=True)
        upd = jnp.concatenate([_colsum(dy * n0), jnp.broadcast_to(loss, (1, D_MODEL)), jnp.zeros((6, D_MODEL), F32)], axis=0)

        @pl.when(i == 0)
        def _():
            sums_ref[...] = upd

        @pl.when(i > 0)
        def _():
            sums_ref[...] += upd

    tok = pl.BlockSpec((tm, D_MODEL), lambda i: (i, 0))
    return pl.pallas_call(
        body, name="final_loss", grid=(T // tm,),
        out_shape=[jax.ShapeDtypeStruct((T, D_MODEL), F32), jax.ShapeDtypeStruct((8, D_MODEL), F32)],
        in_specs=[tok, tok, pl.BlockSpec(prm.shape, lambda i: (0, 0))],
        out_specs=[tok, pl.BlockSpec((8, D_MODEL), lambda i: (0, 0))],
        compiler_params=_params("arbitrary"),
    )(x, target, prm)


def _ssm_discretise(lam_re_log, lam_im, log_dt):
    lr = -jnp.exp(lam_re_log)
    dt = jnp.exp(log_dt)
    mag = jnp.exp(lr * dt)
    ang = lam_im * dt
    ab_re = mag * jnp.cos(ang)
    ab_im = mag * jnp.sin(ang)
    num_re = ab_re - 1.0
    num_im = ab_im
    den = lr * lr + lam_im * lam_im
    f_re = (num_re * lr + num_im * lam_im) / den
    f_im = (num_im * lr - num_re * lam_im) / den
    return ab_re, ab_im, f_re, f_im


def _ssm_params_forward(lam_re_log, lam_im, log_dt):
    def body(a_ref, b_ref, c_ref, o0, o1, o2, o3):
        outs = _ssm_discretise(a_ref[...], b_ref[...], c_ref[...])
        for o, v in zip((o0, o1, o2, o3), outs):
            o[...] = v

    return pl.pallas_call(body, name="ssm_params_forward",
                          out_shape=[jax.ShapeDtypeStruct(lam_im.shape, F32)] * 4)(lam_re_log, lam_im, log_dt)


def _ssm_params_backward(lam_re_log, lam_im, log_dt, cot):
    def body(a_ref, b_ref, c_ref, g0, g1, g2, g3, o0, o1, o2):
        _, vjp = jax.vjp(_ssm_discretise, a_ref[...], b_ref[...], c_ref[...])
        d0, d1, d2 = vjp((g0[...], g1[...], g2[...], g3[...]))
        o0[...] = d0
        o1[...] = d1
        o2[...] = d2

    return pl.pallas_call(
        body, name="ssm_params_backward",
        out_shape=[jax.ShapeDtypeStruct(lam_im.shape, F32), jax.ShapeDtypeStruct(lam_im.shape, F32),
                   jax.ShapeDtypeStruct(log_dt.shape, F32)])(lam_re_log, lam_im, log_dt, *cot)


def _ssm_dense_forward(srow, b_dense, c_dense):
    def body(srow_ref, bd_ref, cd_ref, bb_ref, ct_ref):
        for j in range(SSM_BLOCKS):
            lanes = slice(j * SSM_BLOCK_STATE, (j + 1) * SSM_BLOCK_STATE)
            f_re, f_im = srow_ref[2:3, lanes], srow_ref[3:4, lanes]
            bb_ref[0, j] = (f_re * bd_ref[0, j] - f_im * bd_ref[1, j]).astype(MXU_DTYPE)
            bb_ref[1, j] = (f_re * bd_ref[1, j] + f_im * bd_ref[0, j]).astype(MXU_DTYPE)
            ct_ref[0, j] = cd_ref[0, j].astype(MXU_DTYPE)
            ct_ref[1, j] = cd_ref[1, j].astype(MXU_DTYPE)

    return pl.pallas_call(body, name="ssm_dense_forward",
                          out_shape=[jax.ShapeDtypeStruct(b_dense.shape, MXU_DTYPE),
                                     jax.ShapeDtypeStruct(c_dense.shape, MXU_DTYPE)],
                          compiler_params=pltpu.CompilerParams(vmem_limit_bytes=VMEM_LIMIT))(srow, b_dense, c_dense)


def _cmul(p, q):
    return p[0] * q[0] - p[1] * q[1], p[0] * q[1] + p[1] * q[0]


def _scan_coefficients(ar, ai, reverse):
    n = ar.shape[1]
    p = {1: (ar, ai)}
    p[2] = _cmul(p[1], p[1])
    p[3] = _cmul(p[2], p[1])
    p[4] = _cmul(p[2], p[2])
    p[5] = _cmul(p[4], p[1])
    p[6] = _cmul(p[4], p[2])
    p[7] = _cmul(p[4], p[3])
    p[8] = _cmul(p[4], p[4])
    rowi = lax.broadcasted_iota(jnp.int32, (SCAN_ROWS, n), 0)
    tiles = []
    for dstep in (1, 2, 4):
        keep = (rowi < SCAN_ROWS - dstep) if reverse else (rowi >= dstep)
        for part in p[dstep]:
            tiles.append(jnp.where(keep, jnp.broadcast_to(part, (SCAN_ROWS, n)), 0.0))
    for comp in (0, 1):
        t = jnp.zeros((SCAN_ROWS, n), F32)
        for rr in range(SCAN_ROWS):
            power = SCAN_ROWS - rr if reverse else rr + 1
            t = jnp.where(rowi == rr, jnp.broadcast_to(p[power][comp], (SCAN_ROWS, n)), t)
        tiles.append(t)
    return tiles


def _load_stack(stack_hbm, dst, sems, base):
    cols = stack_hbm.shape[2]
    cps = [pltpu.make_async_copy(stack_hbm.at[k], dst.at[:, pl.ds(k * cols, cols)], sems.at[base + k])
           for k in range(NDEV)]
    for cp in cps:
        cp.start()
    return cps


def _window_lanes():
    lane = lax.broadcasted_iota(jnp.int32, (1, POOL_WIDTH), 1)
    return jnp.where(lane < 128, 2.0, jnp.where(lane < 256, 4.0, jnp.where(lane < 384, 8.0, 16.0)))


def _gelu(y):
    return 0.5 * y * (1.0 + lax.erf(y * 0.7071067811865476))


def _gelu_grad(y):
    return 0.5 * (1.0 + lax.erf(y * 0.7071067811865476)) + y * jnp.exp(-0.5 * y * y) * 0.3989422804014327


def _mixer_forward(x, prm, w_in_s, w_pu_s, w_glu_s, w_su_s, w_out, pool_w, mvec, srow, bb, ct, carry=None):
    T = x.shape[0]
    tm = min(T, TM_MIX)
    nt = T // tm
    n_tiles = tm // SCAN_ROWS

    def body(x_ref, prm_ref, w_in_h, w_pu_h, w_glu_h, w_su_h, w_out_h, pw_ref, mv_ref, srow_ref, bb, ct,
             x2_ref, mo_ref, z_ref, sre_ref, sim_ref, zp_ref, q_ref, yp_ref, yss_ref, vg_ref, ys_ref,
             w_in, w_pu, w_glu, w_su, w_o, coef, carry, hist, bu, sems):
        i = pl.program_id(0)

        @pl.when(i == 0)
        def _():
            cps = (_load_stack(w_in_h, w_in, sems, 0) + _load_stack(w_pu_h, w_pu, sems, 8)
                   + _load_stack(w_glu_h, w_glu, sems, 16) + _load_stack(w_su_h, w_su, sems, 24))
            cps.append(pltpu.make_async_copy(w_out_h, w_o, sems.at[32]))
            cps[-1].start()
            for j in range(SSM_BLOCKS):
                lanes = slice(j * SSM_BLOCK_STATE, (j + 1) * SSM_BLOCK_STATE)
                for k, t in enumerate(_scan_coefficients(srow_ref[0:1, lanes], srow_ref[1:2, lanes], False)):
                    coef[j, k] = t
            carry[...] = jnp.zeros_like(carry)
            hist[...] = jnp.zeros_like(hist)
            for cp in cps:
                cp.wait()

        xv = x_ref[...]
        h, _, _ = _modulated(xv, prm_ref, 1, ROW_G_MIX)
        z = _mm(h, w_in[...])
        z_ref[...] = z.astype(SAVE_DTYPE)
        u_pool, u_ssm = z[:, 0:512], z[:, 512:1024]
        gl_pool, gl_ssm = z[:, 1024:2048], z[:, 2048:3072]

        ext = jnp.concatenate([hist[...], u_pool], axis=0)
        w2 = ext + pltpu.roll(ext, 1, 0)
        w4 = w2[:, 128:] + pltpu.roll(w2[:, 128:], 2, 0)
        w8 = w4[:, 128:] + pltpu.roll(w4[:, 128:], 4, 0)
        w16 = w8[:, 128:] + pltpu.roll(w8[:, 128:], 8, 0)
        wsum = jnp.concatenate([w2[POOL_HALO:, :128], w4[POOL_HALO:, :128], w8[POOL_HALO:, :128], w16[POOL_HALO:]], axis=1)
        hist[...] = u_pool[tm - POOL_HALO:, :]
        t1 = (lax.broadcasted_iota(jnp.int32, (tm, 1), 0) + (i * tm + 1)).astype(F32)
        zp = wsum / jnp.minimum(t1, _window_lanes()) - u_pool
        zp_ref[...] = zp.astype(SAVE_DTYPE)
        q = jnp.concatenate([_mm(zp[:, k * 128:(k + 1) * 128], pw_ref[k]) for k in range(4)], axis=1)
        q = q + mv_ref[ROW_POOL_B:ROW_POOL_B + 1, 0:512]
        q_ref[...] = q.astype(SAVE_DTYPE)
        y_pool = _mm(q * mv_ref[ROW_POOL_SCALE:ROW_POOL_SCALE + 1, 0:512], w_pu[...])
        yp_ref[...] = y_pool.astype(SAVE_DTYPE)

        y_blocks = []
        for j in range(SSM_BLOCKS):
            lanes = pl.ds(j * SSM_BLOCK_STATE, SSM_BLOCK_STATE)
            ub = u_ssm[:, j * 128:(j + 1) * 128].astype(MXU_DTYPE)
            bu[0] = _mm(ub, bb[0, j])
            bu[1] = _mm(ub, bb[1, j])
            a1r, a1i, a2r, a2i, a4r, a4i, pr, pi = [coef[j, k] for k in range(8)]

            def step(tt, c, lanes=lanes, a1r=a1r, a1i=a1i, a2r=a2r, a2i=a2i, a4r=a4r, a4i=a4i, pr=pr, pi=pi):
                cr, ci = c
                rows = pl.ds(pl.multiple_of(tt * SCAN_ROWS, SCAN_ROWS), SCAN_ROWS)
                xr, xi = bu[0, rows, :], bu[1, rows, :]
                for dstep, kr, ki in ((1, a1r, a1i), (2, a2r, a2i), (4, a4r, a4i)):
                    sr, si = pltpu.roll(xr, dstep, 0), pltpu.roll(xi, dstep, 0)
                    xr, xi = xr + kr * sr - ki * si, xi + kr * si + ki * sr
                xr, xi = xr + pr * cr - pi * ci, xi + pr * ci + pi * cr
                sre_ref[rows, lanes] = xr
                sim_ref[rows, lanes] = xi
                return (jnp.broadcast_to(xr[SCAN_ROWS - 1:SCAN_ROWS, :], xr.shape),
                        jnp.broadcast_to(xi[SCAN_ROWS - 1:SCAN_ROWS, :], xi.shape))

            cr, ci = lax.fori_loop(0, n_tiles, step, (carry[j, 0], carry[j, 1]))
            carry[j, 0] = cr
            carry[j, 1] = ci
            y_blocks.append(_mm(sre_ref[:, lanes], ct[0, j]) - _mm(sim_ref[:, lanes], ct[1, j]))
        yss = jnp.concatenate(y_blocks, axis=1) + mv_ref[ROW_SSM_D:ROW_SSM_D + 1, 0:512] * u_ssm
        yss_ref[...] = yss.astype(SAVE_DTYPE)
        vg = _mm(_gelu(yss), w_glu[...]) + mv_ref[ROW_B_GLU:ROW_B_GLU + 1, :]
        vg_ref[...] = vg.astype(SAVE_DTYPE)
        y_ssm = _mm(vg[:, 0:512] * _sigmoid(vg[:, 512:1024]), w_su[...])
        ys_ref[...] = y_ssm.astype(SAVE_DTYPE)

        merged = _sigmoid(gl_pool) * y_pool + _sigmoid(gl_ssm) * y_ssm
        mo = _mm(merged, w_o[...])
        mo_ref[...] = mo.astype(SAVE_DTYPE)
        x2_ref[...] = xv + _row(prm_ref, 5) * mo

    def tok(width):
        return pl.BlockSpec((tm, width), lambda i: (i, 0))

    hbm = _HBM
    widths = (D_MODEL, D_MODEL, IN_WIDTH, N_STATE, N_STATE, 512, 512, D_MODEL, 512, D_MODEL, D_MODEL)
    dtypes = (F32, SAVE_DTYPE, SAVE_DTYPE, F32, F32) + (SAVE_DTYPE,) * 6
    return _launch(
        body, "mixer_forward", grid=(nt,), semantics=("arbitrary",), carry=carry, steps=_grid_steps(nt),
        out_shape=[jax.ShapeDtypeStruct((T, w), dt) for w, dt in zip(widths, dtypes)],
        in_specs=[tok(D_MODEL), _resident(prm), hbm, hbm, hbm, hbm, hbm, _resident(pool_w), _resident(mvec),
                  _resident(srow), _resident(bb), _resident(ct)],
        out_specs=[tok(w) for w in widths],
        operands=(x, prm, w_in_s, w_pu_s, w_glu_s, w_su_s, w_out, pool_w, mvec, srow, bb, ct),
        scratch=[
            pltpu.VMEM((D_MODEL, IN_WIDTH), MXU_DTYPE), pltpu.VMEM((512, D_MODEL), MXU_DTYPE),
            pltpu.VMEM((512, D_MODEL), MXU_DTYPE), pltpu.VMEM((512, D_MODEL), MXU_DTYPE),
            pltpu.VMEM((D_MODEL, D_MODEL), MXU_DTYPE),
            pltpu.VMEM((SSM_BLOCKS, 8, SCAN_ROWS, SSM_BLOCK_STATE), F32),
            pltpu.VMEM((SSM_BLOCKS, 2, SCAN_ROWS, SSM_BLOCK_STATE), F32),
            pltpu.VMEM((POOL_HALO, POOL_WIDTH), F32),
            pltpu.VMEM((2, tm, SSM_BLOCK_STATE), F32),
            pltpu.SemaphoreType.DMA((33,)),
        ])


def _mixer_backward(d2, prm, saved, w_pu_s, w_glu_s, w_su_s, w_out, pool_w, mvec, srow, bb, ct, carry=None):
    z, s_re, s_im, zp, q, y_pool, yss, vg, y_ssm = saved
    T = d2.shape[0]
    tm = min(T, TM_MIX_BWD)
    nt = T // tm
    n_tiles = tm // SCAN_ROWS

    def body(d_ref, prm_ref, z_ref, sre_ref, sim_ref, zp_ref, q_ref, yp_ref, yss_ref, vg_ref, ys_ref,
             w_pu_h, w_glu_h, w_su_h, w_out_h, pw_ref, mv_ref, srow_ref, bb, ct,
             dz_ref, dwo_h, dwpu_h, dwglu_h, dwsu_h, dpw_h, dbb_h, dct_h, vsum_h, da_h,
             w_pu, w_glu, w_su, w_o, pwb, coef, carry, hist, dre, lam,
             a_wo, a_wpu, a_wglu, a_wsu, a_pw, a_bb, a_ct, a_vs, a_da, st_wo, st_up, sems):
        i = pl.program_id(0)
        tile = nt - 1 - i

        @pl.when(i == 0)
        def _():
            cps = (_load_stack(w_pu_h, w_pu, sems, 0) + _load_stack(w_glu_h, w_glu, sems, 8)
                   + _load_stack(w_su_h, w_su, sems, 16))
            cps.append(pltpu.make_async_copy(w_out_h, w_o, sems.at[24]))
            cps[-1].start()
            pwb[...] = pw_ref[...].astype(MXU_DTYPE)
            for j in range(SSM_BLOCKS):
                lanes = slice(j * SSM_BLOCK_STATE, (j + 1) * SSM_BLOCK_STATE)
                for k, t in enumerate(_scan_coefficients(srow_ref[0:1, lanes], srow_ref[1:2, lanes], True)):
                    coef[j, k] = t
            for acc in (carry, hist, a_wo, a_wpu, a_wglu, a_wsu, a_pw, a_bb, a_ct, a_vs, a_da):
                acc[...] = jnp.zeros_like(acc)
            for cp in cps:
                cp.wait()

        dv = d_ref[...]
        zt = z_ref[...].astype(F32)
        u_ssm, gl_pool, gl_ssm = zt[:, 512:1024], zt[:, 1024:2048], zt[:, 2048:3072]
        y_p, y_s = yp_ref[...].astype(F32), ys_ref[...].astype(F32)
        sgp, sgs = _sigmoid(gl_pool), _sigmoid(gl_ssm)
        dmo = (_row(prm_ref, 5) * dv).astype(MXU_DTYPE)
        a_wo[...] += _mm_tn(sgp * y_p + sgs * y_s, dmo)
        dmerged = _mm_nt(dmo, w_o[...])
        dy_pool = dmerged * sgp
        dgl_pool = dmerged * y_p * (sgp * (1.0 - sgp))
        dy_ssm = dmerged * sgs
        dgl_ssm = dmerged * y_s * (sgs * (1.0 - sgs))

        scale = mv_ref[ROW_POOL_SCALE:ROW_POOL_SCALE + 1, 0:512]
        qv, zpv = q_ref[...].astype(F32), zp_ref[...]
        a_wpu[...] += _mm_tn(qv * scale, dy_pool)
        dp = _mm_nt(dy_pool, w_pu[...])
        dq = dp * scale
        a_vs[0:1, 0:512] += _colsum(dp * qv)
        a_vs[1:2, 0:512] += _colsum(dq)
        dzp_blocks = []
        for k in range(4):
            lanes = slice(k * 128, (k + 1) * 128)
            dzp_blocks.append(_mm_nt(dq[:, lanes], pwb[k]))
            a_pw[k] += _mm_tn(zpv[:, lanes], dq[:, lanes])
        dzp = jnp.concatenate(dzp_blocks, axis=1)
        t1 = (lax.broadcasted_iota(jnp.int32, (tm, 1), 0) + (tile * tm + 1)).astype(F32)
        gs = dzp / jnp.minimum(t1, _window_lanes())
        n_ext = tm + POOL_HALO
        ext = jnp.concatenate([gs, hist[...]], axis=0)
        v2 = ext + pltpu.roll(ext, n_ext - 1, 0)
        v4 = v2[:, 128:] + pltpu.roll(v2[:, 128:], n_ext - 2, 0)
        v8 = v4[:, 128:] + pltpu.roll(v4[:, 128:], n_ext - 4, 0)
        v16 = v8[:, 128:] + pltpu.roll(v8[:, 128:], n_ext - 8, 0)
        msum = jnp.concatenate([v2[:tm, :128], v4[:tm, :128], v8[:tm, :128], v16[:tm]], axis=1)
        hist[...] = gs[0:POOL_HALO, :]
        du_pool = msum - dzp

        vgv = vg_ref[...].astype(F32)
        val, gate = vgv[:, 0:512], vgv[:, 512:1024]
        sgg = _sigmoid(gate)
        a_wsu[...] += _mm_tn(val * sgg, dy_ssm)
        do = _mm_nt(dy_ssm, w_su[...])
        dvg = jnp.concatenate([do * sgg, do * val * (sgg * (1.0 - sgg))], axis=1)
        a_vs[3:4, :] += _colsum(dvg)
        yv = yss_ref[...].astype(F32)
        a_wglu[...] += _mm_tn(_gelu(yv), dvg)
        dyss = _mm_nt(dvg, w_glu[...]) * _gelu_grad(yv)
        a_vs[2:3, 0:512] += _colsum(dyss * u_ssm)
        du_blocks = []
        for j in range(SSM_BLOCKS):
            lanes = pl.ds(j * SSM_BLOCK_STATE, SSM_BLOCK_STATE)
            in_lanes = slice(j * 128, (j + 1) * 128)
            dyb = dyss[:, in_lanes].astype(MXU_DTYPE)
            ub = u_ssm[:, in_lanes].astype(MXU_DTYPE)
            dre[0] = _mm_nt(dyb, ct[0, j])
            dre[1] = -_mm_nt(dyb, ct[1, j])
            a_ct[0, j] += _mm_tn(sre_ref[:, lanes], dyb)
            a_ct[1, j] -= _mm_tn(sim_ref[:, lanes], dyb)
            a1r, a1i, a2r, a2i, a4r, a4i, pr, pi = [coef[j, k] for k in range(8)]
            rowi = lax.broadcasted_iota(jnp.int32, (SCAN_ROWS, SSM_BLOCK_STATE), 0)

            def step(tt, c, lanes=lanes, a1r=a1r, a1i=a1i, a2r=a2r, a2i=a2i, a4r=a4r, a4i=a4i, pr=pr, pi=pi, rowi=rowi):
                cr, ci, acc_r, acc_i = c
                rows = pl.ds(pl.multiple_of((n_tiles - 1 - tt) * SCAN_ROWS, SCAN_ROWS), SCAN_ROWS)
                xr, xi = dre[0, rows, :], dre[1, rows, :]
                for dstep, kr, ki in ((1, a1r, a1i), (2, a2r, a2i), (4, a4r, a4i)):
                    sr, si = pltpu.roll(xr, SCAN_ROWS - dstep, 0), pltpu.roll(xi, SCAN_ROWS - dstep, 0)
                    xr, xi = xr + kr * sr + ki * si, xi + kr * si - ki * sr
                xr, xi = xr + pr * cr + pi * ci, xi + pr * ci - pi * cr
                lam[0, rows, :] = xr
                lam[1, rows, :] = xi
                nr = jnp.where(rowi == SCAN_ROWS - 1, cr, pltpu.roll(xr, SCAN_ROWS - 1, 0))
                ni = jnp.where(rowi == SCAN_ROWS - 1, ci, pltpu.roll(xi, SCAN_ROWS - 1, 0))
                s_r, s_i = sre_ref[rows, lanes], sim_ref[rows, lanes]
                acc_r = acc_r + nr * s_r + ni * s_i
                acc_i = acc_i + ni * s_r - nr * s_i
                return (jnp.broadcast_to(xr[0:1, :], xr.shape), jnp.broadcast_to(xi[0:1, :], xi.shape), acc_r, acc_i)

            cr, ci, acc_r, acc_i = lax.fori_loop(0, n_tiles, step, (carry[j, 0], carry[j, 1], a_da[0, j], a_da[1, j]))
            carry[j, 0] = cr
            carry[j, 1] = ci
            a_da[0, j] = acc_r
            a_da[1, j] = acc_i
            lr_b, li_b = lam[0].astype(MXU_DTYPE), lam[1].astype(MXU_DTYPE)
            a_bb[0, j] += _mm_tn(ub, lr_b)
            a_bb[1, j] += _mm_tn(ub, li_b)
            du_blocks.append(_mm_nt(lr_b, bb[0, j]) + _mm_nt(li_b, bb[1, j]))
        du_ssm = jnp.concatenate(du_blocks, axis=1) + dyss * mv_ref[ROW_SSM_D:ROW_SSM_D + 1, 0:512]
        dz_ref[...] = jnp.concatenate([du_pool, du_ssm, dgl_pool, dgl_ssm], axis=1).astype(SAVE_DTYPE)

        @pl.when(i == nt - 1)
        def _():
            rows = D_MODEL // NDEV
            for k in range(NDEV):
                st_wo[k] = a_wo[k * rows:(k + 1) * rows, :].astype(WIRE_DTYPE)
                for a, acc in enumerate((a_wpu, a_wglu, a_wsu)):
                    st_up[a, k] = acc[:, k * 128:(k + 1) * 128].astype(WIRE_DTYPE)
            outs = ((st_wo, dwo_h), (st_up.at[0], dwpu_h), (st_up.at[1], dwglu_h), (st_up.at[2], dwsu_h),
                    (a_pw, dpw_h), (a_bb, dbb_h), (a_ct, dct_h), (a_vs, vsum_h), (a_da, da_h))
            cps = [pltpu.make_async_copy(src, dst, sems.at[k]) for k, (src, dst) in enumerate(outs)]
            for cp in cps:
                cp.start()
            for cp in cps:
                cp.wait()

    def tok(width):
        return pl.BlockSpec((tm, width), lambda i: (nt - 1 - i, 0))

    hbm = _HBM
    acc_shapes = [(D_MODEL, D_MODEL), (512, D_MODEL), (512, D_MODEL), (512, D_MODEL), (4, 128, 128),
                  (2, SSM_BLOCKS, 128, SSM_BLOCK_STATE), (2, SSM_BLOCKS, SSM_BLOCK_STATE, 128), (8, D_MODEL),
                  (2, SSM_BLOCKS, SCAN_ROWS, SSM_BLOCK_STATE)]
    stack_out = [jax.ShapeDtypeStruct((NDEV, D_MODEL // NDEV, D_MODEL), WIRE_DTYPE)] \
        + [jax.ShapeDtypeStruct((NDEV, 512, 128), WIRE_DTYPE)] * 3
    return _launch(
        body, "mixer_backward", grid=(nt,), semantics=("arbitrary",), carry=carry, steps=_grid_steps(nt),
        out_shape=[jax.ShapeDtypeStruct((T, IN_WIDTH), SAVE_DTYPE)] + stack_out
        + [jax.ShapeDtypeStruct(s, F32) for s in acc_shapes[4:]],
        in_specs=[tok(D_MODEL), _resident(prm), tok(IN_WIDTH), tok(N_STATE), tok(N_STATE), tok(512), tok(512),
                  tok(D_MODEL), tok(512), tok(D_MODEL), tok(D_MODEL), hbm, hbm, hbm, hbm, _resident(pool_w),
                  _resident(mvec), _resident(srow), _resident(bb), _resident(ct)],
        out_specs=[tok(IN_WIDTH)] + [hbm] * len(acc_shapes),
        operands=(d2, prm, z, s_re, s_im, zp, q, y_pool, yss, vg, y_ssm, w_pu_s, w_glu_s, w_su_s, w_out, pool_w, mvec,
                  srow, bb, ct),
        scratch=[
            pltpu.VMEM((512, D_MODEL), MXU_DTYPE), pltpu.VMEM((512, D_MODEL), MXU_DTYPE),
            pltpu.VMEM((512, D_MODEL), MXU_DTYPE), pltpu.VMEM((D_MODEL, D_MODEL), MXU_DTYPE),
            pltpu.VMEM((4, 128, 128), MXU_DTYPE),
            pltpu.VMEM((SSM_BLOCKS, 8, SCAN_ROWS, SSM_BLOCK_STATE), F32),
            pltpu.VMEM((SSM_BLOCKS, 2, SCAN_ROWS, SSM_BLOCK_STATE), F32),
            pltpu.VMEM((POOL_HALO, POOL_WIDTH), F32),
            pltpu.VMEM((2, tm, SSM_BLOCK_STATE), F32), pltpu.VMEM((2, tm, SSM_BLOCK_STATE), F32),
        ] + [pltpu.VMEM(s, F32) for s in acc_shapes]
        + [pltpu.VMEM((NDEV, D_MODEL // NDEV, D_MODEL), WIRE_DTYPE), pltpu.VMEM((3, NDEV, 512, 128), WIRE_DTYPE),
           pltpu.SemaphoreType.DMA((25,))])


def _mixer_in_backward(x, dz, prm, w_in_s):
    T = x.shape[0]
    tm = min(T, TM_MIX)
    nt = T // tm
    cols = IN_WIDTH // NDEV

    def body(x_ref, dz_ref, prm_ref, w_in_h, dh_ref, dw_ref, w_in, acc, sems):
        i = pl.program_id(0)

        @pl.when(i == 0)
        def _():
            cps = _load_stack(w_in_h, w_in, sems, 0)
            acc[...] = jnp.zeros_like(acc)
            for cp in cps:
                cp.wait()

        h, _, _ = _modulated(x_ref[...], prm_ref, 1, ROW_G_MIX)
        dzb = dz_ref[...].astype(MXU_DTYPE)
        dh_ref[0] = _mm_nt(dzb, w_in[...]).astype(SAVE_DTYPE)
        acc[...] += _mm_tn(h, dzb)

        @pl.when(i == nt - 1)
        def _():
            for k in range(NDEV):
                dw_ref[k] = acc[:, k * cols:(k + 1) * cols].astype(WIRE_DTYPE)

    return pl.pallas_call(
        body, name="mixer_in_backward", grid=(nt,),
        out_shape=[jax.ShapeDtypeStruct((1, T, D_MODEL), SAVE_DTYPE), jax.ShapeDtypeStruct((NDEV, D_MODEL, cols), WIRE_DTYPE)],
        in_specs=[pl.BlockSpec((tm, D_MODEL), lambda i: (i, 0)), pl.BlockSpec((tm, IN_WIDTH), lambda i: (i, 0)),
                  pl.BlockSpec(prm.shape, lambda i: (0, 0)), pl.BlockSpec(memory_space=pl.ANY)],
        out_specs=[pl.BlockSpec((1, tm, D_MODEL), lambda i: (0, i, 0)),
                   pl.BlockSpec((NDEV, D_MODEL, cols), lambda i: (0, 0, 0))],
        scratch_shapes=[pltpu.VMEM((D_MODEL, IN_WIDTH), MXU_DTYPE), pltpu.VMEM((D_MODEL, IN_WIDTH), F32),
                        pltpu.SemaphoreType.DMA((8,))],
        compiler_params=_params("arbitrary"),
    )(x, dz, prm, w_in_s)


def _ssm_dense_backward(dbb, da, srow, b_dense):
    def body(dbb_ref, da_ref, srow_ref, bd_ref, db_ref, df_ref):
        df_re, df_im = [], []
        da_re = [_colsum(da_ref[0, j]) for j in range(SSM_BLOCKS)]
        da_im = [_colsum(da_ref[1, j]) for j in range(SSM_BLOCKS)]
        for j in range(SSM_BLOCKS):
            lanes = slice(j * SSM_BLOCK_STATE, (j + 1) * SSM_BLOCK_STATE)
            f_re, f_im = srow_ref[2:3, lanes], srow_ref[3:4, lanes]
            g_re, g_im = dbb_ref[0, j], dbb_ref[1, j]
            b_re, b_im = bd_ref[0, j], bd_ref[1, j]
            db_ref[0, j] = f_re * g_re + f_im * g_im
            db_ref[1, j] = f_re * g_im - f_im * g_re
            df_re.append(_colsum(g_re * b_re + g_im * b_im))
            df_im.append(_colsum(g_im * b_re - g_re * b_im))
        df_ref[...] = jnp.concatenate([jnp.concatenate(df_re, axis=1), jnp.concatenate(df_im, axis=1),
                                       jnp.concatenate(da_re, axis=1), jnp.concatenate(da_im, axis=1),
                                       jnp.zeros((4, N_STATE), F32)], axis=0)

    return pl.pallas_call(body, name="ssm_dense_backward",
                          out_shape=[jax.ShapeDtypeStruct(b_dense.shape, F32), jax.ShapeDtypeStruct((8, N_STATE), F32)],
                          compiler_params=pltpu.CompilerParams(vmem_limit_bytes=VMEM_LIMIT))(dbb, da, srow, b_dense)


def _adamw_update(w, g, m, v):
    m = ADAM_B1 * m + (1.0 - ADAM_B1) * g
    v = ADAM_B2 * v + (1.0 - ADAM_B2) * (g * g)
    m_hat = m / (1.0 - ADAM_B1 ** ADAM_STEP)
    v_hat = v / (1.0 - ADAM_B2 ** ADAM_STEP)
    delta = -ADAM_LR * (m_hat / (jnp.sqrt(v_hat) + ADAM_EPS) + ADAM_WD * w)
    return delta, m, v


def _adam_rows(shape):
    rows, cols = shape
    tr = rows
    while tr * cols * 4 > (1 << 20) and tr % 16 == 0:
        tr //= 2
    return tr


def _adam_sharded(w, m, v, land, order, name):
    R, C = w.shape
    tr = _adam_rows((R, C))

    def body(w_ref, m_ref, v_ref, land_ref, order_ref, g_ref, d_ref, mo_ref, vo_ref):
        g = land_ref[0].astype(F32)
        for b in range(1, NDEV):
            g = g + land_ref[b].astype(F32)
        g_ref[...] = g
        d_ref[...], mo_ref[...], vo_ref[...] = _adamw_update(w_ref[...], g, m_ref[...], v_ref[...])

    blk = pl.BlockSpec((tr, C), lambda i: (i, 0))
    return pl.pallas_call(
        body, name=name, grid=(R // tr,),
        out_shape=[jax.ShapeDtypeStruct((R, C), F32)] * 4,
        in_specs=[blk, blk, blk, pl.BlockSpec((NDEV, tr, C), lambda i: (0, i, 0)), _HBM],
        out_specs=[blk] * 4,
        compiler_params=_params("arbitrary"),
    )(w, m, v, land, order)


def _adam_ada(w, m, v, sc_all, dmod_cols):
    R, C = w.shape
    tr = 256

    def body(w_ref, m_ref, v_ref, sc_ref, dm_ref, g_ref, d_ref, mo_ref, vo_ref):
        g = _mm_tn(sc_ref[...], dm_ref[...])
        g_ref[...] = g
        d_ref[...], mo_ref[...], vo_ref[...] = _adamw_update(w_ref[...], g, m_ref[...], v_ref[...])

    blk = pl.BlockSpec((tr, C), lambda i: (i, 0))
    return pl.pallas_call(
        body, name="adam_w_ada", grid=(R // tr,),
        out_shape=[jax.ShapeDtypeStruct((R, C), F32)] * 4,
        in_specs=[blk, blk, blk, pl.BlockSpec((8, tr), lambda i: (0, i)), pl.BlockSpec((8, C), lambda i: (0, 0))],
        out_specs=[blk] * 4,
        compiler_params=_params("arbitrary"),
    )(w, m, v, sc_all, dmod_cols)


def _adam_small(w, g, m, v, name):
    def body(w_ref, g_ref, m_ref, v_ref, d_ref, mo_ref, vo_ref):
        d_ref[...], mo_ref[...], vo_ref[...] = _adamw_update(w_ref[...], g_ref[...], m_ref[...], v_ref[...])

    return pl.pallas_call(body, name=name, out_shape=[jax.ShapeDtypeStruct(w.shape, F32)] * 3,
                          compiler_params=pltpu.CompilerParams(vmem_limit_bytes=VMEM_LIMIT))(w, g, m, v)


def _block_diag_in(b):
    bt = jnp.transpose(b, (0, 2, 1)).reshape(SSM_BLOCKS, 8, SSM_GROUP, SSM_STATE)
    eye = jnp.eye(8, dtype=bool)[None, :, None, :, None]
    return jnp.where(eye, bt[:, :, :, None, :], 0.0).reshape(SSM_BLOCKS, 128, SSM_BLOCK_STATE)


def _block_diag_out(c):
    ct = jnp.transpose(c, (0, 2, 1)).reshape(SSM_BLOCKS, 8, SSM_STATE, SSM_GROUP)
    eye = jnp.eye(8, dtype=bool)[None, :, None, :, None]
    return jnp.where(eye, ct[:, :, :, None, :], 0.0).reshape(SSM_BLOCKS, SSM_BLOCK_STATE, 128)


def _diag_blocks(dense, rows, cols):
    d5 = dense.reshape(SSM_BLOCKS, 8, rows, 8, cols)
    return jnp.stack([d5[:, a, :, a, :] for a in range(8)], axis=1).reshape(32, rows, cols)


def _pack_small(ada_vec, parts, params):
    rest_rows, rows = _pack_rows(params, ada_vec is not None)
    rest = jnp.concatenate([parts[n].reshape(-1) for n, _ in params])
    rest = jnp.pad(rest, (0, NDEV * rest_rows * 128 - rest.shape[0])).reshape(NDEV, rest_rows, 128)
    head = [] if ada_vec is None else [ada_vec.reshape(NDEV, ADA_ROWS, 128)]
    pad = rows - rest_rows - (0 if ada_vec is None else ADA_ROWS)
    return jnp.concatenate(head + [rest] + ([jnp.zeros((NDEV, pad, 128), F32)] if pad else []), axis=1)


def _unpack_small(pack, shapes, params, with_ada):
    rest_rows, _ = _pack_rows(params, with_ada)
    first = ADA_ROWS if with_ada else 0
    ada_vec = pack[:, :first].reshape(-1) if with_ada else None
    rest = pack[:, first:first + rest_rows].reshape(-1)
    out, off = {}, 0
    for n, size in params:
        out[n] = rest[off:off + size].reshape(shapes[n])
        off += size
    return ada_vec, out


WEIGHT_ORDER = ('w_ada', 'b_ada', 'g_ffn1', 'w_ffn1_in', 'w_ffn1_out', 'g_mix', 'w_in', 'pool_w', 'pool_b',
                'pool_scale', 'w_pool_up', 'ssm_lam_re_log', 'ssm_lam_im', 'ssm_log_dt', 'ssm_b_re', 'ssm_b_im',
                'ssm_c_re', 'ssm_c_im', 'ssm_d', 'w_glu', 'b_glu', 'w_ssm_up', 'w_out', 'g_ffn2', 'w_ffn2_in',
                'w_ffn2_out', 'g_final')
GATHERED = ('w_ffn1_in', 'w_ffn1_out', 'w_in', 'w_pool_up', 'w_glu', 'w_ssm_up', 'w_out', 'w_ffn2_in', 'w_ffn2_out')
TRANSPOSED = ('w_ffn1_in', 'w_ffn2_in')


def kernel(x, c, w_ada, b_ada, g_ffn1, w_ffn1_in, w_ffn1_out, g_mix, w_in, pool_w, pool_b, pool_scale, w_pool_up, ssm_lam_re_log, ssm_lam_im, ssm_log_dt, ssm_b_re, ssm_b_im, ssm_c_re, ssm_c_im, ssm_d, w_glu, b_glu, w_ssm_up, w_out, g_ffn2, w_ffn2_in, w_ffn2_out, g_final, loss_target, m_w_ada, m_b_ada, m_g_ffn1, m_w_ffn1_in, m_w_ffn1_out, m_g_mix, m_w_in, m_pool_w, m_pool_b, m_pool_scale, m_w_pool_up, m_ssm_lam_re_log, m_ssm_lam_im, m_ssm_log_dt, m_ssm_b_re, m_ssm_b_im, m_ssm_c_re, m_ssm_c_im, m_ssm_d, m_w_glu, m_b_glu, m_w_ssm_up, m_w_out, m_g_ffn2, m_w_ffn2_in, m_w_ffn2_out, m_g_final, v_w_ada, v_b_ada, v_g_ffn1, v_w_ffn1_in, v_w_ffn1_out, v_g_mix, v_w_in, v_pool_w, v_pool_b, v_pool_scale, v_w_pool_up, v_ssm_lam_re_log, v_ssm_lam_im, v_ssm_log_dt, v_ssm_b_re, v_ssm_b_im, v_ssm_c_re, v_ssm_c_im, v_ssm_d, v_w_glu, v_b_glu, v_w_ssm_up, v_w_out, v_g_ffn2, v_w_ffn2_in, v_w_ffn2_out, v_g_final):
    args = locals()
    W = {n: args[n] for n in WEIGHT_ORDER}
    M = {n: args["m_" + n] for n in WEIGHT_ORDER}
    V = {n: args["v_" + n] for n in WEIGHT_ORDER}
    shapes = {n: W[n].shape for n in WEIGHT_ORDER}
    xt, tgt = x[0], loss_target[0]

    def local(tree, n):
        return jnp.swapaxes(tree[n][0], 0, 1) if n in TRANSPOSED else tree[n][0]

    def as_output(n, a):
        return (jnp.swapaxes(a, 0, 1) if n in TRANSPOSED else a)[None]

    shard = dict(zip(GATHERED, _cast_shards([local(W, n) for n in GATHERED])))
    stacks = {}

    def gather(names):
        return _Gather([shard[n] for n in names])

    def gathered(names, results):
        stacks.update(zip(names, results))

    ffn1_w, ffn2_w = ('w_ffn1_in', 'w_ffn1_out'), ('w_ffn2_in', 'w_ffn2_out')
    mix_w = ('w_in', 'w_pool_up', 'w_glu', 'w_ssm_up', 'w_out')
    mod_cols, sc_all, *res = _ada_forward(c, W['w_ada'][0], b_ada.reshape(NDEV, -1), gather(ffn1_w[:1]))
    gathered(ffn1_w[:1], res)
    win1 = stacks['w_ffn1_in'].reshape(2, 4, FF_SHARD, D_MODEL)
    prm = jnp.concatenate([mod_cols.reshape(9, D_MODEL), g_ffn1, g_mix, g_ffn2, g_final[None], jnp.zeros((3, D_MODEL), F32)], axis=0)
    pad512 = jnp.zeros((1, D_MODEL - 512), F32)
    mvec = jnp.concatenate([jnp.concatenate([pool_b, pad512], axis=1), jnp.concatenate([pool_scale, pad512], axis=1),
                            jnp.concatenate([ssm_d, pad512], axis=1), b_glu, jnp.zeros((4, D_MODEL), F32)], axis=0)
    log_dt_col = ssm_log_dt[0][:, None]
    coeffs = _ssm_params_forward(ssm_lam_re_log[0], ssm_lam_im[0], log_dt_col)
    srow = jnp.stack([t.reshape(N_STATE) for t in coeffs], axis=0)
    b_dense = jnp.stack([_block_diag_in(ssm_b_re[0]), _block_diag_in(ssm_b_im[0])], axis=0)
    c_dense = jnp.stack([_block_diag_out(ssm_c_re[0]), _block_diag_out(ssm_c_im[0])], axis=0)
    bb, ct = _ssm_dense_forward(srow, b_dense, c_dense)
    pw = pool_w[0]

    next_w = ffn1_w[1:] + mix_w[:1]
    ab1, s1, *res = _ffn_hidden(xt, prm, win1, 0, ROW_G_FFN1, "ffn1_hidden", gather(next_w))
    gathered(next_w, res)
    wout1 = stacks['w_ffn1_out'].reshape(4, FF_SHARD, D_MODEL)
    x1, f1, *res = _ffn_out(xt, s1, prm, wout1, 0, "ffn1_out", gather(mix_w[1:]))
    gathered(mix_w[1:], res)
    w_out_full = stacks['w_out'].reshape(D_MODEL, D_MODEL)
    res = _mixer_forward(x1, prm, stacks['w_in'], stacks['w_pool_up'], stacks['w_glu'], stacks['w_ssm_up'],
                         w_out_full, pw, mvec, srow, bb, ct, gather(ffn2_w))
    x2, mo, saved = res[0], res[1], res[2:11]
    gathered(ffn2_w, res[11:])
    win2 = stacks['w_ffn2_in'].reshape(2, 4, FF_SHARD, D_MODEL)
    wout2 = stacks['w_ffn2_out'].reshape(4, FF_SHARD, D_MODEL)
    x3, f3, ab3 = _ffn_forward(x2, prm, win2, wout2, 2, ROW_G_FFN2, "ffn2_forward")
    d3, fin = _final_loss(x3, tgt, prm)
    loss = lax.psum(fin[1, 0], ("x", "y", "c"))

    lands = {}

    def scatter(grads):
        names = list(grads)
        return _Scatter([grads[n][0] for n in names], [grads[n][1] for n in names], [local(W, n).shape for n in names])

    def scattered(grads, results):
        lands.update(zip(grads, results))

    parts3, dwin2, dwout2 = _ffn_backward(x2, d3, ab3, prm, win2, wout2, prm, 2, ROW_G_FFN2, "ffn2_backward")
    d2, sums3 = _norm_backward(parts3, x2, d3, f3, prm, 2, ROW_G_FFN2, 0.5, "ffn2_norm_backward")
    g_ffn2_w = {'w_ffn2_in': (dwin2, _halves), 'w_ffn2_out': (dwout2.reshape(NDEV, -1, D_MODEL), _stacked)}
    res = _mixer_backward(d2, prm, saved, stacks['w_pool_up'], stacks['w_glu'], stacks['w_ssm_up'], w_out_full, pw, mvec,
                          srow, bb, ct, scatter(g_ffn2_w))
    dz, dwo, dwpu, dwglu, dwsu, dpw, dbb, dct, vsum, da = res[:10]
    scattered(g_ffn2_w, res[10:])
    parts2, dwin_mix = _mixer_in_backward(x1, dz, prm, stacks['w_in'])
    d1, sums2 = _norm_backward(parts2, x1, d2, mo, prm, 1, ROW_G_MIX, 1.0, "mixer_norm_backward")
    g_mix_w = {'w_in': (dwin_mix, _stacked), 'w_pool_up': (dwpu, _stacked), 'w_glu': (dwglu, _stacked),
               'w_ssm_up': (dwsu, _stacked), 'w_out': (dwo, _stacked)}
    db_dense, df_rows = _ssm_dense_backward(dbb, da, srow, b_dense)
    cot = [df_rows[r].reshape(32, 64) for r in (2, 3, 0, 1)]
    d_lrl, d_li, d_ldt = _ssm_params_backward(ssm_lam_re_log[0], ssm_lam_im[0], log_dt_col, cot)
    small_grads = {
        'g_mix': sums2[0], 'g_ffn2': sums3[0], 'g_final': fin[0], 'pool_w': dpw,
        'pool_b': vsum[1, :512], 'pool_scale': vsum[0, :512], 'ssm_lam_re_log': d_lrl, 'ssm_lam_im': d_li,
        'ssm_log_dt': d_ldt, 'ssm_b_re': jnp.transpose(_diag_blocks(db_dense[0], SSM_GROUP, SSM_STATE), (0, 2, 1)),
        'ssm_b_im': jnp.transpose(_diag_blocks(db_dense[1], SSM_GROUP, SSM_STATE), (0, 2, 1)),
        'ssm_c_re': jnp.transpose(_diag_blocks(dct[0], SSM_STATE, SSM_GROUP), (0, 2, 1)),
        'ssm_c_im': jnp.transpose(_diag_blocks(dct[1], SSM_STATE, SSM_GROUP), (0, 2, 1)),
        'ssm_d': vsum[2, :512], 'b_glu': vsum[3],
    }
    total_early, _ = _allreduce_small(_pack_small(None, small_grads, SMALL_EARLY), prm, "allreduce_early")
    parts1, dwin1, dwout1, *res = _ffn_backward(xt, d1, ab1, prm, win1, wout1, total_early, 0, ROW_G_FFN1,
                                                "ffn1_backward", scatter(g_mix_w))
    scattered(g_mix_w, res)

    g_ffn1_w = {'w_ffn1_in': (dwin1, _halves), 'w_ffn1_out': (dwout1.reshape(NDEV, -1, D_MODEL), _stacked)}
    last_views = [g_ffn1_w[n][1] for n in ffn1_w]
    send_sems, recv_sems, last_src, last_land, token = _scatter_start(
        [g_ffn1_w[n][0] for n in ffn1_w], last_views, [local(W, n).shape for n in ffn1_w],
        [jnp.full((8, 128), loss, F32)])
    after_start = token[0:1, 0:1]
    d0, sums1 = _norm_backward(parts1, xt, d1, f1, prm + after_start, 0, ROW_G_FFN1, 0.5, "ffn1_norm_backward")

    grad, delta, new_m, new_v = {}, {}, {}, {}

    def adam_sharded(n):
        res = _adam_sharded(local(W, n), local(M, n), local(V, n), lands[n], token, "adam_" + n)
        grad[n], delta[n], new_m[n], new_v[n] = [as_output(n, r) for r in res]
        return res[3]

    def adam_small(params, ada, total, name, order):
        rows = _pack_rows(params, ada)[1]
        packs = [(_pack_small(t['b_ada'].reshape(-1) if ada else None, t, params) + order).reshape(NDEV * rows, 128)
                 for t in (W, M, V)]
        res = _adam_small(packs[0], total.reshape(NDEV * rows, 128), packs[1], packs[2], name)
        for dst, packed in zip((grad, delta, new_m, new_v), (total, *res)):
            ada_vec, rest = _unpack_small(packed.reshape(NDEV, rows, 128), shapes, params, ada)
            dst.update(rest)
            if ada:
                dst['b_ada'] = ada_vec.reshape(shapes['b_ada'])
        return res[2]

    done = [d0] + [adam_sharded(n) for n in GATHERED if n not in ffn1_w]
    done.append(adam_small(SMALL_EARLY, False, total_early, "adam_small_early", after_start))
    firsts = [dst[n].reshape(-1)[0:1] for dst in (grad, delta, new_m, new_v) for n, _ in SMALL_EARLY]
    done.append(jnp.pad(jnp.concatenate(firsts), (0, 1024 - len(firsts))).reshape(8, 128))
    lands.update(zip(ffn1_w, _scatter_wait(send_sems, recv_sems, last_src, last_land, last_views, done)))

    dmod = jnp.concatenate([sums1[1:4], sums2[1:4], sums3[1:4]], axis=0).reshape(-1)
    total_late, landed = _allreduce_small(_pack_small(dmod, {'g_ffn1': sums1[0]}, SMALL_LATE), lands[ffn1_w[1]],
                                          "allreduce_late")
    dmod_cols = landed[:, :ADA_ROWS].reshape(NDEV, ADA_ROWS * 128)
    res = _adam_ada(W['w_ada'][0], M['w_ada'][0], V['w_ada'][0], sc_all, dmod_cols)
    grad['w_ada'], delta['w_ada'], new_m['w_ada'], new_v['w_ada'] = [r[None] for r in res]
    adam_small(SMALL_LATE, True, total_late, "adam_small_late", 0.0)
    for n in ffn1_w:
        adam_sharded(n)

    return (loss, d0[None], *[grad[n] for n in WEIGHT_ORDER], *[delta[n] for n in WEIGHT_ORDER],
            *[new_m[n] for n in WEIGHT_ORDER], *[new_v[n] for n in WEIGHT_ORDER])
```

```python
import functools

import jax
import jax.numpy as jnp
from jax import lax
from jax.experimental import pallas as pl
from jax.experimental.pallas import tpu as pltpu

F32 = jnp.float32
MXU_DTYPE = jnp.bfloat16
WIRE_DTYPE = jnp.bfloat16
SAVE_DTYPE = jnp.bfloat16

NDEV = 8
D_MODEL = 1024
D_FF = 2816
FF_SHARD = 2 * D_FF // NDEV
POOL_WIDTH = 512
POOL_GROUP = 128
SSM_WIDTH = 512
SSM_STATE = 64
SSM_GROUP = 16
SSM_BLOCKS = 4
SSM_BLOCK_STATE = 512
N_STATE = 2048
IN_WIDTH = 3072
EPS = 1e-6
ADAM_LR = 0.001
ADAM_B1 = 0.9
ADAM_B2 = 0.999
ADAM_EPS = 1e-08
ADAM_WD = 0.01
ADAM_STEP = 10

TM_FFN = 512
FFN_BWD_CHUNK = 256
TM_MIX = 256
TM_MIX_BWD = 256
TM_EW = 512
SCAN_ROWS = 8
POOL_HALO = 16
VMEM_LIMIT = 60 * 1024 * 1024

ROW_G_FFN1, ROW_G_MIX, ROW_G_FFN2, ROW_G_FINAL = 9, 10, 11, 12
ROW_POOL_B, ROW_POOL_SCALE, ROW_SSM_D, ROW_B_GLU = 0, 1, 2, 3

SMALL_EARLY = (
    ("g_mix", 1024), ("g_ffn2", 1024), ("g_final", 1024), ("pool_w", 65536),
    ("pool_b", 512), ("pool_scale", 512), ("ssm_lam_re_log", 2048), ("ssm_lam_im", 2048),
    ("ssm_log_dt", 32), ("ssm_b_re", 32768), ("ssm_b_im", 32768), ("ssm_c_re", 32768),
    ("ssm_c_im", 32768), ("ssm_d", 512), ("b_glu", 1024),
)
SMALL_LATE = (("g_ffn1", 1024),)
ADA_ROWS = 9
MESH = pl.DeviceIdType.MESH


def _pack_rows(params, with_ada):
    rest = -(-sum(n for _, n in params) // (NDEV * 128))
    return rest, -(-(rest + (ADA_ROWS if with_ada else 0)) // 8) * 8


def _mm(a, b):
    return jnp.dot(a.astype(MXU_DTYPE), b.astype(MXU_DTYPE), preferred_element_type=F32)


def _mm_nt(a, b):
    return lax.dot_general(a.astype(MXU_DTYPE), b.astype(MXU_DTYPE), (((1,), (1,)), ((), ())),
                           preferred_element_type=F32)


def _mm_tn(a, b):
    return lax.dot_general(a.astype(MXU_DTYPE), b.astype(MXU_DTYPE), (((0,), (0,)), ((), ())),
                           preferred_element_type=F32)


def _rms_scale(x):
    return lax.rsqrt(jnp.mean(x * x, axis=-1, keepdims=True) + EPS)


def _sigmoid(x):
    return jax.nn.sigmoid(x)


def _colsum(x):
    return jnp.sum(x, axis=0, keepdims=True)


def _row(ref, r):
    return ref[r:r + 1, :]


def _params(*sem):
    return pltpu.CompilerParams(dimension_semantics=sem, vmem_limit_bytes=VMEM_LIMIT)


def _resident(a):
    return pl.BlockSpec(a.shape, lambda *_: (0,) * a.ndim, pipeline_mode=pl.Buffered(1))


def _me():
    return lax.axis_index("x"), lax.axis_index("y"), lax.axis_index("c")


def _peer(rel):
    x, y, c = _me()
    px = 1 - x if rel & 4 else x
    py = 1 - y if rel & 2 else y
    pc = 1 - c if rel & 1 else c
    return (px, py, pc), 4 * px + 2 * py + pc


_HBM = pl.BlockSpec(memory_space=pl.ANY)
_HBM_ONLY = pl.BlockSpec(memory_space=pltpu.HBM)


def _stacked(ref, p):
    return ref.at[p]


def _halves(ref, p):
    return ref.at[p // 4, p % 4]


class _Gather:
    def __init__(self, shards):
        self.operands = list(shards)
        self.n = len(shards)
        self.out_shape = [jax.ShapeDtypeStruct((NDEV,) + s.shape, s.dtype) for s in shards]
        self.scratch = [pltpu.SemaphoreType.DMA((7 * self.n,)), pltpu.SemaphoreType.DMA((7 * self.n,)),
                        pltpu.SemaphoreType.DMA((self.n,))]

    def plan(self, srcs, outs, sems):
        send_sems, recv_sems, local_sems = sems
        n = self.n
        x, y, c = _me()
        me = 4 * x + 2 * y + c
        here, sibling = (x, y, c), (x, y, 1 - c)
        chips = [(1 - x, y), (x, 1 - y), (1 - x, 1 - y)]

        def blk(px, py, pc):
            return 4 * px + 2 * py + pc

        def copy(a, k, block, to, src=None):
            return pltpu.make_async_remote_copy(
                src_ref=outs[a].at[block] if src is None else src, dst_ref=outs[a].at[block],
                send_sem=send_sems.at[7 * a + k], recv_sem=recv_sems.at[7 * a + k], device_id=to, device_id_type=MESH)

        def mine(a):
            return pltpu.make_async_copy(srcs[a], outs[a].at[me], local_sems.at[a])

        def first(a):
            return [copy(a, 0, me, sibling, src=srcs[a])] + [copy(a, 1 + j, me, (*chip, c), src=srcs[a])
                                                              for j, chip in enumerate(chips)]

        def start():
            for a in range(n):
                mine(a).start()
                for cp in first(a):
                    cp.start()

        def forward():
            for a in range(n):
                for j, chip in enumerate(chips):
                    copy(a, 1 + j, blk(*chip, c), here).wait_recv()
                    copy(a, 4 + j, blk(*chip, c), sibling).start()

        def finish():
            for a in range(n):
                copy(a, 0, blk(x, y, 1 - c), here).wait_recv()
                for j, chip in enumerate(chips):
                    copy(a, 4 + j, blk(*chip, 1 - c), here).wait_recv()
            for a in range(n):
                mine(a).wait()
                for cp in first(a):
                    cp.wait_send()
                for j, chip in enumerate(chips):
                    copy(a, 4 + j, blk(*chip, c), sibling).wait_send()

        return start, forward, finish


class _Scatter:
    def __init__(self, arrays, views, shard_shapes):
        self.operands = list(arrays)
        self.views = list(views)
        self.n = len(arrays)
        self.out_shape = [jax.ShapeDtypeStruct((NDEV,) + tuple(s), a.dtype) for s, a in zip(shard_shapes, arrays)]
        self.scratch = [pltpu.SemaphoreType.DMA((7 * self.n,)), pltpu.SemaphoreType.DMA((7 * self.n,)),
                        pltpu.SemaphoreType.DMA((self.n,))]

    def plan(self, srcs, outs, sems):
        send_sems, recv_sems, local_sems = sems
        n, views = self.n, self.views
        x, y, c = _me()
        me = 4 * x + 2 * y + c

        def mine(a):
            return pltpu.make_async_copy(views[a](srcs[a], me), outs[a].at[me], local_sems.at[a])

        def copy(a, rel, sending):
            to, p = _peer(rel)
            return pltpu.make_async_remote_copy(
                src_ref=views[a](srcs[a], p), dst_ref=outs[a].at[me if sending else p],
                send_sem=send_sems.at[7 * a + rel - 1], recv_sem=recv_sems.at[7 * a + rel - 1],
                device_id=to if sending else (x, y, c), device_id_type=MESH)

        def start():
            for a in range(n):
                mine(a).start()
            for rel in range(1, 8):
                for a in range(n):
                    copy(a, rel, True).start()

        def forward():
            pass

        def finish():
            for rel in range(1, 8):
                for a in range(n):
                    copy(a, rel, False).wait_recv()
            for rel in range(1, 8):
                for a in range(n):
                    copy(a, rel, True).wait_send()
            for a in range(n):
                mine(a).wait()

        return start, forward, finish


def _launch(body, name, out_shape, in_specs, out_specs, operands, scratch=(), grid=None, semantics=None,
            carry=None, steps=None):
    out_shape, in_specs, out_specs = list(out_shape), list(in_specs), list(out_specs)
    operands, scratch = list(operands), list(scratch)
    n_in, n_out, n_scr = len(in_specs), len(out_shape), len(scratch)
    kernel_body = body
    if carry is not None:
        k = carry.n

        def kernel_body(*refs):
            ins, cin = refs[:n_in], refs[n_in:n_in + k]
            outs, cout = refs[n_in + k:n_in + k + n_out], refs[n_in + k + n_out:n_in + 2 * k + n_out]
            rest = refs[n_in + 2 * k + n_out:]
            scr, csem = rest[:n_scr], rest[n_scr:]
            start, forward, finish = carry.plan(cin, cout, csem)
            if steps is None:
                start()
                body(*ins, *outs, *scr)
                forward()
                finish()
            else:
                pl.when(steps()[0])(start)
                pl.when(steps()[1])(forward)
                body(*ins, *outs, *scr)
                pl.when(steps()[2])(finish)

        in_specs += [_HBM] * k
        out_shape += carry.out_shape
        out_specs += [_HBM] * k
        operands += carry.operands
        scratch += carry.scratch
    kwargs = {} if grid is None else {"grid": grid}
    params = pltpu.CompilerParams(vmem_limit_bytes=VMEM_LIMIT) if semantics is None else _params(*semantics)
    return pl.pallas_call(kernel_body, name=name, out_shape=out_shape, in_specs=in_specs, out_specs=out_specs,
                          scratch_shapes=scratch, compiler_params=params, **kwargs)(*operands)


def _grid_steps(nt):
    def steps():
        i = pl.program_id(0)
        return i == 0, i == nt - 1, i == nt - 1
    return steps


def _cast_shards(shards):
    n = len(shards)

    def body(*refs):
        for a in range(n):
            refs[n + a][...] = refs[a][...].astype(WIRE_DTYPE)

    return pl.pallas_call(body, name="cast_shards",
                          out_shape=[jax.ShapeDtypeStruct(s.shape, WIRE_DTYPE) for s in shards],
                          compiler_params=pltpu.CompilerParams(vmem_limit_bytes=VMEM_LIMIT))(*shards)


_SEM = pl.BlockSpec(memory_space=pltpu.SEMAPHORE)
_DATAFLOW = pltpu.SideEffectType.DATAFLOW_SIDE_EFFECTING


def _split_copy(arrays, views, landing, send_sems, recv_sems, a, rel):
    to, p = _peer(rel)
    x, y, c = _me()
    return pltpu.make_async_remote_copy(
        src_ref=views[a](arrays[a], p), dst_ref=landing[a].at[4 * x + 2 * y + c],
        send_sem=send_sems.at[NDEV * a + rel], recv_sem=recv_sems.at[NDEV * a + rel], device_id=to, device_id_type=MESH)


def _scatter_start(arrays, views, shard_shapes, after):
    n = len(arrays)
    landing = [pltpu.with_memory_space_constraint(lax.empty((NDEV,) + tuple(s), a.dtype), pltpu.HBM)
               for s, a in zip(shard_shapes, arrays)]
    arrays = [pltpu.with_memory_space_constraint(a, pltpu.HBM) for a in arrays]

    def body(*refs):
        ins, land = refs[:n], refs[n:2 * n]
        send_sems, recv_sems = refs[2 * n + len(after)], refs[2 * n + len(after) + 1]
        token = refs[-1]
        for rel in range(NDEV):
            for a in range(n):
                _split_copy(ins, views, land, send_sems, recv_sems, a, rel).start()
        token[...] = jnp.zeros_like(token)

    res = pl.pallas_call(
        body, name="scatter_start",
        out_shape=[pltpu.SemaphoreType.DMA((NDEV * n,)), pltpu.SemaphoreType.DMA((NDEV * n,))]
        + [pltpu.HBM(a.shape, a.dtype) for a in arrays] + [pltpu.HBM(l.shape, l.dtype) for l in landing]
        + [jax.ShapeDtypeStruct((8, 128), F32)],
        in_specs=[_HBM_ONLY] * (2 * n) + [_HBM] * len(after),
        out_specs=[_SEM, _SEM] + [_HBM_ONLY] * (2 * n) + [pl.BlockSpec(memory_space=pltpu.VMEM)],
        input_output_aliases={i: 2 + i for i in range(2 * n)},
        compiler_params=pltpu.CompilerParams(has_side_effects=_DATAFLOW),
    )(*arrays, *landing, *after)
    return res[0], res[1], res[2:2 + n], res[2 + n:2 + 2 * n], res[-1]


def _scatter_wait(send_sems, recv_sems, arrays, landing, views, after):
    n = len(arrays)

    def body(*refs):
        ins, land = refs[:n], refs[n:2 * n]
        send, recv = refs[2 * n], refs[2 * n + 1]
        for rel in range(NDEV):
            for a in range(n):
                cp = _split_copy(ins, views, land, send, recv, a, rel)
                cp.wait_send()
                cp.wait_recv()

    res = pl.pallas_call(
        body, name="scatter_wait",
        out_shape=[pltpu.HBM(a.shape, a.dtype) for a in arrays] + [pltpu.HBM(l.shape, l.dtype) for l in landing],
        in_specs=[_HBM_ONLY] * (2 * n) + [_SEM, _SEM] + [_HBM] * len(after),
        out_specs=[_HBM_ONLY] * (2 * n),
        input_output_aliases={i: i for i in range(2 * n)},
        compiler_params=pltpu.CompilerParams(has_side_effects=_DATAFLOW),
    )(*arrays, *landing, send_sems, recv_sems, *after)
    return res[n:]


def _ada_forward(c_row, w_ada, b_ada8, carry):
    cols = w_ada.shape[1]

    def body(c_ref, w_ref, b_ref, mod_ref, sc_ref, c_all, send_buf, recv_buf, send1, recv1, send2, recv2):
        x, y, c = _me()
        me = 4 * x + 2 * y + c
        rowi = lax.broadcasted_iota(jnp.int32, (8, D_MODEL), 0)
        c_all[me] = jnp.broadcast_to(c_ref[...], (8, D_MODEL))
        copies = []
        for rel in range(1, 8):
            to, _ = _peer(rel)
            cp = pltpu.make_async_remote_copy(src_ref=c_all.at[me], dst_ref=c_all.at[me], send_sem=send1.at[rel - 1],
                                              recv_sem=recv1.at[rel - 1], device_id=to, device_id_type=MESH)
            cp.start()
            copies.append(cp)
        for rel in range(1, 8):
            _, p = _peer(rel)
            pltpu.make_async_remote_copy(src_ref=c_all.at[p], dst_ref=c_all.at[p], send_sem=send1.at[rel - 1],
                                         recv_sem=recv1.at[rel - 1], device_id=(x, y, c), device_id_type=MESH).wait_recv()
        for cp in copies:
            cp.wait_send()
        cmat = jnp.zeros((8, D_MODEL), F32)
        for b in range(8):
            cmat = jnp.where(rowi == b, c_all[b], cmat)
        sc = cmat * _sigmoid(cmat)
        sc_ref[...] = sc
        modcols = _mm(sc, w_ref[...]) + b_ref[pl.ds(me, 1), :]
        for b in range(8):
            send_buf[b] = jnp.broadcast_to(modcols[b:b + 1, :], (8, cols))
        recv_buf[me] = send_buf[me]
        copies = []
        for rel in range(1, 8):
            to, p = _peer(rel)
            cp = pltpu.make_async_remote_copy(src_ref=send_buf.at[p], dst_ref=recv_buf.at[me], send_sem=send2.at[rel - 1],
                                              recv_sem=recv2.at[rel - 1], device_id=to, device_id_type=MESH)
            cp.start()
            copies.append(cp)
        for rel in range(1, 8):
            _, p = _peer(rel)
            pltpu.make_async_remote_copy(src_ref=send_buf.at[p], dst_ref=recv_buf.at[p], send_sem=send2.at[rel - 1],
                                         recv_sem=recv2.at[rel - 1], device_id=(x, y, c), device_id_type=MESH).wait_recv()
        for cp in copies:
            cp.wait_send()
        rowc = lax.broadcasted_iota(jnp.int32, (8, cols), 0)
        out = jnp.zeros((8, cols), F32)
        for k in range(8):
            out = jnp.where(rowc == k, recv_buf[k], out)
        mod_ref[...] = out

    return _launch(
        body, "ada_forward",
        out_shape=[jax.ShapeDtypeStruct((8, cols), F32), jax.ShapeDtypeStruct((8, D_MODEL), F32)],
        in_specs=[pl.BlockSpec(memory_space=pltpu.VMEM)] * 3,
        out_specs=[pl.BlockSpec(memory_space=pltpu.VMEM)] * 2,
        operands=(c_row, w_ada, b_ada8),
        scratch=[pltpu.VMEM((8, 8, D_MODEL), F32), pltpu.VMEM((8, 8, cols), F32), pltpu.VMEM((8, 8, cols), F32)]
        + [pltpu.SemaphoreType.DMA((7,))] * 4,
        carry=carry)


def _allreduce_small(pack, order, name):
    rows = pack.shape[1]

    def body(pack_ref, order_ref, total_ref, land_ref, send1, recv1, send2, recv2):
        x, y, c = _me()
        me = 4 * x + 2 * y + c
        land_ref[me] = pack_ref[me]
        copies = []
        for rel in range(1, 8):
            to, p = _peer(rel)
            cp = pltpu.make_async_remote_copy(src_ref=pack_ref.at[p], dst_ref=land_ref.at[me], send_sem=send1.at[rel - 1],
                                              recv_sem=recv1.at[rel - 1], device_id=to, device_id_type=MESH)
            cp.start()
            copies.append(cp)
        for rel in range(1, 8):
            _, p = _peer(rel)
            pltpu.make_async_remote_copy(src_ref=pack_ref.at[p], dst_ref=land_ref.at[p], send_sem=send1.at[rel - 1],
                                         recv_sem=recv1.at[rel - 1], device_id=(x, y, c), device_id_type=MESH).wait_recv()
        for cp in copies:
            cp.wait_send()
        acc = land_ref[0]
        for b in range(1, 8):
            acc = acc + land_ref[b]
        total_ref[me] = acc
        copies = []
        for rel in range(1, 8):
            to, _ = _peer(rel)
            cp = pltpu.make_async_remote_copy(src_ref=total_ref.at[me], dst_ref=total_ref.at[me], send_sem=send2.at[rel - 1],
                                              recv_sem=recv2.at[rel - 1], device_id=to, device_id_type=MESH)
            cp.start()
            copies.append(cp)
        for rel in range(1, 8):
            _, p = _peer(rel)
            pltpu.make_async_remote_copy(src_ref=total_ref.at[p], dst_ref=total_ref.at[p], send_sem=send2.at[rel - 1],
                                         recv_sem=recv2.at[rel - 1], device_id=(x, y, c), device_id_type=MESH).wait_recv()
        for cp in copies:
            cp.wait_send()

    return pl.pallas_call(
        body, name=name,
        out_shape=[jax.ShapeDtypeStruct((8, rows, 128), F32), jax.ShapeDtypeStruct((8, rows, 128), F32)],
        in_specs=[pl.BlockSpec(memory_space=pltpu.VMEM), _HBM],
        out_specs=[pl.BlockSpec(memory_space=pltpu.VMEM)] * 2,
        scratch_shapes=[pltpu.SemaphoreType.DMA((7,))] * 4,
        compiler_params=pltpu.CompilerParams(vmem_limit_bytes=VMEM_LIMIT),
    )(pack, order)


def _modulated(x, prm_ref, sub, g_row):
    shift, scale = _row(prm_ref, 3 * sub), _row(prm_ref, 3 * sub + 1)
    g = _row(prm_ref, g_row)
    r = _rms_scale(x)
    n0 = x * r
    return (n0 * g) * (1.0 + scale) + shift, r, n0


def _ffn_forward(x, prm, win, wout, sub, g_row, name, carry=None):
    T = x.shape[0]
    tm = min(T, TM_FFN)

    def body(x_ref, prm_ref, win_ref, wout_ref, xo_ref, f_ref, ab_ref):
        xv = x_ref[...]
        h, _, _ = _modulated(xv, prm_ref, sub, g_row)
        hb = h.astype(MXU_DTYPE)
        acc = None
        for j in range(4):
            a = _mm_nt(hb, win_ref[0, j])
            b = _mm_nt(hb, win_ref[1, j])
            ab_ref[0, j] = a.astype(SAVE_DTYPE)
            ab_ref[1, j] = b.astype(SAVE_DTYPE)
            s = (a * _sigmoid(a)) * b
            t = _mm(s, wout_ref[j])
            acc = t if acc is None else acc + t
        f_ref[...] = acc.astype(SAVE_DTYPE)
        xo_ref[...] = xv + (0.5 * _row(prm_ref, 3 * sub + 2)) * acc

    tok = pl.BlockSpec((tm, D_MODEL), lambda i: (i, 0))
    return _launch(
        body, name, grid=(T // tm,), semantics=("arbitrary",),
        out_shape=[jax.ShapeDtypeStruct((T, D_MODEL), F32), jax.ShapeDtypeStruct((T, D_MODEL), SAVE_DTYPE),
                   jax.ShapeDtypeStruct((2, 4, T, FF_SHARD), SAVE_DTYPE)],
        in_specs=[tok, _resident(prm), _resident(win), _resident(wout)],
        out_specs=[tok, tok, pl.BlockSpec((2, 4, tm, FF_SHARD), lambda i: (0, 0, i, 0))],
        operands=(x, prm, win, wout), carry=carry, steps=_grid_steps(T // tm))


def _ffn_hidden(x, prm, win, sub, g_row, name, carry=None):
    T = x.shape[0]
    tm = min(T, TM_FFN)

    def body(x_ref, prm_ref, win_ref, ab_ref, s_ref):
        h, _, _ = _modulated(x_ref[...], prm_ref, sub, g_row)
        hb = h.astype(MXU_DTYPE)
        for j in range(4):
            a = _mm_nt(hb, win_ref[0, j])
            b = _mm_nt(hb, win_ref[1, j])
            ab_ref[0, j] = a.astype(SAVE_DTYPE)
            ab_ref[1, j] = b.astype(SAVE_DTYPE)
            s_ref[j] = ((a * _sigmoid(a)) * b).astype(MXU_DTYPE)

    return _launch(
        body, name, grid=(T // tm,), semantics=("arbitrary",),
        out_shape=[jax.ShapeDtypeStruct((2, 4, T, FF_SHARD), SAVE_DTYPE), jax.ShapeDtypeStruct((4, T, FF_SHARD), MXU_DTYPE)],
        in_specs=[pl.BlockSpec((tm, D_MODEL), lambda i: (i, 0)), _resident(prm), _resident(win)],
        out_specs=[pl.BlockSpec((2, 4, tm, FF_SHARD), lambda i: (0, 0, i, 0)),
                   pl.BlockSpec((4, tm, FF_SHARD), lambda i: (0, i, 0))],
        operands=(x, prm, win), carry=carry, steps=_grid_steps(T // tm))


def _ffn_out(x, s, prm, wout, sub, name, carry=None):
    T = x.shape[0]
    tm = min(T, TM_FFN)

    def body(x_ref, s_ref, prm_ref, wout_ref, xo_ref, f_ref):
        acc = None
        for j in range(4):
            t = _mm(s_ref[j], wout_ref[j])
            acc = t if acc is None else acc + t
        f_ref[...] = acc.astype(SAVE_DTYPE)
        xo_ref[...] = x_ref[...] + (0.5 * _row(prm_ref, 3 * sub + 2)) * acc

    tok = pl.BlockSpec((tm, D_MODEL), lambda i: (i, 0))
    return _launch(
        body, name, grid=(T // tm,), semantics=("arbitrary",),
        out_shape=[jax.ShapeDtypeStruct((T, D_MODEL), F32), jax.ShapeDtypeStruct((T, D_MODEL), SAVE_DTYPE)],
        in_specs=[tok, pl.BlockSpec((4, tm, FF_SHARD), lambda i: (0, i, 0)), _resident(prm), _resident(wout)],
        out_specs=[tok, tok], operands=(x, s, prm, wout), carry=carry, steps=_grid_steps(T // tm))


def _ffn_backward(x, d, ab, prm, win, wout, order, sub, g_row, name, carry=None):
    T = x.shape[0]
    tm = min(T, TM_FFN)
    nt = T // tm
    chunk = min(tm, FFN_BWD_CHUNK)

    def body(x_ref, d_ref, ab_ref, prm_ref, win_ref, wout_ref, order_ref, dh_ref, dwin_ref, dwout_ref, acc_in, acc_out):
        i = pl.program_id(1)

        @pl.when(i == 0)
        def _():
            acc_in[...] = jnp.zeros_like(acc_in)
            acc_out[...] = jnp.zeros_like(acc_out)

        wa, wb, wo = win_ref[0, 0], win_ref[1, 0], wout_ref[0]
        half_gate = 0.5 * _row(prm_ref, 3 * sub + 2)
        das, dbs, ss, hbs, dfss = [], [], [], [], []
        for ck in range(tm // chunk):
            rows = slice(ck * chunk, (ck + 1) * chunk)
            h, _, _ = _modulated(x_ref[rows, :], prm_ref, sub, g_row)
            hbs.append(h.astype(MXU_DTYPE))
            a = ab_ref[0, 0, rows, :].astype(F32)
            b = ab_ref[1, 0, rows, :].astype(F32)
            sg = _sigmoid(a)
            si = a * sg
            dfs = (half_gate * d_ref[rows, :]).astype(MXU_DTYPE)
            ds = _mm_nt(dfs, wo)
            da = (ds * b * (sg * (1.0 + a * (1.0 - sg)))).astype(MXU_DTYPE)
            db = (ds * si).astype(MXU_DTYPE)
            dh_ref[0, rows, :] = (_mm(da, wa) + _mm(db, wb)).astype(SAVE_DTYPE)
            das.append(da)
            dbs.append(db)
            ss.append((si * b).astype(MXU_DTYPE))
            dfss.append(dfs)
        cat = (lambda v: v[0]) if len(das) == 1 else (lambda v: jnp.concatenate(v, axis=0))
        hb = cat(hbs)
        acc_out[...] += _mm_tn(cat(ss), cat(dfss))
        acc_in[0] += _mm_tn(cat(das), hb)
        acc_in[1] += _mm_tn(cat(dbs), hb)

        @pl.when(i == nt - 1)
        def _():
            dwin_ref[0, 0] = acc_in[0].astype(WIRE_DTYPE)
            dwin_ref[1, 0] = acc_in[1].astype(WIRE_DTYPE)
            dwout_ref[0] = acc_out[...].astype(WIRE_DTYPE)

    def steps():
        j, i = pl.program_id(0), pl.program_id(1)
        return (j == 0) & (i == 0), (j == 2) & (i == 0), (j == 3) & (i == nt - 1)

    tok = pl.BlockSpec((tm, D_MODEL), lambda j, i: (i, 0))
    return _launch(
        body, name, grid=(4, nt), semantics=("arbitrary", "arbitrary"),
        out_shape=[jax.ShapeDtypeStruct((4, T, D_MODEL), SAVE_DTYPE),
                   jax.ShapeDtypeStruct(win.shape, WIRE_DTYPE), jax.ShapeDtypeStruct(wout.shape, WIRE_DTYPE)],
        in_specs=[tok, tok, pl.BlockSpec((2, 1, tm, FF_SHARD), lambda j, i: (0, j, i, 0)), _resident(prm),
                  pl.BlockSpec((2, 1, FF_SHARD, D_MODEL), lambda j, i: (0, j, 0, 0)),
                  pl.BlockSpec((1, FF_SHARD, D_MODEL), lambda j, i: (j, 0, 0)), _HBM],
        out_specs=[pl.BlockSpec((1, tm, D_MODEL), lambda j, i: (j, i, 0)),
                   pl.BlockSpec((2, 1, FF_SHARD, D_MODEL), lambda j, i: (0, j, 0, 0)),
                   pl.BlockSpec((1, FF_SHARD, D_MODEL), lambda j, i: (j, 0, 0))],
        operands=(x, d, ab, prm, win, wout, order),
        scratch=[pltpu.VMEM((2, FF_SHARD, D_MODEL), F32), pltpu.VMEM((FF_SHARD, D_MODEL), F32)],
        carry=carry, steps=steps)


def _norm_backward(parts, x, d, f, prm, sub, g_row, gate_coef, name, carry=None):
    T = x.shape[0]
    tm = min(T, TM_EW)
    P = parts.shape[0]

    def body(p_ref, x_ref, d_ref, f_ref, prm_ref, dx_ref, sums_ref):
        i = pl.program_id(0)
        dh = p_ref[0].astype(F32)
        for k in range(1, P):
            dh = dh + p_ref[k].astype(F32)
        xv, dv = x_ref[...], d_ref[...]
        scale, g = _row(prm_ref, 3 * sub + 1), _row(prm_ref, g_row)
        r = _rms_scale(xv)
        n0 = xv * r
        dn = dh * (1.0 + scale)
        dn0 = dn * g
        dx_ref[...] = dv + r * (dn0 - n0 * jnp.mean(dn0 * n0, axis=-1, keepdims=True))
        upd = jnp.concatenate([_colsum(dn * n0), _colsum(dh), _colsum(dh * (n0 * g)),
                               gate_coef * _colsum(dv * f_ref[...].astype(F32)), jnp.zeros((4, D_MODEL), F32)], axis=0)

        @pl.when(i == 0)
        def _():
            sums_ref[...] = upd

        @pl.when(i > 0)
        def _():
            sums_ref[...] += upd

    tok = pl.BlockSpec((tm, D_MODEL), lambda i: (i, 0))
    return _launch(
        body, name, grid=(T // tm,), semantics=("arbitrary",),
        out_shape=[jax.ShapeDtypeStruct((T, D_MODEL), F32), jax.ShapeDtypeStruct((8, D_MODEL), F32)],
        in_specs=[pl.BlockSpec((P, tm, D_MODEL), lambda i: (0, i, 0)), tok, tok, tok, _resident(prm)],
        out_specs=[tok, pl.BlockSpec((8, D_MODEL), lambda i: (0, 0))],
        operands=(parts, x, d, f, prm), carry=carry, steps=_grid_steps(T // tm))


def _final_loss(x, target, prm):
    T = x.shape[0]
    tm = min(T, TM_EW)

    def body(x_ref, t_ref, prm_ref, dx_ref, sums_ref):
        i = pl.program_id(0)
        xv = x_ref[...]
        g = _row(prm_ref, ROW_G_FINAL)
        r = _rms_scale(xv)
        n0 = xv * r
        err = n0 * g - t_ref[...]
        dy = err / float(D_MODEL)
        dn0 = dy * g
        dx_ref[...] = r * (dn0 - n0 * jnp.mean(dn0 * n0, axis=-1, keepdims=True))
        loss = 0.5 * jnp.sum(jnp.mean(err * err, axis=-1, keepdims=True), axis=0, keepdims=True)
        upd = jnp.concatenate([_colsum(dy * n0), jnp.broadcast_to(loss, (1, D_MODEL)), jnp.zeros((6, D_MODEL), F32)], axis=0)

        @pl.when(i == 0)
        def _():
            sums_ref[...] = upd

        @pl.when(i > 0)
        def _():
            sums_ref[...] += upd

    tok = pl.BlockSpec((tm, D_MODEL), lambda i: (i, 0))
    return pl.pallas_call(
        body, name="final_loss", grid=(T // tm,),
        out_shape=[jax.ShapeDtypeStruct((T, D_MODEL), F32), jax.ShapeDtypeStruct((8, D_MODEL), F32)],
        in_specs=[tok, tok, pl.BlockSpec(prm.shape, lambda i: (0, 0))],
        out_specs=[tok, pl.BlockSpec((8, D_MODEL), lambda i: (0, 0))],
        compiler_params=_params("arbitrary"),
    )(x, target, prm)


def _ssm_discretise(lam_re_log, lam_im, log_dt):
    lr = -jnp.exp(lam_re_log)
    dt = jnp.exp(log_dt)
    mag = jnp.exp(lr * dt)
    ang = lam_im * dt
    ab_re = mag * jnp.cos(ang)
    ab_im = mag * jnp.sin(ang)
    num_re = ab_re - 1.0
    num_im = ab_im
    den = lr * lr + lam_im * lam_im
    f_re = (num_re * lr + num_im * lam_im) / den
    f_im = (num_im * lr - num_re * lam_im) / den
    return ab_re, ab_im, f_re, f_im


def _ssm_params_forward(lam_re_log, lam_im, log_dt):
    def body(a_ref, b_ref, c_ref, o0, o1, o2, o3):
        outs = _ssm_discretise(a_ref[...], b_ref[...], c_ref[...])
        for o, v in zip((o0, o1, o2, o3), outs):
            o[...] = v

    return pl.pallas_call(body, name="ssm_params_forward",
                          out_shape=[jax.ShapeDtypeStruct(lam_im.shape, F32)] * 4)(lam_re_log, lam_im, log_dt)


def _ssm_params_backward(lam_re_log, lam_im, log_dt, cot):
    def body(a_ref, b_ref, c_ref, g0, g1, g2, g3, o0, o1, o2):
        _, vjp = jax.vjp(_ssm_discretise, a_ref[...], b_ref[...], c_ref[...])
        d0, d1, d2 = vjp((g0[...], g1[...], g2[...], g3[...]))
        o0[...] = d0
        o1[...] = d1
        o2[...] = d2

    return pl.pallas_call(
        body, name="ssm_params_backward",
        out_shape=[jax.ShapeDtypeStruct(lam_im.shape, F32), jax.ShapeDtypeStruct(lam_im.shape, F32),
                   jax.ShapeDtypeStruct(log_dt.shape, F32)])(lam_re_log, lam_im, log_dt, *cot)


def _ssm_dense_forward(srow, b_dense, c_dense):
    def body(srow_ref, bd_ref, cd_ref, bb_ref, ct_ref):
        for j in range(SSM_BLOCKS):
            lanes = slice(j * SSM_BLOCK_STATE, (j + 1) * SSM_BLOCK_STATE)
            f_re, f_im = srow_ref[2:3, lanes], srow_ref[3:4, lanes]
            bb_ref[0, j] = (f_re * bd_ref[0, j] - f_im * bd_ref[1, j]).astype(MXU_DTYPE)
            bb_ref[1, j] = (f_re * bd_ref[1, j] + f_im * bd_ref[0, j]).astype(MXU_DTYPE)
            ct_ref[0, j] = cd_ref[0, j].astype(MXU_DTYPE)
            ct_ref[1, j] = cd_ref[1, j].astype(MXU_DTYPE)

    return pl.pallas_call(body, name="ssm_dense_forward",
                          out_shape=[jax.ShapeDtypeStruct(b_dense.shape, MXU_DTYPE),
                                     jax.ShapeDtypeStruct(c_dense.shape, MXU_DTYPE)],
                          compiler_params=pltpu.CompilerParams(vmem_limit_bytes=VMEM_LIMIT))(srow, b_dense, c_dense)


def _cmul(p, q):
    return p[0] * q[0] - p[1] * q[1], p[0] * q[1] + p[1] * q[0]


def _scan_coefficients(ar, ai, reverse):
    n = ar.shape[1]
    p = {1: (ar, ai)}
    p[2] = _cmul(p[1], p[1])
    p[3] = _cmul(p[2], p[1])
    p[4] = _cmul(p[2], p[2])
    p[5] = _cmul(p[4], p[1])
    p[6] = _cmul(p[4], p[2])
    p[7] = _cmul(p[4], p[3])
    p[8] = _cmul(p[4], p[4])
    rowi = lax.broadcasted_iota(jnp.int32, (SCAN_ROWS, n), 0)
    tiles = []
    for dstep in (1, 2, 4):
        keep = (rowi < SCAN_ROWS - dstep) if reverse else (rowi >= dstep)
        for part in p[dstep]:
            tiles.append(jnp.where(keep, jnp.broadcast_to(part, (SCAN_ROWS, n)), 0.0))
    for comp in (0, 1):
        t = jnp.zeros((SCAN_ROWS, n), F32)
        for rr in range(SCAN_ROWS):
            power = SCAN_ROWS - rr if reverse else rr + 1
            t = jnp.where(rowi == rr, jnp.broadcast_to(p[power][comp], (SCAN_ROWS, n)), t)
        tiles.append(t)
    return tiles


def _load_stack(stack_hbm, dst, sems, base):
    cols = stack_hbm.shape[2]
    cps = [pltpu.make_async_copy(stack_hbm.at[k], dst.at[:, pl.ds(k * cols, cols)], sems.at[base + k])
           for k in range(NDEV)]
    for cp in cps:
        cp.start()
    return cps


def _window_lanes():
    lane = lax.broadcasted_iota(jnp.int32, (1, POOL_WIDTH), 1)
    return jnp.where(lane < 128, 2.0, jnp.where(lane < 256, 4.0, jnp.where(lane < 384, 8.0, 16.0)))


def _gelu(y):
    return 0.5 * y * (1.0 + lax.erf(y * 0.7071067811865476))


def _gelu_grad(y):
    return 0.5 * (1.0 + lax.erf(y * 0.7071067811865476)) + y * jnp.exp(-0.5 * y * y) * 0.3989422804014327


def _mixer_forward(x, prm, w_in_s, w_pu_s, w_glu_s, w_su_s, w_out, pool_w, mvec, srow, bb, ct, carry=None):
    T = x.shape[0]
    tm = min(T, TM_MIX)
    nt = T // tm
    n_tiles = tm // SCAN_ROWS

    def body(x_ref, prm_ref, w_in_h, w_pu_h, w_glu_h, w_su_h, w_out_h, pw_ref, mv_ref, srow_ref, bb, ct,
             x2_ref, mo_ref, z_ref, sre_ref, sim_ref, zp_ref, q_ref, yp_ref, yss_ref, vg_ref, ys_ref,
             w_in, w_pu, w_glu, w_su, w_o, coef, carry, hist, bu, sems):
        i = pl.program_id(0)

        @pl.when(i == 0)
        def _():
            cps = (_load_stack(w_in_h, w_in, sems, 0) + _load_stack(w_pu_h, w_pu, sems, 8)
                   + _load_stack(w_glu_h, w_glu, sems, 16) + _load_stack(w_su_h, w_su, sems, 24))
            cps.append(pltpu.make_async_copy(w_out_h, w_o, sems.at[32]))
            cps[-1].start()
            for j in range(SSM_BLOCKS):
                lanes = slice(j * SSM_BLOCK_STATE, (j + 1) * SSM_BLOCK_STATE)
                for k, t in enumerate(_scan_coefficients(srow_ref[0:1, lanes], srow_ref[1:2, lanes], False)):
                    coef[j, k] = t
            carry[...] = jnp.zeros_like(carry)
            hist[...] = jnp.zeros_like(hist)
            for cp in cps:
                cp.wait()

        xv = x_ref[...]
        h, _, _ = _modulated(xv, prm_ref, 1, ROW_G_MIX)
        z = _mm(h, w_in[...])
        z_ref[...] = z.astype(SAVE_DTYPE)
        u_pool, u_ssm = z[:, 0:512], z[:, 512:1024]
        gl_pool, gl_ssm = z[:, 1024:2048], z[:, 2048:3072]

        ext = jnp.concatenate([hist[...], u_pool], axis=0)
        w2 = ext + pltpu.roll(ext, 1, 0)
        w4 = w2[:, 128:] + pltpu.roll(w2[:, 128:], 2, 0)
        w8 = w4[:, 128:] + pltpu.roll(w4[:, 128:], 4, 0)
        w16 = w8[:, 128:] + pltpu.roll(w8[:, 128:], 8, 0)
        wsum = jnp.concatenate([w2[POOL_HALO:, :128], w4[POOL_HALO:, :128], w8[POOL_HALO:, :128], w16[POOL_HALO:]], axis=1)
        hist[...] = u_pool[tm - POOL_HALO:, :]
        t1 = (lax.broadcasted_iota(jnp.int32, (tm, 1), 0) + (i * tm + 1)).astype(F32)
        zp = wsum / jnp.minimum(t1, _window_lanes()) - u_pool
        zp_ref[...] = zp.astype(SAVE_DTYPE)
        q = jnp.concatenate([_mm(zp[:, k * 128:(k + 1) * 128], pw_ref[k]) for k in range(4)], axis=1)
        q = q + mv_ref[ROW_POOL_B:ROW_POOL_B + 1, 0:512]
        q_ref[...] = q.astype(SAVE_DTYPE)
        y_pool = _mm(q * mv_ref[ROW_POOL_SCALE:ROW_POOL_SCALE + 1, 0:512], w_pu[...])
        yp_ref[...] = y_pool.astype(SAVE_DTYPE)

        y_blocks = []
        for j in range(SSM_BLOCKS):
            lanes = pl.ds(j * SSM_BLOCK_STATE, SSM_BLOCK_STATE)
            ub = u_ssm[:, j * 128:(j + 1) * 128].astype(MXU_DTYPE)
            bu[0] = _mm(ub, bb[0, j])
            bu[1] = _mm(ub, bb[1, j])
            a1r, a1i, a2r, a2i, a4r, a4i, pr, pi = [coef[j, k] for k in range(8)]

            def step(tt, c, lanes=lanes, a1r=a1r, a1i=a1i, a2r=a2r, a2i=a2i, a4r=a4r, a4i=a4i, pr=pr, pi=pi):
                cr, ci = c
                rows = pl.ds(pl.multiple_of(tt * SCAN_ROWS, SCAN_ROWS), SCAN_ROWS)
                xr, xi = bu[0, rows, :], bu[1, rows, :]
                for dstep, kr, ki in ((1, a1r, a1i), (2, a2r, a2i), (4, a4r, a4i)):
                    sr, si = pltpu.roll(xr, dstep, 0), pltpu.roll(xi, dstep, 0)
                    xr, xi = xr + kr * sr - ki * si, xi + kr * si + ki * sr
                xr, xi = xr + pr * cr - pi * ci, xi + pr * ci + pi * cr
                sre_ref[rows, lanes] = xr
                sim_ref[rows, lanes] = xi
                return (jnp.broadcast_to(xr[SCAN_ROWS - 1:SCAN_ROWS, :], xr.shape),
                        jnp.broadcast_to(xi[SCAN_ROWS - 1:SCAN_ROWS, :], xi.shape))

            cr, ci = lax.fori_loop(0, n_tiles, step, (carry[j, 0], carry[j, 1]))
            carry[j, 0] = cr
            carry[j, 1] = ci
            y_blocks.append(_mm(sre_ref[:, lanes], ct[0, j]) - _mm(sim_ref[:, lanes], ct[1, j]))
        yss = jnp.concatenate(y_blocks, axis=1) + mv_ref[ROW_SSM_D:ROW_SSM_D + 1, 0:512] * u_ssm
        yss_ref[...] = yss.astype(SAVE_DTYPE)
        vg = _mm(_gelu(yss), w_glu[...]) + mv_ref[ROW_B_GLU:ROW_B_GLU + 1, :]
        vg_ref[...] = vg.astype(SAVE_DTYPE)
        y_ssm = _mm(vg[:, 0:512] * _sigmoid(vg[:, 512:1024]), w_su[...])
        ys_ref[...] = y_ssm.astype(SAVE_DTYPE)

        merged = _sigmoid(gl_pool) * y_pool + _sigmoid(gl_ssm) * y_ssm
        mo = _mm(merged, w_o[...])
        mo_ref[...] = mo.astype(SAVE_DTYPE)
        x2_ref[...] = xv + _row(prm_ref, 5) * mo

    def tok(width):
        return pl.BlockSpec((tm, width), lambda i: (i, 0))

    hbm = _HBM
    widths = (D_MODEL, D_MODEL, IN_WIDTH, N_STATE, N_STATE, 512, 512, D_MODEL, 512, D_MODEL, D_MODEL)
    dtypes = (F32, SAVE_DTYPE, SAVE_DTYPE, F32, F32) + (SAVE_DTYPE,) * 6
    return _launch(
        body, "mixer_forward", grid=(nt,), semantics=("arbitrary",), carry=carry, steps=_grid_steps(nt),
        out_shape=[jax.ShapeDtypeStruct((T, w), dt) for w, dt in zip(widths, dtypes)],
        in_specs=[tok(D_MODEL), _resident(prm), hbm, hbm, hbm, hbm, hbm, _resident(pool_w), _resident(mvec),
                  _resident(srow), _resident(bb), _resident(ct)],
        out_specs=[tok(w) for w in widths],
        operands=(x, prm, w_in_s, w_pu_s, w_glu_s, w_su_s, w_out, pool_w, mvec, srow, bb, ct),
        scratch=[
            pltpu.VMEM((D_MODEL, IN_WIDTH), MXU_DTYPE), pltpu.VMEM((512, D_MODEL), MXU_DTYPE),
            pltpu.VMEM((512, D_MODEL), MXU_DTYPE), pltpu.VMEM((512, D_MODEL), MXU_DTYPE),
            pltpu.VMEM((D_MODEL, D_MODEL), MXU_DTYPE),
            pltpu.VMEM((SSM_BLOCKS, 8, SCAN_ROWS, SSM_BLOCK_STATE), F32),
            pltpu.VMEM((SSM_BLOCKS, 2, SCAN_ROWS, SSM_BLOCK_STATE), F32),
            pltpu.VMEM((POOL_HALO, POOL_WIDTH), F32),
            pltpu.VMEM((2, tm, SSM_BLOCK_STATE), F32),
            pltpu.SemaphoreType.DMA((33,)),
        ])


def _mixer_backward(d2, prm, saved, w_pu_s, w_glu_s, w_su_s, w_out, pool_w, mvec, srow, bb, ct, carry=None):
    z, s_re, s_im, zp, q, y_pool, yss, vg, y_ssm = saved
    T = d2.shape[0]
    tm = min(T, TM_MIX_BWD)
    nt = T // tm
    n_tiles = tm // SCAN_ROWS

    def body(d_ref, prm_ref, z_ref, sre_ref, sim_ref, zp_ref, q_ref, yp_ref, yss_ref, vg_ref, ys_ref,
             w_pu_h, w_glu_h, w_su_h, w_out_h, pw_ref, mv_ref, srow_ref, bb, ct,
             dz_ref, dwo_h, dwpu_h, dwglu_h, dwsu_h, dpw_h, dbb_h, dct_h, vsum_h, da_h,
             w_pu, w_glu, w_su, w_o, pwb, coef, carry, hist, dre, lam,
             a_wo, a_wpu, a_wglu, a_wsu, a_pw, a_bb, a_ct, a_vs, a_da, st_wo, st_up, sems):
        i = pl.program_id(0)
        tile = nt - 1 - i

        @pl.when(i == 0)
        def _():
            cps = (_load_stack(w_pu_h, w_pu, sems, 0) + _load_stack(w_glu_h, w_glu, sems, 8)
                   + _load_stack(w_su_h, w_su, sems, 16))
            cps.append(pltpu.make_async_copy(w_out_h, w_o, sems.at[24]))
            cps[-1].start()
            pwb[...] = pw_ref[...].astype(MXU_DTYPE)
            for j in range(SSM_BLOCKS):
                lanes = slice(j * SSM_BLOCK_STATE, (j + 1) * SSM_BLOCK_STATE)
                for k, t in enumerate(_scan_coefficients(srow_ref[0:1, lanes], srow_ref[1:2, lanes], True)):
                    coef[j, k] = t
            for acc in (carry, hist, a_wo, a_wpu, a_wglu, a_wsu, a_pw, a_bb, a_ct, a_vs, a_da):
                acc[...] = jnp.zeros_like(acc)
            for cp in cps:
                cp.wait()

        dv = d_ref[...]
        zt = z_ref[...].astype(F32)
        u_ssm, gl_pool, gl_ssm = zt[:, 512:1024], zt[:, 1024:2048], zt[:, 2048:3072]
        y_p, y_s = yp_ref[...].astype(F32), ys_ref[...].astype(F32)
        sgp, sgs = _sigmoid(gl_pool), _sigmoid(gl_ssm)
        dmo = (_row(prm_ref, 5) * dv).astype(MXU_DTYPE)
        a_wo[...] += _mm_tn(sgp * y_p + sgs * y_s, dmo)
        dmerged = _mm_nt(dmo, w_o[...])
        dy_pool = dmerged * sgp
        dgl_pool = dmerged * y_p * (sgp * (1.0 - sgp))
        dy_ssm = dmerged * sgs
        dgl_ssm = dmerged * y_s * (sgs * (1.0 - sgs))

        scale = mv_ref[ROW_POOL_SCALE:ROW_POOL_SCALE + 1, 0:512]
        qv, zpv = q_ref[...].astype(F32), zp_ref[...]
        a_wpu[...] += _mm_tn(qv * scale, dy_pool)
        dp = _mm_nt(dy_pool, w_pu[...])
        dq = dp * scale
        a_vs[0:1, 0:512] += _colsum(dp * qv)
        a_vs[1:2, 0:512] += _colsum(dq)
        dzp_blocks = []
        for k in range(4):
            lanes = slice(k * 128, (k + 1) * 128)
            dzp_blocks.append(_mm_nt(dq[:, lanes], pwb[k]))
            a_pw[k] += _mm_tn(zpv[:, lanes], dq[:, lanes])
        dzp = jnp.concatenate(dzp_blocks, axis=1)
        t1 = (lax.broadcasted_iota(jnp.int32, (tm, 1), 0) + (tile * tm + 1)).astype(F32)
        gs = dzp / jnp.minimum(t1, _window_lanes())
        n_ext = tm + POOL_HALO
        ext = jnp.concatenate([gs, hist[...]], axis=0)
        v2 = ext + pltpu.roll(ext, n_ext - 1, 0)
        v4 = v2[:, 128:] + pltpu.roll(v2[:, 128:], n_ext - 2, 0)
        v8 = v4[:, 128:] + pltpu.roll(v4[:, 128:], n_ext - 4, 0)
        v16 = v8[:, 128:] + pltpu.roll(v8[:, 128:], n_ext - 8, 0)
        msum = jnp.concatenate([v2[:tm, :128], v4[:tm, :128], v8[:tm, :128], v16[:tm]], axis=1)
        hist[...] = gs[0:POOL_HALO, :]
        du_pool = msum - dzp

        vgv = vg_ref[...].astype(F32)
        val, gate = vgv[:, 0:512], vgv[:, 512:1024]
        sgg = _sigmoid(gate)
        a_wsu[...] += _mm_tn(val * sgg, dy_ssm)
        do = _mm_nt(dy_ssm, w_su[...])
        dvg = jnp.concatenate([do * sgg, do * val * (sgg * (1.0 - sgg))], axis=1)
        a_vs[3:4, :] += _colsum(dvg)
        yv = yss_ref[...].astype(F32)
        a_wglu[...] += _mm_tn(_gelu(yv), dvg)
        dyss = _mm_nt(dvg, w_glu[...]) * _gelu_grad(yv)
        a_vs[2:3, 0:512] += _colsum(dyss * u_ssm)
        du_blocks = []
        for j in range(SSM_BLOCKS):
            lanes = pl.ds(j * SSM_BLOCK_STATE, SSM_BLOCK_STATE)
            in_lanes = slice(j * 128, (j + 1) * 128)
            dyb = dyss[:, in_lanes].astype(MXU_DTYPE)
            ub = u_ssm[:, in_lanes].astype(MXU_DTYPE)
            dre[0] = _mm_nt(dyb, ct[0, j])
            dre[1] = -_mm_nt(dyb, ct[1, j])
            a_ct[0, j] += _mm_tn(sre_ref[:, lanes], dyb)
            a_ct[1, j] -= _mm_tn(sim_ref[:, lanes], dyb)
            a1r, a1i, a2r, a2i, a4r, a4i, pr, pi = [coef[j, k] for k in range(8)]
            rowi = lax.broadcasted_iota(jnp.int32, (SCAN_ROWS, SSM_BLOCK_STATE), 0)

            def step(tt, c, lanes=lanes, a1r=a1r, a1i=a1i, a2r=a2r, a2i=a2i, a4r=a4r, a4i=a4i, pr=pr, pi=pi, rowi=rowi):
                cr, ci, acc_r, acc_i = c
                rows = pl.ds(pl.multiple_of((n_tiles - 1 - tt) * SCAN_ROWS, SCAN_ROWS), SCAN_ROWS)
                xr, xi = dre[0, rows, :], dre[1, rows, :]
                for dstep, kr, ki in ((1, a1r, a1i), (2, a2r, a2i), (4, a4r, a4i)):
                    sr, si = pltpu.roll(xr, SCAN_ROWS - dstep, 0), pltpu.roll(xi, SCAN_ROWS - dstep, 0)
                    xr, xi = xr + kr * sr + ki * si, xi + kr * si - ki * sr
                xr, xi = xr + pr * cr + pi * ci, xi + pr * ci - pi * cr
                lam[0, rows, :] = xr
                lam[1, rows, :] = xi
                nr = jnp.where(rowi == SCAN_ROWS - 1, cr, pltpu.roll(xr, SCAN_ROWS - 1, 0))
                ni = jnp.where(rowi == SCAN_ROWS - 1, ci, pltpu.roll(xi, SCAN_ROWS - 1, 0))
                s_r, s_i = sre_ref[rows, lanes], sim_ref[rows, lanes]
                acc_r = acc_r + nr * s_r + ni * s_i
                acc_i = acc_i + ni * s_r - nr * s_i
                return (jnp.broadcast_to(xr[0:1, :], xr.shape), jnp.broadcast_to(xi[0:1, :], xi.shape), acc_r, acc_i)

            cr, ci, acc_r, acc_i = lax.fori_loop(0, n_tiles, step, (carry[j, 0], carry[j, 1], a_da[0, j], a_da[1, j]))
            carry[j, 0] = cr
            carry[j, 1] = ci
            a_da[0, j] = acc_r
            a_da[1, j] = acc_i
            lr_b, li_b = lam[0].astype(MXU_DTYPE), lam[1].astype(MXU_DTYPE)
            a_bb[0, j] += _mm_tn(ub, lr_b)
            a_bb[1, j] += _mm_tn(ub, li_b)
            du_blocks.append(_mm_nt(lr_b, bb[0, j]) + _mm_nt(li_b, bb[1, j]))
        du_ssm = jnp.concatenate(du_blocks, axis=1) + dyss * mv_ref[ROW_SSM_D:ROW_SSM_D + 1, 0:512]
        dz_ref[...] = jnp.concatenate([du_pool, du_ssm, dgl_pool, dgl_ssm], axis=1).astype(SAVE_DTYPE)

        @pl.when(i == nt - 1)
        def _():
            rows = D_MODEL // NDEV
            for k in range(NDEV):
                st_wo[k] = a_wo[k * rows:(k + 1) * rows, :].astype(WIRE_DTYPE)
                for a, acc in enumerate((a_wpu, a_wglu, a_wsu)):
                    st_up[a, k] = acc[:, k * 128:(k + 1) * 128].astype(WIRE_DTYPE)
            outs = ((st_wo, dwo_h), (st_up.at[0], dwpu_h), (st_up.at[1], dwglu_h), (st_up.at[2], dwsu_h),
                    (a_pw, dpw_h), (a_bb, dbb_h), (a_ct, dct_h), (a_vs, vsum_h), (a_da, da_h))
            cps = [pltpu.make_async_copy(src, dst, sems.at[k]) for k, (src, dst) in enumerate(outs)]
            for cp in cps:
                cp.start()
            for cp in cps:
                cp.wait()

    def tok(width):
        return pl.BlockSpec((tm, width), lambda i: (nt - 1 - i, 0))

    hbm = _HBM
    acc_shapes = [(D_MODEL, D_MODEL), (512, D_MODEL), (512, D_MODEL), (512, D_MODEL), (4, 128, 128),
                  (2, SSM_BLOCKS, 128, SSM_BLOCK_STATE), (2, SSM_BLOCKS, SSM_BLOCK_STATE, 128), (8, D_MODEL),
                  (2, SSM_BLOCKS, SCAN_ROWS, SSM_BLOCK_STATE)]
    stack_out = [jax.ShapeDtypeStruct((NDEV, D_MODEL // NDEV, D_MODEL), WIRE_DTYPE)] \
        + [jax.ShapeDtypeStruct((NDEV, 512, 128), WIRE_DTYPE)] * 3
    return _launch(
        body, "mixer_backward", grid=(nt,), semantics=("arbitrary",), carry=carry, steps=_grid_steps(nt),
        out_shape=[jax.ShapeDtypeStruct((T, IN_WIDTH), SAVE_DTYPE)] + stack_out
        + [jax.ShapeDtypeStruct(s, F32) for s in acc_shapes[4:]],
        in_specs=[tok(D_MODEL), _resident(prm), tok(IN_WIDTH), tok(N_STATE), tok(N_STATE), tok(512), tok(512),
                  tok(D_MODEL), tok(512), tok(D_MODEL), tok(D_MODEL), hbm, hbm, hbm, hbm, _resident(pool_w),
                  _resident(mvec), _resident(srow), _resident(bb), _resident(ct)],
        out_specs=[tok(IN_WIDTH)] + [hbm] * len(acc_shapes),
        operands=(d2, prm, z, s_re, s_im, zp, q, y_pool, yss, vg, y_ssm, w_pu_s, w_glu_s, w_su_s, w_out, pool_w, mvec,
                  srow, bb, ct),
        scratch=[
            pltpu.VMEM((512, D_MODEL), MXU_DTYPE), pltpu.VMEM((512, D_MODEL), MXU_DTYPE),
            pltpu.VMEM((512, D_MODEL), MXU_DTYPE), pltpu.VMEM((D_MODEL, D_MODEL), MXU_DTYPE),
            pltpu.VMEM((4, 128, 128), MXU_DTYPE),
            pltpu.VMEM((SSM_BLOCKS, 8, SCAN_ROWS, SSM_BLOCK_STATE), F32),
            pltpu.VMEM((SSM_BLOCKS, 2, SCAN_ROWS, SSM_BLOCK_STATE), F32),
            pltpu.VMEM((POOL_HALO, POOL_WIDTH), F32),
            pltpu.VMEM((2, tm, SSM_BLOCK_STATE), F32), pltpu.VMEM((2, tm, SSM_BLOCK_STATE), F32),
        ] + [pltpu.VMEM(s, F32) for s in acc_shapes]
        + [pltpu.VMEM((NDEV, D_MODEL // NDEV, D_MODEL), WIRE_DTYPE), pltpu.VMEM((3, NDEV, 512, 128), WIRE_DTYPE),
           pltpu.SemaphoreType.DMA((25,))])


def _mixer_in_backward(x, dz, prm, w_in_s):
    T = x.shape[0]
    tm = min(T, TM_MIX)
    nt = T // tm
    cols = IN_WIDTH // NDEV

    def body(x_ref, dz_ref, prm_ref, w_in_h, dh_ref, dw_ref, w_in, acc, sems):
        i = pl.program_id(0)

        @pl.when(i == 0)
        def _():
            cps = _load_stack(w_in_h, w_in, sems, 0)
            acc[...] = jnp.zeros_like(acc)
            for cp in cps:
                cp.wait()

        h, _, _ = _modulated(x_ref[...], prm_ref, 1, ROW_G_MIX)
        dzb = dz_ref[...].astype(MXU_DTYPE)
        dh_ref[0] = _mm_nt(dzb, w_in[...]).astype(SAVE_DTYPE)
        acc[...] += _mm_tn(h, dzb)

        @pl.when(i == nt - 1)
        def _():
            for k in range(NDEV):
                dw_ref[k] = acc[:, k * cols:(k + 1) * cols].astype(WIRE_DTYPE)

    return pl.pallas_call(
        body, name="mixer_in_backward", grid=(nt,),
        out_shape=[jax.ShapeDtypeStruct((1, T, D_MODEL), SAVE_DTYPE), jax.ShapeDtypeStruct((NDEV, D_MODEL, cols), WIRE_DTYPE)],
        in_specs=[pl.BlockSpec((tm, D_MODEL), lambda i: (i, 0)), pl.BlockSpec((tm, IN_WIDTH), lambda i: (i, 0)),
                  pl.BlockSpec(prm.shape, lambda i: (0, 0)), pl.BlockSpec(memory_space=pl.ANY)],
        out_specs=[pl.BlockSpec((1, tm, D_MODEL), lambda i: (0, i, 0)),
                   pl.BlockSpec((NDEV, D_MODEL, cols), lambda i: (0, 0, 0))],
        scratch_shapes=[pltpu.VMEM((D_MODEL, IN_WIDTH), MXU_DTYPE), pltpu.VMEM((D_MODEL, IN_WIDTH), F32),
                        pltpu.SemaphoreType.DMA((8,))],
        compiler_params=_params("arbitrary"),
    )(x, dz, prm, w_in_s)


def _ssm_dense_backward(dbb, da, srow, b_dense):
    def body(dbb_ref, da_ref, srow_ref, bd_ref, db_ref, df_ref):
        df_re, df_im = [], []
        da_re = [_colsum(da_ref[0, j]) for j in range(SSM_BLOCKS)]
        da_im = [_colsum(da_ref[1, j]) for j in range(SSM_BLOCKS)]
        for j in range(SSM_BLOCKS):
            lanes = slice(j * SSM_BLOCK_STATE, (j + 1) * SSM_BLOCK_STATE)
            f_re, f_im = srow_ref[2:3, lanes], srow_ref[3:4, lanes]
            g_re, g_im = dbb_ref[0, j], dbb_ref[1, j]
            b_re, b_im = bd_ref[0, j], bd_ref[1, j]
            db_ref[0, j] = f_re * g_re + f_im * g_im
            db_ref[1, j] = f_re * g_im - f_im * g_re
            df_re.append(_colsum(g_re * b_re + g_im * b_im))
            df_im.append(_colsum(g_im * b_re - g_re * b_im))
        df_ref[...] = jnp.concatenate([jnp.concatenate(df_re, axis=1), jnp.concatenate(df_im, axis=1),
                                       jnp.concatenate(da_re, axis=1), jnp.concatenate(da_im, axis=1),
                                       jnp.zeros((4, N_STATE), F32)], axis=0)

    return pl.pallas_call(body, name="ssm_dense_backward",
                          out_shape=[jax.ShapeDtypeStruct(b_dense.shape, F32), jax.ShapeDtypeStruct((8, N_STATE), F32)],
                          compiler_params=pltpu.CompilerParams(vmem_limit_bytes=VMEM_LIMIT))(dbb, da, srow, b_dense)


def _adamw_update(w, g, m, v):
    m = ADAM_B1 * m + (1.0 - ADAM_B1) * g
    v = ADAM_B2 * v + (1.0 - ADAM_B2) * (g * g)
    m_hat = m / (1.0 - ADAM_B1 ** ADAM_STEP)
    v_hat = v / (1.0 - ADAM_B2 ** ADAM_STEP)
    delta = -ADAM_LR * (m_hat / (jnp.sqrt(v_hat) + ADAM_EPS) + ADAM_WD * w)
    return delta, m, v


def _adam_rows(shape):
    rows, cols = shape
    tr = rows
    while tr * cols * 4 > (1 << 20) and tr % 16 == 0:
        tr //= 2
    return tr


def _adam_sharded(w, m, v, land, order, name):
    R, C = w.shape
    tr = _adam_rows((R, C))

    def body(w_ref, m_ref, v_ref, land_ref, order_ref, g_ref, d_ref, mo_ref, vo_ref):
        g = land_ref[0].astype(F32)
        for b in range(1, NDEV):
            g = g + land_ref[b].astype(F32)
        g_ref[...] = g
        d_ref[...], mo_ref[...], vo_ref[...] = _adamw_update(w_ref[...], g, m_ref[...], v_ref[...])

    blk = pl.BlockSpec((tr, C), lambda i: (i, 0))
    return pl.pallas_call(
        body, name=name, grid=(R // tr,),
        out_shape=[jax.ShapeDtypeStruct((R, C), F32)] * 4,
        in_specs=[blk, blk, blk, pl.BlockSpec((NDEV, tr, C), lambda i: (0, i, 0)), _HBM],
        out_specs=[blk] * 4,
        compiler_params=_params("arbitrary"),
    )(w, m, v, land, order)


def _adam_ada(w, m, v, sc_all, dmod_cols):
    R, C = w.shape
    tr = 256

    def body(w_ref, m_ref, v_ref, sc_ref, dm_ref, g_ref, d_ref, mo_ref, vo_ref):
        g = _mm_tn(sc_ref[...], dm_ref[...])
        g_ref[...] = g
        d_ref[...], mo_ref[...], vo_ref[...] = _adamw_update(w_ref[...], g, m_ref[...], v_ref[...])

    blk = pl.BlockSpec((tr, C), lambda i: (i, 0))
    return pl.pallas_call(
        body, name="adam_w_ada", grid=(R // tr,),
        out_shape=[jax.ShapeDtypeStruct((R, C), F32)] * 4,
        in_specs=[blk, blk, blk, pl.BlockSpec((8, tr), lambda i: (0, i)), pl.BlockSpec((8, C), lambda i: (0, 0))],
        out_specs=[blk] * 4,
        compiler_params=_params("arbitrary"),
    )(w, m, v, sc_all, dmod_cols)


def _adam_small(w, g, m, v, name):
    def body(w_ref, g_ref, m_ref, v_ref, d_ref, mo_ref, vo_ref):
        d_ref[...], mo_ref[...], vo_ref[...] = _adamw_update(w_ref[...], g_ref[...], m_ref[...], v_ref[...])

    return pl.pallas_call(body, name=name, out_shape=[jax.ShapeDtypeStruct(w.shape, F32)] * 3,
                          compiler_params=pltpu.CompilerParams(vmem_limit_bytes=VMEM_LIMIT))(w, g, m, v)


def _block_diag_in(b):
    bt = jnp.transpose(b, (0, 2, 1)).reshape(SSM_BLOCKS, 8, SSM_GROUP, SSM_STATE)
    eye = jnp.eye(8, dtype=bool)[None, :, None, :, None]
    return jnp.where(eye, bt[:, :, :, None, :], 0.0).reshape(SSM_BLOCKS, 128, SSM_BLOCK_STATE)


def _block_diag_out(c):
    ct = jnp.transpose(c, (0, 2, 1)).reshape(SSM_BLOCKS, 8, SSM_STATE, SSM_GROUP)
    eye = jnp.eye(8, dtype=bool)[None, :, None, :, None]
    return jnp.where(eye, ct[:, :, :, None, :], 0.0).reshape(SSM_BLOCKS, SSM_BLOCK_STATE, 128)


def _diag_blocks(dense, rows, cols):
    d5 = dense.reshape(SSM_BLOCKS, 8, rows, 8, cols)
    return jnp.stack([d5[:, a, :, a, :] for a in range(8)], axis=1).reshape(32, rows, cols)


def _pack_small(ada_vec, parts, params, tail=None):
    rest_rows, rows = _pack_rows(params, ada_vec is not None)
    rest = jnp.concatenate([parts[n].reshape(-1) for n, _ in params])
    rest = jnp.pad(rest, (0, NDEV * rest_rows * 128 - rest.shape[0])).reshape(NDEV, rest_rows, 128)
    head = [] if ada_vec is None else [ada_vec.reshape(NDEV, ADA_ROWS, 128)]
    pad = rows - rest_rows - (0 if ada_vec is None else ADA_ROWS)
    fill = jnp.zeros((NDEV, pad, 128), F32) if tail is None else jnp.pad(tail[None], ((0, NDEV - 1), (0, pad - 1), (0, 127)))
    return jnp.concatenate(head + [rest] + ([fill] if pad else []), axis=1)


def _unpack_small(pack, shapes, params, with_ada):
    rest_rows, _ = _pack_rows(params, with_ada)
    first = ADA_ROWS if with_ada else 0
    ada_vec = pack[:, :first].reshape(-1) if with_ada else None
    rest = pack[:, first:first + rest_rows].reshape(-1)
    out, off = {}, 0
    for n, size in params:
        out[n] = rest[off:off + size].reshape(shapes[n])
        off += size
    return ada_vec, out


WEIGHT_ORDER = ('w_ada', 'b_ada', 'g_ffn1', 'w_ffn1_in', 'w_ffn1_out', 'g_mix', 'w_in', 'pool_w', 'pool_b',
                'pool_scale', 'w_pool_up', 'ssm_lam_re_log', 'ssm_lam_im', 'ssm_log_dt', 'ssm_b_re', 'ssm_b_im',
                'ssm_c_re', 'ssm_c_im', 'ssm_d', 'w_glu', 'b_glu', 'w_ssm_up', 'w_out', 'g_ffn2', 'w_ffn2_in',
                'w_ffn2_out', 'g_final')
GATHERED = ('w_ffn1_in', 'w_ffn1_out', 'w_in', 'w_pool_up', 'w_glu', 'w_ssm_up', 'w_out', 'w_ffn2_in', 'w_ffn2_out')
TRANSPOSED = ('w_ffn1_in', 'w_ffn2_in')
STATE_MINOR = ('ssm_b_re', 'ssm_b_im')


def kernel(x, c, w_ada, b_ada, g_ffn1, w_ffn1_in, w_ffn1_out, g_mix, w_in, pool_w, pool_b, pool_scale, w_pool_up, ssm_lam_re_log, ssm_lam_im, ssm_log_dt, ssm_b_re, ssm_b_im, ssm_c_re, ssm_c_im, ssm_d, w_glu, b_glu, w_ssm_up, w_out, g_ffn2, w_ffn2_in, w_ffn2_out, g_final, loss_target, m_w_ada, m_b_ada, m_g_ffn1, m_w_ffn1_in, m_w_ffn1_out, m_g_mix, m_w_in, m_pool_w, m_pool_b, m_pool_scale, m_w_pool_up, m_ssm_lam_re_log, m_ssm_lam_im, m_ssm_log_dt, m_ssm_b_re, m_ssm_b_im, m_ssm_c_re, m_ssm_c_im, m_ssm_d, m_w_glu, m_b_glu, m_w_ssm_up, m_w_out, m_g_ffn2, m_w_ffn2_in, m_w_ffn2_out, m_g_final, v_w_ada, v_b_ada, v_g_ffn1, v_w_ffn1_in, v_w_ffn1_out, v_g_mix, v_w_in, v_pool_w, v_pool_b, v_pool_scale, v_w_pool_up, v_ssm_lam_re_log, v_ssm_lam_im, v_ssm_log_dt, v_ssm_b_re, v_ssm_b_im, v_ssm_c_re, v_ssm_c_im, v_ssm_d, v_w_glu, v_b_glu, v_w_ssm_up, v_w_out, v_g_ffn2, v_w_ffn2_in, v_w_ffn2_out, v_g_final):
    args = locals()
    W = {n: args[n] for n in WEIGHT_ORDER}
    M = {n: args["m_" + n] for n in WEIGHT_ORDER}
    V = {n: args["v_" + n] for n in WEIGHT_ORDER}
    shapes = {n: W[n].shape for n in WEIGHT_ORDER}
    xt, tgt = x[0], loss_target[0]

    def local(tree, n):
        return jnp.swapaxes(tree[n][0], 0, 1) if n in TRANSPOSED else tree[n][0]

    def as_output(n, a):
        return (jnp.swapaxes(a, 0, 1) if n in TRANSPOSED else a)[None]

    shard = dict(zip(GATHERED, _cast_shards([local(W, n) for n in GATHERED])))
    stacks = {}

    def gather(names):
        return _Gather([shard[n] for n in names])

    def gathered(names, results):
        stacks.update(zip(names, results))

    ffn1_w, ffn2_w = ('w_ffn1_in', 'w_ffn1_out'), ('w_ffn2_in', 'w_ffn2_out')
    mix_w = ('w_in', 'w_pool_up', 'w_glu', 'w_ssm_up', 'w_out')
    mod_cols, sc_all, *res = _ada_forward(c, W['w_ada'][0], b_ada.reshape(NDEV, -1), gather(ffn1_w[:1]))
    gathered(ffn1_w[:1], res)
    win1 = stacks['w_ffn1_in'].reshape(2, 4, FF_SHARD, D_MODEL)
    prm = jnp.concatenate([mod_cols.reshape(9, D_MODEL), g_ffn1, g_mix, g_ffn2, g_final[None], jnp.zeros((3, D_MODEL), F32)], axis=0)
    pad512 = jnp.zeros((1, D_MODEL - 512), F32)
    mvec = jnp.concatenate([jnp.concatenate([pool_b, pad512], axis=1), jnp.concatenate([pool_scale, pad512], axis=1),
                            jnp.concatenate([ssm_d, pad512], axis=1), b_glu, jnp.zeros((4, D_MODEL), F32)], axis=0)
    log_dt_col = ssm_log_dt[0][:, None]
    coeffs = _ssm_params_forward(ssm_lam_re_log[0], ssm_lam_im[0], log_dt_col)
    srow = jnp.stack([t.reshape(N_STATE) for t in coeffs], axis=0)
    b_dense = jnp.stack([_block_diag_in(ssm_b_re[0]), _block_diag_in(ssm_b_im[0])], axis=0)
    c_dense = jnp.stack([_block_diag_out(ssm_c_re[0]), _block_diag_out(ssm_c_im[0])], axis=0)
    bb, ct = _ssm_dense_forward(srow, b_dense, c_dense)
    pw = pool_w[0]

    next_w = ffn1_w[1:] + mix_w[:1]
    ab1, s1, *res = _ffn_hidden(xt, prm, win1, 0, ROW_G_FFN1, "ffn1_hidden", gather(next_w))
    gathered(next_w, res)
    wout1 = stacks['w_ffn1_out'].reshape(4, FF_SHARD, D_MODEL)
    x1, f1, *res = _ffn_out(xt, s1, prm, wout1, 0, "ffn1_out", gather(mix_w[1:]))
    gathered(mix_w[1:], res)
    w_out_full = stacks['w_out'].reshape(D_MODEL, D_MODEL)
    res = _mixer_forward(x1, prm, stacks['w_in'], stacks['w_pool_up'], stacks['w_glu'], stacks['w_ssm_up'],
                         w_out_full, pw, mvec, srow, bb, ct, gather(ffn2_w))
    x2, mo, saved = res[0], res[1], res[2:11]
    gathered(ffn2_w, res[11:])
    win2 = stacks['w_ffn2_in'].reshape(2, 4, FF_SHARD, D_MODEL)
    wout2 = stacks['w_ffn2_out'].reshape(4, FF_SHARD, D_MODEL)
    x3, f3, ab3 = _ffn_forward(x2, prm, win2, wout2, 2, ROW_G_FFN2, "ffn2_forward")
    d3, fin = _final_loss(x3, tgt, prm)

    lands = {}

    def scatter(grads):
        names = list(grads)
        return _Scatter([grads[n][0] for n in names], [grads[n][1] for n in names], [local(W, n).shape for n in names])

    def scattered(grads, results):
        lands.update(zip(grads, results))

    parts3, dwin2, dwout2 = _ffn_backward(x2, d3, ab3, prm, win2, wout2, prm, 2, ROW_G_FFN2, "ffn2_backward")
    d2, sums3 = _norm_backward(parts3, x2, d3, f3, prm, 2, ROW_G_FFN2, 0.5, "ffn2_norm_backward")
    g_ffn2_w = {'w_ffn2_in': (dwin2, _halves), 'w_ffn2_out': (dwout2.reshape(NDEV, -1, D_MODEL), _stacked)}
    res = _mixer_backward(d2, prm, saved, stacks['w_pool_up'], stacks['w_glu'], stacks['w_ssm_up'], w_out_full, pw, mvec,
                          srow, bb, ct, scatter(g_ffn2_w))
    dz, dwo, dwpu, dwglu, dwsu, dpw, dbb, dct, vsum, da = res[:10]
    scattered(g_ffn2_w, res[10:])
    parts2, dwin_mix = _mixer_in_backward(x1, dz, prm, stacks['w_in'])
    d1, sums2 = _norm_backward(parts2, x1, d2, mo, prm, 1, ROW_G_MIX, 1.0, "mixer_norm_backward")
    g_mix_w = {'w_in': (dwin_mix, _stacked), 'w_pool_up': (dwpu, _stacked), 'w_glu': (dwglu, _stacked),
               'w_ssm_up': (dwsu, _stacked), 'w_out': (dwo, _stacked)}
    db_dense, df_rows = _ssm_dense_backward(dbb, da, srow, b_dense)
    cot = [df_rows[r].reshape(32, 64) for r in (2, 3, 0, 1)]
    d_lrl, d_li, d_ldt = _ssm_params_backward(ssm_lam_re_log[0], ssm_lam_im[0], log_dt_col, cot)
    small_grads = {
        'g_mix': sums2[0], 'g_ffn2': sums3[0], 'g_final': fin[0], 'pool_w': dpw,
        'pool_b': vsum[1, :512], 'pool_scale': vsum[0, :512], 'ssm_lam_re_log': d_lrl, 'ssm_lam_im': d_li,
        'ssm_log_dt': d_ldt, 'ssm_b_re': _diag_blocks(db_dense[0], SSM_GROUP, SSM_STATE),
        'ssm_b_im': _diag_blocks(db_dense[1], SSM_GROUP, SSM_STATE),
        'ssm_c_re': jnp.transpose(_diag_blocks(dct[0], SSM_STATE, SSM_GROUP), (0, 2, 1)),
        'ssm_c_im': jnp.transpose(_diag_blocks(dct[1], SSM_STATE, SSM_GROUP), (0, 2, 1)),
        'ssm_d': vsum[2, :512], 'b_glu': vsum[3],
    }
    total_early, _ = _allreduce_small(_pack_small(None, small_grads, SMALL_EARLY, fin[1:2, 0:1]), prm, "allreduce_early")
    loss = total_early[0, _pack_rows(SMALL_EARLY, False)[0], 0]
    parts1, dwin1, dwout1, *res = _ffn_backward(xt, d1, ab1, prm, win1, wout1, total_early, 0, ROW_G_FFN1,
                                                "ffn1_backward", scatter(g_mix_w))
    scattered(g_mix_w, res)

    g_ffn1_w = {'w_ffn1_in': (dwin1, _halves), 'w_ffn1_out': (dwout1.reshape(NDEV, -1, D_MODEL), _stacked)}
    last_views = [g_ffn1_w[n][1] for n in ffn1_w]
    send_sems, recv_sems, last_src, last_land, token = _scatter_start(
        [g_ffn1_w[n][0] for n in ffn1_w], last_views, [local(W, n).shape for n in ffn1_w], [total_early])
    after_start = token[0:1, 0:1]
    d0, sums1 = _norm_backward(parts1, xt, d1, f1, prm + after_start, 0, ROW_G_FFN1, 0.5, "ffn1_norm_backward")

    grad, delta, new_m, new_v = {}, {}, {}, {}

    def adam_sharded(n):
        res = _adam_sharded(local(W, n), local(M, n), local(V, n), lands[n], token, "adam_" + n)
        grad[n], delta[n], new_m[n], new_v[n] = [as_output(n, r) for r in res]
        return res[3]

    def adam_small(params, ada, total, name, order):
        rows = _pack_rows(params, ada)[1]
        views = [{n: jnp.transpose(t[n][0], (0, 2, 1)) if n in STATE_MINOR else t[n] for n, _ in params} for t in (W, M, V)]
        packs = [(_pack_small(t['b_ada'].reshape(-1) if ada else None, v, params) + order).reshape(NDEV * rows, 128)
                 for t, v in zip((W, M, V), views)]
        res = _adam_small(packs[0], total.reshape(NDEV * rows, 128), packs[1], packs[2], name)
        view_shapes = {n: (32, SSM_GROUP, SSM_STATE) if n in STATE_MINOR else shapes[n] for n, _ in params}
        for dst, packed in zip((grad, delta, new_m, new_v), (total, *res)):
            ada_vec, rest = _unpack_small(packed.reshape(NDEV, rows, 128), view_shapes, params, ada)
            dst.update({n: jnp.transpose(a, (0, 2, 1))[None] if n in STATE_MINOR else a for n, a in rest.items()})
            if ada:
                dst['b_ada'] = ada_vec.reshape(shapes['b_ada'])
        return res[2]

    done = [d0] + [adam_sharded(n) for n in GATHERED if n not in ffn1_w]
    done.append(adam_small(SMALL_EARLY, False, total_early, "adam_small_early", after_start))
    firsts = [lax.slice(dst[n], (0,) * dst[n].ndim, (1,) * dst[n].ndim).reshape(1)
              for dst in (grad, delta, new_m, new_v) for n, _ in SMALL_EARLY]
    done.append(jnp.pad(jnp.concatenate(firsts), (0, 1024 - len(firsts))).reshape(8, 128))
    lands.update(zip(ffn1_w, _scatter_wait(send_sems, recv_sems, last_src, last_land, last_views, done)))

    dmod = jnp.concatenate([sums1[1:4], sums2[1:4], sums3[1:4]], axis=0).reshape(-1)
    total_late, landed = _allreduce_small(_pack_small(dmod, {'g_ffn1': sums1[0]}, SMALL_LATE), lands[ffn1_w[1]],
                                          "allreduce_late")
    dmod_cols = landed[:, :ADA_ROWS].reshape(NDEV, ADA_ROWS * 128)
    res = _adam_ada(W['w_ada'][0], M['w_ada'][0], V['w_ada'][0], sc_all, dmod_cols)
    grad['w_ada'], delta['w_ada'], new_m['w_ada'], new_v['w_ada'] = [r[None] for r in res]
    adam_small(SMALL_LATE, True, total_late, "adam_small_late", 0.0)
    for n in ffn1_w:
        adam_sharded(n)

    return (loss, d0[None], *[grad[n] for n in WEIGHT_ORDER], *[delta[n] for n in WEIGHT_ORDER],
            *[new_m[n] for n in WEIGHT_ORDER], *[new_v[n] for n in WEIGHT_ORDER])
```

```python
import functools

import jax
import jax.numpy as jnp
from jax import lax
from jax.experimental import pallas as pl
from jax.experimental.pallas import tpu as pltpu

F32 = jnp.float32
MXU_DTYPE = jnp.bfloat16
WIRE_DTYPE = jnp.bfloat16
SAVE_DTYPE = jnp.bfloat16

NDEV = 8
D_MODEL = 1024
D_FF = 2816
FF_SHARD = 2 * D_FF // NDEV
POOL_WIDTH = 512
POOL_GROUP = 128
SSM_WIDTH = 512
SSM_STATE = 64
SSM_GROUP = 16
SSM_BLOCKS = 4
SSM_BLOCK_STATE = 512
N_STATE = 2048
IN_WIDTH = 3072
EPS = 1e-6
ADAM_LR = 0.001
ADAM_B1 = 0.9
ADAM_B2 = 0.999
ADAM_EPS = 1e-08
ADAM_WD = 0.01
ADAM_STEP = 10

TM_FFN = 512
FFN_BWD_CHUNK = 256
TM_MIX = 256
TM_MIX_BWD = 256
TM_EW = 512
SCAN_ROWS = 8
POOL_HALO = 16
VMEM_LIMIT = 60 * 1024 * 1024

ROW_G_FFN1, ROW_G_MIX, ROW_G_FFN2, ROW_G_FINAL = 9, 10, 11, 12
ROW_POOL_B, ROW_POOL_SCALE, ROW_SSM_D, ROW_B_GLU = 0, 1, 2, 3

SMALL_EARLY = (
    ("g_mix", 1024), ("g_ffn2", 1024), ("g_final", 1024), ("pool_w", 65536),
    ("pool_b", 512), ("pool_scale", 512), ("ssm_lam_re_log", 2048), ("ssm_lam_im", 2048),
    ("ssm_log_dt", 32), ("ssm_b_re", 32768), ("ssm_b_im", 32768), ("ssm_c_re", 32768),
    ("ssm_c_im", 32768), ("ssm_d", 512), ("b_glu", 1024),
)
SMALL_LATE = (("g_ffn1", 1024),)
ADA_ROWS = 9
MESH = pl.DeviceIdType.MESH


def _pack_rows(params, with_ada):
    rest = -(-sum(n for _, n in params) // (NDEV * 128))
    return rest, -(-(rest + (ADA_ROWS if with_ada else 0)) // 8) * 8


def _mm(a, b):
    return jnp.dot(a.astype(MXU_DTYPE), b.astype(MXU_DTYPE), preferred_element_type=F32)


def _mm_nt(a, b):
    return lax.dot_general(a.astype(MXU_DTYPE), b.astype(MXU_DTYPE), (((1,), (1,)), ((), ())),
                           preferred_element_type=F32)


def _mm_tn(a, b):
    return lax.dot_general(a.astype(MXU_DTYPE), b.astype(MXU_DTYPE), (((0,), (0,)), ((), ())),
                           preferred_element_type=F32)


def _rms_scale(x):
    return lax.rsqrt(jnp.mean(x * x, axis=-1, keepdims=True) + EPS)


def _sigmoid(x):
    return jax.nn.sigmoid(x)


def _colsum(x):
    return jnp.sum(x, axis=0, keepdims=True)


def _row(ref, r):
    return ref[r:r + 1, :]


def _params(*sem):
    return pltpu.CompilerParams(dimension_semantics=sem, vmem_limit_bytes=VMEM_LIMIT)


def _resident(a):
    return pl.BlockSpec(a.shape, lambda *_: (0,) * a.ndim, pipeline_mode=pl.Buffered(1))


def _me():
    return lax.axis_index("x"), lax.axis_index("y"), lax.axis_index("c")


def _peer(rel):
    x, y, c = _me()
    px = 1 - x if rel & 4 else x
    py = 1 - y if rel & 2 else y
    pc = 1 - c if rel & 1 else c
    return (px, py, pc), 4 * px + 2 * py + pc


_HBM = pl.BlockSpec(memory_space=pl.ANY)
_HBM_ONLY = pl.BlockSpec(memory_space=pltpu.HBM)


def _stacked(ref, p):
    return ref.at[p]


def _halves(ref, p):
    return ref.at[p // 4, p % 4]


class _Gather:
    def __init__(self, shards):
        self.operands = list(shards)
        self.n = len(shards)
        self.out_shape = [jax.ShapeDtypeStruct((NDEV,) + s.shape, s.dtype) for s in shards]
        self.scratch = [pltpu.SemaphoreType.DMA((7 * self.n,)), pltpu.SemaphoreType.DMA((7 * self.n,)),
                        pltpu.SemaphoreType.DMA((self.n,))]

    def plan(self, srcs, outs, sems):
        send_sems, recv_sems, local_sems = sems
        n = self.n
        x, y, c = _me()
        me = 4 * x + 2 * y + c
        here, sibling = (x, y, c), (x, y, 1 - c)
        chips = [(1 - x, y), (x, 1 - y), (1 - x, 1 - y)]

        def blk(px, py, pc):
            return 4 * px + 2 * py + pc

        def copy(a, k, block, to, src=None):
            return pltpu.make_async_remote_copy(
                src_ref=outs[a].at[block] if src is None else src, dst_ref=outs[a].at[block],
                send_sem=send_sems.at[7 * a + k], recv_sem=recv_sems.at[7 * a + k], device_id=to, device_id_type=MESH)

        def mine(a):
            return pltpu.make_async_copy(srcs[a], outs[a].at[me], local_sems.at[a])

        def first(a):
            return [copy(a, 0, me, sibling, src=srcs[a])] + [copy(a, 1 + j, me, (*chip, c), src=srcs[a])
                                                              for j, chip in enumerate(chips)]

        def start():
            for a in range(n):
                mine(a).start()
                for cp in first(a):
                    cp.start()

        def forward():
            for a in range(n):
                for j, chip in enumerate(chips):
                    copy(a, 1 + j, blk(*chip, c), here).wait_recv()
                    copy(a, 4 + j, blk(*chip, c), sibling).start()

        def finish():
            for a in range(n):
                copy(a, 0, blk(x, y, 1 - c), here).wait_recv()
                for j, chip in enumerate(chips):
                    copy(a, 4 + j, blk(*chip, 1 - c), here).wait_recv()
            for a in range(n):
                mine(a).wait()
                for cp in first(a):
                    cp.wait_send()
                for j, chip in enumerate(chips):
                    copy(a, 4 + j, blk(*chip, c), sibling).wait_send()

        return start, forward, finish


class _Scatter:
    def __init__(self, arrays, views, shard_shapes):
        self.operands = list(arrays)
        self.views = list(views)
        self.n = len(arrays)
        self.out_shape = [jax.ShapeDtypeStruct((NDEV,) + tuple(s), a.dtype) for s, a in zip(shard_shapes, arrays)]
        self.scratch = [pltpu.SemaphoreType.DMA((7 * self.n,)), pltpu.SemaphoreType.DMA((7 * self.n,)),
                        pltpu.SemaphoreType.DMA((self.n,))]

    def plan(self, srcs, outs, sems):
        send_sems, recv_sems, local_sems = sems
        n, views = self.n, self.views
        x, y, c = _me()
        me = 4 * x + 2 * y + c

        def mine(a):
            return pltpu.make_async_copy(views[a](srcs[a], me), outs[a].at[me], local_sems.at[a])

        def copy(a, rel, sending):
            to, p = _peer(rel)
            return pltpu.make_async_remote_copy(
                src_ref=views[a](srcs[a], p), dst_ref=outs[a].at[me if sending else p],
                send_sem=send_sems.at[7 * a + rel - 1], recv_sem=recv_sems.at[7 * a + rel - 1],
                device_id=to if sending else (x, y, c), device_id_type=MESH)

        def start():
            for a in range(n):
                mine(a).start()
            for rel in range(1, 8):
                for a in range(n):
                    copy(a, rel, True).start()

        def forward():
            pass

        def finish():
            for rel in range(1, 8):
                for a in range(n):
                    copy(a, rel, False).wait_recv()
            for rel in range(1, 8):
                for a in range(n):
                    copy(a, rel, True).wait_send()
            for a in range(n):
                mine(a).wait()

        return start, forward, finish


def _launch(body, name, out_shape, in_specs, out_specs, operands, scratch=(), grid=None, semantics=None,
            carry=None, steps=None):
    out_shape, in_specs, out_specs = list(out_shape), list(in_specs), list(out_specs)
    operands, scratch = list(operands), list(scratch)
    n_in, n_out, n_scr = len(in_specs), len(out_shape), len(scratch)
    kernel_body = body
    if carry is not None:
        k = carry.n

        def kernel_body(*refs):
            ins, cin = refs[:n_in], refs[n_in:n_in + k]
            outs, cout = refs[n_in + k:n_in + k + n_out], refs[n_in + k + n_out:n_in + 2 * k + n_out]
            rest = refs[n_in + 2 * k + n_out:]
            scr, csem = rest[:n_scr], rest[n_scr:]
            start, forward, finish = carry.plan(cin, cout, csem)
            if steps is None:
                start()
                body(*ins, *outs, *scr)
                forward()
                finish()
            else:
                pl.when(steps()[0])(start)
                pl.when(steps()[1])(forward)
                body(*ins, *outs, *scr)
                pl.when(steps()[2])(finish)

        in_specs += [_HBM] * k
        out_shape += carry.out_shape
        out_specs += [_HBM] * k
        operands += carry.operands
        scratch += carry.scratch
    kwargs = {} if grid is None else {"grid": grid}
    params = pltpu.CompilerParams(vmem_limit_bytes=VMEM_LIMIT) if semantics is None else _params(*semantics)
    return pl.pallas_call(kernel_body, name=name, out_shape=out_shape, in_specs=in_specs, out_specs=out_specs,
                          scratch_shapes=scratch, compiler_params=params, **kwargs)(*operands)


def _grid_steps(nt):
    def steps():
        i = pl.program_id(0)
        return i == 0, i == nt - 1, i == nt - 1
    return steps


def _cast_shards(shards):
    n = len(shards)

    def body(*refs):
        for a in range(n):
            refs[n + a][...] = refs[a][...].astype(WIRE_DTYPE)

    return pl.pallas_call(body, name="cast_shards",
                          out_shape=[jax.ShapeDtypeStruct(s.shape, WIRE_DTYPE) for s in shards],
                          compiler_params=pltpu.CompilerParams(vmem_limit_bytes=VMEM_LIMIT))(*shards)


_SEM = pl.BlockSpec(memory_space=pltpu.SEMAPHORE)
_DATAFLOW = pltpu.SideEffectType.DATAFLOW_SIDE_EFFECTING


def _split_copy(arrays, views, landing, send_sems, recv_sems, a, rel):
    to, p = _peer(rel)
    x, y, c = _me()
    return pltpu.make_async_remote_copy(
        src_ref=views[a](arrays[a], p), dst_ref=landing[a].at[4 * x + 2 * y + c],
        send_sem=send_sems.at[NDEV * a + rel], recv_sem=recv_sems.at[NDEV * a + rel], device_id=to, device_id_type=MESH)


def _scatter_start(arrays, views, shard_shapes, after):
    n = len(arrays)
    landing = [pltpu.with_memory_space_constraint(lax.empty((NDEV,) + tuple(s), a.dtype), pltpu.HBM)
               for s, a in zip(shard_shapes, arrays)]
    arrays = [pltpu.with_memory_space_constraint(a, pltpu.HBM) for a in arrays]

    def body(*refs):
        ins, land = refs[:n], refs[n:2 * n]
        send_sems, recv_sems = refs[2 * n + len(after)], refs[2 * n + len(after) + 1]
        token = refs[-1]
        for rel in range(NDEV):
            for a in range(n):
                _split_copy(ins, views, land, send_sems, recv_sems, a, rel).start()
        token[...] = jnp.zeros_like(token)

    res = pl.pallas_call(
        body, name="scatter_start",
        out_shape=[pltpu.SemaphoreType.DMA((NDEV * n,)), pltpu.SemaphoreType.DMA((NDEV * n,))]
        + [pltpu.HBM(a.shape, a.dtype) for a in arrays] + [pltpu.HBM(l.shape, l.dtype) for l in landing]
        + [jax.ShapeDtypeStruct((8, 128), F32)],
        in_specs=[_HBM_ONLY] * (2 * n) + [_HBM] * len(after),
        out_specs=[_SEM, _SEM] + [_HBM_ONLY] * (2 * n) + [pl.BlockSpec(memory_space=pltpu.VMEM)],
        input_output_aliases={i: 2 + i for i in range(2 * n)},
        compiler_params=pltpu.CompilerParams(has_side_effects=_DATAFLOW),
    )(*arrays, *landing, *after)
    return res[0], res[1], res[2:2 + n], res[2 + n:2 + 2 * n], res[-1]


def _scatter_wait(send_sems, recv_sems, arrays, landing, views, after):
    n = len(arrays)

    def body(*refs):
        ins, land = refs[:n], refs[n:2 * n]
        send, recv = refs[2 * n], refs[2 * n + 1]
        for rel in range(NDEV):
            for a in range(n):
                cp = _split_copy(ins, views, land, send, recv, a, rel)
                cp.wait_send()
                cp.wait_recv()

    res = pl.pallas_call(
        body, name="scatter_wait",
        out_shape=[pltpu.HBM(a.shape, a.dtype) for a in arrays] + [pltpu.HBM(l.shape, l.dtype) for l in landing],
        in_specs=[_HBM_ONLY] * (2 * n) + [_SEM, _SEM] + [_HBM] * len(after),
        out_specs=[_HBM_ONLY] * (2 * n),
        input_output_aliases={i: i for i in range(2 * n)},
        compiler_params=pltpu.CompilerParams(has_side_effects=_DATAFLOW),
    )(*arrays, *landing, send_sems, recv_sems, *after)
    return res[n:]


def _ada_forward(c_row, w_ada, b_ada8, carry):
    cols = w_ada.shape[1]

    def body(c_ref, w_ref, b_ref, mod_ref, sc_ref, c_all, send_buf, recv_buf, send1, recv1, send2, recv2):
        x, y, c = _me()
        me = 4 * x + 2 * y + c
        rowi = lax.broadcasted_iota(jnp.int32, (8, D_MODEL), 0)
        c_all[me] = jnp.broadcast_to(c_ref[...], (8, D_MODEL))
        copies = []
        for rel in range(1, 8):
            to, _ = _peer(rel)
            cp = pltpu.make_async_remote_copy(src_ref=c_all.at[me], dst_ref=c_all.at[me], send_sem=send1.at[rel - 1],
                                              recv_sem=recv1.at[rel - 1], device_id=to, device_id_type=MESH)
            cp.start()
            copies.append(cp)
        for rel in range(1, 8):
            _, p = _peer(rel)
            pltpu.make_async_remote_copy(src_ref=c_all.at[p], dst_ref=c_all.at[p], send_sem=send1.at[rel - 1],
                                         recv_sem=recv1.at[rel - 1], device_id=(x, y, c), device_id_type=MESH).wait_recv()
        for cp in copies:
            cp.wait_send()
        cmat = jnp.zeros((8, D_MODEL), F32)
        for b in range(8):
            cmat = jnp.where(rowi == b, c_all[b], cmat)
        sc = cmat * _sigmoid(cmat)
        sc_ref[...] = sc
        modcols = _mm(sc, w_ref[...]) + b_ref[pl.ds(me, 1), :]
        for b in range(8):
            send_buf[b] = jnp.broadcast_to(modcols[b:b + 1, :], (8, cols))
        recv_buf[me] = send_buf[me]
        copies = []
        for rel in range(1, 8):
            to, p = _peer(rel)
            cp = pltpu.make_async_remote_copy(src_ref=send_buf.at[p], dst_ref=recv_buf.at[me], send_sem=send2.at[rel - 1],
                                              recv_sem=recv2.at[rel - 1], device_id=to, device_id_type=MESH)
            cp.start()
            copies.append(cp)
        for rel in range(1, 8):
            _, p = _peer(rel)
            pltpu.make_async_remote_copy(src_ref=send_buf.at[p], dst_ref=recv_buf.at[p], send_sem=send2.at[rel - 1],
                                         recv_sem=recv2.at[rel - 1], device_id=(x, y, c), device_id_type=MESH).wait_recv()
        for cp in copies:
            cp.wait_send()
        rowc = lax.broadcasted_iota(jnp.int32, (8, cols), 0)
        out = jnp.zeros((8, cols), F32)
        for k in range(8):
            out = jnp.where(rowc == k, recv_buf[k], out)
        mod_ref[...] = out

    return _launch(
        body, "ada_forward",
        out_shape=[jax.ShapeDtypeStruct((8, cols), F32), jax.ShapeDtypeStruct((8, D_MODEL), F32)],
        in_specs=[pl.BlockSpec(memory_space=pltpu.VMEM)] * 3,
        out_specs=[pl.BlockSpec(memory_space=pltpu.VMEM)] * 2,
        operands=(c_row, w_ada, b_ada8),
        scratch=[pltpu.VMEM((8, 8, D_MODEL), F32), pltpu.VMEM((8, 8, cols), F32), pltpu.VMEM((8, 8, cols), F32)]
        + [pltpu.SemaphoreType.DMA((7,))] * 4,
        carry=carry)


def _allreduce_small(pack, order, name):
    rows = pack.shape[1]

    def body(pack_ref, order_ref, total_ref, land_ref, send1, recv1, send2, recv2):
        x, y, c = _me()
        me = 4 * x + 2 * y + c
        land_ref[me] = pack_ref[me]
        copies = []
        for rel in range(1, 8):
            to, p = _peer(rel)
            cp = pltpu.make_async_remote_copy(src_ref=pack_ref.at[p], dst_ref=land_ref.at[me], send_sem=send1.at[rel - 1],
                                              recv_sem=recv1.at[rel - 1], device_id=to, device_id_type=MESH)
            cp.start()
            copies.append(cp)
        for rel in range(1, 8):
            _, p = _peer(rel)
            pltpu.make_async_remote_copy(src_ref=pack_ref.at[p], dst_ref=land_ref.at[p], send_sem=send1.at[rel - 1],
                                         recv_sem=recv1.at[rel - 1], device_id=(x, y, c), device_id_type=MESH).wait_recv()
        for cp in copies:
            cp.wait_send()
        acc = land_ref[0]
        for b in range(1, 8):
            acc = acc + land_ref[b]
        total_ref[me] = acc
        copies = []
        for rel in range(1, 8):
            to, _ = _peer(rel)
            cp = pltpu.make_async_remote_copy(src_ref=total_ref.at[me], dst_ref=total_ref.at[me], send_sem=send2.at[rel - 1],
                                              recv_sem=recv2.at[rel - 1], device_id=to, device_id_type=MESH)
            cp.start()
            copies.append(cp)
        for rel in range(1, 8):
            _, p = _peer(rel)
            pltpu.make_async_remote_copy(src_ref=total_ref.at[p], dst_ref=total_ref.at[p], send_sem=send2.at[rel - 1],
                                         recv_sem=recv2.at[rel - 1], device_id=(x, y, c), device_id_type=MESH).wait_recv()
        for cp in copies:
            cp.wait_send()

    return pl.pallas_call(
        body, name=name,
        out_shape=[jax.ShapeDtypeStruct((8, rows, 128), F32), jax.ShapeDtypeStruct((8, rows, 128), F32)],
        in_specs=[pl.BlockSpec(memory_space=pltpu.VMEM), _HBM],
        out_specs=[pl.BlockSpec(memory_space=pltpu.VMEM)] * 2,
        scratch_shapes=[pltpu.SemaphoreType.DMA((7,))] * 4,
        compiler_params=pltpu.CompilerParams(vmem_limit_bytes=VMEM_LIMIT),
    )(pack, order)


def _modulated(x, prm_ref, sub, g_row):
    shift, scale = _row(prm_ref, 3 * sub), _row(prm_ref, 3 * sub + 1)
    g = _row(prm_ref, g_row)
    r = _rms_scale(x)
    n0 = x * r
    return (n0 * g) * (1.0 + scale) + shift, r, n0


def _swiglu_tile(xv, prm_ref, win_ref, wout_ref, ab_ref, sub, g_row):
    h, _, _ = _modulated(xv, prm_ref, sub, g_row)
    hb = h.astype(MXU_DTYPE)
    acc = None
    for j in range(4):
        a = _mm_nt(hb, win_ref[0, j])
        b = _mm_nt(hb, win_ref[1, j])
        ab_ref[0, j] = a.astype(SAVE_DTYPE)
        ab_ref[1, j] = b.astype(SAVE_DTYPE)
        t = _mm((a * _sigmoid(a)) * b, wout_ref[j])
        acc = t if acc is None else acc + t
    return acc


def _loss_tile(xv, target, g):
    r = _rms_scale(xv)
    n0 = xv * r
    err = n0 * g - target
    dy = err / float(D_MODEL)
    dn0 = dy * g
    dx = r * (dn0 - n0 * jnp.mean(dn0 * n0, axis=-1, keepdims=True))
    loss = 0.5 * jnp.sum(jnp.mean(err * err, axis=-1, keepdims=True), axis=0, keepdims=True)
    return dx, _colsum(dy * n0), loss


def _ffn_forward_loss(x, target, prm, win, wout, sub, g_row, name):
    T = x.shape[0]
    tm = min(T, TM_FFN)

    def body(x_ref, t_ref, prm_ref, win_ref, wout_ref, dx_ref, sums_ref, f_ref, ab_ref):
        i = pl.program_id(0)
        xv = x_ref[...]
        acc = _swiglu_tile(xv, prm_ref, win_ref, wout_ref, ab_ref, sub, g_row)
        f_ref[...] = acc.astype(SAVE_DTYPE)
        dx, dg, loss = _loss_tile(xv + (0.5 * _row(prm_ref, 3 * sub + 2)) * acc, t_ref[...], _row(prm_ref, ROW_G_FINAL))
        dx_ref[...] = dx
        upd = jnp.concatenate([dg, jnp.broadcast_to(loss, (1, D_MODEL)), jnp.zeros((6, D_MODEL), F32)], axis=0)

        @pl.when(i == 0)
        def _():
            sums_ref[...] = upd

        @pl.when(i > 0)
        def _():
            sums_ref[...] += upd

    tok = pl.BlockSpec((tm, D_MODEL), lambda i: (i, 0))
    return _launch(
        body, name, grid=(T // tm,), semantics=("arbitrary",),
        out_shape=[jax.ShapeDtypeStruct((T, D_MODEL), F32), jax.ShapeDtypeStruct((8, D_MODEL), F32),
                   jax.ShapeDtypeStruct((T, D_MODEL), SAVE_DTYPE), jax.ShapeDtypeStruct((2, 4, T, FF_SHARD), SAVE_DTYPE)],
        in_specs=[tok, tok, _resident(prm), _resident(win), _resident(wout)],
        out_specs=[tok, pl.BlockSpec((8, D_MODEL), lambda i: (0, 0)), tok,
                   pl.BlockSpec((2, 4, tm, FF_SHARD), lambda i: (0, 0, i, 0))],
        operands=(x, target, prm, win, wout))


def _ffn_hidden(x, prm, win, sub, g_row, name, carry=None):
    T = x.shape[0]
    tm = min(T, TM_FFN)

    def body(x_ref, prm_ref, win_ref, ab_ref, s_ref):
        h, _, _ = _modulated(x_ref[...], prm_ref, sub, g_row)
        hb = h.astype(MXU_DTYPE)
        for j in range(4):
            a = _mm_nt(hb, win_ref[0, j])
            b = _mm_nt(hb, win_ref[1, j])
            ab_ref[0, j] = a.astype(SAVE_DTYPE)
            ab_ref[1, j] = b.astype(SAVE_DTYPE)
            s_ref[j] = ((a * _sigmoid(a)) * b).astype(MXU_DTYPE)

    return _launch(
        body, name, grid=(T // tm,), semantics=("arbitrary",),
        out_shape=[jax.ShapeDtypeStruct((2, 4, T, FF_SHARD), SAVE_DTYPE), jax.ShapeDtypeStruct((4, T, FF_SHARD), MXU_DTYPE)],
        in_specs=[pl.BlockSpec((tm, D_MODEL), lambda i: (i, 0)), _resident(prm), _resident(win)],
        out_specs=[pl.BlockSpec((2, 4, tm, FF_SHARD), lambda i: (0, 0, i, 0)),
                   pl.BlockSpec((4, tm, FF_SHARD), lambda i: (0, i, 0))],
        operands=(x, prm, win), carry=carry, steps=_grid_steps(T // tm))


def _ffn_out(x, s, prm, wout, sub, name, carry=None):
    T = x.shape[0]
    tm = min(T, TM_FFN)

    def body(x_ref, s_ref, prm_ref, wout_ref, xo_ref, f_ref):
        acc = None
        for j in range(4):
            t = _mm(s_ref[j], wout_ref[j])
            acc = t if acc is None else acc + t
        f_ref[...] = acc.astype(SAVE_DTYPE)
        xo_ref[...] = x_ref[...] + (0.5 * _row(prm_ref, 3 * sub + 2)) * acc

    tok = pl.BlockSpec((tm, D_MODEL), lambda i: (i, 0))
    return _launch(
        body, name, grid=(T // tm,), semantics=("arbitrary",),
        out_shape=[jax.ShapeDtypeStruct((T, D_MODEL), F32), jax.ShapeDtypeStruct((T, D_MODEL), SAVE_DTYPE)],
        in_specs=[tok, pl.BlockSpec((4, tm, FF_SHARD), lambda i: (0, i, 0)), _resident(prm), _resident(wout)],
        out_specs=[tok, tok], operands=(x, s, prm, wout), carry=carry, steps=_grid_steps(T // tm))


def _ffn_backward(x, d, ab, prm, win, wout, order, sub, g_row, name, carry=None, defer_dwin=False):
    T = x.shape[0]
    tm = min(T, TM_FFN)
    nt = T // tm
    chunk = min(tm, FFN_BWD_CHUNK)

    def body(x_ref, d_ref, ab_ref, prm_ref, win_ref, wout_ref, order_ref, dh_ref, second_ref, dwout_ref, *scratch):
        i = pl.program_id(1)
        acc_out = scratch[-1]
        acc_in = None if defer_dwin else scratch[0]

        @pl.when(i == 0)
        def _():
            if not defer_dwin:
                acc_in[...] = jnp.zeros_like(acc_in)
            acc_out[...] = jnp.zeros_like(acc_out)

        wa, wb, wo = win_ref[0, 0], win_ref[1, 0], wout_ref[0]
        half_gate = 0.5 * _row(prm_ref, 3 * sub + 2)
        das, dbs, ss, hbs, dfss = [], [], [], [], []
        for ck in range(tm // chunk):
            rows = slice(ck * chunk, (ck + 1) * chunk)
            if not defer_dwin:
                hbs.append(_modulated(x_ref[rows, :], prm_ref, sub, g_row)[0].astype(MXU_DTYPE))
            a = ab_ref[0, 0, rows, :].astype(F32)
            b = ab_ref[1, 0, rows, :].astype(F32)
            sg = _sigmoid(a)
            si = a * sg
            dfs = (half_gate * d_ref[rows, :]).astype(MXU_DTYPE)
            ds = _mm_nt(dfs, wo)
            da = (ds * b * (sg * (1.0 + a * (1.0 - sg)))).astype(MXU_DTYPE)
            db = (ds * si).astype(MXU_DTYPE)
            dh_ref[0, rows, :] = (_mm(da, wa) + _mm(db, wb)).astype(SAVE_DTYPE)
            if defer_dwin:
                second_ref[0, 0, rows, :] = da
                second_ref[1, 0, rows, :] = db
            das.append(da)
            dbs.append(db)
            ss.append((si * b).astype(MXU_DTYPE))
            dfss.append(dfs)
        cat = (lambda v: v[0]) if len(das) == 1 else (lambda v: jnp.concatenate(v, axis=0))
        acc_out[...] += _mm_tn(cat(ss), cat(dfss))
        if not defer_dwin:
            hb = cat(hbs)
            acc_in[0] += _mm_tn(cat(das), hb)
            acc_in[1] += _mm_tn(cat(dbs), hb)

        @pl.when(i == nt - 1)
        def _():
            if not defer_dwin:
                second_ref[0, 0] = acc_in[0].astype(WIRE_DTYPE)
                second_ref[1, 0] = acc_in[1].astype(WIRE_DTYPE)
            dwout_ref[0] = acc_out[...].astype(WIRE_DTYPE)

    def steps():
        j, i = pl.program_id(0), pl.program_id(1)
        return (j == 0) & (i == 0), (j == 2) & (i == 0), (j == 3) & (i == nt - 1)

    tok = pl.BlockSpec((tm, D_MODEL), lambda j, i: (i, 0))
    pre = pl.BlockSpec((2, 1, tm, FF_SHARD), lambda j, i: (0, j, i, 0))
    wblock = pl.BlockSpec((2, 1, FF_SHARD, D_MODEL), lambda j, i: (0, j, 0, 0))
    second = jax.ShapeDtypeStruct(ab.shape, MXU_DTYPE) if defer_dwin else jax.ShapeDtypeStruct(win.shape, WIRE_DTYPE)
    return _launch(
        body, name, grid=(4, nt), semantics=("arbitrary", "arbitrary"),
        out_shape=[jax.ShapeDtypeStruct((4, T, D_MODEL), SAVE_DTYPE), second, jax.ShapeDtypeStruct(wout.shape, WIRE_DTYPE)],
        in_specs=[tok, tok, pre, _resident(prm), wblock, pl.BlockSpec((1, FF_SHARD, D_MODEL), lambda j, i: (j, 0, 0)), _HBM],
        out_specs=[pl.BlockSpec((1, tm, D_MODEL), lambda j, i: (j, i, 0)), pre if defer_dwin else wblock,
                   pl.BlockSpec((1, FF_SHARD, D_MODEL), lambda j, i: (j, 0, 0))],
        operands=(x, d, ab, prm, win, wout, order),
        scratch=([] if defer_dwin else [pltpu.VMEM((2, FF_SHARD, D_MODEL), F32)]) + [pltpu.VMEM((FF_SHARD, D_MODEL), F32)],
        carry=carry, steps=steps)


def _ffn_dwin(x, dab, prm, sub, g_row, name, carry=None):
    T = x.shape[0]
    tm = min(T, TM_FFN)
    nt = T // tm

    def body(x_ref, dab_ref, prm_ref, dwin_ref, acc):
        i = pl.program_id(1)

        @pl.when(i == 0)
        def _():
            acc[...] = jnp.zeros_like(acc)

        h, _, _ = _modulated(x_ref[...], prm_ref, sub, g_row)
        hb = h.astype(MXU_DTYPE)
        acc[0] += _mm_tn(dab_ref[0, 0], hb)
        acc[1] += _mm_tn(dab_ref[1, 0], hb)

        @pl.when(i == nt - 1)
        def _():
            dwin_ref[0, 0] = acc[0].astype(WIRE_DTYPE)
            dwin_ref[1, 0] = acc[1].astype(WIRE_DTYPE)

    def steps():
        j, i = pl.program_id(0), pl.program_id(1)
        return (j == 0) & (i == 0), (j == 2) & (i == 0), (j == 3) & (i == nt - 1)

    return _launch(
        body, name, grid=(4, nt), semantics=("arbitrary", "arbitrary"),
        out_shape=[jax.ShapeDtypeStruct((2, 4, FF_SHARD, D_MODEL), WIRE_DTYPE)],
        in_specs=[pl.BlockSpec((tm, D_MODEL), lambda j, i: (i, 0)),
                  pl.BlockSpec((2, 1, tm, FF_SHARD), lambda j, i: (0, j, i, 0)), _resident(prm)],
        out_specs=[pl.BlockSpec((2, 1, FF_SHARD, D_MODEL), lambda j, i: (0, j, 0, 0))],
        operands=(x, dab, prm), scratch=[pltpu.VMEM((2, FF_SHARD, D_MODEL), F32)], carry=carry, steps=steps)


def _norm_backward_tile(dh, xv, dv, fv, prm_ref, sub, g_row, gate_coef):
    scale, g = _row(prm_ref, 3 * sub + 1), _row(prm_ref, g_row)
    r = _rms_scale(xv)
    n0 = xv * r
    dn = dh * (1.0 + scale)
    dn0 = dn * g
    dx = dv + r * (dn0 - n0 * jnp.mean(dn0 * n0, axis=-1, keepdims=True))
    upd = jnp.concatenate([_colsum(dn * n0), _colsum(dh), _colsum(dh * (n0 * g)),
                           gate_coef * _colsum(dv * fv.astype(F32)), jnp.zeros((4, D_MODEL), F32)], axis=0)
    return dx, upd


def _norm_backward(parts, x, d, f, prm, sub, g_row, gate_coef, name, carry=None):
    T = x.shape[0]
    tm = min(T, TM_EW)
    P = parts.shape[0]

    def body(p_ref, x_ref, d_ref, f_ref, prm_ref, dx_ref, sums_ref):
        i = pl.program_id(0)
        dh = p_ref[0].astype(F32)
        for k in range(1, P):
            dh = dh + p_ref[k].astype(F32)
        dx_ref[...], upd = _norm_backward_tile(dh, x_ref[...], d_ref[...], f_ref[...], prm_ref, sub, g_row, gate_coef)

        @pl.when(i == 0)
        def _():
            sums_ref[...] = upd

        @pl.when(i > 0)
        def _():
            sums_ref[...] += upd

    tok = pl.BlockSpec((tm, D_MODEL), lambda i: (i, 0))
    return _launch(
        body, name, grid=(T // tm,), semantics=("arbitrary",),
        out_shape=[jax.ShapeDtypeStruct((T, D_MODEL), F32), jax.ShapeDtypeStruct((8, D_MODEL), F32)],
        in_specs=[pl.BlockSpec((P, tm, D_MODEL), lambda i: (0, i, 0)), tok, tok, tok, _resident(prm)],
        out_specs=[tok, pl.BlockSpec((8, D_MODEL), lambda i: (0, 0))],
        operands=(parts, x, d, f, prm), carry=carry, steps=_grid_steps(T // tm))


def _ssm_discretise(lam_re_log, lam_im, log_dt):
    lr = -jnp.exp(lam_re_log)
    dt = jnp.exp(log_dt)
    mag = jnp.exp(lr * dt)
    ang = lam_im * dt
    ab_re = mag * jnp.cos(ang)
    ab_im = mag * jnp.sin(ang)
    num_re = ab_re - 1.0
    num_im = ab_im
    den = lr * lr + lam_im * lam_im
    f_re = (num_re * lr + num_im * lam_im) / den
    f_im = (num_im * lr - num_re * lam_im) / den
    return ab_re, ab_im, f_re, f_im


def _ssm_params_forward(lam_re_log, lam_im, log_dt):
    def body(a_ref, b_ref, c_ref, o0, o1, o2, o3):
        outs = _ssm_discretise(a_ref[...], b_ref[...], c_ref[...])
        for o, v in zip((o0, o1, o2, o3), outs):
            o[...] = v

    return pl.pallas_call(body, name="ssm_params_forward",
                          out_shape=[jax.ShapeDtypeStruct(lam_im.shape, F32)] * 4)(lam_re_log, lam_im, log_dt)


def _ssm_params_backward(lam_re_log, lam_im, log_dt, cot):
    def body(a_ref, b_ref, c_ref, g0, g1, g2, g3, o0, o1, o2):
        _, vjp = jax.vjp(_ssm_discretise, a_ref[...], b_ref[...], c_ref[...])
        d0, d1, d2 = vjp((g0[...], g1[...], g2[...], g3[...]))
        o0[...] = d0
        o1[...] = d1
        o2[...] = d2

    return pl.pallas_call(
        body, name="ssm_params_backward",
        out_shape=[jax.ShapeDtypeStruct(lam_im.shape, F32), jax.ShapeDtypeStruct(lam_im.shape, F32),
                   jax.ShapeDtypeStruct(log_dt.shape, F32)])(lam_re_log, lam_im, log_dt, *cot)


def _ssm_dense_forward(srow, b_dense, c_dense):
    def body(srow_ref, bd_ref, cd_ref, bb_ref, ct_ref):
        for j in range(SSM_BLOCKS):
            lanes = slice(j * SSM_BLOCK_STATE, (j + 1) * SSM_BLOCK_STATE)
            f_re, f_im = srow_ref[2:3, lanes], srow_ref[3:4, lanes]
            bb_ref[0, j] = (f_re * bd_ref[0, j] - f_im * bd_ref[1, j]).astype(MXU_DTYPE)
            bb_ref[1, j] = (f_re * bd_ref[1, j] + f_im * bd_ref[0, j]).astype(MXU_DTYPE)
            ct_ref[0, j] = cd_ref[0, j].astype(MXU_DTYPE)
            ct_ref[1, j] = cd_ref[1, j].astype(MXU_DTYPE)

    return pl.pallas_call(body, name="ssm_dense_forward",
                          out_shape=[jax.ShapeDtypeStruct(b_dense.shape, MXU_DTYPE),
                                     jax.ShapeDtypeStruct(c_dense.shape, MXU_DTYPE)],
                          compiler_params=pltpu.CompilerParams(vmem_limit_bytes=VMEM_LIMIT))(srow, b_dense, c_dense)


def _cmul(p, q):
    return p[0] * q[0] - p[1] * q[1], p[0] * q[1] + p[1] * q[0]


def _scan_coefficients(ar, ai, reverse):
    n = ar.shape[1]
    p = {1: (ar, ai)}
    p[2] = _cmul(p[1], p[1])
    p[3] = _cmul(p[2], p[1])
    p[4] = _cmul(p[2], p[2])
    p[5] = _cmul(p[4], p[1])
    p[6] = _cmul(p[4], p[2])
    p[7] = _cmul(p[4], p[3])
    p[8] = _cmul(p[4], p[4])
    rowi = lax.broadcasted_iota(jnp.int32, (SCAN_ROWS, n), 0)
    tiles = []
    for dstep in (1, 2, 4):
        keep = (rowi < SCAN_ROWS - dstep) if reverse else (rowi >= dstep)
        for part in p[dstep]:
            tiles.append(jnp.where(keep, jnp.broadcast_to(part, (SCAN_ROWS, n)), 0.0))
    for comp in (0, 1):
        t = jnp.zeros((SCAN_ROWS, n), F32)
        for rr in range(SCAN_ROWS):
            power = SCAN_ROWS - rr if reverse else rr + 1
            t = jnp.where(rowi == rr, jnp.broadcast_to(p[power][comp], (SCAN_ROWS, n)), t)
        tiles.append(t)
    return tiles


def _load_stack(stack_hbm, dst, sems, base):
    cols = stack_hbm.shape[2]
    cps = [pltpu.make_async_copy(stack_hbm.at[k], dst.at[:, pl.ds(k * cols, cols)], sems.at[base + k])
           for k in range(NDEV)]
    for cp in cps:
        cp.start()
    return cps


def _window_lanes():
    lane = lax.broadcasted_iota(jnp.int32, (1, POOL_WIDTH), 1)
    return jnp.where(lane < 128, 2.0, jnp.where(lane < 256, 4.0, jnp.where(lane < 384, 8.0, 16.0)))


def _gelu(y):
    return 0.5 * y * (1.0 + lax.erf(y * 0.7071067811865476))


def _gelu_grad(y):
    return 0.5 * (1.0 + lax.erf(y * 0.7071067811865476)) + y * jnp.exp(-0.5 * y * y) * 0.3989422804014327


def _mixer_forward(x, prm, w_in_s, w_pu_s, w_glu_s, w_su_s, w_out, pool_w, mvec, srow, bb, ct, carry=None):
    T = x.shape[0]
    tm = min(T, TM_MIX)
    nt = T // tm
    n_tiles = tm // SCAN_ROWS

    def body(x_ref, prm_ref, w_in_h, w_pu_h, w_glu_h, w_su_h, w_out_h, pw_ref, mv_ref, srow_ref, bb, ct,
             x2_ref, mo_ref, z_ref, sre_ref, sim_ref, zp_ref, q_ref, yp_ref, yss_ref, vg_ref, ys_ref,
             w_in, w_pu, w_glu, w_su, w_o, coef, carry, hist, bu, sems):
        i = pl.program_id(0)

        @pl.when(i == 0)
        def _():
            cps = (_load_stack(w_in_h, w_in, sems, 0) + _load_stack(w_pu_h, w_pu, sems, 8)
                   + _load_stack(w_glu_h, w_glu, sems, 16) + _load_stack(w_su_h, w_su, sems, 24))
            cps.append(pltpu.make_async_copy(w_out_h, w_o, sems.at[32]))
            cps[-1].start()
            for j in range(SSM_BLOCKS):
                lanes = slice(j * SSM_BLOCK_STATE, (j + 1) * SSM_BLOCK_STATE)
                for k, t in enumerate(_scan_coefficients(srow_ref[0:1, lanes], srow_ref[1:2, lanes], False)):
                    coef[j, k] = t
            carry[...] = jnp.zeros_like(carry)
            hist[...] = jnp.zeros_like(hist)
            for cp in cps:
                cp.wait()

        xv = x_ref[...]
        h, _, _ = _modulated(xv, prm_ref, 1, ROW_G_MIX)
        z = _mm(h, w_in[...])
        z_ref[...] = z.astype(SAVE_DTYPE)
        u_pool, u_ssm = z[:, 0:512], z[:, 512:1024]
        gl_pool, gl_ssm = z[:, 1024:2048], z[:, 2048:3072]

        ext = jnp.concatenate([hist[...], u_pool], axis=0)
        w2 = ext + pltpu.roll(ext, 1, 0)
        w4 = w2[:, 128:] + pltpu.roll(w2[:, 128:], 2, 0)
        w8 = w4[:, 128:] + pltpu.roll(w4[:, 128:], 4, 0)
        w16 = w8[:, 128:] + pltpu.roll(w8[:, 128:], 8, 0)
        wsum = jnp.concatenate([w2[POOL_HALO:, :128], w4[POOL_HALO:, :128], w8[POOL_HALO:, :128], w16[POOL_HALO:]], axis=1)
        hist[...] = u_pool[tm - POOL_HALO:, :]
        t1 = (lax.broadcasted_iota(jnp.int32, (tm, 1), 0) + (i * tm + 1)).astype(F32)
        zp = wsum / jnp.minimum(t1, _window_lanes()) - u_pool
        zp_ref[...] = zp.astype(SAVE_DTYPE)
        q = jnp.concatenate([_mm(zp[:, k * 128:(k + 1) * 128], pw_ref[k]) for k in range(4)], axis=1)
        q = q + mv_ref[ROW_POOL_B:ROW_POOL_B + 1, 0:512]
        q_ref[...] = q.astype(SAVE_DTYPE)
        y_pool = _mm(q * mv_ref[ROW_POOL_SCALE:ROW_POOL_SCALE + 1, 0:512], w_pu[...])
        yp_ref[...] = y_pool.astype(SAVE_DTYPE)

        y_blocks = []
        for j in range(SSM_BLOCKS):
            lanes = pl.ds(j * SSM_BLOCK_STATE, SSM_BLOCK_STATE)
            ub = u_ssm[:, j * 128:(j + 1) * 128].astype(MXU_DTYPE)
            bu[0] = _mm(ub, bb[0, j])
            bu[1] = _mm(ub, bb[1, j])
            a1r, a1i, a2r, a2i, a4r, a4i, pr, pi = [coef[j, k] for k in range(8)]

            def step(tt, c, lanes=lanes, a1r=a1r, a1i=a1i, a2r=a2r, a2i=a2i, a4r=a4r, a4i=a4i, pr=pr, pi=pi):
                cr, ci = c
                rows = pl.ds(pl.multiple_of(tt * SCAN_ROWS, SCAN_ROWS), SCAN_ROWS)
                xr, xi = bu[0, rows, :], bu[1, rows, :]
                for dstep, kr, ki in ((1, a1r, a1i), (2, a2r, a2i), (4, a4r, a4i)):
                    sr, si = pltpu.roll(xr, dstep, 0), pltpu.roll(xi, dstep, 0)
                    xr, xi = xr + kr * sr - ki * si, xi + kr * si + ki * sr
                xr, xi = xr + pr * cr - pi * ci, xi + pr * ci + pi * cr
                sre_ref[rows, lanes] = xr
                sim_ref[rows, lanes] = xi
                return (jnp.broadcast_to(xr[SCAN_ROWS - 1:SCAN_ROWS, :], xr.shape),
                        jnp.broadcast_to(xi[SCAN_ROWS - 1:SCAN_ROWS, :], xi.shape))

            cr, ci = lax.fori_loop(0, n_tiles, step, (carry[j, 0], carry[j, 1]))
            carry[j, 0] = cr
            carry[j, 1] = ci
            y_blocks.append(_mm(sre_ref[:, lanes], ct[0, j]) - _mm(sim_ref[:, lanes], ct[1, j]))
        yss = jnp.concatenate(y_blocks, axis=1) + mv_ref[ROW_SSM_D:ROW_SSM_D + 1, 0:512] * u_ssm
        yss_ref[...] = yss.astype(SAVE_DTYPE)
        vg = _mm(_gelu(yss), w_glu[...]) + mv_ref[ROW_B_GLU:ROW_B_GLU + 1, :]
        vg_ref[...] = vg.astype(SAVE_DTYPE)
        y_ssm = _mm(vg[:, 0:512] * _sigmoid(vg[:, 512:1024]), w_su[...])
        ys_ref[...] = y_ssm.astype(SAVE_DTYPE)

        merged = _sigmoid(gl_pool) * y_pool + _sigmoid(gl_ssm) * y_ssm
        mo = _mm(merged, w_o[...])
        mo_ref[...] = mo.astype(SAVE_DTYPE)
        x2_ref[...] = xv + _row(prm_ref, 5) * mo

    def tok(width):
        return pl.BlockSpec((tm, width), lambda i: (i, 0))

    hbm = _HBM
    widths = (D_MODEL, D_MODEL, IN_WIDTH, N_STATE, N_STATE, 512, 512, D_MODEL, 512, D_MODEL, D_MODEL)
    dtypes = (F32, SAVE_DTYPE, SAVE_DTYPE, F32, F32) + (SAVE_DTYPE,) * 6
    return _launch(
        body, "mixer_forward", grid=(nt,), semantics=("arbitrary",), carry=carry, steps=_grid_steps(nt),
        out_shape=[jax.ShapeDtypeStruct((T, w), dt) for w, dt in zip(widths, dtypes)],
        in_specs=[tok(D_MODEL), _resident(prm), hbm, hbm, hbm, hbm, hbm, _resident(pool_w), _resident(mvec),
                  _resident(srow), _resident(bb), _resident(ct)],
        out_specs=[tok(w) for w in widths],
        operands=(x, prm, w_in_s, w_pu_s, w_glu_s, w_su_s, w_out, pool_w, mvec, srow, bb, ct),
        scratch=[
            pltpu.VMEM((D_MODEL, IN_WIDTH), MXU_DTYPE), pltpu.VMEM((512, D_MODEL), MXU_DTYPE),
            pltpu.VMEM((512, D_MODEL), MXU_DTYPE), pltpu.VMEM((512, D_MODEL), MXU_DTYPE),
            pltpu.VMEM((D_MODEL, D_MODEL), MXU_DTYPE),
            pltpu.VMEM((SSM_BLOCKS, 8, SCAN_ROWS, SSM_BLOCK_STATE), F32),
            pltpu.VMEM((SSM_BLOCKS, 2, SCAN_ROWS, SSM_BLOCK_STATE), F32),
            pltpu.VMEM((POOL_HALO, POOL_WIDTH), F32),
            pltpu.VMEM((2, tm, SSM_BLOCK_STATE), F32),
            pltpu.SemaphoreType.DMA((33,)),
        ])


def _mixer_backward(d2, prm, saved, w_pu_s, w_glu_s, w_su_s, w_out, pool_w, mvec, srow, bb, ct, carry=None):
    z, s_re, s_im, zp, q, y_pool, yss, vg, y_ssm = saved
    T = d2.shape[0]
    tm = min(T, TM_MIX_BWD)
    nt = T // tm
    n_tiles = tm // SCAN_ROWS

    def body(d_ref, prm_ref, z_ref, sre_ref, sim_ref, zp_ref, q_ref, yp_ref, yss_ref, vg_ref, ys_ref,
             w_pu_h, w_glu_h, w_su_h, w_out_h, pw_ref, mv_ref, srow_ref, bb, ct,
             dz_ref, dwo_h, dwpu_h, dwglu_h, dwsu_h, dpw_h, dbb_h, dct_h, vsum_h, da_h,
             w_pu, w_glu, w_su, w_o, pwb, coef, carry, hist, dre, lam,
             a_wo, a_wpu, a_wglu, a_wsu, a_pw, a_bb, a_ct, a_vs, a_da, st_wo, st_up, sems):
        i = pl.program_id(0)
        tile = nt - 1 - i

        @pl.when(i == 0)
        def _():
            cps = (_load_stack(w_pu_h, w_pu, sems, 0) + _load_stack(w_glu_h, w_glu, sems, 8)
                   + _load_stack(w_su_h, w_su, sems, 16))
            cps.append(pltpu.make_async_copy(w_out_h, w_o, sems.at[24]))
            cps[-1].start()
            pwb[...] = pw_ref[...].astype(MXU_DTYPE)
            for j in range(SSM_BLOCKS):
                lanes = slice(j * SSM_BLOCK_STATE, (j + 1) * SSM_BLOCK_STATE)
                for k, t in enumerate(_scan_coefficients(srow_ref[0:1, lanes], srow_ref[1:2, lanes], True)):
                    coef[j, k] = t
            for acc in (carry, hist, a_wo, a_wpu, a_wglu, a_wsu, a_pw, a_bb, a_ct, a_vs, a_da):
                acc[...] = jnp.zeros_like(acc)
            for cp in cps:
                cp.wait()

        dv = d_ref[...]
        zt = z_ref[...].astype(F32)
        u_ssm, gl_pool, gl_ssm = zt[:, 512:1024], zt[:, 1024:2048], zt[:, 2048:3072]
        y_p, y_s = yp_ref[...].astype(F32), ys_ref[...].astype(F32)
        sgp, sgs = _sigmoid(gl_pool), _sigmoid(gl_ssm)
        dmo = (_row(prm_ref, 5) * dv).astype(MXU_DTYPE)
        a_wo[...] += _mm_tn(sgp * y_p + sgs * y_s, dmo)
        dmerged = _mm_nt(dmo, w_o[...])
        dy_pool = dmerged * sgp
        dgl_pool = dmerged * y_p * (sgp * (1.0 - sgp))
        dy_ssm = dmerged * sgs
        dgl_ssm = dmerged * y_s * (sgs * (1.0 - sgs))

        scale = mv_ref[ROW_POOL_SCALE:ROW_POOL_SCALE + 1, 0:512]
        qv, zpv = q_ref[...].astype(F32), zp_ref[...]
        a_wpu[...] += _mm_tn(qv * scale, dy_pool)
        dp = _mm_nt(dy_pool, w_pu[...])
        dq = dp * scale
        a_vs[0:1, 0:512] += _colsum(dp * qv)
        a_vs[1:2, 0:512] += _colsum(dq)
        dzp_blocks = []
        for k in range(4):
            lanes = slice(k * 128, (k + 1) * 128)
            dzp_blocks.append(_mm_nt(dq[:, lanes], pwb[k]))
            a_pw[k] += _mm_tn(zpv[:, lanes], dq[:, lanes])
        dzp = jnp.concatenate(dzp_blocks, axis=1)
        t1 = (lax.broadcasted_iota(jnp.int32, (tm, 1), 0) + (tile * tm + 1)).astype(F32)
        gs = dzp / jnp.minimum(t1, _window_lanes())
        n_ext = tm + POOL_HALO
        ext = jnp.concatenate([gs, hist[...]], axis=0)
        v2 = ext + pltpu.roll(ext, n_ext - 1, 0)
        v4 = v2[:, 128:] + pltpu.roll(v2[:, 128:], n_ext - 2, 0)
        v8 = v4[:, 128:] + pltpu.roll(v4[:, 128:], n_ext - 4, 0)
        v16 = v8[:, 128:] + pltpu.roll(v8[:, 128:], n_ext - 8, 0)
        msum = jnp.concatenate([v2[:tm, :128], v4[:tm, :128], v8[:tm, :128], v16[:tm]], axis=1)
        hist[...] = gs[0:POOL_HALO, :]
        du_pool = msum - dzp

        vgv = vg_ref[...].astype(F32)
        val, gate = vgv[:, 0:512], vgv[:, 512:1024]
        sgg = _sigmoid(gate)
        a_wsu[...] += _mm_tn(val * sgg, dy_ssm)
        do = _mm_nt(dy_ssm, w_su[...])
        dvg = jnp.concatenate([do * sgg, do * val * (sgg * (1.0 - sgg))], axis=1)
        a_vs[3:4, :] += _colsum(dvg)
        yv = yss_ref[...].astype(F32)
        a_wglu[...] += _mm_tn(_gelu(yv), dvg)
        dyss = _mm_nt(dvg, w_glu[...]) * _gelu_grad(yv)
        a_vs[2:3, 0:512] += _colsum(dyss * u_ssm)
        du_blocks = []
        for j in range(SSM_BLOCKS):
            lanes = pl.ds(j * SSM_BLOCK_STATE, SSM_BLOCK_STATE)
            in_lanes = slice(j * 128, (j + 1) * 128)
            dyb = dyss[:, in_lanes].astype(MXU_DTYPE)
            ub = u_ssm[:, in_lanes].astype(MXU_DTYPE)
            dre[0] = _mm_nt(dyb, ct[0, j])
            dre[1] = -_mm_nt(dyb, ct[1, j])
            a_ct[0, j] += _mm_tn(sre_ref[:, lanes], dyb)
            a_ct[1, j] -= _mm_tn(sim_ref[:, lanes], dyb)
            a1r, a1i, a2r, a2i, a4r, a4i, pr, pi = [coef[j, k] for k in range(8)]
            rowi = lax.broadcasted_iota(jnp.int32, (SCAN_ROWS, SSM_BLOCK_STATE), 0)

            def step(tt, c, lanes=lanes, a1r=a1r, a1i=a1i, a2r=a2r, a2i=a2i, a4r=a4r, a4i=a4i, pr=pr, pi=pi, rowi=rowi):
                cr, ci, acc_r, acc_i = c
                rows = pl.ds(pl.multiple_of((n_tiles - 1 - tt) * SCAN_ROWS, SCAN_ROWS), SCAN_ROWS)
                xr, xi = dre[0, rows, :], dre[1, rows, :]
                for dstep, kr, ki in ((1, a1r, a1i), (2, a2r, a2i), (4, a4r, a4i)):
                    sr, si = pltpu.roll(xr, SCAN_ROWS - dstep, 0), pltpu.roll(xi, SCAN_ROWS - dstep, 0)
                    xr, xi = xr + kr * sr + ki * si, xi + kr * si - ki * sr
                xr, xi = xr + pr * cr + pi * ci, xi + pr * ci - pi * cr
                lam[0, rows, :] = xr
                lam[1, rows, :] = xi
                nr = jnp.where(rowi == SCAN_ROWS - 1, cr, pltpu.roll(xr, SCAN_ROWS - 1, 0))
                ni = jnp.where(rowi == SCAN_ROWS - 1, ci, pltpu.roll(xi, SCAN_ROWS - 1, 0))
                s_r, s_i = sre_ref[rows, lanes], sim_ref[rows, lanes]
                acc_r = acc_r + nr * s_r + ni * s_i
                acc_i = acc_i + ni * s_r - nr * s_i
                return (jnp.broadcast_to(xr[0:1, :], xr.shape), jnp.broadcast_to(xi[0:1, :], xi.shape), acc_r, acc_i)

            cr, ci, acc_r, acc_i = lax.fori_loop(0, n_tiles, step, (carry[j, 0], carry[j, 1], a_da[0, j], a_da[1, j]))
            carry[j, 0] = cr
            carry[j, 1] = ci
            a_da[0, j] = acc_r
            a_da[1, j] = acc_i
            lr_b, li_b = lam[0].astype(MXU_DTYPE), lam[1].astype(MXU_DTYPE)
            a_bb[0, j] += _mm_tn(ub, lr_b)
            a_bb[1, j] += _mm_tn(ub, li_b)
            du_blocks.append(_mm_nt(lr_b, bb[0, j]) + _mm_nt(li_b, bb[1, j]))
        du_ssm = jnp.concatenate(du_blocks, axis=1) + dyss * mv_ref[ROW_SSM_D:ROW_SSM_D + 1, 0:512]
        dz_ref[...] = jnp.concatenate([du_pool, du_ssm, dgl_pool, dgl_ssm], axis=1).astype(SAVE_DTYPE)

        @pl.when(i == nt - 1)
        def _():
            rows = D_MODEL // NDEV
            for k in range(NDEV):
                st_wo[k] = a_wo[k * rows:(k + 1) * rows, :].astype(WIRE_DTYPE)
                for a, acc in enumerate((a_wpu, a_wglu, a_wsu)):
                    st_up[a, k] = acc[:, k * 128:(k + 1) * 128].astype(WIRE_DTYPE)
            outs = ((st_wo, dwo_h), (st_up.at[0], dwpu_h), (st_up.at[1], dwglu_h), (st_up.at[2], dwsu_h),
                    (a_pw, dpw_h), (a_bb, dbb_h), (a_ct, dct_h), (a_vs, vsum_h), (a_da, da_h))
            cps = [pltpu.make_async_copy(src, dst, sems.at[k]) for k, (src, dst) in enumerate(outs)]
            for cp in cps:
                cp.start()
            for cp in cps:
                cp.wait()

    def tok(width):
        return pl.BlockSpec((tm, width), lambda i: (nt - 1 - i, 0))

    hbm = _HBM
    acc_shapes = [(D_MODEL, D_MODEL), (512, D_MODEL), (512, D_MODEL), (512, D_MODEL), (4, 128, 128),
                  (2, SSM_BLOCKS, 128, SSM_BLOCK_STATE), (2, SSM_BLOCKS, SSM_BLOCK_STATE, 128), (8, D_MODEL),
                  (2, SSM_BLOCKS, SCAN_ROWS, SSM_BLOCK_STATE)]
    stack_out = [jax.ShapeDtypeStruct((NDEV, D_MODEL // NDEV, D_MODEL), WIRE_DTYPE)] \
        + [jax.ShapeDtypeStruct((NDEV, 512, 128), WIRE_DTYPE)] * 3
    return _launch(
        body, "mixer_backward", grid=(nt,), semantics=("arbitrary",), carry=carry, steps=_grid_steps(nt),
        out_shape=[jax.ShapeDtypeStruct((T, IN_WIDTH), SAVE_DTYPE)] + stack_out
        + [jax.ShapeDtypeStruct(s, F32) for s in acc_shapes[4:]],
        in_specs=[tok(D_MODEL), _resident(prm), tok(IN_WIDTH), tok(N_STATE), tok(N_STATE), tok(512), tok(512),
                  tok(D_MODEL), tok(512), tok(D_MODEL), tok(D_MODEL), hbm, hbm, hbm, hbm, _resident(pool_w),
                  _resident(mvec), _resident(srow), _resident(bb), _resident(ct)],
        out_specs=[tok(IN_WIDTH)] + [hbm] * len(acc_shapes),
        operands=(d2, prm, z, s_re, s_im, zp, q, y_pool, yss, vg, y_ssm, w_pu_s, w_glu_s, w_su_s, w_out, pool_w, mvec,
                  srow, bb, ct),
        scratch=[
            pltpu.VMEM((512, D_MODEL), MXU_DTYPE), pltpu.VMEM((512, D_MODEL), MXU_DTYPE),
            pltpu.VMEM((512, D_MODEL), MXU_DTYPE), pltpu.VMEM((D_MODEL, D_MODEL), MXU_DTYPE),
            pltpu.VMEM((4, 128, 128), MXU_DTYPE),
            pltpu.VMEM((SSM_BLOCKS, 8, SCAN_ROWS, SSM_BLOCK_STATE), F32),
            pltpu.VMEM((SSM_BLOCKS, 2, SCAN_ROWS, SSM_BLOCK_STATE), F32),
            pltpu.VMEM((POOL_HALO, POOL_WIDTH), F32),
            pltpu.VMEM((2, tm, SSM_BLOCK_STATE), F32), pltpu.VMEM((2, tm, SSM_BLOCK_STATE), F32),
        ] + [pltpu.VMEM(s, F32) for s in acc_shapes]
        + [pltpu.VMEM((NDEV, D_MODEL // NDEV, D_MODEL), WIRE_DTYPE), pltpu.VMEM((3, NDEV, 512, 128), WIRE_DTYPE),
           pltpu.SemaphoreType.DMA((25,))])


def _mixer_in_backward(x, dz, d, mo, prm, w_in_s):
    T = x.shape[0]
    tm = min(T, TM_MIX)
    nt = T // tm
    cols = IN_WIDTH // NDEV

    def body(x_ref, dz_ref, d_ref, mo_ref, prm_ref, w_in_h, dx_ref, sums_ref, dw_ref, w_in, acc, sems):
        i = pl.program_id(0)

        @pl.when(i == 0)
        def _():
            cps = _load_stack(w_in_h, w_in, sems, 0)
            acc[...] = jnp.zeros_like(acc)
            for cp in cps:
                cp.wait()

        xv = x_ref[...]
        h, _, _ = _modulated(xv, prm_ref, 1, ROW_G_MIX)
        dzb = dz_ref[...].astype(MXU_DTYPE)
        acc[...] += _mm_tn(h, dzb)
        dx_ref[...], upd = _norm_backward_tile(_mm_nt(dzb, w_in[...]), xv, d_ref[...], mo_ref[...], prm_ref, 1,
                                               ROW_G_MIX, 1.0)

        @pl.when(i == 0)
        def _():
            sums_ref[...] = upd

        @pl.when(i > 0)
        def _():
            sums_ref[...] += upd

        @pl.when(i == nt - 1)
        def _():
            for k in range(NDEV):
                dw_ref[k] = acc[:, k * cols:(k + 1) * cols].astype(WIRE_DTYPE)

    tok = pl.BlockSpec((tm, D_MODEL), lambda i: (i, 0))
    return pl.pallas_call(
        body, name="mixer_in_backward", grid=(nt,),
        out_shape=[jax.ShapeDtypeStruct((T, D_MODEL), F32), jax.ShapeDtypeStruct((8, D_MODEL), F32),
                   jax.ShapeDtypeStruct((NDEV, D_MODEL, cols), WIRE_DTYPE)],
        in_specs=[tok, pl.BlockSpec((tm, IN_WIDTH), lambda i: (i, 0)), tok, tok, _resident(prm), _HBM],
        out_specs=[tok, pl.BlockSpec((8, D_MODEL), lambda i: (0, 0)),
                   pl.BlockSpec((NDEV, D_MODEL, cols), lambda i: (0, 0, 0))],
        scratch_shapes=[pltpu.VMEM((D_MODEL, IN_WIDTH), MXU_DTYPE), pltpu.VMEM((D_MODEL, IN_WIDTH), F32),
                        pltpu.SemaphoreType.DMA((8,))],
        compiler_params=_params("arbitrary"),
    )(x, dz, d, mo, prm, w_in_s)


def _ssm_dense_backward(dbb, da, srow, b_dense):
    def body(dbb_ref, da_ref, srow_ref, bd_ref, db_ref, df_ref):
        df_re, df_im = [], []
        da_re = [_colsum(da_ref[0, j]) for j in range(SSM_BLOCKS)]
        da_im = [_colsum(da_ref[1, j]) for j in range(SSM_BLOCKS)]
        for j in range(SSM_BLOCKS):
            lanes = slice(j * SSM_BLOCK_STATE, (j + 1) * SSM_BLOCK_STATE)
            f_re, f_im = srow_ref[2:3, lanes], srow_ref[3:4, lanes]
            g_re, g_im = dbb_ref[0, j], dbb_ref[1, j]
            b_re, b_im = bd_ref[0, j], bd_ref[1, j]
            db_ref[0, j] = f_re * g_re + f_im * g_im
            db_ref[1, j] = f_re * g_im - f_im * g_re
            df_re.append(_colsum(g_re * b_re + g_im * b_im))
            df_im.append(_colsum(g_im * b_re - g_re * b_im))
        df_ref[...] = jnp.concatenate([jnp.concatenate(df_re, axis=1), jnp.concatenate(df_im, axis=1),
                                       jnp.concatenate(da_re, axis=1), jnp.concatenate(da_im, axis=1),
                                       jnp.zeros((4, N_STATE), F32)], axis=0)

    return pl.pallas_call(body, name="ssm_dense_backward",
                          out_shape=[jax.ShapeDtypeStruct(b_dense.shape, F32), jax.ShapeDtypeStruct((8, N_STATE), F32)],
                          compiler_params=pltpu.CompilerParams(vmem_limit_bytes=VMEM_LIMIT))(dbb, da, srow, b_dense)


def _adamw_update(w, g, m, v):
    m = ADAM_B1 * m + (1.0 - ADAM_B1) * g
    v = ADAM_B2 * v + (1.0 - ADAM_B2) * (g * g)
    m_hat = m / (1.0 - ADAM_B1 ** ADAM_STEP)
    v_hat = v / (1.0 - ADAM_B2 ** ADAM_STEP)
    delta = -ADAM_LR * (m_hat / (jnp.sqrt(v_hat) + ADAM_EPS) + ADAM_WD * w)
    return delta, m, v


def _adam_rows(shape):
    rows, cols = shape
    tr = rows
    while tr * cols * 4 > (1 << 20) and tr % 16 == 0:
        tr //= 2
    return tr


def _adam_sharded(w, m, v, land, order, name):
    R, C = w.shape
    tr = _adam_rows((R, C))

    def body(w_ref, m_ref, v_ref, land_ref, order_ref, g_ref, d_ref, mo_ref, vo_ref):
        g = land_ref[0].astype(F32)
        for b in range(1, NDEV):
            g = g + land_ref[b].astype(F32)
        g_ref[...] = g
        d_ref[...], mo_ref[...], vo_ref[...] = _adamw_update(w_ref[...], g, m_ref[...], v_ref[...])

    blk = pl.BlockSpec((tr, C), lambda i: (i, 0))
    return pl.pallas_call(
        body, name=name, grid=(R // tr,),
        out_shape=[jax.ShapeDtypeStruct((R, C), F32)] * 4,
        in_specs=[blk, blk, blk, pl.BlockSpec((NDEV, tr, C), lambda i: (0, i, 0)), _HBM],
        out_specs=[blk] * 4,
        compiler_params=_params("arbitrary"),
    )(w, m, v, land, order)


def _adam_ada(w, m, v, sc_all, dmod_cols):
    R, C = w.shape
    tr = 256

    def body(w_ref, m_ref, v_ref, sc_ref, dm_ref, g_ref, d_ref, mo_ref, vo_ref):
        g = _mm_tn(sc_ref[...], dm_ref[...])
        g_ref[...] = g
        d_ref[...], mo_ref[...], vo_ref[...] = _adamw_update(w_ref[...], g, m_ref[...], v_ref[...])

    blk = pl.BlockSpec((tr, C), lambda i: (i, 0))
    return pl.pallas_call(
        body, name="adam_w_ada", grid=(R // tr,),
        out_shape=[jax.ShapeDtypeStruct((R, C), F32)] * 4,
        in_specs=[blk, blk, blk, pl.BlockSpec((8, tr), lambda i: (0, i)), pl.BlockSpec((8, C), lambda i: (0, 0))],
        out_specs=[blk] * 4,
        compiler_params=_params("arbitrary"),
    )(w, m, v, sc_all, dmod_cols)


def _adam_small(w, g, m, v, name):
    def body(w_ref, g_ref, m_ref, v_ref, d_ref, mo_ref, vo_ref):
        d_ref[...], mo_ref[...], vo_ref[...] = _adamw_update(w_ref[...], g_ref[...], m_ref[...], v_ref[...])

    return pl.pallas_call(body, name=name, out_shape=[jax.ShapeDtypeStruct(w.shape, F32)] * 3,
                          compiler_params=pltpu.CompilerParams(vmem_limit_bytes=VMEM_LIMIT))(w, g, m, v)


def _block_diag_in(b):
    bt = jnp.transpose(b, (0, 2, 1)).reshape(SSM_BLOCKS, 8, SSM_GROUP, SSM_STATE)
    eye = jnp.eye(8, dtype=bool)[None, :, None, :, None]
    return jnp.where(eye, bt[:, :, :, None, :], 0.0).reshape(SSM_BLOCKS, 128, SSM_BLOCK_STATE)


def _block_diag_out(c):
    ct = jnp.transpose(c, (0, 2, 1)).reshape(SSM_BLOCKS, 8, SSM_STATE, SSM_GROUP)
    eye = jnp.eye(8, dtype=bool)[None, :, None, :, None]
    return jnp.where(eye, ct[:, :, :, None, :], 0.0).reshape(SSM_BLOCKS, SSM_BLOCK_STATE, 128)


def _diag_blocks(dense, rows, cols):
    d5 = dense.reshape(SSM_BLOCKS, 8, rows, 8, cols)
    return jnp.stack([d5[:, a, :, a, :] for a in range(8)], axis=1).reshape(32, rows, cols)


def _pack_small(ada_vec, parts, params, tail=None):
    rest_rows, rows = _pack_rows(params, ada_vec is not None)
    rest = jnp.concatenate([parts[n].reshape(-1) for n, _ in params])
    rest = jnp.pad(rest, (0, NDEV * rest_rows * 128 - rest.shape[0])).reshape(NDEV, rest_rows, 128)
    head = [] if ada_vec is None else [ada_vec.reshape(NDEV, ADA_ROWS, 128)]
    pad = rows - rest_rows - (0 if ada_vec is None else ADA_ROWS)
    fill = jnp.zeros((NDEV, pad, 128), F32) if tail is None else jnp.pad(tail[None], ((0, NDEV - 1), (0, pad - 1), (0, 127)))
    return jnp.concatenate(head + [rest] + ([fill] if pad else []), axis=1)


def _unpack_small(pack, shapes, params, with_ada):
    rest_rows, _ = _pack_rows(params, with_ada)
    first = ADA_ROWS if with_ada else 0
    ada_vec = pack[:, :first].reshape(-1) if with_ada else None
    rest = pack[:, first:first + rest_rows].reshape(-1)
    out, off = {}, 0
    for n, size in params:
        out[n] = rest[off:off + size].reshape(shapes[n])
        off += size
    return ada_vec, out


WEIGHT_ORDER = ('w_ada', 'b_ada', 'g_ffn1', 'w_ffn1_in', 'w_ffn1_out', 'g_mix', 'w_in', 'pool_w', 'pool_b',
                'pool_scale', 'w_pool_up', 'ssm_lam_re_log', 'ssm_lam_im', 'ssm_log_dt', 'ssm_b_re', 'ssm_b_im',
                'ssm_c_re', 'ssm_c_im', 'ssm_d', 'w_glu', 'b_glu', 'w_ssm_up', 'w_out', 'g_ffn2', 'w_ffn2_in',
                'w_ffn2_out', 'g_final')
GATHERED = ('w_ffn1_in', 'w_ffn1_out', 'w_in', 'w_pool_up', 'w_glu', 'w_ssm_up', 'w_out', 'w_ffn2_in', 'w_ffn2_out')
TRANSPOSED = ('w_ffn1_in', 'w_ffn2_in')
STATE_MINOR = ('ssm_b_re', 'ssm_b_im')


def kernel(x, c, w_ada, b_ada, g_ffn1, w_ffn1_in, w_ffn1_out, g_mix, w_in, pool_w, pool_b, pool_scale, w_pool_up, ssm_lam_re_log, ssm_lam_im, ssm_log_dt, ssm_b_re, ssm_b_im, ssm_c_re, ssm_c_im, ssm_d, w_glu, b_glu, w_ssm_up, w_out, g_ffn2, w_ffn2_in, w_ffn2_out, g_final, loss_target, m_w_ada, m_b_ada, m_g_ffn1, m_w_ffn1_in, m_w_ffn1_out, m_g_mix, m_w_in, m_pool_w, m_pool_b, m_pool_scale, m_w_pool_up, m_ssm_lam_re_log, m_ssm_lam_im, m_ssm_log_dt, m_ssm_b_re, m_ssm_b_im, m_ssm_c_re, m_ssm_c_im, m_ssm_d, m_w_glu, m_b_glu, m_w_ssm_up, m_w_out, m_g_ffn2, m_w_ffn2_in, m_w_ffn2_out, m_g_final, v_w_ada, v_b_ada, v_g_ffn1, v_w_ffn1_in, v_w_ffn1_out, v_g_mix, v_w_in, v_pool_w, v_pool_b, v_pool_scale, v_w_pool_up, v_ssm_lam_re_log, v_ssm_lam_im, v_ssm_log_dt, v_ssm_b_re, v_ssm_b_im, v_ssm_c_re, v_ssm_c_im, v_ssm_d, v_w_glu, v_b_glu, v_w_ssm_up, v_w_out, v_g_ffn2, v_w_ffn2_in, v_w_ffn2_out, v_g_final):
    args = locals()
    W = {n: args[n] for n in WEIGHT_ORDER}
    M = {n: args["m_" + n] for n in WEIGHT_ORDER}
    V = {n: args["v_" + n] for n in WEIGHT_ORDER}
    shapes = {n: W[n].shape for n in WEIGHT_ORDER}
    xt, tgt = x[0], loss_target[0]

    def local(tree, n):
        return jnp.swapaxes(tree[n][0], 0, 1) if n in TRANSPOSED else tree[n][0]

    def as_output(n, a):
        return (jnp.swapaxes(a, 0, 1) if n in TRANSPOSED else a)[None]

    shard = dict(zip(GATHERED, _cast_shards([local(W, n) for n in GATHERED])))
    stacks = {}

    def gather(names):
        return _Gather([shard[n] for n in names])

    def gathered(names, results):
        stacks.update(zip(names, results))

    ffn1_w, ffn2_w = ('w_ffn1_in', 'w_ffn1_out'), ('w_ffn2_in', 'w_ffn2_out')
    mix_w = ('w_in', 'w_pool_up', 'w_glu', 'w_ssm_up', 'w_out')
    mod_cols, sc_all, *res = _ada_forward(c, W['w_ada'][0], b_ada.reshape(NDEV, -1), gather(ffn1_w[:1]))
    gathered(ffn1_w[:1], res)
    win1 = stacks['w_ffn1_in'].reshape(2, 4, FF_SHARD, D_MODEL)
    prm = jnp.concatenate([mod_cols.reshape(9, D_MODEL), g_ffn1, g_mix, g_ffn2, g_final[None], jnp.zeros((3, D_MODEL), F32)], axis=0)
    pad512 = jnp.zeros((1, D_MODEL - 512), F32)
    mvec = jnp.concatenate([jnp.concatenate([pool_b, pad512], axis=1), jnp.concatenate([pool_scale, pad512], axis=1),
                            jnp.concatenate([ssm_d, pad512], axis=1), b_glu, jnp.zeros((4, D_MODEL), F32)], axis=0)
    log_dt_col = ssm_log_dt[0][:, None]
    coeffs = _ssm_params_forward(ssm_lam_re_log[0], ssm_lam_im[0], log_dt_col)
    srow = jnp.stack([t.reshape(N_STATE) for t in coeffs], axis=0)
    b_dense = jnp.stack([_block_diag_in(ssm_b_re[0]), _block_diag_in(ssm_b_im[0])], axis=0)
    c_dense = jnp.stack([_block_diag_out(ssm_c_re[0]), _block_diag_out(ssm_c_im[0])], axis=0)
    bb, ct = _ssm_dense_forward(srow, b_dense, c_dense)
    pw = pool_w[0]

    next_w = ffn1_w[1:] + mix_w[:1]
    ab1, s1, *res = _ffn_hidden(xt, prm, win1, 0, ROW_G_FFN1, "ffn1_hidden", gather(next_w))
    gathered(next_w, res)
    wout1 = stacks['w_ffn1_out'].reshape(4, FF_SHARD, D_MODEL)
    x1, f1, *res = _ffn_out(xt, s1, prm, wout1, 0, "ffn1_out", gather(mix_w[1:]))
    gathered(mix_w[1:], res)
    w_out_full = stacks['w_out'].reshape(D_MODEL, D_MODEL)
    res = _mixer_forward(x1, prm, stacks['w_in'], stacks['w_pool_up'], stacks['w_glu'], stacks['w_ssm_up'],
                         w_out_full, pw, mvec, srow, bb, ct, gather(ffn2_w))
    x2, mo, saved = res[0], res[1], res[2:11]
    gathered(ffn2_w, res[11:])
    win2 = stacks['w_ffn2_in'].reshape(2, 4, FF_SHARD, D_MODEL)
    wout2 = stacks['w_ffn2_out'].reshape(4, FF_SHARD, D_MODEL)
    d3, fin, f3, ab3 = _ffn_forward_loss(x2, tgt, prm, win2, wout2, 2, ROW_G_FFN2, "ffn2_forward_loss")

    lands = {}

    def scatter(grads):
        names = list(grads)
        return _Scatter([grads[n][0] for n in names], [grads[n][1] for n in names], [local(W, n).shape for n in names])

    def scattered(grads, results):
        lands.update(zip(grads, results))

    parts3, dwin2, dwout2 = _ffn_backward(x2, d3, ab3, prm, win2, wout2, prm, 2, ROW_G_FFN2, "ffn2_backward")
    d2, sums3 = _norm_backward(parts3, x2, d3, f3, prm, 2, ROW_G_FFN2, 0.5, "ffn2_norm_backward")
    g_ffn2_w = {'w_ffn2_in': (dwin2, _halves), 'w_ffn2_out': (dwout2.reshape(NDEV, -1, D_MODEL), _stacked)}
    res = _mixer_backward(d2, prm, saved, stacks['w_pool_up'], stacks['w_glu'], stacks['w_ssm_up'], w_out_full, pw, mvec,
                          srow, bb, ct, scatter(g_ffn2_w))
    dz, dwo, dwpu, dwglu, dwsu, dpw, dbb, dct, vsum, da = res[:10]
    scattered(g_ffn2_w, res[10:])
    d1, sums2, dwin_mix = _mixer_in_backward(x1, dz, d2, mo, prm, stacks['w_in'])
    g_mix_w = {'w_in': (dwin_mix, _stacked), 'w_pool_up': (dwpu, _stacked), 'w_glu': (dwglu, _stacked),
               'w_ssm_up': (dwsu, _stacked), 'w_out': (dwo, _stacked)}
    db_dense, df_rows = _ssm_dense_backward(dbb, da, srow, b_dense)
    cot = [df_rows[r].reshape(32, 64) for r in (2, 3, 0, 1)]
    d_lrl, d_li, d_ldt = _ssm_params_backward(ssm_lam_re_log[0], ssm_lam_im[0], log_dt_col, cot)
    small_grads = {
        'g_mix': sums2[0], 'g_ffn2': sums3[0], 'g_final': fin[0], 'pool_w': dpw,
        'pool_b': vsum[1, :512], 'pool_scale': vsum[0, :512], 'ssm_lam_re_log': d_lrl, 'ssm_lam_im': d_li,
        'ssm_log_dt': d_ldt, 'ssm_b_re': _diag_blocks(db_dense[0], SSM_GROUP, SSM_STATE),
        'ssm_b_im': _diag_blocks(db_dense[1], SSM_GROUP, SSM_STATE),
        'ssm_c_re': jnp.transpose(_diag_blocks(dct[0], SSM_STATE, SSM_GROUP), (0, 2, 1)),
        'ssm_c_im': jnp.transpose(_diag_blocks(dct[1], SSM_STATE, SSM_GROUP), (0, 2, 1)),
        'ssm_d': vsum[2, :512], 'b_glu': vsum[3],
    }
    total_early, _ = _allreduce_small(_pack_small(None, small_grads, SMALL_EARLY, fin[1:2, 0:1]), prm, "allreduce_early")
    loss = total_early[0, _pack_rows(SMALL_EARLY, False)[0], 0]
    parts1, dab1, dwout1, *res = _ffn_backward(xt, d1, ab1, prm, win1, wout1, total_early, 0, ROW_G_FFN1,
                                               "ffn1_backward", scatter(g_mix_w), defer_dwin=True)
    scattered(g_mix_w, res)
    g_wout1 = {'w_ffn1_out': (dwout1.reshape(NDEV, -1, D_MODEL), _stacked)}
    dwin1, *res = _ffn_dwin(xt, dab1, prm, 0, ROW_G_FFN1, "ffn1_dwin", scatter(g_wout1))
    scattered(g_wout1, res)

    last_w, last_views = ffn1_w[:1], [_halves]
    send_sems, recv_sems, last_src, last_land, token = _scatter_start(
        [dwin1], last_views, [local(W, n).shape for n in last_w], [total_early])
    after_start = token[0:1, 0:1]
    d0, sums1 = _norm_backward(parts1, xt, d1, f1, prm + after_start, 0, ROW_G_FFN1, 0.5, "ffn1_norm_backward")

    grad, delta, new_m, new_v = {}, {}, {}, {}

    def adam_sharded(n):
        res = _adam_sharded(local(W, n), local(M, n), local(V, n), lands[n], token, "adam_" + n)
        grad[n], delta[n], new_m[n], new_v[n] = [as_output(n, r) for r in res]
        return res[3]

    def adam_small(params, ada, total, name, order):
        rows = _pack_rows(params, ada)[1]
        views = [{n: jnp.transpose(t[n][0], (0, 2, 1)) if n in STATE_MINOR else t[n] for n, _ in params} for t in (W, M, V)]
        packs = [(_pack_small(t['b_ada'].reshape(-1) if ada else None, v, params) + order).reshape(NDEV * rows, 128)
                 for t, v in zip((W, M, V), views)]
        res = _adam_small(packs[0], total.reshape(NDEV * rows, 128), packs[1], packs[2], name)
        view_shapes = {n: (32, SSM_GROUP, SSM_STATE) if n in STATE_MINOR else shapes[n] for n, _ in params}
        for dst, packed in zip((grad, delta, new_m, new_v), (total, *res)):
            ada_vec, rest = _unpack_small(packed.reshape(NDEV, rows, 128), view_shapes, params, ada)
            dst.update({n: jnp.transpose(a, (0, 2, 1))[None] if n in STATE_MINOR else a for n, a in rest.items()})
            if ada:
                dst['b_ada'] = ada_vec.reshape(shapes['b_ada'])
        return res[2]

    done = [d0] + [adam_sharded(n) for n in GATHERED if n not in last_w]
    done.append(adam_small(SMALL_EARLY, False, total_early, "adam_small_early", after_start))
    firsts = [lax.slice(dst[n], (0,) * dst[n].ndim, (1,) * dst[n].ndim).reshape(1)
              for dst in (grad, delta, new_m, new_v) for n, _ in SMALL_EARLY]
    done.append(jnp.pad(jnp.concatenate(firsts), (0, 1024 - len(firsts))).reshape(8, 128))
    lands.update(zip(last_w, _scatter_wait(send_sems, recv_sems, last_src, last_land, last_views, done)))

    dmod = jnp.concatenate([sums1[1:4], sums2[1:4], sums3[1:4]], axis=0).reshape(-1)
    total_late, landed = _allreduce_small(_pack_small(dmod, {'g_ffn1': sums1[0]}, SMALL_LATE), lands[last_w[0]],
                                          "allreduce_late")
    dmod_cols = landed[:, :ADA_ROWS].reshape(NDEV, ADA_ROWS * 128)
    res = _adam_ada(W['w_ada'][0], M['w_ada'][0], V['w_ada'][0], sc_all, dmod_cols)
    grad['w_ada'], delta['w_ada'], new_m['w_ada'], new_v['w_ada'] = [r[None] for r in res]
    adam_small(SMALL_LATE, True, total_late, "adam_small_late", 0.0)
    for n in last_w:
        adam_sharded(n)

    return (loss, d0[None], *[grad[n] for n in WEIGHT_ORDER], *[delta[n] for n in WEIGHT_ORDER],
            *[new_m[n] for n in WEIGHT_ORDER], *[new_v[n] for n in WEIGHT_ORDER])
```

```python
import functools

import jax
import jax.numpy as jnp
from jax import lax
from jax.experimental import pallas as pl
from jax.experimental.pallas import tpu as pltpu

F32 = jnp.float32
MXU_DTYPE = jnp.bfloat16
WIRE_DTYPE = jnp.bfloat16
SAVE_DTYPE = jnp.bfloat16

NDEV = 8
D_MODEL = 1024
D_FF = 2816
FF_SHARD = 2 * D_FF // NDEV
POOL_WIDTH = 512
POOL_GROUP = 128
SSM_WIDTH = 512
SSM_STATE = 64
SSM_GROUP = 16
SSM_BLOCKS = 4
SSM_BLOCK_STATE = 512
N_STATE = 2048
IN_WIDTH = 3072
EPS = 1e-6
ADAM_LR = 0.001
ADAM_B1 = 0.9
ADAM_B2 = 0.999
ADAM_EPS = 1e-08
ADAM_WD = 0.01
ADAM_STEP = 10

TM_FFN = 512
FFN_BWD_CHUNK = 256
TM_MIX = 256
TM_MIX_BWD = 256
TM_EW = 512
SCAN_ROWS = 8
POOL_HALO = 16
VMEM_LIMIT = 60 * 1024 * 1024

ROW_G_FFN1, ROW_G_MIX, ROW_G_FFN2, ROW_G_FINAL = 9, 10, 11, 12
ROW_POOL_B, ROW_POOL_SCALE, ROW_SSM_D, ROW_B_GLU = 0, 1, 2, 3

SMALL_EARLY = (
    ("g_mix", 1024), ("g_ffn2", 1024), ("g_final", 1024), ("pool_w", 65536),
    ("pool_b", 512), ("pool_scale", 512), ("ssm_lam_re_log", 2048), ("ssm_lam_im", 2048),
    ("ssm_log_dt", 32), ("ssm_b_re", 32768), ("ssm_b_im", 32768), ("ssm_c_re", 32768),
    ("ssm_c_im", 32768), ("ssm_d", 512), ("b_glu", 1024),
)
SMALL_LATE = (("g_ffn1", 1024),)
ADA_ROWS = 9
MESH = pl.DeviceIdType.MESH


def _pack_rows(params, with_ada):
    rest = -(-sum(n for _, n in params) // (NDEV * 128))
    return rest, -(-(rest + (ADA_ROWS if with_ada else 0)) // 8) * 8


def _mm(a, b):
    return jnp.dot(a.astype(MXU_DTYPE), b.astype(MXU_DTYPE), preferred_element_type=F32)


def _mm_nt(a, b):
    return lax.dot_general(a.astype(MXU_DTYPE), b.astype(MXU_DTYPE), (((1,), (1,)), ((), ())),
                           preferred_element_type=F32)


def _mm_tn(a, b):
    return lax.dot_general(a.astype(MXU_DTYPE), b.astype(MXU_DTYPE), (((0,), (0,)), ((), ())),
                           preferred_element_type=F32)


def _rms_scale(x):
    return lax.rsqrt(jnp.mean(x * x, axis=-1, keepdims=True) + EPS)


def _sigmoid(x):
    return jax.nn.sigmoid(x)


def _colsum(x):
    return jnp.sum(x, axis=0, keepdims=True)


def _row(ref, r):
    return ref[r:r + 1, :]


def _params(*sem):
    return pltpu.CompilerParams(dimension_semantics=sem, vmem_limit_bytes=VMEM_LIMIT)


def _resident(a):
    return pl.BlockSpec(a.shape, lambda *_: (0,) * a.ndim, pipeline_mode=pl.Buffered(1))


def _me():
    return lax.axis_index("x"), lax.axis_index("y"), lax.axis_index("c")


def _peer(rel):
    x, y, c = _me()
    px = 1 - x if rel & 4 else x
    py = 1 - y if rel & 2 else y
    pc = 1 - c if rel & 1 else c
    return (px, py, pc), 4 * px + 2 * py + pc


_HBM = pl.BlockSpec(memory_space=pl.ANY)
_HBM_ONLY = pl.BlockSpec(memory_space=pltpu.HBM)


def _stacked(ref, p):
    return ref.at[p]


def _halves(ref, p):
    return ref.at[p // 4, p % 4]


class _Gather:
    def __init__(self, shards):
        self.operands = list(shards)
        self.n = len(shards)
        self.out_shape = [jax.ShapeDtypeStruct((NDEV,) + s.shape, s.dtype) for s in shards]
        self.scratch = [pltpu.SemaphoreType.DMA((7 * self.n,)), pltpu.SemaphoreType.DMA((7 * self.n,)),
                        pltpu.SemaphoreType.DMA((self.n,))]

    def plan(self, srcs, outs, sems):
        send_sems, recv_sems, local_sems = sems
        n = self.n
        x, y, c = _me()
        me = 4 * x + 2 * y + c
        here, sibling = (x, y, c), (x, y, 1 - c)
        chips = [(1 - x, y), (x, 1 - y), (1 - x, 1 - y)]

        def blk(px, py, pc):
            return 4 * px + 2 * py + pc

        def copy(a, k, block, to, src=None):
            return pltpu.make_async_remote_copy(
                src_ref=outs[a].at[block] if src is None else src, dst_ref=outs[a].at[block],
                send_sem=send_sems.at[7 * a + k], recv_sem=recv_sems.at[7 * a + k], device_id=to, device_id_type=MESH)

        def mine(a):
            return pltpu.make_async_copy(srcs[a], outs[a].at[me], local_sems.at[a])

        def first(a):
            return [copy(a, 0, me, sibling, src=srcs[a])] + [copy(a, 1 + j, me, (*chip, c), src=srcs[a])
                                                              for j, chip in enumerate(chips)]

        def start():
            for a in range(n):
                mine(a).start()
                for cp in first(a):
                    cp.start()

        def forward():
            for a in range(n):
                for j, chip in enumerate(chips):
                    copy(a, 1 + j, blk(*chip, c), here).wait_recv()
                    copy(a, 4 + j, blk(*chip, c), sibling).start()

        def finish():
            for a in range(n):
                copy(a, 0, blk(x, y, 1 - c), here).wait_recv()
                for j, chip in enumerate(chips):
                    copy(a, 4 + j, blk(*chip, 1 - c), here).wait_recv()
            for a in range(n):
                mine(a).wait()
                for cp in first(a):
                    cp.wait_send()
                for j, chip in enumerate(chips):
                    copy(a, 4 + j, blk(*chip, c), sibling).wait_send()

        return start, forward, finish


class _Scatter:
    def __init__(self, arrays, views, shard_shapes):
        self.operands = list(arrays)
        self.views = list(views)
        self.n = len(arrays)
        self.out_shape = [jax.ShapeDtypeStruct((NDEV,) + tuple(s), a.dtype) for s, a in zip(shard_shapes, arrays)]
        self.scratch = [pltpu.SemaphoreType.DMA((7 * self.n,)), pltpu.SemaphoreType.DMA((7 * self.n,)),
                        pltpu.SemaphoreType.DMA((self.n,))]

    def plan(self, srcs, outs, sems):
        send_sems, recv_sems, local_sems = sems
        n, views = self.n, self.views
        x, y, c = _me()
        me = 4 * x + 2 * y + c

        def mine(a):
            return pltpu.make_async_copy(views[a](srcs[a], me), outs[a].at[me], local_sems.at[a])

        def copy(a, rel, sending):
            to, p = _peer(rel)
            return pltpu.make_async_remote_copy(
                src_ref=views[a](srcs[a], p), dst_ref=outs[a].at[me if sending else p],
                send_sem=send_sems.at[7 * a + rel - 1], recv_sem=recv_sems.at[7 * a + rel - 1],
                device_id=to if sending else (x, y, c), device_id_type=MESH)

        def start():
            for a in range(n):
                mine(a).start()
            for rel in range(1, 8):
                for a in range(n):
                    copy(a, rel, True).start()

        def forward():
            pass

        def finish():
            for rel in range(1, 8):
                for a in range(n):
                    copy(a, rel, False).wait_recv()
            for rel in range(1, 8):
                for a in range(n):
                    copy(a, rel, True).wait_send()
            for a in range(n):
                mine(a).wait()

        return start, forward, finish


class _SmallAllReduce:
    def __init__(self, pack):
        rows = pack.shape[1]
        self.operands = [pack]
        self.n = 1
        self.out_shape = [jax.ShapeDtypeStruct(pack.shape, F32)]
        self.scratch = [pltpu.VMEM(pack.shape, F32), pltpu.VMEM((rows, 128), F32)] \
            + [pltpu.SemaphoreType.DMA((7,))] * 4 + [pltpu.SemaphoreType.DMA((2,))]

    def plan(self, srcs, outs, scratch):
        pack, total = srcs[0], outs[0]
        land, mine, send1, recv1, send2, recv2, local = scratch
        x, y, c = _me()
        me = 4 * x + 2 * y + c

        def slab(rel, sending):
            to, p = _peer(rel)
            return pltpu.make_async_remote_copy(
                src_ref=pack.at[p], dst_ref=land.at[me if sending else p], send_sem=send1.at[rel - 1],
                recv_sem=recv1.at[rel - 1], device_id=to if sending else (x, y, c), device_id_type=MESH)

        def summed(rel, sending):
            to, p = _peer(rel)
            return pltpu.make_async_remote_copy(
                src_ref=mine, dst_ref=total.at[me if sending else p], send_sem=send2.at[rel - 1],
                recv_sem=recv2.at[rel - 1], device_id=to if sending else (x, y, c), device_id_type=MESH)

        own_slab = pltpu.make_async_copy(pack.at[me], land.at[me], local.at[0])
        own_sum = pltpu.make_async_copy(mine, total.at[me], local.at[1])

        def start():
            own_slab.start()
            for rel in range(1, 8):
                slab(rel, True).start()

        def forward():
            own_slab.wait()
            for rel in range(1, 8):
                slab(rel, False).wait_recv()
            acc = land[0]
            for b in range(1, NDEV):
                acc = acc + land[b]
            mine[...] = acc
            own_sum.start()
            for rel in range(1, 8):
                summed(rel, True).start()

        def finish():
            for rel in range(1, 8):
                summed(rel, False).wait_recv()
            for rel in range(1, 8):
                slab(rel, True).wait_send()
                summed(rel, True).wait_send()
            own_sum.wait()

        return start, forward, finish


class _Carried:
    def __init__(self, *parts):
        self.parts = parts
        self.operands = [o for p in parts for o in p.operands]
        self.n = len(self.operands)
        self.out_shape = [s for p in parts for s in p.out_shape]
        self.scratch = [s for p in parts for s in p.scratch]

    def plan(self, srcs, outs, scratch):
        plans, a, b = [], 0, 0
        for p in self.parts:
            plans.append(p.plan(srcs[a:a + p.n], outs[a:a + p.n], scratch[b:b + len(p.scratch)]))
            a, b = a + p.n, b + len(p.scratch)

        def every(k):
            def run():
                for plan in plans:
                    plan[k]()
            return run

        return every(0), every(1), every(2)


def _launch(body, name, out_shape, in_specs, out_specs, operands, scratch=(), grid=None, semantics=None,
            carry=None, steps=None):
    out_shape, in_specs, out_specs = list(out_shape), list(in_specs), list(out_specs)
    operands, scratch = list(operands), list(scratch)
    n_in, n_out, n_scr = len(in_specs), len(out_shape), len(scratch)
    kernel_body = body
    if carry is not None:
        k = carry.n

        def kernel_body(*refs):
            ins, cin = refs[:n_in], refs[n_in:n_in + k]
            outs, cout = refs[n_in + k:n_in + k + n_out], refs[n_in + k + n_out:n_in + 2 * k + n_out]
            rest = refs[n_in + 2 * k + n_out:]
            scr, csem = rest[:n_scr], rest[n_scr:]
            start, forward, finish = carry.plan(cin, cout, csem)
            if steps is None:
                start()
                body(*ins, *outs, *scr)
                forward()
                finish()
            else:
                pl.when(steps()[0])(start)
                pl.when(steps()[1])(forward)
                body(*ins, *outs, *scr)
                pl.when(steps()[2])(finish)

        in_specs += [_HBM] * k
        out_shape += carry.out_shape
        out_specs += [_HBM] * k
        operands += carry.operands
        scratch += carry.scratch
    kwargs = {} if grid is None else {"grid": grid}
    params = pltpu.CompilerParams(vmem_limit_bytes=VMEM_LIMIT) if semantics is None else _params(*semantics)
    return pl.pallas_call(kernel_body, name=name, out_shape=out_shape, in_specs=in_specs, out_specs=out_specs,
                          scratch_shapes=scratch, compiler_params=params, **kwargs)(*operands)


def _grid_steps(nt):
    def steps():
        i = pl.program_id(0)
        return i == 0, i == nt - 1, i == nt - 1
    return steps


def _cast_shards(shards):
    n = len(shards)

    def body(*refs):
        for a in range(n):
            refs[n + a][...] = refs[a][...].astype(WIRE_DTYPE)

    return pl.pallas_call(body, name="cast_shards",
                          out_shape=[jax.ShapeDtypeStruct(s.shape, WIRE_DTYPE) for s in shards],
                          compiler_params=pltpu.CompilerParams(vmem_limit_bytes=VMEM_LIMIT))(*shards)


_SEM = pl.BlockSpec(memory_space=pltpu.SEMAPHORE)
_DATAFLOW = pltpu.SideEffectType.DATAFLOW_SIDE_EFFECTING


def _split_copy(arrays, views, landing, send_sems, recv_sems, a, rel):
    to, p = _peer(rel)
    x, y, c = _me()
    return pltpu.make_async_remote_copy(
        src_ref=views[a](arrays[a], p), dst_ref=landing[a].at[4 * x + 2 * y + c],
        send_sem=send_sems.at[NDEV * a + rel], recv_sem=recv_sems.at[NDEV * a + rel], device_id=to, device_id_type=MESH)


def _scatter_start(arrays, views, shard_shapes, after):
    n = len(arrays)
    landing = [pltpu.with_memory_space_constraint(lax.empty((NDEV,) + tuple(s), a.dtype), pltpu.HBM)
               for s, a in zip(shard_shapes, arrays)]
    arrays = [pltpu.with_memory_space_constraint(a, pltpu.HBM) for a in arrays]

    def body(*refs):
        ins, land = refs[:n], refs[n:2 * n]
        send_sems, recv_sems = refs[2 * n + len(after)], refs[2 * n + len(after) + 1]
        token = refs[-1]
        for rel in range(NDEV):
            for a in range(n):
                _split_copy(ins, views, land, send_sems, recv_sems, a, rel).start()
        token[...] = jnp.zeros_like(token)

    res = pl.pallas_call(
        body, name="scatter_start",
        out_shape=[pltpu.SemaphoreType.DMA((NDEV * n,)), pltpu.SemaphoreType.DMA((NDEV * n,))]
        + [pltpu.HBM(a.shape, a.dtype) for a in arrays] + [pltpu.HBM(l.shape, l.dtype) for l in landing]
        + [jax.ShapeDtypeStruct((8, 128), F32)],
        in_specs=[_HBM_ONLY] * (2 * n) + [_HBM] * len(after),
        out_specs=[_SEM, _SEM] + [_HBM_ONLY] * (2 * n) + [pl.BlockSpec(memory_space=pltpu.VMEM)],
        input_output_aliases={i: 2 + i for i in range(2 * n)},
        compiler_params=pltpu.CompilerParams(has_side_effects=_DATAFLOW),
    )(*arrays, *landing, *after)
    return res[0], res[1], res[2:2 + n], res[2 + n:2 + 2 * n], res[-1]


def _scatter_wait(send_sems, recv_sems, arrays, landing, views, after):
    n = len(arrays)

    def body(*refs):
        ins, land = refs[:n], refs[n:2 * n]
        send, recv = refs[2 * n], refs[2 * n + 1]
        for rel in range(NDEV):
            for a in range(n):
                cp = _split_copy(ins, views, land, send, recv, a, rel)
                cp.wait_send()
                cp.wait_recv()

    res = pl.pallas_call(
        body, name="scatter_wait",
        out_shape=[pltpu.HBM(a.shape, a.dtype) for a in arrays] + [pltpu.HBM(l.shape, l.dtype) for l in landing],
        in_specs=[_HBM_ONLY] * (2 * n) + [_SEM, _SEM] + [_HBM] * len(after),
        out_specs=[_HBM_ONLY] * (2 * n),
        input_output_aliases={i: i for i in range(2 * n)},
        compiler_params=pltpu.CompilerParams(has_side_effects=_DATAFLOW),
    )(*arrays, *landing, send_sems, recv_sems, *after)
    return res[n:]


def _ada_forward(c_row, w_ada, b_ada8, carry):
    cols = w_ada.shape[1]

    def body(c_ref, w_ref, b_ref, mod_ref, sc_ref, c_all, send_buf, recv_buf, send1, recv1, send2, recv2):
        x, y, c = _me()
        me = 4 * x + 2 * y + c
        rowi = lax.broadcasted_iota(jnp.int32, (8, D_MODEL), 0)
        c_all[me] = jnp.broadcast_to(c_ref[...], (8, D_MODEL))
        copies = []
        for rel in range(1, 8):
            to, _ = _peer(rel)
            cp = pltpu.make_async_remote_copy(src_ref=c_all.at[me], dst_ref=c_all.at[me], send_sem=send1.at[rel - 1],
                                              recv_sem=recv1.at[rel - 1], device_id=to, device_id_type=MESH)
            cp.start()
            copies.append(cp)
        for rel in range(1, 8):
            _, p = _peer(rel)
            pltpu.make_async_remote_copy(src_ref=c_all.at[p], dst_ref=c_all.at[p], send_sem=send1.at[rel - 1],
                                         recv_sem=recv1.at[rel - 1], device_id=(x, y, c), device_id_type=MESH).wait_recv()
        for cp in copies:
            cp.wait_send()
        cmat = jnp.zeros((8, D_MODEL), F32)
        for b in range(8):
            cmat = jnp.where(rowi == b, c_all[b], cmat)
        sc = cmat * _sigmoid(cmat)
        sc_ref[...] = sc
        modcols = _mm(sc, w_ref[...]) + b_ref[pl.ds(me, 1), :]
        for b in range(8):
            send_buf[b] = jnp.broadcast_to(modcols[b:b + 1, :], (8, cols))
        recv_buf[me] = send_buf[me]
        copies = []
        for rel in range(1, 8):
            to, p = _peer(rel)
            cp = pltpu.make_async_remote_copy(src_ref=send_buf.at[p], dst_ref=recv_buf.at[me], send_sem=send2.at[rel - 1],
                                              recv_sem=recv2.at[rel - 1], device_id=to, device_id_type=MESH)
            cp.start()
            copies.append(cp)
        for rel in range(1, 8):
            _, p = _peer(rel)
            pltpu.make_async_remote_copy(src_ref=send_buf.at[p], dst_ref=recv_buf.at[p], send_sem=send2.at[rel - 1],
                                         recv_sem=recv2.at[rel - 1], device_id=(x, y, c), device_id_type=MESH).wait_recv()
        for cp in copies:
            cp.wait_send()
        rowc = lax.broadcasted_iota(jnp.int32, (8, cols), 0)
        out = jnp.zeros((8, cols), F32)
        for k in range(8):
            out = jnp.where(rowc == k, recv_buf[k], out)
        mod_ref[...] = out

    return _launch(
        body, "ada_forward",
        out_shape=[jax.ShapeDtypeStruct((8, cols), F32), jax.ShapeDtypeStruct((8, D_MODEL), F32)],
        in_specs=[pl.BlockSpec(memory_space=pltpu.VMEM)] * 3,
        out_specs=[pl.BlockSpec(memory_space=pltpu.VMEM)] * 2,
        operands=(c_row, w_ada, b_ada8),
        scratch=[pltpu.VMEM((8, 8, D_MODEL), F32), pltpu.VMEM((8, 8, cols), F32), pltpu.VMEM((8, 8, cols), F32)]
        + [pltpu.SemaphoreType.DMA((7,))] * 4,
        carry=carry)


def _allreduce_small(pack, order, name):
    rows = pack.shape[1]

    def body(pack_ref, order_ref, total_ref, land_ref, send1, recv1, send2, recv2):
        x, y, c = _me()
        me = 4 * x + 2 * y + c
        land_ref[me] = pack_ref[me]
        copies = []
        for rel in range(1, 8):
            to, p = _peer(rel)
            cp = pltpu.make_async_remote_copy(src_ref=pack_ref.at[p], dst_ref=land_ref.at[me], send_sem=send1.at[rel - 1],
                                              recv_sem=recv1.at[rel - 1], device_id=to, device_id_type=MESH)
            cp.start()
            copies.append(cp)
        for rel in range(1, 8):
            _, p = _peer(rel)
            pltpu.make_async_remote_copy(src_ref=pack_ref.at[p], dst_ref=land_ref.at[p], send_sem=send1.at[rel - 1],
                                         recv_sem=recv1.at[rel - 1], device_id=(x, y, c), device_id_type=MESH).wait_recv()
        for cp in copies:
            cp.wait_send()
        acc = land_ref[0]
        for b in range(1, 8):
            acc = acc + land_ref[b]
        total_ref[me] = acc
        copies = []
        for rel in range(1, 8):
            to, _ = _peer(rel)
            cp = pltpu.make_async_remote_copy(src_ref=total_ref.at[me], dst_ref=total_ref.at[me], send_sem=send2.at[rel - 1],
                                              recv_sem=recv2.at[rel - 1], device_id=to, device_id_type=MESH)
            cp.start()
            copies.append(cp)
        for rel in range(1, 8):
            _, p = _peer(rel)
            pltpu.make_async_remote_copy(src_ref=total_ref.at[p], dst_ref=total_ref.at[p], send_sem=send2.at[rel - 1],
                                         recv_sem=recv2.at[rel - 1], device_id=(x, y, c), device_id_type=MESH).wait_recv()
        for cp in copies:
            cp.wait_send()

    return pl.pallas_call(
        body, name=name,
        out_shape=[jax.ShapeDtypeStruct((8, rows, 128), F32), jax.ShapeDtypeStruct((8, rows, 128), F32)],
        in_specs=[pl.BlockSpec(memory_space=pltpu.VMEM), _HBM],
        out_specs=[pl.BlockSpec(memory_space=pltpu.VMEM)] * 2,
        scratch_shapes=[pltpu.SemaphoreType.DMA((7,))] * 4,
        compiler_params=pltpu.CompilerParams(vmem_limit_bytes=VMEM_LIMIT),
    )(pack, order)


def _modulated(x, prm_ref, sub, g_row):
    shift, scale = _row(prm_ref, 3 * sub), _row(prm_ref, 3 * sub + 1)
    g = _row(prm_ref, g_row)
    r = _rms_scale(x)
    n0 = x * r
    return (n0 * g) * (1.0 + scale) + shift, r, n0


def _swiglu_tile(xv, prm_ref, win_ref, wout_ref, ab_ref, sub, g_row):
    h, _, _ = _modulated(xv, prm_ref, sub, g_row)
    hb = h.astype(MXU_DTYPE)
    acc = None
    for j in range(4):
        a = _mm_nt(hb, win_ref[0, j])
        b = _mm_nt(hb, win_ref[1, j])
        ab_ref[0, j] = a.astype(SAVE_DTYPE)
        ab_ref[1, j] = b.astype(SAVE_DTYPE)
        t = _mm((a * _sigmoid(a)) * b, wout_ref[j])
        acc = t if acc is None else acc + t
    return acc


def _loss_tile(xv, target, g):
    r = _rms_scale(xv)
    n0 = xv * r
    err = n0 * g - target
    dy = err / float(D_MODEL)
    dn0 = dy * g
    dx = r * (dn0 - n0 * jnp.mean(dn0 * n0, axis=-1, keepdims=True))
    loss = 0.5 * jnp.sum(jnp.mean(err * err, axis=-1, keepdims=True), axis=0, keepdims=True)
    return dx, _colsum(dy * n0), loss


def _ffn_forward_loss(x, target, prm, win, wout, sub, g_row, name):
    T = x.shape[0]
    tm = min(T, TM_FFN)

    def body(x_ref, t_ref, prm_ref, win_ref, wout_ref, dx_ref, sums_ref, f_ref, ab_ref):
        i = pl.program_id(0)
        xv = x_ref[...]
        acc = _swiglu_tile(xv, prm_ref, win_ref, wout_ref, ab_ref, sub, g_row)
        f_ref[...] = acc.astype(SAVE_DTYPE)
        dx, dg, loss = _loss_tile(xv + (0.5 * _row(prm_ref, 3 * sub + 2)) * acc, t_ref[...], _row(prm_ref, ROW_G_FINAL))
        dx_ref[...] = dx
        upd = jnp.concatenate([dg, jnp.broadcast_to(loss, (1, D_MODEL)), jnp.zeros((6, D_MODEL), F32)], axis=0)

        @pl.when(i == 0)
        def _():
            sums_ref[...] = upd

        @pl.when(i > 0)
        def _():
            sums_ref[...] += upd

    tok = pl.BlockSpec((tm, D_MODEL), lambda i: (i, 0))
    return _launch(
        body, name, grid=(T // tm,), semantics=("arbitrary",),
        out_shape=[jax.ShapeDtypeStruct((T, D_MODEL), F32), jax.ShapeDtypeStruct((8, D_MODEL), F32),
                   jax.ShapeDtypeStruct((T, D_MODEL), SAVE_DTYPE), jax.ShapeDtypeStruct((2, 4, T, FF_SHARD), SAVE_DTYPE)],
        in_specs=[tok, tok, _resident(prm), _resident(win), _resident(wout)],
        out_specs=[tok, pl.BlockSpec((8, D_MODEL), lambda i: (0, 0)), tok,
                   pl.BlockSpec((2, 4, tm, FF_SHARD), lambda i: (0, 0, i, 0))],
        operands=(x, target, prm, win, wout))


def _ffn_hidden(x, prm, win, sub, g_row, name, carry=None):
    T = x.shape[0]
    tm = min(T, TM_FFN)

    def body(x_ref, prm_ref, win_ref, ab_ref, s_ref):
        h, _, _ = _modulated(x_ref[...], prm_ref, sub, g_row)
        hb = h.astype(MXU_DTYPE)
        for j in range(4):
            a = _mm_nt(hb, win_ref[0, j])
            b = _mm_nt(hb, win_ref[1, j])
            ab_ref[0, j] = a.astype(SAVE_DTYPE)
            ab_ref[1, j] = b.astype(SAVE_DTYPE)
            s_ref[j] = ((a * _sigmoid(a)) * b).astype(MXU_DTYPE)

    return _launch(
        body, name, grid=(T // tm,), semantics=("arbitrary",),
        out_shape=[jax.ShapeDtypeStruct((2, 4, T, FF_SHARD), SAVE_DTYPE), jax.ShapeDtypeStruct((4, T, FF_SHARD), MXU_DTYPE)],
        in_specs=[pl.BlockSpec((tm, D_MODEL), lambda i: (i, 0)), _resident(prm), _resident(win)],
        out_specs=[pl.BlockSpec((2, 4, tm, FF_SHARD), lambda i: (0, 0, i, 0)),
                   pl.BlockSpec((4, tm, FF_SHARD), lambda i: (0, i, 0))],
        operands=(x, prm, win), carry=carry, steps=_grid_steps(T // tm))


def _ffn_out(x, s, prm, wout, sub, name, carry=None):
    T = x.shape[0]
    tm = min(T, TM_FFN)

    def body(x_ref, s_ref, prm_ref, wout_ref, xo_ref, f_ref):
        acc = None
        for j in range(4):
            t = _mm(s_ref[j], wout_ref[j])
            acc = t if acc is None else acc + t
        f_ref[...] = acc.astype(SAVE_DTYPE)
        xo_ref[...] = x_ref[...] + (0.5 * _row(prm_ref, 3 * sub + 2)) * acc

    tok = pl.BlockSpec((tm, D_MODEL), lambda i: (i, 0))
    return _launch(
        body, name, grid=(T // tm,), semantics=("arbitrary",),
        out_shape=[jax.ShapeDtypeStruct((T, D_MODEL), F32), jax.ShapeDtypeStruct((T, D_MODEL), SAVE_DTYPE)],
        in_specs=[tok, pl.BlockSpec((4, tm, FF_SHARD), lambda i: (0, i, 0)), _resident(prm), _resident(wout)],
        out_specs=[tok, tok], operands=(x, s, prm, wout), carry=carry, steps=_grid_steps(T // tm))


def _ffn_backward(x, d, ab, prm, win, wout, order, sub, g_row, name, carry=None, defer_dwin=False):
    T = x.shape[0]
    tm = min(T, TM_FFN)
    nt = T // tm
    chunk = min(tm, FFN_BWD_CHUNK)

    def body(x_ref, d_ref, ab_ref, prm_ref, win_ref, wout_ref, order_ref, dh_ref, second_ref, dwout_ref, *scratch):
        i = pl.program_id(1)
        acc_out = scratch[-1]
        acc_in = None if defer_dwin else scratch[0]

        @pl.when(i == 0)
        def _():
            if not defer_dwin:
                acc_in[...] = jnp.zeros_like(acc_in)
            acc_out[...] = jnp.zeros_like(acc_out)

        wa, wb, wo = win_ref[0, 0], win_ref[1, 0], wout_ref[0]
        half_gate = 0.5 * _row(prm_ref, 3 * sub + 2)
        das, dbs, ss, hbs, dfss = [], [], [], [], []
        for ck in range(tm // chunk):
            rows = slice(ck * chunk, (ck + 1) * chunk)
            if not defer_dwin:
                hbs.append(_modulated(x_ref[rows, :], prm_ref, sub, g_row)[0].astype(MXU_DTYPE))
            a = ab_ref[0, 0, rows, :].astype(F32)
            b = ab_ref[1, 0, rows, :].astype(F32)
            sg = _sigmoid(a)
            si = a * sg
            dfs = (half_gate * d_ref[rows, :]).astype(MXU_DTYPE)
            ds = _mm_nt(dfs, wo)
            da = (ds * b * (sg * (1.0 + a * (1.0 - sg)))).astype(MXU_DTYPE)
            db = (ds * si).astype(MXU_DTYPE)
            dh_ref[0, rows, :] = (_mm(da, wa) + _mm(db, wb)).astype(SAVE_DTYPE)
            if defer_dwin:
                second_ref[0, 0, rows, :] = da
                second_ref[1, 0, rows, :] = db
            das.append(da)
            dbs.append(db)
            ss.append((si * b).astype(MXU_DTYPE))
            dfss.append(dfs)
        cat = (lambda v: v[0]) if len(das) == 1 else (lambda v: jnp.concatenate(v, axis=0))
        acc_out[...] += _mm_tn(cat(ss), cat(dfss))
        if not defer_dwin:
            hb = cat(hbs)
            acc_in[0] += _mm_tn(cat(das), hb)
            acc_in[1] += _mm_tn(cat(dbs), hb)

        @pl.when(i == nt - 1)
        def _():
            if not defer_dwin:
                second_ref[0, 0] = acc_in[0].astype(WIRE_DTYPE)
                second_ref[1, 0] = acc_in[1].astype(WIRE_DTYPE)
            dwout_ref[0] = acc_out[...].astype(WIRE_DTYPE)

    def steps():
        j, i = pl.program_id(0), pl.program_id(1)
        return (j == 0) & (i == 0), (j == 2) & (i == 0), (j == 3) & (i == nt - 1)

    tok = pl.BlockSpec((tm, D_MODEL), lambda j, i: (i, 0))
    pre = pl.BlockSpec((2, 1, tm, FF_SHARD), lambda j, i: (0, j, i, 0))
    wblock = pl.BlockSpec((2, 1, FF_SHARD, D_MODEL), lambda j, i: (0, j, 0, 0))
    second = jax.ShapeDtypeStruct(ab.shape, MXU_DTYPE) if defer_dwin else jax.ShapeDtypeStruct(win.shape, WIRE_DTYPE)
    return _launch(
        body, name, grid=(4, nt), semantics=("arbitrary", "arbitrary"),
        out_shape=[jax.ShapeDtypeStruct((4, T, D_MODEL), SAVE_DTYPE), second, jax.ShapeDtypeStruct(wout.shape, WIRE_DTYPE)],
        in_specs=[tok, tok, pre, _resident(prm), wblock, pl.BlockSpec((1, FF_SHARD, D_MODEL), lambda j, i: (j, 0, 0)), _HBM],
        out_specs=[pl.BlockSpec((1, tm, D_MODEL), lambda j, i: (j, i, 0)), pre if defer_dwin else wblock,
                   pl.BlockSpec((1, FF_SHARD, D_MODEL), lambda j, i: (j, 0, 0))],
        operands=(x, d, ab, prm, win, wout, order),
        scratch=([] if defer_dwin else [pltpu.VMEM((2, FF_SHARD, D_MODEL), F32)]) + [pltpu.VMEM((FF_SHARD, D_MODEL), F32)],
        carry=carry, steps=steps)


def _ffn_dwin(x, dab, prm, sub, g_row, name, carry=None):
    T = x.shape[0]
    tm = min(T, TM_FFN)
    nt = T // tm

    def body(x_ref, dab_ref, prm_ref, dwin_ref, acc):
        i = pl.program_id(1)

        @pl.when(i == 0)
        def _():
            acc[...] = jnp.zeros_like(acc)

        h, _, _ = _modulated(x_ref[...], prm_ref, sub, g_row)
        hb = h.astype(MXU_DTYPE)
        acc[0] += _mm_tn(dab_ref[0, 0], hb)
        acc[1] += _mm_tn(dab_ref[1, 0], hb)

        @pl.when(i == nt - 1)
        def _():
            dwin_ref[0, 0] = acc[0].astype(WIRE_DTYPE)
            dwin_ref[1, 0] = acc[1].astype(WIRE_DTYPE)

    def steps():
        j, i = pl.program_id(0), pl.program_id(1)
        return (j == 0) & (i == 0), (j == 2) & (i == 0), (j == 3) & (i == nt - 1)

    return _launch(
        body, name, grid=(4, nt), semantics=("arbitrary", "arbitrary"),
        out_shape=[jax.ShapeDtypeStruct((2, 4, FF_SHARD, D_MODEL), WIRE_DTYPE)],
        in_specs=[pl.BlockSpec((tm, D_MODEL), lambda j, i: (i, 0)),
                  pl.BlockSpec((2, 1, tm, FF_SHARD), lambda j, i: (0, j, i, 0)), _resident(prm)],
        out_specs=[pl.BlockSpec((2, 1, FF_SHARD, D_MODEL), lambda j, i: (0, j, 0, 0))],
        operands=(x, dab, prm), scratch=[pltpu.VMEM((2, FF_SHARD, D_MODEL), F32)], carry=carry, steps=steps)


def _norm_backward_tile(dh, xv, dv, fv, prm_ref, sub, g_row, gate_coef):
    scale, g = _row(prm_ref, 3 * sub + 1), _row(prm_ref, g_row)
    r = _rms_scale(xv)
    n0 = xv * r
    dn = dh * (1.0 + scale)
    dn0 = dn * g
    dx = dv + r * (dn0 - n0 * jnp.mean(dn0 * n0, axis=-1, keepdims=True))
    upd = jnp.concatenate([_colsum(dn * n0), _colsum(dh), _colsum(dh * (n0 * g)),
                           gate_coef * _colsum(dv * fv.astype(F32)), jnp.zeros((4, D_MODEL), F32)], axis=0)
    return dx, upd


def _norm_backward(parts, x, d, f, prm, sub, g_row, gate_coef, name, carry=None):
    T = x.shape[0]
    tm = min(T, TM_EW)
    P = parts.shape[0]

    def body(p_ref, x_ref, d_ref, f_ref, prm_ref, dx_ref, sums_ref):
        i = pl.program_id(0)
        dh = p_ref[0].astype(F32)
        for k in range(1, P):
            dh = dh + p_ref[k].astype(F32)
        dx_ref[...], upd = _norm_backward_tile(dh, x_ref[...], d_ref[...], f_ref[...], prm_ref, sub, g_row, gate_coef)

        @pl.when(i == 0)
        def _():
            sums_ref[...] = upd

        @pl.when(i > 0)
        def _():
            sums_ref[...] += upd

    tok = pl.BlockSpec((tm, D_MODEL), lambda i: (i, 0))
    return _launch(
        body, name, grid=(T // tm,), semantics=("arbitrary",),
        out_shape=[jax.ShapeDtypeStruct((T, D_MODEL), F32), jax.ShapeDtypeStruct((8, D_MODEL), F32)],
        in_specs=[pl.BlockSpec((P, tm, D_MODEL), lambda i: (0, i, 0)), tok, tok, tok, _resident(prm)],
        out_specs=[tok, pl.BlockSpec((8, D_MODEL), lambda i: (0, 0))],
        operands=(parts, x, d, f, prm), carry=carry, steps=_grid_steps(T // tm))


def _ssm_discretise(lam_re_log, lam_im, log_dt):
    lr = -jnp.exp(lam_re_log)
    dt = jnp.exp(log_dt)
    mag = jnp.exp(lr * dt)
    ang = lam_im * dt
    ab_re = mag * jnp.cos(ang)
    ab_im = mag * jnp.sin(ang)
    num_re = ab_re - 1.0
    num_im = ab_im
    den = lr * lr + lam_im * lam_im
    f_re = (num_re * lr + num_im * lam_im) / den
    f_im = (num_im * lr - num_re * lam_im) / den
    return ab_re, ab_im, f_re, f_im


def _ssm_params_forward(lam_re_log, lam_im, log_dt):
    def body(a_ref, b_ref, c_ref, o0, o1, o2, o3):
        outs = _ssm_discretise(a_ref[...], b_ref[...], c_ref[...])
        for o, v in zip((o0, o1, o2, o3), outs):
            o[...] = v

    return pl.pallas_call(body, name="ssm_params_forward",
                          out_shape=[jax.ShapeDtypeStruct(lam_im.shape, F32)] * 4)(lam_re_log, lam_im, log_dt)


def _ssm_params_backward(lam_re_log, lam_im, log_dt, cot):
    def body(a_ref, b_ref, c_ref, g0, g1, g2, g3, o0, o1, o2):
        _, vjp = jax.vjp(_ssm_discretise, a_ref[...], b_ref[...], c_ref[...])
        d0, d1, d2 = vjp((g0[...], g1[...], g2[...], g3[...]))
        o0[...] = d0
        o1[...] = d1
        o2[...] = d2

    return pl.pallas_call(
        body, name="ssm_params_backward",
        out_shape=[jax.ShapeDtypeStruct(lam_im.shape, F32), jax.ShapeDtypeStruct(lam_im.shape, F32),
                   jax.ShapeDtypeStruct(log_dt.shape, F32)])(lam_re_log, lam_im, log_dt, *cot)


def _ssm_dense_forward(srow, b_dense, c_dense):
    def body(srow_ref, bd_ref, cd_ref, bb_ref, ct_ref):
        for j in range(SSM_BLOCKS):
            lanes = slice(j * SSM_BLOCK_STATE, (j + 1) * SSM_BLOCK_STATE)
            f_re, f_im = srow_ref[2:3, lanes], srow_ref[3:4, lanes]
            bb_ref[0, j] = (f_re * bd_ref[0, j] - f_im * bd_ref[1, j]).astype(MXU_DTYPE)
            bb_ref[1, j] = (f_re * bd_ref[1, j] + f_im * bd_ref[0, j]).astype(MXU_DTYPE)
            ct_ref[0, j] = cd_ref[0, j].astype(MXU_DTYPE)
            ct_ref[1, j] = cd_ref[1, j].astype(MXU_DTYPE)

    return pl.pallas_call(body, name="ssm_dense_forward",
                          out_shape=[jax.ShapeDtypeStruct(b_dense.shape, MXU_DTYPE),
                                     jax.ShapeDtypeStruct(c_dense.shape, MXU_DTYPE)],
                          compiler_params=pltpu.CompilerParams(vmem_limit_bytes=VMEM_LIMIT))(srow, b_dense, c_dense)


def _cmul(p, q):
    return p[0] * q[0] - p[1] * q[1], p[0] * q[1] + p[1] * q[0]


def _scan_coefficients(ar, ai, reverse):
    n = ar.shape[1]
    p = {1: (ar, ai)}
    p[2] = _cmul(p[1], p[1])
    p[3] = _cmul(p[2], p[1])
    p[4] = _cmul(p[2], p[2])
    p[5] = _cmul(p[4], p[1])
    p[6] = _cmul(p[4], p[2])
    p[7] = _cmul(p[4], p[3])
    p[8] = _cmul(p[4], p[4])
    rowi = lax.broadcasted_iota(jnp.int32, (SCAN_ROWS, n), 0)
    tiles = []
    for dstep in (1, 2, 4):
        keep = (rowi < SCAN_ROWS - dstep) if reverse else (rowi >= dstep)
        for part in p[dstep]:
            tiles.append(jnp.where(keep, jnp.broadcast_to(part, (SCAN_ROWS, n)), 0.0))
    for comp in (0, 1):
        t = jnp.zeros((SCAN_ROWS, n), F32)
        for rr in range(SCAN_ROWS):
            power = SCAN_ROWS - rr if reverse else rr + 1
            t = jnp.where(rowi == rr, jnp.broadcast_to(p[power][comp], (SCAN_ROWS, n)), t)
        tiles.append(t)
    return tiles


def _load_stack(stack_hbm, dst, sems, base):
    cols = stack_hbm.shape[2]
    cps = [pltpu.make_async_copy(stack_hbm.at[k], dst.at[:, pl.ds(k * cols, cols)], sems.at[base + k])
           for k in range(NDEV)]
    for cp in cps:
        cp.start()
    return cps


def _window_lanes():
    lane = lax.broadcasted_iota(jnp.int32, (1, POOL_WIDTH), 1)
    return jnp.where(lane < 128, 2.0, jnp.where(lane < 256, 4.0, jnp.where(lane < 384, 8.0, 16.0)))


def _gelu(y):
    return 0.5 * y * (1.0 + lax.erf(y * 0.7071067811865476))


def _gelu_grad(y):
    return 0.5 * (1.0 + lax.erf(y * 0.7071067811865476)) + y * jnp.exp(-0.5 * y * y) * 0.3989422804014327


def _mixer_forward(x, prm, w_in_s, w_pu_s, w_glu_s, w_su_s, w_out, pool_w, mvec, srow, bb, ct, carry=None):
    T = x.shape[0]
    tm = min(T, TM_MIX)
    nt = T // tm
    n_tiles = tm // SCAN_ROWS

    def body(x_ref, prm_ref, w_in_h, w_pu_h, w_glu_h, w_su_h, w_out_h, pw_ref, mv_ref, srow_ref, bb, ct,
             x2_ref, mo_ref, z_ref, sre_ref, sim_ref, zp_ref, q_ref, yp_ref, yss_ref, vg_ref, ys_ref,
             w_in, w_pu, w_glu, w_su, w_o, coef, carry, hist, bu, sems):
        i = pl.program_id(0)

        @pl.when(i == 0)
        def _():
            cps = (_load_stack(w_in_h, w_in, sems, 0) + _load_stack(w_pu_h, w_pu, sems, 8)
                   + _load_stack(w_glu_h, w_glu, sems, 16) + _load_stack(w_su_h, w_su, sems, 24))
            cps.append(pltpu.make_async_copy(w_out_h, w_o, sems.at[32]))
            cps[-1].start()
            for j in range(SSM_BLOCKS):
                lanes = slice(j * SSM_BLOCK_STATE, (j + 1) * SSM_BLOCK_STATE)
                for k, t in enumerate(_scan_coefficients(srow_ref[0:1, lanes], srow_ref[1:2, lanes], False)):
                    coef[j, k] = t
            carry[...] = jnp.zeros_like(carry)
            hist[...] = jnp.zeros_like(hist)
            for cp in cps:
                cp.wait()

        xv = x_ref[...]
        h, _, _ = _modulated(xv, prm_ref, 1, ROW_G_MIX)
        z = _mm(h, w_in[...])
        z_ref[...] = z.astype(SAVE_DTYPE)
        u_pool, u_ssm = z[:, 0:512], z[:, 512:1024]
        gl_pool, gl_ssm = z[:, 1024:2048], z[:, 2048:3072]

        ext = jnp.concatenate([hist[...], u_pool], axis=0)
        w2 = ext + pltpu.roll(ext, 1, 0)
        w4 = w2[:, 128:] + pltpu.roll(w2[:, 128:], 2, 0)
        w8 = w4[:, 128:] + pltpu.roll(w4[:, 128:], 4, 0)
        w16 = w8[:, 128:] + pltpu.roll(w8[:, 128:], 8, 0)
        wsum = jnp.concatenate([w2[POOL_HALO:, :128], w4[POOL_HALO:, :128], w8[POOL_HALO:, :128], w16[POOL_HALO:]], axis=1)
        hist[...] = u_pool[tm - POOL_HALO:, :]
        t1 = (lax.broadcasted_iota(jnp.int32, (tm, 1), 0) + (i * tm + 1)).astype(F32)
        zp = wsum / jnp.minimum(t1, _window_lanes()) - u_pool
        zp_ref[...] = zp.astype(SAVE_DTYPE)
        q = jnp.concatenate([_mm(zp[:, k * 128:(k + 1) * 128], pw_ref[k]) for k in range(4)], axis=1)
        q = q + mv_ref[ROW_POOL_B:ROW_POOL_B + 1, 0:512]
        q_ref[...] = q.astype(SAVE_DTYPE)
        y_pool = _mm(q * mv_ref[ROW_POOL_SCALE:ROW_POOL_SCALE + 1, 0:512], w_pu[...])
        yp_ref[...] = y_pool.astype(SAVE_DTYPE)

        y_blocks = []
        for j in range(SSM_BLOCKS):
            lanes = pl.ds(j * SSM_BLOCK_STATE, SSM_BLOCK_STATE)
            ub = u_ssm[:, j * 128:(j + 1) * 128].astype(MXU_DTYPE)
            bu[0] = _mm(ub, bb[0, j])
            bu[1] = _mm(ub, bb[1, j])
            a1r, a1i, a2r, a2i, a4r, a4i, pr, pi = [coef[j, k] for k in range(8)]

            def step(tt, c, lanes=lanes, a1r=a1r, a1i=a1i, a2r=a2r, a2i=a2i, a4r=a4r, a4i=a4i, pr=pr, pi=pi):
                cr, ci = c
                rows = pl.ds(pl.multiple_of(tt * SCAN_ROWS, SCAN_ROWS), SCAN_ROWS)
                xr, xi = bu[0, rows, :], bu[1, rows, :]
                for dstep, kr, ki in ((1, a1r, a1i), (2, a2r, a2i), (4, a4r, a4i)):
                    sr, si = pltpu.roll(xr, dstep, 0), pltpu.roll(xi, dstep, 0)
                    xr, xi = xr + kr * sr - ki * si, xi + kr * si + ki * sr
                xr, xi = xr + pr * cr - pi * ci, xi + pr * ci + pi * cr
                sre_ref[rows, lanes] = xr
                sim_ref[rows, lanes] = xi
                return (jnp.broadcast_to(xr[SCAN_ROWS - 1:SCAN_ROWS, :], xr.shape),
                        jnp.broadcast_to(xi[SCAN_ROWS - 1:SCAN_ROWS, :], xi.shape))

            cr, ci = lax.fori_loop(0, n_tiles, step, (carry[j, 0], carry[j, 1]))
            carry[j, 0] = cr
            carry[j, 1] = ci
            y_blocks.append(_mm(sre_ref[:, lanes], ct[0, j]) - _mm(sim_ref[:, lanes], ct[1, j]))
        yss = jnp.concatenate(y_blocks, axis=1) + mv_ref[ROW_SSM_D:ROW_SSM_D + 1, 0:512] * u_ssm
        yss_ref[...] = yss.astype(SAVE_DTYPE)
        vg = _mm(_gelu(yss), w_glu[...]) + mv_ref[ROW_B_GLU:ROW_B_GLU + 1, :]
        vg_ref[...] = vg.astype(SAVE_DTYPE)
        y_ssm = _mm(vg[:, 0:512] * _sigmoid(vg[:, 512:1024]), w_su[...])
        ys_ref[...] = y_ssm.astype(SAVE_DTYPE)

        merged = _sigmoid(gl_pool) * y_pool + _sigmoid(gl_ssm) * y_ssm
        mo = _mm(merged, w_o[...])
        mo_ref[...] = mo.astype(SAVE_DTYPE)
        x2_ref[...] = xv + _row(prm_ref, 5) * mo

    def tok(width):
        return pl.BlockSpec((tm, width), lambda i: (i, 0))

    hbm = _HBM
    widths = (D_MODEL, D_MODEL, IN_WIDTH, N_STATE, N_STATE, 512, 512, D_MODEL, 512, D_MODEL, D_MODEL)
    dtypes = (F32, SAVE_DTYPE, SAVE_DTYPE, F32, F32) + (SAVE_DTYPE,) * 6
    return _launch(
        body, "mixer_forward", grid=(nt,), semantics=("arbitrary",), carry=carry, steps=_grid_steps(nt),
        out_shape=[jax.ShapeDtypeStruct((T, w), dt) for w, dt in zip(widths, dtypes)],
        in_specs=[tok(D_MODEL), _resident(prm), hbm, hbm, hbm, hbm, hbm, _resident(pool_w), _resident(mvec),
                  _resident(srow), _resident(bb), _resident(ct)],
        out_specs=[tok(w) for w in widths],
        operands=(x, prm, w_in_s, w_pu_s, w_glu_s, w_su_s, w_out, pool_w, mvec, srow, bb, ct),
        scratch=[
            pltpu.VMEM((D_MODEL, IN_WIDTH), MXU_DTYPE), pltpu.VMEM((512, D_MODEL), MXU_DTYPE),
            pltpu.VMEM((512, D_MODEL), MXU_DTYPE), pltpu.VMEM((512, D_MODEL), MXU_DTYPE),
            pltpu.VMEM((D_MODEL, D_MODEL), MXU_DTYPE),
            pltpu.VMEM((SSM_BLOCKS, 8, SCAN_ROWS, SSM_BLOCK_STATE), F32),
            pltpu.VMEM((SSM_BLOCKS, 2, SCAN_ROWS, SSM_BLOCK_STATE), F32),
            pltpu.VMEM((POOL_HALO, POOL_WIDTH), F32),
            pltpu.VMEM((2, tm, SSM_BLOCK_STATE), F32),
            pltpu.SemaphoreType.DMA((33,)),
        ])


def _mixer_backward(d2, prm, saved, w_pu_s, w_glu_s, w_su_s, w_out, pool_w, mvec, srow, bb, ct, carry=None):
    z, s_re, s_im, zp, q, y_pool, yss, vg, y_ssm = saved
    T = d2.shape[0]
    tm = min(T, TM_MIX_BWD)
    nt = T // tm
    n_tiles = tm // SCAN_ROWS

    def body(d_ref, prm_ref, z_ref, sre_ref, sim_ref, zp_ref, q_ref, yp_ref, yss_ref, vg_ref, ys_ref,
             w_pu_h, w_glu_h, w_su_h, w_out_h, pw_ref, mv_ref, srow_ref, bb, ct,
             dz_ref, dwo_h, dwpu_h, dwglu_h, dwsu_h, dpw_h, dbb_h, dct_h, vsum_h, da_h,
             w_pu, w_glu, w_su, w_o, pwb, coef, carry, hist, dre, lam,
             a_wo, a_wpu, a_wglu, a_wsu, a_pw, a_bb, a_ct, a_vs, a_da, st_wo, st_up, sems):
        i = pl.program_id(0)
        tile = nt - 1 - i

        @pl.when(i == 0)
        def _():
            cps = (_load_stack(w_pu_h, w_pu, sems, 0) + _load_stack(w_glu_h, w_glu, sems, 8)
                   + _load_stack(w_su_h, w_su, sems, 16))
            cps.append(pltpu.make_async_copy(w_out_h, w_o, sems.at[24]))
            cps[-1].start()
            pwb[...] = pw_ref[...].astype(MXU_DTYPE)
            for j in range(SSM_BLOCKS):
                lanes = slice(j * SSM_BLOCK_STATE, (j + 1) * SSM_BLOCK_STATE)
                for k, t in enumerate(_scan_coefficients(srow_ref[0:1, lanes], srow_ref[1:2, lanes], True)):
                    coef[j, k] = t
            for acc in (carry, hist, a_wo, a_wpu, a_wglu, a_wsu, a_pw, a_bb, a_ct, a_vs, a_da):
                acc[...] = jnp.zeros_like(acc)
            for cp in cps:
                cp.wait()

        dv = d_ref[...]
        zt = z_ref[...].astype(F32)
        u_ssm, gl_pool, gl_ssm = zt[:, 512:1024], zt[:, 1024:2048], zt[:, 2048:3072]
        y_p, y_s = yp_ref[...].astype(F32), ys_ref[...].astype(F32)
        sgp, sgs = _sigmoid(gl_pool), _sigmoid(gl_ssm)
        dmo = (_row(prm_ref, 5) * dv).astype(MXU_DTYPE)
        a_wo[...] += _mm_tn(sgp * y_p + sgs * y_s, dmo)
        dmerged = _mm_nt(dmo, w_o[...])
        dy_pool = dmerged * sgp
        dgl_pool = dmerged * y_p * (sgp * (1.0 - sgp))
        dy_ssm = dmerged * sgs
        dgl_ssm = dmerged * y_s * (sgs * (1.0 - sgs))

        scale = mv_ref[ROW_POOL_SCALE:ROW_POOL_SCALE + 1, 0:512]
        qv, zpv = q_ref[...].astype(F32), zp_ref[...]
        a_wpu[...] += _mm_tn(qv * scale, dy_pool)
        dp = _mm_nt(dy_pool, w_pu[...])
        dq = dp * scale
        a_vs[0:1, 0:512] += _colsum(dp * qv)
        a_vs[1:2, 0:512] += _colsum(dq)
        dzp_blocks = []
        for k in range(4):
            lanes = slice(k * 128, (k + 1) * 128)
            dzp_blocks.append(_mm_nt(dq[:, lanes], pwb[k]))
            a_pw[k] += _mm_tn(zpv[:, lanes], dq[:, lanes])
        dzp = jnp.concatenate(dzp_blocks, axis=1)
        t1 = (lax.broadcasted_iota(jnp.int32, (tm, 1), 0) + (tile * tm + 1)).astype(F32)
        gs = dzp / jnp.minimum(t1, _window_lanes())
        n_ext = tm + POOL_HALO
        ext = jnp.concatenate([gs, hist[...]], axis=0)
        v2 = ext + pltpu.roll(ext, n_ext - 1, 0)
        v4 = v2[:, 128:] + pltpu.roll(v2[:, 128:], n_ext - 2, 0)
        v8 = v4[:, 128:] + pltpu.roll(v4[:, 128:], n_ext - 4, 0)
        v16 = v8[:, 128:] + pltpu.roll(v8[:, 128:], n_ext - 8, 0)
        msum = jnp.concatenate([v2[:tm, :128], v4[:tm, :128], v8[:tm, :128], v16[:tm]], axis=1)
        hist[...] = gs[0:POOL_HALO, :]
        du_pool = msum - dzp

        vgv = vg_ref[...].astype(F32)
        val, gate = vgv[:, 0:512], vgv[:, 512:1024]
        sgg = _sigmoid(gate)
        a_wsu[...] += _mm_tn(val * sgg, dy_ssm)
        do = _mm_nt(dy_ssm, w_su[...])
        dvg = jnp.concatenate([do * sgg, do * val * (sgg * (1.0 - sgg))], axis=1)
        a_vs[3:4, :] += _colsum(dvg)
        yv = yss_ref[...].astype(F32)
        a_wglu[...] += _mm_tn(_gelu(yv), dvg)
        dyss = _mm_nt(dvg, w_glu[...]) * _gelu_grad(yv)
        a_vs[2:3, 0:512] += _colsum(dyss * u_ssm)
        du_blocks = []
        for j in range(SSM_BLOCKS):
            lanes = pl.ds(j * SSM_BLOCK_STATE, SSM_BLOCK_STATE)
            in_lanes = slice(j * 128, (j + 1) * 128)
            dyb = dyss[:, in_lanes].astype(MXU_DTYPE)
            ub = u_ssm[:, in_lanes].astype(MXU_DTYPE)
            dre[0] = _mm_nt(dyb, ct[0, j])
            dre[1] = -_mm_nt(dyb, ct[1, j])
            a_ct[0, j] += _mm_tn(sre_ref[:, lanes], dyb)
            a_ct[1, j] -= _mm_tn(sim_ref[:, lanes], dyb)
            a1r, a1i, a2r, a2i, a4r, a4i, pr, pi = [coef[j, k] for k in range(8)]
            rowi = lax.broadcasted_iota(jnp.int32, (SCAN_ROWS, SSM_BLOCK_STATE), 0)

            def step(tt, c, lanes=lanes, a1r=a1r, a1i=a1i, a2r=a2r, a2i=a2i, a4r=a4r, a4i=a4i, pr=pr, pi=pi, rowi=rowi):
                cr, ci, acc_r, acc_i = c
                rows = pl.ds(pl.multiple_of((n_tiles - 1 - tt) * SCAN_ROWS, SCAN_ROWS), SCAN_ROWS)
                xr, xi = dre[0, rows, :], dre[1, rows, :]
                for dstep, kr, ki in ((1, a1r, a1i), (2, a2r, a2i), (4, a4r, a4i)):
                    sr, si = pltpu.roll(xr, SCAN_ROWS - dstep, 0), pltpu.roll(xi, SCAN_ROWS - dstep, 0)
                    xr, xi = xr + kr * sr + ki * si, xi + kr * si - ki * sr
                xr, xi = xr + pr * cr + pi * ci, xi + pr * ci - pi * cr
                lam[0, rows, :] = xr
                lam[1, rows, :] = xi
                nr = jnp.where(rowi == SCAN_ROWS - 1, cr, pltpu.roll(xr, SCAN_ROWS - 1, 0))
                ni = jnp.where(rowi == SCAN_ROWS - 1, ci, pltpu.roll(xi, SCAN_ROWS - 1, 0))
                s_r, s_i = sre_ref[rows, lanes], sim_ref[rows, lanes]
                acc_r = acc_r + nr * s_r + ni * s_i
                acc_i = acc_i + ni * s_r - nr * s_i
                return (jnp.broadcast_to(xr[0:1, :], xr.shape), jnp.broadcast_to(xi[0:1, :], xi.shape), acc_r, acc_i)

            cr, ci, acc_r, acc_i = lax.fori_loop(0, n_tiles, step, (carry[j, 0], carry[j, 1], a_da[0, j], a_da[1, j]))
            carry[j, 0] = cr
            carry[j, 1] = ci
            a_da[0, j] = acc_r
            a_da[1, j] = acc_i
            lr_b, li_b = lam[0].astype(MXU_DTYPE), lam[1].astype(MXU_DTYPE)
            a_bb[0, j] += _mm_tn(ub, lr_b)
            a_bb[1, j] += _mm_tn(ub, li_b)
            du_blocks.append(_mm_nt(lr_b, bb[0, j]) + _mm_nt(li_b, bb[1, j]))
        du_ssm = jnp.concatenate(du_blocks, axis=1) + dyss * mv_ref[ROW_SSM_D:ROW_SSM_D + 1, 0:512]
        dz_ref[...] = jnp.concatenate([du_pool, du_ssm, dgl_pool, dgl_ssm], axis=1).astype(SAVE_DTYPE)

        @pl.when(i == nt - 1)
        def _():
            rows = D_MODEL // NDEV
            for k in range(NDEV):
                st_wo[k] = a_wo[k * rows:(k + 1) * rows, :].astype(WIRE_DTYPE)
                for a, acc in enumerate((a_wpu, a_wglu, a_wsu)):
                    st_up[a, k] = acc[:, k * 128:(k + 1) * 128].astype(WIRE_DTYPE)
            outs = ((st_wo, dwo_h), (st_up.at[0], dwpu_h), (st_up.at[1], dwglu_h), (st_up.at[2], dwsu_h),
                    (a_pw, dpw_h), (a_bb, dbb_h), (a_ct, dct_h), (a_vs, vsum_h), (a_da, da_h))
            cps = [pltpu.make_async_copy(src, dst, sems.at[k]) for k, (src, dst) in enumerate(outs)]
            for cp in cps:
                cp.start()
            for cp in cps:
                cp.wait()

    def tok(width):
        return pl.BlockSpec((tm, width), lambda i: (nt - 1 - i, 0))

    hbm = _HBM
    acc_shapes = [(D_MODEL, D_MODEL), (512, D_MODEL), (512, D_MODEL), (512, D_MODEL), (4, 128, 128),
                  (2, SSM_BLOCKS, 128, SSM_BLOCK_STATE), (2, SSM_BLOCKS, SSM_BLOCK_STATE, 128), (8, D_MODEL),
                  (2, SSM_BLOCKS, SCAN_ROWS, SSM_BLOCK_STATE)]
    stack_out = [jax.ShapeDtypeStruct((NDEV, D_MODEL // NDEV, D_MODEL), WIRE_DTYPE)] \
        + [jax.ShapeDtypeStruct((NDEV, 512, 128), WIRE_DTYPE)] * 3
    return _launch(
        body, "mixer_backward", grid=(nt,), semantics=("arbitrary",), carry=carry, steps=_grid_steps(nt),
        out_shape=[jax.ShapeDtypeStruct((T, IN_WIDTH), SAVE_DTYPE)] + stack_out
        + [jax.ShapeDtypeStruct(s, F32) for s in acc_shapes[4:]],
        in_specs=[tok(D_MODEL), _resident(prm), tok(IN_WIDTH), tok(N_STATE), tok(N_STATE), tok(512), tok(512),
                  tok(D_MODEL), tok(512), tok(D_MODEL), tok(D_MODEL), hbm, hbm, hbm, hbm, _resident(pool_w),
                  _resident(mvec), _resident(srow), _resident(bb), _resident(ct)],
        out_specs=[tok(IN_WIDTH)] + [hbm] * len(acc_shapes),
        operands=(d2, prm, z, s_re, s_im, zp, q, y_pool, yss, vg, y_ssm, w_pu_s, w_glu_s, w_su_s, w_out, pool_w, mvec,
                  srow, bb, ct),
        scratch=[
            pltpu.VMEM((512, D_MODEL), MXU_DTYPE), pltpu.VMEM((512, D_MODEL), MXU_DTYPE),
            pltpu.VMEM((512, D_MODEL), MXU_DTYPE), pltpu.VMEM((D_MODEL, D_MODEL), MXU_DTYPE),
            pltpu.VMEM((4, 128, 128), MXU_DTYPE),
            pltpu.VMEM((SSM_BLOCKS, 8, SCAN_ROWS, SSM_BLOCK_STATE), F32),
            pltpu.VMEM((SSM_BLOCKS, 2, SCAN_ROWS, SSM_BLOCK_STATE), F32),
            pltpu.VMEM((POOL_HALO, POOL_WIDTH), F32),
            pltpu.VMEM((2, tm, SSM_BLOCK_STATE), F32), pltpu.VMEM((2, tm, SSM_BLOCK_STATE), F32),
        ] + [pltpu.VMEM(s, F32) for s in acc_shapes]
        + [pltpu.VMEM((NDEV, D_MODEL // NDEV, D_MODEL), WIRE_DTYPE), pltpu.VMEM((3, NDEV, 512, 128), WIRE_DTYPE),
           pltpu.SemaphoreType.DMA((25,))])


def _mixer_in_backward(x, dz, d, mo, prm, w_in_s):
    T = x.shape[0]
    tm = min(T, TM_MIX)
    nt = T // tm
    cols = IN_WIDTH // NDEV

    def body(x_ref, dz_ref, d_ref, mo_ref, prm_ref, w_in_h, dx_ref, sums_ref, dw_ref, w_in, acc, sems):
        i = pl.program_id(0)

        @pl.when(i == 0)
        def _():
            cps = _load_stack(w_in_h, w_in, sems, 0)
            acc[...] = jnp.zeros_like(acc)
            for cp in cps:
                cp.wait()

        xv = x_ref[...]
        h, _, _ = _modulated(xv, prm_ref, 1, ROW_G_MIX)
        dzb = dz_ref[...].astype(MXU_DTYPE)
        acc[...] += _mm_tn(h, dzb)
        dx_ref[...], upd = _norm_backward_tile(_mm_nt(dzb, w_in[...]), xv, d_ref[...], mo_ref[...], prm_ref, 1,
                                               ROW_G_MIX, 1.0)

        @pl.when(i == 0)
        def _():
            sums_ref[...] = upd

        @pl.when(i > 0)
        def _():
            sums_ref[...] += upd

        @pl.when(i == nt - 1)
        def _():
            for k in range(NDEV):
                dw_ref[k] = acc[:, k * cols:(k + 1) * cols].astype(WIRE_DTYPE)

    tok = pl.BlockSpec((tm, D_MODEL), lambda i: (i, 0))
    return pl.pallas_call(
        body, name="mixer_in_backward", grid=(nt,),
        out_shape=[jax.ShapeDtypeStruct((T, D_MODEL), F32), jax.ShapeDtypeStruct((8, D_MODEL), F32),
                   jax.ShapeDtypeStruct((NDEV, D_MODEL, cols), WIRE_DTYPE)],
        in_specs=[tok, pl.BlockSpec((tm, IN_WIDTH), lambda i: (i, 0)), tok, tok, _resident(prm), _HBM],
        out_specs=[tok, pl.BlockSpec((8, D_MODEL), lambda i: (0, 0)),
                   pl.BlockSpec((NDEV, D_MODEL, cols), lambda i: (0, 0, 0))],
        scratch_shapes=[pltpu.VMEM((D_MODEL, IN_WIDTH), MXU_DTYPE), pltpu.VMEM((D_MODEL, IN_WIDTH), F32),
                        pltpu.SemaphoreType.DMA((8,))],
        compiler_params=_params("arbitrary"),
    )(x, dz, d, mo, prm, w_in_s)


def _ssm_dense_backward(dbb, da, srow, b_dense):
    def body(dbb_ref, da_ref, srow_ref, bd_ref, db_ref, df_ref):
        df_re, df_im = [], []
        da_re = [_colsum(da_ref[0, j]) for j in range(SSM_BLOCKS)]
        da_im = [_colsum(da_ref[1, j]) for j in range(SSM_BLOCKS)]
        for j in range(SSM_BLOCKS):
            lanes = slice(j * SSM_BLOCK_STATE, (j + 1) * SSM_BLOCK_STATE)
            f_re, f_im = srow_ref[2:3, lanes], srow_ref[3:4, lanes]
            g_re, g_im = dbb_ref[0, j], dbb_ref[1, j]
            b_re, b_im = bd_ref[0, j], bd_ref[1, j]
            db_ref[0, j] = f_re * g_re + f_im * g_im
            db_ref[1, j] = f_re * g_im - f_im * g_re
            df_re.append(_colsum(g_re * b_re + g_im * b_im))
            df_im.append(_colsum(g_im * b_re - g_re * b_im))
        df_ref[...] = jnp.concatenate([jnp.concatenate(df_re, axis=1), jnp.concatenate(df_im, axis=1),
                                       jnp.concatenate(da_re, axis=1), jnp.concatenate(da_im, axis=1),
                                       jnp.zeros((4, N_STATE), F32)], axis=0)

    return pl.pallas_call(body, name="ssm_dense_backward",
                          out_shape=[jax.ShapeDtypeStruct(b_dense.shape, F32), jax.ShapeDtypeStruct((8, N_STATE), F32)],
                          compiler_params=pltpu.CompilerParams(vmem_limit_bytes=VMEM_LIMIT))(dbb, da, srow, b_dense)


def _adamw_update(w, g, m, v):
    m = ADAM_B1 * m + (1.0 - ADAM_B1) * g
    v = ADAM_B2 * v + (1.0 - ADAM_B2) * (g * g)
    m_hat = m / (1.0 - ADAM_B1 ** ADAM_STEP)
    v_hat = v / (1.0 - ADAM_B2 ** ADAM_STEP)
    delta = -ADAM_LR * (m_hat / (jnp.sqrt(v_hat) + ADAM_EPS) + ADAM_WD * w)
    return delta, m, v


def _adam_rows(shape):
    rows, cols = shape
    tr = rows
    while tr * cols * 4 > (1 << 20) and tr % 16 == 0:
        tr //= 2
    return tr


def _adam_sharded(w, m, v, land, order, name):
    R, C = w.shape
    tr = _adam_rows((R, C))

    def body(w_ref, m_ref, v_ref, land_ref, order_ref, g_ref, d_ref, mo_ref, vo_ref):
        g = land_ref[0].astype(F32)
        for b in range(1, NDEV):
            g = g + land_ref[b].astype(F32)
        g_ref[...] = g
        d_ref[...], mo_ref[...], vo_ref[...] = _adamw_update(w_ref[...], g, m_ref[...], v_ref[...])

    blk = pl.BlockSpec((tr, C), lambda i: (i, 0))
    return pl.pallas_call(
        body, name=name, grid=(R // tr,),
        out_shape=[jax.ShapeDtypeStruct((R, C), F32)] * 4,
        in_specs=[blk, blk, blk, pl.BlockSpec((NDEV, tr, C), lambda i: (0, i, 0)), _HBM],
        out_specs=[blk] * 4,
        compiler_params=_params("arbitrary"),
    )(w, m, v, land, order)


def _adam_ada(w, m, v, sc_all, dmod_cols):
    R, C = w.shape
    tr = 256

    def body(w_ref, m_ref, v_ref, sc_ref, dm_ref, g_ref, d_ref, mo_ref, vo_ref):
        g = _mm_tn(sc_ref[...], dm_ref[...])
        g_ref[...] = g
        d_ref[...], mo_ref[...], vo_ref[...] = _adamw_update(w_ref[...], g, m_ref[...], v_ref[...])

    blk = pl.BlockSpec((tr, C), lambda i: (i, 0))
    return pl.pallas_call(
        body, name="adam_w_ada", grid=(R // tr,),
        out_shape=[jax.ShapeDtypeStruct((R, C), F32)] * 4,
        in_specs=[blk, blk, blk, pl.BlockSpec((8, tr), lambda i: (0, i)), pl.BlockSpec((8, C), lambda i: (0, 0))],
        out_specs=[blk] * 4,
        compiler_params=_params("arbitrary"),
    )(w, m, v, sc_all, dmod_cols)


def _adam_small(w, g, m, v, name):
    def body(w_ref, g_ref, m_ref, v_ref, d_ref, mo_ref, vo_ref):
        d_ref[...], mo_ref[...], vo_ref[...] = _adamw_update(w_ref[...], g_ref[...], m_ref[...], v_ref[...])

    return pl.pallas_call(body, name=name, out_shape=[jax.ShapeDtypeStruct(w.shape, F32)] * 3,
                          compiler_params=pltpu.CompilerParams(vmem_limit_bytes=VMEM_LIMIT))(w, g, m, v)


def _block_diag_in(b):
    bt = jnp.transpose(b, (0, 2, 1)).reshape(SSM_BLOCKS, 8, SSM_GROUP, SSM_STATE)
    eye = jnp.eye(8, dtype=bool)[None, :, None, :, None]
    return jnp.where(eye, bt[:, :, :, None, :], 0.0).reshape(SSM_BLOCKS, 128, SSM_BLOCK_STATE)


def _block_diag_out(c):
    ct = jnp.transpose(c, (0, 2, 1)).reshape(SSM_BLOCKS, 8, SSM_STATE, SSM_GROUP)
    eye = jnp.eye(8, dtype=bool)[None, :, None, :, None]
    return jnp.where(eye, ct[:, :, :, None, :], 0.0).reshape(SSM_BLOCKS, SSM_BLOCK_STATE, 128)


def _diag_blocks(dense, rows, cols):
    d5 = dense.reshape(SSM_BLOCKS, 8, rows, 8, cols)
    return jnp.stack([d5[:, a, :, a, :] for a in range(8)], axis=1).reshape(32, rows, cols)


def _pack_small(ada_vec, parts, params, tail=None):
    rest_rows, rows = _pack_rows(params, ada_vec is not None)
    rest = jnp.concatenate([parts[n].reshape(-1) for n, _ in params])
    rest = jnp.pad(rest, (0, NDEV * rest_rows * 128 - rest.shape[0])).reshape(NDEV, rest_rows, 128)
    head = [] if ada_vec is None else [ada_vec.reshape(NDEV, ADA_ROWS, 128)]
    pad = rows - rest_rows - (0 if ada_vec is None else ADA_ROWS)
    fill = jnp.zeros((NDEV, pad, 128), F32) if tail is None else jnp.pad(tail[None], ((0, NDEV - 1), (0, pad - 1), (0, 127)))
    return jnp.concatenate(head + [rest] + ([fill] if pad else []), axis=1)


def _unpack_small(pack, shapes, params, with_ada):
    rest_rows, _ = _pack_rows(params, with_ada)
    first = ADA_ROWS if with_ada else 0
    ada_vec = pack[:, :first].reshape(-1) if with_ada else None
    rest = pack[:, first:first + rest_rows].reshape(-1)
    out, off = {}, 0
    for n, size in params:
        out[n] = rest[off:off + size].reshape(shapes[n])
        off += size
    return ada_vec, out


WEIGHT_ORDER = ('w_ada', 'b_ada', 'g_ffn1', 'w_ffn1_in', 'w_ffn1_out', 'g_mix', 'w_in', 'pool_w', 'pool_b',
                'pool_scale', 'w_pool_up', 'ssm_lam_re_log', 'ssm_lam_im', 'ssm_log_dt', 'ssm_b_re', 'ssm_b_im',
                'ssm_c_re', 'ssm_c_im', 'ssm_d', 'w_glu', 'b_glu', 'w_ssm_up', 'w_out', 'g_ffn2', 'w_ffn2_in',
                'w_ffn2_out', 'g_final')
GATHERED = ('w_ffn1_in', 'w_ffn1_out', 'w_in', 'w_pool_up', 'w_glu', 'w_ssm_up', 'w_out', 'w_ffn2_in', 'w_ffn2_out')
TRANSPOSED = ('w_ffn1_in', 'w_ffn2_in')
STATE_MINOR = ('ssm_b_re', 'ssm_b_im')


def kernel(x, c, w_ada, b_ada, g_ffn1, w_ffn1_in, w_ffn1_out, g_mix, w_in, pool_w, pool_b, pool_scale, w_pool_up, ssm_lam_re_log, ssm_lam_im, ssm_log_dt, ssm_b_re, ssm_b_im, ssm_c_re, ssm_c_im, ssm_d, w_glu, b_glu, w_ssm_up, w_out, g_ffn2, w_ffn2_in, w_ffn2_out, g_final, loss_target, m_w_ada, m_b_ada, m_g_ffn1, m_w_ffn1_in, m_w_ffn1_out, m_g_mix, m_w_in, m_pool_w, m_pool_b, m_pool_scale, m_w_pool_up, m_ssm_lam_re_log, m_ssm_lam_im, m_ssm_log_dt, m_ssm_b_re, m_ssm_b_im, m_ssm_c_re, m_ssm_c_im, m_ssm_d, m_w_glu, m_b_glu, m_w_ssm_up, m_w_out, m_g_ffn2, m_w_ffn2_in, m_w_ffn2_out, m_g_final, v_w_ada, v_b_ada, v_g_ffn1, v_w_ffn1_in, v_w_ffn1_out, v_g_mix, v_w_in, v_pool_w, v_pool_b, v_pool_scale, v_w_pool_up, v_ssm_lam_re_log, v_ssm_lam_im, v_ssm_log_dt, v_ssm_b_re, v_ssm_b_im, v_ssm_c_re, v_ssm_c_im, v_ssm_d, v_w_glu, v_b_glu, v_w_ssm_up, v_w_out, v_g_ffn2, v_w_ffn2_in, v_w_ffn2_out, v_g_final):
    args = locals()
    W = {n: args[n] for n in WEIGHT_ORDER}
    M = {n: args["m_" + n] for n in WEIGHT_ORDER}
    V = {n: args["v_" + n] for n in WEIGHT_ORDER}
    shapes = {n: W[n].shape for n in WEIGHT_ORDER}
    xt, tgt = x[0], loss_target[0]

    def local(tree, n):
        return jnp.swapaxes(tree[n][0], 0, 1) if n in TRANSPOSED else tree[n][0]

    def as_output(n, a):
        return (jnp.swapaxes(a, 0, 1) if n in TRANSPOSED else a)[None]

    shard = dict(zip(GATHERED, _cast_shards([local(W, n) for n in GATHERED])))
    stacks = {}

    def gather(names):
        return _Gather([shard[n] for n in names])

    def gathered(names, results):
        stacks.update(zip(names, results))

    ffn1_w, ffn2_w = ('w_ffn1_in', 'w_ffn1_out'), ('w_ffn2_in', 'w_ffn2_out')
    mix_w = ('w_in', 'w_pool_up', 'w_glu', 'w_ssm_up', 'w_out')
    mod_cols, sc_all, *res = _ada_forward(c, W['w_ada'][0], b_ada.reshape(NDEV, -1), gather(ffn1_w[:1]))
    gathered(ffn1_w[:1], res)
    win1 = stacks['w_ffn1_in'].reshape(2, 4, FF_SHARD, D_MODEL)
    prm = jnp.concatenate([mod_cols.reshape(9, D_MODEL), g_ffn1, g_mix, g_ffn2, g_final[None], jnp.zeros((3, D_MODEL), F32)], axis=0)
    pad512 = jnp.zeros((1, D_MODEL - 512), F32)
    mvec = jnp.concatenate([jnp.concatenate([pool_b, pad512], axis=1), jnp.concatenate([pool_scale, pad512], axis=1),
                            jnp.concatenate([ssm_d, pad512], axis=1), b_glu, jnp.zeros((4, D_MODEL), F32)], axis=0)
    log_dt_col = ssm_log_dt[0][:, None]
    coeffs = _ssm_params_forward(ssm_lam_re_log[0], ssm_lam_im[0], log_dt_col)
    srow = jnp.stack([t.reshape(N_STATE) for t in coeffs], axis=0)
    b_dense = jnp.stack([_block_diag_in(ssm_b_re[0]), _block_diag_in(ssm_b_im[0])], axis=0)
    c_dense = jnp.stack([_block_diag_out(ssm_c_re[0]), _block_diag_out(ssm_c_im[0])], axis=0)
    bb, ct = _ssm_dense_forward(srow, b_dense, c_dense)
    pw = pool_w[0]

    next_w = ffn1_w[1:] + mix_w[:1]
    ab1, s1, *res = _ffn_hidden(xt, prm, win1, 0, ROW_G_FFN1, "ffn1_hidden", gather(next_w))
    gathered(next_w, res)
    wout1 = stacks['w_ffn1_out'].reshape(4, FF_SHARD, D_MODEL)
    x1, f1, *res = _ffn_out(xt, s1, prm, wout1, 0, "ffn1_out", gather(mix_w[1:]))
    gathered(mix_w[1:], res)
    w_out_full = stacks['w_out'].reshape(D_MODEL, D_MODEL)
    res = _mixer_forward(x1, prm, stacks['w_in'], stacks['w_pool_up'], stacks['w_glu'], stacks['w_ssm_up'],
                         w_out_full, pw, mvec, srow, bb, ct, gather(ffn2_w))
    x2, mo, saved = res[0], res[1], res[2:11]
    gathered(ffn2_w, res[11:])
    win2 = stacks['w_ffn2_in'].reshape(2, 4, FF_SHARD, D_MODEL)
    wout2 = stacks['w_ffn2_out'].reshape(4, FF_SHARD, D_MODEL)
    d3, fin, f3, ab3 = _ffn_forward_loss(x2, tgt, prm, win2, wout2, 2, ROW_G_FFN2, "ffn2_forward_loss")

    lands = {}

    def scatter(grads):
        names = list(grads)
        return _Scatter([grads[n][0] for n in names], [grads[n][1] for n in names], [local(W, n).shape for n in names])

    def scattered(grads, results):
        lands.update(zip(grads, results))

    parts3, dwin2, dwout2 = _ffn_backward(x2, d3, ab3, prm, win2, wout2, prm, 2, ROW_G_FFN2, "ffn2_backward")
    d2, sums3 = _norm_backward(parts3, x2, d3, f3, prm, 2, ROW_G_FFN2, 0.5, "ffn2_norm_backward")
    g_ffn2_w = {'w_ffn2_in': (dwin2, _halves), 'w_ffn2_out': (dwout2.reshape(NDEV, -1, D_MODEL), _stacked)}
    res = _mixer_backward(d2, prm, saved, stacks['w_pool_up'], stacks['w_glu'], stacks['w_ssm_up'], w_out_full, pw, mvec,
                          srow, bb, ct, scatter(g_ffn2_w))
    dz, dwo, dwpu, dwglu, dwsu, dpw, dbb, dct, vsum, da = res[:10]
    scattered(g_ffn2_w, res[10:])
    d1, sums2, dwin_mix = _mixer_in_backward(x1, dz, d2, mo, prm, stacks['w_in'])
    g_mix_w = {'w_in': (dwin_mix, _stacked), 'w_pool_up': (dwpu, _stacked), 'w_glu': (dwglu, _stacked),
               'w_ssm_up': (dwsu, _stacked), 'w_out': (dwo, _stacked)}
    db_dense, df_rows = _ssm_dense_backward(dbb, da, srow, b_dense)
    cot = [df_rows[r].reshape(32, 64) for r in (2, 3, 0, 1)]
    d_lrl, d_li, d_ldt = _ssm_params_backward(ssm_lam_re_log[0], ssm_lam_im[0], log_dt_col, cot)
    small_grads = {
        'g_mix': sums2[0], 'g_ffn2': sums3[0], 'g_final': fin[0], 'pool_w': dpw,
        'pool_b': vsum[1, :512], 'pool_scale': vsum[0, :512], 'ssm_lam_re_log': d_lrl, 'ssm_lam_im': d_li,
        'ssm_log_dt': d_ldt, 'ssm_b_re': _diag_blocks(db_dense[0], SSM_GROUP, SSM_STATE),
        'ssm_b_im': _diag_blocks(db_dense[1], SSM_GROUP, SSM_STATE),
        'ssm_c_re': jnp.transpose(_diag_blocks(dct[0], SSM_STATE, SSM_GROUP), (0, 2, 1)),
        'ssm_c_im': jnp.transpose(_diag_blocks(dct[1], SSM_STATE, SSM_GROUP), (0, 2, 1)),
        'ssm_d': vsum[2, :512], 'b_glu': vsum[3],
    }
    early = _SmallAllReduce(_pack_small(None, small_grads, SMALL_EARLY, fin[1:2, 0:1]))
    parts1, dab1, dwout1, total_early, *res = _ffn_backward(
        xt, d1, ab1, prm, win1, wout1, prm, 0, ROW_G_FFN1, "ffn1_backward", _Carried(early, scatter(g_mix_w)),
        defer_dwin=True)
    scattered(g_mix_w, res)
    loss = total_early[0, _pack_rows(SMALL_EARLY, False)[0], 0]
    g_wout1 = {'w_ffn1_out': (dwout1.reshape(NDEV, -1, D_MODEL), _stacked)}
    dwin1, *res = _ffn_dwin(xt, dab1, prm, 0, ROW_G_FFN1, "ffn1_dwin", scatter(g_wout1))
    scattered(g_wout1, res)

    last_w, last_views = ffn1_w[:1], [_halves]
    send_sems, recv_sems, last_src, last_land, token = _scatter_start(
        [dwin1], last_views, [local(W, n).shape for n in last_w], [total_early])
    after_start = token[0:1, 0:1]
    d0, sums1 = _norm_backward(parts1, xt, d1, f1, prm + after_start, 0, ROW_G_FFN1, 0.5, "ffn1_norm_backward")

    grad, delta, new_m, new_v = {}, {}, {}, {}

    def adam_sharded(n):
        res = _adam_sharded(local(W, n), local(M, n), local(V, n), lands[n], token, "adam_" + n)
        grad[n], delta[n], new_m[n], new_v[n] = [as_output(n, r) for r in res]
        return res[3]

    def adam_small(params, ada, total, name, order):
        rows = _pack_rows(params, ada)[1]
        views = [{n: jnp.transpose(t[n][0], (0, 2, 1)) if n in STATE_MINOR else t[n] for n, _ in params} for t in (W, M, V)]
        packs = [(_pack_small(t['b_ada'].reshape(-1) if ada else None, v, params) + order).reshape(NDEV * rows, 128)
                 for t, v in zip((W, M, V), views)]
        res = _adam_small(packs[0], total.reshape(NDEV * rows, 128), packs[1], packs[2], name)
        view_shapes = {n: (32, SSM_GROUP, SSM_STATE) if n in STATE_MINOR else shapes[n] for n, _ in params}
        for dst, packed in zip((grad, delta, new_m, new_v), (total, *res)):
            ada_vec, rest = _unpack_small(packed.reshape(NDEV, rows, 128), view_shapes, params, ada)
            dst.update({n: jnp.transpose(a, (0, 2, 1))[None] if n in STATE_MINOR else a for n, a in rest.items()})
            if ada:
                dst['b_ada'] = ada_vec.reshape(shapes['b_ada'])
        return res[2]

    done = [d0] + [adam_sharded(n) for n in GATHERED if n not in last_w]
    done.append(adam_small(SMALL_EARLY, False, total_early, "adam_small_early", after_start))
    lands.update(zip(last_w, _scatter_wait(send_sems, recv_sems, last_src, last_land, last_views, done)))

    dmod = jnp.concatenate([sums1[1:4], sums2[1:4], sums3[1:4]], axis=0).reshape(-1)
    total_late, landed = _allreduce_small(_pack_small(dmod, {'g_ffn1': sums1[0]}, SMALL_LATE), lands[last_w[0]],
                                          "allreduce_late")
    dmod_cols = landed[:, :ADA_ROWS].reshape(NDEV, ADA_ROWS * 128)
    res = _adam_ada(W['w_ada'][0], M['w_ada'][0], V['w_ada'][0], sc_all, dmod_cols)
    grad['w_ada'], delta['w_ada'], new_m['w_ada'], new_v['w_ada'] = [r[None] for r in res]
    adam_small(SMALL_LATE, True, total_late, "adam_small_late", 0.0)
    for n in last_w:
        adam_sharded(n)

    return (loss, d0[None], *[grad[n] for n in WEIGHT_ORDER], *[delta[n] for n in WEIGHT_ORDER],
            *[new_m[n] for n in WEIGHT_ORDER], *[new_v[n] for n in WEIGHT_ORDER])
```

```python
import jax
import jax.numpy as jnp
from jax import lax
from jax.experimental import pallas as pl
from jax.experimental.pallas import tpu as pltpu

F32 = jnp.float32
MXU_DTYPE = jnp.bfloat16
WIRE_DTYPE = jnp.bfloat16
SAVE_DTYPE = jnp.bfloat16

NDEV = 8
D_MODEL = 1024
D_FF = 2816
FF_SHARD = 2 * D_FF // NDEV
POOL_WIDTH = 512
POOL_GROUP = 128
SSM_WIDTH = 512
SSM_STATE = 64
SSM_GROUP = 16
N_SSM_GROUPS = SSM_WIDTH // SSM_GROUP
SSM_BLOCKS = 4
SSM_BLOCK_STATE = 512
N_STATE = 2048
IN_WIDTH = 3072
EPS = 1e-6
ADAM_LR = 0.001
ADAM_B1 = 0.9
ADAM_B2 = 0.999
ADAM_EPS = 1e-08
ADAM_WD = 0.01
ADAM_STEP = 10

TM_FFN = 512
FFN_BWD_CHUNK = 256
TM_MIX = 256
TM_MIX_BWD = 256
TM_EW = 512
SCAN_ROWS = 8
POOL_HALO = 16
VMEM_LIMIT = 60 * 1024 * 1024

ROW_G_FFN1, ROW_G_MIX, ROW_G_FFN2, ROW_G_FINAL = 9, 10, 11, 12
ROW_POOL_B, ROW_POOL_SCALE, ROW_SSM_D, ROW_B_GLU = 0, 1, 2, 3

SMALL_EARLY = (
    ("g_mix", 1024), ("g_ffn2", 1024), ("g_final", 1024), ("pool_w", 65536),
    ("pool_b", 512), ("pool_scale", 512), ("ssm_lam_re_log", 2048), ("ssm_lam_im", 2048),
    ("ssm_log_dt", 32), ("ssm_b_re", 32768), ("ssm_b_im", 32768), ("ssm_c_re", 32768),
    ("ssm_c_im", 32768), ("ssm_d", 512), ("b_glu", 1024),
)
SMALL_LATE = (("g_ffn1", 1024),)
ADA_ROWS = 9
MESH = pl.DeviceIdType.MESH


def _pack_rows(params, with_ada):
    rest = -(-sum(n for _, n in params) // (NDEV * 128))
    return rest, -(-(rest + (ADA_ROWS if with_ada else 0)) // 8) * 8


def _mm(a, b):
    return jnp.dot(a.astype(MXU_DTYPE), b.astype(MXU_DTYPE), preferred_element_type=F32)


def _mm_nt(a, b):
    return lax.dot_general(a.astype(MXU_DTYPE), b.astype(MXU_DTYPE), (((1,), (1,)), ((), ())),
                           preferred_element_type=F32)


def _mm_tn(a, b):
    return lax.dot_general(a.astype(MXU_DTYPE), b.astype(MXU_DTYPE), (((0,), (0,)), ((), ())),
                           preferred_element_type=F32)


def _rms_scale(x):
    return lax.rsqrt(jnp.mean(x * x, axis=-1, keepdims=True) + EPS)


def _sigmoid(x):
    return jax.nn.sigmoid(x)


def _colsum(x):
    return jnp.sum(x, axis=0, keepdims=True)


def _row(ref, r):
    return ref[r:r + 1, :]


def _params(*sem):
    return pltpu.CompilerParams(dimension_semantics=sem, vmem_limit_bytes=VMEM_LIMIT)


def _resident(a):
    return pl.BlockSpec(a.shape, lambda *_: (0,) * a.ndim, pipeline_mode=pl.Buffered(1))


def _me():
    return lax.axis_index("x"), lax.axis_index("y"), lax.axis_index("c")


def _peer(rel):
    x, y, c = _me()
    px = 1 - x if rel & 4 else x
    py = 1 - y if rel & 2 else y
    pc = 1 - c if rel & 1 else c
    return (px, py, pc), 4 * px + 2 * py + pc


_HBM = pl.BlockSpec(memory_space=pl.ANY)
_HBM_ONLY = pl.BlockSpec(memory_space=pltpu.HBM)


def _stacked(ref, p):
    return ref.at[p]


def _halves(ref, p):
    return ref.at[p // 4, p % 4]


class _Gather:
    def __init__(self, shards):
        self.operands = list(shards)
        self.n = len(shards)
        self.out_shape = [jax.ShapeDtypeStruct((NDEV,) + s.shape, s.dtype) for s in shards]
        self.scratch = [pltpu.SemaphoreType.DMA((7 * self.n,)), pltpu.SemaphoreType.DMA((7 * self.n,)),
                        pltpu.SemaphoreType.DMA((self.n,))]

    def plan(self, srcs, outs, sems):
        send_sems, recv_sems, local_sems = sems
        n = self.n
        x, y, c = _me()
        me = 4 * x + 2 * y + c
        here, sibling = (x, y, c), (x, y, 1 - c)
        chips = [(1 - x, y), (x, 1 - y), (1 - x, 1 - y)]

        def blk(px, py, pc):
            return 4 * px + 2 * py + pc

        def copy(a, k, block, to, src=None):
            return pltpu.make_async_remote_copy(
                src_ref=outs[a].at[block] if src is None else src, dst_ref=outs[a].at[block],
                send_sem=send_sems.at[7 * a + k], recv_sem=recv_sems.at[7 * a + k], device_id=to, device_id_type=MESH)

        def mine(a):
            return pltpu.make_async_copy(srcs[a], outs[a].at[me], local_sems.at[a])

        def first(a):
            return [copy(a, 0, me, sibling, src=srcs[a])] + [copy(a, 1 + j, me, (*chip, c), src=srcs[a])
                                                              for j, chip in enumerate(chips)]

        def start():
            for a in range(n):
                mine(a).start()
                for cp in first(a):
                    cp.start()

        def forward():
            for a in range(n):
                for j, chip in enumerate(chips):
                    copy(a, 1 + j, blk(*chip, c), here).wait_recv()
                    copy(a, 4 + j, blk(*chip, c), sibling).start()

        def finish():
            for a in range(n):
                copy(a, 0, blk(x, y, 1 - c), here).wait_recv()
                for j, chip in enumerate(chips):
                    copy(a, 4 + j, blk(*chip, 1 - c), here).wait_recv()
            for a in range(n):
                mine(a).wait()
                for cp in first(a):
                    cp.wait_send()
                for j, chip in enumerate(chips):
                    copy(a, 4 + j, blk(*chip, c), sibling).wait_send()

        return start, forward, finish


class _Scatter:
    def __init__(self, arrays, views, shard_shapes):
        self.operands = list(arrays)
        self.views = list(views)
        self.n = len(arrays)
        self.out_shape = [jax.ShapeDtypeStruct((NDEV,) + tuple(s), a.dtype) for s, a in zip(shard_shapes, arrays)]
        self.scratch = [pltpu.SemaphoreType.DMA((7 * self.n,)), pltpu.SemaphoreType.DMA((7 * self.n,)),
                        pltpu.SemaphoreType.DMA((self.n,))]

    def plan(self, srcs, outs, sems):
        send_sems, recv_sems, local_sems = sems
        n, views = self.n, self.views
        x, y, c = _me()
        me = 4 * x + 2 * y + c

        def mine(a):
            return pltpu.make_async_copy(views[a](srcs[a], me), outs[a].at[me], local_sems.at[a])

        def copy(a, rel, sending):
            to, p = _peer(rel)
            return pltpu.make_async_remote_copy(
                src_ref=views[a](srcs[a], p), dst_ref=outs[a].at[me if sending else p],
                send_sem=send_sems.at[7 * a + rel - 1], recv_sem=recv_sems.at[7 * a + rel - 1],
                device_id=to if sending else (x, y, c), device_id_type=MESH)

        def start():
            for a in range(n):
                mine(a).start()
            for rel in range(1, 8):
                for a in range(n):
                    copy(a, rel, True).start()

        def forward():
            pass

        def finish():
            for rel in range(1, 8):
                for a in range(n):
                    copy(a, rel, False).wait_recv()
            for rel in range(1, 8):
                for a in range(n):
                    copy(a, rel, True).wait_send()
            for a in range(n):
                mine(a).wait()

        return start, forward, finish


class _SmallAllReduce:
    def __init__(self, pack):
        rows = pack.shape[1]
        self.operands = [pack]
        self.n = 1
        self.out_shape = [jax.ShapeDtypeStruct(pack.shape, F32)]
        self.scratch = [pltpu.VMEM(pack.shape, F32), pltpu.VMEM((rows, 128), F32)] \
            + [pltpu.SemaphoreType.DMA((7,))] * 4 + [pltpu.SemaphoreType.DMA((2,))]

    def plan(self, srcs, outs, scratch):
        pack, total = srcs[0], outs[0]
        land, mine, send1, recv1, send2, recv2, local = scratch
        x, y, c = _me()
        me = 4 * x + 2 * y + c

        def slab(rel, sending):
            to, p = _peer(rel)
            return pltpu.make_async_remote_copy(
                src_ref=pack.at[p], dst_ref=land.at[me if sending else p], send_sem=send1.at[rel - 1],
                recv_sem=recv1.at[rel - 1], device_id=to if sending else (x, y, c), device_id_type=MESH)

        def summed(rel, sending):
            to, p = _peer(rel)
            return pltpu.make_async_remote_copy(
                src_ref=mine, dst_ref=total.at[me if sending else p], send_sem=send2.at[rel - 1],
                recv_sem=recv2.at[rel - 1], device_id=to if sending else (x, y, c), device_id_type=MESH)

        own_slab = pltpu.make_async_copy(pack.at[me], land.at[me], local.at[0])
        own_sum = pltpu.make_async_copy(mine, total.at[me], local.at[1])

        def start():
            own_slab.start()
            for rel in range(1, 8):
                slab(rel, True).start()

        def forward():
            own_slab.wait()
            for rel in range(1, 8):
                slab(rel, False).wait_recv()
            acc = land[0]
            for b in range(1, NDEV):
                acc = acc + land[b]
            mine[...] = acc
            own_sum.start()
            for rel in range(1, 8):
                summed(rel, True).start()

        def finish():
            for rel in range(1, 8):
                summed(rel, False).wait_recv()
            for rel in range(1, 8):
                slab(rel, True).wait_send()
                summed(rel, True).wait_send()
            own_sum.wait()

        return start, forward, finish


class _Carried:
    def __init__(self, *parts):
        self.parts = parts
        self.operands = [o for p in parts for o in p.operands]
        self.n = len(self.operands)
        self.out_shape = [s for p in parts for s in p.out_shape]
        self.scratch = [s for p in parts for s in p.scratch]

    def plan(self, srcs, outs, scratch):
        plans, a, b = [], 0, 0
        for p in self.parts:
            plans.append(p.plan(srcs[a:a + p.n], outs[a:a + p.n], scratch[b:b + len(p.scratch)]))
            a, b = a + p.n, b + len(p.scratch)

        def every(k):
            def run():
                for plan in plans:
                    plan[k]()
            return run

        return every(0), every(1), every(2)


def _launch(body, name, out_shape, in_specs, out_specs, operands, scratch=(), grid=None, semantics=None,
            carry=None, steps=None):
    out_shape, in_specs, out_specs = list(out_shape), list(in_specs), list(out_specs)
    operands, scratch = list(operands), list(scratch)
    n_in, n_out, n_scr = len(in_specs), len(out_shape), len(scratch)
    kernel_body = body
    if carry is not None:
        k = carry.n

        def kernel_body(*refs):
            ins, cin = refs[:n_in], refs[n_in:n_in + k]
            outs, cout = refs[n_in + k:n_in + k + n_out], refs[n_in + k + n_out:n_in + 2 * k + n_out]
            rest = refs[n_in + 2 * k + n_out:]
            scr, csem = rest[:n_scr], rest[n_scr:]
            start, forward, finish = carry.plan(cin, cout, csem)
            if steps is None:
                start()
                body(*ins, *outs, *scr)
                forward()
                finish()
            else:
                pl.when(steps()[0])(start)
                pl.when(steps()[1])(forward)
                body(*ins, *outs, *scr)
                pl.when(steps()[2])(finish)

        in_specs += [_HBM] * k
        out_shape += carry.out_shape
        out_specs += [_HBM] * k
        operands += carry.operands
        scratch += carry.scratch
    kwargs = {} if grid is None else {"grid": grid}
    params = pltpu.CompilerParams(vmem_limit_bytes=VMEM_LIMIT) if semantics is None else _params(*semantics)
    return pl.pallas_call(kernel_body, name=name, out_shape=out_shape, in_specs=in_specs, out_specs=out_specs,
                          scratch_shapes=scratch, compiler_params=params, **kwargs)(*operands)


def _grid_steps(nt):
    def steps():
        i = pl.program_id(0)
        return i == 0, i == nt - 1, i == nt - 1
    return steps


def _cast_shards(shards):
    n = len(shards)

    def body(*refs):
        for a in range(n):
            refs[n + a][...] = refs[a][...].astype(WIRE_DTYPE)

    return pl.pallas_call(body, name="cast_shards",
                          out_shape=[jax.ShapeDtypeStruct(s.shape, WIRE_DTYPE) for s in shards],
                          compiler_params=pltpu.CompilerParams(vmem_limit_bytes=VMEM_LIMIT))(*shards)


_SEM = pl.BlockSpec(memory_space=pltpu.SEMAPHORE)
_DATAFLOW = pltpu.SideEffectType.DATAFLOW_SIDE_EFFECTING


def _split_copy(arrays, views, landing, send_sems, recv_sems, a, rel):
    to, p = _peer(rel)
    x, y, c = _me()
    return pltpu.make_async_remote_copy(
        src_ref=views[a](arrays[a], p), dst_ref=landing[a].at[4 * x + 2 * y + c],
        send_sem=send_sems.at[NDEV * a + rel], recv_sem=recv_sems.at[NDEV * a + rel], device_id=to, device_id_type=MESH)


def _scatter_start(arrays, views, shard_shapes, after):
    n = len(arrays)
    landing = [pltpu.with_memory_space_constraint(lax.empty((NDEV,) + tuple(s), a.dtype), pltpu.HBM)
               for s, a in zip(shard_shapes, arrays)]
    arrays = [pltpu.with_memory_space_constraint(a, pltpu.HBM) for a in arrays]

    def body(*refs):
        ins, land = refs[:n], refs[n:2 * n]
        send_sems, recv_sems = refs[2 * n + len(after)], refs[2 * n + len(after) + 1]
        token = refs[-1]
        for rel in range(NDEV):
            for a in range(n):
                _split_copy(ins, views, land, send_sems, recv_sems, a, rel).start()
        token[...] = jnp.zeros_like(token)

    res = pl.pallas_call(
        body, name="scatter_start",
        out_shape=[pltpu.SemaphoreType.DMA((NDEV * n,)), pltpu.SemaphoreType.DMA((NDEV * n,))]
        + [pltpu.HBM(a.shape, a.dtype) for a in arrays] + [pltpu.HBM(l.shape, l.dtype) for l in landing]
        + [jax.ShapeDtypeStruct((8, 128), F32)],
        in_specs=[_HBM_ONLY] * (2 * n) + [_HBM] * len(after),
        out_specs=[_SEM, _SEM] + [_HBM_ONLY] * (2 * n) + [pl.BlockSpec(memory_space=pltpu.VMEM)],
        input_output_aliases={i: 2 + i for i in range(2 * n)},
        compiler_params=pltpu.CompilerParams(has_side_effects=_DATAFLOW),
    )(*arrays, *landing, *after)
    return res[0], res[1], res[2:2 + n], res[2 + n:2 + 2 * n], res[-1]


def _scatter_wait(send_sems, recv_sems, arrays, landing, views, after):
    n = len(arrays)

    def body(*refs):
        ins, land = refs[:n], refs[n:2 * n]
        send, recv = refs[2 * n], refs[2 * n + 1]
        for rel in range(NDEV):
            for a in range(n):
                cp = _split_copy(ins, views, land, send, recv, a, rel)
                cp.wait_send()
                cp.wait_recv()

    res = pl.pallas_call(
        body, name="scatter_wait",
        out_shape=[pltpu.HBM(a.shape, a.dtype) for a in arrays] + [pltpu.HBM(l.shape, l.dtype) for l in landing],
        in_specs=[_HBM_ONLY] * (2 * n) + [_SEM, _SEM] + [_HBM] * len(after),
        out_specs=[_HBM_ONLY] * (2 * n),
        input_output_aliases={i: i for i in range(2 * n)},
        compiler_params=pltpu.CompilerParams(has_side_effects=_DATAFLOW),
    )(*arrays, *landing, send_sems, recv_sems, *after)
    return res[n:]


def _ada_forward(c_row, w_ada, b_ada8, carry):
    cols = w_ada.shape[1]

    def body(c_ref, w_ref, b_ref, mod_ref, sc_ref, c_all, send_buf, recv_buf, send1, recv1, send2, recv2):
        x, y, c = _me()
        me = 4 * x + 2 * y + c
        rowi = lax.broadcasted_iota(jnp.int32, (8, D_MODEL), 0)
        c_all[me] = jnp.broadcast_to(c_ref[...], (8, D_MODEL))
        copies = []
        for rel in range(1, 8):
            to, _ = _peer(rel)
            cp = pltpu.make_async_remote_copy(src_ref=c_all.at[me], dst_ref=c_all.at[me], send_sem=send1.at[rel - 1],
                                              recv_sem=recv1.at[rel - 1], device_id=to, device_id_type=MESH)
            cp.start()
            copies.append(cp)
        for rel in range(1, 8):
            _, p = _peer(rel)
            pltpu.make_async_remote_copy(src_ref=c_all.at[p], dst_ref=c_all.at[p], send_sem=send1.at[rel - 1],
                                         recv_sem=recv1.at[rel - 1], device_id=(x, y, c), device_id_type=MESH).wait_recv()
        for cp in copies:
            cp.wait_send()
        cmat = jnp.zeros((8, D_MODEL), F32)
        for b in range(8):
            cmat = jnp.where(rowi == b, c_all[b], cmat)
        sc = cmat * _sigmoid(cmat)
        sc_ref[...] = sc
        modcols = _mm(sc, w_ref[...]) + b_ref[pl.ds(me, 1), :]
        for b in range(8):
            send_buf[b] = jnp.broadcast_to(modcols[b:b + 1, :], (8, cols))
        recv_buf[me] = send_buf[me]
        copies = []
        for rel in range(1, 8):
            to, p = _peer(rel)
            cp = pltpu.make_async_remote_copy(src_ref=send_buf.at[p], dst_ref=recv_buf.at[me], send_sem=send2.at[rel - 1],
                                              recv_sem=recv2.at[rel - 1], device_id=to, device_id_type=MESH)
            cp.start()
            copies.append(cp)
        for rel in range(1, 8):
            _, p = _peer(rel)
            pltpu.make_async_remote_copy(src_ref=send_buf.at[p], dst_ref=recv_buf.at[p], send_sem=send2.at[rel - 1],
                                         recv_sem=recv2.at[rel - 1], device_id=(x, y, c), device_id_type=MESH).wait_recv()
        for cp in copies:
            cp.wait_send()
        rowc = lax.broadcasted_iota(jnp.int32, (8, cols), 0)
        out = jnp.zeros((8, cols), F32)
        for k in range(8):
            out = jnp.where(rowc == k, recv_buf[k], out)
        mod_ref[...] = out

    return _launch(
        body, "ada_forward",
        out_shape=[jax.ShapeDtypeStruct((8, cols), F32), jax.ShapeDtypeStruct((8, D_MODEL), F32)],
        in_specs=[pl.BlockSpec(memory_space=pltpu.VMEM)] * 3,
        out_specs=[pl.BlockSpec(memory_space=pltpu.VMEM)] * 2,
        operands=(c_row, w_ada, b_ada8),
        scratch=[pltpu.VMEM((8, 8, D_MODEL), F32), pltpu.VMEM((8, 8, cols), F32), pltpu.VMEM((8, 8, cols), F32)]
        + [pltpu.SemaphoreType.DMA((7,))] * 4,
        carry=carry)


def _allreduce_small(pack, order, name):
    rows = pack.shape[1]

    def body(pack_ref, order_ref, total_ref, land_ref, send1, recv1, send2, recv2):
        x, y, c = _me()
        me = 4 * x + 2 * y + c
        land_ref[me] = pack_ref[me]
        copies = []
        for rel in range(1, 8):
            to, p = _peer(rel)
            cp = pltpu.make_async_remote_copy(src_ref=pack_ref.at[p], dst_ref=land_ref.at[me], send_sem=send1.at[rel - 1],
                                              recv_sem=recv1.at[rel - 1], device_id=to, device_id_type=MESH)
            cp.start()
            copies.append(cp)
        for rel in range(1, 8):
            _, p = _peer(rel)
            pltpu.make_async_remote_copy(src_ref=pack_ref.at[p], dst_ref=land_ref.at[p], send_sem=send1.at[rel - 1],
                                         recv_sem=recv1.at[rel - 1], device_id=(x, y, c), device_id_type=MESH).wait_recv()
        for cp in copies:
            cp.wait_send()
        acc = land_ref[0]
        for b in range(1, 8):
            acc = acc + land_ref[b]
        total_ref[me] = acc
        copies = []
        for rel in range(1, 8):
            to, _ = _peer(rel)
            cp = pltpu.make_async_remote_copy(src_ref=total_ref.at[me], dst_ref=total_ref.at[me], send_sem=send2.at[rel - 1],
                                              recv_sem=recv2.at[rel - 1], device_id=to, device_id_type=MESH)
            cp.start()
            copies.append(cp)
        for rel in range(1, 8):
            _, p = _peer(rel)
            pltpu.make_async_remote_copy(src_ref=total_ref.at[p], dst_ref=total_ref.at[p], send_sem=send2.at[rel - 1],
                                         recv_sem=recv2.at[rel - 1], device_id=(x, y, c), device_id_type=MESH).wait_recv()
        for cp in copies:
            cp.wait_send()

    return pl.pallas_call(
        body, name=name,
        out_shape=[jax.ShapeDtypeStruct((8, rows, 128), F32), jax.ShapeDtypeStruct((8, rows, 128), F32)],
        in_specs=[pl.BlockSpec(memory_space=pltpu.VMEM), _HBM],
        out_specs=[pl.BlockSpec(memory_space=pltpu.VMEM)] * 2,
        scratch_shapes=[pltpu.SemaphoreType.DMA((7,))] * 4,
        compiler_params=pltpu.CompilerParams(vmem_limit_bytes=VMEM_LIMIT),
    )(pack, order)


def _modulated(x, prm_ref, sub, g_row):
    shift, scale = _row(prm_ref, 3 * sub), _row(prm_ref, 3 * sub + 1)
    g = _row(prm_ref, g_row)
    r = _rms_scale(x)
    n0 = x * r
    return (n0 * g) * (1.0 + scale) + shift, r, n0


def _swiglu_tile(xv, prm_ref, win_ref, wout_ref, ab_ref, sub, g_row):
    h, _, _ = _modulated(xv, prm_ref, sub, g_row)
    hb = h.astype(MXU_DTYPE)
    acc = None
    for j in range(4):
        a = _mm_nt(hb, win_ref[0, j])
        b = _mm_nt(hb, win_ref[1, j])
        ab_ref[0, j] = a.astype(SAVE_DTYPE)
        ab_ref[1, j] = b.astype(SAVE_DTYPE)
        t = _mm((a * _sigmoid(a)) * b, wout_ref[j])
        acc = t if acc is None else acc + t
    return acc


def _loss_tile(xv, target, g):
    r = _rms_scale(xv)
    n0 = xv * r
    err = n0 * g - target
    dy = err / float(D_MODEL)
    dn0 = dy * g
    dx = r * (dn0 - n0 * jnp.mean(dn0 * n0, axis=-1, keepdims=True))
    loss = 0.5 * jnp.sum(jnp.mean(err * err, axis=-1, keepdims=True), axis=0, keepdims=True)
    return dx, _colsum(dy * n0), loss


def _ffn_forward_loss(x, target, prm, win, wout, sub, g_row, name):
    T = x.shape[0]
    tm = min(T, TM_FFN)

    def body(x_ref, t_ref, prm_ref, win_ref, wout_ref, dx_ref, sums_ref, f_ref, ab_ref):
        i = pl.program_id(0)
        xv = x_ref[...]
        acc = _swiglu_tile(xv, prm_ref, win_ref, wout_ref, ab_ref, sub, g_row)
        f_ref[...] = acc.astype(SAVE_DTYPE)
        dx, dg, loss = _loss_tile(xv + (0.5 * _row(prm_ref, 3 * sub + 2)) * acc, t_ref[...], _row(prm_ref, ROW_G_FINAL))
        dx_ref[...] = dx
        upd = jnp.concatenate([dg, jnp.broadcast_to(loss, (1, D_MODEL)), jnp.zeros((6, D_MODEL), F32)], axis=0)

        @pl.when(i == 0)
        def _():
            sums_ref[...] = upd

        @pl.when(i > 0)
        def _():
            sums_ref[...] += upd

    tok = pl.BlockSpec((tm, D_MODEL), lambda i: (i, 0))
    return _launch(
        body, name, grid=(T // tm,), semantics=("arbitrary",),
        out_shape=[jax.ShapeDtypeStruct((T, D_MODEL), F32), jax.ShapeDtypeStruct((8, D_MODEL), F32),
                   jax.ShapeDtypeStruct((T, D_MODEL), SAVE_DTYPE), jax.ShapeDtypeStruct((2, 4, T, FF_SHARD), SAVE_DTYPE)],
        in_specs=[tok, tok, _resident(prm), _resident(win), _resident(wout)],
        out_specs=[tok, pl.BlockSpec((8, D_MODEL), lambda i: (0, 0)), tok,
                   pl.BlockSpec((2, 4, tm, FF_SHARD), lambda i: (0, 0, i, 0))],
        operands=(x, target, prm, win, wout))


def _ffn_hidden(x, prm, win, sub, g_row, name, carry=None):
    T = x.shape[0]
    tm = min(T, TM_FFN)

    def body(x_ref, prm_ref, win_ref, ab_ref, s_ref):
        h, _, _ = _modulated(x_ref[...], prm_ref, sub, g_row)
        hb = h.astype(MXU_DTYPE)
        for j in range(4):
            a = _mm_nt(hb, win_ref[0, j])
            b = _mm_nt(hb, win_ref[1, j])
            ab_ref[0, j] = a.astype(SAVE_DTYPE)
            ab_ref[1, j] = b.astype(SAVE_DTYPE)
            s_ref[j] = ((a * _sigmoid(a)) * b).astype(MXU_DTYPE)

    return _launch(
        body, name, grid=(T // tm,), semantics=("arbitrary",),
        out_shape=[jax.ShapeDtypeStruct((2, 4, T, FF_SHARD), SAVE_DTYPE), jax.ShapeDtypeStruct((4, T, FF_SHARD), MXU_DTYPE)],
        in_specs=[pl.BlockSpec((tm, D_MODEL), lambda i: (i, 0)), _resident(prm), _resident(win)],
        out_specs=[pl.BlockSpec((2, 4, tm, FF_SHARD), lambda i: (0, 0, i, 0)),
                   pl.BlockSpec((4, tm, FF_SHARD), lambda i: (0, i, 0))],
        operands=(x, prm, win), carry=carry, steps=_grid_steps(T // tm))


def _ffn_out(x, s, prm, wout, sub, name, carry=None):
    T = x.shape[0]
    tm = min(T, TM_FFN)

    def body(x_ref, s_ref, prm_ref, wout_ref, xo_ref, f_ref):
        acc = None
        for j in range(4):
            t = _mm(s_ref[j], wout_ref[j])
            acc = t if acc is None else acc + t
        f_ref[...] = acc.astype(SAVE_DTYPE)
        xo_ref[...] = x_ref[...] + (0.5 * _row(prm_ref, 3 * sub + 2)) * acc

    tok = pl.BlockSpec((tm, D_MODEL), lambda i: (i, 0))
    return _launch(
        body, name, grid=(T // tm,), semantics=("arbitrary",),
        out_shape=[jax.ShapeDtypeStruct((T, D_MODEL), F32), jax.ShapeDtypeStruct((T, D_MODEL), SAVE_DTYPE)],
        in_specs=[tok, pl.BlockSpec((4, tm, FF_SHARD), lambda i: (0, i, 0)), _resident(prm), _resident(wout)],
        out_specs=[tok, tok], operands=(x, s, prm, wout), carry=carry, steps=_grid_steps(T // tm))


def _ffn_backward(d, ab, prm, win, wout, sub, name, carry=None):
    T = d.shape[0]
    tm = min(T, TM_FFN)
    nt = T // tm
    chunk = min(tm, FFN_BWD_CHUNK)

    def body(d_ref, ab_ref, prm_ref, win_ref, wout_ref, dh_ref, dab_ref, dwout_ref, acc_out):
        i = pl.program_id(1)

        @pl.when(i == 0)
        def _():
            acc_out[...] = jnp.zeros_like(acc_out)

        wa, wb, wo = win_ref[0, 0], win_ref[1, 0], wout_ref[0]
        half_gate = 0.5 * _row(prm_ref, 3 * sub + 2)
        ss, dfss = [], []
        for ck in range(tm // chunk):
            rows = slice(ck * chunk, (ck + 1) * chunk)
            a = ab_ref[0, 0, rows, :].astype(F32)
            b = ab_ref[1, 0, rows, :].astype(F32)
            sg = _sigmoid(a)
            si = a * sg
            dfs = (half_gate * d_ref[rows, :]).astype(MXU_DTYPE)
            ds = _mm_nt(dfs, wo)
            da = (ds * b * (sg * (1.0 + a * (1.0 - sg)))).astype(MXU_DTYPE)
            db = (ds * si).astype(MXU_DTYPE)
            dh_ref[0, rows, :] = (_mm(da, wa) + _mm(db, wb)).astype(SAVE_DTYPE)
            dab_ref[0, 0, rows, :] = da
            dab_ref[1, 0, rows, :] = db
            ss.append((si * b).astype(MXU_DTYPE))
            dfss.append(dfs)
        cat = (lambda v: v[0]) if len(ss) == 1 else (lambda v: jnp.concatenate(v, axis=0))
        acc_out[...] += _mm_tn(cat(ss), cat(dfss))

        @pl.when(i == nt - 1)
        def _():
            dwout_ref[0] = acc_out[...].astype(WIRE_DTYPE)

    def steps():
        j, i = pl.program_id(0), pl.program_id(1)
        return (j == 0) & (i == 0), (j == 2) & (i == 0), (j == 3) & (i == nt - 1)

    pre = pl.BlockSpec((2, 1, tm, FF_SHARD), lambda j, i: (0, j, i, 0))
    return _launch(
        body, name, grid=(4, nt), semantics=("arbitrary", "arbitrary"),
        out_shape=[jax.ShapeDtypeStruct((4, T, D_MODEL), SAVE_DTYPE), jax.ShapeDtypeStruct(ab.shape, MXU_DTYPE),
                   jax.ShapeDtypeStruct(wout.shape, WIRE_DTYPE)],
        in_specs=[pl.BlockSpec((tm, D_MODEL), lambda j, i: (i, 0)), pre, _resident(prm),
                  pl.BlockSpec((2, 1, FF_SHARD, D_MODEL), lambda j, i: (0, j, 0, 0)),
                  pl.BlockSpec((1, FF_SHARD, D_MODEL), lambda j, i: (j, 0, 0))],
        out_specs=[pl.BlockSpec((1, tm, D_MODEL), lambda j, i: (j, i, 0)), pre,
                   pl.BlockSpec((1, FF_SHARD, D_MODEL), lambda j, i: (j, 0, 0))],
        operands=(d, ab, prm, win, wout), scratch=[pltpu.VMEM((FF_SHARD, D_MODEL), F32)], carry=carry, steps=steps)


def _ffn_dwin(x, dab, parts, d, f, prm, sub, g_row, name, carry=None):
    T = x.shape[0]
    tm = min(T, TM_FFN)
    nt = T // tm
    P = parts.shape[0]

    def body(x_ref, dab_ref, p_ref, d_ref, f_ref, prm_ref, dwin_ref, dx_ref, sums_ref, acc):
        j, i = pl.program_id(0), pl.program_id(1)

        @pl.when(i == 0)
        def _():
            acc[...] = jnp.zeros_like(acc)

        xv = x_ref[...]
        h, _, _ = _modulated(xv, prm_ref, sub, g_row)
        hb = h.astype(MXU_DTYPE)
        acc[0] += _mm_tn(dab_ref[0, 0], hb)
        acc[1] += _mm_tn(dab_ref[1, 0], hb)

        @pl.when(j == 0)
        def _():
            dh = p_ref[0].astype(F32)
            for k in range(1, P):
                dh = dh + p_ref[k].astype(F32)
            dx_ref[...], upd = _norm_backward_tile(dh, xv, d_ref[...], f_ref[...], prm_ref, sub, g_row, 0.5)

            @pl.when(i == 0)
            def _():
                sums_ref[...] = upd

            @pl.when(i > 0)
            def _():
                sums_ref[...] += upd

        @pl.when(i == nt - 1)
        def _():
            dwin_ref[0, 0] = acc[0].astype(WIRE_DTYPE)
            dwin_ref[1, 0] = acc[1].astype(WIRE_DTYPE)

    def steps():
        j, i = pl.program_id(0), pl.program_id(1)
        return (j == 0) & (i == 0), (j == 2) & (i == 0), (j == 3) & (i == nt - 1)

    def first_pass(j, i):
        return jnp.where(j == 0, i, nt - 1)

    tok = pl.BlockSpec((tm, D_MODEL), lambda j, i: (i, 0))
    once = pl.BlockSpec((tm, D_MODEL), lambda j, i: (first_pass(j, i), 0))
    return _launch(
        body, name, grid=(4, nt), semantics=("arbitrary", "arbitrary"),
        out_shape=[jax.ShapeDtypeStruct((2, 4, FF_SHARD, D_MODEL), WIRE_DTYPE), jax.ShapeDtypeStruct((T, D_MODEL), F32),
                   jax.ShapeDtypeStruct((8, D_MODEL), F32)],
        in_specs=[tok, pl.BlockSpec((2, 1, tm, FF_SHARD), lambda j, i: (0, j, i, 0)),
                  pl.BlockSpec((P, tm, D_MODEL), lambda j, i: (0, first_pass(j, i), 0)), once, once, _resident(prm)],
        out_specs=[pl.BlockSpec((2, 1, FF_SHARD, D_MODEL), lambda j, i: (0, j, 0, 0)), once,
                   pl.BlockSpec((8, D_MODEL), lambda j, i: (0, 0))],
        operands=(x, dab, parts, d, f, prm), scratch=[pltpu.VMEM((2, FF_SHARD, D_MODEL), F32)], carry=carry, steps=steps)


def _norm_backward_tile(dh, xv, dv, fv, prm_ref, sub, g_row, gate_coef):
    scale, g = _row(prm_ref, 3 * sub + 1), _row(prm_ref, g_row)
    r = _rms_scale(xv)
    n0 = xv * r
    dn = dh * (1.0 + scale)
    dn0 = dn * g
    dx = dv + r * (dn0 - n0 * jnp.mean(dn0 * n0, axis=-1, keepdims=True))
    upd = jnp.concatenate([_colsum(dn * n0), _colsum(dh), _colsum(dh * (n0 * g)),
                           gate_coef * _colsum(dv * fv.astype(F32)), jnp.zeros((4, D_MODEL), F32)], axis=0)
    return dx, upd


def _ssm_discretise(lam_re_log, lam_im, log_dt):
    lr = -jnp.exp(lam_re_log)
    dt = jnp.exp(log_dt)
    mag = jnp.exp(lr * dt)
    ang = lam_im * dt
    ab_re = mag * jnp.cos(ang)
    ab_im = mag * jnp.sin(ang)
    num_re = ab_re - 1.0
    num_im = ab_im
    den = lr * lr + lam_im * lam_im
    f_re = (num_re * lr + num_im * lam_im) / den
    f_im = (num_im * lr - num_re * lam_im) / den
    return ab_re, ab_im, f_re, f_im


def _ssm_params_forward(lam_re_log, lam_im, log_dt):
    def body(a_ref, b_ref, c_ref, o0, o1, o2, o3):
        outs = _ssm_discretise(a_ref[...], b_ref[...], c_ref[...])
        for o, v in zip((o0, o1, o2, o3), outs):
            o[...] = v

    return pl.pallas_call(body, name="ssm_params_forward",
                          out_shape=[jax.ShapeDtypeStruct(lam_im.shape, F32)] * 4)(lam_re_log, lam_im, log_dt)


def _ssm_params_backward(lam_re_log, lam_im, log_dt, cot):
    def body(a_ref, b_ref, c_ref, g0, g1, g2, g3, o0, o1, o2):
        _, vjp = jax.vjp(_ssm_discretise, a_ref[...], b_ref[...], c_ref[...])
        d0, d1, d2 = vjp((g0[...], g1[...], g2[...], g3[...]))
        o0[...] = d0
        o1[...] = d1
        o2[...] = d2

    return pl.pallas_call(
        body, name="ssm_params_backward",
        out_shape=[jax.ShapeDtypeStruct(lam_im.shape, F32), jax.ShapeDtypeStruct(lam_im.shape, F32),
                   jax.ShapeDtypeStruct(log_dt.shape, F32)])(lam_re_log, lam_im, log_dt, *cot)


def _ssm_dense_forward(srow, b_dense, c_dense):
    def body(srow_ref, bd_ref, cd_ref, bb_ref, ct_ref):
        for j in range(SSM_BLOCKS):
            lanes = slice(j * SSM_BLOCK_STATE, (j + 1) * SSM_BLOCK_STATE)
            f_re, f_im = srow_ref[2:3, lanes], srow_ref[3:4, lanes]
            bb_ref[0, j] = (f_re * bd_ref[0, j] - f_im * bd_ref[1, j]).astype(MXU_DTYPE)
            bb_ref[1, j] = (f_re * bd_ref[1, j] + f_im * bd_ref[0, j]).astype(MXU_DTYPE)
            ct_ref[0, j] = cd_ref[0, j].astype(MXU_DTYPE)
            ct_ref[1, j] = cd_ref[1, j].astype(MXU_DTYPE)

    return pl.pallas_call(body, name="ssm_dense_forward",
                          out_shape=[jax.ShapeDtypeStruct(b_dense.shape, MXU_DTYPE),
                                     jax.ShapeDtypeStruct(c_dense.shape, MXU_DTYPE)],
                          compiler_params=pltpu.CompilerParams(vmem_limit_bytes=VMEM_LIMIT))(srow, b_dense, c_dense)


def _cmul(p, q):
    return p[0] * q[0] - p[1] * q[1], p[0] * q[1] + p[1] * q[0]


def _scan_coefficients(ar, ai, reverse):
    n = ar.shape[1]
    p = {1: (ar, ai)}
    p[2] = _cmul(p[1], p[1])
    p[3] = _cmul(p[2], p[1])
    p[4] = _cmul(p[2], p[2])
    p[5] = _cmul(p[4], p[1])
    p[6] = _cmul(p[4], p[2])
    p[7] = _cmul(p[4], p[3])
    p[8] = _cmul(p[4], p[4])
    rowi = lax.broadcasted_iota(jnp.int32, (SCAN_ROWS, n), 0)
    tiles = []
    for dstep in (1, 2, 4):
        keep = (rowi < SCAN_ROWS - dstep) if reverse else (rowi >= dstep)
        for part in p[dstep]:
            tiles.append(jnp.where(keep, jnp.broadcast_to(part, (SCAN_ROWS, n)), 0.0))
    for comp in (0, 1):
        t = jnp.zeros((SCAN_ROWS, n), F32)
        for rr in range(SCAN_ROWS):
            power = SCAN_ROWS - rr if reverse else rr + 1
            t = jnp.where(rowi == rr, jnp.broadcast_to(p[power][comp], (SCAN_ROWS, n)), t)
        tiles.append(t)
    return tiles


def _load_stack(stack_hbm, dst, sems, base):
    cols = stack_hbm.shape[2]
    cps = [pltpu.make_async_copy(stack_hbm.at[k], dst.at[:, pl.ds(k * cols, cols)], sems.at[base + k])
           for k in range(NDEV)]
    for cp in cps:
        cp.start()
    return cps


def _window_lanes():
    lane = lax.broadcasted_iota(jnp.int32, (1, POOL_WIDTH), 1)
    return jnp.where(lane < 128, 2.0, jnp.where(lane < 256, 4.0, jnp.where(lane < 384, 8.0, 16.0)))


def _gelu(y):
    return 0.5 * y * (1.0 + lax.erf(y * 0.7071067811865476))


def _gelu_grad(y):
    return 0.5 * (1.0 + lax.erf(y * 0.7071067811865476)) + y * jnp.exp(-0.5 * y * y) * 0.3989422804014327


def _mixer_forward(x, prm, w_in_s, w_pu_s, w_glu_s, w_su_s, w_out, pool_w, mvec, srow, bb, ct, carry=None):
    T = x.shape[0]
    tm = min(T, TM_MIX)
    nt = T // tm
    n_tiles = tm // SCAN_ROWS

    def body(x_ref, prm_ref, w_in_h, w_pu_h, w_glu_h, w_su_h, w_out_h, pw_ref, mv_ref, srow_ref, bb, ct,
             x2_ref, mo_ref, z_ref, sre_ref, sim_ref, zp_ref, q_ref, yp_ref, yss_ref, vg_ref, ys_ref,
             w_in, w_pu, w_glu, w_su, w_o, coef, carry, hist, bu, sems):
        i = pl.program_id(0)

        @pl.when(i == 0)
        def _():
            cps = (_load_stack(w_in_h, w_in, sems, 0) + _load_stack(w_pu_h, w_pu, sems, 8)
                   + _load_stack(w_glu_h, w_glu, sems, 16) + _load_stack(w_su_h, w_su, sems, 24))
            cps.append(pltpu.make_async_copy(w_out_h, w_o, sems.at[32]))
            cps[-1].start()
            for j in range(SSM_BLOCKS):
                lanes = slice(j * SSM_BLOCK_STATE, (j + 1) * SSM_BLOCK_STATE)
                for k, t in enumerate(_scan_coefficients(srow_ref[0:1, lanes], srow_ref[1:2, lanes], False)):
                    coef[j, k] = t
            carry[...] = jnp.zeros_like(carry)
            hist[...] = jnp.zeros_like(hist)
            for cp in cps:
                cp.wait()

        xv = x_ref[...]
        h, _, _ = _modulated(xv, prm_ref, 1, ROW_G_MIX)
        z = _mm(h, w_in[...])
        z_ref[...] = z.astype(SAVE_DTYPE)
        u_pool, u_ssm = z[:, 0:512], z[:, 512:1024]
        gl_pool, gl_ssm = z[:, 1024:2048], z[:, 2048:3072]

        ext = jnp.concatenate([hist[...], u_pool], axis=0)
        w2 = ext + pltpu.roll(ext, 1, 0)
        w4 = w2[:, 128:] + pltpu.roll(w2[:, 128:], 2, 0)
        w8 = w4[:, 128:] + pltpu.roll(w4[:, 128:], 4, 0)
        w16 = w8[:, 128:] + pltpu.roll(w8[:, 128:], 8, 0)
        wsum = jnp.concatenate([w2[POOL_HALO:, :128], w4[POOL_HALO:, :128], w8[POOL_HALO:, :128], w16[POOL_HALO:]], axis=1)
        hist[...] = u_pool[tm - POOL_HALO:, :]
        t1 = (lax.broadcasted_iota(jnp.int32, (tm, 1), 0) + (i * tm + 1)).astype(F32)
        zp = wsum / jnp.minimum(t1, _window_lanes()) - u_pool
        zp_ref[...] = zp.astype(SAVE_DTYPE)
        q = jnp.concatenate([_mm(zp[:, k * 128:(k + 1) * 128], pw_ref[k]) for k in range(4)], axis=1)
        q = q + mv_ref[ROW_POOL_B:ROW_POOL_B + 1, 0:512]
        q_ref[...] = q.astype(SAVE_DTYPE)
        y_pool = _mm(q * mv_ref[ROW_POOL_SCALE:ROW_POOL_SCALE + 1, 0:512], w_pu[...])
        yp_ref[...] = y_pool.astype(SAVE_DTYPE)

        y_blocks = []
        for j in range(SSM_BLOCKS):
            lanes = pl.ds(j * SSM_BLOCK_STATE, SSM_BLOCK_STATE)
            ub = u_ssm[:, j * 128:(j + 1) * 128].astype(MXU_DTYPE)
            bu[0] = _mm(ub, bb[0, j])
            bu[1] = _mm(ub, bb[1, j])
            a1r, a1i, a2r, a2i, a4r, a4i, pr, pi = [coef[j, k] for k in range(8)]

            def step(tt, c, lanes=lanes, a1r=a1r, a1i=a1i, a2r=a2r, a2i=a2i, a4r=a4r, a4i=a4i, pr=pr, pi=pi):
                cr, ci = c
                rows = pl.ds(pl.multiple_of(tt * SCAN_ROWS, SCAN_ROWS), SCAN_ROWS)
                xr, xi = bu[0, rows, :], bu[1, rows, :]
                for dstep, kr, ki in ((1, a1r, a1i), (2, a2r, a2i), (4, a4r, a4i)):
                    sr, si = pltpu.roll(xr, dstep, 0), pltpu.roll(xi, dstep, 0)
                    xr, xi = xr + kr * sr - ki * si, xi + kr * si + ki * sr
                xr, xi = xr + pr * cr - pi * ci, xi + pr * ci + pi * cr
                sre_ref[rows, lanes] = xr
                sim_ref[rows, lanes] = xi
                return (jnp.broadcast_to(xr[SCAN_ROWS - 1:SCAN_ROWS, :], xr.shape),
                        jnp.broadcast_to(xi[SCAN_ROWS - 1:SCAN_ROWS, :], xi.shape))

            cr, ci = lax.fori_loop(0, n_tiles, step, (carry[j, 0], carry[j, 1]))
            carry[j, 0] = cr
            carry[j, 1] = ci
            y_blocks.append(_mm(sre_ref[:, lanes], ct[0, j]) - _mm(sim_ref[:, lanes], ct[1, j]))
        yss = jnp.concatenate(y_blocks, axis=1) + mv_ref[ROW_SSM_D:ROW_SSM_D + 1, 0:512] * u_ssm
        yss_ref[...] = yss.astype(SAVE_DTYPE)
        vg = _mm(_gelu(yss), w_glu[...]) + mv_ref[ROW_B_GLU:ROW_B_GLU + 1, :]
        vg_ref[...] = vg.astype(SAVE_DTYPE)
        y_ssm = _mm(vg[:, 0:512] * _sigmoid(vg[:, 512:1024]), w_su[...])
        ys_ref[...] = y_ssm.astype(SAVE_DTYPE)

        merged = _sigmoid(gl_pool) * y_pool + _sigmoid(gl_ssm) * y_ssm
        mo = _mm(merged, w_o[...])
        mo_ref[...] = mo.astype(SAVE_DTYPE)
        x2_ref[...] = xv + _row(prm_ref, 5) * mo

    def tok(width):
        return pl.BlockSpec((tm, width), lambda i: (i, 0))

    hbm = _HBM
    widths = (D_MODEL, D_MODEL, IN_WIDTH, N_STATE, N_STATE, 512, 512, D_MODEL, 512, D_MODEL, D_MODEL)
    dtypes = (F32, SAVE_DTYPE, SAVE_DTYPE, F32, F32) + (SAVE_DTYPE,) * 6
    return _launch(
        body, "mixer_forward", grid=(nt,), semantics=("arbitrary",), carry=carry, steps=_grid_steps(nt),
        out_shape=[jax.ShapeDtypeStruct((T, w), dt) for w, dt in zip(widths, dtypes)],
        in_specs=[tok(D_MODEL), _resident(prm), hbm, hbm, hbm, hbm, hbm, _resident(pool_w), _resident(mvec),
                  _resident(srow), _resident(bb), _resident(ct)],
        out_specs=[tok(w) for w in widths],
        operands=(x, prm, w_in_s, w_pu_s, w_glu_s, w_su_s, w_out, pool_w, mvec, srow, bb, ct),
        scratch=[
            pltpu.VMEM((D_MODEL, IN_WIDTH), MXU_DTYPE), pltpu.VMEM((512, D_MODEL), MXU_DTYPE),
            pltpu.VMEM((512, D_MODEL), MXU_DTYPE), pltpu.VMEM((512, D_MODEL), MXU_DTYPE),
            pltpu.VMEM((D_MODEL, D_MODEL), MXU_DTYPE),
            pltpu.VMEM((SSM_BLOCKS, 8, SCAN_ROWS, SSM_BLOCK_STATE), F32),
            pltpu.VMEM((SSM_BLOCKS, 2, SCAN_ROWS, SSM_BLOCK_STATE), F32),
            pltpu.VMEM((POOL_HALO, POOL_WIDTH), F32),
            pltpu.VMEM((2, tm, SSM_BLOCK_STATE), F32),
            pltpu.SemaphoreType.DMA((33,)),
        ])


def _mixer_backward(d2, prm, saved, w_pu_s, w_glu_s, w_su_s, w_out, pool_w, mvec, srow, bb, ct, carry=None):
    z, s_re, s_im, zp, q, y_pool, yss, vg, y_ssm = saved
    T = d2.shape[0]
    tm = min(T, TM_MIX_BWD)
    nt = T // tm
    n_tiles = tm // SCAN_ROWS

    def body(d_ref, prm_ref, z_ref, sre_ref, sim_ref, zp_ref, q_ref, yp_ref, yss_ref, vg_ref, ys_ref,
             w_pu_h, w_glu_h, w_su_h, w_out_h, pw_ref, mv_ref, srow_ref, bb, ct,
             dz_ref, dwo_h, dwpu_h, dwglu_h, dwsu_h, dpw_h, dbb_h, dct_h, vsum_h, da_h,
             w_pu, w_glu, w_su, w_o, pwb, coef, carry, hist, dre, lam,
             a_wo, a_wpu, a_wglu, a_wsu, a_pw, a_bb, a_ct, a_vs, a_da, st_wo, st_up, sems):
        i = pl.program_id(0)
        tile = nt - 1 - i

        @pl.when(i == 0)
        def _():
            cps = (_load_stack(w_pu_h, w_pu, sems, 0) + _load_stack(w_glu_h, w_glu, sems, 8)
                   + _load_stack(w_su_h, w_su, sems, 16))
            cps.append(pltpu.make_async_copy(w_out_h, w_o, sems.at[24]))
            cps[-1].start()
            pwb[...] = pw_ref[...].astype(MXU_DTYPE)
            for j in range(SSM_BLOCKS):
                lanes = slice(j * SSM_BLOCK_STATE, (j + 1) * SSM_BLOCK_STATE)
                for k, t in enumerate(_scan_coefficients(srow_ref[0:1, lanes], srow_ref[1:2, lanes], True)):
                    coef[j, k] = t
            for acc in (carry, hist, a_wo, a_wpu, a_wglu, a_wsu, a_pw, a_bb, a_ct, a_vs, a_da):
                acc[...] = jnp.zeros_like(acc)
            for cp in cps:
                cp.wait()

        dv = d_ref[...]
        zt = z_ref[...].astype(F32)
        u_ssm, gl_pool, gl_ssm = zt[:, 512:1024], zt[:, 1024:2048], zt[:, 2048:3072]
        y_p, y_s = yp_ref[...].astype(F32), ys_ref[...].astype(F32)
        sgp, sgs = _sigmoid(gl_pool), _sigmoid(gl_ssm)
        dmo = (_row(prm_ref, 5) * dv).astype(MXU_DTYPE)
        a_wo[...] += _mm_tn(sgp * y_p + sgs * y_s, dmo)
        dmerged = _mm_nt(dmo, w_o[...])
        dy_pool = dmerged * sgp
        dgl_pool = dmerged * y_p * (sgp * (1.0 - sgp))
        dy_ssm = dmerged * sgs
        dgl_ssm = dmerged * y_s * (sgs * (1.0 - sgs))

        scale = mv_ref[ROW_POOL_SCALE:ROW_POOL_SCALE + 1, 0:512]
        qv, zpv = q_ref[...].astype(F32), zp_ref[...]
        a_wpu[...] += _mm_tn(qv * scale, dy_pool)
        dp = _mm_nt(dy_pool, w_pu[...])
        dq = dp * scale
        a_vs[0:1, 0:512] += _colsum(dp * qv)
        a_vs[1:2, 0:512] += _colsum(dq)
        dzp_blocks = []
        for k in range(4):
            lanes = slice(k * 128, (k + 1) * 128)
            dzp_blocks.append(_mm_nt(dq[:, lanes], pwb[k]))
            a_pw[k] += _mm_tn(zpv[:, lanes], dq[:, lanes])
        dzp = jnp.concatenate(dzp_blocks, axis=1)
        t1 = (lax.broadcasted_iota(jnp.int32, (tm, 1), 0) + (tile * tm + 1)).astype(F32)
        gs = dzp / jnp.minimum(t1, _window_lanes())
        n_ext = tm + POOL_HALO
        ext = jnp.concatenate([gs, hist[...]], axis=0)
        v2 = ext + pltpu.roll(ext, n_ext - 1, 0)
        v4 = v2[:, 128:] + pltpu.roll(v2[:, 128:], n_ext - 2, 0)
        v8 = v4[:, 128:] + pltpu.roll(v4[:, 128:], n_ext - 4, 0)
        v16 = v8[:, 128:] + pltpu.roll(v8[:, 128:], n_ext - 8, 0)
        msum = jnp.concatenate([v2[:tm, :128], v4[:tm, :128], v8[:tm, :128], v16[:tm]], axis=1)
        hist[...] = gs[0:POOL_HALO, :]
        du_pool = msum - dzp

        vgv = vg_ref[...].astype(F32)
        val, gate = vgv[:, 0:512], vgv[:, 512:1024]
        sgg = _sigmoid(gate)
        a_wsu[...] += _mm_tn(val * sgg, dy_ssm)
        do = _mm_nt(dy_ssm, w_su[...])
        dvg = jnp.concatenate([do * sgg, do * val * (sgg * (1.0 - sgg))], axis=1)
        a_vs[3:4, :] += _colsum(dvg)
        yv = yss_ref[...].astype(F32)
        a_wglu[...] += _mm_tn(_gelu(yv), dvg)
        dyss = _mm_nt(dvg, w_glu[...]) * _gelu_grad(yv)
        a_vs[2:3, 0:512] += _colsum(dyss * u_ssm)
        du_blocks = []
        for j in range(SSM_BLOCKS):
            lanes = pl.ds(j * SSM_BLOCK_STATE, SSM_BLOCK_STATE)
            in_lanes = slice(j * 128, (j + 1) * 128)
            dyb = dyss[:, in_lanes].astype(MXU_DTYPE)
            ub = u_ssm[:, in_lanes].astype(MXU_DTYPE)
            dre[0] = _mm_nt(dyb, ct[0, j])
            dre[1] = -_mm_nt(dyb, ct[1, j])
            a_ct[0, j] += _mm_tn(sre_ref[:, lanes], dyb)
            a_ct[1, j] -= _mm_tn(sim_ref[:, lanes], dyb)
            a1r, a1i, a2r, a2i, a4r, a4i, pr, pi = [coef[j, k] for k in range(8)]
            rowi = lax.broadcasted_iota(jnp.int32, (SCAN_ROWS, SSM_BLOCK_STATE), 0)

            def step(tt, c, lanes=lanes, a1r=a1r, a1i=a1i, a2r=a2r, a2i=a2i, a4r=a4r, a4i=a4i, pr=pr, pi=pi, rowi=rowi):
                cr, ci, acc_r, acc_i = c
                rows = pl.ds(pl.multiple_of((n_tiles - 1 - tt) * SCAN_ROWS, SCAN_ROWS), SCAN_ROWS)
                xr, xi = dre[0, rows, :], dre[1, rows, :]
                for dstep, kr, ki in ((1, a1r, a1i), (2, a2r, a2i), (4, a4r, a4i)):
                    sr, si = pltpu.roll(xr, SCAN_ROWS - dstep, 0), pltpu.roll(xi, SCAN_ROWS - dstep, 0)
                    xr, xi = xr + kr * sr + ki * si, xi + kr * si - ki * sr
                xr, xi = xr + pr * cr + pi * ci, xi + pr * ci - pi * cr
                lam[0, rows, :] = xr
                lam[1, rows, :] = xi
                nr = jnp.where(rowi == SCAN_ROWS - 1, cr, pltpu.roll(xr, SCAN_ROWS - 1, 0))
                ni = jnp.where(rowi == SCAN_ROWS - 1, ci, pltpu.roll(xi, SCAN_ROWS - 1, 0))
                s_r, s_i = sre_ref[rows, lanes], sim_ref[rows, lanes]
                acc_r = acc_r + nr * s_r + ni * s_i
                acc_i = acc_i + ni * s_r - nr * s_i
                return (jnp.broadcast_to(xr[0:1, :], xr.shape), jnp.broadcast_to(xi[0:1, :], xi.shape), acc_r, acc_i)

            cr, ci, acc_r, acc_i = lax.fori_loop(0, n_tiles, step, (carry[j, 0], carry[j, 1], a_da[0, j], a_da[1, j]))
            carry[j, 0] = cr
            carry[j, 1] = ci
            a_da[0, j] = acc_r
            a_da[1, j] = acc_i
            lr_b, li_b = lam[0].astype(MXU_DTYPE), lam[1].astype(MXU_DTYPE)
            a_bb[0, j] += _mm_tn(ub, lr_b)
            a_bb[1, j] += _mm_tn(ub, li_b)
            du_blocks.append(_mm_nt(lr_b, bb[0, j]) + _mm_nt(li_b, bb[1, j]))
        du_ssm = jnp.concatenate(du_blocks, axis=1) + dyss * mv_ref[ROW_SSM_D:ROW_SSM_D + 1, 0:512]
        dz_ref[...] = jnp.concatenate([du_pool, du_ssm, dgl_pool, dgl_ssm], axis=1).astype(SAVE_DTYPE)

        @pl.when(i == nt - 1)
        def _():
            rows = D_MODEL // NDEV
            for k in range(NDEV):
                st_wo[k] = a_wo[k * rows:(k + 1) * rows, :].astype(WIRE_DTYPE)
                for a, acc in enumerate((a_wpu, a_wglu, a_wsu)):
                    st_up[a, k] = acc[:, k * 128:(k + 1) * 128].astype(WIRE_DTYPE)
            outs = ((st_wo, dwo_h), (st_up.at[0], dwpu_h), (st_up.at[1], dwglu_h), (st_up.at[2], dwsu_h),
                    (a_pw, dpw_h), (a_bb, dbb_h), (a_ct, dct_h), (a_vs, vsum_h), (a_da, da_h))
            cps = [pltpu.make_async_copy(src, dst, sems.at[k]) for k, (src, dst) in enumerate(outs)]
            for cp in cps:
                cp.start()
            for cp in cps:
                cp.wait()

    def tok(width):
        return pl.BlockSpec((tm, width), lambda i: (nt - 1 - i, 0))

    hbm = _HBM
    acc_shapes = [(D_MODEL, D_MODEL), (512, D_MODEL), (512, D_MODEL), (512, D_MODEL), (4, 128, 128),
                  (2, SSM_BLOCKS, 128, SSM_BLOCK_STATE), (2, SSM_BLOCKS, SSM_BLOCK_STATE, 128), (8, D_MODEL),
                  (2, SSM_BLOCKS, SCAN_ROWS, SSM_BLOCK_STATE)]
    stack_out = [jax.ShapeDtypeStruct((NDEV, D_MODEL // NDEV, D_MODEL), WIRE_DTYPE)] \
        + [jax.ShapeDtypeStruct((NDEV, 512, 128), WIRE_DTYPE)] * 3
    return _launch(
        body, "mixer_backward", grid=(nt,), semantics=("arbitrary",), carry=carry, steps=_grid_steps(nt),
        out_shape=[jax.ShapeDtypeStruct((T, IN_WIDTH), SAVE_DTYPE)] + stack_out
        + [jax.ShapeDtypeStruct(s, F32) for s in acc_shapes[4:]],
        in_specs=[tok(D_MODEL), _resident(prm), tok(IN_WIDTH), tok(N_STATE), tok(N_STATE), tok(512), tok(512),
                  tok(D_MODEL), tok(512), tok(D_MODEL), tok(D_MODEL), hbm, hbm, hbm, hbm, _resident(pool_w),
                  _resident(mvec), _resident(srow), _resident(bb), _resident(ct)],
        out_specs=[tok(IN_WIDTH)] + [hbm] * len(acc_shapes),
        operands=(d2, prm, z, s_re, s_im, zp, q, y_pool, yss, vg, y_ssm, w_pu_s, w_glu_s, w_su_s, w_out, pool_w, mvec,
                  srow, bb, ct),
        scratch=[
            pltpu.VMEM((512, D_MODEL), MXU_DTYPE), pltpu.VMEM((512, D_MODEL), MXU_DTYPE),
            pltpu.VMEM((512, D_MODEL), MXU_DTYPE), pltpu.VMEM((D_MODEL, D_MODEL), MXU_DTYPE),
            pltpu.VMEM((4, 128, 128), MXU_DTYPE),
            pltpu.VMEM((SSM_BLOCKS, 8, SCAN_ROWS, SSM_BLOCK_STATE), F32),
            pltpu.VMEM((SSM_BLOCKS, 2, SCAN_ROWS, SSM_BLOCK_STATE), F32),
            pltpu.VMEM((POOL_HALO, POOL_WIDTH), F32),
            pltpu.VMEM((2, tm, SSM_BLOCK_STATE), F32), pltpu.VMEM((2, tm, SSM_BLOCK_STATE), F32),
        ] + [pltpu.VMEM(s, F32) for s in acc_shapes]
        + [pltpu.VMEM((NDEV, D_MODEL // NDEV, D_MODEL), WIRE_DTYPE), pltpu.VMEM((3, NDEV, 512, 128), WIRE_DTYPE),
           pltpu.SemaphoreType.DMA((25,))])


def _mixer_in_backward(x, dz, d, mo, prm, w_in_s, carry=None):
    T = x.shape[0]
    tm = min(T, TM_MIX)
    nt = T // tm
    cols = IN_WIDTH // NDEV

    def body(x_ref, dz_ref, d_ref, mo_ref, prm_ref, w_in_h, dx_ref, sums_ref, dw_ref, w_in, acc, sems):
        i = pl.program_id(0)

        @pl.when(i == 0)
        def _():
            cps = _load_stack(w_in_h, w_in, sems, 0)
            acc[...] = jnp.zeros_like(acc)
            for cp in cps:
                cp.wait()

        xv = x_ref[...]
        h, _, _ = _modulated(xv, prm_ref, 1, ROW_G_MIX)
        dzb = dz_ref[...].astype(MXU_DTYPE)
        acc[...] += _mm_tn(h, dzb)
        dx_ref[...], upd = _norm_backward_tile(_mm_nt(dzb, w_in[...]), xv, d_ref[...], mo_ref[...], prm_ref, 1,
                                               ROW_G_MIX, 1.0)

        @pl.when(i == 0)
        def _():
            sums_ref[...] = upd

        @pl.when(i > 0)
        def _():
            sums_ref[...] += upd

        @pl.when(i == nt - 1)
        def _():
            for k in range(NDEV):
                dw_ref[k] = acc[:, k * cols:(k + 1) * cols].astype(WIRE_DTYPE)

    tok = pl.BlockSpec((tm, D_MODEL), lambda i: (i, 0))
    return _launch(
        body, "mixer_in_backward", grid=(nt,), semantics=("arbitrary",), carry=carry, steps=_grid_steps(nt),
        out_shape=[jax.ShapeDtypeStruct((T, D_MODEL), F32), jax.ShapeDtypeStruct((8, D_MODEL), F32),
                   jax.ShapeDtypeStruct((NDEV, D_MODEL, cols), WIRE_DTYPE)],
        in_specs=[tok, pl.BlockSpec((tm, IN_WIDTH), lambda i: (i, 0)), tok, tok, _resident(prm), _HBM],
        out_specs=[tok, pl.BlockSpec((8, D_MODEL), lambda i: (0, 0)),
                   pl.BlockSpec((NDEV, D_MODEL, cols), lambda i: (0, 0, 0))],
        operands=(x, dz, d, mo, prm, w_in_s),
        scratch=[pltpu.VMEM((D_MODEL, IN_WIDTH), MXU_DTYPE), pltpu.VMEM((D_MODEL, IN_WIDTH), F32),
                 pltpu.SemaphoreType.DMA((8,))])


def _ssm_dense_backward(dbb, da, srow, b_dense):
    def body(dbb_ref, da_ref, srow_ref, bd_ref, db_ref, df_ref):
        df_re, df_im = [], []
        da_re = [_colsum(da_ref[0, j]) for j in range(SSM_BLOCKS)]
        da_im = [_colsum(da_ref[1, j]) for j in range(SSM_BLOCKS)]
        for j in range(SSM_BLOCKS):
            lanes = slice(j * SSM_BLOCK_STATE, (j + 1) * SSM_BLOCK_STATE)
            f_re, f_im = srow_ref[2:3, lanes], srow_ref[3:4, lanes]
            g_re, g_im = dbb_ref[0, j], dbb_ref[1, j]
            b_re, b_im = bd_ref[0, j], bd_ref[1, j]
            db_ref[0, j] = f_re * g_re + f_im * g_im
            db_ref[1, j] = f_re * g_im - f_im * g_re
            df_re.append(_colsum(g_re * b_re + g_im * b_im))
            df_im.append(_colsum(g_im * b_re - g_re * b_im))
        df_ref[...] = jnp.concatenate([jnp.concatenate(df_re, axis=1), jnp.concatenate(df_im, axis=1),
                                       jnp.concatenate(da_re, axis=1), jnp.concatenate(da_im, axis=1),
                                       jnp.zeros((4, N_STATE), F32)], axis=0)

    return pl.pallas_call(body, name="ssm_dense_backward",
                          out_shape=[jax.ShapeDtypeStruct(b_dense.shape, F32), jax.ShapeDtypeStruct((8, N_STATE), F32)],
                          compiler_params=pltpu.CompilerParams(vmem_limit_bytes=VMEM_LIMIT))(dbb, da, srow, b_dense)


def _adamw_update(w, g, m, v):
    m = ADAM_B1 * m + (1.0 - ADAM_B1) * g
    v = ADAM_B2 * v + (1.0 - ADAM_B2) * (g * g)
    m_hat = m / (1.0 - ADAM_B1 ** ADAM_STEP)
    v_hat = v / (1.0 - ADAM_B2 ** ADAM_STEP)
    delta = -ADAM_LR * (m_hat / (jnp.sqrt(v_hat) + ADAM_EPS) + ADAM_WD * w)
    return delta, m, v


def _adam_rows(shape):
    rows, cols = shape
    tr = rows
    while tr * cols * 4 > (1 << 20) and tr % 16 == 0:
        tr //= 2
    return tr


def _adam_sharded(w, m, v, land, order, name):
    R, C = w.shape
    tr = _adam_rows((R, C))

    def body(w_ref, m_ref, v_ref, land_ref, order_ref, g_ref, d_ref, mo_ref, vo_ref):
        g = land_ref[0].astype(F32)
        for b in range(1, NDEV):
            g = g + land_ref[b].astype(F32)
        g_ref[...] = g
        d_ref[...], mo_ref[...], vo_ref[...] = _adamw_update(w_ref[...], g, m_ref[...], v_ref[...])

    blk = pl.BlockSpec((tr, C), lambda i: (i, 0))
    return pl.pallas_call(
        body, name=name, grid=(R // tr,),
        out_shape=[jax.ShapeDtypeStruct((R, C), F32)] * 4,
        in_specs=[blk, blk, blk, pl.BlockSpec((NDEV, tr, C), lambda i: (0, i, 0)), _HBM],
        out_specs=[blk] * 4,
        compiler_params=_params("arbitrary"),
    )(w, m, v, land, order)


def _adam_ada(w, m, v, sc_all, dmod_cols):
    R, C = w.shape
    tr = 256

    def body(w_ref, m_ref, v_ref, sc_ref, dm_ref, g_ref, d_ref, mo_ref, vo_ref):
        g = _mm_tn(sc_ref[...], dm_ref[...])
        g_ref[...] = g
        d_ref[...], mo_ref[...], vo_ref[...] = _adamw_update(w_ref[...], g, m_ref[...], v_ref[...])

    blk = pl.BlockSpec((tr, C), lambda i: (i, 0))
    return pl.pallas_call(
        body, name="adam_w_ada", grid=(R // tr,),
        out_shape=[jax.ShapeDtypeStruct((R, C), F32)] * 4,
        in_specs=[blk, blk, blk, pl.BlockSpec((8, tr), lambda i: (0, i)), pl.BlockSpec((8, C), lambda i: (0, 0))],
        out_specs=[blk] * 4,
        compiler_params=_params("arbitrary"),
    )(w, m, v, sc_all, dmod_cols)


def _adam_small(w, g, m, v, name):
    def body(w_ref, g_ref, m_ref, v_ref, d_ref, mo_ref, vo_ref):
        d_ref[...], mo_ref[...], vo_ref[...] = _adamw_update(w_ref[...], g_ref[...], m_ref[...], v_ref[...])

    return pl.pallas_call(body, name=name, out_shape=[jax.ShapeDtypeStruct(w.shape, F32)] * 3,
                          compiler_params=pltpu.CompilerParams(vmem_limit_bytes=VMEM_LIMIT))(w, g, m, v)


def _block_diag_in(b):
    bt = jnp.transpose(b, (0, 2, 1)).reshape(SSM_BLOCKS, 8, SSM_GROUP, SSM_STATE)
    eye = jnp.eye(8, dtype=bool)[None, :, None, :, None]
    return jnp.where(eye, bt[:, :, :, None, :], 0.0).reshape(SSM_BLOCKS, 128, SSM_BLOCK_STATE)


def _block_diag_out(c):
    ct = jnp.transpose(c, (0, 2, 1)).reshape(SSM_BLOCKS, 8, SSM_STATE, SSM_GROUP)
    eye = jnp.eye(8, dtype=bool)[None, :, None, :, None]
    return jnp.where(eye, ct[:, :, :, None, :], 0.0).reshape(SSM_BLOCKS, SSM_BLOCK_STATE, 128)


def _diag_blocks(dense, rows, cols):
    d5 = dense.reshape(SSM_BLOCKS, 8, rows, 8, cols)
    return jnp.stack([d5[:, a, :, a, :] for a in range(8)], axis=1).reshape(N_SSM_GROUPS, rows, cols)


def _pack_small(ada_vec, parts, params, tail=None):
    rest_rows, rows = _pack_rows(params, ada_vec is not None)
    rest = jnp.concatenate([parts[n].reshape(-1) for n, _ in params])
    rest = jnp.pad(rest, (0, NDEV * rest_rows * 128 - rest.shape[0])).reshape(NDEV, rest_rows, 128)
    head = [] if ada_vec is None else [ada_vec.reshape(NDEV, ADA_ROWS, 128)]
    pad = rows - rest_rows - (0 if ada_vec is None else ADA_ROWS)
    fill = jnp.zeros((NDEV, pad, 128), F32) if tail is None else jnp.pad(tail[None], ((0, NDEV - 1), (0, pad - 1), (0, 127)))
    return jnp.concatenate(head + [rest] + ([fill] if pad else []), axis=1)


def _unpack_small(pack, shapes, params, with_ada):
    rest_rows, _ = _pack_rows(params, with_ada)
    first = ADA_ROWS if with_ada else 0
    ada_vec = pack[:, :first].reshape(-1) if with_ada else None
    rest = pack[:, first:first + rest_rows].reshape(-1)
    out, off = {}, 0
    for n, size in params:
        out[n] = rest[off:off + size].reshape(shapes[n])
        off += size
    return ada_vec, out


WEIGHT_ORDER = ('w_ada', 'b_ada', 'g_ffn1', 'w_ffn1_in', 'w_ffn1_out', 'g_mix', 'w_in', 'pool_w', 'pool_b',
                'pool_scale', 'w_pool_up', 'ssm_lam_re_log', 'ssm_lam_im', 'ssm_log_dt', 'ssm_b_re', 'ssm_b_im',
                'ssm_c_re', 'ssm_c_im', 'ssm_d', 'w_glu', 'b_glu', 'w_ssm_up', 'w_out', 'g_ffn2', 'w_ffn2_in',
                'w_ffn2_out', 'g_final')
GATHERED = ('w_ffn1_in', 'w_ffn1_out', 'w_in', 'w_pool_up', 'w_glu', 'w_ssm_up', 'w_out', 'w_ffn2_in', 'w_ffn2_out')
TRANSPOSED = ('w_ffn1_in', 'w_ffn2_in')
STATE_MINOR = ('ssm_b_re', 'ssm_b_im')


def kernel(x, c, w_ada, b_ada, g_ffn1, w_ffn1_in, w_ffn1_out, g_mix, w_in, pool_w, pool_b, pool_scale, w_pool_up, ssm_lam_re_log, ssm_lam_im, ssm_log_dt, ssm_b_re, ssm_b_im, ssm_c_re, ssm_c_im, ssm_d, w_glu, b_glu, w_ssm_up, w_out, g_ffn2, w_ffn2_in, w_ffn2_out, g_final, loss_target, m_w_ada, m_b_ada, m_g_ffn1, m_w_ffn1_in, m_w_ffn1_out, m_g_mix, m_w_in, m_pool_w, m_pool_b, m_pool_scale, m_w_pool_up, m_ssm_lam_re_log, m_ssm_lam_im, m_ssm_log_dt, m_ssm_b_re, m_ssm_b_im, m_ssm_c_re, m_ssm_c_im, m_ssm_d, m_w_glu, m_b_glu, m_w_ssm_up, m_w_out, m_g_ffn2, m_w_ffn2_in, m_w_ffn2_out, m_g_final, v_w_ada, v_b_ada, v_g_ffn1, v_w_ffn1_in, v_w_ffn1_out, v_g_mix, v_w_in, v_pool_w, v_pool_b, v_pool_scale, v_w_pool_up, v_ssm_lam_re_log, v_ssm_lam_im, v_ssm_log_dt, v_ssm_b_re, v_ssm_b_im, v_ssm_c_re, v_ssm_c_im, v_ssm_d, v_w_glu, v_b_glu, v_w_ssm_up, v_w_out, v_g_ffn2, v_w_ffn2_in, v_w_ffn2_out, v_g_final):
    args = locals()
    W = {n: args[n] for n in WEIGHT_ORDER}
    M = {n: args["m_" + n] for n in WEIGHT_ORDER}
    V = {n: args["v_" + n] for n in WEIGHT_ORDER}
    shapes = {n: W[n].shape for n in WEIGHT_ORDER}
    xt, tgt = x[0], loss_target[0]

    def local(tree, n):
        return jnp.swapaxes(tree[n][0], 0, 1) if n in TRANSPOSED else tree[n][0]

    def as_output(n, a):
        return (jnp.swapaxes(a, 0, 1) if n in TRANSPOSED else a)[None]

    shard = dict(zip(GATHERED, _cast_shards([local(W, n) for n in GATHERED])))
    stacks = {}

    def gather(names):
        return _Gather([shard[n] for n in names])

    def gathered(names, results):
        stacks.update(zip(names, results))

    ffn1_w, ffn2_w = ('w_ffn1_in', 'w_ffn1_out'), ('w_ffn2_in', 'w_ffn2_out')
    mix_w = ('w_in', 'w_pool_up', 'w_glu', 'w_ssm_up', 'w_out')
    mod_cols, sc_all, *res = _ada_forward(c, W['w_ada'][0], b_ada.reshape(NDEV, -1), gather(ffn1_w[:1]))
    gathered(ffn1_w[:1], res)
    win1 = stacks['w_ffn1_in'].reshape(2, 4, FF_SHARD, D_MODEL)
    prm = jnp.concatenate([mod_cols.reshape(9, D_MODEL), g_ffn1, g_mix, g_ffn2, g_final[None], jnp.zeros((3, D_MODEL), F32)], axis=0)
    pad512 = jnp.zeros((1, D_MODEL - 512), F32)
    mvec = jnp.concatenate([jnp.concatenate([pool_b, pad512], axis=1), jnp.concatenate([pool_scale, pad512], axis=1),
                            jnp.concatenate([ssm_d, pad512], axis=1), b_glu, jnp.zeros((4, D_MODEL), F32)], axis=0)
    log_dt_col = ssm_log_dt[0][:, None]
    coeffs = _ssm_params_forward(ssm_lam_re_log[0], ssm_lam_im[0], log_dt_col)
    srow = jnp.stack([t.reshape(N_STATE) for t in coeffs], axis=0)
    b_dense = jnp.stack([_block_diag_in(ssm_b_re[0]), _block_diag_in(ssm_b_im[0])], axis=0)
    c_dense = jnp.stack([_block_diag_out(ssm_c_re[0]), _block_diag_out(ssm_c_im[0])], axis=0)
    bb, ct = _ssm_dense_forward(srow, b_dense, c_dense)
    pw = pool_w[0]

    next_w = ffn1_w[1:] + mix_w[:1]
    ab1, s1, *res = _ffn_hidden(xt, prm, win1, 0, ROW_G_FFN1, "ffn1_hidden", gather(next_w))
    gathered(next_w, res)
    wout1 = stacks['w_ffn1_out'].reshape(4, FF_SHARD, D_MODEL)
    x1, f1, *res = _ffn_out(xt, s1, prm, wout1, 0, "ffn1_out", gather(mix_w[1:]))
    gathered(mix_w[1:], res)
    w_out_full = stacks['w_out'].reshape(D_MODEL, D_MODEL)
    res = _mixer_forward(x1, prm, stacks['w_in'], stacks['w_pool_up'], stacks['w_glu'], stacks['w_ssm_up'],
                         w_out_full, pw, mvec, srow, bb, ct, gather(ffn2_w))
    x2, mo, saved = res[0], res[1], res[2:11]
    gathered(ffn2_w, res[11:])
    win2 = stacks['w_ffn2_in'].reshape(2, 4, FF_SHARD, D_MODEL)
    wout2 = stacks['w_ffn2_out'].reshape(4, FF_SHARD, D_MODEL)
    d3, fin, f3, ab3 = _ffn_forward_loss(x2, tgt, prm, win2, wout2, 2, ROW_G_FFN2, "ffn2_forward_loss")

    lands = {}

    def scatter(grads):
        names = list(grads)
        return _Scatter([grads[n][0] for n in names], [grads[n][1] for n in names], [local(W, n).shape for n in names])

    def scattered(grads, results):
        lands.update(zip(grads, results))

    parts3, dab3, dwout2 = _ffn_backward(d3, ab3, prm, win2, wout2, 2, "ffn2_backward")
    dwin2, d2, sums3 = _ffn_dwin(x2, dab3, parts3, d3, f3, prm, 2, ROW_G_FFN2, "ffn2_dwin")
    g_ffn2_w = {'w_ffn2_in': (dwin2, _halves), 'w_ffn2_out': (dwout2.reshape(NDEV, -1, D_MODEL), _stacked)}
    res = _mixer_backward(d2, prm, saved, stacks['w_pool_up'], stacks['w_glu'], stacks['w_ssm_up'], w_out_full, pw, mvec,
                          srow, bb, ct, scatter(g_ffn2_w))
    dz, dwo, dwpu, dwglu, dwsu, dpw, dbb, dct, vsum, da = res[:10]
    scattered(g_ffn2_w, res[10:])
    g_mix_up = {'w_pool_up': (dwpu, _stacked), 'w_glu': (dwglu, _stacked), 'w_ssm_up': (dwsu, _stacked),
                'w_out': (dwo, _stacked)}
    d1, sums2, dwin_mix, *res = _mixer_in_backward(x1, dz, d2, mo, prm, stacks['w_in'], scatter(g_mix_up))
    scattered(g_mix_up, res)
    g_mix_w = {'w_in': (dwin_mix, _stacked)}
    db_dense, df_rows = _ssm_dense_backward(dbb, da, srow, b_dense)
    cot = [df_rows[r].reshape(N_SSM_GROUPS, SSM_STATE) for r in (2, 3, 0, 1)]
    d_lrl, d_li, d_ldt = _ssm_params_backward(ssm_lam_re_log[0], ssm_lam_im[0], log_dt_col, cot)
    small_grads = {
        'g_mix': sums2[0], 'g_ffn2': sums3[0], 'g_final': fin[0], 'pool_w': dpw,
        'pool_b': vsum[1, :512], 'pool_scale': vsum[0, :512], 'ssm_lam_re_log': d_lrl, 'ssm_lam_im': d_li,
        'ssm_log_dt': d_ldt, 'ssm_b_re': _diag_blocks(db_dense[0], SSM_GROUP, SSM_STATE),
        'ssm_b_im': _diag_blocks(db_dense[1], SSM_GROUP, SSM_STATE),
        'ssm_c_re': jnp.transpose(_diag_blocks(dct[0], SSM_STATE, SSM_GROUP), (0, 2, 1)),
        'ssm_c_im': jnp.transpose(_diag_blocks(dct[1], SSM_STATE, SSM_GROUP), (0, 2, 1)),
        'ssm_d': vsum[2, :512], 'b_glu': vsum[3],
    }
    early = _SmallAllReduce(_pack_small(None, small_grads, SMALL_EARLY, fin[1:2, 0:1]))
    parts1, dab1, dwout1, total_early, *res = _ffn_backward(d1, ab1, prm, win1, wout1, 0, "ffn1_backward",
                                                            _Carried(early, scatter(g_mix_w)))
    scattered(g_mix_w, res)
    loss = total_early[0, _pack_rows(SMALL_EARLY, False)[0], 0]
    g_wout1 = {'w_ffn1_out': (dwout1.reshape(NDEV, -1, D_MODEL), _stacked)}
    dwin1, d0, sums1, *res = _ffn_dwin(xt, dab1, parts1, d1, f1, prm, 0, ROW_G_FFN1, "ffn1_dwin", scatter(g_wout1))
    scattered(g_wout1, res)

    dmod = jnp.concatenate([sums1[1:4], sums2[1:4], sums3[1:4]], axis=0).reshape(-1)
    total_late, landed = _allreduce_small(_pack_small(dmod, {'g_ffn1': sums1[0]}, SMALL_LATE), total_early, "allreduce_late")
    dmod_cols = landed[:, :ADA_ROWS].reshape(NDEV, ADA_ROWS * 128)

    last_w, last_views = ffn1_w[:1], [_halves]
    send_sems, recv_sems, last_src, last_land, token = _scatter_start(
        [dwin1], last_views, [local(W, n).shape for n in last_w], [total_late])
    after_start = token[0:1, 0:1]

    grad, delta, new_m, new_v = {}, {}, {}, {}

    def adam_sharded(n):
        res = _adam_sharded(local(W, n), local(M, n), local(V, n), lands[n], token, "adam_" + n)
        grad[n], delta[n], new_m[n], new_v[n] = [as_output(n, r) for r in res]
        return res[3]

    def adam_small(params, ada, total, name, order):
        rows = _pack_rows(params, ada)[1]
        views = [{n: jnp.transpose(t[n][0], (0, 2, 1)) if n in STATE_MINOR else t[n] for n, _ in params} for t in (W, M, V)]
        packs = [(_pack_small(t['b_ada'].reshape(-1) if ada else None, v, params) + order).reshape(NDEV * rows, 128)
                 for t, v in zip((W, M, V), views)]
        res = _adam_small(packs[0], total.reshape(NDEV * rows, 128), packs[1], packs[2], name)
        view_shapes = {n: (N_SSM_GROUPS, SSM_GROUP, SSM_STATE) if n in STATE_MINOR else shapes[n] for n, _ in params}
        for dst, packed in zip((grad, delta, new_m, new_v), (total, *res)):
            ada_vec, rest = _unpack_small(packed.reshape(NDEV, rows, 128), view_shapes, params, ada)
            dst.update({n: jnp.transpose(a, (0, 2, 1))[None] if n in STATE_MINOR else a for n, a in rest.items()})
            if ada:
                dst['b_ada'] = ada_vec.reshape(shapes['b_ada'])
        return res[2]

    done = [adam_sharded(n) for n in GATHERED if n not in last_w]
    res = _adam_ada(W['w_ada'][0], M['w_ada'][0], V['w_ada'][0], sc_all, dmod_cols + after_start)
    grad['w_ada'], delta['w_ada'], new_m['w_ada'], new_v['w_ada'] = [r[None] for r in res]
    done.append(res[3])
    done.append(adam_small(SMALL_EARLY, False, total_early, "adam_small_early", after_start))
    done.append(adam_small(SMALL_LATE, True, total_late, "adam_small_late", after_start))
    lands.update(zip(last_w, _scatter_wait(send_sems, recv_sems, last_src, last_land, last_views, done)))
    for n in last_w:
        adam_sharded(n)

    return (loss, d0[None], *[grad[n] for n in WEIGHT_ORDER], *[delta[n] for n in WEIGHT_ORDER],
            *[new_m[n] for n in WEIGHT_ORDER], *[new_v[n] for n in WEIGHT_ORDER])
```

```python
import jax
import jax.numpy as jnp
from jax import lax
from jax.experimental import pallas as pl
from jax.experimental.pallas import tpu as pltpu

F32 = jnp.float32
MXU_DTYPE = jnp.bfloat16
WIRE_DTYPE = jnp.bfloat16
SAVE_DTYPE = jnp.bfloat16

NDEV = 8
D_MODEL = 1024
D_FF = 2816
FF_SHARD = 2 * D_FF // NDEV
POOL_WIDTH = 512
POOL_GROUP = 128
SSM_WIDTH = 512
SSM_STATE = 64
SSM_GROUP = 16
N_SSM_GROUPS = SSM_WIDTH // SSM_GROUP
SSM_BLOCKS = 4
SSM_BLOCK_STATE = 512
N_STATE = 2048
IN_WIDTH = 3072
EPS = 1e-6
ADAM_LR = 0.001
ADAM_B1 = 0.9
ADAM_B2 = 0.999
ADAM_EPS = 1e-08
ADAM_WD = 0.01
ADAM_STEP = 10

TM_FFN = 512
FFN_BWD_CHUNK = 256
TM_MIX = 256
TM_MIX_BWD = 256
TM_EW = 512
SCAN_ROWS = 8
POOL_HALO = 16
VMEM_LIMIT = 60 * 1024 * 1024

ROW_G_FFN1, ROW_G_MIX, ROW_G_FFN2, ROW_G_FINAL = 9, 10, 11, 12
ROW_POOL_B, ROW_POOL_SCALE, ROW_SSM_D, ROW_B_GLU = 0, 1, 2, 3

SMALL_EARLY = (
    ("g_mix", 1024), ("g_ffn2", 1024), ("g_final", 1024), ("pool_w", 65536),
    ("pool_b", 512), ("pool_scale", 512), ("ssm_lam_re_log", 2048), ("ssm_lam_im", 2048),
    ("ssm_log_dt", 32), ("ssm_b_re", 32768), ("ssm_b_im", 32768), ("ssm_c_re", 32768),
    ("ssm_c_im", 32768), ("ssm_d", 512), ("b_glu", 1024),
)
SMALL_LATE = (("g_ffn1", 1024),)
ADA_ROWS = 9
MESH = pl.DeviceIdType.MESH


def _pack_rows(params, with_ada):
    rest = -(-sum(n for _, n in params) // (NDEV * 128))
    return rest, -(-(rest + (ADA_ROWS if with_ada else 0)) // 8) * 8


def _mm(a, b):
    return jnp.dot(a.astype(MXU_DTYPE), b.astype(MXU_DTYPE), preferred_element_type=F32)


def _mm_nt(a, b):
    return lax.dot_general(a.astype(MXU_DTYPE), b.astype(MXU_DTYPE), (((1,), (1,)), ((), ())),
                           preferred_element_type=F32)


def _mm_tn(a, b):
    return lax.dot_general(a.astype(MXU_DTYPE), b.astype(MXU_DTYPE), (((0,), (0,)), ((), ())),
                           preferred_element_type=F32)


def _rms_scale(x):
    return lax.rsqrt(jnp.mean(x * x, axis=-1, keepdims=True) + EPS)


def _sigmoid(x):
    return jax.nn.sigmoid(x)


def _colsum(x):
    return jnp.sum(x, axis=0, keepdims=True)


def _row(ref, r):
    return ref[r:r + 1, :]


def _params(*sem):
    return pltpu.CompilerParams(dimension_semantics=sem, vmem_limit_bytes=VMEM_LIMIT)


def _resident(a):
    return pl.BlockSpec(a.shape, lambda *_: (0,) * a.ndim, pipeline_mode=pl.Buffered(1))


def _me():
    return lax.axis_index("x"), lax.axis_index("y"), lax.axis_index("c")


def _peer(rel):
    x, y, c = _me()
    px = 1 - x if rel & 4 else x
    py = 1 - y if rel & 2 else y
    pc = 1 - c if rel & 1 else c
    return (px, py, pc), 4 * px + 2 * py + pc


_HBM = pl.BlockSpec(memory_space=pl.ANY)
_HBM_ONLY = pl.BlockSpec(memory_space=pltpu.HBM)


def _stacked(ref, p):
    return ref.at[p]


def _halves(ref, p):
    return ref.at[p // 4, p % 4]


class _Gather:
    def __init__(self, shards):
        self.operands = list(shards)
        self.n = len(shards)
        self.out_shape = [jax.ShapeDtypeStruct((NDEV,) + s.shape, s.dtype) for s in shards]
        self.scratch = [pltpu.SemaphoreType.DMA((7 * self.n,)), pltpu.SemaphoreType.DMA((7 * self.n,)),
                        pltpu.SemaphoreType.DMA((self.n,))]

    def plan(self, srcs, outs, sems):
        send_sems, recv_sems, local_sems = sems
        n = self.n
        x, y, c = _me()
        me = 4 * x + 2 * y + c
        here, sibling = (x, y, c), (x, y, 1 - c)
        chips = [(1 - x, y), (x, 1 - y), (1 - x, 1 - y)]

        def blk(px, py, pc):
            return 4 * px + 2 * py + pc

        def copy(a, k, block, to, src=None):
            return pltpu.make_async_remote_copy(
                src_ref=outs[a].at[block] if src is None else src, dst_ref=outs[a].at[block],
                send_sem=send_sems.at[7 * a + k], recv_sem=recv_sems.at[7 * a + k], device_id=to, device_id_type=MESH)

        def mine(a):
            return pltpu.make_async_copy(srcs[a], outs[a].at[me], local_sems.at[a])

        def first(a):
            return [copy(a, 0, me, sibling, src=srcs[a])] + [copy(a, 1 + j, me, (*chip, c), src=srcs[a])
                                                              for j, chip in enumerate(chips)]

        def start():
            for a in range(n):
                mine(a).start()
                for cp in first(a):
                    cp.start()

        def forward():
            for a in range(n):
                for j, chip in enumerate(chips):
                    copy(a, 1 + j, blk(*chip, c), here).wait_recv()
                    copy(a, 4 + j, blk(*chip, c), sibling).start()

        def finish():
            for a in range(n):
                copy(a, 0, blk(x, y, 1 - c), here).wait_recv()
                for j, chip in enumerate(chips):
                    copy(a, 4 + j, blk(*chip, 1 - c), here).wait_recv()
            for a in range(n):
                mine(a).wait()
                for cp in first(a):
                    cp.wait_send()
                for j, chip in enumerate(chips):
                    copy(a, 4 + j, blk(*chip, c), sibling).wait_send()

        return start, forward, finish


class _Scatter:
    def __init__(self, arrays, views, shard_shapes):
        self.operands = list(arrays)
        self.views = list(views)
        self.n = len(arrays)
        self.out_shape = [jax.ShapeDtypeStruct((NDEV,) + tuple(s), a.dtype) for s, a in zip(shard_shapes, arrays)]
        self.scratch = [pltpu.SemaphoreType.DMA((7 * self.n,)), pltpu.SemaphoreType.DMA((7 * self.n,)),
                        pltpu.SemaphoreType.DMA((self.n,))]

    def plan(self, srcs, outs, sems):
        send_sems, recv_sems, local_sems = sems
        n, views = self.n, self.views
        x, y, c = _me()
        me = 4 * x + 2 * y + c

        def mine(a):
            return pltpu.make_async_copy(views[a](srcs[a], me), outs[a].at[me], local_sems.at[a])

        def copy(a, rel, sending):
            to, p = _peer(rel)
            return pltpu.make_async_remote_copy(
                src_ref=views[a](srcs[a], p), dst_ref=outs[a].at[me if sending else p],
                send_sem=send_sems.at[7 * a + rel - 1], recv_sem=recv_sems.at[7 * a + rel - 1],
                device_id=to if sending else (x, y, c), device_id_type=MESH)

        def start():
            for a in range(n):
                mine(a).start()
            for rel in range(1, 8):
                for a in range(n):
                    copy(a, rel, True).start()

        def forward():
            pass

        def finish():
            for rel in range(1, 8):
                for a in range(n):
                    copy(a, rel, False).wait_recv()
            for rel in range(1, 8):
                for a in range(n):
                    copy(a, rel, True).wait_send()
            for a in range(n):
                mine(a).wait()

        return start, forward, finish


class _SmallAllReduce:
    def __init__(self, pack):
        rows = pack.shape[1]
        self.operands = [pack]
        self.n = 1
        self.out_shape = [jax.ShapeDtypeStruct(pack.shape, F32)]
        self.scratch = [pltpu.VMEM(pack.shape, F32), pltpu.VMEM((rows, 128), F32)] \
            + [pltpu.SemaphoreType.DMA((7,))] * 4 + [pltpu.SemaphoreType.DMA((2,))]

    def plan(self, srcs, outs, scratch):
        pack, total = srcs[0], outs[0]
        land, mine, send1, recv1, send2, recv2, local = scratch
        x, y, c = _me()
        me = 4 * x + 2 * y + c

        def slab(rel, sending):
            to, p = _peer(rel)
            return pltpu.make_async_remote_copy(
                src_ref=pack.at[p], dst_ref=land.at[me if sending else p], send_sem=send1.at[rel - 1],
                recv_sem=recv1.at[rel - 1], device_id=to if sending else (x, y, c), device_id_type=MESH)

        def summed(rel, sending):
            to, p = _peer(rel)
            return pltpu.make_async_remote_copy(
                src_ref=mine, dst_ref=total.at[me if sending else p], send_sem=send2.at[rel - 1],
                recv_sem=recv2.at[rel - 1], device_id=to if sending else (x, y, c), device_id_type=MESH)

        own_slab = pltpu.make_async_copy(pack.at[me], land.at[me], local.at[0])
        own_sum = pltpu.make_async_copy(mine, total.at[me], local.at[1])

        def start():
            own_slab.start()
            for rel in range(1, 8):
                slab(rel, True).start()

        def forward():
            own_slab.wait()
            for rel in range(1, 8):
                slab(rel, False).wait_recv()
            acc = land[0]
            for b in range(1, NDEV):
                acc = acc + land[b]
            mine[...] = acc
            own_sum.start()
            for rel in range(1, 8):
                summed(rel, True).start()

        def finish():
            for rel in range(1, 8):
                summed(rel, False).wait_recv()
            for rel in range(1, 8):
                slab(rel, True).wait_send()
                summed(rel, True).wait_send()
            own_sum.wait()

        return start, forward, finish


class _Carried:
    def __init__(self, *parts):
        self.parts = parts
        self.operands = [o for p in parts for o in p.operands]
        self.n = len(self.operands)
        self.out_shape = [s for p in parts for s in p.out_shape]
        self.scratch = [s for p in parts for s in p.scratch]

    def plan(self, srcs, outs, scratch):
        plans, a, b = [], 0, 0
        for p in self.parts:
            plans.append(p.plan(srcs[a:a + p.n], outs[a:a + p.n], scratch[b:b + len(p.scratch)]))
            a, b = a + p.n, b + len(p.scratch)

        def every(k):
            def run():
                for plan in plans:
                    plan[k]()
            return run

        return every(0), every(1), every(2)


def _launch(body, name, out_shape, in_specs, out_specs, operands, scratch=(), grid=None, semantics=None,
            carry=None, steps=None):
    out_shape, in_specs, out_specs = list(out_shape), list(in_specs), list(out_specs)
    operands, scratch = list(operands), list(scratch)
    n_in, n_out, n_scr = len(in_specs), len(out_shape), len(scratch)
    kernel_body = body
    if carry is not None:
        k = carry.n

        def kernel_body(*refs):
            ins, cin = refs[:n_in], refs[n_in:n_in + k]
            outs, cout = refs[n_in + k:n_in + k + n_out], refs[n_in + k + n_out:n_in + 2 * k + n_out]
            rest = refs[n_in + 2 * k + n_out:]
            scr, csem = rest[:n_scr], rest[n_scr:]
            start, forward, finish = carry.plan(cin, cout, csem)
            if steps is None:
                start()
                body(*ins, *outs, *scr)
                forward()
                finish()
            else:
                pl.when(steps()[0])(start)
                pl.when(steps()[1])(forward)
                body(*ins, *outs, *scr)
                pl.when(steps()[2])(finish)

        in_specs += [_HBM] * k
        out_shape += carry.out_shape
        out_specs += [_HBM] * k
        operands += carry.operands
        scratch += carry.scratch
    kwargs = {} if grid is None else {"grid": grid}
    params = pltpu.CompilerParams(vmem_limit_bytes=VMEM_LIMIT) if semantics is None else _params(*semantics)
    return pl.pallas_call(kernel_body, name=name, out_shape=out_shape, in_specs=in_specs, out_specs=out_specs,
                          scratch_shapes=scratch, compiler_params=params, **kwargs)(*operands)


def _grid_steps(nt):
    def steps():
        i = pl.program_id(0)
        return i == 0, i == nt - 1, i == nt - 1
    return steps


def _cast_shards(shards):
    n = len(shards)

    def body(*refs):
        for a in range(n):
            refs[n + a][...] = refs[a][...].astype(WIRE_DTYPE)

    return pl.pallas_call(body, name="cast_shards",
                          out_shape=[jax.ShapeDtypeStruct(s.shape, WIRE_DTYPE) for s in shards],
                          compiler_params=pltpu.CompilerParams(vmem_limit_bytes=VMEM_LIMIT))(*shards)


_SEM = pl.BlockSpec(memory_space=pltpu.SEMAPHORE)
_DATAFLOW = pltpu.SideEffectType.DATAFLOW_SIDE_EFFECTING


def _split_copy(arrays, views, landing, send_sems, recv_sems, a, rel):
    to, p = _peer(rel)
    x, y, c = _me()
    return pltpu.make_async_remote_copy(
        src_ref=views[a](arrays[a], p), dst_ref=landing[a].at[4 * x + 2 * y + c],
        send_sem=send_sems.at[NDEV * a + rel], recv_sem=recv_sems.at[NDEV * a + rel], device_id=to, device_id_type=MESH)


def _scatter_start(arrays, views, shard_shapes, after):
    n = len(arrays)
    landing = [pltpu.with_memory_space_constraint(lax.empty((NDEV,) + tuple(s), a.dtype), pltpu.HBM)
               for s, a in zip(shard_shapes, arrays)]
    arrays = [pltpu.with_memory_space_constraint(a, pltpu.HBM) for a in arrays]

    def body(*refs):
        ins, land = refs[:n], refs[n:2 * n]
        send_sems, recv_sems = refs[2 * n + len(after)], refs[2 * n + len(after) + 1]
        token = refs[-1]
        for rel in range(NDEV):
            for a in range(n):
                _split_copy(ins, views, land, send_sems, recv_sems, a, rel).start()
        token[...] = jnp.zeros_like(token)

    res = pl.pallas_call(
        body, name="scatter_start",
        out_shape=[pltpu.SemaphoreType.DMA((NDEV * n,)), pltpu.SemaphoreType.DMA((NDEV * n,))]
        + [pltpu.HBM(a.shape, a.dtype) for a in arrays] + [pltpu.HBM(l.shape, l.dtype) for l in landing]
        + [jax.ShapeDtypeStruct((8, 128), F32)],
        in_specs=[_HBM_ONLY] * (2 * n) + [_HBM] * len(after),
        out_specs=[_SEM, _SEM] + [_HBM_ONLY] * (2 * n) + [pl.BlockSpec(memory_space=pltpu.VMEM)],
        input_output_aliases={i: 2 + i for i in range(2 * n)},
        compiler_params=pltpu.CompilerParams(has_side_effects=_DATAFLOW),
    )(*arrays, *landing, *after)
    return res[0], res[1], res[2:2 + n], res[2 + n:2 + 2 * n], res[-1]


def _scatter_wait(send_sems, recv_sems, arrays, landing, views, after):
    n = len(arrays)

    def body(*refs):
        ins, land = refs[:n], refs[n:2 * n]
        send, recv = refs[2 * n], refs[2 * n + 1]
        for rel in range(NDEV):
            for a in range(n):
                cp = _split_copy(ins, views, land, send, recv, a, rel)
                cp.wait_send()
                cp.wait_recv()

    res = pl.pallas_call(
        body, name="scatter_wait",
        out_shape=[pltpu.HBM(a.shape, a.dtype) for a in arrays] + [pltpu.HBM(l.shape, l.dtype) for l in landing],
        in_specs=[_HBM_ONLY] * (2 * n) + [_SEM, _SEM] + [_HBM] * len(after),
        out_specs=[_HBM_ONLY] * (2 * n),
        input_output_aliases={i: i for i in range(2 * n)},
        compiler_params=pltpu.CompilerParams(has_side_effects=_DATAFLOW),
    )(*arrays, *landing, send_sems, recv_sems, *after)
    return res[n:]


def _ada_forward(c_row, w_ada, b_ada8, carry):
    cols = w_ada.shape[1]

    def body(c_ref, w_ref, b_ref, mod_ref, sc_ref, c_all, send_buf, recv_buf, send1, recv1, send2, recv2):
        x, y, c = _me()
        me = 4 * x + 2 * y + c
        rowi = lax.broadcasted_iota(jnp.int32, (8, D_MODEL), 0)
        c_all[me] = jnp.broadcast_to(c_ref[...], (8, D_MODEL))
        copies = []
        for rel in range(1, 8):
            to, _ = _peer(rel)
            cp = pltpu.make_async_remote_copy(src_ref=c_all.at[me], dst_ref=c_all.at[me], send_sem=send1.at[rel - 1],
                                              recv_sem=recv1.at[rel - 1], device_id=to, device_id_type=MESH)
            cp.start()
            copies.append(cp)
        for rel in range(1, 8):
            _, p = _peer(rel)
            pltpu.make_async_remote_copy(src_ref=c_all.at[p], dst_ref=c_all.at[p], send_sem=send1.at[rel - 1],
                                         recv_sem=recv1.at[rel - 1], device_id=(x, y, c), device_id_type=MESH).wait_recv()
        for cp in copies:
            cp.wait_send()
        cmat = jnp.zeros((8, D_MODEL), F32)
        for b in range(8):
            cmat = jnp.where(rowi == b, c_all[b], cmat)
        sc = cmat * _sigmoid(cmat)
        sc_ref[...] = sc
        modcols = _mm(sc, w_ref[...]) + b_ref[pl.ds(me, 1), :]
        for b in range(8):
            send_buf[b] = jnp.broadcast_to(modcols[b:b + 1, :], (8, cols))
        recv_buf[me] = send_buf[me]
        copies = []
        for rel in range(1, 8):
            to, p = _peer(rel)
            cp = pltpu.make_async_remote_copy(src_ref=send_buf.at[p], dst_ref=recv_buf.at[me], send_sem=send2.at[rel - 1],
                                              recv_sem=recv2.at[rel - 1], device_id=to, device_id_type=MESH)
            cp.start()
            copies.append(cp)
        for rel in range(1, 8):
            _, p = _peer(rel)
            pltpu.make_async_remote_copy(src_ref=send_buf.at[p], dst_ref=recv_buf.at[p], send_sem=send2.at[rel - 1],
                                         recv_sem=recv2.at[rel - 1], device_id=(x, y, c), device_id_type=MESH).wait_recv()
        for cp in copies:
            cp.wait_send()
        rowc = lax.broadcasted_iota(jnp.int32, (8, cols), 0)
        out = jnp.zeros((8, cols), F32)
        for k in range(8):
            out = jnp.where(rowc == k, recv_buf[k], out)
        mod_ref[...] = out

    return _launch(
        body, "ada_forward",
        out_shape=[jax.ShapeDtypeStruct((8, cols), F32), jax.ShapeDtypeStruct((8, D_MODEL), F32)],
        in_specs=[pl.BlockSpec(memory_space=pltpu.VMEM)] * 3,
        out_specs=[pl.BlockSpec(memory_space=pltpu.VMEM)] * 2,
        operands=(c_row, w_ada, b_ada8),
        scratch=[pltpu.VMEM((8, 8, D_MODEL), F32), pltpu.VMEM((8, 8, cols), F32), pltpu.VMEM((8, 8, cols), F32)]
        + [pltpu.SemaphoreType.DMA((7,))] * 4,
        carry=carry)


def _allreduce_small(pack, order, name):
    rows = pack.shape[1]

    def body(pack_ref, order_ref, total_ref, land_ref, send1, recv1, send2, recv2):
        x, y, c = _me()
        me = 4 * x + 2 * y + c
        land_ref[me] = pack_ref[me]
        copies = []
        for rel in range(1, 8):
            to, p = _peer(rel)
            cp = pltpu.make_async_remote_copy(src_ref=pack_ref.at[p], dst_ref=land_ref.at[me], send_sem=send1.at[rel - 1],
                                              recv_sem=recv1.at[rel - 1], device_id=to, device_id_type=MESH)
            cp.start()
            copies.append(cp)
        for rel in range(1, 8):
            _, p = _peer(rel)
            pltpu.make_async_remote_copy(src_ref=pack_ref.at[p], dst_ref=land_ref.at[p], send_sem=send1.at[rel - 1],
                                         recv_sem=recv1.at[rel - 1], device_id=(x, y, c), device_id_type=MESH).wait_recv()
        for cp in copies:
            cp.wait_send()
        acc = land_ref[0]
        for b in range(1, 8):
            acc = acc + land_ref[b]
        total_ref[me] = acc
        copies = []
        for rel in range(1, 8):
            to, _ = _peer(rel)
            cp = pltpu.make_async_remote_copy(src_ref=total_ref.at[me], dst_ref=total_ref.at[me], send_sem=send2.at[rel - 1],
                                              recv_sem=recv2.at[rel - 1], device_id=to, device_id_type=MESH)
            cp.start()
            copies.append(cp)
        for rel in range(1, 8):
            _, p = _peer(rel)
            pltpu.make_async_remote_copy(src_ref=total_ref.at[p], dst_ref=total_ref.at[p], send_sem=send2.at[rel - 1],
                                         recv_sem=recv2.at[rel - 1], device_id=(x, y, c), device_id_type=MESH).wait_recv()
        for cp in copies:
            cp.wait_send()

    return pl.pallas_call(
        body, name=name,
        out_shape=[jax.ShapeDtypeStruct((8, rows, 128), F32), jax.ShapeDtypeStruct((8, rows, 128), F32)],
        in_specs=[pl.BlockSpec(memory_space=pltpu.VMEM), _HBM],
        out_specs=[pl.BlockSpec(memory_space=pltpu.VMEM)] * 2,
        scratch_shapes=[pltpu.SemaphoreType.DMA((7,))] * 4,
        compiler_params=pltpu.CompilerParams(vmem_limit_bytes=VMEM_LIMIT),
    )(pack, order)


def _modulated(x, prm_ref, sub, g_row):
    shift, scale = _row(prm_ref, 3 * sub), _row(prm_ref, 3 * sub + 1)
    g = _row(prm_ref, g_row)
    r = _rms_scale(x)
    n0 = x * r
    return (n0 * g) * (1.0 + scale) + shift, r, n0


def _swiglu_tile(xv, prm_ref, win_ref, wout_ref, ab_ref, sub, g_row):
    h, _, _ = _modulated(xv, prm_ref, sub, g_row)
    hb = h.astype(MXU_DTYPE)
    acc = None
    for j in range(4):
        a = _mm_nt(hb, win_ref[0, j])
        b = _mm_nt(hb, win_ref[1, j])
        ab_ref[0, j] = a.astype(SAVE_DTYPE)
        ab_ref[1, j] = b.astype(SAVE_DTYPE)
        t = _mm((a * _sigmoid(a)) * b, wout_ref[j])
        acc = t if acc is None else acc + t
    return acc


def _loss_tile(xv, target, g):
    r = _rms_scale(xv)
    n0 = xv * r
    err = n0 * g - target
    dy = err / float(D_MODEL)
    dn0 = dy * g
    dx = r * (dn0 - n0 * jnp.mean(dn0 * n0, axis=-1, keepdims=True))
    loss = 0.5 * jnp.sum(jnp.mean(err * err, axis=-1, keepdims=True), axis=0, keepdims=True)
    return dx, _colsum(dy * n0), loss


def _ffn_forward_loss(x, target, prm, win, wout, sub, g_row, name):
    T = x.shape[0]
    tm = min(T, TM_FFN)

    def body(x_ref, t_ref, prm_ref, win_ref, wout_ref, dx_ref, sums_ref, f_ref, ab_ref):
        i = pl.program_id(0)
        xv = x_ref[...]
        acc = _swiglu_tile(xv, prm_ref, win_ref, wout_ref, ab_ref, sub, g_row)
        f_ref[...] = acc.astype(SAVE_DTYPE)
        dx, dg, loss = _loss_tile(xv + (0.5 * _row(prm_ref, 3 * sub + 2)) * acc, t_ref[...], _row(prm_ref, ROW_G_FINAL))
        dx_ref[...] = dx
        upd = jnp.concatenate([dg, jnp.broadcast_to(loss, (1, D_MODEL)), jnp.zeros((6, D_MODEL), F32)], axis=0)

        @pl.when(i == 0)
        def _():
            sums_ref[...] = upd

        @pl.when(i > 0)
        def _():
            sums_ref[...] += upd

    tok = pl.BlockSpec((tm, D_MODEL), lambda i: (i, 0))
    return _launch(
        body, name, grid=(T // tm,), semantics=("arbitrary",),
        out_shape=[jax.ShapeDtypeStruct((T, D_MODEL), F32), jax.ShapeDtypeStruct((8, D_MODEL), F32),
                   jax.ShapeDtypeStruct((T, D_MODEL), SAVE_DTYPE), jax.ShapeDtypeStruct((2, 4, T, FF_SHARD), SAVE_DTYPE)],
        in_specs=[tok, tok, _resident(prm), _resident(win), _resident(wout)],
        out_specs=[tok, pl.BlockSpec((8, D_MODEL), lambda i: (0, 0)), tok,
                   pl.BlockSpec((2, 4, tm, FF_SHARD), lambda i: (0, 0, i, 0))],
        operands=(x, target, prm, win, wout))


def _ffn_hidden(x, prm, win, sub, g_row, name, carry=None):
    T = x.shape[0]
    tm = min(T, TM_FFN)

    def body(x_ref, prm_ref, win_ref, ab_ref, s_ref):
        h, _, _ = _modulated(x_ref[...], prm_ref, sub, g_row)
        hb = h.astype(MXU_DTYPE)
        for j in range(4):
            a = _mm_nt(hb, win_ref[0, j])
            b = _mm_nt(hb, win_ref[1, j])
            ab_ref[0, j] = a.astype(SAVE_DTYPE)
            ab_ref[1, j] = b.astype(SAVE_DTYPE)
            s_ref[j] = ((a * _sigmoid(a)) * b).astype(MXU_DTYPE)

    return _launch(
        body, name, grid=(T // tm,), semantics=("arbitrary",),
        out_shape=[jax.ShapeDtypeStruct((2, 4, T, FF_SHARD), SAVE_DTYPE), jax.ShapeDtypeStruct((4, T, FF_SHARD), MXU_DTYPE)],
        in_specs=[pl.BlockSpec((tm, D_MODEL), lambda i: (i, 0)), _resident(prm), _resident(win)],
        out_specs=[pl.BlockSpec((2, 4, tm, FF_SHARD), lambda i: (0, 0, i, 0)),
                   pl.BlockSpec((4, tm, FF_SHARD), lambda i: (0, i, 0))],
        operands=(x, prm, win), carry=carry, steps=_grid_steps(T // tm))


def _ffn_out(x, s, prm, wout, sub, name, carry=None):
    T = x.shape[0]
    tm = min(T, TM_FFN)

    def body(x_ref, s_ref, prm_ref, wout_ref, xo_ref, f_ref):
        acc = None
        for j in range(4):
            t = _mm(s_ref[j], wout_ref[j])
            acc = t if acc is None else acc + t
        f_ref[...] = acc.astype(SAVE_DTYPE)
        xo_ref[...] = x_ref[...] + (0.5 * _row(prm_ref, 3 * sub + 2)) * acc

    tok = pl.BlockSpec((tm, D_MODEL), lambda i: (i, 0))
    return _launch(
        body, name, grid=(T // tm,), semantics=("arbitrary",),
        out_shape=[jax.ShapeDtypeStruct((T, D_MODEL), F32), jax.ShapeDtypeStruct((T, D_MODEL), SAVE_DTYPE)],
        in_specs=[tok, pl.BlockSpec((4, tm, FF_SHARD), lambda i: (0, i, 0)), _resident(prm), _resident(wout)],
        out_specs=[tok, tok], operands=(x, s, prm, wout), carry=carry, steps=_grid_steps(T // tm))


def _ffn_backward(d, ab, prm, win, wout, sub, name, carry=None):
    T = d.shape[0]
    tm = min(T, TM_FFN)
    nt = T // tm
    chunk = min(tm, FFN_BWD_CHUNK)

    def body(d_ref, ab_ref, prm_ref, win_ref, wout_ref, dh_ref, dab_ref, dwout_ref, acc_out):
        i = pl.program_id(1)

        @pl.when(i == 0)
        def _():
            acc_out[...] = jnp.zeros_like(acc_out)

        wa, wb, wo = win_ref[0, 0], win_ref[1, 0], wout_ref[0]
        half_gate = 0.5 * _row(prm_ref, 3 * sub + 2)
        ss, dfss = [], []
        for ck in range(tm // chunk):
            rows = slice(ck * chunk, (ck + 1) * chunk)
            a = ab_ref[0, 0, rows, :].astype(F32)
            b = ab_ref[1, 0, rows, :].astype(F32)
            sg = _sigmoid(a)
            si = a * sg
            dfs = (half_gate * d_ref[rows, :]).astype(MXU_DTYPE)
            ds = _mm_nt(dfs, wo)
            da = (ds * b * (sg * (1.0 + a * (1.0 - sg)))).astype(MXU_DTYPE)
            db = (ds * si).astype(MXU_DTYPE)
            dh_ref[0, rows, :] = (_mm(da, wa) + _mm(db, wb)).astype(SAVE_DTYPE)
            dab_ref[0, 0, rows, :] = da
            dab_ref[1, 0, rows, :] = db
            ss.append((si * b).astype(MXU_DTYPE))
            dfss.append(dfs)
        cat = (lambda v: v[0]) if len(ss) == 1 else (lambda v: jnp.concatenate(v, axis=0))
        acc_out[...] += _mm_tn(cat(ss), cat(dfss))

        @pl.when(i == nt - 1)
        def _():
            dwout_ref[0] = acc_out[...].astype(WIRE_DTYPE)

    def steps():
        j, i = pl.program_id(0), pl.program_id(1)
        return (j == 0) & (i == 0), (j == 2) & (i == 0), (j == 3) & (i == nt - 1)

    pre = pl.BlockSpec((2, 1, tm, FF_SHARD), lambda j, i: (0, j, i, 0))
    return _launch(
        body, name, grid=(4, nt), semantics=("arbitrary", "arbitrary"),
        out_shape=[jax.ShapeDtypeStruct((4, T, D_MODEL), SAVE_DTYPE), jax.ShapeDtypeStruct(ab.shape, MXU_DTYPE),
                   jax.ShapeDtypeStruct(wout.shape, WIRE_DTYPE)],
        in_specs=[pl.BlockSpec((tm, D_MODEL), lambda j, i: (i, 0)), pre, _resident(prm),
                  pl.BlockSpec((2, 1, FF_SHARD, D_MODEL), lambda j, i: (0, j, 0, 0)),
                  pl.BlockSpec((1, FF_SHARD, D_MODEL), lambda j, i: (j, 0, 0))],
        out_specs=[pl.BlockSpec((1, tm, D_MODEL), lambda j, i: (j, i, 0)), pre,
                   pl.BlockSpec((1, FF_SHARD, D_MODEL), lambda j, i: (j, 0, 0))],
        operands=(d, ab, prm, win, wout), scratch=[pltpu.VMEM((FF_SHARD, D_MODEL), F32)], carry=carry, steps=steps)


def _ffn_dwin(x, dab, prm, sub, g_row, name, carry=None):
    T = x.shape[0]
    tm = min(T, TM_FFN)
    nt = T // tm

    def body(x_ref, dab_ref, prm_ref, dwin_ref, acc):
        i = pl.program_id(1)

        @pl.when(i == 0)
        def _():
            acc[...] = jnp.zeros_like(acc)

        h, _, _ = _modulated(x_ref[...], prm_ref, sub, g_row)
        hb = h.astype(MXU_DTYPE)
        acc[0] += _mm_tn(dab_ref[0, 0], hb)
        acc[1] += _mm_tn(dab_ref[1, 0], hb)

        @pl.when(i == nt - 1)
        def _():
            dwin_ref[0, 0] = acc[0].astype(WIRE_DTYPE)
            dwin_ref[1, 0] = acc[1].astype(WIRE_DTYPE)

    def steps():
        j, i = pl.program_id(0), pl.program_id(1)
        return (j == 0) & (i == 0), (j == 2) & (i == 0), (j == 3) & (i == nt - 1)

    return _launch(
        body, name, grid=(4, nt), semantics=("arbitrary", "arbitrary"),
        out_shape=[jax.ShapeDtypeStruct((2, 4, FF_SHARD, D_MODEL), WIRE_DTYPE)],
        in_specs=[pl.BlockSpec((tm, D_MODEL), lambda j, i: (i, 0)),
                  pl.BlockSpec((2, 1, tm, FF_SHARD), lambda j, i: (0, j, i, 0)), _resident(prm)],
        out_specs=[pl.BlockSpec((2, 1, FF_SHARD, D_MODEL), lambda j, i: (0, j, 0, 0))],
        operands=(x, dab, prm), scratch=[pltpu.VMEM((2, FF_SHARD, D_MODEL), F32)], carry=carry, steps=steps)


def _norm_backward_tile(dh, xv, dv, fv, prm_ref, sub, g_row, gate_coef):
    scale, g = _row(prm_ref, 3 * sub + 1), _row(prm_ref, g_row)
    r = _rms_scale(xv)
    n0 = xv * r
    dn = dh * (1.0 + scale)
    dn0 = dn * g
    dx = dv + r * (dn0 - n0 * jnp.mean(dn0 * n0, axis=-1, keepdims=True))
    upd = jnp.concatenate([_colsum(dn * n0), _colsum(dh), _colsum(dh * (n0 * g)),
                           gate_coef * _colsum(dv * fv.astype(F32)), jnp.zeros((4, D_MODEL), F32)], axis=0)
    return dx, upd


def _norm_backward(parts, x, d, f, prm, sub, g_row, gate_coef, name):
    T = x.shape[0]
    tm = min(T, TM_EW)
    P = parts.shape[0]

    def body(p_ref, x_ref, d_ref, f_ref, prm_ref, dx_ref, sums_ref):
        i = pl.program_id(0)
        dh = p_ref[0].astype(F32)
        for k in range(1, P):
            dh = dh + p_ref[k].astype(F32)
        dx_ref[...], upd = _norm_backward_tile(dh, x_ref[...], d_ref[...], f_ref[...], prm_ref, sub, g_row, gate_coef)

        @pl.when(i == 0)
        def _():
            sums_ref[...] = upd

        @pl.when(i > 0)
        def _():
            sums_ref[...] += upd

    tok = pl.BlockSpec((tm, D_MODEL), lambda i: (i, 0))
    return _launch(
        body, name, grid=(T // tm,), semantics=("arbitrary",),
        out_shape=[jax.ShapeDtypeStruct((T, D_MODEL), F32), jax.ShapeDtypeStruct((8, D_MODEL), F32)],
        in_specs=[pl.BlockSpec((P, tm, D_MODEL), lambda i: (0, i, 0)), tok, tok, tok, _resident(prm)],
        out_specs=[tok, pl.BlockSpec((8, D_MODEL), lambda i: (0, 0))],
        operands=(parts, x, d, f, prm))


def _ssm_discretise(lam_re_log, lam_im, log_dt):
    lr = -jnp.exp(lam_re_log)
    dt = jnp.exp(log_dt)
    mag = jnp.exp(lr * dt)
    ang = lam_im * dt
    ab_re = mag * jnp.cos(ang)
    ab_im = mag * jnp.sin(ang)
    num_re = ab_re - 1.0
    num_im = ab_im
    den = lr * lr + lam_im * lam_im
    f_re = (num_re * lr + num_im * lam_im) / den
    f_im = (num_im * lr - num_re * lam_im) / den
    return ab_re, ab_im, f_re, f_im


def _ssm_params_forward(lam_re_log, lam_im, log_dt):
    def body(a_ref, b_ref, c_ref, o0, o1, o2, o3):
        outs = _ssm_discretise(a_ref[...], b_ref[...], c_ref[...])
        for o, v in zip((o0, o1, o2, o3), outs):
            o[...] = v

    return pl.pallas_call(body, name="ssm_params_forward",
                          out_shape=[jax.ShapeDtypeStruct(lam_im.shape, F32)] * 4)(lam_re_log, lam_im, log_dt)


def _ssm_params_backward(lam_re_log, lam_im, log_dt, cot):
    def body(a_ref, b_ref, c_ref, g0, g1, g2, g3, o0, o1, o2):
        _, vjp = jax.vjp(_ssm_discretise, a_ref[...], b_ref[...], c_ref[...])
        d0, d1, d2 = vjp((g0[...], g1[...], g2[...], g3[...]))
        o0[...] = d0
        o1[...] = d1
        o2[...] = d2

    return pl.pallas_call(
        body, name="ssm_params_backward",
        out_shape=[jax.ShapeDtypeStruct(lam_im.shape, F32), jax.ShapeDtypeStruct(lam_im.shape, F32),
                   jax.ShapeDtypeStruct(log_dt.shape, F32)])(lam_re_log, lam_im, log_dt, *cot)


def _ssm_dense_forward(srow, b_dense, c_dense):
    def body(srow_ref, bd_ref, cd_ref, bb_ref, ct_ref):
        for j in range(SSM_BLOCKS):
            lanes = slice(j * SSM_BLOCK_STATE, (j + 1) * SSM_BLOCK_STATE)
            f_re, f_im = srow_ref[2:3, lanes], srow_ref[3:4, lanes]
            bb_ref[0, j] = (f_re * bd_ref[0, j] - f_im * bd_ref[1, j]).astype(MXU_DTYPE)
            bb_ref[1, j] = (f_re * bd_ref[1, j] + f_im * bd_ref[0, j]).astype(MXU_DTYPE)
            ct_ref[0, j] = cd_ref[0, j].astype(MXU_DTYPE)
            ct_ref[1, j] = cd_ref[1, j].astype(MXU_DTYPE)

    return pl.pallas_call(body, name="ssm_dense_forward",
                          out_shape=[jax.ShapeDtypeStruct(b_dense.shape, MXU_DTYPE),
                                     jax.ShapeDtypeStruct(c_dense.shape, MXU_DTYPE)],
                          compiler_params=pltpu.CompilerParams(vmem_limit_bytes=VMEM_LIMIT))(srow, b_dense, c_dense)


def _cmul(p, q):
    return p[0] * q[0] - p[1] * q[1], p[0] * q[1] + p[1] * q[0]


def _scan_coefficients(ar, ai, reverse):
    n = ar.shape[1]
    p = {1: (ar, ai)}
    p[2] = _cmul(p[1], p[1])
    p[3] = _cmul(p[2], p[1])
    p[4] = _cmul(p[2], p[2])
    p[5] = _cmul(p[4], p[1])
    p[6] = _cmul(p[4], p[2])
    p[7] = _cmul(p[4], p[3])
    p[8] = _cmul(p[4], p[4])
    rowi = lax.broadcasted_iota(jnp.int32, (SCAN_ROWS, n), 0)
    tiles = []
    for dstep in (1, 2, 4):
        keep = (rowi < SCAN_ROWS - dstep) if reverse else (rowi >= dstep)
        for part in p[dstep]:
            tiles.append(jnp.where(keep, jnp.broadcast_to(part, (SCAN_ROWS, n)), 0.0))
    for comp in (0, 1):
        t = jnp.zeros((SCAN_ROWS, n), F32)
        for rr in range(SCAN_ROWS):
            power = SCAN_ROWS - rr if reverse else rr + 1
            t = jnp.where(rowi == rr, jnp.broadcast_to(p[power][comp], (SCAN_ROWS, n)), t)
        tiles.append(t)
    return tiles


def _load_stack(stack_hbm, dst, sems, base):
    cols = stack_hbm.shape[2]
    cps = [pltpu.make_async_copy(stack_hbm.at[k], dst.at[:, pl.ds(k * cols, cols)], sems.at[base + k])
           for k in range(NDEV)]
    for cp in cps:
        cp.start()
    return cps


def _window_lanes():
    lane = lax.broadcasted_iota(jnp.int32, (1, POOL_WIDTH), 1)
    return jnp.where(lane < 128, 2.0, jnp.where(lane < 256, 4.0, jnp.where(lane < 384, 8.0, 16.0)))


def _gelu(y):
    return 0.5 * y * (1.0 + lax.erf(y * 0.7071067811865476))


def _gelu_grad(y):
    return 0.5 * (1.0 + lax.erf(y * 0.7071067811865476)) + y * jnp.exp(-0.5 * y * y) * 0.3989422804014327


def _mixer_forward(x, prm, w_in_s, w_pu_s, w_glu_s, w_su_s, w_out, pool_w, mvec, srow, bb, ct, carry=None):
    T = x.shape[0]
    tm = min(T, TM_MIX)
    nt = T // tm
    n_tiles = tm // SCAN_ROWS

    def body(x_ref, prm_ref, w_in_h, w_pu_h, w_glu_h, w_su_h, w_out_h, pw_ref, mv_ref, srow_ref, bb, ct,
             x2_ref, mo_ref, z_ref, sre_ref, sim_ref, zp_ref, q_ref, yp_ref, yss_ref, vg_ref, ys_ref,
             w_in, w_pu, w_glu, w_su, w_o, coef, carry, hist, bu, sems):
        i = pl.program_id(0)

        @pl.when(i == 0)
        def _():
            cps = (_load_stack(w_in_h, w_in, sems, 0) + _load_stack(w_pu_h, w_pu, sems, 8)
                   + _load_stack(w_glu_h, w_glu, sems, 16) + _load_stack(w_su_h, w_su, sems, 24))
            cps.append(pltpu.make_async_copy(w_out_h, w_o, sems.at[32]))
            cps[-1].start()
            for j in range(SSM_BLOCKS):
                lanes = slice(j * SSM_BLOCK_STATE, (j + 1) * SSM_BLOCK_STATE)
                for k, t in enumerate(_scan_coefficients(srow_ref[0:1, lanes], srow_ref[1:2, lanes], False)):
                    coef[j, k] = t
            carry[...] = jnp.zeros_like(carry)
            hist[...] = jnp.zeros_like(hist)
            for cp in cps:
                cp.wait()

        xv = x_ref[...]
        h, _, _ = _modulated(xv, prm_ref, 1, ROW_G_MIX)
        z = _mm(h, w_in[...])
        z_ref[...] = z.astype(SAVE_DTYPE)
        u_pool, u_ssm = z[:, 0:512], z[:, 512:1024]
        gl_pool, gl_ssm = z[:, 1024:2048], z[:, 2048:3072]

        ext = jnp.concatenate([hist[...], u_pool], axis=0)
        w2 = ext + pltpu.roll(ext, 1, 0)
        w4 = w2[:, 128:] + pltpu.roll(w2[:, 128:], 2, 0)
        w8 = w4[:, 128:] + pltpu.roll(w4[:, 128:], 4, 0)
        w16 = w8[:, 128:] + pltpu.roll(w8[:, 128:], 8, 0)
        wsum = jnp.concatenate([w2[POOL_HALO:, :128], w4[POOL_HALO:, :128], w8[POOL_HALO:, :128], w16[POOL_HALO:]], axis=1)
        hist[...] = u_pool[tm - POOL_HALO:, :]
        t1 = (lax.broadcasted_iota(jnp.int32, (tm, 1), 0) + (i * tm + 1)).astype(F32)
        zp = wsum / jnp.minimum(t1, _window_lanes()) - u_pool
        zp_ref[...] = zp.astype(SAVE_DTYPE)
        q = jnp.concatenate([_mm(zp[:, k * 128:(k + 1) * 128], pw_ref[k]) for k in range(4)], axis=1)
        q = q + mv_ref[ROW_POOL_B:ROW_POOL_B + 1, 0:512]
        q_ref[...] = q.astype(SAVE_DTYPE)
        y_pool = _mm(q * mv_ref[ROW_POOL_SCALE:ROW_POOL_SCALE + 1, 0:512], w_pu[...])
        yp_ref[...] = y_pool.astype(SAVE_DTYPE)

        y_blocks = []
        for j in range(SSM_BLOCKS):
            lanes = pl.ds(j * SSM_BLOCK_STATE, SSM_BLOCK_STATE)
            ub = u_ssm[:, j * 128:(j + 1) * 128].astype(MXU_DTYPE)
            bu[0] = _mm(ub, bb[0, j])
            bu[1] = _mm(ub, bb[1, j])
            a1r, a1i, a2r, a2i, a4r, a4i, pr, pi = [coef[j, k] for k in range(8)]

            def step(tt, c, lanes=lanes, a1r=a1r, a1i=a1i, a2r=a2r, a2i=a2i, a4r=a4r, a4i=a4i, pr=pr, pi=pi):
                cr, ci = c
                rows = pl.ds(pl.multiple_of(tt * SCAN_ROWS, SCAN_ROWS), SCAN_ROWS)
                xr, xi = bu[0, rows, :], bu[1, rows, :]
                for dstep, kr, ki in ((1, a1r, a1i), (2, a2r, a2i), (4, a4r, a4i)):
                    sr, si = pltpu.roll(xr, dstep, 0), pltpu.roll(xi, dstep, 0)
                    xr, xi = xr + kr * sr - ki * si, xi + kr * si + ki * sr
                xr, xi = xr + pr * cr - pi * ci, xi + pr * ci + pi * cr
                sre_ref[rows, lanes] = xr
                sim_ref[rows, lanes] = xi
                return (jnp.broadcast_to(xr[SCAN_ROWS - 1:SCAN_ROWS, :], xr.shape),
                        jnp.broadcast_to(xi[SCAN_ROWS - 1:SCAN_ROWS, :], xi.shape))

            cr, ci = lax.fori_loop(0, n_tiles, step, (carry[j, 0], carry[j, 1]))
            carry[j, 0] = cr
            carry[j, 1] = ci
            y_blocks.append(_mm(sre_ref[:, lanes], ct[0, j]) - _mm(sim_ref[:, lanes], ct[1, j]))
        yss = jnp.concatenate(y_blocks, axis=1) + mv_ref[ROW_SSM_D:ROW_SSM_D + 1, 0:512] * u_ssm
        yss_ref[...] = yss.astype(SAVE_DTYPE)
        vg = _mm(_gelu(yss), w_glu[...]) + mv_ref[ROW_B_GLU:ROW_B_GLU + 1, :]
        vg_ref[...] = vg.astype(SAVE_DTYPE)
        y_ssm = _mm(vg[:, 0:512] * _sigmoid(vg[:, 512:1024]), w_su[...])
        ys_ref[...] = y_ssm.astype(SAVE_DTYPE)

        merged = _sigmoid(gl_pool) * y_pool + _sigmoid(gl_ssm) * y_ssm
        mo = _mm(merged, w_o[...])
        mo_ref[...] = mo.astype(SAVE_DTYPE)
        x2_ref[...] = xv + _row(prm_ref, 5) * mo

    def tok(width):
        return pl.BlockSpec((tm, width), lambda i: (i, 0))

    hbm = _HBM
    widths = (D_MODEL, D_MODEL, IN_WIDTH, N_STATE, N_STATE, 512, 512, D_MODEL, 512, D_MODEL, D_MODEL)
    dtypes = (F32, SAVE_DTYPE, SAVE_DTYPE, F32, F32) + (SAVE_DTYPE,) * 6
    return _launch(
        body, "mixer_forward", grid=(nt,), semantics=("arbitrary",), carry=carry, steps=_grid_steps(nt),
        out_shape=[jax.ShapeDtypeStruct((T, w), dt) for w, dt in zip(widths, dtypes)],
        in_specs=[tok(D_MODEL), _resident(prm), hbm, hbm, hbm, hbm, hbm, _resident(pool_w), _resident(mvec),
                  _resident(srow), _resident(bb), _resident(ct)],
        out_specs=[tok(w) for w in widths],
        operands=(x, prm, w_in_s, w_pu_s, w_glu_s, w_su_s, w_out, pool_w, mvec, srow, bb, ct),
        scratch=[
            pltpu.VMEM((D_MODEL, IN_WIDTH), MXU_DTYPE), pltpu.VMEM((512, D_MODEL), MXU_DTYPE),
            pltpu.VMEM((512, D_MODEL), MXU_DTYPE), pltpu.VMEM((512, D_MODEL), MXU_DTYPE),
            pltpu.VMEM((D_MODEL, D_MODEL), MXU_DTYPE),
            pltpu.VMEM((SSM_BLOCKS, 8, SCAN_ROWS, SSM_BLOCK_STATE), F32),
            pltpu.VMEM((SSM_BLOCKS, 2, SCAN_ROWS, SSM_BLOCK_STATE), F32),
            pltpu.VMEM((POOL_HALO, POOL_WIDTH), F32),
            pltpu.VMEM((2, tm, SSM_BLOCK_STATE), F32),
            pltpu.SemaphoreType.DMA((33,)),
        ])


def _mixer_backward(d2, prm, saved, w_pu_s, w_glu_s, w_su_s, w_out, pool_w, mvec, srow, bb, ct, carry=None):
    z, s_re, s_im, zp, q, y_pool, yss, vg, y_ssm = saved
    T = d2.shape[0]
    tm = min(T, TM_MIX_BWD)
    nt = T // tm
    n_tiles = tm // SCAN_ROWS

    def body(d_ref, prm_ref, z_ref, sre_ref, sim_ref, zp_ref, q_ref, yp_ref, yss_ref, vg_ref, ys_ref,
             w_pu_h, w_glu_h, w_su_h, w_out_h, pw_ref, mv_ref, srow_ref, bb, ct,
             dz_ref, dwo_h, dwpu_h, dwglu_h, dwsu_h, dpw_h, dbb_h, dct_h, vsum_h, da_h,
             w_pu, w_glu, w_su, w_o, pwb, coef, carry, hist, dre, lam,
             a_wo, a_wpu, a_wglu, a_wsu, a_pw, a_bb, a_ct, a_vs, a_da, st_wo, st_up, sems):
        i = pl.program_id(0)
        tile = nt - 1 - i

        @pl.when(i == 0)
        def _():
            cps = (_load_stack(w_pu_h, w_pu, sems, 0) + _load_stack(w_glu_h, w_glu, sems, 8)
                   + _load_stack(w_su_h, w_su, sems, 16))
            cps.append(pltpu.make_async_copy(w_out_h, w_o, sems.at[24]))
            cps[-1].start()
            pwb[...] = pw_ref[...].astype(MXU_DTYPE)
            for j in range(SSM_BLOCKS):
                lanes = slice(j * SSM_BLOCK_STATE, (j + 1) * SSM_BLOCK_STATE)
                for k, t in enumerate(_scan_coefficients(srow_ref[0:1, lanes], srow_ref[1:2, lanes], True)):
                    coef[j, k] = t
            for acc in (carry, hist, a_wo, a_wpu, a_wglu, a_wsu, a_pw, a_bb, a_ct, a_vs, a_da):
                acc[...] = jnp.zeros_like(acc)
            for cp in cps:
                cp.wait()

        dv = d_ref[...]
        zt = z_ref[...].astype(F32)
        u_ssm, gl_pool, gl_ssm = zt[:, 512:1024], zt[:, 1024:2048], zt[:, 2048:3072]
        y_p, y_s = yp_ref[...].astype(F32), ys_ref[...].astype(F32)
        sgp, sgs = _sigmoid(gl_pool), _sigmoid(gl_ssm)
        dmo = (_row(prm_ref, 5) * dv).astype(MXU_DTYPE)
        a_wo[...] += _mm_tn(sgp * y_p + sgs * y_s, dmo)
        dmerged = _mm_nt(dmo, w_o[...])
        dy_pool = dmerged * sgp
        dgl_pool = dmerged * y_p * (sgp * (1.0 - sgp))
        dy_ssm = dmerged * sgs
        dgl_ssm = dmerged * y_s * (sgs * (1.0 - sgs))

        scale = mv_ref[ROW_POOL_SCALE:ROW_POOL_SCALE + 1, 0:512]
        qv, zpv = q_ref[...].astype(F32), zp_ref[...]
        a_wpu[...] += _mm_tn(qv * scale, dy_pool)
        dp = _mm_nt(dy_pool, w_pu[...])
        dq = dp * scale
        a_vs[0:1, 0:512] += _colsum(dp * qv)
        a_vs[1:2, 0:512] += _colsum(dq)
        dzp_blocks = []
        for k in range(4):
            lanes = slice(k * 128, (k + 1) * 128)
            dzp_blocks.append(_mm_nt(dq[:, lanes], pwb[k]))
            a_pw[k] += _mm_tn(zpv[:, lanes], dq[:, lanes])
        dzp = jnp.concatenate(dzp_blocks, axis=1)
        t1 = (lax.broadcasted_iota(jnp.int32, (tm, 1), 0) + (tile * tm + 1)).astype(F32)
        gs = dzp / jnp.minimum(t1, _window_lanes())
        n_ext = tm + POOL_HALO
        ext = jnp.concatenate([gs, hist[...]], axis=0)
        v2 = ext + pltpu.roll(ext, n_ext - 1, 0)
        v4 = v2[:, 128:] + pltpu.roll(v2[:, 128:], n_ext - 2, 0)
        v8 = v4[:, 128:] + pltpu.roll(v4[:, 128:], n_ext - 4, 0)
        v16 = v8[:, 128:] + pltpu.roll(v8[:, 128:], n_ext - 8, 0)
        msum = jnp.concatenate([v2[:tm, :128], v4[:tm, :128], v8[:tm, :128], v16[:tm]], axis=1)
        hist[...] = gs[0:POOL_HALO, :]
        du_pool = msum - dzp

        vgv = vg_ref[...].astype(F32)
        val, gate = vgv[:, 0:512], vgv[:, 512:1024]
        sgg = _sigmoid(gate)
        a_wsu[...] += _mm_tn(val * sgg, dy_ssm)
        do = _mm_nt(dy_ssm, w_su[...])
        dvg = jnp.concatenate([do * sgg, do * val * (sgg * (1.0 - sgg))], axis=1)
        a_vs[3:4, :] += _colsum(dvg)
        yv = yss_ref[...].astype(F32)
        a_wglu[...] += _mm_tn(_gelu(yv), dvg)
        dyss = _mm_nt(dvg, w_glu[...]) * _gelu_grad(yv)
        a_vs[2:3, 0:512] += _colsum(dyss * u_ssm)
        du_blocks = []
        for j in range(SSM_BLOCKS):
            lanes = pl.ds(j * SSM_BLOCK_STATE, SSM_BLOCK_STATE)
            in_lanes = slice(j * 128, (j + 1) * 128)
            dyb = dyss[:, in_lanes].astype(MXU_DTYPE)
            ub = u_ssm[:, in_lanes].astype(MXU_DTYPE)
            dre[0] = _mm_nt(dyb, ct[0, j])
            dre[1] = -_mm_nt(dyb, ct[1, j])
            a_ct[0, j] += _mm_tn(sre_ref[:, lanes], dyb)
            a_ct[1, j] -= _mm_tn(sim_ref[:, lanes], dyb)
            a1r, a1i, a2r, a2i, a4r, a4i, pr, pi = [coef[j, k] for k in range(8)]
            rowi = lax.broadcasted_iota(jnp.int32, (SCAN_ROWS, SSM_BLOCK_STATE), 0)

            def step(tt, c, lanes=lanes, a1r=a1r, a1i=a1i, a2r=a2r, a2i=a2i, a4r=a4r, a4i=a4i, pr=pr, pi=pi, rowi=rowi):
                cr, ci, acc_r, acc_i = c
                rows = pl.ds(pl.multiple_of((n_tiles - 1 - tt) * SCAN_ROWS, SCAN_ROWS), SCAN_ROWS)
                xr, xi = dre[0, rows, :], dre[1, rows, :]
                for dstep, kr, ki in ((1, a1r, a1i), (2, a2r, a2i), (4, a4r, a4i)):
                    sr, si = pltpu.roll(xr, SCAN_ROWS - dstep, 0), pltpu.roll(xi, SCAN_ROWS - dstep, 0)
                    xr, xi = xr + kr * sr + ki * si, xi + kr * si - ki * sr
                xr, xi = xr + pr * cr + pi * ci, xi + pr * ci - pi * cr
                lam[0, rows, :] = xr
                lam[1, rows, :] = xi
                nr = jnp.where(rowi == SCAN_ROWS - 1, cr, pltpu.roll(xr, SCAN_ROWS - 1, 0))
                ni = jnp.where(rowi == SCAN_ROWS - 1, ci, pltpu.roll(xi, SCAN_ROWS - 1, 0))
                s_r, s_i = sre_ref[rows, lanes], sim_ref[rows, lanes]
                acc_r = acc_r + nr * s_r + ni * s_i
                acc_i = acc_i + ni * s_r - nr * s_i
                return (jnp.broadcast_to(xr[0:1, :], xr.shape), jnp.broadcast_to(xi[0:1, :], xi.shape), acc_r, acc_i)

            cr, ci, acc_r, acc_i = lax.fori_loop(0, n_tiles, step, (carry[j, 0], carry[j, 1], a_da[0, j], a_da[1, j]))
            carry[j, 0] = cr
            carry[j, 1] = ci
            a_da[0, j] = acc_r
            a_da[1, j] = acc_i
            lr_b, li_b = lam[0].astype(MXU_DTYPE), lam[1].astype(MXU_DTYPE)
            a_bb[0, j] += _mm_tn(ub, lr_b)
            a_bb[1, j] += _mm_tn(ub, li_b)
            du_blocks.append(_mm_nt(lr_b, bb[0, j]) + _mm_nt(li_b, bb[1, j]))
        du_ssm = jnp.concatenate(du_blocks, axis=1) + dyss * mv_ref[ROW_SSM_D:ROW_SSM_D + 1, 0:512]
        dz_ref[...] = jnp.concatenate([du_pool, du_ssm, dgl_pool, dgl_ssm], axis=1).astype(SAVE_DTYPE)

        @pl.when(i == nt - 1)
        def _():
            rows = D_MODEL // NDEV
            for k in range(NDEV):
                st_wo[k] = a_wo[k * rows:(k + 1) * rows, :].astype(WIRE_DTYPE)
                for a, acc in enumerate((a_wpu, a_wglu, a_wsu)):
                    st_up[a, k] = acc[:, k * 128:(k + 1) * 128].astype(WIRE_DTYPE)
            outs = ((st_wo, dwo_h), (st_up.at[0], dwpu_h), (st_up.at[1], dwglu_h), (st_up.at[2], dwsu_h),
                    (a_pw, dpw_h), (a_bb, dbb_h), (a_ct, dct_h), (a_vs, vsum_h), (a_da, da_h))
            cps = [pltpu.make_async_copy(src, dst, sems.at[k]) for k, (src, dst) in enumerate(outs)]
            for cp in cps:
                cp.start()
            for cp in cps:
                cp.wait()

    def tok(width):
        return pl.BlockSpec((tm, width), lambda i: (nt - 1 - i, 0))

    hbm = _HBM
    acc_shapes = [(D_MODEL, D_MODEL), (512, D_MODEL), (512, D_MODEL), (512, D_MODEL), (4, 128, 128),
                  (2, SSM_BLOCKS, 128, SSM_BLOCK_STATE), (2, SSM_BLOCKS, SSM_BLOCK_STATE, 128), (8, D_MODEL),
                  (2, SSM_BLOCKS, SCAN_ROWS, SSM_BLOCK_STATE)]
    stack_out = [jax.ShapeDtypeStruct((NDEV, D_MODEL // NDEV, D_MODEL), WIRE_DTYPE)] \
        + [jax.ShapeDtypeStruct((NDEV, 512, 128), WIRE_DTYPE)] * 3
    return _launch(
        body, "mixer_backward", grid=(nt,), semantics=("arbitrary",), carry=carry, steps=_grid_steps(nt),
        out_shape=[jax.ShapeDtypeStruct((T, IN_WIDTH), SAVE_DTYPE)] + stack_out
        + [jax.ShapeDtypeStruct(s, F32) for s in acc_shapes[4:]],
        in_specs=[tok(D_MODEL), _resident(prm), tok(IN_WIDTH), tok(N_STATE), tok(N_STATE), tok(512), tok(512),
                  tok(D_MODEL), tok(512), tok(D_MODEL), tok(D_MODEL), hbm, hbm, hbm, hbm, _resident(pool_w),
                  _resident(mvec), _resident(srow), _resident(bb), _resident(ct)],
        out_specs=[tok(IN_WIDTH)] + [hbm] * len(acc_shapes),
        operands=(d2, prm, z, s_re, s_im, zp, q, y_pool, yss, vg, y_ssm, w_pu_s, w_glu_s, w_su_s, w_out, pool_w, mvec,
                  srow, bb, ct),
        scratch=[
            pltpu.VMEM((512, D_MODEL), MXU_DTYPE), pltpu.VMEM((512, D_MODEL), MXU_DTYPE),
            pltpu.VMEM((512, D_MODEL), MXU_DTYPE), pltpu.VMEM((D_MODEL, D_MODEL), MXU_DTYPE),
            pltpu.VMEM((4, 128, 128), MXU_DTYPE),
            pltpu.VMEM((SSM_BLOCKS, 8, SCAN_ROWS, SSM_BLOCK_STATE), F32),
            pltpu.VMEM((SSM_BLOCKS, 2, SCAN_ROWS, SSM_BLOCK_STATE), F32),
            pltpu.VMEM((POOL_HALO, POOL_WIDTH), F32),
            pltpu.VMEM((2, tm, SSM_BLOCK_STATE), F32), pltpu.VMEM((2, tm, SSM_BLOCK_STATE), F32),
        ] + [pltpu.VMEM(s, F32) for s in acc_shapes]
        + [pltpu.VMEM((NDEV, D_MODEL // NDEV, D_MODEL), WIRE_DTYPE), pltpu.VMEM((3, NDEV, 512, 128), WIRE_DTYPE),
           pltpu.SemaphoreType.DMA((25,))])


def _mixer_in_backward(x, dz, d, mo, prm, w_in_s, carry=None):
    T = x.shape[0]
    tm = min(T, TM_MIX)
    nt = T // tm
    cols = IN_WIDTH // NDEV

    def body(x_ref, dz_ref, d_ref, mo_ref, prm_ref, w_in_h, dx_ref, sums_ref, dw_ref, w_in, acc, sems):
        i = pl.program_id(0)

        @pl.when(i == 0)
        def _():
            cps = _load_stack(w_in_h, w_in, sems, 0)
            acc[...] = jnp.zeros_like(acc)
            for cp in cps:
                cp.wait()

        xv = x_ref[...]
        h, _, _ = _modulated(xv, prm_ref, 1, ROW_G_MIX)
        dzb = dz_ref[...].astype(MXU_DTYPE)
        acc[...] += _mm_tn(h, dzb)
        dx_ref[...], upd = _norm_backward_tile(_mm_nt(dzb, w_in[...]), xv, d_ref[...], mo_ref[...], prm_ref, 1,
                                               ROW_G_MIX, 1.0)

        @pl.when(i == 0)
        def _():
            sums_ref[...] = upd

        @pl.when(i > 0)
        def _():
            sums_ref[...] += upd

        @pl.when(i == nt - 1)
        def _():
            for k in range(NDEV):
                dw_ref[k] = acc[:, k * cols:(k + 1) * cols].astype(WIRE_DTYPE)

    tok = pl.BlockSpec((tm, D_MODEL), lambda i: (i, 0))
    return _launch(
        body, "mixer_in_backward", grid=(nt,), semantics=("arbitrary",), carry=carry, steps=_grid_steps(nt),
        out_shape=[jax.ShapeDtypeStruct((T, D_MODEL), F32), jax.ShapeDtypeStruct((8, D_MODEL), F32),
                   jax.ShapeDtypeStruct((NDEV, D_MODEL, cols), WIRE_DTYPE)],
        in_specs=[tok, pl.BlockSpec((tm, IN_WIDTH), lambda i: (i, 0)), tok, tok, _resident(prm), _HBM],
        out_specs=[tok, pl.BlockSpec((8, D_MODEL), lambda i: (0, 0)),
                   pl.BlockSpec((NDEV, D_MODEL, cols), lambda i: (0, 0, 0))],
        operands=(x, dz, d, mo, prm, w_in_s),
        scratch=[pltpu.VMEM((D_MODEL, IN_WIDTH), MXU_DTYPE), pltpu.VMEM((D_MODEL, IN_WIDTH), F32),
                 pltpu.SemaphoreType.DMA((8,))])


def _ssm_dense_backward(dbb, da, srow, b_dense):
    def body(dbb_ref, da_ref, srow_ref, bd_ref, db_ref, df_ref):
        df_re, df_im = [], []
        da_re = [_colsum(da_ref[0, j]) for j in range(SSM_BLOCKS)]
        da_im = [_colsum(da_ref[1, j]) for j in range(SSM_BLOCKS)]
        for j in range(SSM_BLOCKS):
            lanes = slice(j * SSM_BLOCK_STATE, (j + 1) * SSM_BLOCK_STATE)
            f_re, f_im = srow_ref[2:3, lanes], srow_ref[3:4, lanes]
            g_re, g_im = dbb_ref[0, j], dbb_ref[1, j]
            b_re, b_im = bd_ref[0, j], bd_ref[1, j]
            db_ref[0, j] = f_re * g_re + f_im * g_im
            db_ref[1, j] = f_re * g_im - f_im * g_re
            df_re.append(_colsum(g_re * b_re + g_im * b_im))
            df_im.append(_colsum(g_im * b_re - g_re * b_im))
        df_ref[...] = jnp.concatenate([jnp.concatenate(df_re, axis=1), jnp.concatenate(df_im, axis=1),
                                       jnp.concatenate(da_re, axis=1), jnp.concatenate(da_im, axis=1),
                                       jnp.zeros((4, N_STATE), F32)], axis=0)

    return pl.pallas_call(body, name="ssm_dense_backward",
                          out_shape=[jax.ShapeDtypeStruct(b_dense.shape, F32), jax.ShapeDtypeStruct((8, N_STATE), F32)],
                          compiler_params=pltpu.CompilerParams(vmem_limit_bytes=VMEM_LIMIT))(dbb, da, srow, b_dense)


def _adamw_update(w, g, m, v):
    m = ADAM_B1 * m + (1.0 - ADAM_B1) * g
    v = ADAM_B2 * v + (1.0 - ADAM_B2) * (g * g)
    m_hat = m / (1.0 - ADAM_B1 ** ADAM_STEP)
    v_hat = v / (1.0 - ADAM_B2 ** ADAM_STEP)
    delta = -ADAM_LR * (m_hat / (jnp.sqrt(v_hat) + ADAM_EPS) + ADAM_WD * w)
    return delta, m, v


def _adam_rows(shape):
    rows, cols = shape
    tr = rows
    while tr * cols * 4 > (1 << 20) and tr % 16 == 0:
        tr //= 2
    return tr


def _adam_sharded(w, m, v, land, order, name):
    R, C = w.shape
    tr = _adam_rows((R, C))

    def body(w_ref, m_ref, v_ref, land_ref, order_ref, g_ref, d_ref, mo_ref, vo_ref):
        g = land_ref[0].astype(F32)
        for b in range(1, NDEV):
            g = g + land_ref[b].astype(F32)
        g_ref[...] = g
        d_ref[...], mo_ref[...], vo_ref[...] = _adamw_update(w_ref[...], g, m_ref[...], v_ref[...])

    blk = pl.BlockSpec((tr, C), lambda i: (i, 0))
    return pl.pallas_call(
        body, name=name, grid=(R // tr,),
        out_shape=[jax.ShapeDtypeStruct((R, C), F32)] * 4,
        in_specs=[blk, blk, blk, pl.BlockSpec((NDEV, tr, C), lambda i: (0, i, 0)), _HBM],
        out_specs=[blk] * 4,
        compiler_params=_params("arbitrary"),
    )(w, m, v, land, order)


def _adam_ada(w, m, v, sc_all, dmod_cols):
    R, C = w.shape
    tr = 256

    def body(w_ref, m_ref, v_ref, sc_ref, dm_ref, g_ref, d_ref, mo_ref, vo_ref):
        g = _mm_tn(sc_ref[...], dm_ref[...])
        g_ref[...] = g
        d_ref[...], mo_ref[...], vo_ref[...] = _adamw_update(w_ref[...], g, m_ref[...], v_ref[...])

    blk = pl.BlockSpec((tr, C), lambda i: (i, 0))
    return pl.pallas_call(
        body, name="adam_w_ada", grid=(R // tr,),
        out_shape=[jax.ShapeDtypeStruct((R, C), F32)] * 4,
        in_specs=[blk, blk, blk, pl.BlockSpec((8, tr), lambda i: (0, i)), pl.BlockSpec((8, C), lambda i: (0, 0))],
        out_specs=[blk] * 4,
        compiler_params=_params("arbitrary"),
    )(w, m, v, sc_all, dmod_cols)


def _adam_small(w, g, m, v, name):
    def body(w_ref, g_ref, m_ref, v_ref, d_ref, mo_ref, vo_ref):
        d_ref[...], mo_ref[...], vo_ref[...] = _adamw_update(w_ref[...], g_ref[...], m_ref[...], v_ref[...])

    return pl.pallas_call(body, name=name, out_shape=[jax.ShapeDtypeStruct(w.shape, F32)] * 3,
                          compiler_params=pltpu.CompilerParams(vmem_limit_bytes=VMEM_LIMIT))(w, g, m, v)


def _block_diag_in(b):
    bt = jnp.transpose(b, (0, 2, 1)).reshape(SSM_BLOCKS, 8, SSM_GROUP, SSM_STATE)
    eye = jnp.eye(8, dtype=bool)[None, :, None, :, None]
    return jnp.where(eye, bt[:, :, :, None, :], 0.0).reshape(SSM_BLOCKS, 128, SSM_BLOCK_STATE)


def _block_diag_out(c):
    ct = jnp.transpose(c, (0, 2, 1)).reshape(SSM_BLOCKS, 8, SSM_STATE, SSM_GROUP)
    eye = jnp.eye(8, dtype=bool)[None, :, None, :, None]
    return jnp.where(eye, ct[:, :, :, None, :], 0.0).reshape(SSM_BLOCKS, SSM_BLOCK_STATE, 128)


def _diag_blocks(dense, rows, cols):
    d5 = dense.reshape(SSM_BLOCKS, 8, rows, 8, cols)
    return jnp.stack([d5[:, a, :, a, :] for a in range(8)], axis=1).reshape(N_SSM_GROUPS, rows, cols)


def _pack_small(ada_vec, parts, params, tail=None):
    rest_rows, rows = _pack_rows(params, ada_vec is not None)
    rest = jnp.concatenate([parts[n].reshape(-1) for n, _ in params])
    rest = jnp.pad(rest, (0, NDEV * rest_rows * 128 - rest.shape[0])).reshape(NDEV, rest_rows, 128)
    head = [] if ada_vec is None else [ada_vec.reshape(NDEV, ADA_ROWS, 128)]
    pad = rows - rest_rows - (0 if ada_vec is None else ADA_ROWS)
    fill = jnp.zeros((NDEV, pad, 128), F32) if tail is None else jnp.pad(tail[None], ((0, NDEV - 1), (0, pad - 1), (0, 127)))
    return jnp.concatenate(head + [rest] + ([fill] if pad else []), axis=1)


def _unpack_small(pack, shapes, params, with_ada):
    rest_rows, _ = _pack_rows(params, with_ada)
    first = ADA_ROWS if with_ada else 0
    ada_vec = pack[:, :first].reshape(-1) if with_ada else None
    rest = pack[:, first:first + rest_rows].reshape(-1)
    out, off = {}, 0
    for n, size in params:
        out[n] = rest[off:off + size].reshape(shapes[n])
        off += size
    return ada_vec, out


WEIGHT_ORDER = ('w_ada', 'b_ada', 'g_ffn1', 'w_ffn1_in', 'w_ffn1_out', 'g_mix', 'w_in', 'pool_w', 'pool_b',
                'pool_scale', 'w_pool_up', 'ssm_lam_re_log', 'ssm_lam_im', 'ssm_log_dt', 'ssm_b_re', 'ssm_b_im',
                'ssm_c_re', 'ssm_c_im', 'ssm_d', 'w_glu', 'b_glu', 'w_ssm_up', 'w_out', 'g_ffn2', 'w_ffn2_in',
                'w_ffn2_out', 'g_final')
GATHERED = ('w_ffn1_in', 'w_ffn1_out', 'w_in', 'w_pool_up', 'w_glu', 'w_ssm_up', 'w_out', 'w_ffn2_in', 'w_ffn2_out')
TRANSPOSED = ('w_ffn1_in', 'w_ffn2_in')
STATE_MINOR = ('ssm_b_re', 'ssm_b_im')


def kernel(x, c, w_ada, b_ada, g_ffn1, w_ffn1_in, w_ffn1_out, g_mix, w_in, pool_w, pool_b, pool_scale, w_pool_up, ssm_lam_re_log, ssm_lam_im, ssm_log_dt, ssm_b_re, ssm_b_im, ssm_c_re, ssm_c_im, ssm_d, w_glu, b_glu, w_ssm_up, w_out, g_ffn2, w_ffn2_in, w_ffn2_out, g_final, loss_target, m_w_ada, m_b_ada, m_g_ffn1, m_w_ffn1_in, m_w_ffn1_out, m_g_mix, m_w_in, m_pool_w, m_pool_b, m_pool_scale, m_w_pool_up, m_ssm_lam_re_log, m_ssm_lam_im, m_ssm_log_dt, m_ssm_b_re, m_ssm_b_im, m_ssm_c_re, m_ssm_c_im, m_ssm_d, m_w_glu, m_b_glu, m_w_ssm_up, m_w_out, m_g_ffn2, m_w_ffn2_in, m_w_ffn2_out, m_g_final, v_w_ada, v_b_ada, v_g_ffn1, v_w_ffn1_in, v_w_ffn1_out, v_g_mix, v_w_in, v_pool_w, v_pool_b, v_pool_scale, v_w_pool_up, v_ssm_lam_re_log, v_ssm_lam_im, v_ssm_log_dt, v_ssm_b_re, v_ssm_b_im, v_ssm_c_re, v_ssm_c_im, v_ssm_d, v_w_glu, v_b_glu, v_w_ssm_up, v_w_out, v_g_ffn2, v_w_ffn2_in, v_w_ffn2_out, v_g_final):
    args = locals()
    W = {n: args[n] for n in WEIGHT_ORDER}
    M = {n: args["m_" + n] for n in WEIGHT_ORDER}
    V = {n: args["v_" + n] for n in WEIGHT_ORDER}
    shapes = {n: W[n].shape for n in WEIGHT_ORDER}
    xt, tgt = x[0], loss_target[0]

    def local(tree, n):
        return jnp.swapaxes(tree[n][0], 0, 1) if n in TRANSPOSED else tree[n][0]

    def as_output(n, a):
        return (jnp.swapaxes(a, 0, 1) if n in TRANSPOSED else a)[None]

    shard = dict(zip(GATHERED, _cast_shards([local(W, n) for n in GATHERED])))
    stacks = {}

    def gather(names):
        return _Gather([shard[n] for n in names])

    def gathered(names, results):
        stacks.update(zip(names, results))

    ffn1_w, ffn2_w = ('w_ffn1_in', 'w_ffn1_out'), ('w_ffn2_in', 'w_ffn2_out')
    mix_w = ('w_in', 'w_pool_up', 'w_glu', 'w_ssm_up', 'w_out')
    mod_cols, sc_all, *res = _ada_forward(c, W['w_ada'][0], b_ada.reshape(NDEV, -1), gather(ffn1_w[:1]))
    gathered(ffn1_w[:1], res)
    win1 = stacks['w_ffn1_in'].reshape(2, 4, FF_SHARD, D_MODEL)
    prm = jnp.concatenate([mod_cols.reshape(9, D_MODEL), g_ffn1, g_mix, g_ffn2, g_final[None], jnp.zeros((3, D_MODEL), F32)], axis=0)
    pad512 = jnp.zeros((1, D_MODEL - 512), F32)
    mvec = jnp.concatenate([jnp.concatenate([pool_b, pad512], axis=1), jnp.concatenate([pool_scale, pad512], axis=1),
                            jnp.concatenate([ssm_d, pad512], axis=1), b_glu, jnp.zeros((4, D_MODEL), F32)], axis=0)
    log_dt_col = ssm_log_dt[0][:, None]
    coeffs = _ssm_params_forward(ssm_lam_re_log[0], ssm_lam_im[0], log_dt_col)
    srow = jnp.stack([t.reshape(N_STATE) for t in coeffs], axis=0)
    b_dense = jnp.stack([_block_diag_in(ssm_b_re[0]), _block_diag_in(ssm_b_im[0])], axis=0)
    c_dense = jnp.stack([_block_diag_out(ssm_c_re[0]), _block_diag_out(ssm_c_im[0])], axis=0)
    bb, ct = _ssm_dense_forward(srow, b_dense, c_dense)
    pw = pool_w[0]

    next_w = ffn1_w[1:] + mix_w[:1]
    ab1, s1, *res = _ffn_hidden(xt, prm, win1, 0, ROW_G_FFN1, "ffn1_hidden", gather(next_w))
    gathered(next_w, res)
    wout1 = stacks['w_ffn1_out'].reshape(4, FF_SHARD, D_MODEL)
    x1, f1, *res = _ffn_out(xt, s1, prm, wout1, 0, "ffn1_out", gather(mix_w[1:]))
    gathered(mix_w[1:], res)
    w_out_full = stacks['w_out'].reshape(D_MODEL, D_MODEL)
    res = _mixer_forward(x1, prm, stacks['w_in'], stacks['w_pool_up'], stacks['w_glu'], stacks['w_ssm_up'],
                         w_out_full, pw, mvec, srow, bb, ct, gather(ffn2_w))
    x2, mo, saved = res[0], res[1], res[2:11]
    gathered(ffn2_w, res[11:])
    win2 = stacks['w_ffn2_in'].reshape(2, 4, FF_SHARD, D_MODEL)
    wout2 = stacks['w_ffn2_out'].reshape(4, FF_SHARD, D_MODEL)
    d3, fin, f3, ab3 = _ffn_forward_loss(x2, tgt, prm, win2, wout2, 2, ROW_G_FFN2, "ffn2_forward_loss")

    lands = {}

    def scatter(grads):
        names = list(grads)
        return _Scatter([grads[n][0] for n in names], [grads[n][1] for n in names], [local(W, n).shape for n in names])

    def scattered(grads, results):
        lands.update(zip(grads, results))

    parts3, dab3, dwout2 = _ffn_backward(d3, ab3, prm, win2, wout2, 2, "ffn2_backward")
    dwin2, = _ffn_dwin(x2, dab3, prm, 2, ROW_G_FFN2, "ffn2_dwin")
    d2, sums3 = _norm_backward(parts3, x2, d3, f3, prm, 2, ROW_G_FFN2, 0.5, "ffn2_norm_backward")
    g_ffn2_w = {'w_ffn2_in': (dwin2, _halves), 'w_ffn2_out': (dwout2.reshape(NDEV, -1, D_MODEL), _stacked)}
    res = _mixer_backward(d2, prm, saved, stacks['w_pool_up'], stacks['w_glu'], stacks['w_ssm_up'], w_out_full, pw, mvec,
                          srow, bb, ct, scatter(g_ffn2_w))
    dz, dwo, dwpu, dwglu, dwsu, dpw, dbb, dct, vsum, da = res[:10]
    scattered(g_ffn2_w, res[10:])
    g_mix_up = {'w_pool_up': (dwpu, _stacked), 'w_glu': (dwglu, _stacked), 'w_ssm_up': (dwsu, _stacked),
                'w_out': (dwo, _stacked)}
    d1, sums2, dwin_mix, *res = _mixer_in_backward(x1, dz, d2, mo, prm, stacks['w_in'], scatter(g_mix_up))
    scattered(g_mix_up, res)
    g_mix_w = {'w_in': (dwin_mix, _stacked)}
    db_dense, df_rows = _ssm_dense_backward(dbb, da, srow, b_dense)
    cot = [df_rows[r].reshape(N_SSM_GROUPS, SSM_STATE) for r in (2, 3, 0, 1)]
    d_lrl, d_li, d_ldt = _ssm_params_backward(ssm_lam_re_log[0], ssm_lam_im[0], log_dt_col, cot)
    small_grads = {
        'g_mix': sums2[0], 'g_ffn2': sums3[0], 'g_final': fin[0], 'pool_w': dpw,
        'pool_b': vsum[1, :512], 'pool_scale': vsum[0, :512], 'ssm_lam_re_log': d_lrl, 'ssm_lam_im': d_li,
        'ssm_log_dt': d_ldt, 'ssm_b_re': _diag_blocks(db_dense[0], SSM_GROUP, SSM_STATE),
        'ssm_b_im': _diag_blocks(db_dense[1], SSM_GROUP, SSM_STATE),
        'ssm_c_re': jnp.transpose(_diag_blocks(dct[0], SSM_STATE, SSM_GROUP), (0, 2, 1)),
        'ssm_c_im': jnp.transpose(_diag_blocks(dct[1], SSM_STATE, SSM_GROUP), (0, 2, 1)),
        'ssm_d': vsum[2, :512], 'b_glu': vsum[3],
    }
    early = _SmallAllReduce(_pack_small(None, small_grads, SMALL_EARLY, fin[1:2, 0:1]))
    parts1, dab1, dwout1, total_early, *res = _ffn_backward(d1, ab1, prm, win1, wout1, 0, "ffn1_backward",
                                                            _Carried(early, scatter(g_mix_w)))
    scattered(g_mix_w, res)
    loss = total_early[0, _pack_rows(SMALL_EARLY, False)[0], 0]
    g_wout1 = {'w_ffn1_out': (dwout1.reshape(NDEV, -1, D_MODEL), _stacked)}
    dwin1, *res = _ffn_dwin(xt, dab1, prm, 0, ROW_G_FFN1, "ffn1_dwin", scatter(g_wout1))
    scattered(g_wout1, res)

    last_w, last_views = ffn1_w[:1], [_halves]
    send_sems, recv_sems, last_src, last_land, token = _scatter_start(
        [dwin1], last_views, [local(W, n).shape for n in last_w], [total_early])
    after_start = token[0:1, 0:1]
    d0, sums1 = _norm_backward(parts1, xt, d1, f1, prm + after_start, 0, ROW_G_FFN1, 0.5, "ffn1_norm_backward")

    grad, delta, new_m, new_v = {}, {}, {}, {}

    def adam_sharded(n):
        res = _adam_sharded(local(W, n), local(M, n), local(V, n), lands[n], token, "adam_" + n)
        grad[n], delta[n], new_m[n], new_v[n] = [as_output(n, r) for r in res]
        return res[3]

    def adam_small(params, ada, total, name, order):
        rows = _pack_rows(params, ada)[1]
        views = [{n: jnp.transpose(t[n][0], (0, 2, 1)) if n in STATE_MINOR else t[n] for n, _ in params} for t in (W, M, V)]
        packs = [(_pack_small(t['b_ada'].reshape(-1) if ada else None, v, params) + order).reshape(NDEV * rows, 128)
                 for t, v in zip((W, M, V), views)]
        res = _adam_small(packs[0], total.reshape(NDEV * rows, 128), packs[1], packs[2], name)
        view_shapes = {n: (N_SSM_GROUPS, SSM_GROUP, SSM_STATE) if n in STATE_MINOR else shapes[n] for n, _ in params}
        for dst, packed in zip((grad, delta, new_m, new_v), (total, *res)):
            ada_vec, rest = _unpack_small(packed.reshape(NDEV, rows, 128), view_shapes, params, ada)
            dst.update({n: jnp.transpose(a, (0, 2, 1))[None] if n in STATE_MINOR else a for n, a in rest.items()})
            if ada:
                dst['b_ada'] = ada_vec.reshape(shapes['b_ada'])
        return res[2]

    done = [d0] + [adam_sharded(n) for n in GATHERED if n not in last_w]
    done.append(adam_small(SMALL_EARLY, False, total_early, "adam_small_early", after_start))
    lands.update(zip(last_w, _scatter_wait(send_sems, recv_sems, last_src, last_land, last_views, done)))

    dmod = jnp.concatenate([sums1[1:4], sums2[1:4], sums3[1:4]], axis=0).reshape(-1)
    total_late, landed = _allreduce_small(_pack_small(dmod, {'g_ffn1': sums1[0]}, SMALL_LATE), lands[last_w[0]],
                                          "allreduce_late")
    dmod_cols = landed[:, :ADA_ROWS].reshape(NDEV, ADA_ROWS * 128)
    res = _adam_ada(W['w_ada'][0], M['w_ada'][0], V['w_ada'][0], sc_all, dmod_cols)
    grad['w_ada'], delta['w_ada'], new_m['w_ada'], new_v['w_ada'] = [r[None] for r in res]
    adam_small(SMALL_LATE, True, total_late, "adam_small_late", 0.0)
    for n in last_w:
        adam_sharded(n)

    return (loss, d0[None], *[grad[n] for n in WEIGHT_ORDER], *[delta[n] for n in WEIGHT_ORDER],
            *[new_m[n] for n in WEIGHT_ORDER], *[new_v[n] for n in WEIGHT_ORDER])
```

```python
import jax
import jax.numpy as jnp
from jax import lax
from jax.experimental import pallas as pl
from jax.experimental.pallas import tpu as pltpu

F32 = jnp.float32
MXU_DTYPE = jnp.bfloat16
WIRE_DTYPE = jnp.bfloat16
SAVE_DTYPE = jnp.bfloat16

NDEV = 8
D_MODEL = 1024
D_FF = 2816
FF_SHARD = 2 * D_FF // NDEV
POOL_WIDTH = 512
POOL_GROUP = 128
SSM_WIDTH = 512
SSM_STATE = 64
SSM_GROUP = 16
N_SSM_GROUPS = SSM_WIDTH // SSM_GROUP
SSM_BLOCKS = 4
SSM_BLOCK_STATE = 512
N_STATE = 2048
IN_WIDTH = 3072
EPS = 1e-6
ADAM_LR = 0.001
ADAM_B1 = 0.9
ADAM_B2 = 0.999
ADAM_EPS = 1e-08
ADAM_WD = 0.01
ADAM_STEP = 10

TM_FFN = 512
TM_FFN_BWD = 1024
FFN_BWD_CHUNK = 256
TM_MIX = 256
TM_MIX_BWD = 256
TM_EW = 512
SCAN_ROWS = 8
POOL_HALO = 16
VMEM_LIMIT = 60 * 1024 * 1024

ROW_G_FFN1, ROW_G_MIX, ROW_G_FFN2, ROW_G_FINAL = 9, 10, 11, 12
ROW_POOL_B, ROW_POOL_SCALE, ROW_SSM_D, ROW_B_GLU = 0, 1, 2, 3

SMALL_EARLY = (
    ("g_mix", 1024), ("g_ffn2", 1024), ("g_final", 1024), ("pool_w", 65536),
    ("pool_b", 512), ("pool_scale", 512), ("ssm_lam_re_log", 2048), ("ssm_lam_im", 2048),
    ("ssm_log_dt", 32), ("ssm_b_re", 32768), ("ssm_b_im", 32768), ("ssm_c_re", 32768),
    ("ssm_c_im", 32768), ("ssm_d", 512), ("b_glu", 1024),
)
SMALL_LATE = (("g_ffn1", 1024),)
ADA_ROWS = 9
MESH = pl.DeviceIdType.MESH


def _pack_rows(params, with_ada):
    rest = -(-sum(n for _, n in params) // (NDEV * 128))
    return rest, -(-(rest + (ADA_ROWS if with_ada else 0)) // 8) * 8


def _mm(a, b):
    return jnp.dot(a.astype(MXU_DTYPE), b.astype(MXU_DTYPE), preferred_element_type=F32)


def _mm_nt(a, b):
    return lax.dot_general(a.astype(MXU_DTYPE), b.astype(MXU_DTYPE), (((1,), (1,)), ((), ())),
                           preferred_element_type=F32)


def _mm_tn(a, b):
    return lax.dot_general(a.astype(MXU_DTYPE), b.astype(MXU_DTYPE), (((0,), (0,)), ((), ())),
                           preferred_element_type=F32)


def _rms_scale(x):
    return lax.rsqrt(jnp.mean(x * x, axis=-1, keepdims=True) + EPS)


def _sigmoid(x):
    return jax.nn.sigmoid(x)


def _colsum(x):
    return jnp.sum(x, axis=0, keepdims=True)


def _row(ref, r):
    return ref[r:r + 1, :]


def _params(*sem):
    return pltpu.CompilerParams(dimension_semantics=sem, vmem_limit_bytes=VMEM_LIMIT)


def _resident(a):
    return pl.BlockSpec(a.shape, lambda *_: (0,) * a.ndim, pipeline_mode=pl.Buffered(1))


def _me():
    return lax.axis_index("x"), lax.axis_index("y"), lax.axis_index("c")


def _peer(rel):
    x, y, c = _me()
    px = 1 - x if rel & 4 else x
    py = 1 - y if rel & 2 else y
    pc = 1 - c if rel & 1 else c
    return (px, py, pc), 4 * px + 2 * py + pc


_HBM = pl.BlockSpec(memory_space=pl.ANY)
_HBM_ONLY = pl.BlockSpec(memory_space=pltpu.HBM)


def _stacked(ref, p):
    return ref.at[p]


def _halves(ref, p):
    return ref.at[p // 4, p % 4]


class _Gather:
    def __init__(self, shards):
        self.operands = list(shards)
        self.n = len(shards)
        self.out_shape = [jax.ShapeDtypeStruct((NDEV,) + s.shape, s.dtype) for s in shards]
        self.scratch = [pltpu.SemaphoreType.DMA((7 * self.n,)), pltpu.SemaphoreType.DMA((7 * self.n,)),
                        pltpu.SemaphoreType.DMA((self.n,))]

    def plan(self, srcs, outs, sems):
        send_sems, recv_sems, local_sems = sems
        n = self.n
        x, y, c = _me()
        me = 4 * x + 2 * y + c
        here, sibling = (x, y, c), (x, y, 1 - c)
        chips = [(1 - x, y), (x, 1 - y), (1 - x, 1 - y)]

        def blk(px, py, pc):
            return 4 * px + 2 * py + pc

        def copy(a, k, block, to, src=None):
            return pltpu.make_async_remote_copy(
                src_ref=outs[a].at[block] if src is None else src, dst_ref=outs[a].at[block],
                send_sem=send_sems.at[7 * a + k], recv_sem=recv_sems.at[7 * a + k], device_id=to, device_id_type=MESH)

        def mine(a):
            return pltpu.make_async_copy(srcs[a], outs[a].at[me], local_sems.at[a])

        def first(a):
            return [copy(a, 0, me, sibling, src=srcs[a])] + [copy(a, 1 + j, me, (*chip, c), src=srcs[a])
                                                              for j, chip in enumerate(chips)]

        def start():
            for a in range(n):
                mine(a).start()
                for cp in first(a):
                    cp.start()

        def forward():
            for a in range(n):
                for j, chip in enumerate(chips):
                    copy(a, 1 + j, blk(*chip, c), here).wait_recv()
                    copy(a, 4 + j, blk(*chip, c), sibling).start()

        def finish():
            for a in range(n):
                copy(a, 0, blk(x, y, 1 - c), here).wait_recv()
                for j, chip in enumerate(chips):
                    copy(a, 4 + j, blk(*chip, 1 - c), here).wait_recv()
            for a in range(n):
                mine(a).wait()
                for cp in first(a):
                    cp.wait_send()
                for j, chip in enumerate(chips):
                    copy(a, 4 + j, blk(*chip, c), sibling).wait_send()

        return start, forward, finish


class _Scatter:
    def __init__(self, arrays, views, shard_shapes):
        self.operands = list(arrays)
        self.views = list(views)
        self.n = len(arrays)
        self.out_shape = [jax.ShapeDtypeStruct((NDEV,) + tuple(s), a.dtype) for s, a in zip(shard_shapes, arrays)]
        self.scratch = [pltpu.SemaphoreType.DMA((7 * self.n,)), pltpu.SemaphoreType.DMA((7 * self.n,)),
                        pltpu.SemaphoreType.DMA((self.n,))]

    def plan(self, srcs, outs, sems):
        send_sems, recv_sems, local_sems = sems
        n, views = self.n, self.views
        x, y, c = _me()
        me = 4 * x + 2 * y + c

        def mine(a):
            return pltpu.make_async_copy(views[a](srcs[a], me), outs[a].at[me], local_sems.at[a])

        def copy(a, rel, sending):
            to, p = _peer(rel)
            return pltpu.make_async_remote_copy(
                src_ref=views[a](srcs[a], p), dst_ref=outs[a].at[me if sending else p],
                send_sem=send_sems.at[7 * a + rel - 1], recv_sem=recv_sems.at[7 * a + rel - 1],
                device_id=to if sending else (x, y, c), device_id_type=MESH)

        def start():
            for a in range(n):
                mine(a).start()
            for rel in range(1, 8):
                for a in range(n):
                    copy(a, rel, True).start()

        def forward():
            pass

        def finish():
            for rel in range(1, 8):
                for a in range(n):
                    copy(a, rel, False).wait_recv()
            for rel in range(1, 8):
                for a in range(n):
                    copy(a, rel, True).wait_send()
            for a in range(n):
                mine(a).wait()

        return start, forward, finish


class _SmallAllReduce:
    def __init__(self, pack):
        rows = pack.shape[1]
        self.operands = [pack]
        self.n = 1
        self.out_shape = [jax.ShapeDtypeStruct(pack.shape, F32)]
        self.scratch = [pltpu.VMEM(pack.shape, F32), pltpu.VMEM((rows, 128), F32)] \
            + [pltpu.SemaphoreType.DMA((7,))] * 4 + [pltpu.SemaphoreType.DMA((2,))]

    def plan(self, srcs, outs, scratch):
        pack, total = srcs[0], outs[0]
        land, mine, send1, recv1, send2, recv2, local = scratch
        x, y, c = _me()
        me = 4 * x + 2 * y + c

        def slab(rel, sending):
            to, p = _peer(rel)
            return pltpu.make_async_remote_copy(
                src_ref=pack.at[p], dst_ref=land.at[me if sending else p], send_sem=send1.at[rel - 1],
                recv_sem=recv1.at[rel - 1], device_id=to if sending else (x, y, c), device_id_type=MESH)

        def summed(rel, sending):
            to, p = _peer(rel)
            return pltpu.make_async_remote_copy(
                src_ref=mine, dst_ref=total.at[me if sending else p], send_sem=send2.at[rel - 1],
                recv_sem=recv2.at[rel - 1], device_id=to if sending else (x, y, c), device_id_type=MESH)

        own_slab = pltpu.make_async_copy(pack.at[me], land.at[me], local.at[0])
        own_sum = pltpu.make_async_copy(mine, total.at[me], local.at[1])

        def start():
            own_slab.start()
            for rel in range(1, 8):
                slab(rel, True).start()

        def forward():
            own_slab.wait()
            for rel in range(1, 8):
                slab(rel, False).wait_recv()
            acc = land[0]
            for b in range(1, NDEV):
                acc = acc + land[b]
            mine[...] = acc
            own_sum.start()
            for rel in range(1, 8):
                summed(rel, True).start()

        def finish():
            for rel in range(1, 8):
                summed(rel, False).wait_recv()
            for rel in range(1, 8):
                slab(rel, True).wait_send()
                summed(rel, True).wait_send()
            own_sum.wait()

        return start, forward, finish


class _Carried:
    def __init__(self, *parts):
        self.parts = parts
        self.operands = [o for p in parts for o in p.operands]
        self.n = len(self.operands)
        self.out_shape = [s for p in parts for s in p.out_shape]
        self.scratch = [s for p in parts for s in p.scratch]

    def plan(self, srcs, outs, scratch):
        plans, a, b = [], 0, 0
        for p in self.parts:
            plans.append(p.plan(srcs[a:a + p.n], outs[a:a + p.n], scratch[b:b + len(p.scratch)]))
            a, b = a + p.n, b + len(p.scratch)

        def every(k):
            def run():
                for plan in plans:
                    plan[k]()
            return run

        return every(0), every(1), every(2)


def _launch(body, name, out_shape, in_specs, out_specs, operands, scratch=(), grid=None, semantics=None,
            carry=None, steps=None):
    out_shape, in_specs, out_specs = list(out_shape), list(in_specs), list(out_specs)
    operands, scratch = list(operands), list(scratch)
    n_in, n_out, n_scr = len(in_specs), len(out_shape), len(scratch)
    kernel_body = body
    if carry is not None:
        k = carry.n

        def kernel_body(*refs):
            ins, cin = refs[:n_in], refs[n_in:n_in + k]
            outs, cout = refs[n_in + k:n_in + k + n_out], refs[n_in + k + n_out:n_in + 2 * k + n_out]
            rest = refs[n_in + 2 * k + n_out:]
            scr, csem = rest[:n_scr], rest[n_scr:]
            start, forward, finish = carry.plan(cin, cout, csem)
            if steps is None:
                start()
                body(*ins, *outs, *scr)
                forward()
                finish()
            else:
                pl.when(steps()[0])(start)
                pl.when(steps()[1])(forward)
                body(*ins, *outs, *scr)
                pl.when(steps()[2])(finish)

        in_specs += [_HBM] * k
        out_shape += carry.out_shape
        out_specs += [_HBM] * k
        operands += carry.operands
        scratch += carry.scratch
    kwargs = {} if grid is None else {"grid": grid}
    params = pltpu.CompilerParams(vmem_limit_bytes=VMEM_LIMIT) if semantics is None else _params(*semantics)
    return pl.pallas_call(kernel_body, name=name, out_shape=out_shape, in_specs=in_specs, out_specs=out_specs,
                          scratch_shapes=scratch, compiler_params=params, **kwargs)(*operands)


def _grid_steps(nt):
    def steps():
        i = pl.program_id(0)
        return i == 0, i == nt - 1, i == nt - 1
    return steps


def _cast_shards(shards):
    n = len(shards)

    def body(*refs):
        for a in range(n):
            refs[n + a][...] = refs[a][...].astype(WIRE_DTYPE)

    return pl.pallas_call(body, name="cast_shards",
                          out_shape=[jax.ShapeDtypeStruct(s.shape, WIRE_DTYPE) for s in shards],
                          compiler_params=pltpu.CompilerParams(vmem_limit_bytes=VMEM_LIMIT))(*shards)


_SEM = pl.BlockSpec(memory_space=pltpu.SEMAPHORE)
_DATAFLOW = pltpu.SideEffectType.DATAFLOW_SIDE_EFFECTING


def _split_copy(arrays, views, landing, send_sems, recv_sems, a, rel):
    to, p = _peer(rel)
    x, y, c = _me()
    return pltpu.make_async_remote_copy(
        src_ref=views[a](arrays[a], p), dst_ref=landing[a].at[4 * x + 2 * y + c],
        send_sem=send_sems.at[NDEV * a + rel], recv_sem=recv_sems.at[NDEV * a + rel], device_id=to, device_id_type=MESH)


def _scatter_start(arrays, views, shard_shapes, after):
    n = len(arrays)
    landing = [pltpu.with_memory_space_constraint(lax.empty((NDEV,) + tuple(s), a.dtype), pltpu.HBM)
               for s, a in zip(shard_shapes, arrays)]
    arrays = [pltpu.with_memory_space_constraint(a, pltpu.HBM) for a in arrays]

    def body(*refs):
        ins, land = refs[:n], refs[n:2 * n]
        send_sems, recv_sems = refs[2 * n + len(after)], refs[2 * n + len(after) + 1]
        token = refs[-1]
        for rel in range(NDEV):
            for a in range(n):
                _split_copy(ins, views, land, send_sems, recv_sems, a, rel).start()
        token[...] = jnp.zeros_like(token)

    res = pl.pallas_call(
        body, name="scatter_start",
        out_shape=[pltpu.SemaphoreType.DMA((NDEV * n,)), pltpu.SemaphoreType.DMA((NDEV * n,))]
        + [pltpu.HBM(a.shape, a.dtype) for a in arrays] + [pltpu.HBM(l.shape, l.dtype) for l in landing]
        + [jax.ShapeDtypeStruct((8, 128), F32)],
        in_specs=[_HBM_ONLY] * (2 * n) + [_HBM] * len(after),
        out_specs=[_SEM, _SEM] + [_HBM_ONLY] * (2 * n) + [pl.BlockSpec(memory_space=pltpu.VMEM)],
        input_output_aliases={i: 2 + i for i in range(2 * n)},
        compiler_params=pltpu.CompilerParams(has_side_effects=_DATAFLOW),
    )(*arrays, *landing, *after)
    return res[0], res[1], res[2:2 + n], res[2 + n:2 + 2 * n], res[-1]


def _scatter_wait(send_sems, recv_sems, arrays, landing, views, after):
    n = len(arrays)

    def body(*refs):
        ins, land = refs[:n], refs[n:2 * n]
        send, recv = refs[2 * n], refs[2 * n + 1]
        for rel in range(NDEV):
            for a in range(n):
                cp = _split_copy(ins, views, land, send, recv, a, rel)
                cp.wait_send()
                cp.wait_recv()

    res = pl.pallas_call(
        body, name="scatter_wait",
        out_shape=[pltpu.HBM(a.shape, a.dtype) for a in arrays] + [pltpu.HBM(l.shape, l.dtype) for l in landing],
        in_specs=[_HBM_ONLY] * (2 * n) + [_SEM, _SEM] + [_HBM] * len(after),
        out_specs=[_HBM_ONLY] * (2 * n),
        input_output_aliases={i: i for i in range(2 * n)},
        compiler_params=pltpu.CompilerParams(has_side_effects=_DATAFLOW),
    )(*arrays, *landing, send_sems, recv_sems, *after)
    return res[n:]


def _ada_forward(c_row, w_ada, b_ada8, carry):
    cols = w_ada.shape[1]

    def body(c_ref, w_ref, b_ref, mod_ref, sc_ref, c_all, send_buf, recv_buf, send1, recv1, send2, recv2):
        x, y, c = _me()
        me = 4 * x + 2 * y + c
        rowi = lax.broadcasted_iota(jnp.int32, (8, D_MODEL), 0)
        c_all[me] = jnp.broadcast_to(c_ref[...], (8, D_MODEL))
        copies = []
        for rel in range(1, 8):
            to, _ = _peer(rel)
            cp = pltpu.make_async_remote_copy(src_ref=c_all.at[me], dst_ref=c_all.at[me], send_sem=send1.at[rel - 1],
                                              recv_sem=recv1.at[rel - 1], device_id=to, device_id_type=MESH)
            cp.start()
            copies.append(cp)
        for rel in range(1, 8):
            _, p = _peer(rel)
            pltpu.make_async_remote_copy(src_ref=c_all.at[p], dst_ref=c_all.at[p], send_sem=send1.at[rel - 1],
                                         recv_sem=recv1.at[rel - 1], device_id=(x, y, c), device_id_type=MESH).wait_recv()
        for cp in copies:
            cp.wait_send()
        cmat = jnp.zeros((8, D_MODEL), F32)
        for b in range(8):
            cmat = jnp.where(rowi == b, c_all[b], cmat)
        sc = cmat * _sigmoid(cmat)
        sc_ref[...] = sc
        modcols = _mm(sc, w_ref[...]) + b_ref[pl.ds(me, 1), :]
        for b in range(8):
            send_buf[b] = jnp.broadcast_to(modcols[b:b + 1, :], (8, cols))
        recv_buf[me] = send_buf[me]
        copies = []
        for rel in range(1, 8):
            to, p = _peer(rel)
            cp = pltpu.make_async_remote_copy(src_ref=send_buf.at[p], dst_ref=recv_buf.at[me], send_sem=send2.at[rel - 1],
                                              recv_sem=recv2.at[rel - 1], device_id=to, device_id_type=MESH)
            cp.start()
            copies.append(cp)
        for rel in range(1, 8):
            _, p = _peer(rel)
            pltpu.make_async_remote_copy(src_ref=send_buf.at[p], dst_ref=recv_buf.at[p], send_sem=send2.at[rel - 1],
                                         recv_sem=recv2.at[rel - 1], device_id=(x, y, c), device_id_type=MESH).wait_recv()
        for cp in copies:
            cp.wait_send()
        rowc = lax.broadcasted_iota(jnp.int32, (8, cols), 0)
        out = jnp.zeros((8, cols), F32)
        for k in range(8):
            out = jnp.where(rowc == k, recv_buf[k], out)
        mod_ref[...] = out

    return _launch(
        body, "ada_forward",
        out_shape=[jax.ShapeDtypeStruct((8, cols), F32), jax.ShapeDtypeStruct((8, D_MODEL), F32)],
        in_specs=[pl.BlockSpec(memory_space=pltpu.VMEM)] * 3,
        out_specs=[pl.BlockSpec(memory_space=pltpu.VMEM)] * 2,
        operands=(c_row, w_ada, b_ada8),
        scratch=[pltpu.VMEM((8, 8, D_MODEL), F32), pltpu.VMEM((8, 8, cols), F32), pltpu.VMEM((8, 8, cols), F32)]
        + [pltpu.SemaphoreType.DMA((7,))] * 4,
        carry=carry)


def _allreduce_small(pack, order, name):
    rows = pack.shape[1]

    def body(pack_ref, order_ref, total_ref, land_ref, send1, recv1, send2, recv2):
        x, y, c = _me()
        me = 4 * x + 2 * y + c
        land_ref[me] = pack_ref[me]
        copies = []
        for rel in range(1, 8):
            to, p = _peer(rel)
            cp = pltpu.make_async_remote_copy(src_ref=pack_ref.at[p], dst_ref=land_ref.at[me], send_sem=send1.at[rel - 1],
                                              recv_sem=recv1.at[rel - 1], device_id=to, device_id_type=MESH)
            cp.start()
            copies.append(cp)
        for rel in range(1, 8):
            _, p = _peer(rel)
            pltpu.make_async_remote_copy(src_ref=pack_ref.at[p], dst_ref=land_ref.at[p], send_sem=send1.at[rel - 1],
                                         recv_sem=recv1.at[rel - 1], device_id=(x, y, c), device_id_type=MESH).wait_recv()
        for cp in copies:
            cp.wait_send()
        acc = land_ref[0]
        for b in range(1, 8):
            acc = acc + land_ref[b]
        total_ref[me] = acc
        copies = []
        for rel in range(1, 8):
            to, _ = _peer(rel)
            cp = pltpu.make_async_remote_copy(src_ref=total_ref.at[me], dst_ref=total_ref.at[me], send_sem=send2.at[rel - 1],
                                              recv_sem=recv2.at[rel - 1], device_id=to, device_id_type=MESH)
            cp.start()
            copies.append(cp)
        for rel in range(1, 8):
            _, p = _peer(rel)
            pltpu.make_async_remote_copy(src_ref=total_ref.at[p], dst_ref=total_ref.at[p], send_sem=send2.at[rel - 1],
                                         recv_sem=recv2.at[rel - 1], device_id=(x, y, c), device_id_type=MESH).wait_recv()
        for cp in copies:
            cp.wait_send()

    return pl.pallas_call(
        body, name=name,
        out_shape=[jax.ShapeDtypeStruct((8, rows, 128), F32), jax.ShapeDtypeStruct((8, rows, 128), F32)],
        in_specs=[pl.BlockSpec(memory_space=pltpu.VMEM), _HBM],
        out_specs=[pl.BlockSpec(memory_space=pltpu.VMEM)] * 2,
        scratch_shapes=[pltpu.SemaphoreType.DMA((7,))] * 4,
        compiler_params=pltpu.CompilerParams(vmem_limit_bytes=VMEM_LIMIT),
    )(pack, order)


def _modulated(x, prm_ref, sub, g_row):
    shift, scale = _row(prm_ref, 3 * sub), _row(prm_ref, 3 * sub + 1)
    g = _row(prm_ref, g_row)
    r = _rms_scale(x)
    n0 = x * r
    return (n0 * g) * (1.0 + scale) + shift, r, n0


def _swiglu_tile(xv, prm_ref, win_ref, wout_ref, ab_ref, sub, g_row):
    h, _, _ = _modulated(xv, prm_ref, sub, g_row)
    hb = h.astype(MXU_DTYPE)
    acc = None
    for j in range(4):
        a = _mm_nt(hb, win_ref[0, j])
        b = _mm_nt(hb, win_ref[1, j])
        ab_ref[0, j] = a.astype(SAVE_DTYPE)
        ab_ref[1, j] = b.astype(SAVE_DTYPE)
        t = _mm((a * _sigmoid(a)) * b, wout_ref[j])
        acc = t if acc is None else acc + t
    return acc


def _loss_tile(xv, target, g):
    r = _rms_scale(xv)
    n0 = xv * r
    err = n0 * g - target
    dy = err / float(D_MODEL)
    dn0 = dy * g
    dx = r * (dn0 - n0 * jnp.mean(dn0 * n0, axis=-1, keepdims=True))
    loss = 0.5 * jnp.sum(jnp.mean(err * err, axis=-1, keepdims=True), axis=0, keepdims=True)
    return dx, _colsum(dy * n0), loss


def _ffn_forward_loss(x, target, prm, win, wout, sub, g_row, name):
    T = x.shape[0]
    tm = min(T, TM_FFN)

    def body(x_ref, t_ref, prm_ref, win_ref, wout_ref, dx_ref, sums_ref, f_ref, ab_ref):
        i = pl.program_id(0)
        xv = x_ref[...]
        acc = _swiglu_tile(xv, prm_ref, win_ref, wout_ref, ab_ref, sub, g_row)
        f_ref[...] = acc.astype(SAVE_DTYPE)
        dx, dg, loss = _loss_tile(xv + (0.5 * _row(prm_ref, 3 * sub + 2)) * acc, t_ref[...], _row(prm_ref, ROW_G_FINAL))
        dx_ref[...] = dx
        upd = jnp.concatenate([dg, jnp.broadcast_to(loss, (1, D_MODEL)), jnp.zeros((6, D_MODEL), F32)], axis=0)

        @pl.when(i == 0)
        def _():
            sums_ref[...] = upd

        @pl.when(i > 0)
        def _():
            sums_ref[...] += upd

    tok = pl.BlockSpec((tm, D_MODEL), lambda i: (i, 0))
    return _launch(
        body, name, grid=(T // tm,), semantics=("arbitrary",),
        out_shape=[jax.ShapeDtypeStruct((T, D_MODEL), F32), jax.ShapeDtypeStruct((8, D_MODEL), F32),
                   jax.ShapeDtypeStruct((T, D_MODEL), SAVE_DTYPE), jax.ShapeDtypeStruct((2, 4, T, FF_SHARD), SAVE_DTYPE)],
        in_specs=[tok, tok, _resident(prm), _resident(win), _resident(wout)],
        out_specs=[tok, pl.BlockSpec((8, D_MODEL), lambda i: (0, 0)), tok,
                   pl.BlockSpec((2, 4, tm, FF_SHARD), lambda i: (0, 0, i, 0))],
        operands=(x, target, prm, win, wout))


def _ffn_hidden(x, prm, win, sub, g_row, name, carry=None):
    T = x.shape[0]
    tm = min(T, TM_FFN)

    def body(x_ref, prm_ref, win_ref, ab_ref, s_ref):
        h, _, _ = _modulated(x_ref[...], prm_ref, sub, g_row)
        hb = h.astype(MXU_DTYPE)
        for j in range(4):
            a = _mm_nt(hb, win_ref[0, j])
            b = _mm_nt(hb, win_ref[1, j])
            ab_ref[0, j] = a.astype(SAVE_DTYPE)
            ab_ref[1, j] = b.astype(SAVE_DTYPE)
            s_ref[j] = ((a * _sigmoid(a)) * b).astype(MXU_DTYPE)

    return _launch(
        body, name, grid=(T // tm,), semantics=("arbitrary",),
        out_shape=[jax.ShapeDtypeStruct((2, 4, T, FF_SHARD), SAVE_DTYPE), jax.ShapeDtypeStruct((4, T, FF_SHARD), MXU_DTYPE)],
        in_specs=[pl.BlockSpec((tm, D_MODEL), lambda i: (i, 0)), _resident(prm), _resident(win)],
        out_specs=[pl.BlockSpec((2, 4, tm, FF_SHARD), lambda i: (0, 0, i, 0)),
                   pl.BlockSpec((4, tm, FF_SHARD), lambda i: (0, i, 0))],
        operands=(x, prm, win), carry=carry, steps=_grid_steps(T // tm))


def _ffn_out(x, s, prm, wout, sub, name, carry=None):
    T = x.shape[0]
    tm = min(T, TM_FFN)

    def body(x_ref, s_ref, prm_ref, wout_ref, xo_ref, f_ref):
        acc = None
        for j in range(4):
            t = _mm(s_ref[j], wout_ref[j])
            acc = t if acc is None else acc + t
        f_ref[...] = acc.astype(SAVE_DTYPE)
        xo_ref[...] = x_ref[...] + (0.5 * _row(prm_ref, 3 * sub + 2)) * acc

    tok = pl.BlockSpec((tm, D_MODEL), lambda i: (i, 0))
    return _launch(
        body, name, grid=(T // tm,), semantics=("arbitrary",),
        out_shape=[jax.ShapeDtypeStruct((T, D_MODEL), F32), jax.ShapeDtypeStruct((T, D_MODEL), SAVE_DTYPE)],
        in_specs=[tok, pl.BlockSpec((4, tm, FF_SHARD), lambda i: (0, i, 0)), _resident(prm), _resident(wout)],
        out_specs=[tok, tok], operands=(x, s, prm, wout), carry=carry, steps=_grid_steps(T // tm))


def _ffn_backward(d, ab, prm, win, wout, sub, name, carry=None):
    T = d.shape[0]
    tm = min(T, TM_FFN_BWD)
    nt = T // tm
    chunk = min(tm, FFN_BWD_CHUNK)

    def body(d_ref, ab_ref, prm_ref, win_ref, wout_ref, dh_ref, dab_ref, dwout_ref, acc_out):
        i = pl.program_id(1)

        @pl.when(i == 0)
        def _():
            acc_out[...] = jnp.zeros_like(acc_out)

        wa, wb, wo = win_ref[0, 0], win_ref[1, 0], wout_ref[0]
        half_gate = 0.5 * _row(prm_ref, 3 * sub + 2)
        ss, dfss = [], []
        for ck in range(tm // chunk):
            rows = slice(ck * chunk, (ck + 1) * chunk)
            a = ab_ref[0, 0, rows, :].astype(F32)
            b = ab_ref[1, 0, rows, :].astype(F32)
            sg = _sigmoid(a)
            si = a * sg
            dfs = (half_gate * d_ref[rows, :]).astype(MXU_DTYPE)
            ds = _mm_nt(dfs, wo)
            da = (ds * b * (sg * (1.0 + a * (1.0 - sg)))).astype(MXU_DTYPE)
            db = (ds * si).astype(MXU_DTYPE)
            dh_ref[0, rows, :] = (_mm(da, wa) + _mm(db, wb)).astype(SAVE_DTYPE)
            dab_ref[0, 0, rows, :] = da
            dab_ref[1, 0, rows, :] = db
            ss.append((si * b).astype(MXU_DTYPE))
            dfss.append(dfs)
        cat = (lambda v: v[0]) if len(ss) == 1 else (lambda v: jnp.concatenate(v, axis=0))
        acc_out[...] += _mm_tn(cat(ss), cat(dfss))

        @pl.when(i == nt - 1)
        def _():
            dwout_ref[0] = acc_out[...].astype(WIRE_DTYPE)

    def steps():
        j, i = pl.program_id(0), pl.program_id(1)
        return (j == 0) & (i == 0), (j == 2) & (i == 0), (j == 3) & (i == nt - 1)

    pre = pl.BlockSpec((2, 1, tm, FF_SHARD), lambda j, i: (0, j, i, 0))
    return _launch(
        body, name, grid=(4, nt), semantics=("arbitrary", "arbitrary"),
        out_shape=[jax.ShapeDtypeStruct((4, T, D_MODEL), SAVE_DTYPE), jax.ShapeDtypeStruct(ab.shape, MXU_DTYPE),
                   jax.ShapeDtypeStruct(wout.shape, WIRE_DTYPE)],
        in_specs=[pl.BlockSpec((tm, D_MODEL), lambda j, i: (i, 0)), pre, _resident(prm),
                  pl.BlockSpec((2, 1, FF_SHARD, D_MODEL), lambda j, i: (0, j, 0, 0)),
                  pl.BlockSpec((1, FF_SHARD, D_MODEL), lambda j, i: (j, 0, 0))],
        out_specs=[pl.BlockSpec((1, tm, D_MODEL), lambda j, i: (j, i, 0)), pre,
                   pl.BlockSpec((1, FF_SHARD, D_MODEL), lambda j, i: (j, 0, 0))],
        operands=(d, ab, prm, win, wout), scratch=[pltpu.VMEM((FF_SHARD, D_MODEL), F32)], carry=carry, steps=steps)


def _ffn_dwin(x, dab, prm, sub, g_row, name, carry=None):
    T = x.shape[0]
    tm = min(T, TM_FFN_BWD)
    nt = T // tm

    def body(x_ref, dab_ref, prm_ref, dwin_ref, acc):
        i = pl.program_id(1)

        @pl.when(i == 0)
        def _():
            acc[...] = jnp.zeros_like(acc)

        h, _, _ = _modulated(x_ref[...], prm_ref, sub, g_row)
        hb = h.astype(MXU_DTYPE)
        acc[0] += _mm_tn(dab_ref[0, 0], hb)
        acc[1] += _mm_tn(dab_ref[1, 0], hb)

        @pl.when(i == nt - 1)
        def _():
            dwin_ref[0, 0] = acc[0].astype(WIRE_DTYPE)
            dwin_ref[1, 0] = acc[1].astype(WIRE_DTYPE)

    def steps():
        j, i = pl.program_id(0), pl.program_id(1)
        return (j == 0) & (i == 0), (j == 2) & (i == 0), (j == 3) & (i == nt - 1)

    return _launch(
        body, name, grid=(4, nt), semantics=("arbitrary", "arbitrary"),
        out_shape=[jax.ShapeDtypeStruct((2, 4, FF_SHARD, D_MODEL), WIRE_DTYPE)],
        in_specs=[pl.BlockSpec((tm, D_MODEL), lambda j, i: (i, 0)),
                  pl.BlockSpec((2, 1, tm, FF_SHARD), lambda j, i: (0, j, i, 0)), _resident(prm)],
        out_specs=[pl.BlockSpec((2, 1, FF_SHARD, D_MODEL), lambda j, i: (0, j, 0, 0))],
        operands=(x, dab, prm), scratch=[pltpu.VMEM((2, FF_SHARD, D_MODEL), F32)], carry=carry, steps=steps)


def _norm_backward_tile(dh, xv, dv, fv, prm_ref, sub, g_row, gate_coef):
    scale, g = _row(prm_ref, 3 * sub + 1), _row(prm_ref, g_row)
    r = _rms_scale(xv)
    n0 = xv * r
    dn = dh * (1.0 + scale)
    dn0 = dn * g
    dx = dv + r * (dn0 - n0 * jnp.mean(dn0 * n0, axis=-1, keepdims=True))
    upd = jnp.concatenate([_colsum(dn * n0), _colsum(dh), _colsum(dh * (n0 * g)),
                           gate_coef * _colsum(dv * fv.astype(F32)), jnp.zeros((4, D_MODEL), F32)], axis=0)
    return dx, upd


def _norm_backward(parts, x, d, f, prm, sub, g_row, gate_coef, name):
    T = x.shape[0]
    tm = min(T, TM_EW)
    P = parts.shape[0]

    def body(p_ref, x_ref, d_ref, f_ref, prm_ref, dx_ref, sums_ref):
        i = pl.program_id(0)
        dh = p_ref[0].astype(F32)
        for k in range(1, P):
            dh = dh + p_ref[k].astype(F32)
        dx_ref[...], upd = _norm_backward_tile(dh, x_ref[...], d_ref[...], f_ref[...], prm_ref, sub, g_row, gate_coef)

        @pl.when(i == 0)
        def _():
            sums_ref[...] = upd

        @pl.when(i > 0)
        def _():
            sums_ref[...] += upd

    tok = pl.BlockSpec((tm, D_MODEL), lambda i: (i, 0))
    return _launch(
        body, name, grid=(T // tm,), semantics=("arbitrary",),
        out_shape=[jax.ShapeDtypeStruct((T, D_MODEL), F32), jax.ShapeDtypeStruct((8, D_MODEL), F32)],
        in_specs=[pl.BlockSpec((P, tm, D_MODEL), lambda i: (0, i, 0)), tok, tok, tok, _resident(prm)],
        out_specs=[tok, pl.BlockSpec((8, D_MODEL), lambda i: (0, 0))],
        operands=(parts, x, d, f, prm))


def _ssm_discretise(lam_re_log, lam_im, log_dt):
    lr = -jnp.exp(lam_re_log)
    dt = jnp.exp(log_dt)
    mag = jnp.exp(lr * dt)
    ang = lam_im * dt
    ab_re = mag * jnp.cos(ang)
    ab_im = mag * jnp.sin(ang)
    num_re = ab_re - 1.0
    num_im = ab_im
    den = lr * lr + lam_im * lam_im
    f_re = (num_re * lr + num_im * lam_im) / den
    f_im = (num_im * lr - num_re * lam_im) / den
    return ab_re, ab_im, f_re, f_im


def _ssm_params_forward(lam_re_log, lam_im, log_dt):
    def body(a_ref, b_ref, c_ref, o0, o1, o2, o3):
        outs = _ssm_discretise(a_ref[...], b_ref[...], c_ref[...])
        for o, v in zip((o0, o1, o2, o3), outs):
            o[...] = v

    return pl.pallas_call(body, name="ssm_params_forward",
                          out_shape=[jax.ShapeDtypeStruct(lam_im.shape, F32)] * 4)(lam_re_log, lam_im, log_dt)


def _ssm_params_backward(lam_re_log, lam_im, log_dt, cot):
    def body(a_ref, b_ref, c_ref, g0, g1, g2, g3, o0, o1, o2):
        _, vjp = jax.vjp(_ssm_discretise, a_ref[...], b_ref[...], c_ref[...])
        d0, d1, d2 = vjp((g0[...], g1[...], g2[...], g3[...]))
        o0[...] = d0
        o1[...] = d1
        o2[...] = d2

    return pl.pallas_call(
        body, name="ssm_params_backward",
        out_shape=[jax.ShapeDtypeStruct(lam_im.shape, F32), jax.ShapeDtypeStruct(lam_im.shape, F32),
                   jax.ShapeDtypeStruct(log_dt.shape, F32)])(lam_re_log, lam_im, log_dt, *cot)


def _ssm_dense_forward(srow, b_dense, c_dense):
    def body(srow_ref, bd_ref, cd_ref, bb_ref, ct_ref):
        for j in range(SSM_BLOCKS):
            lanes = slice(j * SSM_BLOCK_STATE, (j + 1) * SSM_BLOCK_STATE)
            f_re, f_im = srow_ref[2:3, lanes], srow_ref[3:4, lanes]
            bb_ref[0, j] = (f_re * bd_ref[0, j] - f_im * bd_ref[1, j]).astype(MXU_DTYPE)
            bb_ref[1, j] = (f_re * bd_ref[1, j] + f_im * bd_ref[0, j]).astype(MXU_DTYPE)
            ct_ref[0, j] = cd_ref[0, j].astype(MXU_DTYPE)
            ct_ref[1, j] = cd_ref[1, j].astype(MXU_DTYPE)

    return pl.pallas_call(body, name="ssm_dense_forward",
                          out_shape=[jax.ShapeDtypeStruct(b_dense.shape, MXU_DTYPE),
                                     jax.ShapeDtypeStruct(c_dense.shape, MXU_DTYPE)],
                          compiler_params=pltpu.CompilerParams(vmem_limit_bytes=VMEM_LIMIT))(srow, b_dense, c_dense)


def _cmul(p, q):
    return p[0] * q[0] - p[1] * q[1], p[0] * q[1] + p[1] * q[0]


def _scan_coefficients(ar, ai, reverse):
    n = ar.shape[1]
    p = {1: (ar, ai)}
    p[2] = _cmul(p[1], p[1])
    p[3] = _cmul(p[2], p[1])
    p[4] = _cmul(p[2], p[2])
    p[5] = _cmul(p[4], p[1])
    p[6] = _cmul(p[4], p[2])
    p[7] = _cmul(p[4], p[3])
    p[8] = _cmul(p[4], p[4])
    rowi = lax.broadcasted_iota(jnp.int32, (SCAN_ROWS, n), 0)
    tiles = []
    for dstep in (1, 2, 4):
        keep = (rowi < SCAN_ROWS - dstep) if reverse else (rowi >= dstep)
        for part in p[dstep]:
            tiles.append(jnp.where(keep, jnp.broadcast_to(part, (SCAN_ROWS, n)), 0.0))
    for comp in (0, 1):
        t = jnp.zeros((SCAN_ROWS, n), F32)
        for rr in range(SCAN_ROWS):
            power = SCAN_ROWS - rr if reverse else rr + 1
            t = jnp.where(rowi == rr, jnp.broadcast_to(p[power][comp], (SCAN_ROWS, n)), t)
        tiles.append(t)
    return tiles


def _load_stack(stack_hbm, dst, sems, base):
    cols = stack_hbm.shape[2]
    cps = [pltpu.make_async_copy(stack_hbm.at[k], dst.at[:, pl.ds(k * cols, cols)], sems.at[base + k])
           for k in range(NDEV)]
    for cp in cps:
        cp.start()
    return cps


def _window_lanes():
    lane = lax.broadcasted_iota(jnp.int32, (1, POOL_WIDTH), 1)
    return jnp.where(lane < 128, 2.0, jnp.where(lane < 256, 4.0, jnp.where(lane < 384, 8.0, 16.0)))


def _gelu(y):
    return 0.5 * y * (1.0 + lax.erf(y * 0.7071067811865476))


def _gelu_grad(y):
    return 0.5 * (1.0 + lax.erf(y * 0.7071067811865476)) + y * jnp.exp(-0.5 * y * y) * 0.3989422804014327


def _mixer_forward(x, prm, w_in_s, w_pu_s, w_glu_s, w_su_s, w_out, pool_w, mvec, srow, bb, ct, carry=None):
    T = x.shape[0]
    tm = min(T, TM_MIX)
    nt = T // tm
    n_tiles = tm // SCAN_ROWS

    def body(x_ref, prm_ref, w_in_h, w_pu_h, w_glu_h, w_su_h, w_out_h, pw_ref, mv_ref, srow_ref, bb, ct,
             x2_ref, mo_ref, z_ref, sre_ref, sim_ref, zp_ref, q_ref, yp_ref, yss_ref, vg_ref, ys_ref,
             w_in, w_pu, w_glu, w_su, w_o, coef, carry, hist, bu, sems):
        i = pl.program_id(0)

        @pl.when(i == 0)
        def _():
            cps = (_load_stack(w_in_h, w_in, sems, 0) + _load_stack(w_pu_h, w_pu, sems, 8)
                   + _load_stack(w_glu_h, w_glu, sems, 16) + _load_stack(w_su_h, w_su, sems, 24))
            cps.append(pltpu.make_async_copy(w_out_h, w_o, sems.at[32]))
            cps[-1].start()
            for j in range(SSM_BLOCKS):
                lanes = slice(j * SSM_BLOCK_STATE, (j + 1) * SSM_BLOCK_STATE)
                for k, t in enumerate(_scan_coefficients(srow_ref[0:1, lanes], srow_ref[1:2, lanes], False)):
                    coef[j, k] = t
            carry[...] = jnp.zeros_like(carry)
            hist[...] = jnp.zeros_like(hist)
            for cp in cps:
                cp.wait()

        xv = x_ref[...]
        h, _, _ = _modulated(xv, prm_ref, 1, ROW_G_MIX)
        z = _mm(h, w_in[...])
        z_ref[...] = z.astype(SAVE_DTYPE)
        u_pool, u_ssm = z[:, 0:512], z[:, 512:1024]
        gl_pool, gl_ssm = z[:, 1024:2048], z[:, 2048:3072]

        ext = jnp.concatenate([hist[...], u_pool], axis=0)
        w2 = ext + pltpu.roll(ext, 1, 0)
        w4 = w2[:, 128:] + pltpu.roll(w2[:, 128:], 2, 0)
        w8 = w4[:, 128:] + pltpu.roll(w4[:, 128:], 4, 0)
        w16 = w8[:, 128:] + pltpu.roll(w8[:, 128:], 8, 0)
        wsum = jnp.concatenate([w2[POOL_HALO:, :128], w4[POOL_HALO:, :128], w8[POOL_HALO:, :128], w16[POOL_HALO:]], axis=1)
        hist[...] = u_pool[tm - POOL_HALO:, :]
        t1 = (lax.broadcasted_iota(jnp.int32, (tm, 1), 0) + (i * tm + 1)).astype(F32)
        zp = wsum / jnp.minimum(t1, _window_lanes()) - u_pool
        zp_ref[...] = zp.astype(SAVE_DTYPE)
        q = jnp.concatenate([_mm(zp[:, k * 128:(k + 1) * 128], pw_ref[k]) for k in range(4)], axis=1)
        q = q + mv_ref[ROW_POOL_B:ROW_POOL_B + 1, 0:512]
        q_ref[...] = q.astype(SAVE_DTYPE)
        y_pool = _mm(q * mv_ref[ROW_POOL_SCALE:ROW_POOL_SCALE + 1, 0:512], w_pu[...])
        yp_ref[...] = y_pool.astype(SAVE_DTYPE)

        y_blocks = []
        for j in range(SSM_BLOCKS):
            lanes = pl.ds(j * SSM_BLOCK_STATE, SSM_BLOCK_STATE)
            ub = u_ssm[:, j * 128:(j + 1) * 128].astype(MXU_DTYPE)
            bu[0] = _mm(ub, bb[0, j])
            bu[1] = _mm(ub, bb[1, j])
            a1r, a1i, a2r, a2i, a4r, a4i, pr, pi = [coef[j, k] for k in range(8)]

            def step(tt, c, lanes=lanes, a1r=a1r, a1i=a1i, a2r=a2r, a2i=a2i, a4r=a4r, a4i=a4i, pr=pr, pi=pi):
                cr, ci = c
                rows = pl.ds(pl.multiple_of(tt * SCAN_ROWS, SCAN_ROWS), SCAN_ROWS)
                xr, xi = bu[0, rows, :], bu[1, rows, :]
                for dstep, kr, ki in ((1, a1r, a1i), (2, a2r, a2i), (4, a4r, a4i)):
                    sr, si = pltpu.roll(xr, dstep, 0), pltpu.roll(xi, dstep, 0)
                    xr, xi = xr + kr * sr - ki * si, xi + kr * si + ki * sr
                xr, xi = xr + pr * cr - pi * ci, xi + pr * ci + pi * cr
                sre_ref[rows, lanes] = xr
                sim_ref[rows, lanes] = xi
                return (jnp.broadcast_to(xr[SCAN_ROWS - 1:SCAN_ROWS, :], xr.shape),
                        jnp.broadcast_to(xi[SCAN_ROWS - 1:SCAN_ROWS, :], xi.shape))

            cr, ci = lax.fori_loop(0, n_tiles, step, (carry[j, 0], carry[j, 1]))
            carry[j, 0] = cr
            carry[j, 1] = ci
            y_blocks.append(_mm(sre_ref[:, lanes], ct[0, j]) - _mm(sim_ref[:, lanes], ct[1, j]))
        yss = jnp.concatenate(y_blocks, axis=1) + mv_ref[ROW_SSM_D:ROW_SSM_D + 1, 0:512] * u_ssm
        yss_ref[...] = yss.astype(SAVE_DTYPE)
        vg = _mm(_gelu(yss), w_glu[...]) + mv_ref[ROW_B_GLU:ROW_B_GLU + 1, :]
        vg_ref[...] = vg.astype(SAVE_DTYPE)
        y_ssm = _mm(vg[:, 0:512] * _sigmoid(vg[:, 512:1024]), w_su[...])
        ys_ref[...] = y_ssm.astype(SAVE_DTYPE)

        merged = _sigmoid(gl_pool) * y_pool + _sigmoid(gl_ssm) * y_ssm
        mo = _mm(merged, w_o[...])
        mo_ref[...] = mo.astype(SAVE_DTYPE)
        x2_ref[...] = xv + _row(prm_ref, 5) * mo

    def tok(width):
        return pl.BlockSpec((tm, width), lambda i: (i, 0))

    hbm = _HBM
    widths = (D_MODEL, D_MODEL, IN_WIDTH, N_STATE, N_STATE, 512, 512, D_MODEL, 512, D_MODEL, D_MODEL)
    dtypes = (F32, SAVE_DTYPE, SAVE_DTYPE, F32, F32) + (SAVE_DTYPE,) * 6
    return _launch(
        body, "mixer_forward", grid=(nt,), semantics=("arbitrary",), carry=carry, steps=_grid_steps(nt),
        out_shape=[jax.ShapeDtypeStruct((T, w), dt) for w, dt in zip(widths, dtypes)],
        in_specs=[tok(D_MODEL), _resident(prm), hbm, hbm, hbm, hbm, hbm, _resident(pool_w), _resident(mvec),
                  _resident(srow), _resident(bb), _resident(ct)],
        out_specs=[tok(w) for w in widths],
        operands=(x, prm, w_in_s, w_pu_s, w_glu_s, w_su_s, w_out, pool_w, mvec, srow, bb, ct),
        scratch=[
            pltpu.VMEM((D_MODEL, IN_WIDTH), MXU_DTYPE), pltpu.VMEM((512, D_MODEL), MXU_DTYPE),
            pltpu.VMEM((512, D_MODEL), MXU_DTYPE), pltpu.VMEM((512, D_MODEL), MXU_DTYPE),
            pltpu.VMEM((D_MODEL, D_MODEL), MXU_DTYPE),
            pltpu.VMEM((SSM_BLOCKS, 8, SCAN_ROWS, SSM_BLOCK_STATE), F32),
            pltpu.VMEM((SSM_BLOCKS, 2, SCAN_ROWS, SSM_BLOCK_STATE), F32),
            pltpu.VMEM((POOL_HALO, POOL_WIDTH), F32),
            pltpu.VMEM((2, tm, SSM_BLOCK_STATE), F32),
            pltpu.SemaphoreType.DMA((33,)),
        ])


def _mixer_backward(d2, prm, saved, w_pu_s, w_glu_s, w_su_s, w_out, pool_w, mvec, srow, bb, ct, carry=None):
    z, s_re, s_im, zp, q, y_pool, yss, vg, y_ssm = saved
    T = d2.shape[0]
    tm = min(T, TM_MIX_BWD)
    nt = T // tm
    n_tiles = tm // SCAN_ROWS

    def body(d_ref, prm_ref, z_ref, sre_ref, sim_ref, zp_ref, q_ref, yp_ref, yss_ref, vg_ref, ys_ref,
             w_pu_h, w_glu_h, w_su_h, w_out_h, pw_ref, mv_ref, srow_ref, bb, ct,
             dz_ref, dwo_h, dwpu_h, dwglu_h, dwsu_h, dpw_h, dbb_h, dct_h, vsum_h, da_h,
             w_pu, w_glu, w_su, w_o, pwb, coef, carry, hist, dre, lam,
             a_wo, a_wpu, a_wglu, a_wsu, a_pw, a_bb, a_ct, a_vs, a_da, st_wo, st_up, sems):
        i = pl.program_id(0)
        tile = nt - 1 - i

        @pl.when(i == 0)
        def _():
            cps = (_load_stack(w_pu_h, w_pu, sems, 0) + _load_stack(w_glu_h, w_glu, sems, 8)
                   + _load_stack(w_su_h, w_su, sems, 16))
            cps.append(pltpu.make_async_copy(w_out_h, w_o, sems.at[24]))
            cps[-1].start()
            pwb[...] = pw_ref[...].astype(MXU_DTYPE)
            for j in range(SSM_BLOCKS):
                lanes = slice(j * SSM_BLOCK_STATE, (j + 1) * SSM_BLOCK_STATE)
                for k, t in enumerate(_scan_coefficients(srow_ref[0:1, lanes], srow_ref[1:2, lanes], True)):
                    coef[j, k] = t
            for acc in (carry, hist, a_wo, a_wpu, a_wglu, a_wsu, a_pw, a_bb, a_ct, a_vs, a_da):
                acc[...] = jnp.zeros_like(acc)
            for cp in cps:
                cp.wait()

        dv = d_ref[...]
        zt = z_ref[...].astype(F32)
        u_ssm, gl_pool, gl_ssm = zt[:, 512:1024], zt[:, 1024:2048], zt[:, 2048:3072]
        y_p, y_s = yp_ref[...].astype(F32), ys_ref[...].astype(F32)
        sgp, sgs = _sigmoid(gl_pool), _sigmoid(gl_ssm)
        dmo = (_row(prm_ref, 5) * dv).astype(MXU_DTYPE)
        a_wo[...] += _mm_tn(sgp * y_p + sgs * y_s, dmo)
        dmerged = _mm_nt(dmo, w_o[...])
        dy_pool = dmerged * sgp
        dgl_pool = dmerged * y_p * (sgp * (1.0 - sgp))
        dy_ssm = dmerged * sgs
        dgl_ssm = dmerged * y_s * (sgs * (1.0 - sgs))

        scale = mv_ref[ROW_POOL_SCALE:ROW_POOL_SCALE + 1, 0:512]
        qv, zpv = q_ref[...].astype(F32), zp_ref[...]
        a_wpu[...] += _mm_tn(qv * scale, dy_pool)
        dp = _mm_nt(dy_pool, w_pu[...])
        dq = dp * scale
        a_vs[0:1, 0:512] += _colsum(dp * qv)
        a_vs[1:2, 0:512] += _colsum(dq)
        dzp_blocks = []
        for k in range(4):
            lanes = slice(k * 128, (k + 1) * 128)
            dzp_blocks.append(_mm_nt(dq[:, lanes], pwb[k]))
            a_pw[k] += _mm_tn(zpv[:, lanes], dq[:, lanes])
        dzp = jnp.concatenate(dzp_blocks, axis=1)
        t1 = (lax.broadcasted_iota(jnp.int32, (tm, 1), 0) + (tile * tm + 1)).astype(F32)
        gs = dzp / jnp.minimum(t1, _window_lanes())
        n_ext = tm + POOL_HALO
        ext = jnp.concatenate([gs, hist[...]], axis=0)
        v2 = ext + pltpu.roll(ext, n_ext - 1, 0)
        v4 = v2[:, 128:] + pltpu.roll(v2[:, 128:], n_ext - 2, 0)
        v8 = v4[:, 128:] + pltpu.roll(v4[:, 128:], n_ext - 4, 0)
        v16 = v8[:, 128:] + pltpu.roll(v8[:, 128:], n_ext - 8, 0)
        msum = jnp.concatenate([v2[:tm, :128], v4[:tm, :128], v8[:tm, :128], v16[:tm]], axis=1)
        hist[...] = gs[0:POOL_HALO, :]
        du_pool = msum - dzp

        vgv = vg_ref[...].astype(F32)
        val, gate = vgv[:, 0:512], vgv[:, 512:1024]
        sgg = _sigmoid(gate)
        a_wsu[...] += _mm_tn(val * sgg, dy_ssm)
        do = _mm_nt(dy_ssm, w_su[...])
        dvg = jnp.concatenate([do * sgg, do * val * (sgg * (1.0 - sgg))], axis=1)
        a_vs[3:4, :] += _colsum(dvg)
        yv = yss_ref[...].astype(F32)
        a_wglu[...] += _mm_tn(_gelu(yv), dvg)
        dyss = _mm_nt(dvg, w_glu[...]) * _gelu_grad(yv)
        a_vs[2:3, 0:512] += _colsum(dyss * u_ssm)
        du_blocks = []
        for j in range(SSM_BLOCKS):
            lanes = pl.ds(j * SSM_BLOCK_STATE, SSM_BLOCK_STATE)
            in_lanes = slice(j * 128, (j + 1) * 128)
            dyb = dyss[:, in_lanes].astype(MXU_DTYPE)
            ub = u_ssm[:, in_lanes].astype(MXU_DTYPE)
            dre[0] = _mm_nt(dyb, ct[0, j])
            dre[1] = -_mm_nt(dyb, ct[1, j])
            a_ct[0, j] += _mm_tn(sre_ref[:, lanes], dyb)
            a_ct[1, j] -= _mm_tn(sim_ref[:, lanes], dyb)
            a1r, a1i, a2r, a2i, a4r, a4i, pr, pi = [coef[j, k] for k in range(8)]
            rowi = lax.broadcasted_iota(jnp.int32, (SCAN_ROWS, SSM_BLOCK_STATE), 0)

            def step(tt, c, lanes=lanes, a1r=a1r, a1i=a1i, a2r=a2r, a2i=a2i, a4r=a4r, a4i=a4i, pr=pr, pi=pi, rowi=rowi):
                cr, ci, acc_r, acc_i = c
                rows = pl.ds(pl.multiple_of((n_tiles - 1 - tt) * SCAN_ROWS, SCAN_ROWS), SCAN_ROWS)
                xr, xi = dre[0, rows, :], dre[1, rows, :]
                for dstep, kr, ki in ((1, a1r, a1i), (2, a2r, a2i), (4, a4r, a4i)):
                    sr, si = pltpu.roll(xr, SCAN_ROWS - dstep, 0), pltpu.roll(xi, SCAN_ROWS - dstep, 0)
                    xr, xi = xr + kr * sr + ki * si, xi + kr * si - ki * sr
                xr, xi = xr + pr * cr + pi * ci, xi + pr * ci - pi * cr
                lam[0, rows, :] = xr
                lam[1, rows, :] = xi
                nr = jnp.where(rowi == SCAN_ROWS - 1, cr, pltpu.roll(xr, SCAN_ROWS - 1, 0))
                ni = jnp.where(rowi == SCAN_ROWS - 1, ci, pltpu.roll(xi, SCAN_ROWS - 1, 0))
                s_r, s_i = sre_ref[rows, lanes], sim_ref[rows, lanes]
                acc_r = acc_r + nr * s_r + ni * s_i
                acc_i = acc_i + ni * s_r - nr * s_i
                return (jnp.broadcast_to(xr[0:1, :], xr.shape), jnp.broadcast_to(xi[0:1, :], xi.shape), acc_r, acc_i)

            cr, ci, acc_r, acc_i = lax.fori_loop(0, n_tiles, step, (carry[j, 0], carry[j, 1], a_da[0, j], a_da[1, j]))
            carry[j, 0] = cr
            carry[j, 1] = ci
            a_da[0, j] = acc_r
            a_da[1, j] = acc_i
            lr_b, li_b = lam[0].astype(MXU_DTYPE), lam[1].astype(MXU_DTYPE)
            a_bb[0, j] += _mm_tn(ub, lr_b)
            a_bb[1, j] += _mm_tn(ub, li_b)
            du_blocks.append(_mm_nt(lr_b, bb[0, j]) + _mm_nt(li_b, bb[1, j]))
        du_ssm = jnp.concatenate(du_blocks, axis=1) + dyss * mv_ref[ROW_SSM_D:ROW_SSM_D + 1, 0:512]
        dz_ref[...] = jnp.concatenate([du_pool, du_ssm, dgl_pool, dgl_ssm], axis=1).astype(SAVE_DTYPE)

        @pl.when(i == nt - 1)
        def _():
            rows = D_MODEL // NDEV
            for k in range(NDEV):
                st_wo[k] = a_wo[k * rows:(k + 1) * rows, :].astype(WIRE_DTYPE)
                for a, acc in enumerate((a_wpu, a_wglu, a_wsu)):
                    st_up[a, k] = acc[:, k * 128:(k + 1) * 128].astype(WIRE_DTYPE)
            outs = ((st_wo, dwo_h), (st_up.at[0], dwpu_h), (st_up.at[1], dwglu_h), (st_up.at[2], dwsu_h),
                    (a_pw, dpw_h), (a_bb, dbb_h), (a_ct, dct_h), (a_vs, vsum_h), (a_da, da_h))
            cps = [pltpu.make_async_copy(src, dst, sems.at[k]) for k, (src, dst) in enumerate(outs)]
            for cp in cps:
                cp.start()
            for cp in cps:
                cp.wait()

    def tok(width):
        return pl.BlockSpec((tm, width), lambda i: (nt - 1 - i, 0))

    hbm = _HBM
    acc_shapes = [(D_MODEL, D_MODEL), (512, D_MODEL), (512, D_MODEL), (512, D_MODEL), (4, 128, 128),
                  (2, SSM_BLOCKS, 128, SSM_BLOCK_STATE), (2, SSM_BLOCKS, SSM_BLOCK_STATE, 128), (8, D_MODEL),
                  (2, SSM_BLOCKS, SCAN_ROWS, SSM_BLOCK_STATE)]
    stack_out = [jax.ShapeDtypeStruct((NDEV, D_MODEL // NDEV, D_MODEL), WIRE_DTYPE)] \
        + [jax.ShapeDtypeStruct((NDEV, 512, 128), WIRE_DTYPE)] * 3
    return _launch(
        body, "mixer_backward", grid=(nt,), semantics=("arbitrary",), carry=carry, steps=_grid_steps(nt),
        out_shape=[jax.ShapeDtypeStruct((T, IN_WIDTH), SAVE_DTYPE)] + stack_out
        + [jax.ShapeDtypeStruct(s, F32) for s in acc_shapes[4:]],
        in_specs=[tok(D_MODEL), _resident(prm), tok(IN_WIDTH), tok(N_STATE), tok(N_STATE), tok(512), tok(512),
                  tok(D_MODEL), tok(512), tok(D_MODEL), tok(D_MODEL), hbm, hbm, hbm, hbm, _resident(pool_w),
                  _resident(mvec), _resident(srow), _resident(bb), _resident(ct)],
        out_specs=[tok(IN_WIDTH)] + [hbm] * len(acc_shapes),
        operands=(d2, prm, z, s_re, s_im, zp, q, y_pool, yss, vg, y_ssm, w_pu_s, w_glu_s, w_su_s, w_out, pool_w, mvec,
                  srow, bb, ct),
        scratch=[
            pltpu.VMEM((512, D_MODEL), MXU_DTYPE), pltpu.VMEM((512, D_MODEL), MXU_DTYPE),
            pltpu.VMEM((512, D_MODEL), MXU_DTYPE), pltpu.VMEM((D_MODEL, D_MODEL), MXU_DTYPE),
            pltpu.VMEM((4, 128, 128), MXU_DTYPE),
            pltpu.VMEM((SSM_BLOCKS, 8, SCAN_ROWS, SSM_BLOCK_STATE), F32),
            pltpu.VMEM((SSM_BLOCKS, 2, SCAN_ROWS, SSM_BLOCK_STATE), F32),
            pltpu.VMEM((POOL_HALO, POOL_WIDTH), F32),
            pltpu.VMEM((2, tm, SSM_BLOCK_STATE), F32), pltpu.VMEM((2, tm, SSM_BLOCK_STATE), F32),
        ] + [pltpu.VMEM(s, F32) for s in acc_shapes]
        + [pltpu.VMEM((NDEV, D_MODEL // NDEV, D_MODEL), WIRE_DTYPE), pltpu.VMEM((3, NDEV, 512, 128), WIRE_DTYPE),
           pltpu.SemaphoreType.DMA((25,))])


def _mixer_in_backward(x, dz, d, mo, prm, w_in_s, carry=None):
    T = x.shape[0]
    tm = min(T, TM_MIX)
    nt = T // tm
    cols = IN_WIDTH // NDEV

    def body(x_ref, dz_ref, d_ref, mo_ref, prm_ref, w_in_h, dx_ref, sums_ref, dw_ref, w_in, acc, sems):
        i = pl.program_id(0)

        @pl.when(i == 0)
        def _():
            cps = _load_stack(w_in_h, w_in, sems, 0)
            acc[...] = jnp.zeros_like(acc)
            for cp in cps:
                cp.wait()

        xv = x_ref[...]
        h, _, _ = _modulated(xv, prm_ref, 1, ROW_G_MIX)
        dzb = dz_ref[...].astype(MXU_DTYPE)
        acc[...] += _mm_tn(h, dzb)
        dx_ref[...], upd = _norm_backward_tile(_mm_nt(dzb, w_in[...]), xv, d_ref[...], mo_ref[...], prm_ref, 1,
                                               ROW_G_MIX, 1.0)

        @pl.when(i == 0)
        def _():
            sums_ref[...] = upd

        @pl.when(i > 0)
        def _():
            sums_ref[...] += upd

        @pl.when(i == nt - 1)
        def _():
            for k in range(NDEV):
                dw_ref[k] = acc[:, k * cols:(k + 1) * cols].astype(WIRE_DTYPE)

    tok = pl.BlockSpec((tm, D_MODEL), lambda i: (i, 0))
    return _launch(
        body, "mixer_in_backward", grid=(nt,), semantics=("arbitrary",), carry=carry, steps=_grid_steps(nt),
        out_shape=[jax.ShapeDtypeStruct((T, D_MODEL), F32), jax.ShapeDtypeStruct((8, D_MODEL), F32),
                   jax.ShapeDtypeStruct((NDEV, D_MODEL, cols), WIRE_DTYPE)],
        in_specs=[tok, pl.BlockSpec((tm, IN_WIDTH), lambda i: (i, 0)), tok, tok, _resident(prm), _HBM],
        out_specs=[tok, pl.BlockSpec((8, D_MODEL), lambda i: (0, 0)),
                   pl.BlockSpec((NDEV, D_MODEL, cols), lambda i: (0, 0, 0))],
        operands=(x, dz, d, mo, prm, w_in_s),
        scratch=[pltpu.VMEM((D_MODEL, IN_WIDTH), MXU_DTYPE), pltpu.VMEM((D_MODEL, IN_WIDTH), F32),
                 pltpu.SemaphoreType.DMA((8,))])


def _ssm_dense_backward(dbb, da, srow, b_dense):
    def body(dbb_ref, da_ref, srow_ref, bd_ref, db_ref, df_ref):
        df_re, df_im = [], []
        da_re = [_colsum(da_ref[0, j]) for j in range(SSM_BLOCKS)]
        da_im = [_colsum(da_ref[1, j]) for j in range(SSM_BLOCKS)]
        for j in range(SSM_BLOCKS):
            lanes = slice(j * SSM_BLOCK_STATE, (j + 1) * SSM_BLOCK_STATE)
            f_re, f_im = srow_ref[2:3, lanes], srow_ref[3:4, lanes]
            g_re, g_im = dbb_ref[0, j], dbb_ref[1, j]
            b_re, b_im = bd_ref[0, j], bd_ref[1, j]
            db_ref[0, j] = f_re * g_re + f_im * g_im
            db_ref[1, j] = f_re * g_im - f_im * g_re
            df_re.append(_colsum(g_re * b_re + g_im * b_im))
            df_im.append(_colsum(g_im * b_re - g_re * b_im))
        df_ref[...] = jnp.concatenate([jnp.concatenate(df_re, axis=1), jnp.concatenate(df_im, axis=1),
                                       jnp.concatenate(da_re, axis=1), jnp.concatenate(da_im, axis=1),
                                       jnp.zeros((4, N_STATE), F32)], axis=0)

    return pl.pallas_call(body, name="ssm_dense_backward",
                          out_shape=[jax.ShapeDtypeStruct(b_dense.shape, F32), jax.ShapeDtypeStruct((8, N_STATE), F32)],
                          compiler_params=pltpu.CompilerParams(vmem_limit_bytes=VMEM_LIMIT))(dbb, da, srow, b_dense)


def _adamw_update(w, g, m, v):
    m = ADAM_B1 * m + (1.0 - ADAM_B1) * g
    v = ADAM_B2 * v + (1.0 - ADAM_B2) * (g * g)
    m_hat = m / (1.0 - ADAM_B1 ** ADAM_STEP)
    v_hat = v / (1.0 - ADAM_B2 ** ADAM_STEP)
    delta = -ADAM_LR * (m_hat / (jnp.sqrt(v_hat) + ADAM_EPS) + ADAM_WD * w)
    return delta, m, v


def _adam_rows(shape):
    rows, cols = shape
    tr = rows
    while tr * cols * 4 > (1 << 20) and tr % 16 == 0:
        tr //= 2
    return tr


def _adam_sharded(w, m, v, land, order, name):
    R, C = w.shape
    tr = _adam_rows((R, C))

    def body(w_ref, m_ref, v_ref, land_ref, order_ref, g_ref, d_ref, mo_ref, vo_ref):
        g = land_ref[0].astype(F32)
        for b in range(1, NDEV):
            g = g + land_ref[b].astype(F32)
        g_ref[...] = g
        d_ref[...], mo_ref[...], vo_ref[...] = _adamw_update(w_ref[...], g, m_ref[...], v_ref[...])

    blk = pl.BlockSpec((tr, C), lambda i: (i, 0))
    return pl.pallas_call(
        body, name=name, grid=(R // tr,),
        out_shape=[jax.ShapeDtypeStruct((R, C), F32)] * 4,
        in_specs=[blk, blk, blk, pl.BlockSpec((NDEV, tr, C), lambda i: (0, i, 0)), _HBM],
        out_specs=[blk] * 4,
        compiler_params=_params("arbitrary"),
    )(w, m, v, land, order)


def _adam_ada(w, m, v, sc_all, dmod_cols):
    R, C = w.shape
    tr = 256

    def body(w_ref, m_ref, v_ref, sc_ref, dm_ref, g_ref, d_ref, mo_ref, vo_ref):
        g = _mm_tn(sc_ref[...], dm_ref[...])
        g_ref[...] = g
        d_ref[...], mo_ref[...], vo_ref[...] = _adamw_update(w_ref[...], g, m_ref[...], v_ref[...])

    blk = pl.BlockSpec((tr, C), lambda i: (i, 0))
    return pl.pallas_call(
        body, name="adam_w_ada", grid=(R // tr,),
        out_shape=[jax.ShapeDtypeStruct((R, C), F32)] * 4,
        in_specs=[blk, blk, blk, pl.BlockSpec((8, tr), lambda i: (0, i)), pl.BlockSpec((8, C), lambda i: (0, 0))],
        out_specs=[blk] * 4,
        compiler_params=_params("arbitrary"),
    )(w, m, v, sc_all, dmod_cols)


def _adam_small(w, g, m, v, name):
    def body(w_ref, g_ref, m_ref, v_ref, d_ref, mo_ref, vo_ref):
        d_ref[...], mo_ref[...], vo_ref[...] = _adamw_update(w_ref[...], g_ref[...], m_ref[...], v_ref[...])

    return pl.pallas_call(body, name=name, out_shape=[jax.ShapeDtypeStruct(w.shape, F32)] * 3,
                          compiler_params=pltpu.CompilerParams(vmem_limit_bytes=VMEM_LIMIT))(w, g, m, v)


def _block_diag_in(b):
    bt = jnp.transpose(b, (0, 2, 1)).reshape(SSM_BLOCKS, 8, SSM_GROUP, SSM_STATE)
    eye = jnp.eye(8, dtype=bool)[None, :, None, :, None]
    return jnp.where(eye, bt[:, :, :, None, :], 0.0).reshape(SSM_BLOCKS, 128, SSM_BLOCK_STATE)


def _block_diag_out(c):
    ct = jnp.transpose(c, (0, 2, 1)).reshape(SSM_BLOCKS, 8, SSM_STATE, SSM_GROUP)
    eye = jnp.eye(8, dtype=bool)[None, :, None, :, None]
    return jnp.where(eye, ct[:, :, :, None, :], 0.0).reshape(SSM_BLOCKS, SSM_BLOCK_STATE, 128)


def _diag_blocks(dense, rows, cols):
    d5 = dense.reshape(SSM_BLOCKS, 8, rows, 8, cols)
    return jnp.stack([d5[:, a, :, a, :] for a in range(8)], axis=1).reshape(N_SSM_GROUPS, rows, cols)


def _pack_small(ada_vec, parts, params, tail=None):
    rest_rows, rows = _pack_rows(params, ada_vec is not None)
    rest = jnp.concatenate([parts[n].reshape(-1) for n, _ in params])
    rest = jnp.pad(rest, (0, NDEV * rest_rows * 128 - rest.shape[0])).reshape(NDEV, rest_rows, 128)
    head = [] if ada_vec is None else [ada_vec.reshape(NDEV, ADA_ROWS, 128)]
    pad = rows - rest_rows - (0 if ada_vec is None else ADA_ROWS)
    fill = jnp.zeros((NDEV, pad, 128), F32) if tail is None else jnp.pad(tail[None], ((0, NDEV - 1), (0, pad - 1), (0, 127)))
    return jnp.concatenate(head + [rest] + ([fill] if pad else []), axis=1)


def _unpack_small(pack, shapes, params, with_ada):
    rest_rows, _ = _pack_rows(params, with_ada)
    first = ADA_ROWS if with_ada else 0
    ada_vec = pack[:, :first].reshape(-1) if with_ada else None
    rest = pack[:, first:first + rest_rows].reshape(-1)
    out, off = {}, 0
    for n, size in params:
        out[n] = rest[off:off + size].reshape(shapes[n])
        off += size
    return ada_vec, out


WEIGHT_ORDER = ('w_ada', 'b_ada', 'g_ffn1', 'w_ffn1_in', 'w_ffn1_out', 'g_mix', 'w_in', 'pool_w', 'pool_b',
                'pool_scale', 'w_pool_up', 'ssm_lam_re_log', 'ssm_lam_im', 'ssm_log_dt', 'ssm_b_re', 'ssm_b_im',
                'ssm_c_re', 'ssm_c_im', 'ssm_d', 'w_glu', 'b_glu', 'w_ssm_up', 'w_out', 'g_ffn2', 'w_ffn2_in',
                'w_ffn2_out', 'g_final')
GATHERED = ('w_ffn1_in', 'w_ffn1_out', 'w_in', 'w_pool_up', 'w_glu', 'w_ssm_up', 'w_out', 'w_ffn2_in', 'w_ffn2_out')
TRANSPOSED = ('w_ffn1_in', 'w_ffn2_in')
STATE_MINOR = ('ssm_b_re', 'ssm_b_im')


def kernel(x, c, w_ada, b_ada, g_ffn1, w_ffn1_in, w_ffn1_out, g_mix, w_in, pool_w, pool_b, pool_scale, w_pool_up, ssm_lam_re_log, ssm_lam_im, ssm_log_dt, ssm_b_re, ssm_b_im, ssm_c_re, ssm_c_im, ssm_d, w_glu, b_glu, w_ssm_up, w_out, g_ffn2, w_ffn2_in, w_ffn2_out, g_final, loss_target, m_w_ada, m_b_ada, m_g_ffn1, m_w_ffn1_in, m_w_ffn1_out, m_g_mix, m_w_in, m_pool_w, m_pool_b, m_pool_scale, m_w_pool_up, m_ssm_lam_re_log, m_ssm_lam_im, m_ssm_log_dt, m_ssm_b_re, m_ssm_b_im, m_ssm_c_re, m_ssm_c_im, m_ssm_d, m_w_glu, m_b_glu, m_w_ssm_up, m_w_out, m_g_ffn2, m_w_ffn2_in, m_w_ffn2_out, m_g_final, v_w_ada, v_b_ada, v_g_ffn1, v_w_ffn1_in, v_w_ffn1_out, v_g_mix, v_w_in, v_pool_w, v_pool_b, v_pool_scale, v_w_pool_up, v_ssm_lam_re_log, v_ssm_lam_im, v_ssm_log_dt, v_ssm_b_re, v_ssm_b_im, v_ssm_c_re, v_ssm_c_im, v_ssm_d, v_w_glu, v_b_glu, v_w_ssm_up, v_w_out, v_g_ffn2, v_w_ffn2_in, v_w_ffn2_out, v_g_final):
    args = locals()
    W = {n: args[n] for n in WEIGHT_ORDER}
    M = {n: args["m_" + n] for n in WEIGHT_ORDER}
    V = {n: args["v_" + n] for n in WEIGHT_ORDER}
    shapes = {n: W[n].shape for n in WEIGHT_ORDER}
    xt, tgt = x[0], loss_target[0]

    def local(tree, n):
        return jnp.swapaxes(tree[n][0], 0, 1) if n in TRANSPOSED else tree[n][0]

    def as_output(n, a):
        return (jnp.swapaxes(a, 0, 1) if n in TRANSPOSED else a)[None]

    shard = dict(zip(GATHERED, _cast_shards([local(W, n) for n in GATHERED])))
    stacks = {}

    def gather(names):
        return _Gather([shard[n] for n in names])

    def gathered(names, results):
        stacks.update(zip(names, results))

    ffn1_w, ffn2_w = ('w_ffn1_in', 'w_ffn1_out'), ('w_ffn2_in', 'w_ffn2_out')
    mix_w = ('w_in', 'w_pool_up', 'w_glu', 'w_ssm_up', 'w_out')
    mod_cols, sc_all, *res = _ada_forward(c, W['w_ada'][0], b_ada.reshape(NDEV, -1), gather(ffn1_w[:1]))
    gathered(ffn1_w[:1], res)
    win1 = stacks['w_ffn1_in'].reshape(2, 4, FF_SHARD, D_MODEL)
    prm = jnp.concatenate([mod_cols.reshape(9, D_MODEL), g_ffn1, g_mix, g_ffn2, g_final[None], jnp.zeros((3, D_MODEL), F32)], axis=0)
    pad512 = jnp.zeros((1, D_MODEL - 512), F32)
    mvec = jnp.concatenate([jnp.concatenate([pool_b, pad512], axis=1), jnp.concatenate([pool_scale, pad512], axis=1),
                            jnp.concatenate([ssm_d, pad512], axis=1), b_glu, jnp.zeros((4, D_MODEL), F32)], axis=0)
    log_dt_col = ssm_log_dt[0][:, None]
    coeffs = _ssm_params_forward(ssm_lam_re_log[0], ssm_lam_im[0], log_dt_col)
    srow = jnp.stack([t.reshape(N_STATE) for t in coeffs], axis=0)
    b_dense = jnp.stack([_block_diag_in(ssm_b_re[0]), _block_diag_in(ssm_b_im[0])], axis=0)
    c_dense = jnp.stack([_block_diag_out(ssm_c_re[0]), _block_diag_out(ssm_c_im[0])], axis=0)
    bb, ct = _ssm_dense_forward(srow, b_dense, c_dense)
    pw = pool_w[0]

    next_w = ffn1_w[1:] + mix_w[:1]
    ab1, s1, *res = _ffn_hidden(xt, prm, win1, 0, ROW_G_FFN1, "ffn1_hidden", gather(next_w))
    gathered(next_w, res)
    wout1 = stacks['w_ffn1_out'].reshape(4, FF_SHARD, D_MODEL)
    x1, f1, *res = _ffn_out(xt, s1, prm, wout1, 0, "ffn1_out", gather(mix_w[1:]))
    gathered(mix_w[1:], res)
    w_out_full = stacks['w_out'].reshape(D_MODEL, D_MODEL)
    res = _mixer_forward(x1, prm, stacks['w_in'], stacks['w_pool_up'], stacks['w_glu'], stacks['w_ssm_up'],
                         w_out_full, pw, mvec, srow, bb, ct, gather(ffn2_w))
    x2, mo, saved = res[0], res[1], res[2:11]
    gathered(ffn2_w, res[11:])
    win2 = stacks['w_ffn2_in'].reshape(2, 4, FF_SHARD, D_MODEL)
    wout2 = stacks['w_ffn2_out'].reshape(4, FF_SHARD, D_MODEL)
    d3, fin, f3, ab3 = _ffn_forward_loss(x2, tgt, prm, win2, wout2, 2, ROW_G_FFN2, "ffn2_forward_loss")

    lands = {}

    def scatter(grads):
        names = list(grads)
        return _Scatter([grads[n][0] for n in names], [grads[n][1] for n in names], [local(W, n).shape for n in names])

    def scattered(grads, results):
        lands.update(zip(grads, results))

    parts3, dab3, dwout2 = _ffn_backward(d3, ab3, prm, win2, wout2, 2, "ffn2_backward")
    dwin2, = _ffn_dwin(x2, dab3, prm, 2, ROW_G_FFN2, "ffn2_dwin")
    d2, sums3 = _norm_backward(parts3, x2, d3, f3, prm, 2, ROW_G_FFN2, 0.5, "ffn2_norm_backward")
    g_ffn2_w = {'w_ffn2_in': (dwin2, _halves), 'w_ffn2_out': (dwout2.reshape(NDEV, -1, D_MODEL), _stacked)}
    res = _mixer_backward(d2, prm, saved, stacks['w_pool_up'], stacks['w_glu'], stacks['w_ssm_up'], w_out_full, pw, mvec,
                          srow, bb, ct, scatter(g_ffn2_w))
    dz, dwo, dwpu, dwglu, dwsu, dpw, dbb, dct, vsum, da = res[:10]
    scattered(g_ffn2_w, res[10:])
    g_mix_up = {'w_pool_up': (dwpu, _stacked), 'w_glu': (dwglu, _stacked), 'w_ssm_up': (dwsu, _stacked),
                'w_out': (dwo, _stacked)}
    d1, sums2, dwin_mix, *res = _mixer_in_backward(x1, dz, d2, mo, prm, stacks['w_in'], scatter(g_mix_up))
    scattered(g_mix_up, res)
    g_mix_w = {'w_in': (dwin_mix, _stacked)}
    db_dense, df_rows = _ssm_dense_backward(dbb, da, srow, b_dense)
    cot = [df_rows[r].reshape(N_SSM_GROUPS, SSM_STATE) for r in (2, 3, 0, 1)]
    d_lrl, d_li, d_ldt = _ssm_params_backward(ssm_lam_re_log[0], ssm_lam_im[0], log_dt_col, cot)
    small_grads = {
        'g_mix': sums2[0], 'g_ffn2': sums3[0], 'g_final': fin[0], 'pool_w': dpw,
        'pool_b': vsum[1, :512], 'pool_scale': vsum[0, :512], 'ssm_lam_re_log': d_lrl, 'ssm_lam_im': d_li,
        'ssm_log_dt': d_ldt, 'ssm_b_re': _diag_blocks(db_dense[0], SSM_GROUP, SSM_STATE),
        'ssm_b_im': _diag_blocks(db_dense[1], SSM_GROUP, SSM_STATE),
        'ssm_c_re': jnp.transpose(_diag_blocks(dct[0], SSM_STATE, SSM_GROUP), (0, 2, 1)),
        'ssm_c_im': jnp.transpose(_diag_blocks(dct[1], SSM_STATE, SSM_GROUP), (0, 2, 1)),
        'ssm_d': vsum[2, :512], 'b_glu': vsum[3],
    }
    early = _SmallAllReduce(_pack_small(None, small_grads, SMALL_EARLY, fin[1:2, 0:1]))
    parts1, dab1, dwout1, total_early, *res = _ffn_backward(d1, ab1, prm, win1, wout1, 0, "ffn1_backward",
                                                            _Carried(early, scatter(g_mix_w)))
    scattered(g_mix_w, res)
    loss = total_early[0, _pack_rows(SMALL_EARLY, False)[0], 0]
    g_wout1 = {'w_ffn1_out': (dwout1.reshape(NDEV, -1, D_MODEL), _stacked)}
    dwin1, *res = _ffn_dwin(xt, dab1, prm, 0, ROW_G_FFN1, "ffn1_dwin", scatter(g_wout1))
    scattered(g_wout1, res)

    last_w, last_views = ffn1_w[:1], [_halves]
    send_sems, recv_sems, last_src, last_land, token = _scatter_start(
        [dwin1], last_views, [local(W, n).shape for n in last_w], [total_early])
    after_start = token[0:1, 0:1]
    d0, sums1 = _norm_backward(parts1, xt, d1, f1, prm + after_start, 0, ROW_G_FFN1, 0.5, "ffn1_norm_backward")

    grad, delta, new_m, new_v = {}, {}, {}, {}

    def adam_sharded(n):
        res = _adam_sharded(local(W, n), local(M, n), local(V, n), lands[n], token, "adam_" + n)
        grad[n], delta[n], new_m[n], new_v[n] = [as_output(n, r) for r in res]
        return res[3]

    def adam_small(params, ada, total, name, order):
        rows = _pack_rows(params, ada)[1]
        views = [{n: jnp.transpose(t[n][0], (0, 2, 1)) if n in STATE_MINOR else t[n] for n, _ in params} for t in (W, M, V)]
        packs = [(_pack_small(t['b_ada'].reshape(-1) if ada else None, v, params) + order).reshape(NDEV * rows, 128)
                 for t, v in zip((W, M, V), views)]
        res = _adam_small(packs[0], total.reshape(NDEV * rows, 128), packs[1], packs[2], name)
        view_shapes = {n: (N_SSM_GROUPS, SSM_GROUP, SSM_STATE) if n in STATE_MINOR else shapes[n] for n, _ in params}
        for dst, packed in zip((grad, delta, new_m, new_v), (total, *res)):
            ada_vec, rest = _unpack_small(packed.reshape(NDEV, rows, 128), view_shapes, params, ada)
            dst.update({n: jnp.transpose(a, (0, 2, 1))[None] if n in STATE_MINOR else a for n, a in rest.items()})
            if ada:
                dst['b_ada'] = ada_vec.reshape(shapes['b_ada'])
        return res[2]

    done = [d0] + [adam_sharded(n) for n in GATHERED if n not in last_w]
    done.append(adam_small(SMALL_EARLY, False, total_early, "adam_small_early", after_start))
    lands.update(zip(last_w, _scatter_wait(send_sems, recv_sems, last_src, last_land, last_views, done)))

    dmod = jnp.concatenate([sums1[1:4], sums2[1:4], sums3[1:4]], axis=0).reshape(-1)
    total_late, landed = _allreduce_small(_pack_small(dmod, {'g_ffn1': sums1[0]}, SMALL_LATE), lands[last_w[0]],
                                          "allreduce_late")
    dmod_cols = landed[:, :ADA_ROWS].reshape(NDEV, ADA_ROWS * 128)
    res = _adam_ada(W['w_ada'][0], M['w_ada'][0], V['w_ada'][0], sc_all, dmod_cols)
    grad['w_ada'], delta['w_ada'], new_m['w_ada'], new_v['w_ada'] = [r[None] for r in res]
    adam_small(SMALL_LATE, True, total_late, "adam_small_late", 0.0)
    for n in last_w:
        adam_sharded(n)

    return (loss, d0[None], *[grad[n] for n in WEIGHT_ORDER], *[delta[n] for n in WEIGHT_ORDER],
            *[new_m[n] for n in WEIGHT_ORDER], *[new_v[n] for n in WEIGHT_ORDER])
```

```python
import jax
import jax.numpy as jnp
from jax import lax
from jax.experimental import pallas as pl
from jax.experimental.pallas import tpu as pltpu

F32 = jnp.float32
MXU_DTYPE = jnp.bfloat16
WIRE_DTYPE = jnp.bfloat16
SAVE_DTYPE = jnp.bfloat16

NDEV = 8
D_MODEL = 1024
D_FF = 2816
FF_SHARD = 2 * D_FF // NDEV
POOL_WIDTH = 512
POOL_GROUP = 128
SSM_WIDTH = 512
SSM_STATE = 64
SSM_GROUP = 16
N_SSM_GROUPS = SSM_WIDTH // SSM_GROUP
SSM_BLOCKS = 4
SSM_BLOCK_STATE = 512
N_STATE = 2048
IN_WIDTH = 3072
EPS = 1e-6
ADAM_LR = 0.001
ADAM_B1 = 0.9
ADAM_B2 = 0.999
ADAM_EPS = 1e-08
ADAM_WD = 0.01
ADAM_STEP = 10

TM_FFN = 512
TM_FFN_BWD = 1024
TM_DWIN = 2048
TM_MIX_IN = 512
FFN_BWD_CHUNK = 256
TM_MIX = 256
TM_MIX_BWD = 256
TM_EW = 512
SCAN_ROWS = 8
POOL_HALO = 16
VMEM_LIMIT = 60 * 1024 * 1024

ROW_G_FFN1, ROW_G_MIX, ROW_G_FFN2, ROW_G_FINAL = 9, 10, 11, 12
ROW_POOL_B, ROW_POOL_SCALE, ROW_SSM_D, ROW_B_GLU = 0, 1, 2, 3

SMALL_EARLY = (
    ("g_mix", 1024), ("g_ffn2", 1024), ("g_final", 1024), ("pool_w", 65536),
    ("pool_b", 512), ("pool_scale", 512), ("ssm_lam_re_log", 2048), ("ssm_lam_im", 2048),
    ("ssm_log_dt", 32), ("ssm_b_re", 32768), ("ssm_b_im", 32768), ("ssm_c_re", 32768),
    ("ssm_c_im", 32768), ("ssm_d", 512), ("b_glu", 1024),
)
SMALL_LATE = (("g_ffn1", 1024),)
ADA_ROWS = 9
MESH = pl.DeviceIdType.MESH


def _pack_rows(params, with_ada):
    rest = -(-sum(n for _, n in params) // (NDEV * 128))
    return rest, -(-(rest + (ADA_ROWS if with_ada else 0)) // 8) * 8


def _mm(a, b):
    return jnp.dot(a.astype(MXU_DTYPE), b.astype(MXU_DTYPE), preferred_element_type=F32)


def _mm_nt(a, b):
    return lax.dot_general(a.astype(MXU_DTYPE), b.astype(MXU_DTYPE), (((1,), (1,)), ((), ())),
                           preferred_element_type=F32)


def _mm_tn(a, b):
    return lax.dot_general(a.astype(MXU_DTYPE), b.astype(MXU_DTYPE), (((0,), (0,)), ((), ())),
                           preferred_element_type=F32)


def _rms_scale(x):
    return lax.rsqrt(jnp.mean(x * x, axis=-1, keepdims=True) + EPS)


def _sigmoid(x):
    return jax.nn.sigmoid(x)


def _colsum(x):
    return jnp.sum(x, axis=0, keepdims=True)


def _row(ref, r):
    return ref[r:r + 1, :]


def _params(*sem):
    return pltpu.CompilerParams(dimension_semantics=sem, vmem_limit_bytes=VMEM_LIMIT)


def _resident(a):
    return pl.BlockSpec(a.shape, lambda *_: (0,) * a.ndim, pipeline_mode=pl.Buffered(1))


def _me():
    return lax.axis_index("x"), lax.axis_index("y"), lax.axis_index("c")


def _peer(rel):
    x, y, c = _me()
    px = 1 - x if rel & 4 else x
    py = 1 - y if rel & 2 else y
    pc = 1 - c if rel & 1 else c
    return (px, py, pc), 4 * px + 2 * py + pc


_HBM = pl.BlockSpec(memory_space=pl.ANY)
_HBM_ONLY = pl.BlockSpec(memory_space=pltpu.HBM)


def _stacked(ref, p):
    return ref.at[p]


def _halves(ref, p):
    return ref.at[p // 4, p % 4]


class _Gather:
    def __init__(self, shards):
        self.operands = list(shards)
        self.n = len(shards)
        self.out_shape = [jax.ShapeDtypeStruct((NDEV,) + s.shape, s.dtype) for s in shards]
        self.scratch = [pltpu.SemaphoreType.DMA((7 * self.n,)), pltpu.SemaphoreType.DMA((7 * self.n,)),
                        pltpu.SemaphoreType.DMA((self.n,))]

    def plan(self, srcs, outs, sems):
        send_sems, recv_sems, local_sems = sems
        n = self.n
        x, y, c = _me()
        me = 4 * x + 2 * y + c
        here, sibling = (x, y, c), (x, y, 1 - c)
        chips = [(1 - x, y), (x, 1 - y), (1 - x, 1 - y)]

        def blk(px, py, pc):
            return 4 * px + 2 * py + pc

        def copy(a, k, block, to, src=None):
            return pltpu.make_async_remote_copy(
                src_ref=outs[a].at[block] if src is None else src, dst_ref=outs[a].at[block],
                send_sem=send_sems.at[7 * a + k], recv_sem=recv_sems.at[7 * a + k], device_id=to, device_id_type=MESH)

        def mine(a):
            return pltpu.make_async_copy(srcs[a], outs[a].at[me], local_sems.at[a])

        def first(a):
            return [copy(a, 0, me, sibling, src=srcs[a])] + [copy(a, 1 + j, me, (*chip, c), src=srcs[a])
                                                              for j, chip in enumerate(chips)]

        def start():
            for a in range(n):
                mine(a).start()
                for cp in first(a):
                    cp.start()

        def forward():
            for a in range(n):
                for j, chip in enumerate(chips):
                    copy(a, 1 + j, blk(*chip, c), here).wait_recv()
                    copy(a, 4 + j, blk(*chip, c), sibling).start()

        def finish():
            for a in range(n):
                copy(a, 0, blk(x, y, 1 - c), here).wait_recv()
                for j, chip in enumerate(chips):
                    copy(a, 4 + j, blk(*chip, 1 - c), here).wait_recv()
            for a in range(n):
                mine(a).wait()
                for cp in first(a):
                    cp.wait_send()
                for j, chip in enumerate(chips):
                    copy(a, 4 + j, blk(*chip, c), sibling).wait_send()

        return start, forward, finish


class _Scatter:
    def __init__(self, arrays, views, shard_shapes):
        self.operands = list(arrays)
        self.views = list(views)
        self.n = len(arrays)
        self.out_shape = [jax.ShapeDtypeStruct((NDEV,) + tuple(s), a.dtype) for s, a in zip(shard_shapes, arrays)]
        self.scratch = [pltpu.SemaphoreType.DMA((7 * self.n,)), pltpu.SemaphoreType.DMA((7 * self.n,)),
                        pltpu.SemaphoreType.DMA((self.n,))]

    def plan(self, srcs, outs, sems):
        send_sems, recv_sems, local_sems = sems
        n, views = self.n, self.views
        x, y, c = _me()
        me = 4 * x + 2 * y + c

        def mine(a):
            return pltpu.make_async_copy(views[a](srcs[a], me), outs[a].at[me], local_sems.at[a])

        def copy(a, rel, sending):
            to, p = _peer(rel)
            return pltpu.make_async_remote_copy(
                src_ref=views[a](srcs[a], p), dst_ref=outs[a].at[me if sending else p],
                send_sem=send_sems.at[7 * a + rel - 1], recv_sem=recv_sems.at[7 * a + rel - 1],
                device_id=to if sending else (x, y, c), device_id_type=MESH)

        def start():
            for a in range(n):
                mine(a).start()
            for rel in range(1, 8):
                for a in range(n):
                    copy(a, rel, True).start()

        def forward():
            pass

        def finish():
            for rel in range(1, 8):
                for a in range(n):
                    copy(a, rel, False).wait_recv()
            for rel in range(1, 8):
                for a in range(n):
                    copy(a, rel, True).wait_send()
            for a in range(n):
                mine(a).wait()

        return start, forward, finish


class _SmallAllReduce:
    def __init__(self, pack):
        rows = pack.shape[1]
        self.operands = [pack]
        self.n = 1
        self.out_shape = [jax.ShapeDtypeStruct(pack.shape, F32)]
        self.scratch = [pltpu.VMEM(pack.shape, F32), pltpu.VMEM((rows, 128), F32)] \
            + [pltpu.SemaphoreType.DMA((7,))] * 4 + [pltpu.SemaphoreType.DMA((2,))]

    def plan(self, srcs, outs, scratch):
        pack, total = srcs[0], outs[0]
        land, mine, send1, recv1, send2, recv2, local = scratch
        x, y, c = _me()
        me = 4 * x + 2 * y + c

        def slab(rel, sending):
            to, p = _peer(rel)
            return pltpu.make_async_remote_copy(
                src_ref=pack.at[p], dst_ref=land.at[me if sending else p], send_sem=send1.at[rel - 1],
                recv_sem=recv1.at[rel - 1], device_id=to if sending else (x, y, c), device_id_type=MESH)

        def summed(rel, sending):
            to, p = _peer(rel)
            return pltpu.make_async_remote_copy(
                src_ref=mine, dst_ref=total.at[me if sending else p], send_sem=send2.at[rel - 1],
                recv_sem=recv2.at[rel - 1], device_id=to if sending else (x, y, c), device_id_type=MESH)

        own_slab = pltpu.make_async_copy(pack.at[me], land.at[me], local.at[0])
        own_sum = pltpu.make_async_copy(mine, total.at[me], local.at[1])

        def start():
            own_slab.start()
            for rel in range(1, 8):
                slab(rel, True).start()

        def forward():
            own_slab.wait()
            for rel in range(1, 8):
                slab(rel, False).wait_recv()
            acc = land[0]
            for b in range(1, NDEV):
                acc = acc + land[b]
            mine[...] = acc
            own_sum.start()
            for rel in range(1, 8):
                summed(rel, True).start()

        def finish():
            for rel in range(1, 8):
                summed(rel, False).wait_recv()
            for rel in range(1, 8):
                slab(rel, True).wait_send()
                summed(rel, True).wait_send()
            own_sum.wait()

        return start, forward, finish


class _Carried:
    def __init__(self, *parts):
        self.parts = parts
        self.operands = [o for p in parts for o in p.operands]
        self.n = len(self.operands)
        self.out_shape = [s for p in parts for s in p.out_shape]
        self.scratch = [s for p in parts for s in p.scratch]

    def plan(self, srcs, outs, scratch):
        plans, a, b = [], 0, 0
        for p in self.parts:
            plans.append(p.plan(srcs[a:a + p.n], outs[a:a + p.n], scratch[b:b + len(p.scratch)]))
            a, b = a + p.n, b + len(p.scratch)

        def every(k):
            def run():
                for plan in plans:
                    plan[k]()
            return run

        return every(0), every(1), every(2)


def _launch(body, name, out_shape, in_specs, out_specs, operands, scratch=(), grid=None, semantics=None,
            carry=None, steps=None):
    out_shape, in_specs, out_specs = list(out_shape), list(in_specs), list(out_specs)
    operands, scratch = list(operands), list(scratch)
    n_in, n_out, n_scr = len(in_specs), len(out_shape), len(scratch)
    kernel_body = body
    if carry is not None:
        k = carry.n

        def kernel_body(*refs):
            ins, cin = refs[:n_in], refs[n_in:n_in + k]
            outs, cout = refs[n_in + k:n_in + k + n_out], refs[n_in + k + n_out:n_in + 2 * k + n_out]
            rest = refs[n_in + 2 * k + n_out:]
            scr, csem = rest[:n_scr], rest[n_scr:]
            start, forward, finish = carry.plan(cin, cout, csem)
            if steps is None:
                start()
                body(*ins, *outs, *scr)
                forward()
                finish()
            else:
                pl.when(steps()[0])(start)
                pl.when(steps()[1])(forward)
                body(*ins, *outs, *scr)
                pl.when(steps()[2])(finish)

        in_specs += [_HBM] * k
        out_shape += carry.out_shape
        out_specs += [_HBM] * k
        operands += carry.operands
        scratch += carry.scratch
    kwargs = {} if grid is None else {"grid": grid}
    params = pltpu.CompilerParams(vmem_limit_bytes=VMEM_LIMIT) if semantics is None else _params(*semantics)
    return pl.pallas_call(kernel_body, name=name, out_shape=out_shape, in_specs=in_specs, out_specs=out_specs,
                          scratch_shapes=scratch, compiler_params=params, **kwargs)(*operands)


def _grid_steps(nt):
    def steps():
        i = pl.program_id(0)
        return i == 0, i == nt - 1, i == nt - 1
    return steps


def _cast_shards(shards):
    n = len(shards)

    def body(*refs):
        for a in range(n):
            refs[n + a][...] = refs[a][...].astype(WIRE_DTYPE)

    return pl.pallas_call(body, name="cast_shards",
                          out_shape=[jax.ShapeDtypeStruct(s.shape, WIRE_DTYPE) for s in shards],
                          compiler_params=pltpu.CompilerParams(vmem_limit_bytes=VMEM_LIMIT))(*shards)


_SEM = pl.BlockSpec(memory_space=pltpu.SEMAPHORE)
_DATAFLOW = pltpu.SideEffectType.DATAFLOW_SIDE_EFFECTING


def _split_copy(arrays, views, landing, send_sems, recv_sems, a, rel):
    to, p = _peer(rel)
    x, y, c = _me()
    return pltpu.make_async_remote_copy(
        src_ref=views[a](arrays[a], p), dst_ref=landing[a].at[4 * x + 2 * y + c],
        send_sem=send_sems.at[NDEV * a + rel], recv_sem=recv_sems.at[NDEV * a + rel], device_id=to, device_id_type=MESH)


def _scatter_start(arrays, views, shard_shapes, after):
    n = len(arrays)
    landing = [pltpu.with_memory_space_constraint(lax.empty((NDEV,) + tuple(s), a.dtype), pltpu.HBM)
               for s, a in zip(shard_shapes, arrays)]
    arrays = [pltpu.with_memory_space_constraint(a, pltpu.HBM) for a in arrays]

    def body(*refs):
        ins, land = refs[:n], refs[n:2 * n]
        send_sems, recv_sems = refs[2 * n + len(after)], refs[2 * n + len(after) + 1]
        token = refs[-1]
        for rel in range(NDEV):
            for a in range(n):
                _split_copy(ins, views, land, send_sems, recv_sems, a, rel).start()
        token[...] = jnp.zeros_like(token)

    res = pl.pallas_call(
        body, name="scatter_start",
        out_shape=[pltpu.SemaphoreType.DMA((NDEV * n,)), pltpu.SemaphoreType.DMA((NDEV * n,))]
        + [pltpu.HBM(a.shape, a.dtype) for a in arrays] + [pltpu.HBM(l.shape, l.dtype) for l in landing]
        + [jax.ShapeDtypeStruct((8, 128), F32)],
        in_specs=[_HBM_ONLY] * (2 * n) + [_HBM] * len(after),
        out_specs=[_SEM, _SEM] + [_HBM_ONLY] * (2 * n) + [pl.BlockSpec(memory_space=pltpu.VMEM)],
        input_output_aliases={i: 2 + i for i in range(2 * n)},
        compiler_params=pltpu.CompilerParams(has_side_effects=_DATAFLOW),
    )(*arrays, *landing, *after)
    return res[0], res[1], res[2:2 + n], res[2 + n:2 + 2 * n], res[-1]


def _scatter_wait(send_sems, recv_sems, arrays, landing, views, after):
    n = len(arrays)

    def body(*refs):
        ins, land = refs[:n], refs[n:2 * n]
        send, recv = refs[2 * n], refs[2 * n + 1]
        for rel in range(NDEV):
            for a in range(n):
                cp = _split_copy(ins, views, land, send, recv, a, rel)
                cp.wait_send()
                cp.wait_recv()

    res = pl.pallas_call(
        body, name="scatter_wait",
        out_shape=[pltpu.HBM(a.shape, a.dtype) for a in arrays] + [pltpu.HBM(l.shape, l.dtype) for l in landing],
        in_specs=[_HBM_ONLY] * (2 * n) + [_SEM, _SEM] + [_HBM] * len(after),
        out_specs=[_HBM_ONLY] * (2 * n),
        input_output_aliases={i: i for i in range(2 * n)},
        compiler_params=pltpu.CompilerParams(has_side_effects=_DATAFLOW),
    )(*arrays, *landing, send_sems, recv_sems, *after)
    return res[n:]


def _ada_forward(c_row, w_ada, b_ada8, carry):
    cols = w_ada.shape[1]

    def body(c_ref, w_ref, b_ref, mod_ref, sc_ref, c_all, send_buf, recv_buf, send1, recv1, send2, recv2):
        x, y, c = _me()
        me = 4 * x + 2 * y + c
        rowi = lax.broadcasted_iota(jnp.int32, (8, D_MODEL), 0)
        c_all[me] = jnp.broadcast_to(c_ref[...], (8, D_MODEL))
        copies = []
        for rel in range(1, 8):
            to, _ = _peer(rel)
            cp = pltpu.make_async_remote_copy(src_ref=c_all.at[me], dst_ref=c_all.at[me], send_sem=send1.at[rel - 1],
                                              recv_sem=recv1.at[rel - 1], device_id=to, device_id_type=MESH)
            cp.start()
            copies.append(cp)
        for rel in range(1, 8):
            _, p = _peer(rel)
            pltpu.make_async_remote_copy(src_ref=c_all.at[p], dst_ref=c_all.at[p], send_sem=send1.at[rel - 1],
                                         recv_sem=recv1.at[rel - 1], device_id=(x, y, c), device_id_type=MESH).wait_recv()
        for cp in copies:
            cp.wait_send()
        cmat = jnp.zeros((8, D_MODEL), F32)
        for b in range(8):
            cmat = jnp.where(rowi == b, c_all[b], cmat)
        sc = cmat * _sigmoid(cmat)
        sc_ref[...] = sc
        modcols = _mm(sc, w_ref[...]) + b_ref[pl.ds(me, 1), :]
        for b in range(8):
            send_buf[b] = jnp.broadcast_to(modcols[b:b + 1, :], (8, cols))
        recv_buf[me] = send_buf[me]
        copies = []
        for rel in range(1, 8):
            to, p = _peer(rel)
            cp = pltpu.make_async_remote_copy(src_ref=send_buf.at[p], dst_ref=recv_buf.at[me], send_sem=send2.at[rel - 1],
                                              recv_sem=recv2.at[rel - 1], device_id=to, device_id_type=MESH)
            cp.start()
            copies.append(cp)
        for rel in range(1, 8):
            _, p = _peer(rel)
            pltpu.make_async_remote_copy(src_ref=send_buf.at[p], dst_ref=recv_buf.at[p], send_sem=send2.at[rel - 1],
                                         recv_sem=recv2.at[rel - 1], device_id=(x, y, c), device_id_type=MESH).wait_recv()
        for cp in copies:
            cp.wait_send()
        rowc = lax.broadcasted_iota(jnp.int32, (8, cols), 0)
        out = jnp.zeros((8, cols), F32)
        for k in range(8):
            out = jnp.where(rowc == k, recv_buf[k], out)
        mod_ref[...] = out

    return _launch(
        body, "ada_forward",
        out_shape=[jax.ShapeDtypeStruct((8, cols), F32), jax.ShapeDtypeStruct((8, D_MODEL), F32)],
        in_specs=[pl.BlockSpec(memory_space=pltpu.VMEM)] * 3,
        out_specs=[pl.BlockSpec(memory_space=pltpu.VMEM)] * 2,
        operands=(c_row, w_ada, b_ada8),
        scratch=[pltpu.VMEM((8, 8, D_MODEL), F32), pltpu.VMEM((8, 8, cols), F32), pltpu.VMEM((8, 8, cols), F32)]
        + [pltpu.SemaphoreType.DMA((7,))] * 4,
        carry=carry)


def _allreduce_small(pack, order, name):
    rows = pack.shape[1]

    def body(pack_ref, order_ref, total_ref, land_ref, send1, recv1, send2, recv2):
        x, y, c = _me()
        me = 4 * x + 2 * y + c
        land_ref[me] = pack_ref[me]
        copies = []
        for rel in range(1, 8):
            to, p = _peer(rel)
            cp = pltpu.make_async_remote_copy(src_ref=pack_ref.at[p], dst_ref=land_ref.at[me], send_sem=send1.at[rel - 1],
                                              recv_sem=recv1.at[rel - 1], device_id=to, device_id_type=MESH)
            cp.start()
            copies.append(cp)
        for rel in range(1, 8):
            _, p = _peer(rel)
            pltpu.make_async_remote_copy(src_ref=pack_ref.at[p], dst_ref=land_ref.at[p], send_sem=send1.at[rel - 1],
                                         recv_sem=recv1.at[rel - 1], device_id=(x, y, c), device_id_type=MESH).wait_recv()
        for cp in copies:
            cp.wait_send()
        acc = land_ref[0]
        for b in range(1, 8):
            acc = acc + land_ref[b]
        total_ref[me] = acc
        copies = []
        for rel in range(1, 8):
            to, _ = _peer(rel)
            cp = pltpu.make_async_remote_copy(src_ref=total_ref.at[me], dst_ref=total_ref.at[me], send_sem=send2.at[rel - 1],
                                              recv_sem=recv2.at[rel - 1], device_id=to, device_id_type=MESH)
            cp.start()
            copies.append(cp)
        for rel in range(1, 8):
            _, p = _peer(rel)
            pltpu.make_async_remote_copy(src_ref=total_ref.at[p], dst_ref=total_ref.at[p], send_sem=send2.at[rel - 1],
                                         recv_sem=recv2.at[rel - 1], device_id=(x, y, c), device_id_type=MESH).wait_recv()
        for cp in copies:
            cp.wait_send()

    return pl.pallas_call(
        body, name=name,
        out_shape=[jax.ShapeDtypeStruct((8, rows, 128), F32), jax.ShapeDtypeStruct((8, rows, 128), F32)],
        in_specs=[pl.BlockSpec(memory_space=pltpu.VMEM), _HBM],
        out_specs=[pl.BlockSpec(memory_space=pltpu.VMEM)] * 2,
        scratch_shapes=[pltpu.SemaphoreType.DMA((7,))] * 4,
        compiler_params=pltpu.CompilerParams(vmem_limit_bytes=VMEM_LIMIT),
    )(pack, order)


def _modulated(x, prm_ref, sub, g_row):
    shift, scale = _row(prm_ref, 3 * sub), _row(prm_ref, 3 * sub + 1)
    g = _row(prm_ref, g_row)
    r = _rms_scale(x)
    n0 = x * r
    return (n0 * g) * (1.0 + scale) + shift, r, n0


def _swiglu_tile(xv, prm_ref, win_ref, wout_ref, ab_ref, sub, g_row):
    h, _, _ = _modulated(xv, prm_ref, sub, g_row)
    hb = h.astype(MXU_DTYPE)
    acc = None
    for j in range(4):
        a = _mm_nt(hb, win_ref[0, j])
        b = _mm_nt(hb, win_ref[1, j])
        ab_ref[0, j] = a.astype(SAVE_DTYPE)
        ab_ref[1, j] = b.astype(SAVE_DTYPE)
        t = _mm((a * _sigmoid(a)) * b, wout_ref[j])
        acc = t if acc is None else acc + t
    return acc


def _loss_tile(xv, target, g):
    r = _rms_scale(xv)
    n0 = xv * r
    err = n0 * g - target
    dy = err / float(D_MODEL)
    dn0 = dy * g
    dx = r * (dn0 - n0 * jnp.mean(dn0 * n0, axis=-1, keepdims=True))
    loss = 0.5 * jnp.sum(jnp.mean(err * err, axis=-1, keepdims=True), axis=0, keepdims=True)
    return dx, _colsum(dy * n0), loss


def _ffn_forward_loss(x, target, prm, win, wout, sub, g_row, name):
    T = x.shape[0]
    tm = min(T, TM_FFN)

    def body(x_ref, t_ref, prm_ref, win_ref, wout_ref, dx_ref, sums_ref, f_ref, ab_ref):
        i = pl.program_id(0)
        xv = x_ref[...]
        acc = _swiglu_tile(xv, prm_ref, win_ref, wout_ref, ab_ref, sub, g_row)
        f_ref[...] = acc.astype(SAVE_DTYPE)
        dx, dg, loss = _loss_tile(xv + (0.5 * _row(prm_ref, 3 * sub + 2)) * acc, t_ref[...], _row(prm_ref, ROW_G_FINAL))
        dx_ref[...] = dx
        upd = jnp.concatenate([dg, jnp.broadcast_to(loss, (1, D_MODEL)), jnp.zeros((6, D_MODEL), F32)], axis=0)

        @pl.when(i == 0)
        def _():
            sums_ref[...] = upd

        @pl.when(i > 0)
        def _():
            sums_ref[...] += upd

    tok = pl.BlockSpec((tm, D_MODEL), lambda i: (i, 0))
    return _launch(
        body, name, grid=(T // tm,), semantics=("arbitrary",),
        out_shape=[jax.ShapeDtypeStruct((T, D_MODEL), F32), jax.ShapeDtypeStruct((8, D_MODEL), F32),
                   jax.ShapeDtypeStruct((T, D_MODEL), SAVE_DTYPE), jax.ShapeDtypeStruct((2, 4, T, FF_SHARD), SAVE_DTYPE)],
        in_specs=[tok, tok, _resident(prm), _resident(win), _resident(wout)],
        out_specs=[tok, pl.BlockSpec((8, D_MODEL), lambda i: (0, 0)), tok,
                   pl.BlockSpec((2, 4, tm, FF_SHARD), lambda i: (0, 0, i, 0))],
        operands=(x, target, prm, win, wout))


def _ffn_hidden(x, prm, win, sub, g_row, name, carry=None):
    T = x.shape[0]
    tm = min(T, TM_FFN)

    def body(x_ref, prm_ref, win_ref, ab_ref, s_ref):
        h, _, _ = _modulated(x_ref[...], prm_ref, sub, g_row)
        hb = h.astype(MXU_DTYPE)
        for j in range(4):
            a = _mm_nt(hb, win_ref[0, j])
            b = _mm_nt(hb, win_ref[1, j])
            ab_ref[0, j] = a.astype(SAVE_DTYPE)
            ab_ref[1, j] = b.astype(SAVE_DTYPE)
            s_ref[j] = ((a * _sigmoid(a)) * b).astype(MXU_DTYPE)

    return _launch(
        body, name, grid=(T // tm,), semantics=("arbitrary",),
        out_shape=[jax.ShapeDtypeStruct((2, 4, T, FF_SHARD), SAVE_DTYPE), jax.ShapeDtypeStruct((4, T, FF_SHARD), MXU_DTYPE)],
        in_specs=[pl.BlockSpec((tm, D_MODEL), lambda i: (i, 0)), _resident(prm), _resident(win)],
        out_specs=[pl.BlockSpec((2, 4, tm, FF_SHARD), lambda i: (0, 0, i, 0)),
                   pl.BlockSpec((4, tm, FF_SHARD), lambda i: (0, i, 0))],
        operands=(x, prm, win), carry=carry, steps=_grid_steps(T // tm))


def _ffn_out(x, s, prm, wout, sub, name, carry=None):
    T = x.shape[0]
    tm = min(T, TM_FFN)

    def body(x_ref, s_ref, prm_ref, wout_ref, xo_ref, f_ref):
        acc = None
        for j in range(4):
            t = _mm(s_ref[j], wout_ref[j])
            acc = t if acc is None else acc + t
        f_ref[...] = acc.astype(SAVE_DTYPE)
        xo_ref[...] = x_ref[...] + (0.5 * _row(prm_ref, 3 * sub + 2)) * acc

    tok = pl.BlockSpec((tm, D_MODEL), lambda i: (i, 0))
    return _launch(
        body, name, grid=(T // tm,), semantics=("arbitrary",),
        out_shape=[jax.ShapeDtypeStruct((T, D_MODEL), F32), jax.ShapeDtypeStruct((T, D_MODEL), SAVE_DTYPE)],
        in_specs=[tok, pl.BlockSpec((4, tm, FF_SHARD), lambda i: (0, i, 0)), _resident(prm), _resident(wout)],
        out_specs=[tok, tok], operands=(x, s, prm, wout), carry=carry, steps=_grid_steps(T // tm))


def _ffn_backward(d, ab, prm, win, wout, sub, name, carry=None):
    T = d.shape[0]
    tm = min(T, TM_FFN_BWD)
    nt = T // tm
    chunk = min(tm, FFN_BWD_CHUNK)

    def body(d_ref, ab_ref, prm_ref, win_ref, wout_ref, dh_ref, dab_ref, dwout_ref, acc_out):
        i = pl.program_id(1)

        @pl.when(i == 0)
        def _():
            acc_out[...] = jnp.zeros_like(acc_out)

        wa, wb, wo = win_ref[0, 0], win_ref[1, 0], wout_ref[0]
        half_gate = 0.5 * _row(prm_ref, 3 * sub + 2)
        ss, dfss = [], []
        for ck in range(tm // chunk):
            rows = slice(ck * chunk, (ck + 1) * chunk)
            a = ab_ref[0, 0, rows, :].astype(F32)
            b = ab_ref[1, 0, rows, :].astype(F32)
            sg = _sigmoid(a)
            si = a * sg
            dfs = (half_gate * d_ref[rows, :]).astype(MXU_DTYPE)
            ds = _mm_nt(dfs, wo)
            da = (ds * b * (sg * (1.0 + a * (1.0 - sg)))).astype(MXU_DTYPE)
            db = (ds * si).astype(MXU_DTYPE)
            dh_ref[0, rows, :] = (_mm(da, wa) + _mm(db, wb)).astype(SAVE_DTYPE)
            dab_ref[0, 0, rows, :] = da
            dab_ref[1, 0, rows, :] = db
            ss.append((si * b).astype(MXU_DTYPE))
            dfss.append(dfs)
        cat = (lambda v: v[0]) if len(ss) == 1 else (lambda v: jnp.concatenate(v, axis=0))
        acc_out[...] += _mm_tn(cat(ss), cat(dfss))

        @pl.when(i == nt - 1)
        def _():
            dwout_ref[0] = acc_out[...].astype(WIRE_DTYPE)

    def steps():
        j, i = pl.program_id(0), pl.program_id(1)
        return (j == 0) & (i == 0), (j == 2) & (i == 0), (j == 3) & (i == nt - 1)

    pre = pl.BlockSpec((2, 1, tm, FF_SHARD), lambda j, i: (0, j, i, 0))
    return _launch(
        body, name, grid=(4, nt), semantics=("arbitrary", "arbitrary"),
        out_shape=[jax.ShapeDtypeStruct((4, T, D_MODEL), SAVE_DTYPE), jax.ShapeDtypeStruct(ab.shape, MXU_DTYPE),
                   jax.ShapeDtypeStruct(wout.shape, WIRE_DTYPE)],
        in_specs=[pl.BlockSpec((tm, D_MODEL), lambda j, i: (i, 0)), pre, _resident(prm),
                  pl.BlockSpec((2, 1, FF_SHARD, D_MODEL), lambda j, i: (0, j, 0, 0)),
                  pl.BlockSpec((1, FF_SHARD, D_MODEL), lambda j, i: (j, 0, 0))],
        out_specs=[pl.BlockSpec((1, tm, D_MODEL), lambda j, i: (j, i, 0)), pre,
                   pl.BlockSpec((1, FF_SHARD, D_MODEL), lambda j, i: (j, 0, 0))],
        operands=(d, ab, prm, win, wout), scratch=[pltpu.VMEM((FF_SHARD, D_MODEL), F32)], carry=carry, steps=steps)


def _ffn_dwin(x, dab, prm, sub, g_row, name, carry=None):
    T = x.shape[0]
    tm = min(T, TM_DWIN)
    nt = T // tm

    def body(x_ref, dab_ref, prm_ref, dwin_ref, acc):
        i = pl.program_id(1)

        @pl.when(i == 0)
        def _():
            acc[...] = jnp.zeros_like(acc)

        h, _, _ = _modulated(x_ref[...], prm_ref, sub, g_row)
        hb = h.astype(MXU_DTYPE)
        acc[0] += _mm_tn(dab_ref[0, 0], hb)
        acc[1] += _mm_tn(dab_ref[1, 0], hb)

        @pl.when(i == nt - 1)
        def _():
            dwin_ref[0, 0] = acc[0].astype(WIRE_DTYPE)
            dwin_ref[1, 0] = acc[1].astype(WIRE_DTYPE)

    def steps():
        j, i = pl.program_id(0), pl.program_id(1)
        return (j == 0) & (i == 0), (j == 2) & (i == 0), (j == 3) & (i == nt - 1)

    return _launch(
        body, name, grid=(4, nt), semantics=("arbitrary", "arbitrary"),
        out_shape=[jax.ShapeDtypeStruct((2, 4, FF_SHARD, D_MODEL), WIRE_DTYPE)],
        in_specs=[pl.BlockSpec((tm, D_MODEL), lambda j, i: (i, 0)),
                  pl.BlockSpec((2, 1, tm, FF_SHARD), lambda j, i: (0, j, i, 0)), _resident(prm)],
        out_specs=[pl.BlockSpec((2, 1, FF_SHARD, D_MODEL), lambda j, i: (0, j, 0, 0))],
        operands=(x, dab, prm), scratch=[pltpu.VMEM((2, FF_SHARD, D_MODEL), F32)], carry=carry, steps=steps)


def _norm_backward_tile(dh, xv, dv, fv, prm_ref, sub, g_row, gate_coef):
    scale, g = _row(prm_ref, 3 * sub + 1), _row(prm_ref, g_row)
    r = _rms_scale(xv)
    n0 = xv * r
    dn = dh * (1.0 + scale)
    dn0 = dn * g
    dx = dv + r * (dn0 - n0 * jnp.mean(dn0 * n0, axis=-1, keepdims=True))
    upd = jnp.concatenate([_colsum(dn * n0), _colsum(dh), _colsum(dh * (n0 * g)),
                           gate_coef * _colsum(dv * fv.astype(F32)), jnp.zeros((4, D_MODEL), F32)], axis=0)
    return dx, upd


def _norm_backward(parts, x, d, f, prm, sub, g_row, gate_coef, name):
    T = x.shape[0]
    tm = min(T, TM_EW)
    P = parts.shape[0]

    def body(p_ref, x_ref, d_ref, f_ref, prm_ref, dx_ref, sums_ref):
        i = pl.program_id(0)
        dh = p_ref[0].astype(F32)
        for k in range(1, P):
            dh = dh + p_ref[k].astype(F32)
        dx_ref[...], upd = _norm_backward_tile(dh, x_ref[...], d_ref[...], f_ref[...], prm_ref, sub, g_row, gate_coef)

        @pl.when(i == 0)
        def _():
            sums_ref[...] = upd

        @pl.when(i > 0)
        def _():
            sums_ref[...] += upd

    tok = pl.BlockSpec((tm, D_MODEL), lambda i: (i, 0))
    return _launch(
        body, name, grid=(T // tm,), semantics=("arbitrary",),
        out_shape=[jax.ShapeDtypeStruct((T, D_MODEL), F32), jax.ShapeDtypeStruct((8, D_MODEL), F32)],
        in_specs=[pl.BlockSpec((P, tm, D_MODEL), lambda i: (0, i, 0)), tok, tok, tok, _resident(prm)],
        out_specs=[tok, pl.BlockSpec((8, D_MODEL), lambda i: (0, 0))],
        operands=(parts, x, d, f, prm))


def _ssm_discretise(lam_re_log, lam_im, log_dt):
    lr = -jnp.exp(lam_re_log)
    dt = jnp.exp(log_dt)
    mag = jnp.exp(lr * dt)
    ang = lam_im * dt
    ab_re = mag * jnp.cos(ang)
    ab_im = mag * jnp.sin(ang)
    num_re = ab_re - 1.0
    num_im = ab_im
    den = lr * lr + lam_im * lam_im
    f_re = (num_re * lr + num_im * lam_im) / den
    f_im = (num_im * lr - num_re * lam_im) / den
    return ab_re, ab_im, f_re, f_im


def _ssm_params_forward(lam_re_log, lam_im, log_dt):
    def body(a_ref, b_ref, c_ref, o0, o1, o2, o3):
        outs = _ssm_discretise(a_ref[...], b_ref[...], c_ref[...])
        for o, v in zip((o0, o1, o2, o3), outs):
            o[...] = v

    return pl.pallas_call(body, name="ssm_params_forward",
                          out_shape=[jax.ShapeDtypeStruct(lam_im.shape, F32)] * 4)(lam_re_log, lam_im, log_dt)


def _ssm_params_backward(lam_re_log, lam_im, log_dt, cot):
    def body(a_ref, b_ref, c_ref, g0, g1, g2, g3, o0, o1, o2):
        _, vjp = jax.vjp(_ssm_discretise, a_ref[...], b_ref[...], c_ref[...])
        d0, d1, d2 = vjp((g0[...], g1[...], g2[...], g3[...]))
        o0[...] = d0
        o1[...] = d1
        o2[...] = d2

    return pl.pallas_call(
        body, name="ssm_params_backward",
        out_shape=[jax.ShapeDtypeStruct(lam_im.shape, F32), jax.ShapeDtypeStruct(lam_im.shape, F32),
                   jax.ShapeDtypeStruct(log_dt.shape, F32)])(lam_re_log, lam_im, log_dt, *cot)


def _ssm_dense_forward(srow, b_dense, c_dense):
    def body(srow_ref, bd_ref, cd_ref, bb_ref, ct_ref):
        for j in range(SSM_BLOCKS):
            lanes = slice(j * SSM_BLOCK_STATE, (j + 1) * SSM_BLOCK_STATE)
            f_re, f_im = srow_ref[2:3, lanes], srow_ref[3:4, lanes]
            bb_ref[0, j] = (f_re * bd_ref[0, j] - f_im * bd_ref[1, j]).astype(MXU_DTYPE)
            bb_ref[1, j] = (f_re * bd_ref[1, j] + f_im * bd_ref[0, j]).astype(MXU_DTYPE)
            ct_ref[0, j] = cd_ref[0, j].astype(MXU_DTYPE)
            ct_ref[1, j] = cd_ref[1, j].astype(MXU_DTYPE)

    return pl.pallas_call(body, name="ssm_dense_forward",
                          out_shape=[jax.ShapeDtypeStruct(b_dense.shape, MXU_DTYPE),
                                     jax.ShapeDtypeStruct(c_dense.shape, MXU_DTYPE)],
                          compiler_params=pltpu.CompilerParams(vmem_limit_bytes=VMEM_LIMIT))(srow, b_dense, c_dense)


def _cmul(p, q):
    return p[0] * q[0] - p[1] * q[1], p[0] * q[1] + p[1] * q[0]


def _scan_coefficients(ar, ai, reverse):
    n = ar.shape[1]
    p = {1: (ar, ai)}
    p[2] = _cmul(p[1], p[1])
    p[3] = _cmul(p[2], p[1])
    p[4] = _cmul(p[2], p[2])
    p[5] = _cmul(p[4], p[1])
    p[6] = _cmul(p[4], p[2])
    p[7] = _cmul(p[4], p[3])
    p[8] = _cmul(p[4], p[4])
    rowi = lax.broadcasted_iota(jnp.int32, (SCAN_ROWS, n), 0)
    tiles = []
    for dstep in (1, 2, 4):
        keep = (rowi < SCAN_ROWS - dstep) if reverse else (rowi >= dstep)
        for part in p[dstep]:
            tiles.append(jnp.where(keep, jnp.broadcast_to(part, (SCAN_ROWS, n)), 0.0))
    for comp in (0, 1):
        t = jnp.zeros((SCAN_ROWS, n), F32)
        for rr in range(SCAN_ROWS):
            power = SCAN_ROWS - rr if reverse else rr + 1
            t = jnp.where(rowi == rr, jnp.broadcast_to(p[power][comp], (SCAN_ROWS, n)), t)
        tiles.append(t)
    return tiles


def _load_stack(stack_hbm, dst, sems, base):
    cols = stack_hbm.shape[2]
    cps = [pltpu.make_async_copy(stack_hbm.at[k], dst.at[:, pl.ds(k * cols, cols)], sems.at[base + k])
           for k in range(NDEV)]
    for cp in cps:
        cp.start()
    return cps


def _window_lanes():
    lane = lax.broadcasted_iota(jnp.int32, (1, POOL_WIDTH), 1)
    return jnp.where(lane < 128, 2.0, jnp.where(lane < 256, 4.0, jnp.where(lane < 384, 8.0, 16.0)))


def _gelu(y):
    return 0.5 * y * (1.0 + lax.erf(y * 0.7071067811865476))


def _gelu_grad(y):
    return 0.5 * (1.0 + lax.erf(y * 0.7071067811865476)) + y * jnp.exp(-0.5 * y * y) * 0.3989422804014327


def _mixer_forward(x, prm, w_in_s, w_pu_s, w_glu_s, w_su_s, w_out, pool_w, mvec, srow, bb, ct, carry=None):
    T = x.shape[0]
    tm = min(T, TM_MIX)
    nt = T // tm
    n_tiles = tm // SCAN_ROWS

    def body(x_ref, prm_ref, w_in_h, w_pu_h, w_glu_h, w_su_h, w_out_h, pw_ref, mv_ref, srow_ref, bb, ct,
             x2_ref, mo_ref, z_ref, sre_ref, sim_ref, zp_ref, q_ref, yp_ref, yss_ref, vg_ref, ys_ref,
             w_in, w_pu, w_glu, w_su, w_o, coef, carry, hist, bu, sems):
        i = pl.program_id(0)

        @pl.when(i == 0)
        def _():
            cps = (_load_stack(w_in_h, w_in, sems, 0) + _load_stack(w_pu_h, w_pu, sems, 8)
                   + _load_stack(w_glu_h, w_glu, sems, 16) + _load_stack(w_su_h, w_su, sems, 24))
            cps.append(pltpu.make_async_copy(w_out_h, w_o, sems.at[32]))
            cps[-1].start()
            for j in range(SSM_BLOCKS):
                lanes = slice(j * SSM_BLOCK_STATE, (j + 1) * SSM_BLOCK_STATE)
                for k, t in enumerate(_scan_coefficients(srow_ref[0:1, lanes], srow_ref[1:2, lanes], False)):
                    coef[j, k] = t
            carry[...] = jnp.zeros_like(carry)
            hist[...] = jnp.zeros_like(hist)
            for cp in cps:
                cp.wait()

        xv = x_ref[...]
        h, _, _ = _modulated(xv, prm_ref, 1, ROW_G_MIX)
        z = _mm(h, w_in[...])
        z_ref[...] = z.astype(SAVE_DTYPE)
        u_pool, u_ssm = z[:, 0:512], z[:, 512:1024]
        gl_pool, gl_ssm = z[:, 1024:2048], z[:, 2048:3072]

        ext = jnp.concatenate([hist[...], u_pool], axis=0)
        w2 = ext + pltpu.roll(ext, 1, 0)
        w4 = w2[:, 128:] + pltpu.roll(w2[:, 128:], 2, 0)
        w8 = w4[:, 128:] + pltpu.roll(w4[:, 128:], 4, 0)
        w16 = w8[:, 128:] + pltpu.roll(w8[:, 128:], 8, 0)
        wsum = jnp.concatenate([w2[POOL_HALO:, :128], w4[POOL_HALO:, :128], w8[POOL_HALO:, :128], w16[POOL_HALO:]], axis=1)
        hist[...] = u_pool[tm - POOL_HALO:, :]
        t1 = (lax.broadcasted_iota(jnp.int32, (tm, 1), 0) + (i * tm + 1)).astype(F32)
        zp = wsum / jnp.minimum(t1, _window_lanes()) - u_pool
        zp_ref[...] = zp.astype(SAVE_DTYPE)
        q = jnp.concatenate([_mm(zp[:, k * 128:(k + 1) * 128], pw_ref[k]) for k in range(4)], axis=1)
        q = q + mv_ref[ROW_POOL_B:ROW_POOL_B + 1, 0:512]
        q_ref[...] = q.astype(SAVE_DTYPE)
        y_pool = _mm(q * mv_ref[ROW_POOL_SCALE:ROW_POOL_SCALE + 1, 0:512], w_pu[...])
        yp_ref[...] = y_pool.astype(SAVE_DTYPE)

        y_blocks = []
        for j in range(SSM_BLOCKS):
            lanes = pl.ds(j * SSM_BLOCK_STATE, SSM_BLOCK_STATE)
            ub = u_ssm[:, j * 128:(j + 1) * 128].astype(MXU_DTYPE)
            bu[0] = _mm(ub, bb[0, j])
            bu[1] = _mm(ub, bb[1, j])
            a1r, a1i, a2r, a2i, a4r, a4i, pr, pi = [coef[j, k] for k in range(8)]

            def step(tt, c, lanes=lanes, a1r=a1r, a1i=a1i, a2r=a2r, a2i=a2i, a4r=a4r, a4i=a4i, pr=pr, pi=pi):
                cr, ci = c
                rows = pl.ds(pl.multiple_of(tt * SCAN_ROWS, SCAN_ROWS), SCAN_ROWS)
                xr, xi = bu[0, rows, :], bu[1, rows, :]
                for dstep, kr, ki in ((1, a1r, a1i), (2, a2r, a2i), (4, a4r, a4i)):
                    sr, si = pltpu.roll(xr, dstep, 0), pltpu.roll(xi, dstep, 0)
                    xr, xi = xr + kr * sr - ki * si, xi + kr * si + ki * sr
                xr, xi = xr + pr * cr - pi * ci, xi + pr * ci + pi * cr
                sre_ref[rows, lanes] = xr
                sim_ref[rows, lanes] = xi
                return (jnp.broadcast_to(xr[SCAN_ROWS - 1:SCAN_ROWS, :], xr.shape),
                        jnp.broadcast_to(xi[SCAN_ROWS - 1:SCAN_ROWS, :], xi.shape))

            cr, ci = lax.fori_loop(0, n_tiles, step, (carry[j, 0], carry[j, 1]))
            carry[j, 0] = cr
            carry[j, 1] = ci
            y_blocks.append(_mm(sre_ref[:, lanes], ct[0, j]) - _mm(sim_ref[:, lanes], ct[1, j]))
        yss = jnp.concatenate(y_blocks, axis=1) + mv_ref[ROW_SSM_D:ROW_SSM_D + 1, 0:512] * u_ssm
        yss_ref[...] = yss.astype(SAVE_DTYPE)
        vg = _mm(_gelu(yss), w_glu[...]) + mv_ref[ROW_B_GLU:ROW_B_GLU + 1, :]
        vg_ref[...] = vg.astype(SAVE_DTYPE)
        y_ssm = _mm(vg[:, 0:512] * _sigmoid(vg[:, 512:1024]), w_su[...])
        ys_ref[...] = y_ssm.astype(SAVE_DTYPE)

        merged = _sigmoid(gl_pool) * y_pool + _sigmoid(gl_ssm) * y_ssm
        mo = _mm(merged, w_o[...])
        mo_ref[...] = mo.astype(SAVE_DTYPE)
        x2_ref[...] = xv + _row(prm_ref, 5) * mo

    def tok(width):
        return pl.BlockSpec((tm, width), lambda i: (i, 0))

    hbm = _HBM
    widths = (D_MODEL, D_MODEL, IN_WIDTH, N_STATE, N_STATE, 512, 512, D_MODEL, 512, D_MODEL, D_MODEL)
    dtypes = (F32, SAVE_DTYPE, SAVE_DTYPE, F32, F32) + (SAVE_DTYPE,) * 6
    return _launch(
        body, "mixer_forward", grid=(nt,), semantics=("arbitrary",), carry=carry, steps=_grid_steps(nt),
        out_shape=[jax.ShapeDtypeStruct((T, w), dt) for w, dt in zip(widths, dtypes)],
        in_specs=[tok(D_MODEL), _resident(prm), hbm, hbm, hbm, hbm, hbm, _resident(pool_w), _resident(mvec),
                  _resident(srow), _resident(bb), _resident(ct)],
        out_specs=[tok(w) for w in widths],
        operands=(x, prm, w_in_s, w_pu_s, w_glu_s, w_su_s, w_out, pool_w, mvec, srow, bb, ct),
        scratch=[
            pltpu.VMEM((D_MODEL, IN_WIDTH), MXU_DTYPE), pltpu.VMEM((512, D_MODEL), MXU_DTYPE),
            pltpu.VMEM((512, D_MODEL), MXU_DTYPE), pltpu.VMEM((512, D_MODEL), MXU_DTYPE),
            pltpu.VMEM((D_MODEL, D_MODEL), MXU_DTYPE),
            pltpu.VMEM((SSM_BLOCKS, 8, SCAN_ROWS, SSM_BLOCK_STATE), F32),
            pltpu.VMEM((SSM_BLOCKS, 2, SCAN_ROWS, SSM_BLOCK_STATE), F32),
            pltpu.VMEM((POOL_HALO, POOL_WIDTH), F32),
            pltpu.VMEM((2, tm, SSM_BLOCK_STATE), F32),
            pltpu.SemaphoreType.DMA((33,)),
        ])


def _mixer_backward(d2, prm, saved, w_pu_s, w_glu_s, w_su_s, w_out, pool_w, mvec, srow, bb, ct, carry=None):
    z, s_re, s_im, zp, q, y_pool, yss, vg, y_ssm = saved
    T = d2.shape[0]
    tm = min(T, TM_MIX_BWD)
    nt = T // tm
    n_tiles = tm // SCAN_ROWS

    def body(d_ref, prm_ref, z_ref, sre_ref, sim_ref, zp_ref, q_ref, yp_ref, yss_ref, vg_ref, ys_ref,
             w_pu_h, w_glu_h, w_su_h, w_out_h, pw_ref, mv_ref, srow_ref, bb, ct,
             dz_ref, dwo_h, dwpu_h, dwglu_h, dwsu_h, dpw_h, dbb_h, dct_h, vsum_h, da_h,
             w_pu, w_glu, w_su, w_o, pwb, coef, carry, hist, dre, lam,
             a_wo, a_wpu, a_wglu, a_wsu, a_pw, a_bb, a_ct, a_vs, a_da, st_wo, st_up, sems):
        i = pl.program_id(0)
        tile = nt - 1 - i

        @pl.when(i == 0)
        def _():
            cps = (_load_stack(w_pu_h, w_pu, sems, 0) + _load_stack(w_glu_h, w_glu, sems, 8)
                   + _load_stack(w_su_h, w_su, sems, 16))
            cps.append(pltpu.make_async_copy(w_out_h, w_o, sems.at[24]))
            cps[-1].start()
            pwb[...] = pw_ref[...].astype(MXU_DTYPE)
            for j in range(SSM_BLOCKS):
                lanes = slice(j * SSM_BLOCK_STATE, (j + 1) * SSM_BLOCK_STATE)
                for k, t in enumerate(_scan_coefficients(srow_ref[0:1, lanes], srow_ref[1:2, lanes], True)):
                    coef[j, k] = t
            for acc in (carry, hist, a_wo, a_wpu, a_wglu, a_wsu, a_pw, a_bb, a_ct, a_vs, a_da):
                acc[...] = jnp.zeros_like(acc)
            for cp in cps:
                cp.wait()

        dv = d_ref[...]
        zt = z_ref[...].astype(F32)
        u_ssm, gl_pool, gl_ssm = zt[:, 512:1024], zt[:, 1024:2048], zt[:, 2048:3072]
        y_p, y_s = yp_ref[...].astype(F32), ys_ref[...].astype(F32)
        sgp, sgs = _sigmoid(gl_pool), _sigmoid(gl_ssm)
        dmo = (_row(prm_ref, 5) * dv).astype(MXU_DTYPE)
        a_wo[...] += _mm_tn(sgp * y_p + sgs * y_s, dmo)
        dmerged = _mm_nt(dmo, w_o[...])
        dy_pool = dmerged * sgp
        dgl_pool = dmerged * y_p * (sgp * (1.0 - sgp))
        dy_ssm = dmerged * sgs
        dgl_ssm = dmerged * y_s * (sgs * (1.0 - sgs))

        scale = mv_ref[ROW_POOL_SCALE:ROW_POOL_SCALE + 1, 0:512]
        qv, zpv = q_ref[...].astype(F32), zp_ref[...]
        a_wpu[...] += _mm_tn(qv * scale, dy_pool)
        dp = _mm_nt(dy_pool, w_pu[...])
        dq = dp * scale
        a_vs[0:1, 0:512] += _colsum(dp * qv)
        a_vs[1:2, 0:512] += _colsum(dq)
        dzp_blocks = []
        for k in range(4):
            lanes = slice(k * 128, (k + 1) * 128)
            dzp_blocks.append(_mm_nt(dq[:, lanes], pwb[k]))
            a_pw[k] += _mm_tn(zpv[:, lanes], dq[:, lanes])
        dzp = jnp.concatenate(dzp_blocks, axis=1)
        t1 = (lax.broadcasted_iota(jnp.int32, (tm, 1), 0) + (tile * tm + 1)).astype(F32)
        gs = dzp / jnp.minimum(t1, _window_lanes())
        n_ext = tm + POOL_HALO
        ext = jnp.concatenate([gs, hist[...]], axis=0)
        v2 = ext + pltpu.roll(ext, n_ext - 1, 0)
        v4 = v2[:, 128:] + pltpu.roll(v2[:, 128:], n_ext - 2, 0)
        v8 = v4[:, 128:] + pltpu.roll(v4[:, 128:], n_ext - 4, 0)
        v16 = v8[:, 128:] + pltpu.roll(v8[:, 128:], n_ext - 8, 0)
        msum = jnp.concatenate([v2[:tm, :128], v4[:tm, :128], v8[:tm, :128], v16[:tm]], axis=1)
        hist[...] = gs[0:POOL_HALO, :]
        du_pool = msum - dzp

        vgv = vg_ref[...].astype(F32)
        val, gate = vgv[:, 0:512], vgv[:, 512:1024]
        sgg = _sigmoid(gate)
        a_wsu[...] += _mm_tn(val * sgg, dy_ssm)
        do = _mm_nt(dy_ssm, w_su[...])
        dvg = jnp.concatenate([do * sgg, do * val * (sgg * (1.0 - sgg))], axis=1)
        a_vs[3:4, :] += _colsum(dvg)
        yv = yss_ref[...].astype(F32)
        a_wglu[...] += _mm_tn(_gelu(yv), dvg)
        dyss = _mm_nt(dvg, w_glu[...]) * _gelu_grad(yv)
        a_vs[2:3, 0:512] += _colsum(dyss * u_ssm)
        du_blocks = []
        for j in range(SSM_BLOCKS):
            lanes = pl.ds(j * SSM_BLOCK_STATE, SSM_BLOCK_STATE)
            in_lanes = slice(j * 128, (j + 1) * 128)
            dyb = dyss[:, in_lanes].astype(MXU_DTYPE)
            ub = u_ssm[:, in_lanes].astype(MXU_DTYPE)
            dre[0] = _mm_nt(dyb, ct[0, j])
            dre[1] = -_mm_nt(dyb, ct[1, j])
            a_ct[0, j] += _mm_tn(sre_ref[:, lanes], dyb)
            a_ct[1, j] -= _mm_tn(sim_ref[:, lanes], dyb)
            a1r, a1i, a2r, a2i, a4r, a4i, pr, pi = [coef[j, k] for k in range(8)]
            rowi = lax.broadcasted_iota(jnp.int32, (SCAN_ROWS, SSM_BLOCK_STATE), 0)

            def step(tt, c, lanes=lanes, a1r=a1r, a1i=a1i, a2r=a2r, a2i=a2i, a4r=a4r, a4i=a4i, pr=pr, pi=pi, rowi=rowi):
                cr, ci, acc_r, acc_i = c
                rows = pl.ds(pl.multiple_of((n_tiles - 1 - tt) * SCAN_ROWS, SCAN_ROWS), SCAN_ROWS)
                xr, xi = dre[0, rows, :], dre[1, rows, :]
                for dstep, kr, ki in ((1, a1r, a1i), (2, a2r, a2i), (4, a4r, a4i)):
                    sr, si = pltpu.roll(xr, SCAN_ROWS - dstep, 0), pltpu.roll(xi, SCAN_ROWS - dstep, 0)
                    xr, xi = xr + kr * sr + ki * si, xi + kr * si - ki * sr
                xr, xi = xr + pr * cr + pi * ci, xi + pr * ci - pi * cr
                lam[0, rows, :] = xr
                lam[1, rows, :] = xi
                nr = jnp.where(rowi == SCAN_ROWS - 1, cr, pltpu.roll(xr, SCAN_ROWS - 1, 0))
                ni = jnp.where(rowi == SCAN_ROWS - 1, ci, pltpu.roll(xi, SCAN_ROWS - 1, 0))
                s_r, s_i = sre_ref[rows, lanes], sim_ref[rows, lanes]
                acc_r = acc_r + nr * s_r + ni * s_i
                acc_i = acc_i + ni * s_r - nr * s_i
                return (jnp.broadcast_to(xr[0:1, :], xr.shape), jnp.broadcast_to(xi[0:1, :], xi.shape), acc_r, acc_i)

            cr, ci, acc_r, acc_i = lax.fori_loop(0, n_tiles, step, (carry[j, 0], carry[j, 1], a_da[0, j], a_da[1, j]))
            carry[j, 0] = cr
            carry[j, 1] = ci
            a_da[0, j] = acc_r
            a_da[1, j] = acc_i
            lr_b, li_b = lam[0].astype(MXU_DTYPE), lam[1].astype(MXU_DTYPE)
            a_bb[0, j] += _mm_tn(ub, lr_b)
            a_bb[1, j] += _mm_tn(ub, li_b)
            du_blocks.append(_mm_nt(lr_b, bb[0, j]) + _mm_nt(li_b, bb[1, j]))
        du_ssm = jnp.concatenate(du_blocks, axis=1) + dyss * mv_ref[ROW_SSM_D:ROW_SSM_D + 1, 0:512]
        dz_ref[...] = jnp.concatenate([du_pool, du_ssm, dgl_pool, dgl_ssm], axis=1).astype(SAVE_DTYPE)

        @pl.when(i == nt - 1)
        def _():
            rows = D_MODEL // NDEV
            for k in range(NDEV):
                st_wo[k] = a_wo[k * rows:(k + 1) * rows, :].astype(WIRE_DTYPE)
                for a, acc in enumerate((a_wpu, a_wglu, a_wsu)):
                    st_up[a, k] = acc[:, k * 128:(k + 1) * 128].astype(WIRE_DTYPE)
            outs = ((st_wo, dwo_h), (st_up.at[0], dwpu_h), (st_up.at[1], dwglu_h), (st_up.at[2], dwsu_h),
                    (a_pw, dpw_h), (a_bb, dbb_h), (a_ct, dct_h), (a_vs, vsum_h), (a_da, da_h))
            cps = [pltpu.make_async_copy(src, dst, sems.at[k]) for k, (src, dst) in enumerate(outs)]
            for cp in cps:
                cp.start()
            for cp in cps:
                cp.wait()

    def tok(width):
        return pl.BlockSpec((tm, width), lambda i: (nt - 1 - i, 0))

    hbm = _HBM
    acc_shapes = [(D_MODEL, D_MODEL), (512, D_MODEL), (512, D_MODEL), (512, D_MODEL), (4, 128, 128),
                  (2, SSM_BLOCKS, 128, SSM_BLOCK_STATE), (2, SSM_BLOCKS, SSM_BLOCK_STATE, 128), (8, D_MODEL),
                  (2, SSM_BLOCKS, SCAN_ROWS, SSM_BLOCK_STATE)]
    stack_out = [jax.ShapeDtypeStruct((NDEV, D_MODEL // NDEV, D_MODEL), WIRE_DTYPE)] \
        + [jax.ShapeDtypeStruct((NDEV, 512, 128), WIRE_DTYPE)] * 3
    return _launch(
        body, "mixer_backward", grid=(nt,), semantics=("arbitrary",), carry=carry, steps=_grid_steps(nt),
        out_shape=[jax.ShapeDtypeStruct((T, IN_WIDTH), SAVE_DTYPE)] + stack_out
        + [jax.ShapeDtypeStruct(s, F32) for s in acc_shapes[4:]],
        in_specs=[tok(D_MODEL), _resident(prm), tok(IN_WIDTH), tok(N_STATE), tok(N_STATE), tok(512), tok(512),
                  tok(D_MODEL), tok(512), tok(D_MODEL), tok(D_MODEL), hbm, hbm, hbm, hbm, _resident(pool_w),
                  _resident(mvec), _resident(srow), _resident(bb), _resident(ct)],
        out_specs=[tok(IN_WIDTH)] + [hbm] * len(acc_shapes),
        operands=(d2, prm, z, s_re, s_im, zp, q, y_pool, yss, vg, y_ssm, w_pu_s, w_glu_s, w_su_s, w_out, pool_w, mvec,
                  srow, bb, ct),
        scratch=[
            pltpu.VMEM((512, D_MODEL), MXU_DTYPE), pltpu.VMEM((512, D_MODEL), MXU_DTYPE),
            pltpu.VMEM((512, D_MODEL), MXU_DTYPE), pltpu.VMEM((D_MODEL, D_MODEL), MXU_DTYPE),
            pltpu.VMEM((4, 128, 128), MXU_DTYPE),
            pltpu.VMEM((SSM_BLOCKS, 8, SCAN_ROWS, SSM_BLOCK_STATE), F32),
            pltpu.VMEM((SSM_BLOCKS, 2, SCAN_ROWS, SSM_BLOCK_STATE), F32),
            pltpu.VMEM((POOL_HALO, POOL_WIDTH), F32),
            pltpu.VMEM((2, tm, SSM_BLOCK_STATE), F32), pltpu.VMEM((2, tm, SSM_BLOCK_STATE), F32),
        ] + [pltpu.VMEM(s, F32) for s in acc_shapes]
        + [pltpu.VMEM((NDEV, D_MODEL // NDEV, D_MODEL), WIRE_DTYPE), pltpu.VMEM((3, NDEV, 512, 128), WIRE_DTYPE),
           pltpu.SemaphoreType.DMA((25,))])


def _mixer_in_backward(x, dz, d, mo, prm, w_in_s, carry=None):
    T = x.shape[0]
    tm = min(T, TM_MIX_IN)
    nt = T // tm
    cols = IN_WIDTH // NDEV

    def body(x_ref, dz_ref, d_ref, mo_ref, prm_ref, w_in_h, dx_ref, sums_ref, dw_ref, w_in, acc, sems):
        i = pl.program_id(0)

        @pl.when(i == 0)
        def _():
            cps = _load_stack(w_in_h, w_in, sems, 0)
            acc[...] = jnp.zeros_like(acc)
            for cp in cps:
                cp.wait()

        xv = x_ref[...]
        h, _, _ = _modulated(xv, prm_ref, 1, ROW_G_MIX)
        dzb = dz_ref[...].astype(MXU_DTYPE)
        acc[...] += _mm_tn(h, dzb)
        dx_ref[...], upd = _norm_backward_tile(_mm_nt(dzb, w_in[...]), xv, d_ref[...], mo_ref[...], prm_ref, 1,
                                               ROW_G_MIX, 1.0)

        @pl.when(i == 0)
        def _():
            sums_ref[...] = upd

        @pl.when(i > 0)
        def _():
            sums_ref[...] += upd

        @pl.when(i == nt - 1)
        def _():
            for k in range(NDEV):
                dw_ref[k] = acc[:, k * cols:(k + 1) * cols].astype(WIRE_DTYPE)

    tok = pl.BlockSpec((tm, D_MODEL), lambda i: (i, 0))
    return _launch(
        body, "mixer_in_backward", grid=(nt,), semantics=("arbitrary",), carry=carry, steps=_grid_steps(nt),
        out_shape=[jax.ShapeDtypeStruct((T, D_MODEL), F32), jax.ShapeDtypeStruct((8, D_MODEL), F32),
                   jax.ShapeDtypeStruct((NDEV, D_MODEL, cols), WIRE_DTYPE)],
        in_specs=[tok, pl.BlockSpec((tm, IN_WIDTH), lambda i: (i, 0)), tok, tok, _resident(prm), _HBM],
        out_specs=[tok, pl.BlockSpec((8, D_MODEL), lambda i: (0, 0)),
                   pl.BlockSpec((NDEV, D_MODEL, cols), lambda i: (0, 0, 0))],
        operands=(x, dz, d, mo, prm, w_in_s),
        scratch=[pltpu.VMEM((D_MODEL, IN_WIDTH), MXU_DTYPE), pltpu.VMEM((D_MODEL, IN_WIDTH), F32),
                 pltpu.SemaphoreType.DMA((8,))])


def _ssm_dense_backward(dbb, da, srow, b_dense):
    def body(dbb_ref, da_ref, srow_ref, bd_ref, db_ref, df_ref):
        df_re, df_im = [], []
        da_re = [_colsum(da_ref[0, j]) for j in range(SSM_BLOCKS)]
        da_im = [_colsum(da_ref[1, j]) for j in range(SSM_BLOCKS)]
        for j in range(SSM_BLOCKS):
            lanes = slice(j * SSM_BLOCK_STATE, (j + 1) * SSM_BLOCK_STATE)
            f_re, f_im = srow_ref[2:3, lanes], srow_ref[3:4, lanes]
            g_re, g_im = dbb_ref[0, j], dbb_ref[1, j]
            b_re, b_im = bd_ref[0, j], bd_ref[1, j]
            db_ref[0, j] = f_re * g_re + f_im * g_im
            db_ref[1, j] = f_re * g_im - f_im * g_re
            df_re.append(_colsum(g_re * b_re + g_im * b_im))
            df_im.append(_colsum(g_im * b_re - g_re * b_im))
        df_ref[...] = jnp.concatenate([jnp.concatenate(df_re, axis=1), jnp.concatenate(df_im, axis=1),
                                       jnp.concatenate(da_re, axis=1), jnp.concatenate(da_im, axis=1),
                                       jnp.zeros((4, N_STATE), F32)], axis=0)

    return pl.pallas_call(body, name="ssm_dense_backward",
                          out_shape=[jax.ShapeDtypeStruct(b_dense.shape, F32), jax.ShapeDtypeStruct((8, N_STATE), F32)],
                          compiler_params=pltpu.CompilerParams(vmem_limit_bytes=VMEM_LIMIT))(dbb, da, srow, b_dense)


def _adamw_update(w, g, m, v):
    m = ADAM_B1 * m + (1.0 - ADAM_B1) * g
    v = ADAM_B2 * v + (1.0 - ADAM_B2) * (g * g)
    m_hat = m / (1.0 - ADAM_B1 ** ADAM_STEP)
    v_hat = v / (1.0 - ADAM_B2 ** ADAM_STEP)
    delta = -ADAM_LR * (m_hat / (jnp.sqrt(v_hat) + ADAM_EPS) + ADAM_WD * w)
    return delta, m, v


def _adam_rows(shape):
    rows, cols = shape
    tr = rows
    while tr * cols * 4 > (1 << 20) and tr % 16 == 0:
        tr //= 2
    return tr


def _adam_sharded(w, m, v, land, order, name):
    R, C = w.shape
    tr = _adam_rows((R, C))

    def body(w_ref, m_ref, v_ref, land_ref, order_ref, g_ref, d_ref, mo_ref, vo_ref):
        g = land_ref[0].astype(F32)
        for b in range(1, NDEV):
            g = g + land_ref[b].astype(F32)
        g_ref[...] = g
        d_ref[...], mo_ref[...], vo_ref[...] = _adamw_update(w_ref[...], g, m_ref[...], v_ref[...])

    blk = pl.BlockSpec((tr, C), lambda i: (i, 0))
    return pl.pallas_call(
        body, name=name, grid=(R // tr,),
        out_shape=[jax.ShapeDtypeStruct((R, C), F32)] * 4,
        in_specs=[blk, blk, blk, pl.BlockSpec((NDEV, tr, C), lambda i: (0, i, 0)), _HBM],
        out_specs=[blk] * 4,
        compiler_params=_params("arbitrary"),
    )(w, m, v, land, order)


def _adam_ada(w, m, v, sc_all, dmod_cols):
    R, C = w.shape
    tr = 256

    def body(w_ref, m_ref, v_ref, sc_ref, dm_ref, g_ref, d_ref, mo_ref, vo_ref):
        g = _mm_tn(sc_ref[...], dm_ref[...])
        g_ref[...] = g
        d_ref[...], mo_ref[...], vo_ref[...] = _adamw_update(w_ref[...], g, m_ref[...], v_ref[...])

    blk = pl.BlockSpec((tr, C), lambda i: (i, 0))
    return pl.pallas_call(
        body, name="adam_w_ada", grid=(R // tr,),
        out_shape=[jax.ShapeDtypeStruct((R, C), F32)] * 4,
        in_specs=[blk, blk, blk, pl.BlockSpec((8, tr), lambda i: (0, i)), pl.BlockSpec((8, C), lambda i: (0, 0))],
        out_specs=[blk] * 4,
        compiler_params=_params("arbitrary"),
    )(w, m, v, sc_all, dmod_cols)


def _adam_small(w, g, m, v, name):
    def body(w_ref, g_ref, m_ref, v_ref, d_ref, mo_ref, vo_ref):
        d_ref[...], mo_ref[...], vo_ref[...] = _adamw_update(w_ref[...], g_ref[...], m_ref[...], v_ref[...])

    return pl.pallas_call(body, name=name, out_shape=[jax.ShapeDtypeStruct(w.shape, F32)] * 3,
                          compiler_params=pltpu.CompilerParams(vmem_limit_bytes=VMEM_LIMIT))(w, g, m, v)


def _block_diag_in(b):
    bt = jnp.transpose(b, (0, 2, 1)).reshape(SSM_BLOCKS, 8, SSM_GROUP, SSM_STATE)
    eye = jnp.eye(8, dtype=bool)[None, :, None, :, None]
    return jnp.where(eye, bt[:, :, :, None, :], 0.0).reshape(SSM_BLOCKS, 128, SSM_BLOCK_STATE)


def _block_diag_out(c):
    ct = jnp.transpose(c, (0, 2, 1)).reshape(SSM_BLOCKS, 8, SSM_STATE, SSM_GROUP)
    eye = jnp.eye(8, dtype=bool)[None, :, None, :, None]
    return jnp.where(eye, ct[:, :, :, None, :], 0.0).reshape(SSM_BLOCKS, SSM_BLOCK_STATE, 128)


def _diag_blocks(dense, rows, cols):
    d5 = dense.reshape(SSM_BLOCKS, 8, rows, 8, cols)
    return jnp.stack([d5[:, a, :, a, :] for a in range(8)], axis=1).reshape(N_SSM_GROUPS, rows, cols)


def _pack_small(ada_vec, parts, params, tail=None):
    rest_rows, rows = _pack_rows(params, ada_vec is not None)
    rest = jnp.concatenate([parts[n].reshape(-1) for n, _ in params])
    rest = jnp.pad(rest, (0, NDEV * rest_rows * 128 - rest.shape[0])).reshape(NDEV, rest_rows, 128)
    head = [] if ada_vec is None else [ada_vec.reshape(NDEV, ADA_ROWS, 128)]
    pad = rows - rest_rows - (0 if ada_vec is None else ADA_ROWS)
    fill = jnp.zeros((NDEV, pad, 128), F32) if tail is None else jnp.pad(tail[None], ((0, NDEV - 1), (0, pad - 1), (0, 127)))
    return jnp.concatenate(head + [rest] + ([fill] if pad else []), axis=1)


def _unpack_small(pack, shapes, params, with_ada):
    rest_rows, _ = _pack_rows(params, with_ada)
    first = ADA_ROWS if with_ada else 0
    ada_vec = pack[:, :first].reshape(-1) if with_ada else None
    rest = pack[:, first:first + rest_rows].reshape(-1)
    out, off = {}, 0
    for n, size in params:
        out[n] = rest[off:off + size].reshape(shapes[n])
        off += size
    return ada_vec, out


WEIGHT_ORDER = ('w_ada', 'b_ada', 'g_ffn1', 'w_ffn1_in', 'w_ffn1_out', 'g_mix', 'w_in', 'pool_w', 'pool_b',
                'pool_scale', 'w_pool_up', 'ssm_lam_re_log', 'ssm_lam_im', 'ssm_log_dt', 'ssm_b_re', 'ssm_b_im',
                'ssm_c_re', 'ssm_c_im', 'ssm_d', 'w_glu', 'b_glu', 'w_ssm_up', 'w_out', 'g_ffn2', 'w_ffn2_in',
                'w_ffn2_out', 'g_final')
GATHERED = ('w_ffn1_in', 'w_ffn1_out', 'w_in', 'w_pool_up', 'w_glu', 'w_ssm_up', 'w_out', 'w_ffn2_in', 'w_ffn2_out')
TRANSPOSED = ('w_ffn1_in', 'w_ffn2_in')
STATE_MINOR = ('ssm_b_re', 'ssm_b_im')


def kernel(x, c, w_ada, b_ada, g_ffn1, w_ffn1_in, w_ffn1_out, g_mix, w_in, pool_w, pool_b, pool_scale, w_pool_up, ssm_lam_re_log, ssm_lam_im, ssm_log_dt, ssm_b_re, ssm_b_im, ssm_c_re, ssm_c_im, ssm_d, w_glu, b_glu, w_ssm_up, w_out, g_ffn2, w_ffn2_in, w_ffn2_out, g_final, loss_target, m_w_ada, m_b_ada, m_g_ffn1, m_w_ffn1_in, m_w_ffn1_out, m_g_mix, m_w_in, m_pool_w, m_pool_b, m_pool_scale, m_w_pool_up, m_ssm_lam_re_log, m_ssm_lam_im, m_ssm_log_dt, m_ssm_b_re, m_ssm_b_im, m_ssm_c_re, m_ssm_c_im, m_ssm_d, m_w_glu, m_b_glu, m_w_ssm_up, m_w_out, m_g_ffn2, m_w_ffn2_in, m_w_ffn2_out, m_g_final, v_w_ada, v_b_ada, v_g_ffn1, v_w_ffn1_in, v_w_ffn1_out, v_g_mix, v_w_in, v_pool_w, v_pool_b, v_pool_scale, v_w_pool_up, v_ssm_lam_re_log, v_ssm_lam_im, v_ssm_log_dt, v_ssm_b_re, v_ssm_b_im, v_ssm_c_re, v_ssm_c_im, v_ssm_d, v_w_glu, v_b_glu, v_w_ssm_up, v_w_out, v_g_ffn2, v_w_ffn2_in, v_w_ffn2_out, v_g_final):
    args = locals()
    W = {n: args[n] for n in WEIGHT_ORDER}
    M = {n: args["m_" + n] for n in WEIGHT_ORDER}
    V = {n: args["v_" + n] for n in WEIGHT_ORDER}
    shapes = {n: W[n].shape for n in WEIGHT_ORDER}
    xt, tgt = x[0], loss_target[0]

    def local(tree, n):
        return jnp.swapaxes(tree[n][0], 0, 1) if n in TRANSPOSED else tree[n][0]

    def as_output(n, a):
        return (jnp.swapaxes(a, 0, 1) if n in TRANSPOSED else a)[None]

    shard = dict(zip(GATHERED, _cast_shards([local(W, n) for n in GATHERED])))
    stacks = {}

    def gather(names):
        return _Gather([shard[n] for n in names])

    def gathered(names, results):
        stacks.update(zip(names, results))

    ffn1_w, ffn2_w = ('w_ffn1_in', 'w_ffn1_out'), ('w_ffn2_in', 'w_ffn2_out')
    mix_w = ('w_in', 'w_pool_up', 'w_glu', 'w_ssm_up', 'w_out')
    mod_cols, sc_all, *res = _ada_forward(c, W['w_ada'][0], b_ada.reshape(NDEV, -1), gather(ffn1_w[:1]))
    gathered(ffn1_w[:1], res)
    win1 = stacks['w_ffn1_in'].reshape(2, 4, FF_SHARD, D_MODEL)
    prm = jnp.concatenate([mod_cols.reshape(9, D_MODEL), g_ffn1, g_mix, g_ffn2, g_final[None], jnp.zeros((3, D_MODEL), F32)], axis=0)
    pad512 = jnp.zeros((1, D_MODEL - 512), F32)
    mvec = jnp.concatenate([jnp.concatenate([pool_b, pad512], axis=1), jnp.concatenate([pool_scale, pad512], axis=1),
                            jnp.concatenate([ssm_d, pad512], axis=1), b_glu, jnp.zeros((4, D_MODEL), F32)], axis=0)
    log_dt_col = ssm_log_dt[0][:, None]
    coeffs = _ssm_params_forward(ssm_lam_re_log[0], ssm_lam_im[0], log_dt_col)
    srow = jnp.stack([t.reshape(N_STATE) for t in coeffs], axis=0)
    b_dense = jnp.stack([_block_diag_in(ssm_b_re[0]), _block_diag_in(ssm_b_im[0])], axis=0)
    c_dense = jnp.stack([_block_diag_out(ssm_c_re[0]), _block_diag_out(ssm_c_im[0])], axis=0)
    bb, ct = _ssm_dense_forward(srow, b_dense, c_dense)
    pw = pool_w[0]

    next_w = ffn1_w[1:] + mix_w[:1]
    ab1, s1, *res = _ffn_hidden(xt, prm, win1, 0, ROW_G_FFN1, "ffn1_hidden", gather(next_w))
    gathered(next_w, res)
    wout1 = stacks['w_ffn1_out'].reshape(4, FF_SHARD, D_MODEL)
    x1, f1, *res = _ffn_out(xt, s1, prm, wout1, 0, "ffn1_out", gather(mix_w[1:]))
    gathered(mix_w[1:], res)
    w_out_full = stacks['w_out'].reshape(D_MODEL, D_MODEL)
    res = _mixer_forward(x1, prm, stacks['w_in'], stacks['w_pool_up'], stacks['w_glu'], stacks['w_ssm_up'],
                         w_out_full, pw, mvec, srow, bb, ct, gather(ffn2_w))
    x2, mo, saved = res[0], res[1], res[2:11]
    gathered(ffn2_w, res[11:])
    win2 = stacks['w_ffn2_in'].reshape(2, 4, FF_SHARD, D_MODEL)
    wout2 = stacks['w_ffn2_out'].reshape(4, FF_SHARD, D_MODEL)
    d3, fin, f3, ab3 = _ffn_forward_loss(x2, tgt, prm, win2, wout2, 2, ROW_G_FFN2, "ffn2_forward_loss")

    lands = {}

    def scatter(grads):
        names = list(grads)
        return _Scatter([grads[n][0] for n in names], [grads[n][1] for n in names], [local(W, n).shape for n in names])

    def scattered(grads, results):
        lands.update(zip(grads, results))

    parts3, dab3, dwout2 = _ffn_backward(d3, ab3, prm, win2, wout2, 2, "ffn2_backward")
    dwin2, = _ffn_dwin(x2, dab3, prm, 2, ROW_G_FFN2, "ffn2_dwin")
    d2, sums3 = _norm_backward(parts3, x2, d3, f3, prm, 2, ROW_G_FFN2, 0.5, "ffn2_norm_backward")
    g_ffn2_w = {'w_ffn2_in': (dwin2, _halves), 'w_ffn2_out': (dwout2.reshape(NDEV, -1, D_MODEL), _stacked)}
    res = _mixer_backward(d2, prm, saved, stacks['w_pool_up'], stacks['w_glu'], stacks['w_ssm_up'], w_out_full, pw, mvec,
                          srow, bb, ct, scatter(g_ffn2_w))
    dz, dwo, dwpu, dwglu, dwsu, dpw, dbb, dct, vsum, da = res[:10]
    scattered(g_ffn2_w, res[10:])
    g_mix_up = {'w_pool_up': (dwpu, _stacked), 'w_glu': (dwglu, _stacked), 'w_ssm_up': (dwsu, _stacked),
                'w_out': (dwo, _stacked)}
    d1, sums2, dwin_mix, *res = _mixer_in_backward(x1, dz, d2, mo, prm, stacks['w_in'], scatter(g_mix_up))
    scattered(g_mix_up, res)
    g_mix_w = {'w_in': (dwin_mix, _stacked)}
    db_dense, df_rows = _ssm_dense_backward(dbb, da, srow, b_dense)
    cot = [df_rows[r].reshape(N_SSM_GROUPS, SSM_STATE) for r in (2, 3, 0, 1)]
    d_lrl, d_li, d_ldt = _ssm_params_backward(ssm_lam_re_log[0], ssm_lam_im[0], log_dt_col, cot)
    small_grads = {
        'g_mix': sums2[0], 'g_ffn2': sums3[0], 'g_final': fin[0], 'pool_w': dpw,
        'pool_b': vsum[1, :512], 'pool_scale': vsum[0, :512], 'ssm_lam_re_log': d_lrl, 'ssm_lam_im': d_li,
        'ssm_log_dt': d_ldt, 'ssm_b_re': _diag_blocks(db_dense[0], SSM_GROUP, SSM_STATE),
        'ssm_b_im': _diag_blocks(db_dense[1], SSM_GROUP, SSM_STATE),
        'ssm_c_re': jnp.transpose(_diag_blocks(dct[0], SSM_STATE, SSM_GROUP), (0, 2, 1)),
        'ssm_c_im': jnp.transpose(_diag_blocks(dct[1], SSM_STATE, SSM_GROUP), (0, 2, 1)),
        'ssm_d': vsum[2, :512], 'b_glu': vsum[3],
    }
    early = _SmallAllReduce(_pack_small(None, small_grads, SMALL_EARLY, fin[1:2, 0:1]))
    parts1, dab1, dwout1, total_early, *res = _ffn_backward(d1, ab1, prm, win1, wout1, 0, "ffn1_backward",
                                                            _Carried(early, scatter(g_mix_w)))
    scattered(g_mix_w, res)
    loss = total_early[0, _pack_rows(SMALL_EARLY, False)[0], 0]
    g_wout1 = {'w_ffn1_out': (dwout1.reshape(NDEV, -1, D_MODEL), _stacked)}
    dwin1, *res = _ffn_dwin(xt, dab1, prm, 0, ROW_G_FFN1, "ffn1_dwin", scatter(g_wout1))
    scattered(g_wout1, res)

    last_w, last_views = ffn1_w[:1], [_halves]
    send_sems, recv_sems, last_src, last_land, token = _scatter_start(
        [dwin1], last_views, [local(W, n).shape for n in last_w], [total_early])
    after_start = token[0:1, 0:1]
    d0, sums1 = _norm_backward(parts1, xt, d1, f1, prm + after_start, 0, ROW_G_FFN1, 0.5, "ffn1_norm_backward")

    grad, delta, new_m, new_v = {}, {}, {}, {}

    def adam_sharded(n):
        res = _adam_sharded(local(W, n), local(M, n), local(V, n), lands[n], token, "adam_" + n)
        grad[n], delta[n], new_m[n], new_v[n] = [as_output(n, r) for r in res]
        return res[3]

    def adam_small(params, ada, total, name, order):
        rows = _pack_rows(params, ada)[1]
        views = [{n: jnp.transpose(t[n][0], (0, 2, 1)) if n in STATE_MINOR else t[n] for n, _ in params} for t in (W, M, V)]
        packs = [(_pack_small(t['b_ada'].reshape(-1) if ada else None, v, params) + order).reshape(NDEV * rows, 128)
                 for t, v in zip((W, M, V), views)]
        res = _adam_small(packs[0], total.reshape(NDEV * rows, 128), packs[1], packs[2], name)
        view_shapes = {n: (N_SSM_GROUPS, SSM_GROUP, SSM_STATE) if n in STATE_MINOR else shapes[n] for n, _ in params}
        for dst, packed in zip((grad, delta, new_m, new_v), (total, *res)):
            ada_vec, rest = _unpack_small(packed.reshape(NDEV, rows, 128), view_shapes, params, ada)
            dst.update({n: jnp.transpose(a, (0, 2, 1))[None] if n in STATE_MINOR else a for n, a in rest.items()})
            if ada:
                dst['b_ada'] = ada_vec.reshape(shapes['b_ada'])
        return res[2]

    done = [d0] + [adam_sharded(n) for n in GATHERED if n not in last_w]
    done.append(adam_small(SMALL_EARLY, False, total_early, "adam_small_early", after_start))
    lands.update(zip(last_w, _scatter_wait(send_sems, recv_sems, last_src, last_land, last_views, done)))

    dmod = jnp.concatenate([sums1[1:4], sums2[1:4], sums3[1:4]], axis=0).reshape(-1)
    total_late, landed = _allreduce_small(_pack_small(dmod, {'g_ffn1': sums1[0]}, SMALL_LATE), lands[last_w[0]],
                                          "allreduce_late")
    dmod_cols = landed[:, :ADA_ROWS].reshape(NDEV, ADA_ROWS * 128)
    res = _adam_ada(W['w_ada'][0], M['w_ada'][0], V['w_ada'][0], sc_all, dmod_cols)
    grad['w_ada'], delta['w_ada'], new_m['w_ada'], new_v['w_ada'] = [r[None] for r in res]
    adam_small(SMALL_LATE, True, total_late, "adam_small_late", 0.0)
    for n in last_w:
        adam_sharded(n)

    return (loss, d0[None], *[grad[n] for n in WEIGHT_ORDER], *[delta[n] for n in WEIGHT_ORDER],
            *[new_m[n] for n in WEIGHT_ORDER], *[new_v[n] for n in WEIGHT_ORDER])
```

```python
import jax
import jax.numpy as jnp
from jax import lax
from jax.experimental import pallas as pl
from jax.experimental.pallas import tpu as pltpu

F32 = jnp.float32
MXU_DTYPE = jnp.bfloat16
WIRE_DTYPE = jnp.bfloat16
SAVE_DTYPE = jnp.bfloat16

NDEV = 8
D_MODEL = 1024
D_FF = 2816
FF_SHARD = 2 * D_FF // NDEV
POOL_WIDTH = 512
POOL_GROUP = 128
SSM_WIDTH = 512
SSM_STATE = 64
SSM_GROUP = 16
N_SSM_GROUPS = SSM_WIDTH // SSM_GROUP
SSM_BLOCKS = 4
SSM_BLOCK_STATE = 512
N_STATE = 2048
IN_WIDTH = 3072
EPS = 1e-6
ADAM_LR = 0.001
ADAM_B1 = 0.9
ADAM_B2 = 0.999
ADAM_EPS = 1e-08
ADAM_WD = 0.01
ADAM_STEP = 10

TM_FFN = 512
TM_FFN_BWD = 1024
TM_DWIN = 2048
TM_MIX_IN = 512
FFN_BWD_CHUNK = 256
TM_MIX = 256
TM_MIX_BWD = 256
TM_EW = 512
SCAN_ROWS = 8
POOL_HALO = 16
VMEM_LIMIT = 60 * 1024 * 1024

ROW_G_FFN1, ROW_G_MIX, ROW_G_FFN2, ROW_G_FINAL = 9, 10, 11, 12
ROW_POOL_B, ROW_POOL_SCALE, ROW_SSM_D, ROW_B_GLU = 0, 1, 2, 3

SMALL_EARLY = (
    ("g_mix", 1024), ("g_ffn2", 1024), ("g_final", 1024), ("pool_w", 65536),
    ("pool_b", 512), ("pool_scale", 512), ("ssm_lam_re_log", 2048), ("ssm_lam_im", 2048),
    ("ssm_log_dt", 32), ("ssm_b_re", 32768), ("ssm_b_im", 32768), ("ssm_c_re", 32768),
    ("ssm_c_im", 32768), ("ssm_d", 512), ("b_glu", 1024),
)
SMALL_LATE = (("g_ffn1", 1024),)
ADA_ROWS = 9
MESH = pl.DeviceIdType.MESH


def _pack_rows(params, with_ada):
    rest = -(-sum(n for _, n in params) // (NDEV * 128))
    return rest, -(-(rest + (ADA_ROWS if with_ada else 0)) // 8) * 8


def _mm(a, b):
    return jnp.dot(a.astype(MXU_DTYPE), b.astype(MXU_DTYPE), preferred_element_type=F32)


def _mm_nt(a, b):
    return lax.dot_general(a.astype(MXU_DTYPE), b.astype(MXU_DTYPE), (((1,), (1,)), ((), ())),
                           preferred_element_type=F32)


def _mm_tn(a, b):
    return lax.dot_general(a.astype(MXU_DTYPE), b.astype(MXU_DTYPE), (((0,), (0,)), ((), ())),
                           preferred_element_type=F32)


def _rms_scale(x):
    return lax.rsqrt(jnp.mean(x * x, axis=-1, keepdims=True) + EPS)


def _sigmoid(x):
    return jax.nn.sigmoid(x)


def _colsum(x):
    return jnp.sum(x, axis=0, keepdims=True)


def _row(ref, r):
    return ref[r:r + 1, :]


def _params(*sem):
    return pltpu.CompilerParams(dimension_semantics=sem, vmem_limit_bytes=VMEM_LIMIT)


def _resident(a):
    return pl.BlockSpec(a.shape, lambda *_: (0,) * a.ndim, pipeline_mode=pl.Buffered(1))


def _me():
    return lax.axis_index("x"), lax.axis_index("y"), lax.axis_index("c")


def _peer(rel):
    x, y, c = _me()
    px = 1 - x if rel & 4 else x
    py = 1 - y if rel & 2 else y
    pc = 1 - c if rel & 1 else c
    return (px, py, pc), 4 * px + 2 * py + pc


_HBM = pl.BlockSpec(memory_space=pl.ANY)
_HBM_ONLY = pl.BlockSpec(memory_space=pltpu.HBM)


def _stacked(ref, p):
    return ref.at[p]


def _halves(ref, p):
    return ref.at[p // 4, p % 4]


class _Gather:
    def __init__(self, shards):
        self.operands = list(shards)
        self.n = len(shards)
        self.out_shape = [jax.ShapeDtypeStruct((NDEV,) + s.shape, s.dtype) for s in shards]
        self.scratch = [pltpu.SemaphoreType.DMA((7 * self.n,)), pltpu.SemaphoreType.DMA((7 * self.n,)),
                        pltpu.SemaphoreType.DMA((self.n,))]

    def plan(self, srcs, outs, sems):
        send_sems, recv_sems, local_sems = sems
        n = self.n
        x, y, c = _me()
        me = 4 * x + 2 * y + c
        here, sibling = (x, y, c), (x, y, 1 - c)
        chips = [(1 - x, y), (x, 1 - y), (1 - x, 1 - y)]

        def blk(px, py, pc):
            return 4 * px + 2 * py + pc

        def copy(a, k, block, to, src=None):
            return pltpu.make_async_remote_copy(
                src_ref=outs[a].at[block] if src is None else src, dst_ref=outs[a].at[block],
                send_sem=send_sems.at[7 * a + k], recv_sem=recv_sems.at[7 * a + k], device_id=to, device_id_type=MESH)

        def mine(a):
            return pltpu.make_async_copy(srcs[a], outs[a].at[me], local_sems.at[a])

        def first(a):
            return [copy(a, 0, me, sibling, src=srcs[a])] + [copy(a, 1 + j, me, (*chip, c), src=srcs[a])
                                                              for j, chip in enumerate(chips)]

        def start():
            for a in range(n):
                mine(a).start()
                for cp in first(a):
                    cp.start()

        def forward():
            for a in range(n):
                for j, chip in enumerate(chips):
                    copy(a, 1 + j, blk(*chip, c), here).wait_recv()
                    copy(a, 4 + j, blk(*chip, c), sibling).start()

        def finish():
            for a in range(n):
                copy(a, 0, blk(x, y, 1 - c), here).wait_recv()
                for j, chip in enumerate(chips):
                    copy(a, 4 + j, blk(*chip, 1 - c), here).wait_recv()
            for a in range(n):
                mine(a).wait()
                for cp in first(a):
                    cp.wait_send()
                for j, chip in enumerate(chips):
                    copy(a, 4 + j, blk(*chip, c), sibling).wait_send()

        return start, forward, finish


class _Scatter:
    def __init__(self, arrays, views, shard_shapes):
        self.operands = list(arrays)
        self.views = list(views)
        self.n = len(arrays)
        self.out_shape = [jax.ShapeDtypeStruct((NDEV,) + tuple(s), a.dtype) for s, a in zip(shard_shapes, arrays)]
        self.scratch = [pltpu.SemaphoreType.DMA((7 * self.n,)), pltpu.SemaphoreType.DMA((7 * self.n,)),
                        pltpu.SemaphoreType.DMA((self.n,))]

    def plan(self, srcs, outs, sems):
        send_sems, recv_sems, local_sems = sems
        n, views = self.n, self.views
        x, y, c = _me()
        me = 4 * x + 2 * y + c

        def mine(a):
            return pltpu.make_async_copy(views[a](srcs[a], me), outs[a].at[me], local_sems.at[a])

        def copy(a, rel, sending):
            to, p = _peer(rel)
            return pltpu.make_async_remote_copy(
                src_ref=views[a](srcs[a], p), dst_ref=outs[a].at[me if sending else p],
                send_sem=send_sems.at[7 * a + rel - 1], recv_sem=recv_sems.at[7 * a + rel - 1],
                device_id=to if sending else (x, y, c), device_id_type=MESH)

        def start():
            for a in range(n):
                mine(a).start()
            for rel in range(1, 8):
                for a in range(n):
                    copy(a, rel, True).start()

        def forward():
            pass

        def finish():
            for rel in range(1, 8):
                for a in range(n):
                    copy(a, rel, False).wait_recv()
            for rel in range(1, 8):
                for a in range(n):
                    copy(a, rel, True).wait_send()
            for a in range(n):
                mine(a).wait()

        return start, forward, finish


class _SmallAllReduce:
    def __init__(self, pack):
        rows = pack.shape[1]
        self.operands = [pack]
        self.n = 1
        self.out_shape = [jax.ShapeDtypeStruct(pack.shape, F32)]
        self.scratch = [pltpu.VMEM(pack.shape, F32), pltpu.VMEM((rows, 128), F32)] \
            + [pltpu.SemaphoreType.DMA((7,))] * 4 + [pltpu.SemaphoreType.DMA((2,))]

    def plan(self, srcs, outs, scratch):
        pack, total = srcs[0], outs[0]
        land, mine, send1, recv1, send2, recv2, local = scratch
        x, y, c = _me()
        me = 4 * x + 2 * y + c

        def slab(rel, sending):
            to, p = _peer(rel)
            return pltpu.make_async_remote_copy(
                src_ref=pack.at[p], dst_ref=land.at[me if sending else p], send_sem=send1.at[rel - 1],
                recv_sem=recv1.at[rel - 1], device_id=to if sending else (x, y, c), device_id_type=MESH)

        def summed(rel, sending):
            to, p = _peer(rel)
            return pltpu.make_async_remote_copy(
                src_ref=mine, dst_ref=total.at[me if sending else p], send_sem=send2.at[rel - 1],
                recv_sem=recv2.at[rel - 1], device_id=to if sending else (x, y, c), device_id_type=MESH)

        own_slab = pltpu.make_async_copy(pack.at[me], land.at[me], local.at[0])
        own_sum = pltpu.make_async_copy(mine, total.at[me], local.at[1])

        def start():
            own_slab.start()
            for rel in range(1, 8):
                slab(rel, True).start()

        def forward():
            own_slab.wait()
            for rel in range(1, 8):
                slab(rel, False).wait_recv()
            acc = land[0]
            for b in range(1, NDEV):
                acc = acc + land[b]
            mine[...] = acc
            own_sum.start()
            for rel in range(1, 8):
                summed(rel, True).start()

        def finish():
            for rel in range(1, 8):
                summed(rel, False).wait_recv()
            for rel in range(1, 8):
                slab(rel, True).wait_send()
                summed(rel, True).wait_send()
            own_sum.wait()

        return start, forward, finish


class _Carried:
    def __init__(self, *parts):
        self.parts = parts
        self.operands = [o for p in parts for o in p.operands]
        self.n = len(self.operands)
        self.out_shape = [s for p in parts for s in p.out_shape]
        self.scratch = [s for p in parts for s in p.scratch]

    def plan(self, srcs, outs, scratch):
        plans, a, b = [], 0, 0
        for p in self.parts:
            plans.append(p.plan(srcs[a:a + p.n], outs[a:a + p.n], scratch[b:b + len(p.scratch)]))
            a, b = a + p.n, b + len(p.scratch)

        def every(k):
            def run():
                for plan in plans:
                    plan[k]()
            return run

        return every(0), every(1), every(2)


def _launch(body, name, out_shape, in_specs, out_specs, operands, scratch=(), grid=None, semantics=None,
            carry=None, steps=None):
    out_shape, in_specs, out_specs = list(out_shape), list(in_specs), list(out_specs)
    operands, scratch = list(operands), list(scratch)
    n_in, n_out, n_scr = len(in_specs), len(out_shape), len(scratch)
    kernel_body = body
    if carry is not None:
        k = carry.n

        def kernel_body(*refs):
            ins, cin = refs[:n_in], refs[n_in:n_in + k]
            outs, cout = refs[n_in + k:n_in + k + n_out], refs[n_in + k + n_out:n_in + 2 * k + n_out]
            rest = refs[n_in + 2 * k + n_out:]
            scr, csem = rest[:n_scr], rest[n_scr:]
            start, forward, finish = carry.plan(cin, cout, csem)
            if steps is None:
                start()
                body(*ins, *outs, *scr)
                forward()
                finish()
            else:
                pl.when(steps()[0])(start)
                pl.when(steps()[1])(forward)
                body(*ins, *outs, *scr)
                pl.when(steps()[2])(finish)

        in_specs += [_HBM] * k
        out_shape += carry.out_shape
        out_specs += [_HBM] * k
        operands += carry.operands
        scratch += carry.scratch
    kwargs = {} if grid is None else {"grid": grid}
    params = pltpu.CompilerParams(vmem_limit_bytes=VMEM_LIMIT) if semantics is None else _params(*semantics)
    return pl.pallas_call(kernel_body, name=name, out_shape=out_shape, in_specs=in_specs, out_specs=out_specs,
                          scratch_shapes=scratch, compiler_params=params, **kwargs)(*operands)


def _grid_steps(nt):
    def steps():
        i = pl.program_id(0)
        return i == 0, i == nt - 1, i == nt - 1
    return steps


def _cast_shards(shards):
    n = len(shards)

    def body(*refs):
        for a in range(n):
            refs[n + a][...] = refs[a][...].astype(WIRE_DTYPE)

    return pl.pallas_call(body, name="cast_shards",
                          out_shape=[jax.ShapeDtypeStruct(s.shape, WIRE_DTYPE) for s in shards],
                          compiler_params=pltpu.CompilerParams(vmem_limit_bytes=VMEM_LIMIT))(*shards)


_SEM = pl.BlockSpec(memory_space=pltpu.SEMAPHORE)
_DATAFLOW = pltpu.SideEffectType.DATAFLOW_SIDE_EFFECTING


def _split_copy(arrays, views, landing, send_sems, recv_sems, a, rel):
    to, p = _peer(rel)
    x, y, c = _me()
    return pltpu.make_async_remote_copy(
        src_ref=views[a](arrays[a], p), dst_ref=landing[a].at[4 * x + 2 * y + c],
        send_sem=send_sems.at[NDEV * a + rel], recv_sem=recv_sems.at[NDEV * a + rel], device_id=to, device_id_type=MESH)


def _scatter_start(arrays, views, shard_shapes, after):
    n = len(arrays)
    landing = [pltpu.with_memory_space_constraint(lax.empty((NDEV,) + tuple(s), a.dtype), pltpu.HBM)
               for s, a in zip(shard_shapes, arrays)]
    arrays = [pltpu.with_memory_space_constraint(a, pltpu.HBM) for a in arrays]

    def body(*refs):
        ins, land = refs[:n], refs[n:2 * n]
        send_sems, recv_sems = refs[2 * n + len(after)], refs[2 * n + len(after) + 1]
        token = refs[-1]
        for rel in range(NDEV):
            for a in range(n):
                _split_copy(ins, views, land, send_sems, recv_sems, a, rel).start()
        token[...] = jnp.zeros_like(token)

    res = pl.pallas_call(
        body, name="scatter_start",
        out_shape=[pltpu.SemaphoreType.DMA((NDEV * n,)), pltpu.SemaphoreType.DMA((NDEV * n,))]
        + [pltpu.HBM(a.shape, a.dtype) for a in arrays] + [pltpu.HBM(l.shape, l.dtype) for l in landing]
        + [jax.ShapeDtypeStruct((8, 128), F32)],
        in_specs=[_HBM_ONLY] * (2 * n) + [_HBM] * len(after),
        out_specs=[_SEM, _SEM] + [_HBM_ONLY] * (2 * n) + [pl.BlockSpec(memory_space=pltpu.VMEM)],
        input_output_aliases={i: 2 + i for i in range(2 * n)},
        compiler_params=pltpu.CompilerParams(has_side_effects=_DATAFLOW),
    )(*arrays, *landing, *after)
    return res[0], res[1], res[2:2 + n], res[2 + n:2 + 2 * n], res[-1]


def _scatter_wait(send_sems, recv_sems, arrays, landing, views, after):
    n = len(arrays)

    def body(*refs):
        ins, land = refs[:n], refs[n:2 * n]
        send, recv = refs[2 * n], refs[2 * n + 1]
        for rel in range(NDEV):
            for a in range(n):
                cp = _split_copy(ins, views, land, send, recv, a, rel)
                cp.wait_send()
                cp.wait_recv()

    res = pl.pallas_call(
        body, name="scatter_wait",
        out_shape=[pltpu.HBM(a.shape, a.dtype) for a in arrays] + [pltpu.HBM(l.shape, l.dtype) for l in landing],
        in_specs=[_HBM_ONLY] * (2 * n) + [_SEM, _SEM] + [_HBM] * len(after),
        out_specs=[_HBM_ONLY] * (2 * n),
        input_output_aliases={i: i for i in range(2 * n)},
        compiler_params=pltpu.CompilerParams(has_side_effects=_DATAFLOW),
    )(*arrays, *landing, send_sems, recv_sems, *after)
    return res[n:]


def _ada_forward(c_row, w_ada, b_ada8, carry):
    cols = w_ada.shape[1]

    def body(c_ref, w_ref, b_ref, mod_ref, sc_ref, c_all, send_buf, recv_buf, send1, recv1, send2, recv2):
        x, y, c = _me()
        me = 4 * x + 2 * y + c
        rowi = lax.broadcasted_iota(jnp.int32, (8, D_MODEL), 0)
        c_all[me] = jnp.broadcast_to(c_ref[...], (8, D_MODEL))
        copies = []
        for rel in range(1, 8):
            to, _ = _peer(rel)
            cp = pltpu.make_async_remote_copy(src_ref=c_all.at[me], dst_ref=c_all.at[me], send_sem=send1.at[rel - 1],
                                              recv_sem=recv1.at[rel - 1], device_id=to, device_id_type=MESH)
            cp.start()
            copies.append(cp)
        for rel in range(1, 8):
            _, p = _peer(rel)
            pltpu.make_async_remote_copy(src_ref=c_all.at[p], dst_ref=c_all.at[p], send_sem=send1.at[rel - 1],
                                         recv_sem=recv1.at[rel - 1], device_id=(x, y, c), device_id_type=MESH).wait_recv()
        for cp in copies:
            cp.wait_send()
        cmat = jnp.zeros((8, D_MODEL), F32)
        for b in range(8):
            cmat = jnp.where(rowi == b, c_all[b], cmat)
        sc = cmat * _sigmoid(cmat)
        sc_ref[...] = sc
        modcols = _mm(sc, w_ref[...]) + b_ref[pl.ds(me, 1), :]
        for b in range(8):
            send_buf[b] = jnp.broadcast_to(modcols[b:b + 1, :], (8, cols))
        recv_buf[me] = send_buf[me]
        copies = []
        for rel in range(1, 8):
            to, p = _peer(rel)
            cp = pltpu.make_async_remote_copy(src_ref=send_buf.at[p], dst_ref=recv_buf.at[me], send_sem=send2.at[rel - 1],
                                              recv_sem=recv2.at[rel - 1], device_id=to, device_id_type=MESH)
            cp.start()
            copies.append(cp)
        for rel in range(1, 8):
            _, p = _peer(rel)
            pltpu.make_async_remote_copy(src_ref=send_buf.at[p], dst_ref=recv_buf.at[p], send_sem=send2.at[rel - 1],
                                         recv_sem=recv2.at[rel - 1], device_id=(x, y, c), device_id_type=MESH).wait_recv()
        for cp in copies:
            cp.wait_send()
        rowc = lax.broadcasted_iota(jnp.int32, (8, cols), 0)
        out = jnp.zeros((8, cols), F32)
        for k in range(8):
            out = jnp.where(rowc == k, recv_buf[k], out)
        mod_ref[...] = out

    return _launch(
        body, "ada_forward",
        out_shape=[jax.ShapeDtypeStruct((8, cols), F32), jax.ShapeDtypeStruct((8, D_MODEL), F32)],
        in_specs=[pl.BlockSpec(memory_space=pltpu.VMEM)] * 3,
        out_specs=[pl.BlockSpec(memory_space=pltpu.VMEM)] * 2,
        operands=(c_row, w_ada, b_ada8),
        scratch=[pltpu.VMEM((8, 8, D_MODEL), F32), pltpu.VMEM((8, 8, cols), F32), pltpu.VMEM((8, 8, cols), F32)]
        + [pltpu.SemaphoreType.DMA((7,))] * 4,
        carry=carry)


def _allreduce_small(pack, order, name):
    rows = pack.shape[1]

    def body(pack_ref, order_ref, total_ref, land_ref, send1, recv1, send2, recv2):
        x, y, c = _me()
        me = 4 * x + 2 * y + c
        land_ref[me] = pack_ref[me]
        copies = []
        for rel in range(1, 8):
            to, p = _peer(rel)
            cp = pltpu.make_async_remote_copy(src_ref=pack_ref.at[p], dst_ref=land_ref.at[me], send_sem=send1.at[rel - 1],
                                              recv_sem=recv1.at[rel - 1], device_id=to, device_id_type=MESH)
            cp.start()
            copies.append(cp)
        for rel in range(1, 8):
            _, p = _peer(rel)
            pltpu.make_async_remote_copy(src_ref=pack_ref.at[p], dst_ref=land_ref.at[p], send_sem=send1.at[rel - 1],
                                         recv_sem=recv1.at[rel - 1], device_id=(x, y, c), device_id_type=MESH).wait_recv()
        for cp in copies:
            cp.wait_send()
        acc = land_ref[0]
        for b in range(1, 8):
            acc = acc + land_ref[b]
        total_ref[me] = acc
        copies = []
        for rel in range(1, 8):
            to, _ = _peer(rel)
            cp = pltpu.make_async_remote_copy(src_ref=total_ref.at[me], dst_ref=total_ref.at[me], send_sem=send2.at[rel - 1],
                                              recv_sem=recv2.at[rel - 1], device_id=to, device_id_type=MESH)
            cp.start()
            copies.append(cp)
        for rel in range(1, 8):
            _, p = _peer(rel)
            pltpu.make_async_remote_copy(src_ref=total_ref.at[p], dst_ref=total_ref.at[p], send_sem=send2.at[rel - 1],
                                         recv_sem=recv2.at[rel - 1], device_id=(x, y, c), device_id_type=MESH).wait_recv()
        for cp in copies:
            cp.wait_send()

    return pl.pallas_call(
        body, name=name,
        out_shape=[jax.ShapeDtypeStruct((8, rows, 128), F32), jax.ShapeDtypeStruct((8, rows, 128), F32)],
        in_specs=[pl.BlockSpec(memory_space=pltpu.VMEM), _HBM],
        out_specs=[pl.BlockSpec(memory_space=pltpu.VMEM)] * 2,
        scratch_shapes=[pltpu.SemaphoreType.DMA((7,))] * 4,
        compiler_params=pltpu.CompilerParams(vmem_limit_bytes=VMEM_LIMIT),
    )(pack, order)


def _modulated(x, prm_ref, sub, g_row):
    shift, scale = _row(prm_ref, 3 * sub), _row(prm_ref, 3 * sub + 1)
    g = _row(prm_ref, g_row)
    r = _rms_scale(x)
    n0 = x * r
    return (n0 * g) * (1.0 + scale) + shift, r, n0


def _swiglu_tile(xv, prm_ref, win_ref, wout_ref, ab_ref, sub, g_row):
    h, _, _ = _modulated(xv, prm_ref, sub, g_row)
    hb = h.astype(MXU_DTYPE)
    acc = None
    for j in range(4):
        a = _mm_nt(hb, win_ref[0, j])
        b = _mm_nt(hb, win_ref[1, j])
        ab_ref[0, j] = a.astype(SAVE_DTYPE)
        ab_ref[1, j] = b.astype(SAVE_DTYPE)
        t = _mm((a * _sigmoid(a)) * b, wout_ref[j])
        acc = t if acc is None else acc + t
    return acc


def _loss_tile(xv, target, g):
    r = _rms_scale(xv)
    n0 = xv * r
    err = n0 * g - target
    dy = err / float(D_MODEL)
    dn0 = dy * g
    dx = r * (dn0 - n0 * jnp.mean(dn0 * n0, axis=-1, keepdims=True))
    loss = 0.5 * jnp.sum(jnp.mean(err * err, axis=-1, keepdims=True), axis=0, keepdims=True)
    return dx, _colsum(dy * n0), loss


def _ffn_forward_loss(x, target, prm, win, wout, sub, g_row, name):
    T = x.shape[0]
    tm = min(T, TM_FFN)

    def body(x_ref, t_ref, prm_ref, win_ref, wout_ref, dx_ref, sums_ref, f_ref, ab_ref):
        i = pl.program_id(0)
        xv = x_ref[...]
        acc = _swiglu_tile(xv, prm_ref, win_ref, wout_ref, ab_ref, sub, g_row)
        f_ref[...] = acc.astype(SAVE_DTYPE)
        dx, dg, loss = _loss_tile(xv + (0.5 * _row(prm_ref, 3 * sub + 2)) * acc, t_ref[...], _row(prm_ref, ROW_G_FINAL))
        dx_ref[...] = dx
        upd = jnp.concatenate([dg, jnp.broadcast_to(loss, (1, D_MODEL)), jnp.zeros((6, D_MODEL), F32)], axis=0)

        @pl.when(i == 0)
        def _():
            sums_ref[...] = upd

        @pl.when(i > 0)
        def _():
            sums_ref[...] += upd

    tok = pl.BlockSpec((tm, D_MODEL), lambda i: (i, 0))
    return _launch(
        body, name, grid=(T // tm,), semantics=("arbitrary",),
        out_shape=[jax.ShapeDtypeStruct((T, D_MODEL), F32), jax.ShapeDtypeStruct((8, D_MODEL), F32),
                   jax.ShapeDtypeStruct((T, D_MODEL), SAVE_DTYPE), jax.ShapeDtypeStruct((2, 4, T, FF_SHARD), SAVE_DTYPE)],
        in_specs=[tok, tok, _resident(prm), _resident(win), _resident(wout)],
        out_specs=[tok, pl.BlockSpec((8, D_MODEL), lambda i: (0, 0)), tok,
                   pl.BlockSpec((2, 4, tm, FF_SHARD), lambda i: (0, 0, i, 0))],
        operands=(x, target, prm, win, wout))


def _ffn_hidden(x, prm, win, sub, g_row, name, carry=None):
    T = x.shape[0]
    tm = min(T, TM_FFN)

    def body(x_ref, prm_ref, win_ref, ab_ref, s_ref):
        h, _, _ = _modulated(x_ref[...], prm_ref, sub, g_row)
        hb = h.astype(MXU_DTYPE)
        for j in range(4):
            a = _mm_nt(hb, win_ref[0, j])
            b = _mm_nt(hb, win_ref[1, j])
            ab_ref[0, j] = a.astype(SAVE_DTYPE)
            ab_ref[1, j] = b.astype(SAVE_DTYPE)
            s_ref[j] = ((a * _sigmoid(a)) * b).astype(MXU_DTYPE)

    return _launch(
        body, name, grid=(T // tm,), semantics=("arbitrary",),
        out_shape=[jax.ShapeDtypeStruct((2, 4, T, FF_SHARD), SAVE_DTYPE), jax.ShapeDtypeStruct((4, T, FF_SHARD), MXU_DTYPE)],
        in_specs=[pl.BlockSpec((tm, D_MODEL), lambda i: (i, 0)), _resident(prm), _resident(win)],
        out_specs=[pl.BlockSpec((2, 4, tm, FF_SHARD), lambda i: (0, 0, i, 0)),
                   pl.BlockSpec((4, tm, FF_SHARD), lambda i: (0, i, 0))],
        operands=(x, prm, win), carry=carry, steps=_grid_steps(T // tm))


def _ffn_out(x, s, prm, wout, sub, name, carry=None):
    T = x.shape[0]
    tm = min(T, TM_FFN)

    def body(x_ref, s_ref, prm_ref, wout_ref, xo_ref, f_ref):
        acc = None
        for j in range(4):
            t = _mm(s_ref[j], wout_ref[j])
            acc = t if acc is None else acc + t
        f_ref[...] = acc.astype(SAVE_DTYPE)
        xo_ref[...] = x_ref[...] + (0.5 * _row(prm_ref, 3 * sub + 2)) * acc

    tok = pl.BlockSpec((tm, D_MODEL), lambda i: (i, 0))
    return _launch(
        body, name, grid=(T // tm,), semantics=("arbitrary",),
        out_shape=[jax.ShapeDtypeStruct((T, D_MODEL), F32), jax.ShapeDtypeStruct((T, D_MODEL), SAVE_DTYPE)],
        in_specs=[tok, pl.BlockSpec((4, tm, FF_SHARD), lambda i: (0, i, 0)), _resident(prm), _resident(wout)],
        out_specs=[tok, tok], operands=(x, s, prm, wout), carry=carry, steps=_grid_steps(T // tm))


def _ffn_backward(d, ab, prm, win, wout, sub, name, carry=None):
    T = d.shape[0]
    tm = min(T, TM_FFN_BWD)
    nt = T // tm
    chunk = min(tm, FFN_BWD_CHUNK)

    def body(d_ref, ab_ref, prm_ref, win_ref, wout_ref, dh_ref, dab_ref, dwout_ref, acc_out):
        i = pl.program_id(1)

        @pl.when(i == 0)
        def _():
            acc_out[...] = jnp.zeros_like(acc_out)

        wa, wb, wo = win_ref[0, 0], win_ref[1, 0], wout_ref[0]
        half_gate = 0.5 * _row(prm_ref, 3 * sub + 2)
        ss, dfss = [], []
        for ck in range(tm // chunk):
            rows = slice(ck * chunk, (ck + 1) * chunk)
            a = ab_ref[0, 0, rows, :].astype(F32)
            b = ab_ref[1, 0, rows, :].astype(F32)
            sg = _sigmoid(a)
            si = a * sg
            dfs = (half_gate * d_ref[rows, :]).astype(MXU_DTYPE)
            ds = _mm_nt(dfs, wo)
            da = (ds * b * (sg * (1.0 + a * (1.0 - sg)))).astype(MXU_DTYPE)
            db = (ds * si).astype(MXU_DTYPE)
            dh_ref[0, rows, :] = (_mm(da, wa) + _mm(db, wb)).astype(SAVE_DTYPE)
            dab_ref[0, 0, rows, :] = da
            dab_ref[1, 0, rows, :] = db
            ss.append((si * b).astype(MXU_DTYPE))
            dfss.append(dfs)
        cat = (lambda v: v[0]) if len(ss) == 1 else (lambda v: jnp.concatenate(v, axis=0))
        acc_out[...] += _mm_tn(cat(ss), cat(dfss))

        @pl.when(i == nt - 1)
        def _():
            dwout_ref[0] = acc_out[...].astype(WIRE_DTYPE)

    def steps():
        j, i = pl.program_id(0), pl.program_id(1)
        return (j == 0) & (i == 0), (j == 2) & (i == 0), (j == 3) & (i == nt - 1)

    pre = pl.BlockSpec((2, 1, tm, FF_SHARD), lambda j, i: (0, j, i, 0))
    return _launch(
        body, name, grid=(4, nt), semantics=("arbitrary", "arbitrary"),
        out_shape=[jax.ShapeDtypeStruct((4, T, D_MODEL), SAVE_DTYPE), jax.ShapeDtypeStruct(ab.shape, MXU_DTYPE),
                   jax.ShapeDtypeStruct(wout.shape, WIRE_DTYPE)],
        in_specs=[pl.BlockSpec((tm, D_MODEL), lambda j, i: (i, 0)), pre, _resident(prm),
                  pl.BlockSpec((2, 1, FF_SHARD, D_MODEL), lambda j, i: (0, j, 0, 0)),
                  pl.BlockSpec((1, FF_SHARD, D_MODEL), lambda j, i: (j, 0, 0))],
        out_specs=[pl.BlockSpec((1, tm, D_MODEL), lambda j, i: (j, i, 0)), pre,
                   pl.BlockSpec((1, FF_SHARD, D_MODEL), lambda j, i: (j, 0, 0))],
        operands=(d, ab, prm, win, wout), scratch=[pltpu.VMEM((FF_SHARD, D_MODEL), F32)], carry=carry, steps=steps)


def _ffn_dwin(x, dab, prm, sub, g_row, name, carry=None):
    T = x.shape[0]
    tm = min(T, TM_DWIN)
    nt = T // tm

    def body(x_ref, dab_ref, prm_ref, dwin_ref, acc):
        i = pl.program_id(1)

        @pl.when(i == 0)
        def _():
            acc[...] = jnp.zeros_like(acc)

        h, _, _ = _modulated(x_ref[...], prm_ref, sub, g_row)
        hb = h.astype(MXU_DTYPE)
        acc[0] += _mm_tn(dab_ref[0, 0], hb)
        acc[1] += _mm_tn(dab_ref[1, 0], hb)

        @pl.when(i == nt - 1)
        def _():
            dwin_ref[0, 0] = acc[0].astype(WIRE_DTYPE)
            dwin_ref[1, 0] = acc[1].astype(WIRE_DTYPE)

    def steps():
        j, i = pl.program_id(0), pl.program_id(1)
        return (j == 0) & (i == 0), (j == 2) & (i == 0), (j == 3) & (i == nt - 1)

    return _launch(
        body, name, grid=(4, nt), semantics=("arbitrary", "arbitrary"),
        out_shape=[jax.ShapeDtypeStruct((2, 4, FF_SHARD, D_MODEL), WIRE_DTYPE)],
        in_specs=[pl.BlockSpec((tm, D_MODEL), lambda j, i: (i, 0)),
                  pl.BlockSpec((2, 1, tm, FF_SHARD), lambda j, i: (0, j, i, 0)), _resident(prm)],
        out_specs=[pl.BlockSpec((2, 1, FF_SHARD, D_MODEL), lambda j, i: (0, j, 0, 0))],
        operands=(x, dab, prm), scratch=[pltpu.VMEM((2, FF_SHARD, D_MODEL), F32)], carry=carry, steps=steps)


def _norm_backward_tile(dh, xv, dv, fv, prm_ref, sub, g_row, gate_coef):
    scale, g = _row(prm_ref, 3 * sub + 1), _row(prm_ref, g_row)
    r = _rms_scale(xv)
    n0 = xv * r
    dn = dh * (1.0 + scale)
    dn0 = dn * g
    dx = dv + r * (dn0 - n0 * jnp.mean(dn0 * n0, axis=-1, keepdims=True))
    upd = jnp.concatenate([_colsum(dn * n0), _colsum(dh), _colsum(dh * (n0 * g)),
                           gate_coef * _colsum(dv * fv.astype(F32)), jnp.zeros((4, D_MODEL), F32)], axis=0)
    return dx, upd


def _norm_backward(parts, x, d, f, prm, sub, g_row, gate_coef, name):
    T = x.shape[0]
    tm = min(T, TM_EW)
    P = parts.shape[0]

    def body(p_ref, x_ref, d_ref, f_ref, prm_ref, dx_ref, sums_ref):
        i = pl.program_id(0)
        dh = p_ref[0].astype(F32)
        for k in range(1, P):
            dh = dh + p_ref[k].astype(F32)
        dx_ref[...], upd = _norm_backward_tile(dh, x_ref[...], d_ref[...], f_ref[...], prm_ref, sub, g_row, gate_coef)

        @pl.when(i == 0)
        def _():
            sums_ref[...] = upd

        @pl.when(i > 0)
        def _():
            sums_ref[...] += upd

    tok = pl.BlockSpec((tm, D_MODEL), lambda i: (i, 0))
    return _launch(
        body, name, grid=(T // tm,), semantics=("arbitrary",),
        out_shape=[jax.ShapeDtypeStruct((T, D_MODEL), F32), jax.ShapeDtypeStruct((8, D_MODEL), F32)],
        in_specs=[pl.BlockSpec((P, tm, D_MODEL), lambda i: (0, i, 0)), tok, tok, tok, _resident(prm)],
        out_specs=[tok, pl.BlockSpec((8, D_MODEL), lambda i: (0, 0))],
        operands=(parts, x, d, f, prm))


def _ssm_discretise(lam_re_log, lam_im, log_dt):
    lr = -jnp.exp(lam_re_log)
    dt = jnp.exp(log_dt)
    mag = jnp.exp(lr * dt)
    ang = lam_im * dt
    ab_re = mag * jnp.cos(ang)
    ab_im = mag * jnp.sin(ang)
    num_re = ab_re - 1.0
    num_im = ab_im
    den = lr * lr + lam_im * lam_im
    f_re = (num_re * lr + num_im * lam_im) / den
    f_im = (num_im * lr - num_re * lam_im) / den
    return ab_re, ab_im, f_re, f_im


def _ssm_params_forward(lam_re_log, lam_im, log_dt):
    def body(a_ref, b_ref, c_ref, o0, o1, o2, o3):
        outs = _ssm_discretise(a_ref[...], b_ref[...], c_ref[...])
        for o, v in zip((o0, o1, o2, o3), outs):
            o[...] = v

    return pl.pallas_call(body, name="ssm_params_forward",
                          out_shape=[jax.ShapeDtypeStruct(lam_im.shape, F32)] * 4)(lam_re_log, lam_im, log_dt)


def _ssm_params_backward(lam_re_log, lam_im, log_dt, cot):
    def body(a_ref, b_ref, c_ref, g0, g1, g2, g3, o0, o1, o2):
        _, vjp = jax.vjp(_ssm_discretise, a_ref[...], b_ref[...], c_ref[...])
        d0, d1, d2 = vjp((g0[...], g1[...], g2[...], g3[...]))
        o0[...] = d0
        o1[...] = d1
        o2[...] = d2

    return pl.pallas_call(
        body, name="ssm_params_backward",
        out_shape=[jax.ShapeDtypeStruct(lam_im.shape, F32), jax.ShapeDtypeStruct(lam_im.shape, F32),
                   jax.ShapeDtypeStruct(log_dt.shape, F32)])(lam_re_log, lam_im, log_dt, *cot)


def _ssm_dense_forward(srow, b_dense, c_dense):
    def body(srow_ref, bd_ref, cd_ref, bb_ref, ct_ref):
        for j in range(SSM_BLOCKS):
            lanes = slice(j * SSM_BLOCK_STATE, (j + 1) * SSM_BLOCK_STATE)
            f_re, f_im = srow_ref[2:3, lanes], srow_ref[3:4, lanes]
            bb_ref[0, j] = (f_re * bd_ref[0, j] - f_im * bd_ref[1, j]).astype(MXU_DTYPE)
            bb_ref[1, j] = (f_re * bd_ref[1, j] + f_im * bd_ref[0, j]).astype(MXU_DTYPE)
            ct_ref[0, j] = cd_ref[0, j].astype(MXU_DTYPE)
            ct_ref[1, j] = cd_ref[1, j].astype(MXU_DTYPE)

    return pl.pallas_call(body, name="ssm_dense_forward",
                          out_shape=[jax.ShapeDtypeStruct(b_dense.shape, MXU_DTYPE),
                                     jax.ShapeDtypeStruct(c_dense.shape, MXU_DTYPE)],
                          compiler_params=pltpu.CompilerParams(vmem_limit_bytes=VMEM_LIMIT))(srow, b_dense, c_dense)


def _cmul(p, q):
    return p[0] * q[0] - p[1] * q[1], p[0] * q[1] + p[1] * q[0]


def _scan_coefficients(ar, ai, reverse):
    n = ar.shape[1]
    p = {1: (ar, ai)}
    p[2] = _cmul(p[1], p[1])
    p[3] = _cmul(p[2], p[1])
    p[4] = _cmul(p[2], p[2])
    p[5] = _cmul(p[4], p[1])
    p[6] = _cmul(p[4], p[2])
    p[7] = _cmul(p[4], p[3])
    p[8] = _cmul(p[4], p[4])
    rowi = lax.broadcasted_iota(jnp.int32, (SCAN_ROWS, n), 0)
    tiles = []
    for dstep in (1, 2, 4):
        keep = (rowi < SCAN_ROWS - dstep) if reverse else (rowi >= dstep)
        for part in p[dstep]:
            tiles.append(jnp.where(keep, jnp.broadcast_to(part, (SCAN_ROWS, n)), 0.0))
    for comp in (0, 1):
        t = jnp.zeros((SCAN_ROWS, n), F32)
        for rr in range(SCAN_ROWS):
            power = SCAN_ROWS - rr if reverse else rr + 1
            t = jnp.where(rowi == rr, jnp.broadcast_to(p[power][comp], (SCAN_ROWS, n)), t)
        tiles.append(t)
    return tiles


def _load_stack(stack_hbm, dst, sems, base):
    cols = stack_hbm.shape[2]
    cps = [pltpu.make_async_copy(stack_hbm.at[k], dst.at[:, pl.ds(k * cols, cols)], sems.at[base + k])
           for k in range(NDEV)]
    for cp in cps:
        cp.start()
    return cps


def _window_lanes():
    lane = lax.broadcasted_iota(jnp.int32, (1, POOL_WIDTH), 1)
    return jnp.where(lane < 128, 2.0, jnp.where(lane < 256, 4.0, jnp.where(lane < 384, 8.0, 16.0)))


def _gelu(y):
    return 0.5 * y * (1.0 + lax.erf(y * 0.7071067811865476))


def _gelu_grad(y):
    return 0.5 * (1.0 + lax.erf(y * 0.7071067811865476)) + y * jnp.exp(-0.5 * y * y) * 0.3989422804014327


def _mixer_forward(x, prm, w_in_s, w_pu_s, w_glu_s, w_su_s, w_out, pool_w, mvec, srow, bb, ct, carry=None):
    T = x.shape[0]
    tm = min(T, TM_MIX)
    nt = T // tm
    n_tiles = tm // SCAN_ROWS

    def body(x_ref, prm_ref, w_in_h, w_pu_h, w_glu_h, w_su_h, w_out_h, pw_ref, mv_ref, srow_ref, bb, ct,
             x2_ref, mo_ref, z_ref, sre_ref, sim_ref, zp_ref, q_ref, yp_ref, yss_ref, vg_ref, ys_ref,
             w_in, w_pu, w_glu, w_su, w_o, coef, carry, hist, bu, sems):
        i = pl.program_id(0)

        @pl.when(i == 0)
        def _():
            cps = (_load_stack(w_in_h, w_in, sems, 0) + _load_stack(w_pu_h, w_pu, sems, 8)
                   + _load_stack(w_glu_h, w_glu, sems, 16) + _load_stack(w_su_h, w_su, sems, 24))
            cps.append(pltpu.make_async_copy(w_out_h, w_o, sems.at[32]))
            cps[-1].start()
            for j in range(SSM_BLOCKS):
                lanes = slice(j * SSM_BLOCK_STATE, (j + 1) * SSM_BLOCK_STATE)
                for k, t in enumerate(_scan_coefficients(srow_ref[0:1, lanes], srow_ref[1:2, lanes], False)):
                    coef[j, k] = t
            carry[...] = jnp.zeros_like(carry)
            hist[...] = jnp.zeros_like(hist)
            for cp in cps:
                cp.wait()

        xv = x_ref[...]
        h, _, _ = _modulated(xv, prm_ref, 1, ROW_G_MIX)
        z = _mm(h, w_in[...])
        z_ref[...] = z.astype(SAVE_DTYPE)
        u_pool, u_ssm = z[:, 0:512], z[:, 512:1024]
        gl_pool, gl_ssm = z[:, 1024:2048], z[:, 2048:3072]

        ext = jnp.concatenate([hist[...], u_pool], axis=0)
        w2 = ext + pltpu.roll(ext, 1, 0)
        w4 = w2[:, 128:] + pltpu.roll(w2[:, 128:], 2, 0)
        w8 = w4[:, 128:] + pltpu.roll(w4[:, 128:], 4, 0)
        w16 = w8[:, 128:] + pltpu.roll(w8[:, 128:], 8, 0)
        wsum = jnp.concatenate([w2[POOL_HALO:, :128], w4[POOL_HALO:, :128], w8[POOL_HALO:, :128], w16[POOL_HALO:]], axis=1)
        hist[...] = u_pool[tm - POOL_HALO:, :]
        t1 = (lax.broadcasted_iota(jnp.int32, (tm, 1), 0) + (i * tm + 1)).astype(F32)
        zp = wsum / jnp.minimum(t1, _window_lanes()) - u_pool
        zp_ref[...] = zp.astype(SAVE_DTYPE)
        q = jnp.concatenate([_mm(zp[:, k * 128:(k + 1) * 128], pw_ref[k]) for k in range(4)], axis=1)
        q = q + mv_ref[ROW_POOL_B:ROW_POOL_B + 1, 0:512]
        q_ref[...] = q.astype(SAVE_DTYPE)
        y_pool = _mm(q * mv_ref[ROW_POOL_SCALE:ROW_POOL_SCALE + 1, 0:512], w_pu[...])
        yp_ref[...] = y_pool.astype(SAVE_DTYPE)

        y_blocks = []
        for j in range(SSM_BLOCKS):
            lanes = pl.ds(j * SSM_BLOCK_STATE, SSM_BLOCK_STATE)
            ub = u_ssm[:, j * 128:(j + 1) * 128].astype(MXU_DTYPE)
            bu[0] = _mm(ub, bb[0, j])
            bu[1] = _mm(ub, bb[1, j])
            a1r, a1i, a2r, a2i, a4r, a4i, pr, pi = [coef[j, k] for k in range(8)]

            def step(tt, c, lanes=lanes, a1r=a1r, a1i=a1i, a2r=a2r, a2i=a2i, a4r=a4r, a4i=a4i, pr=pr, pi=pi):
                cr, ci = c
                rows = pl.ds(pl.multiple_of(tt * SCAN_ROWS, SCAN_ROWS), SCAN_ROWS)
                xr, xi = bu[0, rows, :], bu[1, rows, :]
                for dstep, kr, ki in ((1, a1r, a1i), (2, a2r, a2i), (4, a4r, a4i)):
                    sr, si = pltpu.roll(xr, dstep, 0), pltpu.roll(xi, dstep, 0)
                    xr, xi = xr + kr * sr - ki * si, xi + kr * si + ki * sr
                xr, xi = xr + pr * cr - pi * ci, xi + pr * ci + pi * cr
                sre_ref[rows, lanes] = xr
                sim_ref[rows, lanes] = xi
                return (jnp.broadcast_to(xr[SCAN_ROWS - 1:SCAN_ROWS, :], xr.shape),
                        jnp.broadcast_to(xi[SCAN_ROWS - 1:SCAN_ROWS, :], xi.shape))

            cr, ci = lax.fori_loop(0, n_tiles, step, (carry[j, 0], carry[j, 1]))
            carry[j, 0] = cr
            carry[j, 1] = ci
            y_blocks.append(_mm(sre_ref[:, lanes], ct[0, j]) - _mm(sim_ref[:, lanes], ct[1, j]))
        yss = jnp.concatenate(y_blocks, axis=1) + mv_ref[ROW_SSM_D:ROW_SSM_D + 1, 0:512] * u_ssm
        yss_ref[...] = yss.astype(SAVE_DTYPE)
        vg = _mm(_gelu(yss), w_glu[...]) + mv_ref[ROW_B_GLU:ROW_B_GLU + 1, :]
        vg_ref[...] = vg.astype(SAVE_DTYPE)
        y_ssm = _mm(vg[:, 0:512] * _sigmoid(vg[:, 512:1024]), w_su[...])
        ys_ref[...] = y_ssm.astype(SAVE_DTYPE)

        merged = _sigmoid(gl_pool) * y_pool + _sigmoid(gl_ssm) * y_ssm
        mo = _mm(merged, w_o[...])
        mo_ref[...] = mo.astype(SAVE_DTYPE)
        x2_ref[...] = xv + _row(prm_ref, 5) * mo

    def tok(width):
        return pl.BlockSpec((tm, width), lambda i: (i, 0))

    hbm = _HBM
    widths = (D_MODEL, D_MODEL, IN_WIDTH, N_STATE, N_STATE, 512, 512, D_MODEL, 512, D_MODEL, D_MODEL)
    dtypes = (F32, SAVE_DTYPE, SAVE_DTYPE, F32, F32) + (SAVE_DTYPE,) * 6
    return _launch(
        body, "mixer_forward", grid=(nt,), semantics=("arbitrary",), carry=carry, steps=_grid_steps(nt),
        out_shape=[jax.ShapeDtypeStruct((T, w), dt) for w, dt in zip(widths, dtypes)],
        in_specs=[tok(D_MODEL), _resident(prm), hbm, hbm, hbm, hbm, hbm, _resident(pool_w), _resident(mvec),
                  _resident(srow), _resident(bb), _resident(ct)],
        out_specs=[tok(w) for w in widths],
        operands=(x, prm, w_in_s, w_pu_s, w_glu_s, w_su_s, w_out, pool_w, mvec, srow, bb, ct),
        scratch=[
            pltpu.VMEM((D_MODEL, IN_WIDTH), MXU_DTYPE), pltpu.VMEM((512, D_MODEL), MXU_DTYPE),
            pltpu.VMEM((512, D_MODEL), MXU_DTYPE), pltpu.VMEM((512, D_MODEL), MXU_DTYPE),
            pltpu.VMEM((D_MODEL, D_MODEL), MXU_DTYPE),
            pltpu.VMEM((SSM_BLOCKS, 8, SCAN_ROWS, SSM_BLOCK_STATE), F32),
            pltpu.VMEM((SSM_BLOCKS, 2, SCAN_ROWS, SSM_BLOCK_STATE), F32),
            pltpu.VMEM((POOL_HALO, POOL_WIDTH), F32),
            pltpu.VMEM((2, tm, SSM_BLOCK_STATE), F32),
            pltpu.SemaphoreType.DMA((33,)),
        ])


def _mixer_backward(d2, prm, saved, w_pu_s, w_glu_s, w_su_s, w_out, pool_w, mvec, srow, bb, ct, carry=None):
    z, s_re, s_im, zp, q, y_pool, yss, vg, y_ssm = saved
    T = d2.shape[0]
    tm = min(T, TM_MIX_BWD)
    nt = T // tm
    n_tiles = tm // SCAN_ROWS

    def body(d_ref, prm_ref, z_ref, sre_ref, sim_ref, zp_ref, q_ref, yp_ref, yss_ref, vg_ref, ys_ref,
             w_pu_h, w_glu_h, w_su_h, w_out_h, pw_ref, mv_ref, srow_ref, bb, ct,
             dz_ref, dwo_h, dwpu_h, dwglu_h, dwsu_h, dpw_h, dbb_h, dct_h, vsum_h, da_h,
             w_pu, w_glu, w_su, w_o, pwb, coef, carry, hist, dre, lam,
             a_wo, a_wpu, a_wglu, a_wsu, a_pw, a_bb, a_ct, a_vs, a_da, st_wo, st_up, sems):
        i = pl.program_id(0)
        tile = nt - 1 - i

        @pl.when(i == 0)
        def _():
            cps = (_load_stack(w_pu_h, w_pu, sems, 0) + _load_stack(w_glu_h, w_glu, sems, 8)
                   + _load_stack(w_su_h, w_su, sems, 16))
            cps.append(pltpu.make_async_copy(w_out_h, w_o, sems.at[24]))
            cps[-1].start()
            pwb[...] = pw_ref[...].astype(MXU_DTYPE)
            for j in range(SSM_BLOCKS):
                lanes = slice(j * SSM_BLOCK_STATE, (j + 1) * SSM_BLOCK_STATE)
                for k, t in enumerate(_scan_coefficients(srow_ref[0:1, lanes], srow_ref[1:2, lanes], True)):
                    coef[j, k] = t
            for acc in (carry, hist, a_wo, a_wpu, a_wglu, a_wsu, a_pw, a_bb, a_ct, a_vs, a_da):
                acc[...] = jnp.zeros_like(acc)
            for cp in cps:
                cp.wait()

        dv = d_ref[...]
        zt = z_ref[...].astype(F32)
        u_ssm, gl_pool, gl_ssm = zt[:, 512:1024], zt[:, 1024:2048], zt[:, 2048:3072]
        y_p, y_s = yp_ref[...].astype(F32), ys_ref[...].astype(F32)
        sgp, sgs = _sigmoid(gl_pool), _sigmoid(gl_ssm)
        dmo = (_row(prm_ref, 5) * dv).astype(MXU_DTYPE)
        a_wo[...] += _mm_tn(sgp * y_p + sgs * y_s, dmo)
        dmerged = _mm_nt(dmo, w_o[...])
        dy_pool = dmerged * sgp
        dgl_pool = dmerged * y_p * (sgp * (1.0 - sgp))
        dy_ssm = dmerged * sgs
        dgl_ssm = dmerged * y_s * (sgs * (1.0 - sgs))

        scale = mv_ref[ROW_POOL_SCALE:ROW_POOL_SCALE + 1, 0:512]
        qv, zpv = q_ref[...].astype(F32), zp_ref[...]
        a_wpu[...] += _mm_tn(qv * scale, dy_pool)
        dp = _mm_nt(dy_pool, w_pu[...])
        dq = dp * scale
        a_vs[0:1, 0:512] += _colsum(dp * qv)
        a_vs[1:2, 0:512] += _colsum(dq)
        dzp_blocks = []
        for k in range(4):
            lanes = slice(k * 128, (k + 1) * 128)
            dzp_blocks.append(_mm_nt(dq[:, lanes], pwb[k]))
            a_pw[k] += _mm_tn(zpv[:, lanes], dq[:, lanes])
        dzp = jnp.concatenate(dzp_blocks, axis=1)
        t1 = (lax.broadcasted_iota(jnp.int32, (tm, 1), 0) + (tile * tm + 1)).astype(F32)
        gs = dzp / jnp.minimum(t1, _window_lanes())
        n_ext = tm + POOL_HALO
        ext = jnp.concatenate([gs, hist[...]], axis=0)
        v2 = ext + pltpu.roll(ext, n_ext - 1, 0)
        v4 = v2[:, 128:] + pltpu.roll(v2[:, 128:], n_ext - 2, 0)
        v8 = v4[:, 128:] + pltpu.roll(v4[:, 128:], n_ext - 4, 0)
        v16 = v8[:, 128:] + pltpu.roll(v8[:, 128:], n_ext - 8, 0)
        msum = jnp.concatenate([v2[:tm, :128], v4[:tm, :128], v8[:tm, :128], v16[:tm]], axis=1)
        hist[...] = gs[0:POOL_HALO, :]
        du_pool = msum - dzp

        vgv = vg_ref[...].astype(F32)
        val, gate = vgv[:, 0:512], vgv[:, 512:1024]
        sgg = _sigmoid(gate)
        a_wsu[...] += _mm_tn(val * sgg, dy_ssm)
        do = _mm_nt(dy_ssm, w_su[...])
        dvg = jnp.concatenate([do * sgg, do * val * (sgg * (1.0 - sgg))], axis=1)
        a_vs[3:4, :] += _colsum(dvg)
        yv = yss_ref[...].astype(F32)
        a_wglu[...] += _mm_tn(_gelu(yv), dvg)
        dyss = _mm_nt(dvg, w_glu[...]) * _gelu_grad(yv)
        a_vs[2:3, 0:512] += _colsum(dyss * u_ssm)
        du_blocks = []
        for j in range(SSM_BLOCKS):
            lanes = pl.ds(j * SSM_BLOCK_STATE, SSM_BLOCK_STATE)
            in_lanes = slice(j * 128, (j + 1) * 128)
            dyb = dyss[:, in_lanes].astype(MXU_DTYPE)
            ub = u_ssm[:, in_lanes].astype(MXU_DTYPE)
            dre[0] = _mm_nt(dyb, ct[0, j])
            dre[1] = -_mm_nt(dyb, ct[1, j])
            a_ct[0, j] += _mm_tn(sre_ref[:, lanes], dyb)
            a_ct[1, j] -= _mm_tn(sim_ref[:, lanes], dyb)
            a1r, a1i, a2r, a2i, a4r, a4i, pr, pi = [coef[j, k] for k in range(8)]
            rowi = lax.broadcasted_iota(jnp.int32, (SCAN_ROWS, SSM_BLOCK_STATE), 0)

            def step(tt, c, lanes=lanes, a1r=a1r, a1i=a1i, a2r=a2r, a2i=a2i, a4r=a4r, a4i=a4i, pr=pr, pi=pi, rowi=rowi):
                cr, ci, acc_r, acc_i = c
                rows = pl.ds(pl.multiple_of((n_tiles - 1 - tt) * SCAN_ROWS, SCAN_ROWS), SCAN_ROWS)
                xr, xi = dre[0, rows, :], dre[1, rows, :]
                for dstep, kr, ki in ((1, a1r, a1i), (2, a2r, a2i), (4, a4r, a4i)):
                    sr, si = pltpu.roll(xr, SCAN_ROWS - dstep, 0), pltpu.roll(xi, SCAN_ROWS - dstep, 0)
                    xr, xi = xr + kr * sr + ki * si, xi + kr * si - ki * sr
                xr, xi = xr + pr * cr + pi * ci, xi + pr * ci - pi * cr
                lam[0, rows, :] = xr
                lam[1, rows, :] = xi
                nr = jnp.where(rowi == SCAN_ROWS - 1, cr, pltpu.roll(xr, SCAN_ROWS - 1, 0))
                ni = jnp.where(rowi == SCAN_ROWS - 1, ci, pltpu.roll(xi, SCAN_ROWS - 1, 0))
                s_r, s_i = sre_ref[rows, lanes], sim_ref[rows, lanes]
                acc_r = acc_r + nr * s_r + ni * s_i
                acc_i = acc_i + ni * s_r - nr * s_i
                return (jnp.broadcast_to(xr[0:1, :], xr.shape), jnp.broadcast_to(xi[0:1, :], xi.shape), acc_r, acc_i)

            cr, ci, acc_r, acc_i = lax.fori_loop(0, n_tiles, step, (carry[j, 0], carry[j, 1], a_da[0, j], a_da[1, j]))
            carry[j, 0] = cr
            carry[j, 1] = ci
            a_da[0, j] = acc_r
            a_da[1, j] = acc_i
            lr_b, li_b = lam[0].astype(MXU_DTYPE), lam[1].astype(MXU_DTYPE)
            a_bb[0, j] += _mm_tn(ub, lr_b)
            a_bb[1, j] += _mm_tn(ub, li_b)
            du_blocks.append(_mm_nt(lr_b, bb[0, j]) + _mm_nt(li_b, bb[1, j]))
        du_ssm = jnp.concatenate(du_blocks, axis=1) + dyss * mv_ref[ROW_SSM_D:ROW_SSM_D + 1, 0:512]
        dz_ref[...] = jnp.concatenate([du_pool, du_ssm, dgl_pool, dgl_ssm], axis=1).astype(SAVE_DTYPE)

        @pl.when(i == nt - 1)
        def _():
            rows = D_MODEL // NDEV
            for k in range(NDEV):
                st_wo[k] = a_wo[k * rows:(k + 1) * rows, :].astype(WIRE_DTYPE)
                for a, acc in enumerate((a_wpu, a_wglu, a_wsu)):
                    st_up[a, k] = acc[:, k * 128:(k + 1) * 128].astype(WIRE_DTYPE)
            outs = ((st_wo, dwo_h), (st_up.at[0], dwpu_h), (st_up.at[1], dwglu_h), (st_up.at[2], dwsu_h),
                    (a_pw, dpw_h), (a_bb, dbb_h), (a_ct, dct_h), (a_vs, vsum_h), (a_da, da_h))
            cps = [pltpu.make_async_copy(src, dst, sems.at[k]) for k, (src, dst) in enumerate(outs)]
            for cp in cps:
                cp.start()
            for cp in cps:
                cp.wait()

    def tok(width):
        return pl.BlockSpec((tm, width), lambda i: (nt - 1 - i, 0))

    hbm = _HBM
    acc_shapes = [(D_MODEL, D_MODEL), (512, D_MODEL), (512, D_MODEL), (512, D_MODEL), (4, 128, 128),
                  (2, SSM_BLOCKS, 128, SSM_BLOCK_STATE), (2, SSM_BLOCKS, SSM_BLOCK_STATE, 128), (8, D_MODEL),
                  (2, SSM_BLOCKS, SCAN_ROWS, SSM_BLOCK_STATE)]
    stack_out = [jax.ShapeDtypeStruct((NDEV, D_MODEL // NDEV, D_MODEL), WIRE_DTYPE)] \
        + [jax.ShapeDtypeStruct((NDEV, 512, 128), WIRE_DTYPE)] * 3
    return _launch(
        body, "mixer_backward", grid=(nt,), semantics=("arbitrary",), carry=carry, steps=_grid_steps(nt),
        out_shape=[jax.ShapeDtypeStruct((T, IN_WIDTH), SAVE_DTYPE)] + stack_out
        + [jax.ShapeDtypeStruct(s, F32) for s in acc_shapes[4:]],
        in_specs=[tok(D_MODEL), _resident(prm), tok(IN_WIDTH), tok(N_STATE), tok(N_STATE), tok(512), tok(512),
                  tok(D_MODEL), tok(512), tok(D_MODEL), tok(D_MODEL), hbm, hbm, hbm, hbm, _resident(pool_w),
                  _resident(mvec), _resident(srow), _resident(bb), _resident(ct)],
        out_specs=[tok(IN_WIDTH)] + [hbm] * len(acc_shapes),
        operands=(d2, prm, z, s_re, s_im, zp, q, y_pool, yss, vg, y_ssm, w_pu_s, w_glu_s, w_su_s, w_out, pool_w, mvec,
                  srow, bb, ct),
        scratch=[
            pltpu.VMEM((512, D_MODEL), MXU_DTYPE), pltpu.VMEM((512, D_MODEL), MXU_DTYPE),
            pltpu.VMEM((512, D_MODEL), MXU_DTYPE), pltpu.VMEM((D_MODEL, D_MODEL), MXU_DTYPE),
            pltpu.VMEM((4, 128, 128), MXU_DTYPE),
            pltpu.VMEM((SSM_BLOCKS, 8, SCAN_ROWS, SSM_BLOCK_STATE), F32),
            pltpu.VMEM((SSM_BLOCKS, 2, SCAN_ROWS, SSM_BLOCK_STATE), F32),
            pltpu.VMEM((POOL_HALO, POOL_WIDTH), F32),
            pltpu.VMEM((2, tm, SSM_BLOCK_STATE), F32), pltpu.VMEM((2, tm, SSM_BLOCK_STATE), F32),
        ] + [pltpu.VMEM(s, F32) for s in acc_shapes]
        + [pltpu.VMEM((NDEV, D_MODEL // NDEV, D_MODEL), WIRE_DTYPE), pltpu.VMEM((3, NDEV, 512, 128), WIRE_DTYPE),
           pltpu.SemaphoreType.DMA((25,))])


def _mixer_in_backward(x, dz, d, mo, prm, w_in_s, carry=None):
    T = x.shape[0]
    tm = min(T, TM_MIX_IN)
    nt = T // tm
    cols = IN_WIDTH // NDEV

    def body(x_ref, dz_ref, d_ref, mo_ref, prm_ref, w_in_h, dx_ref, sums_ref, dw_ref, w_in, acc, sems):
        i = pl.program_id(0)

        @pl.when(i == 0)
        def _():
            cps = _load_stack(w_in_h, w_in, sems, 0)
            acc[...] = jnp.zeros_like(acc)
            for cp in cps:
                cp.wait()

        xv = x_ref[...]
        h, _, _ = _modulated(xv, prm_ref, 1, ROW_G_MIX)
        dzb = dz_ref[...].astype(MXU_DTYPE)
        acc[...] += _mm_tn(h, dzb)
        dx_ref[...], upd = _norm_backward_tile(_mm_nt(dzb, w_in[...]), xv, d_ref[...], mo_ref[...], prm_ref, 1,
                                               ROW_G_MIX, 1.0)

        @pl.when(i == 0)
        def _():
            sums_ref[...] = upd

        @pl.when(i > 0)
        def _():
            sums_ref[...] += upd

        @pl.when(i == nt - 1)
        def _():
            for k in range(NDEV):
                dw_ref[k] = acc[:, k * cols:(k + 1) * cols].astype(WIRE_DTYPE)

    tok = pl.BlockSpec((tm, D_MODEL), lambda i: (i, 0))
    return _launch(
        body, "mixer_in_backward", grid=(nt,), semantics=("arbitrary",), carry=carry, steps=_grid_steps(nt),
        out_shape=[jax.ShapeDtypeStruct((T, D_MODEL), F32), jax.ShapeDtypeStruct((8, D_MODEL), F32),
                   jax.ShapeDtypeStruct((NDEV, D_MODEL, cols), WIRE_DTYPE)],
        in_specs=[tok, pl.BlockSpec((tm, IN_WIDTH), lambda i: (i, 0)), tok, tok, _resident(prm), _HBM],
        out_specs=[tok, pl.BlockSpec((8, D_MODEL), lambda i: (0, 0)),
                   pl.BlockSpec((NDEV, D_MODEL, cols), lambda i: (0, 0, 0))],
        operands=(x, dz, d, mo, prm, w_in_s),
        scratch=[pltpu.VMEM((D_MODEL, IN_WIDTH), MXU_DTYPE), pltpu.VMEM((D_MODEL, IN_WIDTH), F32),
                 pltpu.SemaphoreType.DMA((8,))])


def _ssm_dense_backward(dbb, da, srow, b_dense):
    def body(dbb_ref, da_ref, srow_ref, bd_ref, db_ref, df_ref):
        df_re, df_im = [], []
        da_re = [_colsum(da_ref[0, j]) for j in range(SSM_BLOCKS)]
        da_im = [_colsum(da_ref[1, j]) for j in range(SSM_BLOCKS)]
        for j in range(SSM_BLOCKS):
            lanes = slice(j * SSM_BLOCK_STATE, (j + 1) * SSM_BLOCK_STATE)
            f_re, f_im = srow_ref[2:3, lanes], srow_ref[3:4, lanes]
            g_re, g_im = dbb_ref[0, j], dbb_ref[1, j]
            b_re, b_im = bd_ref[0, j], bd_ref[1, j]
            db_ref[0, j] = f_re * g_re + f_im * g_im
            db_ref[1, j] = f_re * g_im - f_im * g_re
            df_re.append(_colsum(g_re * b_re + g_im * b_im))
            df_im.append(_colsum(g_im * b_re - g_re * b_im))
        df_ref[...] = jnp.concatenate([jnp.concatenate(df_re, axis=1), jnp.concatenate(df_im, axis=1),
                                       jnp.concatenate(da_re, axis=1), jnp.concatenate(da_im, axis=1),
                                       jnp.zeros((4, N_STATE), F32)], axis=0)

    return pl.pallas_call(body, name="ssm_dense_backward",
                          out_shape=[jax.ShapeDtypeStruct(b_dense.shape, F32), jax.ShapeDtypeStruct((8, N_STATE), F32)],
                          compiler_params=pltpu.CompilerParams(vmem_limit_bytes=VMEM_LIMIT))(dbb, da, srow, b_dense)


def _adamw_update(w, g, m, v):
    m = ADAM_B1 * m + (1.0 - ADAM_B1) * g
    v = ADAM_B2 * v + (1.0 - ADAM_B2) * (g * g)
    m_hat = m / (1.0 - ADAM_B1 ** ADAM_STEP)
    v_hat = v / (1.0 - ADAM_B2 ** ADAM_STEP)
    delta = -ADAM_LR * (m_hat / (jnp.sqrt(v_hat) + ADAM_EPS) + ADAM_WD * w)
    return delta, m, v


def _adam_rows(shape):
    rows, cols = shape
    tr = rows
    while tr * cols * 4 > (1 << 20) and tr % 16 == 0:
        tr //= 2
    return tr


def _adam_sharded(w, m, v, land, order, name):
    R, C = w.shape
    tr = _adam_rows((R, C))

    def body(w_ref, m_ref, v_ref, land_ref, order_ref, g_ref, d_ref, mo_ref, vo_ref):
        g = land_ref[0].astype(F32)
        for b in range(1, NDEV):
            g = g + land_ref[b].astype(F32)
        g_ref[...] = g
        d_ref[...], mo_ref[...], vo_ref[...] = _adamw_update(w_ref[...], g, m_ref[...], v_ref[...])

    blk = pl.BlockSpec((tr, C), lambda i: (i, 0))
    return pl.pallas_call(
        body, name=name, grid=(R // tr,),
        out_shape=[jax.ShapeDtypeStruct((R, C), F32)] * 4,
        in_specs=[blk, blk, blk, pl.BlockSpec((NDEV, tr, C), lambda i: (0, i, 0)), _HBM],
        out_specs=[blk] * 4,
        compiler_params=_params("arbitrary"),
    )(w, m, v, land, order)


def _adam_ada(w, m, v, sc_all, dmod_cols):
    R, C = w.shape
    tr = 256

    def body(w_ref, m_ref, v_ref, sc_ref, dm_ref, g_ref, d_ref, mo_ref, vo_ref):
        g = _mm_tn(sc_ref[...], dm_ref[...])
        g_ref[...] = g
        d_ref[...], mo_ref[...], vo_ref[...] = _adamw_update(w_ref[...], g, m_ref[...], v_ref[...])

    blk = pl.BlockSpec((tr, C), lambda i: (i, 0))
    return pl.pallas_call(
        body, name="adam_w_ada", grid=(R // tr,),
        out_shape=[jax.ShapeDtypeStruct((R, C), F32)] * 4,
        in_specs=[blk, blk, blk, pl.BlockSpec((8, tr), lambda i: (0, i)), pl.BlockSpec((8, C), lambda i: (0, 0))],
        out_specs=[blk] * 4,
        compiler_params=_params("arbitrary"),
    )(w, m, v, sc_all, dmod_cols)


def _adam_small(ws, gs, ms, vs, order, name):
    n = len(ws)

    def body(*refs):
        w, g, m, v = (refs[k * n:(k + 1) * n] for k in range(4))
        outs = refs[4 * n + 1:]
        for k in range(n):
            outs[k][...], outs[n + k][...], outs[2 * n + k][...] = _adamw_update(w[k][...], g[k][...], m[k][...], v[k][...])

    vmem = pl.BlockSpec(memory_space=pltpu.VMEM)
    res = pl.pallas_call(body, name=name, out_shape=[jax.ShapeDtypeStruct(w.shape, F32) for w in ws] * 3,
                         in_specs=[vmem] * (4 * n) + [_HBM], out_specs=[vmem] * (3 * n),
                         compiler_params=pltpu.CompilerParams(vmem_limit_bytes=VMEM_LIMIT))(*ws, *gs, *ms, *vs, order)
    return res[:n], res[n:2 * n], res[2 * n:]


def _block_diag_in(b):
    bt = jnp.transpose(b, (0, 2, 1)).reshape(SSM_BLOCKS, 8, SSM_GROUP, SSM_STATE)
    eye = jnp.eye(8, dtype=bool)[None, :, None, :, None]
    return jnp.where(eye, bt[:, :, :, None, :], 0.0).reshape(SSM_BLOCKS, 128, SSM_BLOCK_STATE)


def _block_diag_out(c):
    ct = jnp.transpose(c, (0, 2, 1)).reshape(SSM_BLOCKS, 8, SSM_STATE, SSM_GROUP)
    eye = jnp.eye(8, dtype=bool)[None, :, None, :, None]
    return jnp.where(eye, ct[:, :, :, None, :], 0.0).reshape(SSM_BLOCKS, SSM_BLOCK_STATE, 128)


def _diag_blocks(dense, rows, cols):
    d5 = dense.reshape(SSM_BLOCKS, 8, rows, 8, cols)
    return jnp.stack([d5[:, a, :, a, :] for a in range(8)], axis=1).reshape(N_SSM_GROUPS, rows, cols)


def _pack_small(ada_vec, parts, params, tail=None):
    rest_rows, rows = _pack_rows(params, ada_vec is not None)
    rest = jnp.concatenate([parts[n].reshape(-1) for n, _ in params])
    rest = jnp.pad(rest, (0, NDEV * rest_rows * 128 - rest.shape[0])).reshape(NDEV, rest_rows, 128)
    head = [] if ada_vec is None else [ada_vec.reshape(NDEV, ADA_ROWS, 128)]
    pad = rows - rest_rows - (0 if ada_vec is None else ADA_ROWS)
    fill = jnp.zeros((NDEV, pad, 128), F32) if tail is None else jnp.pad(tail[None], ((0, NDEV - 1), (0, pad - 1), (0, 127)))
    return jnp.concatenate(head + [rest] + ([fill] if pad else []), axis=1)


def _unpack_small(pack, shapes, params, with_ada):
    rest_rows, _ = _pack_rows(params, with_ada)
    first = ADA_ROWS if with_ada else 0
    ada_vec = pack[:, :first].reshape(-1) if with_ada else None
    rest = pack[:, first:first + rest_rows].reshape(-1)
    out, off = {}, 0
    for n, size in params:
        out[n] = rest[off:off + size].reshape(shapes[n])
        off += size
    return ada_vec, out


WEIGHT_ORDER = ('w_ada', 'b_ada', 'g_ffn1', 'w_ffn1_in', 'w_ffn1_out', 'g_mix', 'w_in', 'pool_w', 'pool_b',
                'pool_scale', 'w_pool_up', 'ssm_lam_re_log', 'ssm_lam_im', 'ssm_log_dt', 'ssm_b_re', 'ssm_b_im',
                'ssm_c_re', 'ssm_c_im', 'ssm_d', 'w_glu', 'b_glu', 'w_ssm_up', 'w_out', 'g_ffn2', 'w_ffn2_in',
                'w_ffn2_out', 'g_final')
GATHERED = ('w_ffn1_in', 'w_ffn1_out', 'w_in', 'w_pool_up', 'w_glu', 'w_ssm_up', 'w_out', 'w_ffn2_in', 'w_ffn2_out')
TRANSPOSED = ('w_ffn1_in', 'w_ffn2_in')
STATE_MINOR = ('ssm_b_re', 'ssm_b_im')


def kernel(x, c, w_ada, b_ada, g_ffn1, w_ffn1_in, w_ffn1_out, g_mix, w_in, pool_w, pool_b, pool_scale, w_pool_up, ssm_lam_re_log, ssm_lam_im, ssm_log_dt, ssm_b_re, ssm_b_im, ssm_c_re, ssm_c_im, ssm_d, w_glu, b_glu, w_ssm_up, w_out, g_ffn2, w_ffn2_in, w_ffn2_out, g_final, loss_target, m_w_ada, m_b_ada, m_g_ffn1, m_w_ffn1_in, m_w_ffn1_out, m_g_mix, m_w_in, m_pool_w, m_pool_b, m_pool_scale, m_w_pool_up, m_ssm_lam_re_log, m_ssm_lam_im, m_ssm_log_dt, m_ssm_b_re, m_ssm_b_im, m_ssm_c_re, m_ssm_c_im, m_ssm_d, m_w_glu, m_b_glu, m_w_ssm_up, m_w_out, m_g_ffn2, m_w_ffn2_in, m_w_ffn2_out, m_g_final, v_w_ada, v_b_ada, v_g_ffn1, v_w_ffn1_in, v_w_ffn1_out, v_g_mix, v_w_in, v_pool_w, v_pool_b, v_pool_scale, v_w_pool_up, v_ssm_lam_re_log, v_ssm_lam_im, v_ssm_log_dt, v_ssm_b_re, v_ssm_b_im, v_ssm_c_re, v_ssm_c_im, v_ssm_d, v_w_glu, v_b_glu, v_w_ssm_up, v_w_out, v_g_ffn2, v_w_ffn2_in, v_w_ffn2_out, v_g_final):
    args = locals()
    W = {n: args[n] for n in WEIGHT_ORDER}
    M = {n: args["m_" + n] for n in WEIGHT_ORDER}
    V = {n: args["v_" + n] for n in WEIGHT_ORDER}
    shapes = {n: W[n].shape for n in WEIGHT_ORDER}
    xt, tgt = x[0], loss_target[0]

    def local(tree, n):
        return jnp.swapaxes(tree[n][0], 0, 1) if n in TRANSPOSED else tree[n][0]

    def as_output(n, a):
        return (jnp.swapaxes(a, 0, 1) if n in TRANSPOSED else a)[None]

    shard = dict(zip(GATHERED, _cast_shards([local(W, n) for n in GATHERED])))
    stacks = {}

    def gather(names):
        return _Gather([shard[n] for n in names])

    def gathered(names, results):
        stacks.update(zip(names, results))

    ffn1_w, ffn2_w = ('w_ffn1_in', 'w_ffn1_out'), ('w_ffn2_in', 'w_ffn2_out')
    mix_w = ('w_in', 'w_pool_up', 'w_glu', 'w_ssm_up', 'w_out')
    mod_cols, sc_all, *res = _ada_forward(c, W['w_ada'][0], b_ada.reshape(NDEV, -1), gather(ffn1_w[:1]))
    gathered(ffn1_w[:1], res)
    win1 = stacks['w_ffn1_in'].reshape(2, 4, FF_SHARD, D_MODEL)
    prm = jnp.concatenate([mod_cols.reshape(9, D_MODEL), g_ffn1, g_mix, g_ffn2, g_final[None], jnp.zeros((3, D_MODEL), F32)], axis=0)
    pad512 = jnp.zeros((1, D_MODEL - 512), F32)
    mvec = jnp.concatenate([jnp.concatenate([pool_b, pad512], axis=1), jnp.concatenate([pool_scale, pad512], axis=1),
                            jnp.concatenate([ssm_d, pad512], axis=1), b_glu, jnp.zeros((4, D_MODEL), F32)], axis=0)
    log_dt_col = ssm_log_dt[0][:, None]
    coeffs = _ssm_params_forward(ssm_lam_re_log[0], ssm_lam_im[0], log_dt_col)
    srow = jnp.stack([t.reshape(N_STATE) for t in coeffs], axis=0)
    b_dense = jnp.stack([_block_diag_in(ssm_b_re[0]), _block_diag_in(ssm_b_im[0])], axis=0)
    c_dense = jnp.stack([_block_diag_out(ssm_c_re[0]), _block_diag_out(ssm_c_im[0])], axis=0)
    bb, ct = _ssm_dense_forward(srow, b_dense, c_dense)
    pw = pool_w[0]

    next_w = ffn1_w[1:] + mix_w[:1]
    ab1, s1, *res = _ffn_hidden(xt, prm, win1, 0, ROW_G_FFN1, "ffn1_hidden", gather(next_w))
    gathered(next_w, res)
    wout1 = stacks['w_ffn1_out'].reshape(4, FF_SHARD, D_MODEL)
    x1, f1, *res = _ffn_out(xt, s1, prm, wout1, 0, "ffn1_out", gather(mix_w[1:]))
    gathered(mix_w[1:], res)
    w_out_full = stacks['w_out'].reshape(D_MODEL, D_MODEL)
    res = _mixer_forward(x1, prm, stacks['w_in'], stacks['w_pool_up'], stacks['w_glu'], stacks['w_ssm_up'],
                         w_out_full, pw, mvec, srow, bb, ct, gather(ffn2_w))
    x2, mo, saved = res[0], res[1], res[2:11]
    gathered(ffn2_w, res[11:])
    win2 = stacks['w_ffn2_in'].reshape(2, 4, FF_SHARD, D_MODEL)
    wout2 = stacks['w_ffn2_out'].reshape(4, FF_SHARD, D_MODEL)
    d3, fin, f3, ab3 = _ffn_forward_loss(x2, tgt, prm, win2, wout2, 2, ROW_G_FFN2, "ffn2_forward_loss")

    lands = {}

    def scatter(grads):
        names = list(grads)
        return _Scatter([grads[n][0] for n in names], [grads[n][1] for n in names], [local(W, n).shape for n in names])

    def scattered(grads, results):
        lands.update(zip(grads, results))

    parts3, dab3, dwout2 = _ffn_backward(d3, ab3, prm, win2, wout2, 2, "ffn2_backward")
    dwin2, = _ffn_dwin(x2, dab3, prm, 2, ROW_G_FFN2, "ffn2_dwin")
    d2, sums3 = _norm_backward(parts3, x2, d3, f3, prm, 2, ROW_G_FFN2, 0.5, "ffn2_norm_backward")
    g_ffn2_w = {'w_ffn2_in': (dwin2, _halves), 'w_ffn2_out': (dwout2.reshape(NDEV, -1, D_MODEL), _stacked)}
    res = _mixer_backward(d2, prm, saved, stacks['w_pool_up'], stacks['w_glu'], stacks['w_ssm_up'], w_out_full, pw, mvec,
                          srow, bb, ct, scatter(g_ffn2_w))
    dz, dwo, dwpu, dwglu, dwsu, dpw, dbb, dct, vsum, da = res[:10]
    scattered(g_ffn2_w, res[10:])
    g_mix_up = {'w_pool_up': (dwpu, _stacked), 'w_glu': (dwglu, _stacked), 'w_ssm_up': (dwsu, _stacked),
                'w_out': (dwo, _stacked)}
    d1, sums2, dwin_mix, *res = _mixer_in_backward(x1, dz, d2, mo, prm, stacks['w_in'], scatter(g_mix_up))
    scattered(g_mix_up, res)
    g_mix_w = {'w_in': (dwin_mix, _stacked)}
    db_dense, df_rows = _ssm_dense_backward(dbb, da, srow, b_dense)
    cot = [df_rows[r].reshape(N_SSM_GROUPS, SSM_STATE) for r in (2, 3, 0, 1)]
    d_lrl, d_li, d_ldt = _ssm_params_backward(ssm_lam_re_log[0], ssm_lam_im[0], log_dt_col, cot)
    small_grads = {
        'g_mix': sums2[0], 'g_ffn2': sums3[0], 'g_final': fin[0], 'pool_w': dpw,
        'pool_b': vsum[1, :512], 'pool_scale': vsum[0, :512], 'ssm_lam_re_log': d_lrl, 'ssm_lam_im': d_li,
        'ssm_log_dt': d_ldt, 'ssm_b_re': _diag_blocks(db_dense[0], SSM_GROUP, SSM_STATE),
        'ssm_b_im': _diag_blocks(db_dense[1], SSM_GROUP, SSM_STATE),
        'ssm_c_re': jnp.transpose(_diag_blocks(dct[0], SSM_STATE, SSM_GROUP), (0, 2, 1)),
        'ssm_c_im': jnp.transpose(_diag_blocks(dct[1], SSM_STATE, SSM_GROUP), (0, 2, 1)),
        'ssm_d': vsum[2, :512], 'b_glu': vsum[3],
    }
    early = _SmallAllReduce(_pack_small(None, small_grads, SMALL_EARLY, fin[1:2, 0:1]))
    parts1, dab1, dwout1, total_early, *res = _ffn_backward(d1, ab1, prm, win1, wout1, 0, "ffn1_backward",
                                                            _Carried(early, scatter(g_mix_w)))
    scattered(g_mix_w, res)
    loss = total_early[0, _pack_rows(SMALL_EARLY, False)[0], 0]
    g_wout1 = {'w_ffn1_out': (dwout1.reshape(NDEV, -1, D_MODEL), _stacked)}
    dwin1, *res = _ffn_dwin(xt, dab1, prm, 0, ROW_G_FFN1, "ffn1_dwin", scatter(g_wout1))
    scattered(g_wout1, res)

    last_w, last_views = ffn1_w[:1], [_halves]
    send_sems, recv_sems, last_src, last_land, token = _scatter_start(
        [dwin1], last_views, [local(W, n).shape for n in last_w], [total_early])
    after_start = token[0:1, 0:1]
    d0, sums1 = _norm_backward(parts1, xt, d1, f1, prm + after_start, 0, ROW_G_FFN1, 0.5, "ffn1_norm_backward")

    grad, delta, new_m, new_v = {}, {}, {}, {}

    def adam_sharded(n):
        res = _adam_sharded(local(W, n), local(M, n), local(V, n), lands[n], token, "adam_" + n)
        grad[n], delta[n], new_m[n], new_v[n] = [as_output(n, r) for r in res]
        return res[3]

    def view(n, a):
        if n in STATE_MINOR:
            return jnp.transpose(a[0], (0, 2, 1))
        return a.reshape(1, -1) if a.ndim == 1 else (a[0] if a.ndim > 2 else a)

    def unview(n, a):
        return jnp.transpose(a, (0, 2, 1))[None] if n in STATE_MINOR else a.reshape(shapes[n])

    def adam_small(params, ada, total, name, order):
        names = [n for n, _ in params] + (['b_ada'] if ada else [])
        view_shapes = {n: view(n, W[n]).shape for n in names}
        ada_vec, grads = _unpack_small(total, view_shapes, params, ada)
        if ada:
            grads['b_ada'] = ada_vec.reshape(view_shapes['b_ada'])
        ws, ms, vs = ([view(n, t[n]) for n in names] for t in (W, M, V))
        gs = [grads[n] for n in names]
        res = _adam_small(ws, gs, ms, vs, order, name)
        for dst, vals in zip((grad, delta, new_m, new_v), (gs, *res)):
            dst.update({n: unview(n, a) for n, a in zip(names, vals)})

    done = [d0] + [adam_sharded(n) for n in GATHERED if n not in last_w]
    adam_small(SMALL_EARLY, False, total_early, "adam_small_early", token)
    lands.update(zip(last_w, _scatter_wait(send_sems, recv_sems, last_src, last_land, last_views, done)))

    dmod = jnp.concatenate([sums1[1:4], sums2[1:4], sums3[1:4]], axis=0).reshape(-1)
    total_late, landed = _allreduce_small(_pack_small(dmod, {'g_ffn1': sums1[0]}, SMALL_LATE), lands[last_w[0]],
                                          "allreduce_late")
    dmod_cols = landed[:, :ADA_ROWS].reshape(NDEV, ADA_ROWS * 128)
    res = _adam_ada(W['w_ada'][0], M['w_ada'][0], V['w_ada'][0], sc_all, dmod_cols)
    grad['w_ada'], delta['w_ada'], new_m['w_ada'], new_v['w_ada'] = [r[None] for r in res]
    adam_small(SMALL_LATE, True, total_late, "adam_small_late", token)
    for n in last_w:
        adam_sharded(n)

    return (loss, d0[None], *[grad[n] for n in WEIGHT_ORDER], *[delta[n] for n in WEIGHT_ORDER],
            *[new_m[n] for n in WEIGHT_ORDER], *[new_v[n] for n in WEIGHT_ORDER])
```

```python
import jax
import jax.numpy as jnp
from jax import lax
from jax.experimental import pallas as pl
from jax.experimental.pallas import tpu as pltpu

F32 = jnp.float32
MXU_DTYPE = jnp.bfloat16
WIRE_DTYPE = jnp.bfloat16
SAVE_DTYPE = jnp.bfloat16

NDEV = 8
D_MODEL = 1024
D_FF = 2816
FF_SHARD = 2 * D_FF // NDEV
POOL_WIDTH = 512
POOL_GROUP = 128
SSM_WIDTH = 512
SSM_STATE = 64
SSM_GROUP = 16
N_SSM_GROUPS = SSM_WIDTH // SSM_GROUP
SSM_BLOCKS = 4
SSM_BLOCK_STATE = 512
N_STATE = 2048
IN_WIDTH = 3072
EPS = 1e-6
ADAM_LR = 0.001
ADAM_B1 = 0.9
ADAM_B2 = 0.999
ADAM_EPS = 1e-08
ADAM_WD = 0.01
ADAM_STEP = 10

TM_FFN = 512
TM_FFN_BWD = 1024
TM_DWIN = 2048
TM_MIX_IN = 512
FFN_BWD_CHUNK = 256
TM_MIX = 256
TM_MIX_BWD = 256
TM_EW = 512
SCAN_ROWS = 8
SCAN_UNROLL = 4
POOL_HALO = 16
VMEM_LIMIT = 60 * 1024 * 1024

ROW_G_FFN1, ROW_G_MIX, ROW_G_FFN2, ROW_G_FINAL = 9, 10, 11, 12
ROW_POOL_B, ROW_POOL_SCALE, ROW_SSM_D, ROW_B_GLU = 0, 1, 2, 3

SMALL_EARLY = (
    ("g_mix", 1024), ("g_ffn2", 1024), ("g_final", 1024), ("pool_w", 65536),
    ("pool_b", 512), ("pool_scale", 512), ("ssm_lam_re_log", 2048), ("ssm_lam_im", 2048),
    ("ssm_log_dt", 32), ("ssm_b_re", 32768), ("ssm_b_im", 32768), ("ssm_c_re", 32768),
    ("ssm_c_im", 32768), ("ssm_d", 512), ("b_glu", 1024),
)
SMALL_LATE = (("g_ffn1", 1024),)
ADA_ROWS = 9
MESH = pl.DeviceIdType.MESH


def _pack_rows(params, with_ada):
    rest = -(-sum(n for _, n in params) // (NDEV * 128))
    return rest, -(-(rest + (ADA_ROWS if with_ada else 0)) // 8) * 8


def _mm(a, b):
    return jnp.dot(a.astype(MXU_DTYPE), b.astype(MXU_DTYPE), preferred_element_type=F32)


def _mm_nt(a, b):
    return lax.dot_general(a.astype(MXU_DTYPE), b.astype(MXU_DTYPE), (((1,), (1,)), ((), ())),
                           preferred_element_type=F32)


def _mm_tn(a, b):
    return lax.dot_general(a.astype(MXU_DTYPE), b.astype(MXU_DTYPE), (((0,), (0,)), ((), ())),
                           preferred_element_type=F32)


def _rms_scale(x):
    return lax.rsqrt(jnp.mean(x * x, axis=-1, keepdims=True) + EPS)


def _sigmoid(x):
    return jax.nn.sigmoid(x)


def _colsum(x):
    return jnp.sum(x, axis=0, keepdims=True)


def _row(ref, r):
    return ref[r:r + 1, :]


def _params(*sem):
    return pltpu.CompilerParams(dimension_semantics=sem, vmem_limit_bytes=VMEM_LIMIT)


def _resident(a):
    return pl.BlockSpec(a.shape, lambda *_: (0,) * a.ndim, pipeline_mode=pl.Buffered(1))


def _me():
    return lax.axis_index("x"), lax.axis_index("y"), lax.axis_index("c")


def _peer(rel):
    x, y, c = _me()
    px = 1 - x if rel & 4 else x
    py = 1 - y if rel & 2 else y
    pc = 1 - c if rel & 1 else c
    return (px, py, pc), 4 * px + 2 * py + pc


_HBM = pl.BlockSpec(memory_space=pl.ANY)
_HBM_ONLY = pl.BlockSpec(memory_space=pltpu.HBM)


def _stacked(ref, p):
    return ref.at[p]


def _halves(ref, p):
    return ref.at[p // 4, p % 4]


class _Gather:
    def __init__(self, shards):
        self.operands = list(shards)
        self.n = len(shards)
        self.out_shape = [jax.ShapeDtypeStruct((NDEV,) + s.shape, s.dtype) for s in shards]
        self.scratch = [pltpu.SemaphoreType.DMA((7 * self.n,)), pltpu.SemaphoreType.DMA((7 * self.n,)),
                        pltpu.SemaphoreType.DMA((self.n,))]

    def plan(self, srcs, outs, sems):
        send_sems, recv_sems, local_sems = sems
        n = self.n
        x, y, c = _me()
        me = 4 * x + 2 * y + c
        here, sibling = (x, y, c), (x, y, 1 - c)
        chips = [(1 - x, y), (x, 1 - y), (1 - x, 1 - y)]

        def blk(px, py, pc):
            return 4 * px + 2 * py + pc

        def copy(a, k, block, to, src=None):
            return pltpu.make_async_remote_copy(
                src_ref=outs[a].at[block] if src is None else src, dst_ref=outs[a].at[block],
                send_sem=send_sems.at[7 * a + k], recv_sem=recv_sems.at[7 * a + k], device_id=to, device_id_type=MESH)

        def mine(a):
            return pltpu.make_async_copy(srcs[a], outs[a].at[me], local_sems.at[a])

        def first(a):
            return [copy(a, 0, me, sibling, src=srcs[a])] + [copy(a, 1 + j, me, (*chip, c), src=srcs[a])
                                                              for j, chip in enumerate(chips)]

        def start():
            for a in range(n):
                mine(a).start()
                for cp in first(a):
                    cp.start()

        def forward():
            for a in range(n):
                for j, chip in enumerate(chips):
                    copy(a, 1 + j, blk(*chip, c), here).wait_recv()
                    copy(a, 4 + j, blk(*chip, c), sibling).start()

        def finish():
            for a in range(n):
                copy(a, 0, blk(x, y, 1 - c), here).wait_recv()
                for j, chip in enumerate(chips):
                    copy(a, 4 + j, blk(*chip, 1 - c), here).wait_recv()
            for a in range(n):
                mine(a).wait()
                for cp in first(a):
                    cp.wait_send()
                for j, chip in enumerate(chips):
                    copy(a, 4 + j, blk(*chip, c), sibling).wait_send()

        return start, forward, finish


class _Scatter:
    def __init__(self, arrays, views, shard_shapes):
        self.operands = list(arrays)
        self.views = list(views)
        self.n = len(arrays)
        self.out_shape = [jax.ShapeDtypeStruct((NDEV,) + tuple(s), a.dtype) for s, a in zip(shard_shapes, arrays)]
        self.scratch = [pltpu.SemaphoreType.DMA((7 * self.n,)), pltpu.SemaphoreType.DMA((7 * self.n,)),
                        pltpu.SemaphoreType.DMA((self.n,))]

    def plan(self, srcs, outs, sems):
        send_sems, recv_sems, local_sems = sems
        n, views = self.n, self.views
        x, y, c = _me()
        me = 4 * x + 2 * y + c

        def mine(a):
            return pltpu.make_async_copy(views[a](srcs[a], me), outs[a].at[me], local_sems.at[a])

        def copy(a, rel, sending):
            to, p = _peer(rel)
            return pltpu.make_async_remote_copy(
                src_ref=views[a](srcs[a], p), dst_ref=outs[a].at[me if sending else p],
                send_sem=send_sems.at[7 * a + rel - 1], recv_sem=recv_sems.at[7 * a + rel - 1],
                device_id=to if sending else (x, y, c), device_id_type=MESH)

        def start():
            for a in range(n):
                mine(a).start()
            for rel in range(1, 8):
                for a in range(n):
                    copy(a, rel, True).start()

        def forward():
            pass

        def finish():
            for rel in range(1, 8):
                for a in range(n):
                    copy(a, rel, False).wait_recv()
            for rel in range(1, 8):
                for a in range(n):
                    copy(a, rel, True).wait_send()
            for a in range(n):
                mine(a).wait()

        return start, forward, finish


class _SmallAllReduce:
    def __init__(self, pack):
        rows = pack.shape[1]
        self.operands = [pack]
        self.n = 1
        self.out_shape = [jax.ShapeDtypeStruct(pack.shape, F32)]
        self.scratch = [pltpu.VMEM(pack.shape, F32), pltpu.VMEM((rows, 128), F32)] \
            + [pltpu.SemaphoreType.DMA((7,))] * 4 + [pltpu.SemaphoreType.DMA((2,))]

    def plan(self, srcs, outs, scratch):
        pack, total = srcs[0], outs[0]
        land, mine, send1, recv1, send2, recv2, local = scratch
        x, y, c = _me()
        me = 4 * x + 2 * y + c

        def slab(rel, sending):
            to, p = _peer(rel)
            return pltpu.make_async_remote_copy(
                src_ref=pack.at[p], dst_ref=land.at[me if sending else p], send_sem=send1.at[rel - 1],
                recv_sem=recv1.at[rel - 1], device_id=to if sending else (x, y, c), device_id_type=MESH)

        def summed(rel, sending):
            to, p = _peer(rel)
            return pltpu.make_async_remote_copy(
                src_ref=mine, dst_ref=total.at[me if sending else p], send_sem=send2.at[rel - 1],
                recv_sem=recv2.at[rel - 1], device_id=to if sending else (x, y, c), device_id_type=MESH)

        own_slab = pltpu.make_async_copy(pack.at[me], land.at[me], local.at[0])
        own_sum = pltpu.make_async_copy(mine, total.at[me], local.at[1])

        def start():
            own_slab.start()
            for rel in range(1, 8):
                slab(rel, True).start()

        def forward():
            own_slab.wait()
            for rel in range(1, 8):
                slab(rel, False).wait_recv()
            acc = land[0]
            for b in range(1, NDEV):
                acc = acc + land[b]
            mine[...] = acc
            own_sum.start()
            for rel in range(1, 8):
                summed(rel, True).start()

        def finish():
            for rel in range(1, 8):
                summed(rel, False).wait_recv()
            for rel in range(1, 8):
                slab(rel, True).wait_send()
                summed(rel, True).wait_send()
            own_sum.wait()

        return start, forward, finish


class _Carried:
    def __init__(self, *parts):
        self.parts = parts
        self.operands = [o for p in parts for o in p.operands]
        self.n = len(self.operands)
        self.out_shape = [s for p in parts for s in p.out_shape]
        self.scratch = [s for p in parts for s in p.scratch]

    def plan(self, srcs, outs, scratch):
        plans, a, b = [], 0, 0
        for p in self.parts:
            plans.append(p.plan(srcs[a:a + p.n], outs[a:a + p.n], scratch[b:b + len(p.scratch)]))
            a, b = a + p.n, b + len(p.scratch)

        def every(k):
            def run():
                for plan in plans:
                    plan[k]()
            return run

        return every(0), every(1), every(2)


def _launch(body, name, out_shape, in_specs, out_specs, operands, scratch=(), grid=None, semantics=None,
            carry=None, steps=None):
    out_shape, in_specs, out_specs = list(out_shape), list(in_specs), list(out_specs)
    operands, scratch = list(operands), list(scratch)
    n_in, n_out, n_scr = len(in_specs), len(out_shape), len(scratch)
    kernel_body = body
    if carry is not None:
        k = carry.n

        def kernel_body(*refs):
            ins, cin = refs[:n_in], refs[n_in:n_in + k]
            outs, cout = refs[n_in + k:n_in + k + n_out], refs[n_in + k + n_out:n_in + 2 * k + n_out]
            rest = refs[n_in + 2 * k + n_out:]
            scr, csem = rest[:n_scr], rest[n_scr:]
            start, forward, finish = carry.plan(cin, cout, csem)
            if steps is None:
                start()
                body(*ins, *outs, *scr)
                forward()
                finish()
            else:
                pl.when(steps()[0])(start)
                pl.when(steps()[1])(forward)
                body(*ins, *outs, *scr)
                pl.when(steps()[2])(finish)

        in_specs += [_HBM] * k
        out_shape += carry.out_shape
        out_specs += [_HBM] * k
        operands += carry.operands
        scratch += carry.scratch
    kwargs = {} if grid is None else {"grid": grid}
    params = pltpu.CompilerParams(vmem_limit_bytes=VMEM_LIMIT) if semantics is None else _params(*semantics)
    return pl.pallas_call(kernel_body, name=name, out_shape=out_shape, in_specs=in_specs, out_specs=out_specs,
                          scratch_shapes=scratch, compiler_params=params, **kwargs)(*operands)


def _grid_steps(nt):
    def steps():
        i = pl.program_id(0)
        return i == 0, i == nt - 1, i == nt - 1
    return steps


def _cast_shards(shards):
    n = len(shards)

    def body(*refs):
        for a in range(n):
            refs[n + a][...] = refs[a][...].astype(WIRE_DTYPE)

    return pl.pallas_call(body, name="cast_shards",
                          out_shape=[jax.ShapeDtypeStruct(s.shape, WIRE_DTYPE) for s in shards],
                          compiler_params=pltpu.CompilerParams(vmem_limit_bytes=VMEM_LIMIT))(*shards)


_SEM = pl.BlockSpec(memory_space=pltpu.SEMAPHORE)
_DATAFLOW = pltpu.SideEffectType.DATAFLOW_SIDE_EFFECTING


def _split_copy(arrays, views, landing, send_sems, recv_sems, a, rel):
    to, p = _peer(rel)
    x, y, c = _me()
    return pltpu.make_async_remote_copy(
        src_ref=views[a](arrays[a], p), dst_ref=landing[a].at[4 * x + 2 * y + c],
        send_sem=send_sems.at[NDEV * a + rel], recv_sem=recv_sems.at[NDEV * a + rel], device_id=to, device_id_type=MESH)


def _scatter_start(arrays, views, shard_shapes, after):
    n = len(arrays)
    landing = [pltpu.with_memory_space_constraint(lax.empty((NDEV,) + tuple(s), a.dtype), pltpu.HBM)
               for s, a in zip(shard_shapes, arrays)]
    arrays = [pltpu.with_memory_space_constraint(a, pltpu.HBM) for a in arrays]

    def body(*refs):
        ins, land = refs[:n], refs[n:2 * n]
        send_sems, recv_sems = refs[2 * n + len(after)], refs[2 * n + len(after) + 1]
        token = refs[-1]
        for rel in range(NDEV):
            for a in range(n):
                _split_copy(ins, views, land, send_sems, recv_sems, a, rel).start()
        token[...] = jnp.zeros_like(token)

    res = pl.pallas_call(
        body, name="scatter_start",
        out_shape=[pltpu.SemaphoreType.DMA((NDEV * n,)), pltpu.SemaphoreType.DMA((NDEV * n,))]
        + [pltpu.HBM(a.shape, a.dtype) for a in arrays] + [pltpu.HBM(l.shape, l.dtype) for l in landing]
        + [jax.ShapeDtypeStruct((8, 128), F32)],
        in_specs=[_HBM_ONLY] * (2 * n) + [_HBM] * len(after),
        out_specs=[_SEM, _SEM] + [_HBM_ONLY] * (2 * n) + [pl.BlockSpec(memory_space=pltpu.VMEM)],
        input_output_aliases={i: 2 + i for i in range(2 * n)},
        compiler_params=pltpu.CompilerParams(has_side_effects=_DATAFLOW),
    )(*arrays, *landing, *after)
    return res[0], res[1], res[2:2 + n], res[2 + n:2 + 2 * n], res[-1]


def _scatter_wait(send_sems, recv_sems, arrays, landing, views, after):
    n = len(arrays)

    def body(*refs):
        ins, land = refs[:n], refs[n:2 * n]
        send, recv = refs[2 * n], refs[2 * n + 1]
        for rel in range(NDEV):
            for a in range(n):
                cp = _split_copy(ins, views, land, send, recv, a, rel)
                cp.wait_send()
                cp.wait_recv()

    res = pl.pallas_call(
        body, name="scatter_wait",
        out_shape=[pltpu.HBM(a.shape, a.dtype) for a in arrays] + [pltpu.HBM(l.shape, l.dtype) for l in landing],
        in_specs=[_HBM_ONLY] * (2 * n) + [_SEM, _SEM] + [_HBM] * len(after),
        out_specs=[_HBM_ONLY] * (2 * n),
        input_output_aliases={i: i for i in range(2 * n)},
        compiler_params=pltpu.CompilerParams(has_side_effects=_DATAFLOW),
    )(*arrays, *landing, send_sems, recv_sems, *after)
    return res[n:]


def _ada_forward(c_row, w_ada, b_ada8, carry):
    cols = w_ada.shape[1]

    def body(c_ref, w_ref, b_ref, mod_ref, sc_ref, c_all, send_buf, recv_buf, send1, recv1, send2, recv2):
        x, y, c = _me()
        me = 4 * x + 2 * y + c
        rowi = lax.broadcasted_iota(jnp.int32, (8, D_MODEL), 0)
        c_all[me] = jnp.broadcast_to(c_ref[...], (8, D_MODEL))
        copies = []
        for rel in range(1, 8):
            to, _ = _peer(rel)
            cp = pltpu.make_async_remote_copy(src_ref=c_all.at[me], dst_ref=c_all.at[me], send_sem=send1.at[rel - 1],
                                              recv_sem=recv1.at[rel - 1], device_id=to, device_id_type=MESH)
            cp.start()
            copies.append(cp)
        for rel in range(1, 8):
            _, p = _peer(rel)
            pltpu.make_async_remote_copy(src_ref=c_all.at[p], dst_ref=c_all.at[p], send_sem=send1.at[rel - 1],
                                         recv_sem=recv1.at[rel - 1], device_id=(x, y, c), device_id_type=MESH).wait_recv()
        for cp in copies:
            cp.wait_send()
        cmat = jnp.zeros((8, D_MODEL), F32)
        for b in range(8):
            cmat = jnp.where(rowi == b, c_all[b], cmat)
        sc = cmat * _sigmoid(cmat)
        sc_ref[...] = sc
        modcols = _mm(sc, w_ref[...]) + b_ref[pl.ds(me, 1), :]
        for b in range(8):
            send_buf[b] = jnp.broadcast_to(modcols[b:b + 1, :], (8, cols))
        recv_buf[me] = send_buf[me]
        copies = []
        for rel in range(1, 8):
            to, p = _peer(rel)
            cp = pltpu.make_async_remote_copy(src_ref=send_buf.at[p], dst_ref=recv_buf.at[me], send_sem=send2.at[rel - 1],
                                              recv_sem=recv2.at[rel - 1], device_id=to, device_id_type=MESH)
            cp.start()
            copies.append(cp)
        for rel in range(1, 8):
            _, p = _peer(rel)
            pltpu.make_async_remote_copy(src_ref=send_buf.at[p], dst_ref=recv_buf.at[p], send_sem=send2.at[rel - 1],
                                         recv_sem=recv2.at[rel - 1], device_id=(x, y, c), device_id_type=MESH).wait_recv()
        for cp in copies:
            cp.wait_send()
        rowc = lax.broadcasted_iota(jnp.int32, (8, cols), 0)
        out = jnp.zeros((8, cols), F32)
        for k in range(8):
            out = jnp.where(rowc == k, recv_buf[k], out)
        mod_ref[...] = out

    return _launch(
        body, "ada_forward",
        out_shape=[jax.ShapeDtypeStruct((8, cols), F32), jax.ShapeDtypeStruct((8, D_MODEL), F32)],
        in_specs=[pl.BlockSpec(memory_space=pltpu.VMEM)] * 3,
        out_specs=[pl.BlockSpec(memory_space=pltpu.VMEM)] * 2,
        operands=(c_row, w_ada, b_ada8),
        scratch=[pltpu.VMEM((8, 8, D_MODEL), F32), pltpu.VMEM((8, 8, cols), F32), pltpu.VMEM((8, 8, cols), F32)]
        + [pltpu.SemaphoreType.DMA((7,))] * 4,
        carry=carry)


def _allreduce_small(pack, order, name):
    rows = pack.shape[1]

    def body(pack_ref, order_ref, total_ref, land_ref, send1, recv1, send2, recv2):
        x, y, c = _me()
        me = 4 * x + 2 * y + c
        land_ref[me] = pack_ref[me]
        copies = []
        for rel in range(1, 8):
            to, p = _peer(rel)
            cp = pltpu.make_async_remote_copy(src_ref=pack_ref.at[p], dst_ref=land_ref.at[me], send_sem=send1.at[rel - 1],
                                              recv_sem=recv1.at[rel - 1], device_id=to, device_id_type=MESH)
            cp.start()
            copies.append(cp)
        for rel in range(1, 8):
            _, p = _peer(rel)
            pltpu.make_async_remote_copy(src_ref=pack_ref.at[p], dst_ref=land_ref.at[p], send_sem=send1.at[rel - 1],
                                         recv_sem=recv1.at[rel - 1], device_id=(x, y, c), device_id_type=MESH).wait_recv()
        for cp in copies:
            cp.wait_send()
        acc = land_ref[0]
        for b in range(1, 8):
            acc = acc + land_ref[b]
        total_ref[me] = acc
        copies = []
        for rel in range(1, 8):
            to, _ = _peer(rel)
            cp = pltpu.make_async_remote_copy(src_ref=total_ref.at[me], dst_ref=total_ref.at[me], send_sem=send2.at[rel - 1],
                                              recv_sem=recv2.at[rel - 1], device_id=to, device_id_type=MESH)
            cp.start()
            copies.append(cp)
        for rel in range(1, 8):
            _, p = _peer(rel)
            pltpu.make_async_remote_copy(src_ref=total_ref.at[p], dst_ref=total_ref.at[p], send_sem=send2.at[rel - 1],
                                         recv_sem=recv2.at[rel - 1], device_id=(x, y, c), device_id_type=MESH).wait_recv()
        for cp in copies:
            cp.wait_send()

    return pl.pallas_call(
        body, name=name,
        out_shape=[jax.ShapeDtypeStruct((8, rows, 128), F32), jax.ShapeDtypeStruct((8, rows, 128), F32)],
        in_specs=[pl.BlockSpec(memory_space=pltpu.VMEM), _HBM],
        out_specs=[pl.BlockSpec(memory_space=pltpu.VMEM)] * 2,
        scratch_shapes=[pltpu.SemaphoreType.DMA((7,))] * 4,
        compiler_params=pltpu.CompilerParams(vmem_limit_bytes=VMEM_LIMIT),
    )(pack, order)


def _modulated(x, prm_ref, sub, g_row):
    shift, scale = _row(prm_ref, 3 * sub), _row(prm_ref, 3 * sub + 1)
    g = _row(prm_ref, g_row)
    r = _rms_scale(x)
    n0 = x * r
    return (n0 * g) * (1.0 + scale) + shift, r, n0


def _swiglu_tile(xv, prm_ref, win_ref, wout_ref, ab_ref, sub, g_row):
    h, _, _ = _modulated(xv, prm_ref, sub, g_row)
    hb = h.astype(MXU_DTYPE)
    acc = None
    for j in range(4):
        a = _mm_nt(hb, win_ref[0, j])
        b = _mm_nt(hb, win_ref[1, j])
        ab_ref[0, j] = a.astype(SAVE_DTYPE)
        ab_ref[1, j] = b.astype(SAVE_DTYPE)
        t = _mm((a * _sigmoid(a)) * b, wout_ref[j])
        acc = t if acc is None else acc + t
    return acc


def _loss_tile(xv, target, g):
    r = _rms_scale(xv)
    n0 = xv * r
    err = n0 * g - target
    dy = err / float(D_MODEL)
    dn0 = dy * g
    dx = r * (dn0 - n0 * jnp.mean(dn0 * n0, axis=-1, keepdims=True))
    loss = 0.5 * jnp.sum(jnp.mean(err * err, axis=-1, keepdims=True), axis=0, keepdims=True)
    return dx, _colsum(dy * n0), loss


def _ffn_forward_loss(x, target, prm, win, wout, sub, g_row, name):
    T = x.shape[0]
    tm = min(T, TM_FFN)

    def body(x_ref, t_ref, prm_ref, win_ref, wout_ref, dx_ref, sums_ref, f_ref, ab_ref):
        i = pl.program_id(0)
        xv = x_ref[...]
        acc = _swiglu_tile(xv, prm_ref, win_ref, wout_ref, ab_ref, sub, g_row)
        f_ref[...] = acc.astype(SAVE_DTYPE)
        dx, dg, loss = _loss_tile(xv + (0.5 * _row(prm_ref, 3 * sub + 2)) * acc, t_ref[...], _row(prm_ref, ROW_G_FINAL))
        dx_ref[...] = dx
        upd = jnp.concatenate([dg, jnp.broadcast_to(loss, (1, D_MODEL)), jnp.zeros((6, D_MODEL), F32)], axis=0)

        @pl.when(i == 0)
        def _():
            sums_ref[...] = upd

        @pl.when(i > 0)
        def _():
            sums_ref[...] += upd

    tok = pl.BlockSpec((tm, D_MODEL), lambda i: (i, 0))
    return _launch(
        body, name, grid=(T // tm,), semantics=("arbitrary",),
        out_shape=[jax.ShapeDtypeStruct((T, D_MODEL), F32), jax.ShapeDtypeStruct((8, D_MODEL), F32),
                   jax.ShapeDtypeStruct((T, D_MODEL), SAVE_DTYPE), jax.ShapeDtypeStruct((2, 4, T, FF_SHARD), SAVE_DTYPE)],
        in_specs=[tok, tok, _resident(prm), _resident(win), _resident(wout)],
        out_specs=[tok, pl.BlockSpec((8, D_MODEL), lambda i: (0, 0)), tok,
                   pl.BlockSpec((2, 4, tm, FF_SHARD), lambda i: (0, 0, i, 0))],
        operands=(x, target, prm, win, wout))


def _ffn_hidden(x, prm, win, sub, g_row, name, carry=None):
    T = x.shape[0]
    tm = min(T, TM_FFN)

    def body(x_ref, prm_ref, win_ref, ab_ref, s_ref):
        h, _, _ = _modulated(x_ref[...], prm_ref, sub, g_row)
        hb = h.astype(MXU_DTYPE)
        for j in range(4):
            a = _mm_nt(hb, win_ref[0, j])
            b = _mm_nt(hb, win_ref[1, j])
            ab_ref[0, j] = a.astype(SAVE_DTYPE)
            ab_ref[1, j] = b.astype(SAVE_DTYPE)
            s_ref[j] = ((a * _sigmoid(a)) * b).astype(MXU_DTYPE)

    return _launch(
        body, name, grid=(T // tm,), semantics=("arbitrary",),
        out_shape=[jax.ShapeDtypeStruct((2, 4, T, FF_SHARD), SAVE_DTYPE), jax.ShapeDtypeStruct((4, T, FF_SHARD), MXU_DTYPE)],
        in_specs=[pl.BlockSpec((tm, D_MODEL), lambda i: (i, 0)), _resident(prm), _resident(win)],
        out_specs=[pl.BlockSpec((2, 4, tm, FF_SHARD), lambda i: (0, 0, i, 0)),
                   pl.BlockSpec((4, tm, FF_SHARD), lambda i: (0, i, 0))],
        operands=(x, prm, win), carry=carry, steps=_grid_steps(T // tm))


def _ffn_out(x, s, prm, wout, sub, name, carry=None):
    T = x.shape[0]
    tm = min(T, TM_FFN)

    def body(x_ref, s_ref, prm_ref, wout_ref, xo_ref, f_ref):
        acc = None
        for j in range(4):
            t = _mm(s_ref[j], wout_ref[j])
            acc = t if acc is None else acc + t
        f_ref[...] = acc.astype(SAVE_DTYPE)
        xo_ref[...] = x_ref[...] + (0.5 * _row(prm_ref, 3 * sub + 2)) * acc

    tok = pl.BlockSpec((tm, D_MODEL), lambda i: (i, 0))
    return _launch(
        body, name, grid=(T // tm,), semantics=("arbitrary",),
        out_shape=[jax.ShapeDtypeStruct((T, D_MODEL), F32), jax.ShapeDtypeStruct((T, D_MODEL), SAVE_DTYPE)],
        in_specs=[tok, pl.BlockSpec((4, tm, FF_SHARD), lambda i: (0, i, 0)), _resident(prm), _resident(wout)],
        out_specs=[tok, tok], operands=(x, s, prm, wout), carry=carry, steps=_grid_steps(T // tm))


def _ffn_backward(d, ab, prm, win, wout, sub, name, carry=None):
    T = d.shape[0]
    tm = min(T, TM_FFN_BWD)
    nt = T // tm
    chunk = min(tm, FFN_BWD_CHUNK)

    def body(d_ref, ab_ref, prm_ref, win_ref, wout_ref, dh_ref, dab_ref, dwout_ref, acc_out):
        i = pl.program_id(1)

        @pl.when(i == 0)
        def _():
            acc_out[...] = jnp.zeros_like(acc_out)

        wa, wb, wo = win_ref[0, 0], win_ref[1, 0], wout_ref[0]
        half_gate = 0.5 * _row(prm_ref, 3 * sub + 2)
        ss, dfss = [], []
        for ck in range(tm // chunk):
            rows = slice(ck * chunk, (ck + 1) * chunk)
            a = ab_ref[0, 0, rows, :].astype(F32)
            b = ab_ref[1, 0, rows, :].astype(F32)
            sg = _sigmoid(a)
            si = a * sg
            dfs = (half_gate * d_ref[rows, :]).astype(MXU_DTYPE)
            ds = _mm_nt(dfs, wo)
            da = (ds * b * (sg * (1.0 + a * (1.0 - sg)))).astype(MXU_DTYPE)
            db = (ds * si).astype(MXU_DTYPE)
            dh_ref[0, rows, :] = (_mm(da, wa) + _mm(db, wb)).astype(SAVE_DTYPE)
            dab_ref[0, 0, rows, :] = da
            dab_ref[1, 0, rows, :] = db
            ss.append((si * b).astype(MXU_DTYPE))
            dfss.append(dfs)
        cat = (lambda v: v[0]) if len(ss) == 1 else (lambda v: jnp.concatenate(v, axis=0))
        acc_out[...] += _mm_tn(cat(ss), cat(dfss))

        @pl.when(i == nt - 1)
        def _():
            dwout_ref[0] = acc_out[...].astype(WIRE_DTYPE)

    def steps():
        j, i = pl.program_id(0), pl.program_id(1)
        return (j == 0) & (i == 0), (j == 2) & (i == 0), (j == 3) & (i == nt - 1)

    pre = pl.BlockSpec((2, 1, tm, FF_SHARD), lambda j, i: (0, j, i, 0))
    return _launch(
        body, name, grid=(4, nt), semantics=("arbitrary", "arbitrary"),
        out_shape=[jax.ShapeDtypeStruct((4, T, D_MODEL), SAVE_DTYPE), jax.ShapeDtypeStruct(ab.shape, MXU_DTYPE),
                   jax.ShapeDtypeStruct(wout.shape, WIRE_DTYPE)],
        in_specs=[pl.BlockSpec((tm, D_MODEL), lambda j, i: (i, 0)), pre, _resident(prm),
                  pl.BlockSpec((2, 1, FF_SHARD, D_MODEL), lambda j, i: (0, j, 0, 0)),
                  pl.BlockSpec((1, FF_SHARD, D_MODEL), lambda j, i: (j, 0, 0))],
        out_specs=[pl.BlockSpec((1, tm, D_MODEL), lambda j, i: (j, i, 0)), pre,
                   pl.BlockSpec((1, FF_SHARD, D_MODEL), lambda j, i: (j, 0, 0))],
        operands=(d, ab, prm, win, wout), scratch=[pltpu.VMEM((FF_SHARD, D_MODEL), F32)], carry=carry, steps=steps)


def _ffn_dwin(x, dab, prm, sub, g_row, name, carry=None):
    T = x.shape[0]
    tm = min(T, TM_DWIN)
    nt = T // tm

    def body(x_ref, dab_ref, prm_ref, dwin_ref, acc):
        i = pl.program_id(1)

        @pl.when(i == 0)
        def _():
            acc[...] = jnp.zeros_like(acc)

        h, _, _ = _modulated(x_ref[...], prm_ref, sub, g_row)
        hb = h.astype(MXU_DTYPE)
        acc[0] += _mm_tn(dab_ref[0, 0], hb)
        acc[1] += _mm_tn(dab_ref[1, 0], hb)

        @pl.when(i == nt - 1)
        def _():
            dwin_ref[0, 0] = acc[0].astype(WIRE_DTYPE)
            dwin_ref[1, 0] = acc[1].astype(WIRE_DTYPE)

    def steps():
        j, i = pl.program_id(0), pl.program_id(1)
        return (j == 0) & (i == 0), (j == 2) & (i == 0), (j == 3) & (i == nt - 1)

    return _launch(
        body, name, grid=(4, nt), semantics=("arbitrary", "arbitrary"),
        out_shape=[jax.ShapeDtypeStruct((2, 4, FF_SHARD, D_MODEL), WIRE_DTYPE)],
        in_specs=[pl.BlockSpec((tm, D_MODEL), lambda j, i: (i, 0)),
                  pl.BlockSpec((2, 1, tm, FF_SHARD), lambda j, i: (0, j, i, 0)), _resident(prm)],
        out_specs=[pl.BlockSpec((2, 1, FF_SHARD, D_MODEL), lambda j, i: (0, j, 0, 0))],
        operands=(x, dab, prm), scratch=[pltpu.VMEM((2, FF_SHARD, D_MODEL), F32)], carry=carry, steps=steps)


def _norm_backward_tile(dh, xv, dv, fv, prm_ref, sub, g_row, gate_coef):
    scale, g = _row(prm_ref, 3 * sub + 1), _row(prm_ref, g_row)
    r = _rms_scale(xv)
    n0 = xv * r
    dn = dh * (1.0 + scale)
    dn0 = dn * g
    dx = dv + r * (dn0 - n0 * jnp.mean(dn0 * n0, axis=-1, keepdims=True))
    upd = jnp.concatenate([_colsum(dn * n0), _colsum(dh), _colsum(dh * (n0 * g)),
                           gate_coef * _colsum(dv * fv.astype(F32)), jnp.zeros((4, D_MODEL), F32)], axis=0)
    return dx, upd


def _norm_backward(parts, x, d, f, prm, sub, g_row, gate_coef, name):
    T = x.shape[0]
    tm = min(T, TM_EW)
    P = parts.shape[0]

    def body(p_ref, x_ref, d_ref, f_ref, prm_ref, dx_ref, sums_ref):
        i = pl.program_id(0)
        dh = p_ref[0].astype(F32)
        for k in range(1, P):
            dh = dh + p_ref[k].astype(F32)
        dx_ref[...], upd = _norm_backward_tile(dh, x_ref[...], d_ref[...], f_ref[...], prm_ref, sub, g_row, gate_coef)

        @pl.when(i == 0)
        def _():
            sums_ref[...] = upd

        @pl.when(i > 0)
        def _():
            sums_ref[...] += upd

    tok = pl.BlockSpec((tm, D_MODEL), lambda i: (i, 0))
    return _launch(
        body, name, grid=(T // tm,), semantics=("arbitrary",),
        out_shape=[jax.ShapeDtypeStruct((T, D_MODEL), F32), jax.ShapeDtypeStruct((8, D_MODEL), F32)],
        in_specs=[pl.BlockSpec((P, tm, D_MODEL), lambda i: (0, i, 0)), tok, tok, tok, _resident(prm)],
        out_specs=[tok, pl.BlockSpec((8, D_MODEL), lambda i: (0, 0))],
        operands=(parts, x, d, f, prm))


def _ssm_discretise(lam_re_log, lam_im, log_dt):
    lr = -jnp.exp(lam_re_log)
    dt = jnp.exp(log_dt)
    mag = jnp.exp(lr * dt)
    ang = lam_im * dt
    ab_re = mag * jnp.cos(ang)
    ab_im = mag * jnp.sin(ang)
    num_re = ab_re - 1.0
    num_im = ab_im
    den = lr * lr + lam_im * lam_im
    f_re = (num_re * lr + num_im * lam_im) / den
    f_im = (num_im * lr - num_re * lam_im) / den
    return ab_re, ab_im, f_re, f_im


def _ssm_params_forward(lam_re_log, lam_im, log_dt):
    def body(a_ref, b_ref, c_ref, o0, o1, o2, o3):
        outs = _ssm_discretise(a_ref[...], b_ref[...], c_ref[...])
        for o, v in zip((o0, o1, o2, o3), outs):
            o[...] = v

    return pl.pallas_call(body, name="ssm_params_forward",
                          out_shape=[jax.ShapeDtypeStruct(lam_im.shape, F32)] * 4)(lam_re_log, lam_im, log_dt)


def _ssm_params_backward(lam_re_log, lam_im, log_dt, cot):
    def body(a_ref, b_ref, c_ref, g0, g1, g2, g3, o0, o1, o2):
        _, vjp = jax.vjp(_ssm_discretise, a_ref[...], b_ref[...], c_ref[...])
        d0, d1, d2 = vjp((g0[...], g1[...], g2[...], g3[...]))
        o0[...] = d0
        o1[...] = d1
        o2[...] = d2

    return pl.pallas_call(
        body, name="ssm_params_backward",
        out_shape=[jax.ShapeDtypeStruct(lam_im.shape, F32), jax.ShapeDtypeStruct(lam_im.shape, F32),
                   jax.ShapeDtypeStruct(log_dt.shape, F32)])(lam_re_log, lam_im, log_dt, *cot)


def _ssm_dense_forward(srow, b_dense, c_dense):
    def body(srow_ref, bd_ref, cd_ref, bb_ref, ct_ref):
        for j in range(SSM_BLOCKS):
            lanes = slice(j * SSM_BLOCK_STATE, (j + 1) * SSM_BLOCK_STATE)
            f_re, f_im = srow_ref[2:3, lanes], srow_ref[3:4, lanes]
            bb_ref[0, j] = (f_re * bd_ref[0, j] - f_im * bd_ref[1, j]).astype(MXU_DTYPE)
            bb_ref[1, j] = (f_re * bd_ref[1, j] + f_im * bd_ref[0, j]).astype(MXU_DTYPE)
            ct_ref[0, j] = cd_ref[0, j].astype(MXU_DTYPE)
            ct_ref[1, j] = cd_ref[1, j].astype(MXU_DTYPE)

    return pl.pallas_call(body, name="ssm_dense_forward",
                          out_shape=[jax.ShapeDtypeStruct(b_dense.shape, MXU_DTYPE),
                                     jax.ShapeDtypeStruct(c_dense.shape, MXU_DTYPE)],
                          compiler_params=pltpu.CompilerParams(vmem_limit_bytes=VMEM_LIMIT))(srow, b_dense, c_dense)


def _unrolled(tile_step):
    def body(trip, carry):
        for u in range(SCAN_UNROLL):
            carry = tile_step(trip * SCAN_UNROLL + u, carry)
        return carry
    return body


def _cmul(p, q):
    return p[0] * q[0] - p[1] * q[1], p[0] * q[1] + p[1] * q[0]


def _scan_coefficients(ar, ai, reverse):
    n = ar.shape[1]
    p = {1: (ar, ai)}
    p[2] = _cmul(p[1], p[1])
    p[3] = _cmul(p[2], p[1])
    p[4] = _cmul(p[2], p[2])
    p[5] = _cmul(p[4], p[1])
    p[6] = _cmul(p[4], p[2])
    p[7] = _cmul(p[4], p[3])
    p[8] = _cmul(p[4], p[4])
    rowi = lax.broadcasted_iota(jnp.int32, (SCAN_ROWS, n), 0)
    tiles = []
    for dstep in (1, 2, 4):
        keep = (rowi < SCAN_ROWS - dstep) if reverse else (rowi >= dstep)
        for part in p[dstep]:
            tiles.append(jnp.where(keep, jnp.broadcast_to(part, (SCAN_ROWS, n)), 0.0))
    for comp in (0, 1):
        t = jnp.zeros((SCAN_ROWS, n), F32)
        for rr in range(SCAN_ROWS):
            power = SCAN_ROWS - rr if reverse else rr + 1
            t = jnp.where(rowi == rr, jnp.broadcast_to(p[power][comp], (SCAN_ROWS, n)), t)
        tiles.append(t)
    return tiles


def _load_stack(stack_hbm, dst, sems, base):
    cols = stack_hbm.shape[2]
    cps = [pltpu.make_async_copy(stack_hbm.at[k], dst.at[:, pl.ds(k * cols, cols)], sems.at[base + k])
           for k in range(NDEV)]
    for cp in cps:
        cp.start()
    return cps


def _window_lanes():
    lane = lax.broadcasted_iota(jnp.int32, (1, POOL_WIDTH), 1)
    return jnp.where(lane < 128, 2.0, jnp.where(lane < 256, 4.0, jnp.where(lane < 384, 8.0, 16.0)))


def _gelu(y):
    return 0.5 * y * (1.0 + lax.erf(y * 0.7071067811865476))


def _gelu_grad(y):
    return 0.5 * (1.0 + lax.erf(y * 0.7071067811865476)) + y * jnp.exp(-0.5 * y * y) * 0.3989422804014327


def _mixer_forward(x, prm, w_in_s, w_pu_s, w_glu_s, w_su_s, w_out, pool_w, mvec, srow, bb, ct, carry=None):
    T = x.shape[0]
    tm = min(T, TM_MIX)
    nt = T // tm
    n_tiles = tm // SCAN_ROWS

    def body(x_ref, prm_ref, w_in_h, w_pu_h, w_glu_h, w_su_h, w_out_h, pw_ref, mv_ref, srow_ref, bb, ct,
             x2_ref, mo_ref, z_ref, sre_ref, sim_ref, zp_ref, q_ref, yp_ref, yss_ref, vg_ref, ys_ref,
             w_in, w_pu, w_glu, w_su, w_o, coef, carry, hist, bu, sems):
        i = pl.program_id(0)

        @pl.when(i == 0)
        def _():
            cps = (_load_stack(w_in_h, w_in, sems, 0) + _load_stack(w_pu_h, w_pu, sems, 8)
                   + _load_stack(w_glu_h, w_glu, sems, 16) + _load_stack(w_su_h, w_su, sems, 24))
            cps.append(pltpu.make_async_copy(w_out_h, w_o, sems.at[32]))
            cps[-1].start()
            for j in range(SSM_BLOCKS):
                lanes = slice(j * SSM_BLOCK_STATE, (j + 1) * SSM_BLOCK_STATE)
                for k, t in enumerate(_scan_coefficients(srow_ref[0:1, lanes], srow_ref[1:2, lanes], False)):
                    coef[j, k] = t
            carry[...] = jnp.zeros_like(carry)
            hist[...] = jnp.zeros_like(hist)
            for cp in cps:
                cp.wait()

        xv = x_ref[...]
        h, _, _ = _modulated(xv, prm_ref, 1, ROW_G_MIX)
        z = _mm(h, w_in[...])
        z_ref[...] = z.astype(SAVE_DTYPE)
        u_pool, u_ssm = z[:, 0:512], z[:, 512:1024]
        gl_pool, gl_ssm = z[:, 1024:2048], z[:, 2048:3072]

        ext = jnp.concatenate([hist[...], u_pool], axis=0)
        w2 = ext + pltpu.roll(ext, 1, 0)
        w4 = w2[:, 128:] + pltpu.roll(w2[:, 128:], 2, 0)
        w8 = w4[:, 128:] + pltpu.roll(w4[:, 128:], 4, 0)
        w16 = w8[:, 128:] + pltpu.roll(w8[:, 128:], 8, 0)
        wsum = jnp.concatenate([w2[POOL_HALO:, :128], w4[POOL_HALO:, :128], w8[POOL_HALO:, :128], w16[POOL_HALO:]], axis=1)
        hist[...] = u_pool[tm - POOL_HALO:, :]
        t1 = (lax.broadcasted_iota(jnp.int32, (tm, 1), 0) + (i * tm + 1)).astype(F32)
        zp = wsum / jnp.minimum(t1, _window_lanes()) - u_pool
        zp_ref[...] = zp.astype(SAVE_DTYPE)
        q = jnp.concatenate([_mm(zp[:, k * 128:(k + 1) * 128], pw_ref[k]) for k in range(4)], axis=1)
        q = q + mv_ref[ROW_POOL_B:ROW_POOL_B + 1, 0:512]
        q_ref[...] = q.astype(SAVE_DTYPE)
        y_pool = _mm(q * mv_ref[ROW_POOL_SCALE:ROW_POOL_SCALE + 1, 0:512], w_pu[...])
        yp_ref[...] = y_pool.astype(SAVE_DTYPE)

        y_blocks = []
        for j in range(SSM_BLOCKS):
            lanes = pl.ds(j * SSM_BLOCK_STATE, SSM_BLOCK_STATE)
            ub = u_ssm[:, j * 128:(j + 1) * 128].astype(MXU_DTYPE)
            bu[0] = _mm(ub, bb[0, j])
            bu[1] = _mm(ub, bb[1, j])
            a1r, a1i, a2r, a2i, a4r, a4i, pr, pi = [coef[j, k] for k in range(8)]

            def step(tt, c, lanes=lanes, a1r=a1r, a1i=a1i, a2r=a2r, a2i=a2i, a4r=a4r, a4i=a4i, pr=pr, pi=pi):
                cr, ci = c
                rows = pl.ds(pl.multiple_of(tt * SCAN_ROWS, SCAN_ROWS), SCAN_ROWS)
                xr, xi = bu[0, rows, :], bu[1, rows, :]
                for dstep, kr, ki in ((1, a1r, a1i), (2, a2r, a2i), (4, a4r, a4i)):
                    sr, si = pltpu.roll(xr, dstep, 0), pltpu.roll(xi, dstep, 0)
                    xr, xi = xr + kr * sr - ki * si, xi + kr * si + ki * sr
                xr, xi = xr + pr * cr - pi * ci, xi + pr * ci + pi * cr
                sre_ref[rows, lanes] = xr
                sim_ref[rows, lanes] = xi
                return (jnp.broadcast_to(xr[SCAN_ROWS - 1:SCAN_ROWS, :], xr.shape),
                        jnp.broadcast_to(xi[SCAN_ROWS - 1:SCAN_ROWS, :], xi.shape))

            cr, ci = lax.fori_loop(0, n_tiles // SCAN_UNROLL, _unrolled(step), (carry[j, 0], carry[j, 1]))
            carry[j, 0] = cr
            carry[j, 1] = ci
            y_blocks.append(_mm(sre_ref[:, lanes], ct[0, j]) - _mm(sim_ref[:, lanes], ct[1, j]))
        yss = jnp.concatenate(y_blocks, axis=1) + mv_ref[ROW_SSM_D:ROW_SSM_D + 1, 0:512] * u_ssm
        yss_ref[...] = yss.astype(SAVE_DTYPE)
        vg = _mm(_gelu(yss), w_glu[...]) + mv_ref[ROW_B_GLU:ROW_B_GLU + 1, :]
        vg_ref[...] = vg.astype(SAVE_DTYPE)
        y_ssm = _mm(vg[:, 0:512] * _sigmoid(vg[:, 512:1024]), w_su[...])
        ys_ref[...] = y_ssm.astype(SAVE_DTYPE)

        merged = _sigmoid(gl_pool) * y_pool + _sigmoid(gl_ssm) * y_ssm
        mo = _mm(merged, w_o[...])
        mo_ref[...] = mo.astype(SAVE_DTYPE)
        x2_ref[...] = xv + _row(prm_ref, 5) * mo

    def tok(width):
        return pl.BlockSpec((tm, width), lambda i: (i, 0))

    hbm = _HBM
    widths = (D_MODEL, D_MODEL, IN_WIDTH, N_STATE, N_STATE, 512, 512, D_MODEL, 512, D_MODEL, D_MODEL)
    dtypes = (F32, SAVE_DTYPE, SAVE_DTYPE, F32, F32) + (SAVE_DTYPE,) * 6
    return _launch(
        body, "mixer_forward", grid=(nt,), semantics=("arbitrary",), carry=carry, steps=_grid_steps(nt),
        out_shape=[jax.ShapeDtypeStruct((T, w), dt) for w, dt in zip(widths, dtypes)],
        in_specs=[tok(D_MODEL), _resident(prm), hbm, hbm, hbm, hbm, hbm, _resident(pool_w), _resident(mvec),
                  _resident(srow), _resident(bb), _resident(ct)],
        out_specs=[tok(w) for w in widths],
        operands=(x, prm, w_in_s, w_pu_s, w_glu_s, w_su_s, w_out, pool_w, mvec, srow, bb, ct),
        scratch=[
            pltpu.VMEM((D_MODEL, IN_WIDTH), MXU_DTYPE), pltpu.VMEM((512, D_MODEL), MXU_DTYPE),
            pltpu.VMEM((512, D_MODEL), MXU_DTYPE), pltpu.VMEM((512, D_MODEL), MXU_DTYPE),
            pltpu.VMEM((D_MODEL, D_MODEL), MXU_DTYPE),
            pltpu.VMEM((SSM_BLOCKS, 8, SCAN_ROWS, SSM_BLOCK_STATE), F32),
            pltpu.VMEM((SSM_BLOCKS, 2, SCAN_ROWS, SSM_BLOCK_STATE), F32),
            pltpu.VMEM((POOL_HALO, POOL_WIDTH), F32),
            pltpu.VMEM((2, tm, SSM_BLOCK_STATE), F32),
            pltpu.SemaphoreType.DMA((33,)),
        ])


def _mixer_backward(d2, prm, saved, w_pu_s, w_glu_s, w_su_s, w_out, pool_w, mvec, srow, bb, ct, carry=None):
    z, s_re, s_im, zp, q, y_pool, yss, vg, y_ssm = saved
    T = d2.shape[0]
    tm = min(T, TM_MIX_BWD)
    nt = T // tm
    n_tiles = tm // SCAN_ROWS

    def body(d_ref, prm_ref, z_ref, sre_ref, sim_ref, zp_ref, q_ref, yp_ref, yss_ref, vg_ref, ys_ref,
             w_pu_h, w_glu_h, w_su_h, w_out_h, pw_ref, mv_ref, srow_ref, bb, ct,
             dz_ref, dwo_h, dwpu_h, dwglu_h, dwsu_h, dpw_h, dbb_h, dct_h, vsum_h, da_h,
             w_pu, w_glu, w_su, w_o, pwb, coef, carry, hist, dre, lam,
             a_wo, a_wpu, a_wglu, a_wsu, a_pw, a_bb, a_ct, a_vs, a_da, st_wo, st_up, sems):
        i = pl.program_id(0)
        tile = nt - 1 - i

        @pl.when(i == 0)
        def _():
            cps = (_load_stack(w_pu_h, w_pu, sems, 0) + _load_stack(w_glu_h, w_glu, sems, 8)
                   + _load_stack(w_su_h, w_su, sems, 16))
            cps.append(pltpu.make_async_copy(w_out_h, w_o, sems.at[24]))
            cps[-1].start()
            pwb[...] = pw_ref[...].astype(MXU_DTYPE)
            for j in range(SSM_BLOCKS):
                lanes = slice(j * SSM_BLOCK_STATE, (j + 1) * SSM_BLOCK_STATE)
                for k, t in enumerate(_scan_coefficients(srow_ref[0:1, lanes], srow_ref[1:2, lanes], True)):
                    coef[j, k] = t
            for acc in (carry, hist, a_wo, a_wpu, a_wglu, a_wsu, a_pw, a_bb, a_ct, a_vs, a_da):
                acc[...] = jnp.zeros_like(acc)
            for cp in cps:
                cp.wait()

        dv = d_ref[...]
        zt = z_ref[...].astype(F32)
        u_ssm, gl_pool, gl_ssm = zt[:, 512:1024], zt[:, 1024:2048], zt[:, 2048:3072]
        y_p, y_s = yp_ref[...].astype(F32), ys_ref[...].astype(F32)
        sgp, sgs = _sigmoid(gl_pool), _sigmoid(gl_ssm)
        dmo = (_row(prm_ref, 5) * dv).astype(MXU_DTYPE)
        a_wo[...] += _mm_tn(sgp * y_p + sgs * y_s, dmo)
        dmerged = _mm_nt(dmo, w_o[...])
        dy_pool = dmerged * sgp
        dgl_pool = dmerged * y_p * (sgp * (1.0 - sgp))
        dy_ssm = dmerged * sgs
        dgl_ssm = dmerged * y_s * (sgs * (1.0 - sgs))

        scale = mv_ref[ROW_POOL_SCALE:ROW_POOL_SCALE + 1, 0:512]
        qv, zpv = q_ref[...].astype(F32), zp_ref[...]
        a_wpu[...] += _mm_tn(qv * scale, dy_pool)
        dp = _mm_nt(dy_pool, w_pu[...])
        dq = dp * scale
        a_vs[0:1, 0:512] += _colsum(dp * qv)
        a_vs[1:2, 0:512] += _colsum(dq)
        dzp_blocks = []
        for k in range(4):
            lanes = slice(k * 128, (k + 1) * 128)
            dzp_blocks.append(_mm_nt(dq[:, lanes], pwb[k]))
            a_pw[k] += _mm_tn(zpv[:, lanes], dq[:, lanes])
        dzp = jnp.concatenate(dzp_blocks, axis=1)
        t1 = (lax.broadcasted_iota(jnp.int32, (tm, 1), 0) + (tile * tm + 1)).astype(F32)
        gs = dzp / jnp.minimum(t1, _window_lanes())
        n_ext = tm + POOL_HALO
        ext = jnp.concatenate([gs, hist[...]], axis=0)
        v2 = ext + pltpu.roll(ext, n_ext - 1, 0)
        v4 = v2[:, 128:] + pltpu.roll(v2[:, 128:], n_ext - 2, 0)
        v8 = v4[:, 128:] + pltpu.roll(v4[:, 128:], n_ext - 4, 0)
        v16 = v8[:, 128:] + pltpu.roll(v8[:, 128:], n_ext - 8, 0)
        msum = jnp.concatenate([v2[:tm, :128], v4[:tm, :128], v8[:tm, :128], v16[:tm]], axis=1)
        hist[...] = gs[0:POOL_HALO, :]
        du_pool = msum - dzp

        vgv = vg_ref[...].astype(F32)
        val, gate = vgv[:, 0:512], vgv[:, 512:1024]
        sgg = _sigmoid(gate)
        a_wsu[...] += _mm_tn(val * sgg, dy_ssm)
        do = _mm_nt(dy_ssm, w_su[...])
        dvg = jnp.concatenate([do * sgg, do * val * (sgg * (1.0 - sgg))], axis=1)
        a_vs[3:4, :] += _colsum(dvg)
        yv = yss_ref[...].astype(F32)
        a_wglu[...] += _mm_tn(_gelu(yv), dvg)
        dyss = _mm_nt(dvg, w_glu[...]) * _gelu_grad(yv)
        a_vs[2:3, 0:512] += _colsum(dyss * u_ssm)
        du_blocks = []
        for j in range(SSM_BLOCKS):
            lanes = pl.ds(j * SSM_BLOCK_STATE, SSM_BLOCK_STATE)
            in_lanes = slice(j * 128, (j + 1) * 128)
            dyb = dyss[:, in_lanes].astype(MXU_DTYPE)
            ub = u_ssm[:, in_lanes].astype(MXU_DTYPE)
            dre[0] = _mm_nt(dyb, ct[0, j])
            dre[1] = -_mm_nt(dyb, ct[1, j])
            a_ct[0, j] += _mm_tn(sre_ref[:, lanes], dyb)
            a_ct[1, j] -= _mm_tn(sim_ref[:, lanes], dyb)
            a1r, a1i, a2r, a2i, a4r, a4i, pr, pi = [coef[j, k] for k in range(8)]
            rowi = lax.broadcasted_iota(jnp.int32, (SCAN_ROWS, SSM_BLOCK_STATE), 0)

            def step(tt, c, lanes=lanes, a1r=a1r, a1i=a1i, a2r=a2r, a2i=a2i, a4r=a4r, a4i=a4i, pr=pr, pi=pi, rowi=rowi):
                cr, ci, acc_r, acc_i = c
                rows = pl.ds(pl.multiple_of((n_tiles - 1 - tt) * SCAN_ROWS, SCAN_ROWS), SCAN_ROWS)
                xr, xi = dre[0, rows, :], dre[1, rows, :]
                for dstep, kr, ki in ((1, a1r, a1i), (2, a2r, a2i), (4, a4r, a4i)):
                    sr, si = pltpu.roll(xr, SCAN_ROWS - dstep, 0), pltpu.roll(xi, SCAN_ROWS - dstep, 0)
                    xr, xi = xr + kr * sr + ki * si, xi + kr * si - ki * sr
                xr, xi = xr + pr * cr + pi * ci, xi + pr * ci - pi * cr
                lam[0, rows, :] = xr
                lam[1, rows, :] = xi
                nr = jnp.where(rowi == SCAN_ROWS - 1, cr, pltpu.roll(xr, SCAN_ROWS - 1, 0))
                ni = jnp.where(rowi == SCAN_ROWS - 1, ci, pltpu.roll(xi, SCAN_ROWS - 1, 0))
                s_r, s_i = sre_ref[rows, lanes], sim_ref[rows, lanes]
                acc_r = acc_r + nr * s_r + ni * s_i
                acc_i = acc_i + ni * s_r - nr * s_i
                return (jnp.broadcast_to(xr[0:1, :], xr.shape), jnp.broadcast_to(xi[0:1, :], xi.shape), acc_r, acc_i)

            cr, ci, acc_r, acc_i = lax.fori_loop(0, n_tiles // SCAN_UNROLL, _unrolled(step),
                                                 (carry[j, 0], carry[j, 1], a_da[0, j], a_da[1, j]))
            carry[j, 0] = cr
            carry[j, 1] = ci
            a_da[0, j] = acc_r
            a_da[1, j] = acc_i
            lr_b, li_b = lam[0].astype(MXU_DTYPE), lam[1].astype(MXU_DTYPE)
            a_bb[0, j] += _mm_tn(ub, lr_b)
            a_bb[1, j] += _mm_tn(ub, li_b)
            du_blocks.append(_mm_nt(lr_b, bb[0, j]) + _mm_nt(li_b, bb[1, j]))
        du_ssm = jnp.concatenate(du_blocks, axis=1) + dyss * mv_ref[ROW_SSM_D:ROW_SSM_D + 1, 0:512]
        dz_ref[...] = jnp.concatenate([du_pool, du_ssm, dgl_pool, dgl_ssm], axis=1).astype(SAVE_DTYPE)

        @pl.when(i == nt - 1)
        def _():
            rows = D_MODEL // NDEV
            for k in range(NDEV):
                st_wo[k] = a_wo[k * rows:(k + 1) * rows, :].astype(WIRE_DTYPE)
                for a, acc in enumerate((a_wpu, a_wglu, a_wsu)):
                    st_up[a, k] = acc[:, k * 128:(k + 1) * 128].astype(WIRE_DTYPE)
            outs = ((st_wo, dwo_h), (st_up.at[0], dwpu_h), (st_up.at[1], dwglu_h), (st_up.at[2], dwsu_h),
                    (a_pw, dpw_h), (a_bb, dbb_h), (a_ct, dct_h), (a_vs, vsum_h), (a_da, da_h))
            cps = [pltpu.make_async_copy(src, dst, sems.at[k]) for k, (src, dst) in enumerate(outs)]
            for cp in cps:
                cp.start()
            for cp in cps:
                cp.wait()

    def tok(width):
        return pl.BlockSpec((tm, width), lambda i: (nt - 1 - i, 0))

    hbm = _HBM
    acc_shapes = [(D_MODEL, D_MODEL), (512, D_MODEL), (512, D_MODEL), (512, D_MODEL), (4, 128, 128),
                  (2, SSM_BLOCKS, 128, SSM_BLOCK_STATE), (2, SSM_BLOCKS, SSM_BLOCK_STATE, 128), (8, D_MODEL),
                  (2, SSM_BLOCKS, SCAN_ROWS, SSM_BLOCK_STATE)]
    stack_out = [jax.ShapeDtypeStruct((NDEV, D_MODEL // NDEV, D_MODEL), WIRE_DTYPE)] \
        + [jax.ShapeDtypeStruct((NDEV, 512, 128), WIRE_DTYPE)] * 3
    return _launch(
        body, "mixer_backward", grid=(nt,), semantics=("arbitrary",), carry=carry, steps=_grid_steps(nt),
        out_shape=[jax.ShapeDtypeStruct((T, IN_WIDTH), SAVE_DTYPE)] + stack_out
        + [jax.ShapeDtypeStruct(s, F32) for s in acc_shapes[4:]],
        in_specs=[tok(D_MODEL), _resident(prm), tok(IN_WIDTH), tok(N_STATE), tok(N_STATE), tok(512), tok(512),
                  tok(D_MODEL), tok(512), tok(D_MODEL), tok(D_MODEL), hbm, hbm, hbm, hbm, _resident(pool_w),
                  _resident(mvec), _resident(srow), _resident(bb), _resident(ct)],
        out_specs=[tok(IN_WIDTH)] + [hbm] * len(acc_shapes),
        operands=(d2, prm, z, s_re, s_im, zp, q, y_pool, yss, vg, y_ssm, w_pu_s, w_glu_s, w_su_s, w_out, pool_w, mvec,
                  srow, bb, ct),
        scratch=[
            pltpu.VMEM((512, D_MODEL), MXU_DTYPE), pltpu.VMEM((512, D_MODEL), MXU_DTYPE),
            pltpu.VMEM((512, D_MODEL), MXU_DTYPE), pltpu.VMEM((D_MODEL, D_MODEL), MXU_DTYPE),
            pltpu.VMEM((4, 128, 128), MXU_DTYPE),
            pltpu.VMEM((SSM_BLOCKS, 8, SCAN_ROWS, SSM_BLOCK_STATE), F32),
            pltpu.VMEM((SSM_BLOCKS, 2, SCAN_ROWS, SSM_BLOCK_STATE), F32),
            pltpu.VMEM((POOL_HALO, POOL_WIDTH), F32),
            pltpu.VMEM((2, tm, SSM_BLOCK_STATE), F32), pltpu.VMEM((2, tm, SSM_BLOCK_STATE), F32),
        ] + [pltpu.VMEM(s, F32) for s in acc_shapes]
        + [pltpu.VMEM((NDEV, D_MODEL // NDEV, D_MODEL), WIRE_DTYPE), pltpu.VMEM((3, NDEV, 512, 128), WIRE_DTYPE),
           pltpu.SemaphoreType.DMA((25,))])


def _mixer_in_backward(x, dz, d, mo, prm, w_in_s, carry=None):
    T = x.shape[0]
    tm = min(T, TM_MIX_IN)
    nt = T // tm
    cols = IN_WIDTH // NDEV

    def body(x_ref, dz_ref, d_ref, mo_ref, prm_ref, w_in_h, dx_ref, sums_ref, dw_ref, w_in, acc, sems):
        i = pl.program_id(0)

        @pl.when(i == 0)
        def _():
            cps = _load_stack(w_in_h, w_in, sems, 0)
            acc[...] = jnp.zeros_like(acc)
            for cp in cps:
                cp.wait()

        xv = x_ref[...]
        h, _, _ = _modulated(xv, prm_ref, 1, ROW_G_MIX)
        dzb = dz_ref[...].astype(MXU_DTYPE)
        acc[...] += _mm_tn(h, dzb)
        dx_ref[...], upd = _norm_backward_tile(_mm_nt(dzb, w_in[...]), xv, d_ref[...], mo_ref[...], prm_ref, 1,
                                               ROW_G_MIX, 1.0)

        @pl.when(i == 0)
        def _():
            sums_ref[...] = upd

        @pl.when(i > 0)
        def _():
            sums_ref[...] += upd

        @pl.when(i == nt - 1)
        def _():
            for k in range(NDEV):
                dw_ref[k] = acc[:, k * cols:(k + 1) * cols].astype(WIRE_DTYPE)

    tok = pl.BlockSpec((tm, D_MODEL), lambda i: (i, 0))
    return _launch(
        body, "mixer_in_backward", grid=(nt,), semantics=("arbitrary",), carry=carry, steps=_grid_steps(nt),
        out_shape=[jax.ShapeDtypeStruct((T, D_MODEL), F32), jax.ShapeDtypeStruct((8, D_MODEL), F32),
                   jax.ShapeDtypeStruct((NDEV, D_MODEL, cols), WIRE_DTYPE)],
        in_specs=[tok, pl.BlockSpec((tm, IN_WIDTH), lambda i: (i, 0)), tok, tok, _resident(prm), _HBM],
        out_specs=[tok, pl.BlockSpec((8, D_MODEL), lambda i: (0, 0)),
                   pl.BlockSpec((NDEV, D_MODEL, cols), lambda i: (0, 0, 0))],
        operands=(x, dz, d, mo, prm, w_in_s),
        scratch=[pltpu.VMEM((D_MODEL, IN_WIDTH), MXU_DTYPE), pltpu.VMEM((D_MODEL, IN_WIDTH), F32),
                 pltpu.SemaphoreType.DMA((8,))])


def _ssm_dense_backward(dbb, da, srow, b_dense):
    def body(dbb_ref, da_ref, srow_ref, bd_ref, db_ref, df_ref):
        df_re, df_im = [], []
        da_re = [_colsum(da_ref[0, j]) for j in range(SSM_BLOCKS)]
        da_im = [_colsum(da_ref[1, j]) for j in range(SSM_BLOCKS)]
        for j in range(SSM_BLOCKS):
            lanes = slice(j * SSM_BLOCK_STATE, (j + 1) * SSM_BLOCK_STATE)
            f_re, f_im = srow_ref[2:3, lanes], srow_ref[3:4, lanes]
            g_re, g_im = dbb_ref[0, j], dbb_ref[1, j]
            b_re, b_im = bd_ref[0, j], bd_ref[1, j]
            db_ref[0, j] = f_re * g_re + f_im * g_im
            db_ref[1, j] = f_re * g_im - f_im * g_re
            df_re.append(_colsum(g_re * b_re + g_im * b_im))
            df_im.append(_colsum(g_im * b_re - g_re * b_im))
        df_ref[...] = jnp.concatenate([jnp.concatenate(df_re, axis=1), jnp.concatenate(df_im, axis=1),
                                       jnp.concatenate(da_re, axis=1), jnp.concatenate(da_im, axis=1),
                                       jnp.zeros((4, N_STATE), F32)], axis=0)

    return pl.pallas_call(body, name="ssm_dense_backward",
                          out_shape=[jax.ShapeDtypeStruct(b_dense.shape, F32), jax.ShapeDtypeStruct((8, N_STATE), F32)],
                          compiler_params=pltpu.CompilerParams(vmem_limit_bytes=VMEM_LIMIT))(dbb, da, srow, b_dense)


def _adamw_update(w, g, m, v):
    m = ADAM_B1 * m + (1.0 - ADAM_B1) * g
    v = ADAM_B2 * v + (1.0 - ADAM_B2) * (g * g)
    m_hat = m / (1.0 - ADAM_B1 ** ADAM_STEP)
    v_hat = v / (1.0 - ADAM_B2 ** ADAM_STEP)
    delta = -ADAM_LR * (m_hat / (jnp.sqrt(v_hat) + ADAM_EPS) + ADAM_WD * w)
    return delta, m, v


def _adam_rows(shape):
    rows, cols = shape
    tr = rows
    while tr * cols * 4 > (1 << 20) and tr % 16 == 0:
        tr //= 2
    return tr


def _adam_sharded(w, m, v, land, order, name):
    R, C = w.shape
    tr = _adam_rows((R, C))

    def body(w_ref, m_ref, v_ref, land_ref, order_ref, g_ref, d_ref, mo_ref, vo_ref):
        g = land_ref[0].astype(F32)
        for b in range(1, NDEV):
            g = g + land_ref[b].astype(F32)
        g_ref[...] = g
        d_ref[...], mo_ref[...], vo_ref[...] = _adamw_update(w_ref[...], g, m_ref[...], v_ref[...])

    blk = pl.BlockSpec((tr, C), lambda i: (i, 0))
    return pl.pallas_call(
        body, name=name, grid=(R // tr,),
        out_shape=[jax.ShapeDtypeStruct((R, C), F32)] * 4,
        in_specs=[blk, blk, blk, pl.BlockSpec((NDEV, tr, C), lambda i: (0, i, 0)), _HBM],
        out_specs=[blk] * 4,
        compiler_params=_params("arbitrary"),
    )(w, m, v, land, order)


def _adam_ada(w, m, v, sc_all, dmod_cols):
    R, C = w.shape
    tr = 256

    def body(w_ref, m_ref, v_ref, sc_ref, dm_ref, g_ref, d_ref, mo_ref, vo_ref):
        g = _mm_tn(sc_ref[...], dm_ref[...])
        g_ref[...] = g
        d_ref[...], mo_ref[...], vo_ref[...] = _adamw_update(w_ref[...], g, m_ref[...], v_ref[...])

    blk = pl.BlockSpec((tr, C), lambda i: (i, 0))
    return pl.pallas_call(
        body, name="adam_w_ada", grid=(R // tr,),
        out_shape=[jax.ShapeDtypeStruct((R, C), F32)] * 4,
        in_specs=[blk, blk, blk, pl.BlockSpec((8, tr), lambda i: (0, i)), pl.BlockSpec((8, C), lambda i: (0, 0))],
        out_specs=[blk] * 4,
        compiler_params=_params("arbitrary"),
    )(w, m, v, sc_all, dmod_cols)


def _adam_small(ws, gs, ms, vs, order, name):
    n = len(ws)

    def body(*refs):
        w, g, m, v = (refs[k * n:(k + 1) * n] for k in range(4))
        outs = refs[4 * n + 1:]
        for k in range(n):
            outs[k][...], outs[n + k][...], outs[2 * n + k][...] = _adamw_update(w[k][...], g[k][...], m[k][...], v[k][...])

    vmem = pl.BlockSpec(memory_space=pltpu.VMEM)
    res = pl.pallas_call(body, name=name, out_shape=[jax.ShapeDtypeStruct(w.shape, F32) for w in ws] * 3,
                         in_specs=[vmem] * (4 * n) + [_HBM], out_specs=[vmem] * (3 * n),
                         compiler_params=pltpu.CompilerParams(vmem_limit_bytes=VMEM_LIMIT))(*ws, *gs, *ms, *vs, order)
    return res[:n], res[n:2 * n], res[2 * n:]


def _block_diag_in(b):
    bt = jnp.transpose(b, (0, 2, 1)).reshape(SSM_BLOCKS, 8, SSM_GROUP, SSM_STATE)
    eye = jnp.eye(8, dtype=bool)[None, :, None, :, None]
    return jnp.where(eye, bt[:, :, :, None, :], 0.0).reshape(SSM_BLOCKS, 128, SSM_BLOCK_STATE)


def _block_diag_out(c):
    ct = jnp.transpose(c, (0, 2, 1)).reshape(SSM_BLOCKS, 8, SSM_STATE, SSM_GROUP)
    eye = jnp.eye(8, dtype=bool)[None, :, None, :, None]
    return jnp.where(eye, ct[:, :, :, None, :], 0.0).reshape(SSM_BLOCKS, SSM_BLOCK_STATE, 128)


def _diag_blocks(dense, rows, cols):
    d5 = dense.reshape(SSM_BLOCKS, 8, rows, 8, cols)
    return jnp.stack([d5[:, a, :, a, :] for a in range(8)], axis=1).reshape(N_SSM_GROUPS, rows, cols)


def _pack_small(ada_vec, parts, params, tail=None):
    rest_rows, rows = _pack_rows(params, ada_vec is not None)
    rest = jnp.concatenate([parts[n].reshape(-1) for n, _ in params])
    rest = jnp.pad(rest, (0, NDEV * rest_rows * 128 - rest.shape[0])).reshape(NDEV, rest_rows, 128)
    head = [] if ada_vec is None else [ada_vec.reshape(NDEV, ADA_ROWS, 128)]
    pad = rows - rest_rows - (0 if ada_vec is None else ADA_ROWS)
    fill = jnp.zeros((NDEV, pad, 128), F32) if tail is None else jnp.pad(tail[None], ((0, NDEV - 1), (0, pad - 1), (0, 127)))
    return jnp.concatenate(head + [rest] + ([fill] if pad else []), axis=1)


def _unpack_small(pack, shapes, params, with_ada):
    rest_rows, _ = _pack_rows(params, with_ada)
    first = ADA_ROWS if with_ada else 0
    ada_vec = pack[:, :first].reshape(-1) if with_ada else None
    rest = pack[:, first:first + rest_rows].reshape(-1)
    out, off = {}, 0
    for n, size in params:
        out[n] = rest[off:off + size].reshape(shapes[n])
        off += size
    return ada_vec, out


WEIGHT_ORDER = ('w_ada', 'b_ada', 'g_ffn1', 'w_ffn1_in', 'w_ffn1_out', 'g_mix', 'w_in', 'pool_w', 'pool_b',
                'pool_scale', 'w_pool_up', 'ssm_lam_re_log', 'ssm_lam_im', 'ssm_log_dt', 'ssm_b_re', 'ssm_b_im',
                'ssm_c_re', 'ssm_c_im', 'ssm_d', 'w_glu', 'b_glu', 'w_ssm_up', 'w_out', 'g_ffn2', 'w_ffn2_in',
                'w_ffn2_out', 'g_final')
GATHERED = ('w_ffn1_in', 'w_ffn1_out', 'w_in', 'w_pool_up', 'w_glu', 'w_ssm_up', 'w_out', 'w_ffn2_in', 'w_ffn2_out')
TRANSPOSED = ('w_ffn1_in', 'w_ffn2_in')
STATE_MINOR = ('ssm_b_re', 'ssm_b_im')


def kernel(x, c, w_ada, b_ada, g_ffn1, w_ffn1_in, w_ffn1_out, g_mix, w_in, pool_w, pool_b, pool_scale, w_pool_up, ssm_lam_re_log, ssm_lam_im, ssm_log_dt, ssm_b_re, ssm_b_im, ssm_c_re, ssm_c_im, ssm_d, w_glu, b_glu, w_ssm_up, w_out, g_ffn2, w_ffn2_in, w_ffn2_out, g_final, loss_target, m_w_ada, m_b_ada, m_g_ffn1, m_w_ffn1_in, m_w_ffn1_out, m_g_mix, m_w_in, m_pool_w, m_pool_b, m_pool_scale, m_w_pool_up, m_ssm_lam_re_log, m_ssm_lam_im, m_ssm_log_dt, m_ssm_b_re, m_ssm_b_im, m_ssm_c_re, m_ssm_c_im, m_ssm_d, m_w_glu, m_b_glu, m_w_ssm_up, m_w_out, m_g_ffn2, m_w_ffn2_in, m_w_ffn2_out, m_g_final, v_w_ada, v_b_ada, v_g_ffn1, v_w_ffn1_in, v_w_ffn1_out, v_g_mix, v_w_in, v_pool_w, v_pool_b, v_pool_scale, v_w_pool_up, v_ssm_lam_re_log, v_ssm_lam_im, v_ssm_log_dt, v_ssm_b_re, v_ssm_b_im, v_ssm_c_re, v_ssm_c_im, v_ssm_d, v_w_glu, v_b_glu, v_w_ssm_up, v_w_out, v_g_ffn2, v_w_ffn2_in, v_w_ffn2_out, v_g_final):
    args = locals()
    W = {n: args[n] for n in WEIGHT_ORDER}
    M = {n: args["m_" + n] for n in WEIGHT_ORDER}
    V = {n: args["v_" + n] for n in WEIGHT_ORDER}
    shapes = {n: W[n].shape for n in WEIGHT_ORDER}
    xt, tgt = x[0], loss_target[0]

    def local(tree, n):
        return jnp.swapaxes(tree[n][0], 0, 1) if n in TRANSPOSED else tree[n][0]

    def as_output(n, a):
        return (jnp.swapaxes(a, 0, 1) if n in TRANSPOSED else a)[None]

    shard = dict(zip(GATHERED, _cast_shards([local(W, n) for n in GATHERED])))
    stacks = {}

    def gather(names):
        return _Gather([shard[n] for n in names])

    def gathered(names, results):
        stacks.update(zip(names, results))

    ffn1_w, ffn2_w = ('w_ffn1_in', 'w_ffn1_out'), ('w_ffn2_in', 'w_ffn2_out')
    mix_w = ('w_in', 'w_pool_up', 'w_glu', 'w_ssm_up', 'w_out')
    mod_cols, sc_all, *res = _ada_forward(c, W['w_ada'][0], b_ada.reshape(NDEV, -1), gather(ffn1_w[:1]))
    gathered(ffn1_w[:1], res)
    win1 = stacks['w_ffn1_in'].reshape(2, 4, FF_SHARD, D_MODEL)
    prm = jnp.concatenate([mod_cols.reshape(9, D_MODEL), g_ffn1, g_mix, g_ffn2, g_final[None], jnp.zeros((3, D_MODEL), F32)], axis=0)
    pad512 = jnp.zeros((1, D_MODEL - 512), F32)
    mvec = jnp.concatenate([jnp.concatenate([pool_b, pad512], axis=1), jnp.concatenate([pool_scale, pad512], axis=1),
                            jnp.concatenate([ssm_d, pad512], axis=1), b_glu, jnp.zeros((4, D_MODEL), F32)], axis=0)
    log_dt_col = ssm_log_dt[0][:, None]
    coeffs = _ssm_params_forward(ssm_lam_re_log[0], ssm_lam_im[0], log_dt_col)
    srow = jnp.stack([t.reshape(N_STATE) for t in coeffs], axis=0)
    b_dense = jnp.stack([_block_diag_in(ssm_b_re[0]), _block_diag_in(ssm_b_im[0])], axis=0)
    c_dense = jnp.stack([_block_diag_out(ssm_c_re[0]), _block_diag_out(ssm_c_im[0])], axis=0)
    bb, ct = _ssm_dense_forward(srow, b_dense, c_dense)
    pw = pool_w[0]

    next_w = ffn1_w[1:] + mix_w[:1]
    ab1, s1, *res = _ffn_hidden(xt, prm, win1, 0, ROW_G_FFN1, "ffn1_hidden", gather(next_w))
    gathered(next_w, res)
    wout1 = stacks['w_ffn1_out'].reshape(4, FF_SHARD, D_MODEL)
    x1, f1, *res = _ffn_out(xt, s1, prm, wout1, 0, "ffn1_out", gather(mix_w[1:]))
    gathered(mix_w[1:], res)
    w_out_full = stacks['w_out'].reshape(D_MODEL, D_MODEL)
    res = _mixer_forward(x1, prm, stacks['w_in'], stacks['w_pool_up'], stacks['w_glu'], stacks['w_ssm_up'],
                         w_out_full, pw, mvec, srow, bb, ct, gather(ffn2_w))
    x2, mo, saved = res[0], res[1], res[2:11]
    gathered(ffn2_w, res[11:])
    win2 = stacks['w_ffn2_in'].reshape(2, 4, FF_SHARD, D_MODEL)
    wout2 = stacks['w_ffn2_out'].reshape(4, FF_SHARD, D_MODEL)
    d3, fin, f3, ab3 = _ffn_forward_loss(x2, tgt, prm, win2, wout2, 2, ROW_G_FFN2, "ffn2_forward_loss")

    lands = {}

    def scatter(grads):
        names = list(grads)
        return _Scatter([grads[n][0] for n in names], [grads[n][1] for n in names], [local(W, n).shape for n in names])

    def scattered(grads, results):
        lands.update(zip(grads, results))

    parts3, dab3, dwout2 = _ffn_backward(d3, ab3, prm, win2, wout2, 2, "ffn2_backward")
    dwin2, = _ffn_dwin(x2, dab3, prm, 2, ROW_G_FFN2, "ffn2_dwin")
    d2, sums3 = _norm_backward(parts3, x2, d3, f3, prm, 2, ROW_G_FFN2, 0.5, "ffn2_norm_backward")
    g_ffn2_w = {'w_ffn2_in': (dwin2, _halves), 'w_ffn2_out': (dwout2.reshape(NDEV, -1, D_MODEL), _stacked)}
    res = _mixer_backward(d2, prm, saved, stacks['w_pool_up'], stacks['w_glu'], stacks['w_ssm_up'], w_out_full, pw, mvec,
                          srow, bb, ct, scatter(g_ffn2_w))
    dz, dwo, dwpu, dwglu, dwsu, dpw, dbb, dct, vsum, da = res[:10]
    scattered(g_ffn2_w, res[10:])
    g_mix_up = {'w_pool_up': (dwpu, _stacked), 'w_glu': (dwglu, _stacked), 'w_ssm_up': (dwsu, _stacked),
                'w_out': (dwo, _stacked)}
    d1, sums2, dwin_mix, *res = _mixer_in_backward(x1, dz, d2, mo, prm, stacks['w_in'], scatter(g_mix_up))
    scattered(g_mix_up, res)
    g_mix_w = {'w_in': (dwin_mix, _stacked)}
    db_dense, df_rows = _ssm_dense_backward(dbb, da, srow, b_dense)
    cot = [df_rows[r].reshape(N_SSM_GROUPS, SSM_STATE) for r in (2, 3, 0, 1)]
    d_lrl, d_li, d_ldt = _ssm_params_backward(ssm_lam_re_log[0], ssm_lam_im[0], log_dt_col, cot)
    small_grads = {
        'g_mix': sums2[0], 'g_ffn2': sums3[0], 'g_final': fin[0], 'pool_w': dpw,
        'pool_b': vsum[1, :512], 'pool_scale': vsum[0, :512], 'ssm_lam_re_log': d_lrl, 'ssm_lam_im': d_li,
        'ssm_log_dt': d_ldt, 'ssm_b_re': _diag_blocks(db_dense[0], SSM_GROUP, SSM_STATE),
        'ssm_b_im': _diag_blocks(db_dense[1], SSM_GROUP, SSM_STATE),
        'ssm_c_re': jnp.transpose(_diag_blocks(dct[0], SSM_STATE, SSM_GROUP), (0, 2, 1)),
        'ssm_c_im': jnp.transpose(_diag_blocks(dct[1], SSM_STATE, SSM_GROUP), (0, 2, 1)),
        'ssm_d': vsum[2, :512], 'b_glu': vsum[3],
    }
    early = _SmallAllReduce(_pack_small(None, small_grads, SMALL_EARLY, fin[1:2, 0:1]))
    parts1, dab1, dwout1, total_early, *res = _ffn_backward(d1, ab1, prm, win1, wout1, 0, "ffn1_backward",
                                                            _Carried(early, scatter(g_mix_w)))
    scattered(g_mix_w, res)
    loss = total_early[0, _pack_rows(SMALL_EARLY, False)[0], 0]
    g_wout1 = {'w_ffn1_out': (dwout1.reshape(NDEV, -1, D_MODEL), _stacked)}
    dwin1, *res = _ffn_dwin(xt, dab1, prm, 0, ROW_G_FFN1, "ffn1_dwin", scatter(g_wout1))
    scattered(g_wout1, res)

    last_w, last_views = ffn1_w[:1], [_halves]
    send_sems, recv_sems, last_src, last_land, token = _scatter_start(
        [dwin1], last_views, [local(W, n).shape for n in last_w], [total_early])
    after_start = token[0:1, 0:1]
    d0, sums1 = _norm_backward(parts1, xt, d1, f1, prm + after_start, 0, ROW_G_FFN1, 0.5, "ffn1_norm_backward")

    grad, delta, new_m, new_v = {}, {}, {}, {}

    def adam_sharded(n):
        res = _adam_sharded(local(W, n), local(M, n), local(V, n), lands[n], token, "adam_" + n)
        grad[n], delta[n], new_m[n], new_v[n] = [as_output(n, r) for r in res]
        return res[3]

    def view(n, a):
        if n in STATE_MINOR:
            return jnp.transpose(a[0], (0, 2, 1))
        return a.reshape(1, -1) if a.ndim == 1 else (a[0] if a.ndim > 2 else a)

    def unview(n, a):
        return jnp.transpose(a, (0, 2, 1))[None] if n in STATE_MINOR else a.reshape(shapes[n])

    def adam_small(params, ada, total, name, order):
        names = [n for n, _ in params] + (['b_ada'] if ada else [])
        view_shapes = {n: view(n, W[n]).shape for n in names}
        ada_vec, grads = _unpack_small(total, view_shapes, params, ada)
        if ada:
            grads['b_ada'] = ada_vec.reshape(view_shapes['b_ada'])
        ws, ms, vs = ([view(n, t[n]) for n in names] for t in (W, M, V))
        gs = [grads[n] for n in names]
        res = _adam_small(ws, gs, ms, vs, order, name)
        for dst, vals in zip((grad, delta, new_m, new_v), (gs, *res)):
            dst.update({n: unview(n, a) for n, a in zip(names, vals)})

    done = [d0] + [adam_sharded(n) for n in GATHERED if n not in last_w]
    adam_small(SMALL_EARLY, False, total_early, "adam_small_early", token)
    lands.update(zip(last_w, _scatter_wait(send_sems, recv_sems, last_src, last_land, last_views, done)))

    dmod = jnp.concatenate([sums1[1:4], sums2[1:4], sums3[1:4]], axis=0).reshape(-1)
    total_late, landed = _allreduce_small(_pack_small(dmod, {'g_ffn1': sums1[0]}, SMALL_LATE), lands[last_w[0]],
                                          "allreduce_late")
    dmod_cols = landed[:, :ADA_ROWS].reshape(NDEV, ADA_ROWS * 128)
    res = _adam_ada(W['w_ada'][0], M['w_ada'][0], V['w_ada'][0], sc_all, dmod_cols)
    grad['w_ada'], delta['w_ada'], new_m['w_ada'], new_v['w_ada'] = [r[None] for r in res]
    adam_small(SMALL_LATE, True, total_late, "adam_small_late", token)
    for n in last_w:
        adam_sharded(n)

    return (loss, d0[None], *[grad[n] for n in WEIGHT_ORDER], *[delta[n] for n in WEIGHT_ORDER],
            *[new_m[n] for n in WEIGHT_ORDER], *[new_v[n] for n in WEIGHT_ORDER])
```
